```python
import math
import jax, jax.numpy as jnp
from jax import lax
import numpy as np

D_MODEL = 1024
BATCH = 8
SEQ = 8192
DEPTH = 1

D_MIX = D_MODEL
HEAD_DIM = 64
ATTN_WIDTH = D_MIX // 2
N_Q_HEADS = ATTN_WIDTH // HEAD_DIM
N_KV_HEADS = 2
GQA_GROUP = N_Q_HEADS // N_KV_HEADS
WINDOW = 128
BLOCK = 128
CONV_WIDTH = D_MIX - ATTN_WIDTH
CONV_GROUPS = 8
CONV_KERNEL = 31
D_FF = 2816
FFN_KERNEL = 3
Q_COLS = N_Q_HEADS * HEAD_DIM
KV_COLS = N_KV_HEADS * HEAD_DIM
CONV_IN_COLS = 2 * CONV_WIDTH
IN_COLS = Q_COLS + 2 * KV_COLS + CONV_IN_COLS
EPS = 1e-6
NEG_INF = -1e30

kernel_name = "hymba_swa_sink_conformer_conv_hybrid"


def rmsnorm(x, g):
    xf = x.astype(jnp.float32)
    xf = xf * lax.rsqrt(jnp.mean(xf * xf, axis=-1, keepdims=True) + EPS)
    return (xf * g.astype(jnp.float32)).astype(x.dtype)


def group_layernorm(x, g, b, groups):
    shp = x.shape
    xf = x.astype(jnp.float32).reshape(shp[:-1] + (groups, shp[-1] // groups))
    mu = jnp.mean(xf, axis=-1, keepdims=True)
    var = jnp.mean(jnp.square(xf - mu), axis=-1, keepdims=True)
    xf = ((xf - mu) * lax.rsqrt(var + EPS)).reshape(shp)
    return (xf * g.astype(jnp.float32) + b.astype(jnp.float32)).astype(x.dtype)


def causal_dwconv(x, w, b):
    k_len, ch = w.shape
    y = lax.conv_general_dilated(
        x, w[:, None, :].astype(x.dtype), window_strides=(1,), padding=[(k_len - 1, 0)],
        dimension_numbers=("NWC", "WIO", "NWC"), feature_group_count=ch)
    return y + b.astype(x.dtype)


def alibi_slopes(n_heads):
    return 2.0 ** (-8.0 * (np.arange(n_heads, dtype=np.float32) + 1.0) / n_heads)


def swa_gqa_sinks(q, k, v, sinks):
    b_, s_, _, d_ = q.shape
    nb = s_ // BLOCK
    qb = q.reshape(b_, nb, BLOCK, N_KV_HEADS, GQA_GROUP, d_)
    kb = k.reshape(b_, nb, BLOCK, N_KV_HEADS, d_)
    vb = v.reshape(b_, nb, BLOCK, N_KV_HEADS, d_)
    pad = ((0, 0), (1, 0), (0, 0), (0, 0), (0, 0))
    kk = jnp.concatenate([jnp.pad(kb, pad)[:, :-1], kb], axis=2)
    vv = jnp.concatenate([jnp.pad(vb, pad)[:, :-1], vb], axis=2)
    scores = jnp.einsum("bnqkgd,bnskd->bnkgqs", qb, kk).astype(jnp.float32) / math.sqrt(d_)
    qi = jnp.arange(BLOCK)[:, None]
    kj = jnp.arange(2 * BLOCK)[None, :]
    rel = (qi + BLOCK - kj).astype(jnp.float32)
    band = (rel >= 0) & (rel < WINDOW)
    key_ok = (jnp.arange(nb)[:, None] * BLOCK - BLOCK + jnp.arange(2 * BLOCK)[None, :]) >= 0
    valid = band[None] & key_ok[:, None, :]
    slopes = jnp.asarray(alibi_slopes(N_Q_HEADS)).reshape(N_KV_HEADS, GQA_GROUP)
    scores = scores - slopes[:, :, None, None] * rel
    scores = jnp.where(valid[None, :, None, None], scores, NEG_INF)
    sink = jnp.broadcast_to(sinks.astype(jnp.float32).reshape(N_KV_HEADS, GQA_GROUP, 1, 1),
                            scores.shape[:-1] + (1,))
    probs = jax.nn.softmax(jnp.concatenate([scores, sink], axis=-1), axis=-1)[..., :-1]
    out = jnp.einsum("bnkgqs,bnskd->bnqkgd", probs.astype(v.dtype), vv)
    return out.reshape(b_, s_, N_Q_HEADS * d_)


def _fwd_setup_inputs(seed: int = 0) -> dict:
    key = jax.random.key(seed)
    ks = jax.random.split(key, 20)
    f32 = jnp.float32
    nrm = lambda k, shp, sc: jax.random.normal(k, shp, f32) * sc
    return {
        "x": nrm(ks[0], (BATCH, SEQ, D_MODEL), 1.0),
        "mix_norm_gain": 1.0 + nrm(ks[1], (D_MODEL,), 0.01),
        "w_in": nrm(ks[2], (D_MODEL, IN_COLS), D_MODEL ** -0.5),
        "b_in": nrm(ks[3], (IN_COLS,), 0.01),
        "q_norm_gain": 1.0 + nrm(ks[4], (HEAD_DIM,), 0.01),
        "k_norm_gain": 1.0 + nrm(ks[5], (HEAD_DIM,), 0.01),
        "attn_sinks": nrm(ks[6], (N_Q_HEADS,), 0.5),
        "conv_dw_w": nrm(ks[7], (CONV_KERNEL, CONV_WIDTH), CONV_KERNEL ** -0.5),
        "conv_dw_b": nrm(ks[8], (CONV_WIDTH,), 0.01),
        "conv_norm_gain": 1.0 + nrm(ks[9], (CONV_WIDTH,), 0.01),
        "conv_norm_bias": nrm(ks[10], (CONV_WIDTH,), 0.01),
        "w_out": nrm(ks[11], (D_MIX, D_MODEL), D_MIX ** -0.5),
        "b_out": nrm(ks[12], (D_MODEL,), 0.01),
        "ffn_norm_gain": 1.0 + nrm(ks[13], (D_MODEL,), 0.01),
        "w_up": nrm(ks[14], (D_MODEL, 2 * D_FF), D_MODEL ** -0.5),
        "ffn_dw_w": nrm(ks[15], (FFN_KERNEL, 2 * D_FF), FFN_KERNEL ** -0.5),
        "ffn_dw_b": nrm(ks[16], (2 * D_FF,), 0.01),
        "w_down": nrm(ks[17], (D_FF, D_MODEL), D_FF ** -0.5),
    }


def _fwd_reference(x, mix_norm_gain, w_in, b_in, q_norm_gain, k_norm_gain, attn_sinks,
              conv_dw_w, conv_dw_b, conv_norm_gain, conv_norm_bias, w_out, b_out,
              ffn_norm_gain, w_up, ffn_dw_w, ffn_dw_b, w_down):
    b_, s_, _ = x.shape
    for _layer in range(DEPTH):
        h = rmsnorm(x, mix_norm_gain)
        proj = h @ w_in + b_in
        q, k, v, conv_in = jnp.split(
            proj, [Q_COLS, Q_COLS + KV_COLS, Q_COLS + 2 * KV_COLS], axis=-1)
        q = rmsnorm(q.reshape(b_, s_, N_Q_HEADS, HEAD_DIM), q_norm_gain)
        k = rmsnorm(k.reshape(b_, s_, N_KV_HEADS, HEAD_DIM), k_norm_gain)
        v = v.reshape(b_, s_, N_KV_HEADS, HEAD_DIM)
        attn_out = swa_gqa_sinks(q, k, v, attn_sinks)
        a, gate = jnp.split(conv_in, 2, axis=-1)
        c = a * jax.nn.sigmoid(gate)
        c = causal_dwconv(c, conv_dw_w, conv_dw_b)
        c = jax.nn.silu(group_layernorm(c, conv_norm_gain, conv_norm_bias, CONV_GROUPS))
        mixed = jnp.concatenate([attn_out, c], axis=-1)
        x = x + mixed @ w_out + b_out
        h = rmsnorm(x, ffn_norm_gain)
        up = causal_dwconv(h @ w_up, ffn_dw_w, ffn_dw_b)
        g, u = jnp.split(up, 2, axis=-1)
        x = x + (jax.nn.silu(g) * u) @ w_down
    return x


import jax as _jax
import jax.numpy as _jnp

TWIN_FORMAT = 'train_step'
FWD_PARAMS = ['x', 'mix_norm_gain', 'w_in', 'b_in', 'q_norm_gain', 'k_norm_gain', 'attn_sinks', 'conv_dw_w', 'conv_dw_b', 'conv_norm_gain', 'conv_norm_bias', 'w_out', 'b_out', 'ffn_norm_gain', 'w_up', 'ffn_dw_w', 'ffn_dw_b', 'w_down']
TWIN_WEIGHTS = ['mix_norm_gain', 'w_in', 'b_in', 'q_norm_gain', 'k_norm_gain', 'attn_sinks', 'conv_dw_w', 'conv_dw_b', 'conv_norm_gain', 'conv_norm_bias', 'w_out', 'b_out', 'ffn_norm_gain', 'w_up', 'ffn_dw_w', 'ffn_dw_b', 'w_down']
TWIN_DIFF_INPUT = 'x'
TWIN_INPUTS = ['x', 'mix_norm_gain', 'w_in', 'b_in', 'q_norm_gain', 'k_norm_gain', 'attn_sinks', 'conv_dw_w', 'conv_dw_b', 'conv_norm_gain', 'conv_norm_bias', 'w_out', 'b_out', 'ffn_norm_gain', 'w_up', 'ffn_dw_w', 'ffn_dw_b', 'w_down', 'loss_target', 'm_mix_norm_gain', 'm_w_in', 'm_b_in', 'm_q_norm_gain', 'm_k_norm_gain', 'm_attn_sinks', 'm_conv_dw_w', 'm_conv_dw_b', 'm_conv_norm_gain', 'm_conv_norm_bias', 'm_w_out', 'm_b_out', 'm_ffn_norm_gain', 'm_w_up', 'm_ffn_dw_w', 'm_ffn_dw_b', 'm_w_down', 'v_mix_norm_gain', 'v_w_in', 'v_b_in', 'v_q_norm_gain', 'v_k_norm_gain', 'v_attn_sinks', 'v_conv_dw_w', 'v_conv_dw_b', 'v_conv_norm_gain', 'v_conv_norm_bias', 'v_w_out', 'v_b_out', 'v_ffn_norm_gain', 'v_w_up', 'v_ffn_dw_w', 'v_ffn_dw_b', 'v_w_down']
TWIN_OUTPUTS = ['loss', 'grad_x', 'grad_mix_norm_gain', 'grad_w_in', 'grad_b_in', 'grad_q_norm_gain', 'grad_k_norm_gain', 'grad_attn_sinks', 'grad_conv_dw_w', 'grad_conv_dw_b', 'grad_conv_norm_gain', 'grad_conv_norm_bias', 'grad_w_out', 'grad_b_out', 'grad_ffn_norm_gain', 'grad_w_up', 'grad_ffn_dw_w', 'grad_ffn_dw_b', 'grad_w_down', 'delta_mix_norm_gain', 'delta_w_in', 'delta_b_in', 'delta_q_norm_gain', 'delta_k_norm_gain', 'delta_attn_sinks', 'delta_conv_dw_w', 'delta_conv_dw_b', 'delta_conv_norm_gain', 'delta_conv_norm_bias', 'delta_w_out', 'delta_b_out', 'delta_ffn_norm_gain', 'delta_w_up', 'delta_ffn_dw_w', 'delta_ffn_dw_b', 'delta_w_down', 'new_m_mix_norm_gain', 'new_m_w_in', 'new_m_b_in', 'new_m_q_norm_gain', 'new_m_k_norm_gain', 'new_m_attn_sinks', 'new_m_conv_dw_w', 'new_m_conv_dw_b', 'new_m_conv_norm_gain', 'new_m_conv_norm_bias', 'new_m_w_out', 'new_m_b_out', 'new_m_ffn_norm_gain', 'new_m_w_up', 'new_m_ffn_dw_w', 'new_m_ffn_dw_b', 'new_m_w_down', 'new_v_mix_norm_gain', 'new_v_w_in', 'new_v_b_in', 'new_v_q_norm_gain', 'new_v_k_norm_gain', 'new_v_attn_sinks', 'new_v_conv_dw_w', 'new_v_conv_dw_b', 'new_v_conv_norm_gain', 'new_v_conv_norm_bias', 'new_v_w_out', 'new_v_b_out', 'new_v_ffn_norm_gain', 'new_v_w_up', 'new_v_ffn_dw_w', 'new_v_ffn_dw_b', 'new_v_w_down']
TWIN_LEAF_KINDS = {'loss': 'loss', 'grad_x': 'grad_x', 'grad_mix_norm_gain': 'grad_w', 'grad_w_in': 'grad_w', 'grad_b_in': 'grad_w', 'grad_q_norm_gain': 'grad_w', 'grad_k_norm_gain': 'grad_w', 'grad_attn_sinks': 'grad_w', 'grad_conv_dw_w': 'grad_w', 'grad_conv_dw_b': 'grad_w', 'grad_conv_norm_gain': 'grad_w', 'grad_conv_norm_bias': 'grad_w', 'grad_w_out': 'grad_w', 'grad_b_out': 'grad_w', 'grad_ffn_norm_gain': 'grad_w', 'grad_w_up': 'grad_w', 'grad_ffn_dw_w': 'grad_w', 'grad_ffn_dw_b': 'grad_w', 'grad_w_down': 'grad_w', 'delta_mix_norm_gain': 'delta_w', 'delta_w_in': 'delta_w', 'delta_b_in': 'delta_w', 'delta_q_norm_gain': 'delta_w', 'delta_k_norm_gain': 'delta_w', 'delta_attn_sinks': 'delta_w', 'delta_conv_dw_w': 'delta_w', 'delta_conv_dw_b': 'delta_w', 'delta_conv_norm_gain': 'delta_w', 'delta_conv_norm_bias': 'delta_w', 'delta_w_out': 'delta_w', 'delta_b_out': 'delta_w', 'delta_ffn_norm_gain': 'delta_w', 'delta_w_up': 'delta_w', 'delta_ffn_dw_w': 'delta_w', 'delta_ffn_dw_b': 'delta_w', 'delta_w_down': 'delta_w', 'new_m_mix_norm_gain': 'new_m', 'new_m_w_in': 'new_m', 'new_m_b_in': 'new_m', 'new_m_q_norm_gain': 'new_m', 'new_m_k_norm_gain': 'new_m', 'new_m_attn_sinks': 'new_m', 'new_m_conv_dw_w': 'new_m', 'new_m_conv_dw_b': 'new_m', 'new_m_conv_norm_gain': 'new_m', 'new_m_conv_norm_bias': 'new_m', 'new_m_w_out': 'new_m', 'new_m_b_out': 'new_m', 'new_m_ffn_norm_gain': 'new_m', 'new_m_w_up': 'new_m', 'new_m_ffn_dw_w': 'new_m', 'new_m_ffn_dw_b': 'new_m', 'new_m_w_down': 'new_m', 'new_v_mix_norm_gain': 'new_v', 'new_v_w_in': 'new_v', 'new_v_b_in': 'new_v', 'new_v_q_norm_gain': 'new_v', 'new_v_k_norm_gain': 'new_v', 'new_v_attn_sinks': 'new_v', 'new_v_conv_dw_w': 'new_v', 'new_v_conv_dw_b': 'new_v', 'new_v_conv_norm_gain': 'new_v', 'new_v_conv_norm_bias': 'new_v', 'new_v_w_out': 'new_v', 'new_v_b_out': 'new_v', 'new_v_ffn_norm_gain': 'new_v', 'new_v_w_up': 'new_v', 'new_v_ffn_dw_w': 'new_v', 'new_v_ffn_dw_b': 'new_v', 'new_v_w_down': 'new_v'}


def _forward(args):
    return _fwd_reference(*[args[k] for k in FWD_PARAMS])


def _output_shape():
    def fwd():
        inp = _fwd_setup_inputs(0)
        return _fwd_reference(*[inp[k] for k in FWD_PARAMS])
    out = _jax.eval_shape(fwd)
    return out.shape, out.dtype

N_MICROBATCH = 1
ADAM_LR = 0.001
ADAM_B1 = 0.9
ADAM_B2 = 0.999
ADAM_EPS = 1e-08
ADAM_WD = 0.01
ADAM_STEP = 10
PER_EXAMPLE_BATCH_AXIS = {'x': 0, 'loss_target': 0}
SHARED_INPUTS = []
_WEIGHT_DTYPES = {'mix_norm_gain': _jnp.float32, 'w_in': _jnp.float32, 'b_in': _jnp.float32, 'q_norm_gain': _jnp.float32, 'k_norm_gain': _jnp.float32, 'attn_sinks': _jnp.float32, 'conv_dw_w': _jnp.float32, 'conv_dw_b': _jnp.float32, 'conv_norm_gain': _jnp.float32, 'conv_norm_bias': _jnp.float32, 'w_out': _jnp.float32, 'b_out': _jnp.float32, 'ffn_norm_gain': _jnp.float32, 'w_up': _jnp.float32, 'ffn_dw_w': _jnp.float32, 'ffn_dw_b': _jnp.float32, 'w_down': _jnp.float32}
MOMENT_SCALE = {'mix_norm_gain': 1.842434e+00, 'w_in': 3.442260e-01, 'b_in': 1.335926e+01, 'q_norm_gain': 1.717602e+01, 'k_norm_gain': 1.712246e+01, 'attn_sinks': 6.196931e+01, 'conv_dw_w': 1.087572e+00, 'conv_dw_b': 2.123893e+01, 'conv_norm_gain': 2.963349e+01, 'conv_norm_bias': 2.254686e+01, 'w_out': 3.530829e+00, 'b_out': 2.475399e+01, 'ffn_norm_gain': 5.399320e+01, 'w_up': 1.322521e+00, 'ffn_dw_w': 7.845492e+00, 'ffn_dw_b': 7.072740e+00, 'w_down': 7.522953e-01}


def _to_microbatches(a, axis):
    t = _jnp.moveaxis(a, axis, 0)
    t = t.reshape((N_MICROBATCH, t.shape[0] // N_MICROBATCH) + t.shape[1:])
    return _jnp.moveaxis(t, 1, axis + 1)


def setup_inputs(seed: int = 0) -> dict:
    inp = _fwd_setup_inputs(seed)
    key = _jax.random.fold_in(_jax.random.key(seed), 7919)
    shape, _ = _output_shape()
    out = dict(inp)
    out["loss_target"] = _jax.random.normal(_jax.random.fold_in(key, 0), shape, _jnp.float32)
    for i, name in enumerate(TWIN_WEIGHTS):
        w = inp[name].astype(_jnp.float32)
        if MOMENT_SCALE is None:
            s = _jnp.sqrt(_jnp.mean(_jnp.square(w)) + 1e-30)
        else:
            s = MOMENT_SCALE[name]
        km, kv = _jax.random.split(_jax.random.fold_in(key, i + 1))
        out[name] = w
        out["m_" + name] = s * _jax.random.normal(km, w.shape, _jnp.float32)
        out["v_" + name] = (s * s) * _jax.random.uniform(kv, w.shape, _jnp.float32, 0.5, 1.5)
    if N_MICROBATCH > 1:
        for name, axis in PER_EXAMPLE_BATCH_AXIS.items():
            out[name] = _to_microbatches(out[name], axis)
    return {'x': out['x'], 'mix_norm_gain': out['mix_norm_gain'], 'w_in': out['w_in'], 'b_in': out['b_in'], 'q_norm_gain': out['q_norm_gain'], 'k_norm_gain': out['k_norm_gain'], 'attn_sinks': out['attn_sinks'], 'conv_dw_w': out['conv_dw_w'], 'conv_dw_b': out['conv_dw_b'], 'conv_norm_gain': out['conv_norm_gain'], 'conv_norm_bias': out['conv_norm_bias'], 'w_out': out['w_out'], 'b_out': out['b_out'], 'ffn_norm_gain': out['ffn_norm_gain'], 'w_up': out['w_up'], 'ffn_dw_w': out['ffn_dw_w'], 'ffn_dw_b': out['ffn_dw_b'], 'w_down': out['w_down'], 'loss_target': out['loss_target'], 'm_mix_norm_gain': out['m_mix_norm_gain'], 'm_w_in': out['m_w_in'], 'm_b_in': out['m_b_in'], 'm_q_norm_gain': out['m_q_norm_gain'], 'm_k_norm_gain': out['m_k_norm_gain'], 'm_attn_sinks': out['m_attn_sinks'], 'm_conv_dw_w': out['m_conv_dw_w'], 'm_conv_dw_b': out['m_conv_dw_b'], 'm_conv_norm_gain': out['m_conv_norm_gain'], 'm_conv_norm_bias': out['m_conv_norm_bias'], 'm_w_out': out['m_w_out'], 'm_b_out': out['m_b_out'], 'm_ffn_norm_gain': out['m_ffn_norm_gain'], 'm_w_up': out['m_w_up'], 'm_ffn_dw_w': out['m_ffn_dw_w'], 'm_ffn_dw_b': out['m_ffn_dw_b'], 'm_w_down': out['m_w_down'], 'v_mix_norm_gain': out['v_mix_norm_gain'], 'v_w_in': out['v_w_in'], 'v_b_in': out['v_b_in'], 'v_q_norm_gain': out['v_q_norm_gain'], 'v_k_norm_gain': out['v_k_norm_gain'], 'v_attn_sinks': out['v_attn_sinks'], 'v_conv_dw_w': out['v_conv_dw_w'], 'v_conv_dw_b': out['v_conv_dw_b'], 'v_conv_norm_gain': out['v_conv_norm_gain'], 'v_conv_norm_bias': out['v_conv_norm_bias'], 'v_w_out': out['v_w_out'], 'v_b_out': out['v_b_out'], 'v_ffn_norm_gain': out['v_ffn_norm_gain'], 'v_w_up': out['v_w_up'], 'v_ffn_dw_w': out['v_ffn_dw_w'], 'v_ffn_dw_b': out['v_ffn_dw_b'], 'v_w_down': out['v_w_down']}


def _loss(weights, diff, rest, loss_target):
    with _jax.named_scope("forward"):
        args = {**rest, TWIN_DIFF_INPUT: diff, **{k: w.astype(_WEIGHT_DTYPES[k]) for k, w in weights.items()}}
        y = _forward(args)
    with _jax.named_scope("loss_head"):
        err = _jnp.square(y.astype(_jnp.float32) - loss_target)
        return 0.5 * _jnp.sum(_jnp.mean(err, axis=-1)) if err.ndim else 0.5 * err


def _adamw(w, g, m, v):
    m = ADAM_B1 * m + (1.0 - ADAM_B1) * g
    v = ADAM_B2 * v + (1.0 - ADAM_B2) * _jnp.square(g)
    m_hat = m / (1.0 - ADAM_B1 ** ADAM_STEP)
    v_hat = v / (1.0 - ADAM_B2 ** ADAM_STEP)
    delta = -ADAM_LR * (m_hat / (_jnp.sqrt(v_hat) + ADAM_EPS) + ADAM_WD * w)
    return delta, m, v


def reference(x, mix_norm_gain, w_in, b_in, q_norm_gain, k_norm_gain, attn_sinks, conv_dw_w, conv_dw_b, conv_norm_gain, conv_norm_bias, w_out, b_out, ffn_norm_gain, w_up, ffn_dw_w, ffn_dw_b, w_down, loss_target, m_mix_norm_gain, m_w_in, m_b_in, m_q_norm_gain, m_k_norm_gain, m_attn_sinks, m_conv_dw_w, m_conv_dw_b, m_conv_norm_gain, m_conv_norm_bias, m_w_out, m_b_out, m_ffn_norm_gain, m_w_up, m_ffn_dw_w, m_ffn_dw_b, m_w_down, v_mix_norm_gain, v_w_in, v_b_in, v_q_norm_gain, v_k_norm_gain, v_attn_sinks, v_conv_dw_w, v_conv_dw_b, v_conv_norm_gain, v_conv_norm_bias, v_w_out, v_b_out, v_ffn_norm_gain, v_w_up, v_ffn_dw_w, v_ffn_dw_b, v_w_down):
    given = dict(x=x, mix_norm_gain=mix_norm_gain, w_in=w_in, b_in=b_in, q_norm_gain=q_norm_gain, k_norm_gain=k_norm_gain, attn_sinks=attn_sinks, conv_dw_w=conv_dw_w, conv_dw_b=conv_dw_b, conv_norm_gain=conv_norm_gain, conv_norm_bias=conv_norm_bias, w_out=w_out, b_out=b_out, ffn_norm_gain=ffn_norm_gain, w_up=w_up, ffn_dw_w=ffn_dw_w, ffn_dw_b=ffn_dw_b, w_down=w_down, loss_target=loss_target, m_mix_norm_gain=m_mix_norm_gain, m_w_in=m_w_in, m_b_in=m_b_in, m_q_norm_gain=m_q_norm_gain, m_k_norm_gain=m_k_norm_gain, m_attn_sinks=m_attn_sinks, m_conv_dw_w=m_conv_dw_w, m_conv_dw_b=m_conv_dw_b, m_conv_norm_gain=m_conv_norm_gain, m_conv_norm_bias=m_conv_norm_bias, m_w_out=m_w_out, m_b_out=m_b_out, m_ffn_norm_gain=m_ffn_norm_gain, m_w_up=m_w_up, m_ffn_dw_w=m_ffn_dw_w, m_ffn_dw_b=m_ffn_dw_b, m_w_down=m_w_down, v_mix_norm_gain=v_mix_norm_gain, v_w_in=v_w_in, v_b_in=v_b_in, v_q_norm_gain=v_q_norm_gain, v_k_norm_gain=v_k_norm_gain, v_attn_sinks=v_attn_sinks, v_conv_dw_w=v_conv_dw_w, v_conv_dw_b=v_conv_dw_b, v_conv_norm_gain=v_conv_norm_gain, v_conv_norm_bias=v_conv_norm_bias, v_w_out=v_w_out, v_b_out=v_b_out, v_ffn_norm_gain=v_ffn_norm_gain, v_w_up=v_w_up, v_ffn_dw_w=v_ffn_dw_w, v_ffn_dw_b=v_ffn_dw_b, v_w_down=v_w_down)
    weights = {n: given[n] for n in TWIN_WEIGHTS}
    shared = {n: given[n] for n in SHARED_INPUTS}
    per_example = {n: given[n] for n in ['x']}
    grad_fn = _jax.value_and_grad(_loss, argnums=(0, 1))

    def one_microbatch(ex, loss_target):
        ex = dict(ex)
        diff = ex.pop(TWIN_DIFF_INPUT)
        return grad_fn(weights, diff, {**shared, **ex}, loss_target)

    if N_MICROBATCH == 1:
        loss, (grad_w, grad_x) = one_microbatch(per_example, given["loss_target"])
    else:
        def body(carry, xs):
            loss_sum, grad_sum = carry
            l_k, (gw_k, gx_k) = one_microbatch(xs[0], xs[1])
            with _jax.named_scope("update"):
                return (loss_sum + l_k, _jax.tree.map(_jnp.add, grad_sum, gw_k)), gx_k

        init = (_jnp.zeros((), _jnp.float32), _jax.tree.map(_jnp.zeros_like, weights))
        (loss, grad_w), grad_x = _jax.lax.scan(body, init, (per_example, given["loss_target"]))
    with _jax.named_scope("update"):
        delta_w, new_m, new_v = {}, {}, {}
        for n in TWIN_WEIGHTS:
            delta_w[n], new_m[n], new_v[n] = _adamw(weights[n], grad_w[n], given["m_" + n], given["v_" + n])
    return (loss, grad_x, *[grad_w[n] for n in TWIN_WEIGHTS], *[delta_w[n] for n in TWIN_WEIGHTS],
            *[new_m[n] for n in TWIN_WEIGHTS], *[new_v[n] for n in TWIN_WEIGHTS])
```

```python
import functools
import math

import jax
import jax.numpy as jnp
import numpy as np
from jax import lax
from jax.experimental import pallas as pl
from jax.experimental.pallas import tpu as pltpu

F32 = jnp.float32
BF16 = jnp.bfloat16

D_MODEL = 1024
HEAD_DIM = 64
N_Q_HEADS = 8
N_KV_HEADS = 2
Q_COLS = 512
KV_COLS = 128
QKV_COLS = Q_COLS + 2 * KV_COLS
CONV_WIDTH = 512
CIN_COLS = 2 * CONV_WIDTH
CONV_KERNEL = 31
CONV_HALO = 32
D_FF = 2816
N_DEV = 8
FF_CHUNK = 2 * D_FF // N_DEV
N_FF_PAIRS = N_DEV // 2
ATT_BLOCK = 128
EPS = 1e-6
NEG_INF = -1e30
SLOPES = [float(2.0 ** (-8.0 * (h + 1.0) / N_Q_HEADS)) for h in range(N_Q_HEADS)]

ADAM_LR = 0.001
ADAM_B1 = 0.9
ADAM_B2 = 0.999
ADAM_EPS = 1e-08
ADAM_WD = 0.01
ADAM_STEP = 10

LANES = 128
SUBLANES = 8
VMEM_LIMIT = 56 * 1024 * 1024
MESH = pl.DeviceIdType.MESH


def _cparams(*sem, **kw):
    return pltpu.CompilerParams(dimension_semantics=sem or None, vmem_limit_bytes=VMEM_LIMIT, **kw)


def _resident(shape):
    nd = len(shape)
    return pl.BlockSpec(shape, lambda *_: (0,) * nd, pipeline_mode=pl.Buffered(1))


def _dot(a, b):
    return jnp.dot(a, b, preferred_element_type=F32)


def _dot_nt(a, b):
    return lax.dot_general(a, b, (((1,), (1,)), ((), ())), preferred_element_type=F32)


def _dot_tn(a, b):
    return lax.dot_general(a, b, (((0,), (0,)), ((), ())), preferred_element_type=F32)


def _sigmoid(x):
    return 1.0 / (1.0 + jnp.exp(-x))


def _lo_mask(shape):
    return lax.broadcasted_iota(jnp.int32, shape, len(shape) - 1) % LANES < HEAD_DIM


def _half_sums(t, lo):
    s_lo = jnp.sum(jnp.where(lo, t, 0.0), axis=-1, keepdims=True)
    s_hi = jnp.sum(jnp.where(lo, 0.0, t), axis=-1, keepdims=True)
    return jnp.where(lo, s_lo, s_hi)


def _head_norm(t, lo):
    r = lax.rsqrt(_half_sums(t * t, lo) * (1.0 / HEAD_DIM) + EPS)
    return t * r, r


def _head_norm_bwd(dn, n, r, lo):
    return r * (dn - n * (_half_sums(dn * n, lo) * (1.0 / HEAD_DIM)))


def _tile(s):
    return min(512, s)


def _mix_proj(x, g_mix, w_qkv, w_cin, b_qkv, b_cin):
    s = x.shape[0]
    tm = _tile(s)

    def body(x_ref, g_ref, wq_ref, wc_ref, bq_ref, bc_ref, qkv_ref, cin_ref, h1_ref):
        xv = x_ref[...]
        r = lax.rsqrt(jnp.mean(xv * xv, axis=-1, keepdims=True) + EPS)
        h = (xv * r * g_ref[...]).astype(BF16)
        h1_ref[...] = h
        qkv_ref[...] = _dot(h, wq_ref[...]) + bq_ref[...]
        cin_ref[...] = _dot(h, wc_ref[...]) + bc_ref[...]

    return pl.pallas_call(
        body, grid=(s // tm,),
        in_specs=[pl.BlockSpec((tm, D_MODEL), lambda i: (i, 0)), _resident((1, D_MODEL)),
                  _resident((D_MODEL, QKV_COLS)), _resident((D_MODEL, CIN_COLS)),
                  _resident((1, QKV_COLS)), _resident((1, CIN_COLS))],
        out_specs=[pl.BlockSpec((tm, QKV_COLS), lambda i: (i, 0)), pl.BlockSpec((tm, CIN_COLS), lambda i: (i, 0)),
                   pl.BlockSpec((tm, D_MODEL), lambda i: (i, 0))],
        out_shape=[jax.ShapeDtypeStruct((s, QKV_COLS), F32), jax.ShapeDtypeStruct((s, CIN_COLS), F32),
                   jax.ShapeDtypeStruct((s, D_MODEL), BF16)],
        compiler_params=_cparams("parallel"), name="mix_proj")(x, g_mix, w_qkv, w_cin, b_qkv, b_cin)


def _kv_variants(kv_all, gk2, lo):
    k_all = kv_all[:, :LANES]
    v_all = kv_all[:, LANES:]
    kn_pre, rk = _head_norm(k_all, lo)
    kn = kn_pre * gk2
    kr = pltpu.roll(kn, HEAD_DIM, 1)
    vr = pltpu.roll(v_all, HEAD_DIM, 1)
    zero = jnp.zeros_like(kn)
    k_lo = [jnp.where(lo, kn, zero).astype(BF16), jnp.where(lo, kr, zero).astype(BF16)]
    k_hi = [jnp.where(lo, zero, kr).astype(BF16), jnp.where(lo, zero, kn).astype(BF16)]
    v_lo = [jnp.where(lo, v_all, zero).astype(BF16), jnp.where(lo, vr, zero).astype(BF16)]
    v_hi = [jnp.where(lo, zero, vr).astype(BF16), jnp.where(lo, zero, v_all).astype(BF16)]
    return k_lo, k_hi, v_lo, v_hi, kn_pre, rk


def _att_consts(first_tile, b):
    rows = 2 * ATT_BLOCK
    qi = lax.broadcasted_iota(jnp.int32, (rows, 2 * ATT_BLOCK), 0) % ATT_BLOCK
    kj = lax.broadcasted_iota(jnp.int32, (rows, 2 * ATT_BLOCK), 1)
    rel = qi + ATT_BLOCK - kj
    valid = (rel >= 0) & (rel < ATT_BLOCK)
    if b == 0:
        valid = valid & ((kj >= ATT_BLOCK) | jnp.logical_not(first_tile))
    return rel.astype(F32), valid


def _row_const(va, vb):
    top = lax.broadcasted_iota(jnp.int32, (2 * ATT_BLOCK, 1), 0) < ATT_BLOCK
    return jnp.where(top, va, vb)


def _probs(q2, k_op, rel, valid, slope, sink):
    sc = _dot_nt(q2, k_op) * (1.0 / math.sqrt(HEAD_DIM)) - slope * rel
    sc = jnp.where(valid, sc, NEG_INF)
    m = jnp.maximum(jnp.max(sc, axis=-1, keepdims=True), sink)
    p = jnp.exp(sc - m)
    e_sink = jnp.exp(sink - m)
    inv = 1.0 / (jnp.sum(p, axis=-1, keepdims=True) + e_sink)
    return p * inv, e_sink * inv


def _attn_fwd(qkv, gq2, gk2, sinks):
    s = qkv.shape[0]
    tq = _tile(s)
    nb = tq // ATT_BLOCK

    def body(q_ref, kv_ref, kvp_ref, gq_ref, gk_ref, sink_ref, out_ref):
        i = pl.program_id(0)
        lo = _lo_mask((1, LANES))
        kv_all = jnp.concatenate([kvp_ref[...], kv_ref[...]], axis=0)
        k_lo, k_hi, v_lo, v_hi, _, _ = _kv_variants(kv_all, gk_ref[...], lo)
        for b in range(nb):
            rel, valid = _att_consts(i == 0, b)
            rows = slice(b * ATT_BLOCK, (b + 1) * ATT_BLOCK)
            keys = slice(b * ATT_BLOCK, (b + 2) * ATT_BLOCK)
            for kvh in range(N_KV_HEADS):
                pairs = (2 * kvh, 2 * kvh + 1)
                q2 = jnp.concatenate([q_ref[rows, p * LANES:(p + 1) * LANES] for p in pairs], axis=0)
                qn, _ = _head_norm(q2, lo)
                q2 = (qn * gq_ref[...]).astype(BF16)
                out = None
                for odd, (k_op, v_op) in enumerate(((k_lo[kvh][keys], v_lo[kvh][keys]), (k_hi[kvh][keys], v_hi[kvh][keys]))):
                    ha, hb = 2 * pairs[0] + odd, 2 * pairs[1] + odd
                    p, _ = _probs(q2, k_op, rel, valid, _row_const(SLOPES[ha], SLOPES[hb]),
                                  _row_const(sink_ref[ha], sink_ref[hb]))
                    o = _dot(p.astype(BF16), v_op)
                    out = o if out is None else out + o
                for n, p in enumerate(pairs):
                    out_ref[rows, p * LANES:(p + 1) * LANES] = out[n * ATT_BLOCK:(n + 1) * ATT_BLOCK].astype(BF16)

    return pl.pallas_call(
        body, grid=(s // tq,),
        in_specs=[pl.BlockSpec((tq, Q_COLS), lambda i: (i, 0)),
                  pl.BlockSpec((tq, 2 * KV_COLS), lambda i: (i, 2)),
                  pl.BlockSpec((ATT_BLOCK, 2 * KV_COLS), lambda i: (jnp.maximum(i * nb - 1, 0), 2)),
                  _resident((1, LANES)), _resident((1, LANES)),
                  pl.BlockSpec(memory_space=pltpu.SMEM)],
        out_specs=pl.BlockSpec((tq, Q_COLS), lambda i: (i, 0)),
        out_shape=jax.ShapeDtypeStruct((s, Q_COLS), BF16),
        compiler_params=_cparams("parallel"), name="attn_fwd")(qkv, qkv, qkv, gq2, gk2, sinks)


def _group_stats(c1, lo):
    mu = _half_sums(c1, lo) * (1.0 / HEAD_DIM)
    d = c1 - mu
    rstd = lax.rsqrt(_half_sums(d * d, lo) * (1.0 / HEAD_DIM) + EPS)
    return d * rstd, rstd


def _conv_fwd(cin, cw, cb, gain, bias):
    s = cin.shape[0]
    tm = _tile(s)
    rc = 128

    def body(cin_ref, cw_ref, cb_ref, gain_ref, bias_ref, c3_ref, c1_ref, ext_ref):
        @pl.when(pl.program_id(0) == 0)
        def _():
            ext_ref[0:CONV_HALO, :] = jnp.zeros((CONV_HALO, CONV_WIDTH), F32)

        ext_ref[CONV_HALO:CONV_HALO + tm, :] = cin_ref[:, :CONV_WIDTH] * _sigmoid(cin_ref[:, CONV_WIDTH:])
        lo = _lo_mask((1, LANES))
        for cc in range(CONV_WIDTH // LANES):
            cols = slice(cc * LANES, (cc + 1) * LANES)
            for r in range(tm // rc):
                acc = jnp.zeros((rc, LANES), F32)
                for k in range(CONV_KERNEL):
                    acc = acc + cw_ref[k:k + 1, cols] * ext_ref[pl.ds(r * rc + CONV_HALO - (CONV_KERNEL - 1) + k, rc), cols]
                c1 = acc + cb_ref[:, cols]
                c1_ref[r * rc:(r + 1) * rc, cols] = c1
                nrm, _ = _group_stats(c1, lo)
                c2 = nrm * gain_ref[:, cols] + bias_ref[:, cols]
                c3_ref[r * rc:(r + 1) * rc, cols] = (c2 * _sigmoid(c2)).astype(BF16)
        ext_ref[0:CONV_HALO, :] = ext_ref[tm:tm + CONV_HALO, :]

    return pl.pallas_call(
        body, grid=(s // tm,),
        in_specs=[pl.BlockSpec((tm, CIN_COLS), lambda i: (i, 0)), _resident((CONV_KERNEL, CONV_WIDTH)),
                  _resident((1, CONV_WIDTH)), _resident((1, CONV_WIDTH)), _resident((1, CONV_WIDTH))],
        out_specs=[pl.BlockSpec((tm, CONV_WIDTH), lambda i: (i, 0)), pl.BlockSpec((tm, CONV_WIDTH), lambda i: (i, 0))],
        out_shape=[jax.ShapeDtypeStruct((s, CONV_WIDTH), BF16), jax.ShapeDtypeStruct((s, CONV_WIDTH), F32)],
        scratch_shapes=[pltpu.VMEM((tm + CONV_HALO, CONV_WIDTH), F32)],
        compiler_params=_cparams("arbitrary"), name="conv_fwd")(cin, cw, cb, gain, bias)


def _out_proj(x, attn, c3, wo_a, wo_c, b_out, g_ffn):
    s = x.shape[0]
    tm = _tile(s)

    def body(x_ref, a_ref, c_ref, wa_ref, wc_ref, b_ref, g_ref, x2_ref, h2_ref):
        x2 = x_ref[...] + _dot(a_ref[...], wa_ref[...]) + _dot(c_ref[...], wc_ref[...]) + b_ref[...]
        x2_ref[...] = x2
        r = lax.rsqrt(jnp.mean(x2 * x2, axis=-1, keepdims=True) + EPS)
        h2_ref[...] = (x2 * r * g_ref[...]).astype(BF16)

    return pl.pallas_call(
        body, grid=(s // tm,),
        in_specs=[pl.BlockSpec((tm, D_MODEL), lambda i: (i, 0)), pl.BlockSpec((tm, Q_COLS), lambda i: (i, 0)),
                  pl.BlockSpec((tm, CONV_WIDTH), lambda i: (i, 0)),
                  pl.BlockSpec((Q_COLS, D_MODEL), lambda i: (0, 0), pipeline_mode=pl.Buffered(1)),
                  pl.BlockSpec((CONV_WIDTH, D_MODEL), lambda i: (1, 0), pipeline_mode=pl.Buffered(1)),
                  _resident((1, D_MODEL)), _resident((1, D_MODEL))],
        out_specs=[pl.BlockSpec((tm, D_MODEL), lambda i: (i, 0)), pl.BlockSpec((tm, D_MODEL), lambda i: (i, 0))],
        out_shape=[jax.ShapeDtypeStruct((s, D_MODEL), F32), jax.ShapeDtypeStruct((s, D_MODEL), BF16)],
        compiler_params=_cparams("parallel"), name="out_proj")(x, attn, c3, wo_a, wo_c, b_out, g_ffn)


def _ffn_fwd(h2, x2, target, w_up, fw, fb, w_down):
    s = h2.shape[0]
    tm = _tile(s)
    hal = SUBLANES

    def body(h_ref, x2_ref, t_ref, wu_ref, fw_ref, fb_ref, wd_ref, up0_ref, act_ref, dy_ref, loss_ref,
             ext_ref, carry_ref, y_ref):
        i, ci = pl.program_id(0), pl.program_id(1)

        @pl.when((i == 0) & (ci == 0))
        def _():
            carry_ref[...] = jnp.zeros(carry_ref.shape, F32)
            loss_ref[...] = jnp.zeros((1, 1), F32)

        @pl.when(ci == 0)
        def _():
            y_ref[...] = x2_ref[...]

        h = h_ref[...]
        ups = []
        for half in range(2):
            c = ci + half * N_FF_PAIRS
            u0 = _dot(h, wu_ref[c])
            up0_ref[half, 0] = u0.astype(BF16)
            ext_ref[half, 0:hal, :] = carry_ref[c]
            ext_ref[half, hal:hal + tm, :] = u0
            carry_ref[c] = u0[tm - hal:tm, :]
            w = fw_ref[c]
            ups.append(w[0:1] * ext_ref[half, pl.ds(hal - 2, tm), :] + w[1:2] * ext_ref[half, pl.ds(hal - 1, tm), :]
                       + w[2:3] * u0 + fb_ref[c])
        g, u = ups
        act = (g * _sigmoid(g) * u).astype(BF16)
        act_ref[0] = act
        y_ref[...] += _dot(act, wd_ref[ci])

        @pl.when(ci == N_FF_PAIRS - 1)
        def _():
            e = y_ref[...] - t_ref[...]
            dy_ref[...] = e * (1.0 / D_MODEL)
            loss_ref[...] += (0.5 / D_MODEL) * jnp.sum(e * e).reshape(1, 1)

    tok = lambda i, ci: (i, 0)
    return pl.pallas_call(
        body, grid=(s // tm, N_FF_PAIRS),
        in_specs=[pl.BlockSpec((tm, D_MODEL), tok), pl.BlockSpec((tm, D_MODEL), tok), pl.BlockSpec((tm, D_MODEL), tok),
                  _resident((N_DEV, D_MODEL, FF_CHUNK)), _resident((N_DEV, 3, FF_CHUNK)), _resident((N_DEV, 1, FF_CHUNK)),
                  _resident((N_FF_PAIRS, FF_CHUNK, D_MODEL))],
        out_specs=[pl.BlockSpec((2, 1, tm, FF_CHUNK), lambda i, ci: (0, ci, i, 0)),
                   pl.BlockSpec((1, tm, FF_CHUNK), lambda i, ci: (ci, i, 0)),
                   pl.BlockSpec((tm, D_MODEL), tok), pl.BlockSpec((1, 1), lambda i, ci: (0, 0))],
        out_shape=[jax.ShapeDtypeStruct((2, N_FF_PAIRS, s, FF_CHUNK), BF16),
                   jax.ShapeDtypeStruct((N_FF_PAIRS, s, FF_CHUNK), BF16), jax.ShapeDtypeStruct((s, D_MODEL), F32),
                   jax.ShapeDtypeStruct((1, 1), F32)],
        scratch_shapes=[pltpu.VMEM((2, tm + hal, FF_CHUNK), F32), pltpu.VMEM((N_DEV, hal, FF_CHUNK), F32),
                        pltpu.VMEM((tm, D_MODEL), F32)],
        compiler_params=_cparams("arbitrary", "arbitrary"), name="ffn_fwd")(h2, x2, target, w_up, fw, fb, w_down)


def _ffn_bwd(dy, up0, x2, w_up, fw, fb, w_down, g_ffn, w_out):
    s = dy.shape[0]
    tm = _tile(s)
    nt = s // tm
    hal = 2 * SUBLANES
    nxt = SUBLANES

    def body(dy_ref, up0_ref, up0h_ref, x2_ref, wu_ref, fw_ref, fb_ref, wd_ref, g_ref, wo_ref,
             dup0_ref, dx2_ref, dmix_ref, dfw_ref, dfb_ref, dg_ref, dbo_ref,
             ext_ref, dext_ref, carry_ref, dh2_ref):
        i, ci = pl.program_id(0), pl.program_id(1)
        t = nt - 1 - i

        @pl.when((i == 0) & (ci == 0))
        def _():
            carry_ref[...] = jnp.zeros(carry_ref.shape, F32)
            dfw_ref[...] = jnp.zeros(dfw_ref.shape, F32)
            dfb_ref[...] = jnp.zeros(dfb_ref.shape, F32)
            dg_ref[...] = jnp.zeros(dg_ref.shape, F32)
            dbo_ref[...] = jnp.zeros(dbo_ref.shape, F32)

        @pl.when(ci == 0)
        def _():
            dh2_ref[...] = jnp.zeros(dh2_ref.shape, F32)

        dact = _dot_nt(dy_ref[...].astype(BF16), wd_ref[ci])
        ups = []
        for half in range(2):
            c = ci + half * N_FF_PAIRS
            ext_ref[half, 0:hal, :] = jnp.where(t > 0, up0h_ref[half, 0].astype(F32), 0.0)
            ext_ref[half, hal:hal + tm, :] = up0_ref[half, 0].astype(F32)
            w = fw_ref[c]
            ups.append(w[0:1] * ext_ref[half, pl.ds(hal - 2, tm), :] + w[1:2] * ext_ref[half, pl.ds(hal - 1, tm), :]
                       + w[2:3] * ext_ref[half, pl.ds(hal, tm), :] + fb_ref[c])
        g, u = ups
        sg = _sigmoid(g)
        grads = (dact * u * (sg * (1.0 + g * (1.0 - sg))), dact * (g * sg))
        for half, d in enumerate(grads):
            c = ci + half * N_FF_PAIRS
            dext_ref[half, 0:tm, :] = d
            dext_ref[half, tm:tm + nxt, :] = carry_ref[c]
            carry_ref[c] = d[0:nxt, :]
            w = fw_ref[c]
            dup0 = (w[2:3] * d + w[1:2] * dext_ref[half, pl.ds(1, tm), :] + w[0:1] * dext_ref[half, pl.ds(2, tm), :]).astype(BF16)
            dup0_ref[half, 0] = dup0
            dfb_ref[c] += jnp.sum(d, axis=0, keepdims=True)
            dfw_ref[c] += jnp.concatenate(
                [jnp.sum(d * ext_ref[half, pl.ds(hal - 2 + k, tm), :], axis=0, keepdims=True) for k in range(3)], axis=0)
            dh2_ref[...] += _dot_nt(dup0, wu_ref[c])

        @pl.when(ci == N_FF_PAIRS - 1)
        def _():
            x2v = x2_ref[...]
            r = lax.rsqrt(jnp.mean(x2v * x2v, axis=-1, keepdims=True) + EPS)
            n2 = x2v * r
            dh2 = dh2_ref[...]
            dg_ref[...] += jnp.sum(dh2 * n2, axis=0, keepdims=True)
            dn = dh2 * g_ref[...]
            dx2 = dy_ref[...] + r * (dn - n2 * jnp.mean(dn * n2, axis=-1, keepdims=True))
            dx2_ref[...] = dx2
            dbo_ref[...] += jnp.sum(dx2, axis=0, keepdims=True)
            dmix_ref[...] = _dot_nt(dx2.astype(BF16), wo_ref[...])

    tok = lambda i, ci: (nt - 1 - i, 0)
    acc = lambda shape: pl.BlockSpec(shape, lambda i, ci: (0,) * len(shape))
    return pl.pallas_call(
        body, grid=(nt, N_FF_PAIRS),
        in_specs=[pl.BlockSpec((tm, D_MODEL), tok),
                  pl.BlockSpec((2, 1, tm, FF_CHUNK), lambda i, ci: (0, ci, nt - 1 - i, 0)),
                  pl.BlockSpec((2, 1, hal, FF_CHUNK), lambda i, ci: (0, ci, jnp.maximum((nt - 1 - i) * (tm // hal) - 1, 0), 0)),
                  pl.BlockSpec((tm, D_MODEL), tok),
                  _resident((N_DEV, D_MODEL, FF_CHUNK)), _resident((N_DEV, 3, FF_CHUNK)), _resident((N_DEV, 1, FF_CHUNK)),
                  _resident((N_FF_PAIRS, FF_CHUNK, D_MODEL)), _resident((1, D_MODEL)), _resident((D_MODEL, D_MODEL))],
        out_specs=[pl.BlockSpec((2, 1, tm, FF_CHUNK), lambda i, ci: (0, ci, nt - 1 - i, 0)),
                   pl.BlockSpec((tm, D_MODEL), tok), pl.BlockSpec((tm, D_MODEL), tok),
                   acc((N_DEV, 3, FF_CHUNK)), acc((N_DEV, 1, FF_CHUNK)), acc((1, D_MODEL)), acc((1, D_MODEL))],
        out_shape=[jax.ShapeDtypeStruct((2, N_FF_PAIRS, s, FF_CHUNK), BF16),
                   jax.ShapeDtypeStruct((s, D_MODEL), F32), jax.ShapeDtypeStruct((s, D_MODEL), F32),
                   jax.ShapeDtypeStruct((N_DEV, 3, FF_CHUNK), F32), jax.ShapeDtypeStruct((N_DEV, 1, FF_CHUNK), F32),
                   jax.ShapeDtypeStruct((1, D_MODEL), F32), jax.ShapeDtypeStruct((1, D_MODEL), F32)],
        scratch_shapes=[pltpu.VMEM((2, tm + hal, FF_CHUNK), F32), pltpu.VMEM((2, tm + nxt, FF_CHUNK), F32),
                        pltpu.VMEM((N_DEV, nxt, FF_CHUNK), F32), pltpu.VMEM((tm, D_MODEL), F32)],
        compiler_params=_cparams("arbitrary", "arbitrary"), name="ffn_bwd")(
            dy, up0, up0, x2, w_up, fw, fb, w_down, g_ffn, w_out)


def _conv_bwd(dmixed, c1, cin, cw, gain, bias):
    s = c1.shape[0]
    tm = _tile(s)
    nt = s // tm
    rc = 128
    hal = CONV_HALO
    nchunk = CONV_WIDTH // LANES

    def body(dc3_ref, dc3n_ref, c1_ref, c1n_ref, cin_ref, cinp_ref, cw_ref, gain_ref, bias_ref,
             dcin_ref, dcw_ref, dcb_ref, dgain_ref, dbias_ref, dbcin_ref, c0_ext, dc1_ext, dcw8):
        i = pl.program_id(0)
        first, last = i == 0, i == nt - 1

        @pl.when(first)
        def _():
            for ref in (dcw8, dcb_ref, dgain_ref, dbias_ref, dbcin_ref):
                ref[...] = jnp.zeros(ref.shape, F32)

        c0_prev = cinp_ref[:, :CONV_WIDTH] * _sigmoid(cinp_ref[:, CONV_WIDTH:])
        c0_ext[0:hal, :] = jnp.where(first, 0.0, c0_prev)
        c0_ext[hal:hal + tm, :] = cin_ref[:, :CONV_WIDTH] * _sigmoid(cin_ref[:, CONV_WIDTH:])
        lo = _lo_mask((1, LANES))

        def norm_bwd(dc3, c1v, cols):
            nrm, rstd = _group_stats(c1v, lo)
            c2 = nrm * gain_ref[:, cols] + bias_ref[:, cols]
            sg = _sigmoid(c2)
            dc2 = dc3 * (sg * (1.0 + c2 * (1.0 - sg)))
            dn = dc2 * gain_ref[:, cols]
            inv = 1.0 / HEAD_DIM
            dc1 = rstd * (dn - _half_sums(dn, lo) * inv - nrm * (_half_sums(dn * nrm, lo) * inv))
            return dc1, dc2, nrm

        for cc in range(nchunk):
            cols = slice(cc * LANES, (cc + 1) * LANES)
            dc1, dc2, nrm = norm_bwd(dc3_ref[:, cols], c1_ref[:, cols], cols)
            dc1_ext[0:tm, cols] = dc1
            dgain_ref[:, cols] += jnp.sum(dc2 * nrm, axis=0, keepdims=True)
            dbias_ref[:, cols] += jnp.sum(dc2, axis=0, keepdims=True)
            dcb_ref[:, cols] += jnp.sum(dc1, axis=0, keepdims=True)
            dc1n, _, _ = norm_bwd(dc3n_ref[:, cols], c1n_ref[:, cols], cols)
            dc1_ext[tm:tm + hal, cols] = jnp.where(last, 0.0, dc1n)

        for cc in range(nchunk):
            cols = slice(cc * LANES, (cc + 1) * LANES)
            gcols = slice(CONV_WIDTH + cc * LANES, CONV_WIDTH + (cc + 1) * LANES)
            for r in range(tm // rc):
                rows = slice(r * rc, (r + 1) * rc)
                d = dc1_ext[rows, cols]
                dc0 = jnp.zeros((rc, LANES), F32)
                for k in range(CONV_KERNEL):
                    prod = d * c0_ext[pl.ds(r * rc + hal - (CONV_KERNEL - 1) + k, rc), cols]
                    dcw8[k * SUBLANES:(k + 1) * SUBLANES, cols] += jnp.sum(prod.reshape(rc // SUBLANES, SUBLANES, LANES), axis=0)
                    dc0 = dc0 + cw_ref[k:k + 1, cols] * dc1_ext[pl.ds(r * rc + CONV_KERNEL - 1 - k, rc), cols]
                a = cin_ref[rows, cols]
                sg = _sigmoid(cin_ref[rows, gcols])
                da = dc0 * sg
                dgate = dc0 * a * sg * (1.0 - sg)
                dcin_ref[rows, cols] = da.astype(BF16)
                dcin_ref[rows, gcols] = dgate.astype(BF16)
                dbcin_ref[:, cols] += jnp.sum(da, axis=0, keepdims=True)
                dbcin_ref[:, gcols] += jnp.sum(dgate, axis=0, keepdims=True)

        @pl.when(last)
        def _():
            for k in range(CONV_KERNEL):
                dcw_ref[k:k + 1, :] = jnp.sum(dcw8[k * SUBLANES:(k + 1) * SUBLANES, :], axis=0, keepdims=True)

    nh = tm // hal
    acc = lambda shape: pl.BlockSpec(shape, lambda i: (0,) * len(shape))
    return pl.pallas_call(
        body, grid=(nt,),
        in_specs=[pl.BlockSpec((tm, CONV_WIDTH), lambda i: (i, 1)),
                  pl.BlockSpec((hal, CONV_WIDTH), lambda i: (jnp.minimum((i + 1) * nh, s // hal - 1), 1)),
                  pl.BlockSpec((tm, CONV_WIDTH), lambda i: (i, 0)),
                  pl.BlockSpec((hal, CONV_WIDTH), lambda i: (jnp.minimum((i + 1) * nh, s // hal - 1), 0)),
                  pl.BlockSpec((tm, CIN_COLS), lambda i: (i, 0)),
                  pl.BlockSpec((hal, CIN_COLS), lambda i: (jnp.maximum(i * nh - 1, 0), 0)),
                  _resident((CONV_KERNEL, CONV_WIDTH)), _resident((1, CONV_WIDTH)), _resident((1, CONV_WIDTH))],
        out_specs=[pl.BlockSpec((tm, CIN_COLS), lambda i: (i, 0)), acc((CONV_KERNEL, CONV_WIDTH)), acc((1, CONV_WIDTH)),
                   acc((1, CONV_WIDTH)), acc((1, CONV_WIDTH)), acc((1, CIN_COLS))],
        out_shape=[jax.ShapeDtypeStruct((s, CIN_COLS), BF16), jax.ShapeDtypeStruct((CONV_KERNEL, CONV_WIDTH), F32),
                   jax.ShapeDtypeStruct((1, CONV_WIDTH), F32), jax.ShapeDtypeStruct((1, CONV_WIDTH), F32),
                   jax.ShapeDtypeStruct((1, CONV_WIDTH), F32), jax.ShapeDtypeStruct((1, CIN_COLS), F32)],
        scratch_shapes=[pltpu.VMEM((tm + hal, CONV_WIDTH), F32), pltpu.VMEM((tm + hal, CONV_WIDTH), F32),
                        pltpu.VMEM((CONV_KERNEL * SUBLANES, CONV_WIDTH), F32)],
        compiler_params=_cparams("arbitrary"), name="conv_bwd")(dmixed, dmixed, c1, c1, cin, cin, cw, gain, bias)


def _attn_bwd(qkv, dmixed, gq2, gk2, sinks):
    s = qkv.shape[0]
    tq = _tile(s)
    nb = tq // ATT_BLOCK
    nt = s // tq

    def body(q_ref, kv_ref, kvp_ref, do_ref, gq_ref, gk_ref, sink_ref,
             dqkv_ref, dgq_ref, dgk_ref, dsink_ref, dbqkv_ref, dk_acc, dv_acc, carry_k, carry_v):
        i = pl.program_id(0)
        t = nt - 1 - i

        @pl.when(i == 0)
        def _():
            for ref in (carry_k, carry_v, dgq_ref, dgk_ref, dsink_ref, dbqkv_ref):
                ref[...] = jnp.zeros(ref.shape, F32)

        lo = _lo_mask((1, LANES))
        lane_id = lax.broadcasted_iota(jnp.int32, (1, LANES), 1)
        kv_all = jnp.concatenate([kvp_ref[...], kv_ref[...]], axis=0)
        k_lo, k_hi, v_lo, v_hi, kn_pre, rk = _kv_variants(kv_all, gk_ref[...], lo)
        for acc_ref, carry in ((dk_acc, carry_k), (dv_acc, carry_v)):
            acc_ref[:, 0:tq, :] = jnp.zeros((N_KV_HEADS, tq, LANES), F32)
            acc_ref[:, tq:tq + ATT_BLOCK, :] = carry[...]
        dsink = jnp.zeros((1, LANES), F32)
        dgq = jnp.zeros((1, LANES), F32)
        gq = gq_ref[...]
        for b in range(nb):
            rel, valid = _att_consts(t == 0, b)
            rows = slice(b * ATT_BLOCK, (b + 1) * ATT_BLOCK)
            keys = slice(b * ATT_BLOCK, (b + 2) * ATT_BLOCK)
            for kvh in range(N_KV_HEADS):
                pairs = (2 * kvh, 2 * kvh + 1)
                q_raw = jnp.concatenate([q_ref[rows, p * LANES:(p + 1) * LANES] for p in pairs], axis=0)
                qn_pre, rq = _head_norm(q_raw, lo)
                q2 = (qn_pre * gq).astype(BF16)
                do2 = jnp.concatenate([do_ref[rows, p * LANES:(p + 1) * LANES] for p in pairs], axis=0).astype(BF16)
                dq2 = jnp.zeros((2 * ATT_BLOCK, LANES), F32)
                for odd, (k_op, v_op) in enumerate(((k_lo[kvh][keys], v_lo[kvh][keys]), (k_hi[kvh][keys], v_hi[kvh][keys]))):
                    ha, hb = 2 * pairs[0] + odd, 2 * pairs[1] + odd
                    p, p_sink = _probs(q2, k_op, rel, valid, _row_const(SLOPES[ha], SLOPES[hb]),
                                       _row_const(sink_ref[ha], sink_ref[hb]))
                    dp = _dot_nt(do2, v_op)
                    delta = jnp.sum(p * dp, axis=-1, keepdims=True)
                    ds = (p * (dp - delta) * (1.0 / math.sqrt(HEAD_DIM))).astype(BF16)
                    dsk = p_sink * delta
                    dsink = dsink - jnp.where(lane_id == ha, jnp.sum(dsk[0:ATT_BLOCK]), 0.0) \
                        - jnp.where(lane_id == hb, jnp.sum(dsk[ATT_BLOCK:]), 0.0)
                    dq2 = dq2 + _dot(ds, k_op)
                    half = lo if odd == 0 else jnp.logical_not(lo)
                    dk_acc[kvh, keys, :] += jnp.where(half, _dot_tn(ds, q2), 0.0)
                    dv_acc[kvh, keys, :] += jnp.where(half, _dot_tn(p.astype(BF16), do2), 0.0)
                dgq = dgq + jnp.sum(dq2 * qn_pre, axis=0, keepdims=True)
                dq_raw = _head_norm_bwd(dq2 * gq, qn_pre, rq, lo)
                for n, p_ in enumerate(pairs):
                    blk = dq_raw[n * ATT_BLOCK:(n + 1) * ATT_BLOCK]
                    dqkv_ref[rows, p_ * LANES:(p_ + 1) * LANES] = blk.astype(BF16)
                    dbqkv_ref[:, p_ * LANES:(p_ + 1) * LANES] += jnp.sum(blk, axis=0, keepdims=True)
        carry_k[...] = dk_acc[:, 0:ATT_BLOCK, :]
        carry_v[...] = dv_acc[:, 0:ATT_BLOCK, :]

        def fold(acc_ref):
            both = []
            for kvh in range(N_KV_HEADS):
                a = acc_ref[kvh, ATT_BLOCK:ATT_BLOCK + tq, :]
                both.append(a + pltpu.roll(a, HEAD_DIM, 1))
            return jnp.where(lo, both[0], both[1])

        dkn = fold(dk_acc)
        dv = fold(dv_acc)
        kn_c, rk_c = kn_pre[ATT_BLOCK:], rk[ATT_BLOCK:]
        dgk_ref[...] += jnp.sum(dkn * kn_c, axis=0, keepdims=True)
        dk_raw = _head_norm_bwd(dkn * gk_ref[...], kn_c, rk_c, lo)
        dqkv_ref[:, Q_COLS:Q_COLS + KV_COLS] = dk_raw.astype(BF16)
        dqkv_ref[:, Q_COLS + KV_COLS:] = dv.astype(BF16)
        dbqkv_ref[:, Q_COLS:Q_COLS + KV_COLS] += jnp.sum(dk_raw, axis=0, keepdims=True)
        dbqkv_ref[:, Q_COLS + KV_COLS:] += jnp.sum(dv, axis=0, keepdims=True)
        dgq_ref[...] += dgq
        dsink_ref[...] += dsink

        @pl.when(i == nt - 1)
        def _():
            for ref in (dgq_ref, dgk_ref):
                v = ref[...]
                ref[...] = v + pltpu.roll(v, HEAD_DIM, 1)

    acc = lambda shape: pl.BlockSpec(shape, lambda i: (0,) * len(shape))
    return pl.pallas_call(
        body, grid=(nt,),
        in_specs=[pl.BlockSpec((tq, Q_COLS), lambda i: (nt - 1 - i, 0)),
                  pl.BlockSpec((tq, 2 * KV_COLS), lambda i: (nt - 1 - i, 2)),
                  pl.BlockSpec((ATT_BLOCK, 2 * KV_COLS), lambda i: (jnp.maximum((nt - 1 - i) * nb - 1, 0), 2)),
                  pl.BlockSpec((tq, Q_COLS), lambda i: (nt - 1 - i, 0)),
                  _resident((1, LANES)), _resident((1, LANES)), pl.BlockSpec(memory_space=pltpu.SMEM)],
        out_specs=[pl.BlockSpec((tq, QKV_COLS), lambda i: (nt - 1 - i, 0)), acc((1, LANES)), acc((1, LANES)),
                   acc((1, LANES)), acc((1, QKV_COLS))],
        out_shape=[jax.ShapeDtypeStruct((s, QKV_COLS), BF16), jax.ShapeDtypeStruct((1, LANES), F32),
                   jax.ShapeDtypeStruct((1, LANES), F32), jax.ShapeDtypeStruct((1, LANES), F32),
                   jax.ShapeDtypeStruct((1, QKV_COLS), F32)],
        scratch_shapes=[pltpu.VMEM((N_KV_HEADS, tq + ATT_BLOCK, LANES), F32), pltpu.VMEM((N_KV_HEADS, tq + ATT_BLOCK, LANES), F32),
                        pltpu.VMEM((N_KV_HEADS, ATT_BLOCK, LANES), F32), pltpu.VMEM((N_KV_HEADS, ATT_BLOCK, LANES), F32)],
        compiler_params=_cparams("arbitrary"), name="attn_bwd")(qkv, qkv, qkv, dmixed, gq2, gk2, sinks)


def _in_bwd(dqkv, dcin, w_qkv, w_cin, x, dx2, g_mix):
    s = x.shape[0]
    tm = _tile(s)

    def body(dq_ref, dc_ref, wq_ref, wc_ref, x_ref, dx2_ref, g_ref, gx_ref, dg_ref):
        @pl.when(pl.program_id(0) == 0)
        def _():
            dg_ref[...] = jnp.zeros(dg_ref.shape, F32)

        dh = _dot_nt(dq_ref[...], wq_ref[...]) + _dot_nt(dc_ref[...], wc_ref[...])
        xv = x_ref[...]
        r = lax.rsqrt(jnp.mean(xv * xv, axis=-1, keepdims=True) + EPS)
        n = xv * r
        dg_ref[...] += jnp.sum(dh * n, axis=0, keepdims=True)
        dn = dh * g_ref[...]
        gx_ref[...] = dx2_ref[...] + r * (dn - n * jnp.mean(dn * n, axis=-1, keepdims=True))

    return pl.pallas_call(
        body, grid=(s // tm,),
        in_specs=[pl.BlockSpec((tm, QKV_COLS), lambda i: (i, 0)), pl.BlockSpec((tm, CIN_COLS), lambda i: (i, 0)),
                  _resident((D_MODEL, QKV_COLS)), _resident((D_MODEL, CIN_COLS)),
                  pl.BlockSpec((tm, D_MODEL), lambda i: (i, 0)), pl.BlockSpec((tm, D_MODEL), lambda i: (i, 0)),
                  _resident((1, D_MODEL))],
        out_specs=[pl.BlockSpec((tm, D_MODEL), lambda i: (i, 0)), pl.BlockSpec((1, D_MODEL), lambda i: (0, 0))],
        out_shape=[jax.ShapeDtypeStruct((s, D_MODEL), F32), jax.ShapeDtypeStruct((1, D_MODEL), F32)],
        compiler_params=_cparams("arbitrary"), name="in_bwd")(dqkv, dcin, w_qkv, w_cin, x, dx2, g_mix)


def _tn_matmul(a, b, name):
    ga, s, m = a.shape
    gb, _, n = b.shape
    g = max(ga, gb)
    tk = _tile(s)

    def body(a_ref, b_ref, o_ref):
        @pl.when(pl.program_id(1) == 0)
        def _():
            o_ref[...] = jnp.zeros(o_ref.shape, F32)

        o_ref[0] += _dot_tn(a_ref[0].astype(BF16), b_ref[0].astype(BF16))

    return pl.pallas_call(
        body, grid=(g, s // tk),
        in_specs=[pl.BlockSpec((1, tk, m), (lambda gi, k: (gi, k, 0)) if ga > 1 else (lambda gi, k: (0, k, 0))),
                  pl.BlockSpec((1, tk, n), (lambda gi, k: (gi, k, 0)) if gb > 1 else (lambda gi, k: (0, k, 0)))],
        out_specs=pl.BlockSpec((1, m, n), lambda gi, k: (gi, 0, 0)),
        out_shape=jax.ShapeDtypeStruct((g, m, n), F32),
        compiler_params=_cparams("parallel", "arbitrary"), name=name)(a, b)


def _position():
    return lax.axis_index("x"), lax.axis_index("y"), lax.axis_index("c")


def _dev_index(px, py, pc):
    return 4 * px + 2 * py + pc


def _flip(v, bit):
    return 1 - v if bit else v


OTHER_CHIPS = ((1, 0), (0, 1), (1, 1))


def _allgather_big(shards):
    n = len(shards)
    n_copies = 1 + 2 * len(OTHER_CHIPS)

    def body(*refs):
        ins, outs = refs[:n], refs[n:2 * n]
        send_sems, recv_sems = refs[2 * n:]
        x, y, c = _position()
        me, sibling = (x, y, c), (x, y, 1 - c)
        chips = [(_flip(x, fx), _flip(y, fy)) for fx, fy in OTHER_CHIPS]
        for a in range(n):
            outs[a][_dev_index(*me)] = ins[a][...].astype(BF16)

        def copy(a, k, block, to):
            rows = outs[a].at[_dev_index(*block)]
            return pltpu.make_async_remote_copy(src_ref=rows, dst_ref=rows, send_sem=send_sems.at[a, k],
                                                recv_sem=recv_sems.at[a, k], device_id=to, device_id_type=MESH)

        started = []
        for a in range(n):
            for j, chip in enumerate(chips):
                started.append(copy(a, 1 + j, me, (*chip, c)))
            started.append(copy(a, 0, me, sibling))
        for cp in started:
            cp.start()
        for a in range(n):
            for j, chip in enumerate(chips):
                copy(a, 1 + j, (*chip, c), me).wait_recv()
                fwd = copy(a, 1 + len(chips) + j, (*chip, c), sibling)
                fwd.start()
                started.append(fwd)
        for a in range(n):
            copy(a, 0, sibling, me).wait_recv()
            for j, chip in enumerate(chips):
                copy(a, 1 + len(chips) + j, (*chip, 1 - c), me).wait_recv()
        for cp in started:
            cp.wait_send()

    vmem = pl.BlockSpec(memory_space=pltpu.VMEM)
    return pl.pallas_call(
        body, in_specs=[vmem] * n, out_specs=[vmem] * n,
        out_shape=[jax.ShapeDtypeStruct((N_DEV,) + w.shape, BF16) for w in shards],
        scratch_shapes=[pltpu.SemaphoreType.DMA((n, n_copies)), pltpu.SemaphoreType.DMA((n, n_copies))],
        compiler_params=pltpu.CompilerParams(vmem_limit_bytes=VMEM_LIMIT), name="allgather_weights")(*shards)


def _reduce_scatter(g8, name):
    _, r, c_ = g8.shape
    rel = ((0, 0),) + OTHER_CHIPS

    def body(g_ref, out_ref, stage, send_a, recv_a, send_b, recv_b, sa_send, sa_recv, sb_send, sb_recv):
        x, y, c = _position()
        sibling = (x, y, 1 - c)
        chips = [(_flip(x, fx), _flip(y, fy)) for fx, fy in rel]

        def copy_a(j):
            return pltpu.make_async_remote_copy(src_ref=send_a.at[j], dst_ref=recv_a.at[j], send_sem=sa_send.at[j],
                                                recv_sem=sa_recv.at[j], device_id=sibling, device_id_type=MESH)

        def copy_b(j):
            return pltpu.make_async_remote_copy(src_ref=send_b.at[j], dst_ref=recv_b.at[j], send_sem=sb_send.at[j],
                                                recv_sem=sb_recv.at[j], device_id=(*chips[1 + j], c), device_id_type=MESH)

        for j, chip in enumerate(chips):
            pltpu.sync_copy(g_ref.at[_dev_index(*chip, 1 - c)], stage)
            send_a[j] = stage[...].astype(BF16)
            copy_a(j).start()
        for j, chip in enumerate(chips):
            pltpu.sync_copy(g_ref.at[_dev_index(*chip, c)], stage)
            copy_a(j).wait_recv()
            part = stage[...] + recv_a[j].astype(F32)
            if j == 0:
                out_ref[...] = part
            else:
                send_b[j - 1] = part.astype(BF16)
                copy_b(j - 1).start()
        for j in range(len(OTHER_CHIPS)):
            copy_b(j).wait_recv()
            out_ref[...] += recv_b[j].astype(F32)
        for j in range(len(rel)):
            copy_a(j).wait_send()
        for j in range(len(OTHER_CHIPS)):
            copy_b(j).wait_send()

    na, nb = len(rel), len(OTHER_CHIPS)
    return pl.pallas_call(
        body, in_specs=[pl.BlockSpec(memory_space=pl.ANY)], out_specs=pl.BlockSpec(memory_space=pltpu.VMEM),
        out_shape=jax.ShapeDtypeStruct((r, c_), F32),
        scratch_shapes=[pltpu.VMEM((r, c_), F32), pltpu.VMEM((na, r, c_), BF16), pltpu.VMEM((na, r, c_), BF16),
                        pltpu.VMEM((nb, r, c_), BF16), pltpu.VMEM((nb, r, c_), BF16),
                        pltpu.SemaphoreType.DMA((na,)), pltpu.SemaphoreType.DMA((na,)),
                        pltpu.SemaphoreType.DMA((nb,)), pltpu.SemaphoreType.DMA((nb,))],
        compiler_params=pltpu.CompilerParams(vmem_limit_bytes=VMEM_LIMIT), name=name)(g8)


def _small_exchange(v, reduce, name):
    rows = v.shape[0]

    def body(v_ref, out_ref, *scratch):
        gath = scratch[0] if reduce else out_ref
        send_sems, recv_sems = scratch[-2:]
        x, y, c = _position()
        me = _dev_index(x, y, c)
        gath[me] = v_ref[...]
        peers = [(_flip(x, k >> 2 & 1), _flip(y, k >> 1 & 1), _flip(c, k & 1)) for k in range(1, N_DEV)]

        def copy(k, block):
            return pltpu.make_async_remote_copy(src_ref=gath.at[block], dst_ref=gath.at[block], send_sem=send_sems.at[k],
                                                recv_sem=recv_sems.at[k], device_id=peers[k], device_id_type=MESH)

        for k in range(N_DEV - 1):
            copy(k, me).start()
        for k in range(N_DEV - 1):
            copy(k, _dev_index(*peers[k])).wait_recv()
        for k in range(N_DEV - 1):
            copy(k, me).wait_send()
        if reduce:
            total = gath[0]
            for d in range(1, N_DEV):
                total = total + gath[d]
            out_ref[...] = total

    vmem = pl.BlockSpec(memory_space=pltpu.VMEM)
    out_shape = (rows, LANES) if reduce else (N_DEV, rows, LANES)
    return pl.pallas_call(
        body, in_specs=[vmem], out_specs=vmem, out_shape=jax.ShapeDtypeStruct(out_shape, F32),
        scratch_shapes=([pltpu.VMEM((N_DEV, rows, LANES), F32)] if reduce else [])
        + [pltpu.SemaphoreType.DMA((N_DEV - 1,)), pltpu.SemaphoreType.DMA((N_DEV - 1,))],
        compiler_params=pltpu.CompilerParams(vmem_limit_bytes=VMEM_LIMIT), name=name)(v)


def _row_tile(r):
    for n in (8, 4, 2):
        if r % (n * SUBLANES) == 0:
            return r // n
    return r


def _adamw(w, g, m, v, name):
    r, c_ = w.shape
    tr = _row_tile(r)

    def body(w_ref, g_ref, m_ref, v_ref, d_ref, mo_ref, vo_ref):
        gv = g_ref[...]
        mn = ADAM_B1 * m_ref[...] + (1.0 - ADAM_B1) * gv
        vn = ADAM_B2 * v_ref[...] + (1.0 - ADAM_B2) * (gv * gv)
        m_hat = mn / (1.0 - ADAM_B1 ** ADAM_STEP)
        v_hat = vn / (1.0 - ADAM_B2 ** ADAM_STEP)
        d_ref[...] = -ADAM_LR * (m_hat / (jnp.sqrt(v_hat) + ADAM_EPS) + ADAM_WD * w_ref[...])
        mo_ref[...] = mn
        vo_ref[...] = vn

    spec = pl.BlockSpec((tr, c_), lambda i: (i, 0))
    return pl.pallas_call(
        body, grid=(r // tr,), in_specs=[spec] * 4, out_specs=[spec] * 3,
        out_shape=[jax.ShapeDtypeStruct((r, c_), F32)] * 3,
        compiler_params=_cparams("parallel"), name=name)(w, g, m, v)


def _padded(n):
    return n + (-n) % LANES


def _pack(arrs):
    flat = jnp.concatenate([jnp.pad(a.reshape(-1), (0, _padded(a.size) - a.size)) for a in arrs])
    flat = jnp.pad(flat, (0, (-flat.size) % (SUBLANES * LANES)))
    return flat.reshape(-1, LANES)


def _unpack(packed, shapes):
    flat = packed.reshape(-1)
    out, off = [], 0
    for shp in shapes:
        n = int(np.prod(shp))
        out.append(flat[off:off + n].reshape(shp))
        off += _padded(n)
    return out


def _local_step(x, target, p):
    s = x.shape[0]
    qkv, cin, h1 = _mix_proj(x, p["g_mix"], p["w_qkv"], p["w_cin"], p["b_qkv"], p["b_cin"])
    attn = _attn_fwd(qkv, p["gq2"], p["gk2"], p["sinks"])
    c3, c1 = _conv_fwd(cin, p["cw"], p["cb"], p["cgain"], p["cbias"])
    x2, h2 = _out_proj(x, attn, c3, p["w_out"], p["w_out"], p["b_out"], p["g_ffn"])
    up0, act, dy, loss = _ffn_fwd(h2, x2, target, p["w_up"], p["fw"], p["fb"], p["w_down"])
    dup0, dx2, dmixed, dfw, dfb, dg_ffn, db_out = _ffn_bwd(dy, up0, x2, p["w_up"], p["fw"], p["fb"], p["w_down"],
                                                           p["g_ffn"], p["w_out"])
    dcin, dcw, dcb, dcgain, dcbias, dbcin = _conv_bwd(dmixed, c1, cin, p["cw"], p["cgain"], p["cbias"])
    dqkv, dgq, dgk, dsink, dbqkv = _attn_bwd(qkv, dmixed, p["gq2"], p["gk2"], p["sinks"])
    grad_x, dg_mix = _in_bwd(dqkv, dcin, p["w_qkv"], p["w_cin"], x, dx2, p["g_mix"])
    big = {
        "w_up": _tn_matmul(h2[None], dup0.reshape(N_DEV, s, FF_CHUNK), "dw_up"),
        "w_down": _tn_matmul(act, dy[None], "dw_down"),
        "w_qkv": _tn_matmul(h1[None], dqkv[None], "dw_qkv")[0],
        "w_cin": _tn_matmul(h1[None], dcin[None], "dw_cin")[0],
        "wo_attn": _tn_matmul(attn[None], dx2[None], "dw_out_attn")[0],
        "wo_conv": _tn_matmul(c3[None], dx2[None], "dw_out_conv")[0],
    }
    small = {
        "mix_norm_gain": dg_mix, "b_in": jnp.concatenate([dbqkv, dbcin], axis=1),
        "q_norm_gain": dgq[:, :HEAD_DIM], "k_norm_gain": dgk[:, :HEAD_DIM], "attn_sinks": dsink[:, :N_Q_HEADS],
        "conv_dw_w": dcw, "conv_dw_b": dcb, "conv_norm_gain": dcgain, "conv_norm_bias": dcbias,
        "b_out": db_out, "ffn_norm_gain": dg_ffn, "ffn_dw_w": dfw, "ffn_dw_b": dfb,
    }
    return loss, grad_x, big, small


BIG = ("w_in", "w_out", "w_up", "w_down")
SMALL = ("mix_norm_gain", "b_in", "q_norm_gain", "k_norm_gain", "attn_sinks", "conv_dw_w", "conv_dw_b",
         "conv_norm_gain", "conv_norm_bias", "b_out", "ffn_norm_gain", "ffn_dw_w", "ffn_dw_b")
ORDER = ("mix_norm_gain", "w_in", "b_in", "q_norm_gain", "k_norm_gain", "attn_sinks", "conv_dw_w", "conv_dw_b",
         "conv_norm_gain", "conv_norm_bias", "w_out", "b_out", "ffn_norm_gain", "w_up", "ffn_dw_w", "ffn_dw_b", "w_down")


def kernel(x, mix_norm_gain, w_in, b_in, q_norm_gain, k_norm_gain, attn_sinks, conv_dw_w, conv_dw_b, conv_norm_gain, conv_norm_bias, w_out, b_out, ffn_norm_gain, w_up, ffn_dw_w, ffn_dw_b, w_down, loss_target, m_mix_norm_gain, m_w_in, m_b_in, m_q_norm_gain, m_k_norm_gain, m_attn_sinks, m_conv_dw_w, m_conv_dw_b, m_conv_norm_gain, m_conv_norm_bias, m_w_out, m_b_out, m_ffn_norm_gain, m_w_up, m_ffn_dw_w, m_ffn_dw_b, m_w_down, v_mix_norm_gain, v_w_in, v_b_in, v_q_norm_gain, v_k_norm_gain, v_attn_sinks, v_conv_dw_w, v_conv_dw_b, v_conv_norm_gain, v_conv_norm_bias, v_w_out, v_b_out, v_ffn_norm_gain, v_w_up, v_ffn_dw_w, v_ffn_dw_b, v_w_down):
    w = dict(mix_norm_gain=mix_norm_gain, w_in=w_in, b_in=b_in, q_norm_gain=q_norm_gain, k_norm_gain=k_norm_gain,
             attn_sinks=attn_sinks, conv_dw_w=conv_dw_w, conv_dw_b=conv_dw_b, conv_norm_gain=conv_norm_gain,
             conv_norm_bias=conv_norm_bias, w_out=w_out, b_out=b_out, ffn_norm_gain=ffn_norm_gain, w_up=w_up,
             ffn_dw_w=ffn_dw_w, ffn_dw_b=ffn_dw_b, w_down=w_down)
    m = dict(mix_norm_gain=m_mix_norm_gain, w_in=m_w_in, b_in=m_b_in, q_norm_gain=m_q_norm_gain, k_norm_gain=m_k_norm_gain,
             attn_sinks=m_attn_sinks, conv_dw_w=m_conv_dw_w, conv_dw_b=m_conv_dw_b, conv_norm_gain=m_conv_norm_gain,
             conv_norm_bias=m_conv_norm_bias, w_out=m_w_out, b_out=m_b_out, ffn_norm_gain=m_ffn_norm_gain, w_up=m_w_up,
             ffn_dw_w=m_ffn_dw_w, ffn_dw_b=m_ffn_dw_b, w_down=m_w_down)
    v = dict(mix_norm_gain=v_mix_norm_gain, w_in=v_w_in, b_in=v_b_in, q_norm_gain=v_q_norm_gain, k_norm_gain=v_k_norm_gain,
             attn_sinks=v_attn_sinks, conv_dw_w=v_conv_dw_w, conv_dw_b=v_conv_dw_b, conv_norm_gain=v_conv_norm_gain,
             conv_norm_bias=v_conv_norm_bias, w_out=v_w_out, b_out=v_b_out, ffn_norm_gain=v_ffn_norm_gain, w_up=v_w_up,
             ffn_dw_w=v_ffn_dw_w, ffn_dw_b=v_ffn_dw_b, w_down=v_w_down)
    me = _dev_index(*_position())
    s = x.shape[1]
    cw_cols = CONV_WIDTH // N_DEV

    wi8, wo8, wu8, wd8 = _allgather_big([w_in, w_out, w_up, w_down])
    dw8 = _small_exchange(_pack([conv_dw_w, ffn_dw_w]), False, "allgather_dw").reshape(N_DEV, -1)
    n_cw = CONV_KERNEL * cw_cols
    w_in_full = wi8.transpose(1, 0, 2).reshape(D_MODEL, QKV_COLS + CIN_COLS)
    p = {
        "g_mix": mix_norm_gain.reshape(1, -1), "w_qkv": w_in_full[:, :QKV_COLS], "w_cin": w_in_full[:, QKV_COLS:],
        "b_qkv": b_in[:QKV_COLS].reshape(1, -1), "b_cin": b_in[QKV_COLS:].reshape(1, -1),
        "gq2": jnp.tile(q_norm_gain, 2).reshape(1, -1), "gk2": jnp.tile(k_norm_gain, 2).reshape(1, -1), "sinks": attn_sinks,
        "cw": dw8[:, :n_cw].reshape(N_DEV, CONV_KERNEL, cw_cols).transpose(1, 0, 2).reshape(CONV_KERNEL, CONV_WIDTH),
        "cb": conv_dw_b.reshape(1, -1), "cgain": conv_norm_gain.reshape(1, -1), "cbias": conv_norm_bias.reshape(1, -1),
        "w_out": wo8.reshape(D_MODEL, D_MODEL), "b_out": b_out.reshape(1, -1), "g_ffn": ffn_norm_gain.reshape(1, -1),
        "w_up": wu8, "fw": dw8[:, _padded(n_cw):_padded(n_cw) + 3 * FF_CHUNK].reshape(N_DEV, 3, FF_CHUNK),
        "fb": ffn_dw_b.reshape(N_DEV, 1, FF_CHUNK), "w_down": wd8.reshape(N_FF_PAIRS, FF_CHUNK, D_MODEL),
    }

    loss, grad_x, big, small = _local_step(x[0], loss_target[0], p)

    dw_in = jnp.concatenate([big["w_qkv"], big["w_cin"]], axis=1)
    dw_out = jnp.concatenate([big["wo_attn"], big["wo_conv"]], axis=0)
    g = {
        "w_in": _reduce_scatter(dw_in.reshape(D_MODEL, N_DEV, -1).transpose(1, 0, 2), "rs_w_in"),
        "w_out": _reduce_scatter(dw_out.reshape(N_DEV, -1, D_MODEL), "rs_w_out"),
        "w_up": _reduce_scatter(big["w_up"], "rs_w_up"),
        "w_down": _reduce_scatter(big["w_down"].reshape(N_DEV, -1, D_MODEL), "rs_w_down"),
    }
    full_shapes = [small[n].shape for n in SMALL]
    red = dict(zip(SMALL, _unpack(_small_exchange(_pack([small[n] for n in SMALL]), True, "allreduce_small"), full_shapes)))
    for n in SMALL:
        if n == "conv_dw_w":
            g[n] = lax.dynamic_slice(red[n], (0, me * cw_cols), (CONV_KERNEL, cw_cols))
        elif n == "ffn_dw_w":
            g[n] = lax.dynamic_index_in_dim(red[n], me, axis=0, keepdims=False)
        else:
            g[n] = red[n].reshape(w[n].shape)

    delta, new_m, new_v = {}, {}, {}
    for n in BIG:
        delta[n], new_m[n], new_v[n] = _adamw(w[n], g[n], m[n], v[n], "adamw_" + n)
    shapes = [w[n].shape for n in SMALL]
    outs = _adamw(*[_pack([d[n] for n in SMALL]) for d in (w, g, m, v)], "adamw_small")
    for d, packed in zip((delta, new_m, new_v), outs):
        d.update(zip(SMALL, _unpack(packed, shapes)))

    total = lax.psum(loss[0, 0], ("x", "y", "c"))
    return (total, grad_x.reshape(1, s, D_MODEL), *[g[n] for n in ORDER], *[delta[n] for n in ORDER],
            *[new_m[n] for n in ORDER], *[new_v[n] for n in ORDER])
```

```python
import math

import jax
import jax.numpy as jnp
from jax import lax
from jax.experimental import pallas as pl
from jax.experimental.pallas import tpu as pltpu

F32 = jnp.float32
BF16 = jnp.bfloat16

D_MODEL = 1024
HEAD_DIM = 64
N_Q_HEADS = 8
N_KV_HEADS = 2
Q_COLS = 512
KV_COLS = 128
QKV_COLS = Q_COLS + 2 * KV_COLS
CONV_WIDTH = 512
CIN_COLS = 2 * CONV_WIDTH
CONV_KERNEL = 31
CONV_HALO = 32
D_FF = 2816
N_DEV = 8
FF_CHUNK = 2 * D_FF // N_DEV
N_FF_PAIRS = N_DEV // 2
ATT_BLOCK = 128
EPS = 1e-6
NEG_INF = -1e30
SLOPES = [float(2.0 ** (-8.0 * (h + 1.0) / N_Q_HEADS)) for h in range(N_Q_HEADS)]

ADAM_LR = 0.001
ADAM_B1 = 0.9
ADAM_B2 = 0.999
ADAM_EPS = 1e-08
ADAM_WD = 0.01
ADAM_STEP = 10

LANES = 128
SUBLANES = 8
VMEM_LIMIT = 56 * 1024 * 1024
MESH = pl.DeviceIdType.MESH


def _cparams(*sem, **kw):
    return pltpu.CompilerParams(dimension_semantics=sem or None, vmem_limit_bytes=VMEM_LIMIT, **kw)


def _resident(shape):
    nd = len(shape)
    return pl.BlockSpec(shape, lambda *_: (0,) * nd, pipeline_mode=pl.Buffered(1))


def _dot(a, b):
    return jnp.dot(a, b, preferred_element_type=F32)


def _dot_nt(a, b):
    return lax.dot_general(a, b, (((1,), (1,)), ((), ())), preferred_element_type=F32)


def _dot_tn(a, b):
    return lax.dot_general(a, b, (((0,), (0,)), ((), ())), preferred_element_type=F32)


def _sigmoid(x):
    return 1.0 / (1.0 + jnp.exp(-x))


def _lo_mask(shape):
    return lax.broadcasted_iota(jnp.int32, shape, len(shape) - 1) % LANES < HEAD_DIM


def _half_sums(t, lo):
    s_lo = jnp.sum(jnp.where(lo, t, 0.0), axis=-1, keepdims=True)
    s_hi = jnp.sum(jnp.where(lo, 0.0, t), axis=-1, keepdims=True)
    return jnp.where(lo, s_lo, s_hi)


def _head_norm(t, lo):
    r = lax.rsqrt(_half_sums(t * t, lo) * (1.0 / HEAD_DIM) + EPS)
    return t * r, r


def _head_norm_bwd(dn, n, r, lo):
    return r * (dn - n * (_half_sums(dn * n, lo) * (1.0 / HEAD_DIM)))


def _tile(s):
    return min(512, s)


def _mix_proj(x, g_mix, w_qkv, w_cin, b_qkv, b_cin):
    s = x.shape[0]
    tm = _tile(s)

    def body(x_ref, g_ref, wq_ref, wc_ref, bq_ref, bc_ref, qkv_ref, cin_ref, h1_ref):
        xv = x_ref[...]
        r = lax.rsqrt(jnp.mean(xv * xv, axis=-1, keepdims=True) + EPS)
        h = (xv * r * g_ref[...]).astype(BF16)
        h1_ref[...] = h
        qkv_ref[...] = _dot(h, wq_ref[...]) + bq_ref[...]
        cin_ref[...] = _dot(h, wc_ref[...]) + bc_ref[...]

    return pl.pallas_call(
        body, grid=(s // tm,),
        in_specs=[pl.BlockSpec((tm, D_MODEL), lambda i: (i, 0)), _resident((1, D_MODEL)),
                  _resident((D_MODEL, QKV_COLS)), _resident((D_MODEL, CIN_COLS)),
                  _resident((1, QKV_COLS)), _resident((1, CIN_COLS))],
        out_specs=[pl.BlockSpec((tm, QKV_COLS), lambda i: (i, 0)), pl.BlockSpec((tm, CIN_COLS), lambda i: (i, 0)),
                   pl.BlockSpec((tm, D_MODEL), lambda i: (i, 0))],
        out_shape=[jax.ShapeDtypeStruct((s, QKV_COLS), F32), jax.ShapeDtypeStruct((s, CIN_COLS), F32),
                   jax.ShapeDtypeStruct((s, D_MODEL), BF16)],
        compiler_params=_cparams("parallel"), name="mix_proj")(x, g_mix, w_qkv, w_cin, b_qkv, b_cin)


def _kv_variants(kv_all, gk2, lo):
    k_all = kv_all[:, :LANES]
    v_all = kv_all[:, LANES:]
    kn_pre, rk = _head_norm(k_all, lo)
    kn = kn_pre * gk2
    kr = pltpu.roll(kn, HEAD_DIM, 1)
    vr = pltpu.roll(v_all, HEAD_DIM, 1)
    zero = jnp.zeros_like(kn)
    k_lo = [jnp.where(lo, kn, zero).astype(BF16), jnp.where(lo, kr, zero).astype(BF16)]
    k_hi = [jnp.where(lo, zero, kr).astype(BF16), jnp.where(lo, zero, kn).astype(BF16)]
    v_lo = [jnp.where(lo, v_all, zero).astype(BF16), jnp.where(lo, vr, zero).astype(BF16)]
    v_hi = [jnp.where(lo, zero, vr).astype(BF16), jnp.where(lo, zero, v_all).astype(BF16)]
    return k_lo, k_hi, v_lo, v_hi, kn_pre, rk


def _att_consts(first_tile, b):
    rows = 2 * ATT_BLOCK
    qi = lax.broadcasted_iota(jnp.int32, (rows, 2 * ATT_BLOCK), 0) % ATT_BLOCK
    kj = lax.broadcasted_iota(jnp.int32, (rows, 2 * ATT_BLOCK), 1)
    rel = qi + ATT_BLOCK - kj
    valid = (rel >= 0) & (rel < ATT_BLOCK)
    if b == 0:
        valid = valid & ((kj >= ATT_BLOCK) | jnp.logical_not(first_tile))
    return rel.astype(F32), valid


def _row_const(va, vb):
    top = lax.broadcasted_iota(jnp.int32, (2 * ATT_BLOCK, 1), 0) < ATT_BLOCK
    return jnp.where(top, va, vb)


def _probs(q2, k_op, rel, valid, slope, sink):
    sc = _dot_nt(q2, k_op) * (1.0 / math.sqrt(HEAD_DIM)) - slope * rel
    sc = jnp.where(valid, sc, NEG_INF)
    m = jnp.maximum(jnp.max(sc, axis=-1, keepdims=True), sink)
    p = jnp.exp(sc - m)
    e_sink = jnp.exp(sink - m)
    inv = 1.0 / (jnp.sum(p, axis=-1, keepdims=True) + e_sink)
    return p * inv, e_sink * inv


def _attn_fwd(qkv, gq2, gk2, sinks):
    s = qkv.shape[0]
    tq = _tile(s)
    nb = tq // ATT_BLOCK

    def body(q_ref, kv_ref, kvp_ref, gq_ref, gk_ref, sink_ref, out_ref):
        i = pl.program_id(0)
        lo = _lo_mask((1, LANES))
        kv_all = jnp.concatenate([kvp_ref[...], kv_ref[...]], axis=0)
        k_lo, k_hi, v_lo, v_hi, _, _ = _kv_variants(kv_all, gk_ref[...], lo)
        for b in range(nb):
            rel, valid = _att_consts(i == 0, b)
            rows = slice(b * ATT_BLOCK, (b + 1) * ATT_BLOCK)
            keys = slice(b * ATT_BLOCK, (b + 2) * ATT_BLOCK)
            for kvh in range(N_KV_HEADS):
                pairs = (2 * kvh, 2 * kvh + 1)
                q2 = jnp.concatenate([q_ref[rows, p * LANES:(p + 1) * LANES] for p in pairs], axis=0)
                qn, _ = _head_norm(q2, lo)
                q2 = (qn * gq_ref[...]).astype(BF16)
                out = None
                for odd, (k_op, v_op) in enumerate(((k_lo[kvh][keys], v_lo[kvh][keys]), (k_hi[kvh][keys], v_hi[kvh][keys]))):
                    ha, hb = 2 * pairs[0] + odd, 2 * pairs[1] + odd
                    p, _ = _probs(q2, k_op, rel, valid, _row_const(SLOPES[ha], SLOPES[hb]),
                                  _row_const(sink_ref[ha], sink_ref[hb]))
                    o = _dot(p.astype(BF16), v_op)
                    out = o if out is None else out + o
                for n, p in enumerate(pairs):
                    out_ref[rows, p * LANES:(p + 1) * LANES] = out[n * ATT_BLOCK:(n + 1) * ATT_BLOCK].astype(BF16)

    return pl.pallas_call(
        body, grid=(s // tq,),
        in_specs=[pl.BlockSpec((tq, Q_COLS), lambda i: (i, 0)),
                  pl.BlockSpec((tq, 2 * KV_COLS), lambda i: (i, 2)),
                  pl.BlockSpec((ATT_BLOCK, 2 * KV_COLS), lambda i: (jnp.maximum(i * nb - 1, 0), 2)),
                  _resident((1, LANES)), _resident((1, LANES)),
                  pl.BlockSpec(memory_space=pltpu.SMEM)],
        out_specs=pl.BlockSpec((tq, Q_COLS), lambda i: (i, 0)),
        out_shape=jax.ShapeDtypeStruct((s, Q_COLS), BF16),
        compiler_params=_cparams("parallel"), name="attn_fwd")(qkv, qkv, qkv, gq2, gk2, sinks)


def _group_stats(c1, lo):
    mu = _half_sums(c1, lo) * (1.0 / HEAD_DIM)
    d = c1 - mu
    rstd = lax.rsqrt(_half_sums(d * d, lo) * (1.0 / HEAD_DIM) + EPS)
    return d * rstd, rstd


def _rows(ref, first_row, n):
    return ref[pl.ds(first_row, n, stride=1), :].reshape(n // SUBLANES, SUBLANES, LANES)


def _conv_fwd(cin, cw8, cb, gain, bias):
    s = cin.shape[0]
    tm = _tile(s)
    rc = 64
    nchunk = CONV_WIDTH // LANES
    lead = CONV_HALO - (CONV_KERNEL - 1)

    def body(cin_ref, cw_ref, cb_ref, gain_ref, bias_ref, c3_ref, c1_ref, ext_ref):
        @pl.when(pl.program_id(0) == 0)
        def _():
            ext_ref[:, 0:CONV_HALO, :] = jnp.zeros((nchunk, CONV_HALO, LANES), F32)

        lo = _lo_mask((1, LANES))
        for cc in range(nchunk):
            cols = slice(cc * LANES, (cc + 1) * LANES)
            gcols = slice(CONV_WIDTH + cc * LANES, CONV_WIDTH + (cc + 1) * LANES)
            ext_ref[cc, CONV_HALO:CONV_HALO + tm, :] = cin_ref[:, cols] * _sigmoid(cin_ref[:, gcols])
            ext = ext_ref.at[cc]
            for r in range(tm // rc):
                rows = slice(r * rc, (r + 1) * rc)
                acc = jnp.zeros((rc // SUBLANES, SUBLANES, LANES), F32)
                for k in range(CONV_KERNEL):
                    acc = acc + cw_ref[k * SUBLANES:(k + 1) * SUBLANES, cols][None] * _rows(ext, r * rc + lead + k, rc)
                c1 = acc.reshape(rc, LANES) + cb_ref[:, cols]
                c1_ref[cc, rows, :] = c1
                nrm, _ = _group_stats(c1, lo)
                c2 = nrm * gain_ref[:, cols] + bias_ref[:, cols]
                c3_ref[rows, cols] = (c2 * _sigmoid(c2)).astype(BF16)
        ext_ref[:, 0:CONV_HALO, :] = ext_ref[:, tm:tm + CONV_HALO, :]

    return pl.pallas_call(
        body, grid=(s // tm,),
        in_specs=[pl.BlockSpec((tm, CIN_COLS), lambda i: (i, 0)), _resident((CONV_KERNEL * SUBLANES, CONV_WIDTH)),
                  _resident((1, CONV_WIDTH)), _resident((1, CONV_WIDTH)), _resident((1, CONV_WIDTH))],
        out_specs=[pl.BlockSpec((tm, CONV_WIDTH), lambda i: (i, 0)), pl.BlockSpec((nchunk, tm, LANES), lambda i: (0, i, 0))],
        out_shape=[jax.ShapeDtypeStruct((s, CONV_WIDTH), BF16), jax.ShapeDtypeStruct((nchunk, s, LANES), F32)],
        scratch_shapes=[pltpu.VMEM((nchunk, tm + CONV_HALO, LANES), F32)],
        compiler_params=_cparams("arbitrary"), name="conv_fwd")(cin, cw8, cb, gain, bias)


def _out_proj(x, attn, c3, wo_a, wo_c, b_out, g_ffn):
    s = x.shape[0]
    tm = _tile(s)

    def body(x_ref, a_ref, c_ref, wa_ref, wc_ref, b_ref, g_ref, x2_ref, h2_ref):
        x2 = x_ref[...] + _dot(a_ref[...], wa_ref[...]) + _dot(c_ref[...], wc_ref[...]) + b_ref[...]
        x2_ref[...] = x2
        r = lax.rsqrt(jnp.mean(x2 * x2, axis=-1, keepdims=True) + EPS)
        h2_ref[...] = (x2 * r * g_ref[...]).astype(BF16)

    return pl.pallas_call(
        body, grid=(s // tm,),
        in_specs=[pl.BlockSpec((tm, D_MODEL), lambda i: (i, 0)), pl.BlockSpec((tm, Q_COLS), lambda i: (i, 0)),
                  pl.BlockSpec((tm, CONV_WIDTH), lambda i: (i, 0)),
                  pl.BlockSpec((Q_COLS, D_MODEL), lambda i: (0, 0), pipeline_mode=pl.Buffered(1)),
                  pl.BlockSpec((CONV_WIDTH, D_MODEL), lambda i: (1, 0), pipeline_mode=pl.Buffered(1)),
                  _resident((1, D_MODEL)), _resident((1, D_MODEL))],
        out_specs=[pl.BlockSpec((tm, D_MODEL), lambda i: (i, 0)), pl.BlockSpec((tm, D_MODEL), lambda i: (i, 0))],
        out_shape=[jax.ShapeDtypeStruct((s, D_MODEL), F32), jax.ShapeDtypeStruct((s, D_MODEL), BF16)],
        compiler_params=_cparams("parallel"), name="out_proj")(x, attn, c3, wo_a, wo_c, b_out, g_ffn)


FF_LANE_CHUNKS = -(-FF_CHUNK // LANES)
FF_PADDED = FF_LANE_CHUNKS * LANES


def _stage_chunks(dst, first_row, value):
    rows = value.shape[0]
    for j in range(FF_LANE_CHUNKS):
        w = min(LANES, FF_CHUNK - j * LANES)
        dst[j, first_row:first_row + rows, 0:w] = value[:, j * LANES:j * LANES + w]


def _tap(ref, first_row, n):
    return ref[pl.ds(first_row, n, stride=1), :]


def _ffn_fwd(h2, x2, target, w_up, fw, fb, w_down):
    s = h2.shape[0]
    tm = _tile(s)
    hal = SUBLANES
    rc = min(128, tm)

    def body(h_ref, x2_ref, t_ref, wu_ref, fw_ref, fb_ref, wd_ref, up0_ref, act_ref, dy_ref, loss_ref,
             ext_ref, carry_ref, act_buf, y_ref):
        i, ci = pl.program_id(0), pl.program_id(1)

        @pl.when((i == 0) & (ci == 0))
        def _():
            carry_ref[...] = jnp.zeros(carry_ref.shape, F32)
            ext_ref[...] = jnp.zeros(ext_ref.shape, F32)
            act_buf[...] = jnp.zeros(act_buf.shape, BF16)
            loss_ref[...] = jnp.zeros((1, 1), F32)

        @pl.when(ci == 0)
        def _():
            y_ref[...] = x2_ref[...]

        h = h_ref[...]
        for half in range(2):
            c = ci + half * N_FF_PAIRS
            u0 = _dot(h, wu_ref[c])
            up0_ref[half, 0] = u0.astype(BF16)
            _stage_chunks(ext_ref.at[half], 0, carry_ref[c])
            _stage_chunks(ext_ref.at[half], hal, u0)
            carry_ref[c] = u0[tm - hal:tm, :]
        wg, wu_, bg, bu = fw_ref[ci], fw_ref[ci + N_FF_PAIRS], fb_ref[ci], fb_ref[ci + N_FF_PAIRS]
        for j in range(FF_LANE_CHUNKS):
            lanes = slice(j * LANES, (j + 1) * LANES)
            eg, eu = ext_ref.at[0, j], ext_ref.at[1, j]

            def rows_fn(r, carry, lanes=lanes, eg=eg, eu=eu):
                base = r * rc
                g = (wg[0:1, lanes] * _tap(eg, base + hal - 2, rc) + wg[1:2, lanes] * _tap(eg, base + hal - 1, rc)
                     + wg[2:3, lanes] * _tap(eg, base + hal, rc) + bg[:, lanes])
                u = (wu_[0:1, lanes] * _tap(eu, base + hal - 2, rc) + wu_[1:2, lanes] * _tap(eu, base + hal - 1, rc)
                     + wu_[2:3, lanes] * _tap(eu, base + hal, rc) + bu[:, lanes])
                act_buf[pl.ds(pl.multiple_of(base, rc), rc), lanes] = (g * _sigmoid(g) * u).astype(BF16)
                return carry

            lax.fori_loop(0, tm // rc, rows_fn, 0)
        act = act_buf[:, 0:FF_CHUNK]
        act_ref[0] = act
        y_ref[...] += _dot(act, wd_ref[ci])

        @pl.when(ci == N_FF_PAIRS - 1)
        def _():
            e = y_ref[...] - t_ref[...]
            dy_ref[...] = e * (1.0 / D_MODEL)
            loss_ref[...] += (0.5 / D_MODEL) * jnp.sum(e * e).reshape(1, 1)

    tok = lambda i, ci: (i, 0)
    return pl.pallas_call(
        body, grid=(s // tm, N_FF_PAIRS),
        in_specs=[pl.BlockSpec((tm, D_MODEL), tok), pl.BlockSpec((tm, D_MODEL), tok), pl.BlockSpec((tm, D_MODEL), tok),
                  _resident((N_DEV, D_MODEL, FF_CHUNK)), _resident((N_DEV, 3, FF_PADDED)), _resident((N_DEV, 1, FF_PADDED)),
                  _resident((N_FF_PAIRS, FF_CHUNK, D_MODEL))],
        out_specs=[pl.BlockSpec((2, 1, tm, FF_CHUNK), lambda i, ci: (0, ci, i, 0)),
                   pl.BlockSpec((1, tm, FF_CHUNK), lambda i, ci: (ci, i, 0)),
                   pl.BlockSpec((tm, D_MODEL), tok), pl.BlockSpec((1, 1), lambda i, ci: (0, 0))],
        out_shape=[jax.ShapeDtypeStruct((2, N_FF_PAIRS, s, FF_CHUNK), BF16),
                   jax.ShapeDtypeStruct((N_FF_PAIRS, s, FF_CHUNK), BF16), jax.ShapeDtypeStruct((s, D_MODEL), F32),
                   jax.ShapeDtypeStruct((1, 1), F32)],
        scratch_shapes=[pltpu.VMEM((2, FF_LANE_CHUNKS, tm + hal, LANES), F32), pltpu.VMEM((N_DEV, hal, FF_CHUNK), F32),
                        pltpu.VMEM((tm, FF_PADDED), BF16), pltpu.VMEM((tm, D_MODEL), F32)],
        compiler_params=_cparams("arbitrary", "arbitrary"), name="ffn_fwd")(h2, x2, target, w_up, fw, fb, w_down)


def _ffn_bwd(dy, up0, w_up, fw, fb, w_down):
    s = dy.shape[0]
    tm = _tile(s)
    nt = s // tm
    hal = 2 * SUBLANES
    nxt = SUBLANES
    rc = min(128, tm)
    nacc = 8

    def body(dy_ref, up0_ref, up0h_ref, wu_ref, fw_ref, fb_ref, wd_ref,
             dup0_ref, dh2_ref, dfw_ref, dfb_ref, ext_ref, dext_ref, carry_ref, dact_buf, dup0_buf):
        i, ci = pl.program_id(0), pl.program_id(1)
        t = nt - 1 - i

        @pl.when((i == 0) & (ci == 0))
        def _():
            for ref in (carry_ref, dfw_ref, dfb_ref, ext_ref, dext_ref, dact_buf):
                ref[...] = jnp.zeros(ref.shape, F32)
            dup0_buf[...] = jnp.zeros(dup0_buf.shape, BF16)

        @pl.when(ci == 0)
        def _():
            dh2_ref[...] = jnp.zeros(dh2_ref.shape, F32)

        dact_buf[:, 0:FF_CHUNK] = _dot_nt(dy_ref[...].astype(BF16), wd_ref[ci])
        for half in range(2):
            c = ci + half * N_FF_PAIRS
            _stage_chunks(ext_ref.at[half], 0, jnp.where(t > 0, up0h_ref[half, 0].astype(F32), 0.0))
            _stage_chunks(ext_ref.at[half], hal, up0_ref[half, 0].astype(F32))
            _stage_chunks(dext_ref.at[half], tm, carry_ref[c])
        ws = (fw_ref[ci], fw_ref[ci + N_FF_PAIRS])
        bs = (fb_ref[ci], fb_ref[ci + N_FF_PAIRS])
        for j in range(FF_LANE_CHUNKS):
            lanes = slice(j * LANES, (j + 1) * LANES)

            def grads(r, acc, lanes=lanes, j=j):
                base = r * rc
                rows = pl.ds(pl.multiple_of(base, rc), rc)
                taps, ups = [], []
                for half in range(2):
                    e = ext_ref.at[half, j]
                    x = [_tap(e, base + hal - 2 + k, rc) for k in range(3)]
                    taps.append(x)
                    w = ws[half]
                    ups.append(w[0:1, lanes] * x[0] + w[1:2, lanes] * x[1] + w[2:3, lanes] * x[2] + bs[half][:, lanes])
                g, u = ups
                sg = _sigmoid(g)
                dact = dact_buf[rows, lanes]
                ds = (dact * u * (sg * (1.0 + g * (1.0 - sg))), dact * (g * sg))
                out = []
                for half in range(2):
                    d = ds[half]
                    dext_ref[half, j, rows, :] = d
                    fold = lambda v: jnp.sum(v.reshape(rc // SUBLANES, SUBLANES, LANES), axis=0)
                    out.append(acc[4 * half] + fold(d))
                    for k in range(3):
                        out.append(acc[4 * half + 1 + k] + fold(d * taps[half][k]))
                return tuple(out)

            zero = jnp.zeros((SUBLANES, LANES), F32)
            acc = lax.fori_loop(0, tm // rc, grads, (zero,) * nacc)
            for half in range(2):
                c = ci + half * N_FF_PAIRS
                dfb_ref[c, :, lanes] += jnp.sum(acc[4 * half], axis=0, keepdims=True)
                dfw_ref[c, :, lanes] += jnp.concatenate(
                    [jnp.sum(acc[4 * half + 1 + k], axis=0, keepdims=True) for k in range(3)], axis=0)

            def conv_bwd(r, carry, lanes=lanes, j=j):
                base = r * rc
                rows = pl.ds(pl.multiple_of(base, rc), rc)
                for half in range(2):
                    d = dext_ref.at[half, j]
                    w = ws[half]
                    dup0 = w[2:3, lanes] * _tap(d, base, rc) + w[1:2, lanes] * _tap(d, base + 1, rc) + w[0:1, lanes] * _tap(d, base + 2, rc)
                    dup0_buf[half, rows, lanes] = dup0.astype(BF16)
                return carry

            lax.fori_loop(0, tm // rc, conv_bwd, 0)
        for half in range(2):
            c = ci + half * N_FF_PAIRS
            for j in range(FF_LANE_CHUNKS):
                w = min(LANES, FF_CHUNK - j * LANES)
                carry_ref[c, :, j * LANES:j * LANES + w] = dext_ref[half, j, 0:nxt, 0:w]
            dup0 = dup0_buf[half, :, 0:FF_CHUNK]
            dup0_ref[half, 0] = dup0
            dh2_ref[...] += _dot_nt(dup0, wu_ref[c])

    tok = lambda i, ci: (nt - 1 - i, 0)
    acc = lambda shape: pl.BlockSpec(shape, lambda i, ci: (0,) * len(shape))
    return pl.pallas_call(
        body, grid=(nt, N_FF_PAIRS),
        in_specs=[pl.BlockSpec((tm, D_MODEL), tok),
                  pl.BlockSpec((2, 1, tm, FF_CHUNK), lambda i, ci: (0, ci, nt - 1 - i, 0)),
                  pl.BlockSpec((2, 1, hal, FF_CHUNK), lambda i, ci: (0, ci, jnp.maximum((nt - 1 - i) * (tm // hal) - 1, 0), 0)),
                  _resident((N_DEV, D_MODEL, FF_CHUNK)), _resident((N_DEV, 3, FF_PADDED)), _resident((N_DEV, 1, FF_PADDED)),
                  _resident((N_FF_PAIRS, FF_CHUNK, D_MODEL))],
        out_specs=[pl.BlockSpec((2, 1, tm, FF_CHUNK), lambda i, ci: (0, ci, nt - 1 - i, 0)),
                   pl.BlockSpec((tm, D_MODEL), tok), acc((N_DEV, 3, FF_PADDED)), acc((N_DEV, 1, FF_PADDED))],
        out_shape=[jax.ShapeDtypeStruct((2, N_FF_PAIRS, s, FF_CHUNK), BF16), jax.ShapeDtypeStruct((s, D_MODEL), F32),
                   jax.ShapeDtypeStruct((N_DEV, 3, FF_PADDED), F32), jax.ShapeDtypeStruct((N_DEV, 1, FF_PADDED), F32)],
        scratch_shapes=[pltpu.VMEM((2, FF_LANE_CHUNKS, tm + hal, LANES), F32), pltpu.VMEM((2, FF_LANE_CHUNKS, tm + nxt, LANES), F32),
                        pltpu.VMEM((N_DEV, nxt, FF_CHUNK), F32), pltpu.VMEM((tm, FF_PADDED), F32),
                        pltpu.VMEM((2, tm, FF_PADDED), BF16)],
        compiler_params=_cparams("arbitrary", "arbitrary"), name="ffn_bwd")(dy, up0, up0, w_up, fw, fb, w_down)


def _ffn_norm_bwd(dh2, dy, x2, g_ffn, w_out):
    s = dy.shape[0]
    tm = _tile(s)

    def body(dh_ref, dy_ref, x2_ref, g_ref, wo_ref, dx2_ref, dmix_ref, dg_ref, dbo_ref):
        @pl.when(pl.program_id(0) == 0)
        def _():
            dg_ref[...] = jnp.zeros(dg_ref.shape, F32)
            dbo_ref[...] = jnp.zeros(dbo_ref.shape, F32)

        x2v = x2_ref[...]
        r = lax.rsqrt(jnp.mean(x2v * x2v, axis=-1, keepdims=True) + EPS)
        n2 = x2v * r
        dh2 = dh_ref[...]
        dg_ref[...] += jnp.sum(dh2 * n2, axis=0, keepdims=True)
        dn = dh2 * g_ref[...]
        dx2 = dy_ref[...] + r * (dn - n2 * jnp.mean(dn * n2, axis=-1, keepdims=True))
        dx2_ref[...] = dx2
        dbo_ref[...] += jnp.sum(dx2, axis=0, keepdims=True)
        dmix_ref[...] = _dot_nt(dx2.astype(BF16), wo_ref[...])

    tok = pl.BlockSpec((tm, D_MODEL), lambda i: (i, 0))
    vec = pl.BlockSpec((1, D_MODEL), lambda i: (0, 0))
    return pl.pallas_call(
        body, grid=(s // tm,),
        in_specs=[tok, tok, tok, _resident((1, D_MODEL)), _resident((D_MODEL, D_MODEL))],
        out_specs=[tok, tok, vec, vec],
        out_shape=[jax.ShapeDtypeStruct((s, D_MODEL), F32), jax.ShapeDtypeStruct((s, D_MODEL), F32),
                   jax.ShapeDtypeStruct((1, D_MODEL), F32), jax.ShapeDtypeStruct((1, D_MODEL), F32)],
        compiler_params=_cparams("arbitrary"), name="ffn_norm_bwd")(dh2, dy, x2, g_ffn, w_out)


def _conv_bwd(dmixed, c1, cin, cw8, gain, bias):
    s = cin.shape[0]
    tm = _tile(s)
    nt = s // tm
    rc = 64
    rn = min(256, tm)
    hal = CONV_HALO
    lead = hal - (CONV_KERNEL - 1)
    nchunk = CONV_WIDTH // LANES

    def body(dc3_ref, dc3n_ref, c1_ref, c1n_ref, cin_ref, cinp_ref, cw_ref, gain_ref, bias_ref,
             dcin_ref, dcw_ref, dcb_ref, dgain_ref, dbias_ref, dbcin_ref, c0_ext, dc1_ext, dcw8):
        i = pl.program_id(0)
        first, last = i == 0, i == nt - 1

        @pl.when(first)
        def _():
            for ref in (dcw8, dcb_ref, dgain_ref, dbias_ref, dbcin_ref):
                ref[...] = jnp.zeros(ref.shape, F32)

        lo = _lo_mask((1, LANES))

        def norm_bwd(dc3, c1v, cols):
            nrm, rstd = _group_stats(c1v, lo)
            c2 = nrm * gain_ref[:, cols] + bias_ref[:, cols]
            sg = _sigmoid(c2)
            dc2 = dc3 * (sg * (1.0 + c2 * (1.0 - sg)))
            dn = dc2 * gain_ref[:, cols]
            inv = 1.0 / HEAD_DIM
            dc1 = rstd * (dn - _half_sums(dn, lo) * inv - nrm * (_half_sums(dn * nrm, lo) * inv))
            return dc1, dc2, nrm

        def row_sum(v):
            return jnp.sum(v, axis=0, keepdims=True)

        for cc in range(nchunk):
            cols = slice(cc * LANES, (cc + 1) * LANES)
            gcols = slice(CONV_WIDTH + cc * LANES, CONV_WIDTH + (cc + 1) * LANES)
            c0e, d1e = c0_ext.at[cc], dc1_ext.at[cc]
            c0e[0:hal, :] = jnp.where(first, 0.0, cinp_ref[:, cols] * _sigmoid(cinp_ref[:, gcols]))
            dc1n, _, _ = norm_bwd(dc3n_ref[:, cols], c1n_ref[cc], cols)
            d1e[tm:tm + hal, :] = jnp.where(last, 0.0, dc1n)

            for r in range(tm // rn):
                rows = slice(r * rn, (r + 1) * rn)
                c0e[hal + r * rn:hal + (r + 1) * rn, :] = cin_ref[rows, cols] * _sigmoid(cin_ref[rows, gcols])
                dc1, dc2, nrm = norm_bwd(dc3_ref[rows, cols], c1_ref[cc, rows, :], cols)
                d1e[rows, :] = dc1
                dgain_ref[:, cols] += row_sum(dc2 * nrm)
                dbias_ref[:, cols] += row_sum(dc2)
                dcb_ref[:, cols] += row_sum(dc1)
            zero = jnp.zeros((1, LANES), F32)

            for k0 in range(0, CONV_KERNEL, SUBLANES):
                taps = range(k0, min(k0 + SUBLANES, CONV_KERNEL))

                def tap_sums(r, acc, taps=taps):
                    d = _rows(d1e, r * rc, rc)
                    return tuple(a + jnp.sum(d * _rows(c0e, r * rc + lead + k, rc), axis=0) for a, k in zip(acc, taps))

                acc = lax.fori_loop(0, tm // rc, tap_sums, tuple(dcw8[k * SUBLANES:(k + 1) * SUBLANES, cols] for k in taps))
                for a, k in zip(acc, taps):
                    dcw8[k * SUBLANES:(k + 1) * SUBLANES, cols] = a

            def input_grad(r, sums):
                rows = pl.ds(pl.multiple_of(r * rc, rc), rc)
                dc0 = jnp.zeros((rc // SUBLANES, SUBLANES, LANES), F32)
                for k in range(CONV_KERNEL):
                    dc0 = dc0 + cw_ref[k * SUBLANES:(k + 1) * SUBLANES, cols][None] * _rows(d1e, r * rc + CONV_KERNEL - 1 - k, rc)
                dc0 = dc0.reshape(rc, LANES)
                sg = _sigmoid(cin_ref[rows, gcols])
                da = dc0 * sg
                dgate = dc0 * cin_ref[rows, cols] * sg * (1.0 - sg)
                dcin_ref[rows, cols] = da.astype(BF16)
                dcin_ref[rows, gcols] = dgate.astype(BF16)
                return sums[0] + row_sum(da), sums[1] + row_sum(dgate)

            sums = lax.fori_loop(0, tm // rc, input_grad, (zero, zero))
            dbcin_ref[:, cols] += sums[0]
            dbcin_ref[:, gcols] += sums[1]

        @pl.when(last)
        def _():
            for k in range(CONV_KERNEL):
                dcw_ref[k:k + 1, :] = jnp.sum(dcw8[k * SUBLANES:(k + 1) * SUBLANES, :], axis=0, keepdims=True)

    nh = tm // hal
    acc = lambda shape: pl.BlockSpec(shape, lambda i: (0,) * len(shape))
    return pl.pallas_call(
        body, grid=(nt,),
        in_specs=[pl.BlockSpec((tm, CONV_WIDTH), lambda i: (i, 1)),
                  pl.BlockSpec((hal, CONV_WIDTH), lambda i: (jnp.minimum((i + 1) * nh, s // hal - 1), 1)),
                  pl.BlockSpec((nchunk, tm, LANES), lambda i: (0, i, 0)),
                  pl.BlockSpec((nchunk, hal, LANES), lambda i: (0, jnp.minimum((i + 1) * nh, s // hal - 1), 0)),
                  pl.BlockSpec((tm, CIN_COLS), lambda i: (i, 0)),
                  pl.BlockSpec((hal, CIN_COLS), lambda i: (jnp.maximum(i * nh - 1, 0), 0)),
                  _resident((CONV_KERNEL * SUBLANES, CONV_WIDTH)), _resident((1, CONV_WIDTH)), _resident((1, CONV_WIDTH))],
        out_specs=[pl.BlockSpec((tm, CIN_COLS), lambda i: (i, 0)), acc((CONV_KERNEL, CONV_WIDTH)), acc((1, CONV_WIDTH)),
                   acc((1, CONV_WIDTH)), acc((1, CONV_WIDTH)), acc((1, CIN_COLS))],
        out_shape=[jax.ShapeDtypeStruct((s, CIN_COLS), BF16), jax.ShapeDtypeStruct((CONV_KERNEL, CONV_WIDTH), F32),
                   jax.ShapeDtypeStruct((1, CONV_WIDTH), F32), jax.ShapeDtypeStruct((1, CONV_WIDTH), F32),
                   jax.ShapeDtypeStruct((1, CONV_WIDTH), F32), jax.ShapeDtypeStruct((1, CIN_COLS), F32)],
        scratch_shapes=[pltpu.VMEM((nchunk, tm + hal, LANES), F32), pltpu.VMEM((nchunk, tm + hal, LANES), F32),
                        pltpu.VMEM((CONV_KERNEL * SUBLANES, CONV_WIDTH), F32)],
        compiler_params=_cparams("arbitrary"), name="conv_bwd")(dmixed, dmixed, c1, c1, cin, cin, cw8, gain, bias)


def _attn_bwd(qkv, dmixed, gq2, gk2, sinks):
    s = qkv.shape[0]
    tq = _tile(s)
    nb = tq // ATT_BLOCK
    nt = s // tq

    def body(q_ref, kv_ref, kvp_ref, do_ref, gq_ref, gk_ref, sink_ref,
             dqkv_ref, dgq_ref, dgk_ref, dsink_ref, dbqkv_ref, dk_acc, dv_acc, carry_k, carry_v):
        i = pl.program_id(0)
        t = nt - 1 - i

        @pl.when(i == 0)
        def _():
            for ref in (carry_k, carry_v, dgq_ref, dgk_ref, dsink_ref, dbqkv_ref):
                ref[...] = jnp.zeros(ref.shape, F32)

        lo = _lo_mask((1, LANES))
        lane_id = lax.broadcasted_iota(jnp.int32, (1, LANES), 1)
        kv_all = jnp.concatenate([kvp_ref[...], kv_ref[...]], axis=0)
        k_lo, k_hi, v_lo, v_hi, kn_pre, rk = _kv_variants(kv_all, gk_ref[...], lo)
        for acc_ref, carry in ((dk_acc, carry_k), (dv_acc, carry_v)):
            acc_ref[:, 0:tq, :] = jnp.zeros((N_KV_HEADS, tq, LANES), F32)
            acc_ref[:, tq:tq + ATT_BLOCK, :] = carry[...]
        dsink = jnp.zeros((1, LANES), F32)
        dgq = jnp.zeros((1, LANES), F32)
        gq = gq_ref[...]
        for b in range(nb):
            rel, valid = _att_consts(t == 0, b)
            rows = slice(b * ATT_BLOCK, (b + 1) * ATT_BLOCK)
            keys = slice(b * ATT_BLOCK, (b + 2) * ATT_BLOCK)
            for kvh in range(N_KV_HEADS):
                pairs = (2 * kvh, 2 * kvh + 1)
                q_raw = jnp.concatenate([q_ref[rows, p * LANES:(p + 1) * LANES] for p in pairs], axis=0)
                qn_pre, rq = _head_norm(q_raw, lo)
                q2 = (qn_pre * gq).astype(BF16)
                do2 = jnp.concatenate([do_ref[rows, p * LANES:(p + 1) * LANES] for p in pairs], axis=0).astype(BF16)
                dq2 = jnp.zeros((2 * ATT_BLOCK, LANES), F32)
                for odd, (k_op, v_op) in enumerate(((k_lo[kvh][keys], v_lo[kvh][keys]), (k_hi[kvh][keys], v_hi[kvh][keys]))):
                    ha, hb = 2 * pairs[0] + odd, 2 * pairs[1] + odd
                    p, p_sink = _probs(q2, k_op, rel, valid, _row_const(SLOPES[ha], SLOPES[hb]),
                                       _row_const(sink_ref[ha], sink_ref[hb]))
                    dp = _dot_nt(do2, v_op)
                    delta = jnp.sum(p * dp, axis=-1, keepdims=True)
                    ds = (p * (dp - delta) * (1.0 / math.sqrt(HEAD_DIM))).astype(BF16)
                    dsk = p_sink * delta
                    dsink = dsink - jnp.where(lane_id == ha, jnp.sum(dsk[0:ATT_BLOCK]), 0.0) \
                        - jnp.where(lane_id == hb, jnp.sum(dsk[ATT_BLOCK:]), 0.0)
                    dq2 = dq2 + _dot(ds, k_op)
                    half = lo if odd == 0 else jnp.logical_not(lo)
                    dk_acc[kvh, keys, :] += jnp.where(half, _dot_tn(ds, q2), 0.0)
                    dv_acc[kvh, keys, :] += jnp.where(half, _dot_tn(p.astype(BF16), do2), 0.0)
                dgq = dgq + jnp.sum(dq2 * qn_pre, axis=0, keepdims=True)
                dq_raw = _head_norm_bwd(dq2 * gq, qn_pre, rq, lo)
                for n, p_ in enumerate(pairs):
                    blk = dq_raw[n * ATT_BLOCK:(n + 1) * ATT_BLOCK]
                    dqkv_ref[rows, p_ * LANES:(p_ + 1) * LANES] = blk.astype(BF16)
                    dbqkv_ref[:, p_ * LANES:(p_ + 1) * LANES] += jnp.sum(blk, axis=0, keepdims=True)
        carry_k[...] = dk_acc[:, 0:ATT_BLOCK, :]
        carry_v[...] = dv_acc[:, 0:ATT_BLOCK, :]

        def fold(acc_ref):
            both = []
            for kvh in range(N_KV_HEADS):
                a = acc_ref[kvh, ATT_BLOCK:ATT_BLOCK + tq, :]
                both.append(a + pltpu.roll(a, HEAD_DIM, 1))
            return jnp.where(lo, both[0], both[1])

        dkn = fold(dk_acc)
        dv = fold(dv_acc)
        kn_c, rk_c = kn_pre[ATT_BLOCK:], rk[ATT_BLOCK:]
        dgk_ref[...] += jnp.sum(dkn * kn_c, axis=0, keepdims=True)
        dk_raw = _head_norm_bwd(dkn * gk_ref[...], kn_c, rk_c, lo)
        dqkv_ref[:, Q_COLS:Q_COLS + KV_COLS] = dk_raw.astype(BF16)
        dqkv_ref[:, Q_COLS + KV_COLS:] = dv.astype(BF16)
        dbqkv_ref[:, Q_COLS:Q_COLS + KV_COLS] += jnp.sum(dk_raw, axis=0, keepdims=True)
        dbqkv_ref[:, Q_COLS + KV_COLS:] += jnp.sum(dv, axis=0, keepdims=True)
        dgq_ref[...] += dgq
        dsink_ref[...] += dsink

        @pl.when(i == nt - 1)
        def _():
            for ref in (dgq_ref, dgk_ref):
                v = ref[...]
                ref[...] = v + pltpu.roll(v, HEAD_DIM, 1)

    acc = lambda shape: pl.BlockSpec(shape, lambda i: (0,) * len(shape))
    return pl.pallas_call(
        body, grid=(nt,),
        in_specs=[pl.BlockSpec((tq, Q_COLS), lambda i: (nt - 1 - i, 0)),
                  pl.BlockSpec((tq, 2 * KV_COLS), lambda i: (nt - 1 - i, 2)),
                  pl.BlockSpec((ATT_BLOCK, 2 * KV_COLS), lambda i: (jnp.maximum((nt - 1 - i) * nb - 1, 0), 2)),
                  pl.BlockSpec((tq, Q_COLS), lambda i: (nt - 1 - i, 0)),
                  _resident((1, LANES)), _resident((1, LANES)), pl.BlockSpec(memory_space=pltpu.SMEM)],
        out_specs=[pl.BlockSpec((tq, QKV_COLS), lambda i: (nt - 1 - i, 0)), acc((1, LANES)), acc((1, LANES)),
                   acc((1, LANES)), acc((1, QKV_COLS))],
        out_shape=[jax.ShapeDtypeStruct((s, QKV_COLS), BF16), jax.ShapeDtypeStruct((1, LANES), F32),
                   jax.ShapeDtypeStruct((1, LANES), F32), jax.ShapeDtypeStruct((1, LANES), F32),
                   jax.ShapeDtypeStruct((1, QKV_COLS), F32)],
        scratch_shapes=[pltpu.VMEM((N_KV_HEADS, tq + ATT_BLOCK, LANES), F32), pltpu.VMEM((N_KV_HEADS, tq + ATT_BLOCK, LANES), F32),
                        pltpu.VMEM((N_KV_HEADS, ATT_BLOCK, LANES), F32), pltpu.VMEM((N_KV_HEADS, ATT_BLOCK, LANES), F32)],
        compiler_params=_cparams("arbitrary"), name="attn_bwd")(qkv, qkv, qkv, dmixed, gq2, gk2, sinks)


def _in_bwd(dqkv, dcin, w_qkv, w_cin, x, dx2, g_mix):
    s = x.shape[0]
    tm = _tile(s)

    def body(dq_ref, dc_ref, wq_ref, wc_ref, x_ref, dx2_ref, g_ref, gx_ref, dg_ref):
        @pl.when(pl.program_id(0) == 0)
        def _():
            dg_ref[...] = jnp.zeros(dg_ref.shape, F32)

        dh = _dot_nt(dq_ref[...], wq_ref[...]) + _dot_nt(dc_ref[...], wc_ref[...])
        xv = x_ref[...]
        r = lax.rsqrt(jnp.mean(xv * xv, axis=-1, keepdims=True) + EPS)
        n = xv * r
        dg_ref[...] += jnp.sum(dh * n, axis=0, keepdims=True)
        dn = dh * g_ref[...]
        gx_ref[...] = dx2_ref[...] + r * (dn - n * jnp.mean(dn * n, axis=-1, keepdims=True))

    return pl.pallas_call(
        body, grid=(s // tm,),
        in_specs=[pl.BlockSpec((tm, QKV_COLS), lambda i: (i, 0)), pl.BlockSpec((tm, CIN_COLS), lambda i: (i, 0)),
                  _resident((D_MODEL, QKV_COLS)), _resident((D_MODEL, CIN_COLS)),
                  pl.BlockSpec((tm, D_MODEL), lambda i: (i, 0)), pl.BlockSpec((tm, D_MODEL), lambda i: (i, 0)),
                  _resident((1, D_MODEL))],
        out_specs=[pl.BlockSpec((tm, D_MODEL), lambda i: (i, 0)), pl.BlockSpec((1, D_MODEL), lambda i: (0, 0))],
        out_shape=[jax.ShapeDtypeStruct((s, D_MODEL), F32), jax.ShapeDtypeStruct((1, D_MODEL), F32)],
        compiler_params=_cparams("arbitrary"), name="in_bwd")(dqkv, dcin, w_qkv, w_cin, x, dx2, g_mix)


def _tn_matmul(a, b, name):
    ga, s, m = a.shape
    gb, _, n = b.shape
    g = max(ga, gb)
    tk = _tile(s)

    def body(a_ref, b_ref, o_ref):
        @pl.when(pl.program_id(1) == 0)
        def _():
            o_ref[...] = jnp.zeros(o_ref.shape, F32)

        o_ref[0] += _dot_tn(a_ref[0].astype(BF16), b_ref[0].astype(BF16))

    return pl.pallas_call(
        body, grid=(g, s // tk),
        in_specs=[pl.BlockSpec((1, tk, m), (lambda gi, k: (gi, k, 0)) if ga > 1 else (lambda gi, k: (0, k, 0))),
                  pl.BlockSpec((1, tk, n), (lambda gi, k: (gi, k, 0)) if gb > 1 else (lambda gi, k: (0, k, 0)))],
        out_specs=pl.BlockSpec((1, m, n), lambda gi, k: (gi, 0, 0)),
        out_shape=jax.ShapeDtypeStruct((g, m, n), F32),
        compiler_params=_cparams("parallel", "arbitrary"), name=name)(a, b)


def _position():
    return lax.axis_index("x"), lax.axis_index("y"), lax.axis_index("c")


def _dev_index(px, py, pc):
    return 4 * px + 2 * py + pc


def _flip(v, bit):
    return 1 - v if bit else v


OTHER_CHIPS = ((1, 0), (0, 1), (1, 1))


def _allgather(shards, dtypes):
    n = len(shards)
    n_copies = 1 + 2 * len(OTHER_CHIPS)

    def body(*refs):
        ins, outs = refs[:n], refs[n:2 * n]
        send_sems, recv_sems = refs[2 * n:]
        x, y, c = _position()
        me, sibling = (x, y, c), (x, y, 1 - c)
        chips = [(_flip(x, fx), _flip(y, fy)) for fx, fy in OTHER_CHIPS]
        for a in range(n):
            outs[a][_dev_index(*me)] = ins[a][...].astype(dtypes[a])

        def copy(a, k, block, to):
            rows = outs[a].at[_dev_index(*block)]
            return pltpu.make_async_remote_copy(src_ref=rows, dst_ref=rows, send_sem=send_sems.at[a, k],
                                                recv_sem=recv_sems.at[a, k], device_id=to, device_id_type=MESH)

        started = []
        for a in range(n):
            for j, chip in enumerate(chips):
                started.append(copy(a, 1 + j, me, (*chip, c)))
            started.append(copy(a, 0, me, sibling))
        for cp in started:
            cp.start()
        for a in range(n):
            for j, chip in enumerate(chips):
                copy(a, 1 + j, (*chip, c), me).wait_recv()
                fwd = copy(a, 1 + len(chips) + j, (*chip, c), sibling)
                fwd.start()
                started.append(fwd)
        for a in range(n):
            copy(a, 0, sibling, me).wait_recv()
            for j, chip in enumerate(chips):
                copy(a, 1 + len(chips) + j, (*chip, 1 - c), me).wait_recv()
        for cp in started:
            cp.wait_send()

    vmem = pl.BlockSpec(memory_space=pltpu.VMEM)
    return pl.pallas_call(
        body, in_specs=[vmem] * n, out_specs=[vmem] * n,
        out_shape=[jax.ShapeDtypeStruct((N_DEV,) + w.shape, dt) for w, dt in zip(shards, dtypes)],
        scratch_shapes=[pltpu.SemaphoreType.DMA((n, n_copies)), pltpu.SemaphoreType.DMA((n, n_copies))],
        compiler_params=pltpu.CompilerParams(vmem_limit_bytes=VMEM_LIMIT), name="allgather_weights")(*shards)


def _reduce_scatter(g8, name):
    _, r, c_ = g8.shape
    rel = ((0, 0),) + OTHER_CHIPS

    def body(g_ref, out_ref, stage, send_a, recv_a, send_b, recv_b, sa_send, sa_recv, sb_send, sb_recv):
        x, y, c = _position()
        sibling = (x, y, 1 - c)
        chips = [(_flip(x, fx), _flip(y, fy)) for fx, fy in rel]

        def copy_a(j):
            return pltpu.make_async_remote_copy(src_ref=send_a.at[j], dst_ref=recv_a.at[j], send_sem=sa_send.at[j],
                                                recv_sem=sa_recv.at[j], device_id=sibling, device_id_type=MESH)

        def copy_b(j):
            return pltpu.make_async_remote_copy(src_ref=send_b.at[j], dst_ref=recv_b.at[j], send_sem=sb_send.at[j],
                                                recv_sem=sb_recv.at[j], device_id=(*chips[1 + j], c), device_id_type=MESH)

        for j, chip in enumerate(chips):
            pltpu.sync_copy(g_ref.at[_dev_index(*chip, 1 - c)], stage)
            send_a[j] = stage[...].astype(BF16)
            copy_a(j).start()
        for j, chip in enumerate(chips):
            pltpu.sync_copy(g_ref.at[_dev_index(*chip, c)], stage)
            copy_a(j).wait_recv()
            part = stage[...] + recv_a[j].astype(F32)
            if j == 0:
                out_ref[...] = part
            else:
                send_b[j - 1] = part.astype(BF16)
                copy_b(j - 1).start()
        for j in range(len(OTHER_CHIPS)):
            copy_b(j).wait_recv()
            out_ref[...] += recv_b[j].astype(F32)
        for j in range(len(rel)):
            copy_a(j).wait_send()
        for j in range(len(OTHER_CHIPS)):
            copy_b(j).wait_send()

    na, nb = len(rel), len(OTHER_CHIPS)
    return pl.pallas_call(
        body, in_specs=[pl.BlockSpec(memory_space=pl.ANY)], out_specs=pl.BlockSpec(memory_space=pltpu.VMEM),
        out_shape=jax.ShapeDtypeStruct((r, c_), F32),
        scratch_shapes=[pltpu.VMEM((r, c_), F32), pltpu.VMEM((na, r, c_), BF16), pltpu.VMEM((na, r, c_), BF16),
                        pltpu.VMEM((nb, r, c_), BF16), pltpu.VMEM((nb, r, c_), BF16),
                        pltpu.SemaphoreType.DMA((na,)), pltpu.SemaphoreType.DMA((na,)),
                        pltpu.SemaphoreType.DMA((nb,)), pltpu.SemaphoreType.DMA((nb,))],
        compiler_params=pltpu.CompilerParams(vmem_limit_bytes=VMEM_LIMIT), name=name)(g8)


def _small_exchange(v, reduce, name):
    rows = v.shape[0]

    def body(v_ref, out_ref, *scratch):
        gath = scratch[0] if reduce else out_ref
        send_sems, recv_sems = scratch[-2:]
        x, y, c = _position()
        me = _dev_index(x, y, c)
        gath[me] = v_ref[...]
        peers = [(_flip(x, k >> 2 & 1), _flip(y, k >> 1 & 1), _flip(c, k & 1)) for k in range(1, N_DEV)]

        def copy(k, block):
            return pltpu.make_async_remote_copy(src_ref=gath.at[block], dst_ref=gath.at[block], send_sem=send_sems.at[k],
                                                recv_sem=recv_sems.at[k], device_id=peers[k], device_id_type=MESH)

        for k in range(N_DEV - 1):
            copy(k, me).start()
        for k in range(N_DEV - 1):
            copy(k, _dev_index(*peers[k])).wait_recv()
        for k in range(N_DEV - 1):
            copy(k, me).wait_send()
        if reduce:
            total = gath[0]
            for d in range(1, N_DEV):
                total = total + gath[d]
            out_ref[...] = total

    vmem = pl.BlockSpec(memory_space=pltpu.VMEM)
    out_shape = (rows, LANES) if reduce else (N_DEV, rows, LANES)
    return pl.pallas_call(
        body, in_specs=[vmem], out_specs=vmem, out_shape=jax.ShapeDtypeStruct(out_shape, F32),
        scratch_shapes=([pltpu.VMEM((N_DEV, rows, LANES), F32)] if reduce else [])
        + [pltpu.SemaphoreType.DMA((N_DEV - 1,)), pltpu.SemaphoreType.DMA((N_DEV - 1,))],
        compiler_params=pltpu.CompilerParams(vmem_limit_bytes=VMEM_LIMIT), name=name)(v)


def _row_tile(r):
    for n in (8, 4, 2):
        if r % (n * SUBLANES) == 0:
            return r // n
    return r


def _adam_math(wv, gv, mv, vv):
    mn = ADAM_B1 * mv + (1.0 - ADAM_B1) * gv
    vn = ADAM_B2 * vv + (1.0 - ADAM_B2) * (gv * gv)
    m_hat = mn / (1.0 - ADAM_B1 ** ADAM_STEP)
    v_hat = vn / (1.0 - ADAM_B2 ** ADAM_STEP)
    return -ADAM_LR * (m_hat / (jnp.sqrt(v_hat) + ADAM_EPS) + ADAM_WD * wv), mn, vn


def _adamw(w, g, m, v, name):
    r, c_ = w.shape
    tr = _row_tile(r)

    def body(w_ref, g_ref, m_ref, v_ref, d_ref, mo_ref, vo_ref):
        d_ref[...], mo_ref[...], vo_ref[...] = _adam_math(w_ref[...], g_ref[...], m_ref[...], v_ref[...])

    spec = pl.BlockSpec((tr, c_), lambda i: (i, 0))
    return pl.pallas_call(
        body, grid=(r // tr,), in_specs=[spec] * 4, out_specs=[spec] * 3,
        out_shape=[jax.ShapeDtypeStruct((r, c_), F32)] * 3,
        compiler_params=_cparams("parallel"), name=name)(w, g, m, v)


FW_ROWS = 24
CW_ROWS = 32
R_FW = 0
R_FB = R_FW + N_DEV * FW_ROWS
R_CW = R_FB + 48
R_BQKV = R_CW + (CONV_WIDTH // LANES) * CW_ROWS
R_BCIN = R_BQKV + 8
R_GMIX = R_BCIN + 8
R_BOUT = R_GMIX + 8
R_GFFN = R_BOUT + 8
R_CB = R_GFFN + 8
R_CGAIN = R_CB + 8
R_CBIAS = R_CGAIN + 8
R_QKS = R_CBIAS + 8
SMALL_ROWS = R_QKS + 8


def _pack_small(raw):
    def rows(a, n):
        a = a.reshape(-1, LANES)
        return jnp.pad(a, ((0, n - a.shape[0]), (0, 0)))

    fw = jnp.pad(raw["dfw"].reshape(N_DEV, -1, LANES), ((0, 0), (0, FW_ROWS - 3 * FF_LANE_CHUNKS), (0, 0)))
    cw = jnp.pad(raw["dcw"].reshape(CONV_KERNEL, -1, LANES).transpose(1, 0, 2), ((0, 0), (0, CW_ROWS - CONV_KERNEL), (0, 0)))
    qks = jnp.concatenate([raw["dgq"], raw["dgk"], raw["dsink"]], axis=0)
    return jnp.concatenate([
        fw.reshape(-1, LANES), rows(raw["dfb"][:, 0, :FF_CHUNK], 48), cw.reshape(-1, LANES), rows(raw["dbqkv"], 8),
        rows(raw["dbcin"], 8), rows(raw["dg_mix"], 8), rows(raw["db_out"], 8), rows(raw["dg_ffn"], 8), rows(raw["dcb"], 8),
        rows(raw["dcgain"], 8), rows(raw["dcbias"], 8), rows(qks, 8)], axis=0)


def _adamw_small(gpack, w, m, v):
    n = len(SMALL)
    ix = {name: i for i, name in enumerate(SMALL)}

    def body(g_ref, *refs):
        w_refs, m_refs, v_refs, outs = refs[:n], refs[n:2 * n], refs[2 * n:3 * n], refs[3 * n:]
        d = _dev_index(*_position())

        def step(name, idx, gv):
            i = ix[name]
            delta, mn, vn = _adam_math(w_refs[i][idx], gv, m_refs[i][idx], v_refs[i][idx])
            for ref, val in zip(outs[4 * i:4 * i + 4], (gv, delta, mn, vn)):
                ref[idx] = val

        def whole(name, row, nrows):
            step(name, (slice(None), slice(None)), g_ref[row:row + nrows, :])

        whole("mix_norm_gain", R_GMIX, 8)
        whole("b_out", R_BOUT, 8)
        whole("ffn_norm_gain", R_GFFN, 8)
        whole("conv_dw_b", R_CB, 4)
        whole("conv_norm_gain", R_CGAIN, 4)
        whole("conv_norm_bias", R_CBIAS, 4)
        whole("ffn_dw_b", R_FB, 2 * D_FF // LANES)
        nq = QKV_COLS // LANES
        step("b_in", (slice(0, nq), slice(None)), g_ref[R_BQKV:R_BQKV + nq, :])
        step("b_in", (slice(nq, nq + CIN_COLS // LANES), slice(None)), g_ref[R_BCIN:R_BCIN + CIN_COLS // LANES, :])
        step("q_norm_gain", (slice(None), slice(None)), g_ref[R_QKS:R_QKS + 1, 0:HEAD_DIM])
        step("k_norm_gain", (slice(None), slice(None)), g_ref[R_QKS + 1:R_QKS + 2, 0:HEAD_DIM])
        step("attn_sinks", (slice(None), slice(None)), g_ref[R_QKS + 2:R_QKS + 3, 0:N_Q_HEADS])
        blk = g_ref[pl.ds(pl.multiple_of(R_CW + CW_ROWS * lax.shift_right_logical(d, 1), SUBLANES), CW_ROWS), :]
        blk = jnp.where((d & 1) == 1, pltpu.roll(blk, HEAD_DIM, 1), blk)
        step("conv_dw_w", (slice(None), slice(None)), blk[0:CONV_KERNEL, 0:CONV_WIDTH // N_DEV])
        blk = g_ref[pl.ds(pl.multiple_of(R_FW + FW_ROWS * d, SUBLANES), FW_ROWS), :]
        for k in range(3):
            for j in range(FF_LANE_CHUNKS):
                wd = min(LANES, FF_CHUNK - j * LANES)
                row = k * FF_LANE_CHUNKS + j
                step("ffn_dw_w", (slice(k, k + 1), slice(j * LANES, j * LANES + wd)), blk[row:row + 1, 0:wd])

    vmem = pl.BlockSpec(memory_space=pltpu.VMEM)
    args = [gpack] + [d[name] for d in (w, m, v) for name in SMALL]
    outs = pl.pallas_call(
        body, in_specs=[vmem] * len(args), out_specs=[vmem] * (4 * n),
        out_shape=[jax.ShapeDtypeStruct(w[name].shape, F32) for name in SMALL for _ in range(4)],
        compiler_params=pltpu.CompilerParams(vmem_limit_bytes=VMEM_LIMIT), name="adamw_small")(*args)
    return {name: outs[4 * i:4 * i + 4] for i, name in enumerate(SMALL)}


def _local_step(x, target, p):
    s = x.shape[0]
    qkv, cin, h1 = _mix_proj(x, p["g_mix"], p["w_qkv"], p["w_cin"], p["b_qkv"], p["b_cin"])
    attn = _attn_fwd(qkv, p["gq2"], p["gk2"], p["sinks"])
    cw8 = p["cw8"]
    c3, c1 = _conv_fwd(cin, cw8, p["cb"], p["cgain"], p["cbias"])
    x2, h2 = _out_proj(x, attn, c3, p["w_out"], p["w_out"], p["b_out"], p["g_ffn"])
    fw, fb = p["fw"], p["fb"]
    up0, act, dy, loss = _ffn_fwd(h2, x2, target, p["w_up"], fw, fb, p["w_down"])
    dup0, dh2, dfw, dfb = _ffn_bwd(dy, up0, p["w_up"], fw, fb, p["w_down"])
    dx2, dmixed, dg_ffn, db_out = _ffn_norm_bwd(dh2, dy, x2, p["g_ffn"], p["w_out"])
    dcin, dcw, dcb, dcgain, dcbias, dbcin = _conv_bwd(dmixed, c1, cin, cw8, p["cgain"], p["cbias"])
    dqkv, dgq, dgk, dsink, dbqkv = _attn_bwd(qkv, dmixed, p["gq2"], p["gk2"], p["sinks"])
    grad_x, dg_mix = _in_bwd(dqkv, dcin, p["w_qkv"], p["w_cin"], x, dx2, p["g_mix"])
    big = {
        "w_up": _tn_matmul(h2[None], dup0.reshape(N_DEV, s, FF_CHUNK), "dw_up"),
        "w_down": _tn_matmul(act, dy[None], "dw_down"),
        "w_qkv": _tn_matmul(h1[None], dqkv[None], "dw_qkv")[0],
        "w_cin": _tn_matmul(h1[None], dcin[None], "dw_cin")[0],
        "wo_attn": _tn_matmul(attn[None], dx2[None], "dw_out_attn")[0],
        "wo_conv": _tn_matmul(c3[None], dx2[None], "dw_out_conv")[0],
    }
    small = dict(dg_mix=dg_mix, dbqkv=dbqkv, dbcin=dbcin, dgq=dgq, dgk=dgk, dsink=dsink, dcw=dcw, dcb=dcb, dcgain=dcgain,
                 dcbias=dcbias, db_out=db_out, dg_ffn=dg_ffn, dfw=dfw, dfb=dfb)
    return loss, grad_x, big, small


BIG = ("w_in", "w_out", "w_up", "w_down")
SMALL = ("mix_norm_gain", "b_in", "q_norm_gain", "k_norm_gain", "attn_sinks", "conv_dw_w", "conv_dw_b",
         "conv_norm_gain", "conv_norm_bias", "b_out", "ffn_norm_gain", "ffn_dw_w", "ffn_dw_b")
ORDER = ("mix_norm_gain", "w_in", "b_in", "q_norm_gain", "k_norm_gain", "attn_sinks", "conv_dw_w", "conv_dw_b",
         "conv_norm_gain", "conv_norm_bias", "w_out", "b_out", "ffn_norm_gain", "w_up", "ffn_dw_w", "ffn_dw_b", "w_down")


def kernel(x, mix_norm_gain, w_in, b_in, q_norm_gain, k_norm_gain, attn_sinks, conv_dw_w, conv_dw_b, conv_norm_gain, conv_norm_bias, w_out, b_out, ffn_norm_gain, w_up, ffn_dw_w, ffn_dw_b, w_down, loss_target, m_mix_norm_gain, m_w_in, m_b_in, m_q_norm_gain, m_k_norm_gain, m_attn_sinks, m_conv_dw_w, m_conv_dw_b, m_conv_norm_gain, m_conv_norm_bias, m_w_out, m_b_out, m_ffn_norm_gain, m_w_up, m_ffn_dw_w, m_ffn_dw_b, m_w_down, v_mix_norm_gain, v_w_in, v_b_in, v_q_norm_gain, v_k_norm_gain, v_attn_sinks, v_conv_dw_w, v_conv_dw_b, v_conv_norm_gain, v_conv_norm_bias, v_w_out, v_b_out, v_ffn_norm_gain, v_w_up, v_ffn_dw_w, v_ffn_dw_b, v_w_down):
    w = dict(mix_norm_gain=mix_norm_gain, w_in=w_in, b_in=b_in, q_norm_gain=q_norm_gain, k_norm_gain=k_norm_gain,
             attn_sinks=attn_sinks, conv_dw_w=conv_dw_w, conv_dw_b=conv_dw_b, conv_norm_gain=conv_norm_gain,
             conv_norm_bias=conv_norm_bias, w_out=w_out, b_out=b_out, ffn_norm_gain=ffn_norm_gain, w_up=w_up,
             ffn_dw_w=ffn_dw_w, ffn_dw_b=ffn_dw_b, w_down=w_down)
    m = dict(mix_norm_gain=m_mix_norm_gain, w_in=m_w_in, b_in=m_b_in, q_norm_gain=m_q_norm_gain, k_norm_gain=m_k_norm_gain,
             attn_sinks=m_attn_sinks, conv_dw_w=m_conv_dw_w, conv_dw_b=m_conv_dw_b, conv_norm_gain=m_conv_norm_gain,
             conv_norm_bias=m_conv_norm_bias, w_out=m_w_out, b_out=m_b_out, ffn_norm_gain=m_ffn_norm_gain, w_up=m_w_up,
             ffn_dw_w=m_ffn_dw_w, ffn_dw_b=m_ffn_dw_b, w_down=m_w_down)
    v = dict(mix_norm_gain=v_mix_norm_gain, w_in=v_w_in, b_in=v_b_in, q_norm_gain=v_q_norm_gain, k_norm_gain=v_k_norm_gain,
             attn_sinks=v_attn_sinks, conv_dw_w=v_conv_dw_w, conv_dw_b=v_conv_dw_b, conv_norm_gain=v_conv_norm_gain,
             conv_norm_bias=v_conv_norm_bias, w_out=v_w_out, b_out=v_b_out, ffn_norm_gain=v_ffn_norm_gain, w_up=v_w_up,
             ffn_dw_w=v_ffn_dw_w, ffn_dw_b=v_ffn_dw_b, w_down=v_w_down)
    s = x.shape[1]

    wi8, wo8, wu8, wd8, cw8, fw8 = _allgather([w_in, w_out, w_up, w_down, conv_dw_w, ffn_dw_w], [BF16] * 4 + [F32] * 2)
    w_in_full = wi8.transpose(1, 0, 2).reshape(D_MODEL, QKV_COLS + CIN_COLS)
    lane_pad = ((0, 0), (0, 0), (0, FF_PADDED - FF_CHUNK))
    p = {
        "g_mix": mix_norm_gain.reshape(1, -1), "w_qkv": w_in_full[:, :QKV_COLS], "w_cin": w_in_full[:, QKV_COLS:],
        "b_qkv": b_in[:QKV_COLS].reshape(1, -1), "b_cin": b_in[QKV_COLS:].reshape(1, -1),
        "gq2": jnp.tile(q_norm_gain, 2).reshape(1, -1), "gk2": jnp.tile(k_norm_gain, 2).reshape(1, -1), "sinks": attn_sinks,
        "cw8": jnp.repeat(cw8.transpose(1, 0, 2).reshape(CONV_KERNEL, CONV_WIDTH), SUBLANES, axis=0),
        "cb": conv_dw_b.reshape(1, -1), "cgain": conv_norm_gain.reshape(1, -1), "cbias": conv_norm_bias.reshape(1, -1),
        "w_out": wo8.reshape(D_MODEL, D_MODEL), "b_out": b_out.reshape(1, -1), "g_ffn": ffn_norm_gain.reshape(1, -1),
        "w_up": wu8, "fw": jnp.pad(fw8, lane_pad), "fb": jnp.pad(ffn_dw_b.reshape(N_DEV, 1, FF_CHUNK), lane_pad),
        "w_down": wd8.reshape(N_FF_PAIRS, FF_CHUNK, D_MODEL),
    }

    loss, grad_x, big, small = _local_step(x[0], loss_target[0], p)

    dw_in = jnp.concatenate([big["w_qkv"], big["w_cin"]], axis=1)
    dw_out = jnp.concatenate([big["wo_attn"], big["wo_conv"]], axis=0)
    g = {
        "w_in": _reduce_scatter(dw_in.reshape(D_MODEL, N_DEV, -1).transpose(1, 0, 2), "rs_w_in"),
        "w_out": _reduce_scatter(dw_out.reshape(N_DEV, -1, D_MODEL), "rs_w_out"),
        "w_up": _reduce_scatter(big["w_up"], "rs_w_up"),
        "w_down": _reduce_scatter(big["w_down"].reshape(N_DEV, -1, D_MODEL), "rs_w_down"),
    }
    gpack = _small_exchange(_pack_small(small), True, "allreduce_small")

    delta, new_m, new_v = {}, {}, {}
    for n in BIG:
        delta[n], new_m[n], new_v[n] = _adamw(w[n], g[n], m[n], v[n], "adamw_" + n)

    def view(a):
        return a if a.ndim == 2 else (a.reshape(-1, LANES) if a.size % LANES == 0 else a.reshape(1, -1))

    small_out = _adamw_small(gpack, *[{n: view(d[n]) for n in SMALL} for d in (w, m, v)])
    for n in SMALL:
        g[n], delta[n], new_m[n], new_v[n] = [a.reshape(w[n].shape) for a in small_out[n]]

    total = lax.psum(loss[0, 0], ("x", "y", "c"))
    return (total, grad_x.reshape(1, s, D_MODEL), *[g[n] for n in ORDER], *[delta[n] for n in ORDER],
            *[new_m[n] for n in ORDER], *[new_v[n] for n in ORDER])
```

```python
import math

import jax
import jax.numpy as jnp
from jax import lax
from jax.experimental import pallas as pl
from jax.experimental.pallas import tpu as pltpu

F32 = jnp.float32
BF16 = jnp.bfloat16

D_MODEL = 1024
HEAD_DIM = 64
N_Q_HEADS = 8
N_KV_HEADS = 2
Q_COLS = 512
KV_COLS = 128
QKV_COLS = Q_COLS + 2 * KV_COLS
CONV_WIDTH = 512
CIN_COLS = 2 * CONV_WIDTH
CONV_KERNEL = 31
CONV_HALO = 32
D_FF = 2816
N_DEV = 8
FF_CHUNK = 2 * D_FF // N_DEV
N_FF_PAIRS = N_DEV // 2
ATT_BLOCK = 128
EPS = 1e-6
NEG_INF = -1e30
SLOPES = [float(2.0 ** (-8.0 * (h + 1.0) / N_Q_HEADS)) for h in range(N_Q_HEADS)]

ADAM_LR = 0.001
ADAM_B1 = 0.9
ADAM_B2 = 0.999
ADAM_EPS = 1e-08
ADAM_WD = 0.01
ADAM_STEP = 10

LANES = 128
SUBLANES = 8
VMEM_LIMIT = 56 * 1024 * 1024
MESH = pl.DeviceIdType.MESH


def _cparams(*sem, **kw):
    return pltpu.CompilerParams(dimension_semantics=sem or None, vmem_limit_bytes=VMEM_LIMIT, **kw)


def _resident(shape):
    nd = len(shape)
    return pl.BlockSpec(shape, lambda *_: (0,) * nd, pipeline_mode=pl.Buffered(1))


def _dot(a, b):
    return jnp.dot(a, b, preferred_element_type=F32)


def _dot_nt(a, b):
    return lax.dot_general(a, b, (((1,), (1,)), ((), ())), preferred_element_type=F32)


def _dot_tn(a, b):
    return lax.dot_general(a, b, (((0,), (0,)), ((), ())), preferred_element_type=F32)


def _sigmoid(x):
    return 1.0 / (1.0 + jnp.exp(-x))


def _lo_mask(shape):
    return lax.broadcasted_iota(jnp.int32, shape, len(shape) - 1) % LANES < HEAD_DIM


def _half_sums(t, lo):
    s_lo = jnp.sum(jnp.where(lo, t, 0.0), axis=-1, keepdims=True)
    s_hi = jnp.sum(jnp.where(lo, 0.0, t), axis=-1, keepdims=True)
    return jnp.where(lo, s_lo, s_hi)


def _head_norm(t, lo):
    r = lax.rsqrt(_half_sums(t * t, lo) * (1.0 / HEAD_DIM) + EPS)
    return t * r, r


def _head_norm_bwd(dn, n, r, lo):
    return r * (dn - n * (_half_sums(dn * n, lo) * (1.0 / HEAD_DIM)))


def _tile(s):
    return min(512, s)


TN_TOKENS = 2048
FF_COLS = ((0, 256), (256, 512), (512, 704))


def _mix_proj(x, g_mix, w_qkv, w_cin, b_qkv, b_cin):
    s = x.shape[0]
    tm = _tile(s)

    def body(x_ref, g_ref, wq_ref, wc_ref, bq_ref, bc_ref, qkv_ref, cin_ref, h1_ref):
        xv = x_ref[...]
        r = lax.rsqrt(jnp.mean(xv * xv, axis=-1, keepdims=True) + EPS)
        h = (xv * r * g_ref[...]).astype(BF16)
        h1_ref[...] = h
        qkv_ref[...] = _dot(h, wq_ref[...]) + bq_ref[...]
        cin_ref[...] = _dot(h, wc_ref[...]) + bc_ref[...]

    return pl.pallas_call(
        body, grid=(s // tm,),
        in_specs=[pl.BlockSpec((tm, D_MODEL), lambda i: (i, 0)), _resident((1, D_MODEL)),
                  _resident((D_MODEL, QKV_COLS)), _resident((D_MODEL, CIN_COLS)),
                  _resident((1, QKV_COLS)), _resident((1, CIN_COLS))],
        out_specs=[pl.BlockSpec((tm, QKV_COLS), lambda i: (i, 0)), pl.BlockSpec((tm, CIN_COLS), lambda i: (i, 0)),
                   pl.BlockSpec((tm, D_MODEL), lambda i: (i, 0))],
        out_shape=[jax.ShapeDtypeStruct((s, QKV_COLS), F32), jax.ShapeDtypeStruct((s, CIN_COLS), F32),
                   jax.ShapeDtypeStruct((s, D_MODEL), BF16)],
        compiler_params=_cparams("parallel"), name="mix_proj")(x, g_mix, w_qkv, w_cin, b_qkv, b_cin)


def _kv_variants(kv_all, gk2, lo):
    k_all = kv_all[:, :LANES]
    v_all = kv_all[:, LANES:]
    kn_pre, rk = _head_norm(k_all, lo)
    kn = kn_pre * gk2
    kr = pltpu.roll(kn, HEAD_DIM, 1)
    vr = pltpu.roll(v_all, HEAD_DIM, 1)
    zero = jnp.zeros_like(kn)
    k_lo = [jnp.where(lo, kn, zero).astype(BF16), jnp.where(lo, kr, zero).astype(BF16)]
    k_hi = [jnp.where(lo, zero, kr).astype(BF16), jnp.where(lo, zero, kn).astype(BF16)]
    v_lo = [jnp.where(lo, v_all, zero).astype(BF16), jnp.where(lo, vr, zero).astype(BF16)]
    v_hi = [jnp.where(lo, zero, vr).astype(BF16), jnp.where(lo, zero, v_all).astype(BF16)]
    return k_lo, k_hi, v_lo, v_hi, kn_pre, rk


def _att_consts(first_tile, b):
    rows = 2 * ATT_BLOCK
    qi = lax.broadcasted_iota(jnp.int32, (rows, 2 * ATT_BLOCK), 0) % ATT_BLOCK
    kj = lax.broadcasted_iota(jnp.int32, (rows, 2 * ATT_BLOCK), 1)
    rel = qi + ATT_BLOCK - kj
    valid = (rel >= 0) & (rel < ATT_BLOCK)
    if b == 0:
        valid = valid & ((kj >= ATT_BLOCK) | jnp.logical_not(first_tile))
    return rel.astype(F32), valid


def _row_const(va, vb):
    top = lax.broadcasted_iota(jnp.int32, (2 * ATT_BLOCK, 1), 0) < ATT_BLOCK
    return jnp.where(top, va, vb)


def _probs(q2, k_op, rel, valid, slope, sink):
    sc = _dot_nt(q2, k_op) * (1.0 / math.sqrt(HEAD_DIM)) - slope * rel
    sc = jnp.where(valid, sc, NEG_INF)
    m = jnp.maximum(jnp.max(sc, axis=-1, keepdims=True), sink)
    p = jnp.exp(sc - m)
    e_sink = jnp.exp(sink - m)
    inv = 1.0 / (jnp.sum(p, axis=-1, keepdims=True) + e_sink)
    return p * inv, e_sink * inv


def _attn_fwd(qkv, gq2, gk2, sinks):
    s = qkv.shape[0]
    tq = _tile(s)
    nb = tq // ATT_BLOCK

    def body(q_ref, kv_ref, kvp_ref, gq_ref, gk_ref, sink_ref, out_ref):
        i = pl.program_id(0)
        lo = _lo_mask((1, LANES))
        kv_all = jnp.concatenate([kvp_ref[...], kv_ref[...]], axis=0)
        k_lo, k_hi, v_lo, v_hi, _, _ = _kv_variants(kv_all, gk_ref[...], lo)
        for b in range(nb):
            rel, valid = _att_consts(i == 0, b)
            rows = slice(b * ATT_BLOCK, (b + 1) * ATT_BLOCK)
            keys = slice(b * ATT_BLOCK, (b + 2) * ATT_BLOCK)
            for kvh in range(N_KV_HEADS):
                pairs = (2 * kvh, 2 * kvh + 1)
                q2 = jnp.concatenate([q_ref[rows, p * LANES:(p + 1) * LANES] for p in pairs], axis=0)
                qn, _ = _head_norm(q2, lo)
                q2 = (qn * gq_ref[...]).astype(BF16)
                out = None
                for odd, (k_op, v_op) in enumerate(((k_lo[kvh][keys], v_lo[kvh][keys]), (k_hi[kvh][keys], v_hi[kvh][keys]))):
                    ha, hb = 2 * pairs[0] + odd, 2 * pairs[1] + odd
                    p, _ = _probs(q2, k_op, rel, valid, _row_const(SLOPES[ha], SLOPES[hb]),
                                  _row_const(sink_ref[ha], sink_ref[hb]))
                    o = _dot(p.astype(BF16), v_op)
                    out = o if out is None else out + o
                for n, p in enumerate(pairs):
                    out_ref[rows, p * LANES:(p + 1) * LANES] = out[n * ATT_BLOCK:(n + 1) * ATT_BLOCK].astype(BF16)

    return pl.pallas_call(
        body, grid=(s // tq,),
        in_specs=[pl.BlockSpec((tq, Q_COLS), lambda i: (i, 0)),
                  pl.BlockSpec((tq, 2 * KV_COLS), lambda i: (i, 2)),
                  pl.BlockSpec((ATT_BLOCK, 2 * KV_COLS), lambda i: (jnp.maximum(i * nb - 1, 0), 2)),
                  _resident((1, LANES)), _resident((1, LANES)),
                  pl.BlockSpec(memory_space=pltpu.SMEM)],
        out_specs=pl.BlockSpec((tq, Q_COLS), lambda i: (i, 0)),
        out_shape=jax.ShapeDtypeStruct((s, Q_COLS), BF16),
        compiler_params=_cparams("parallel"), name="attn_fwd")(qkv, qkv, qkv, gq2, gk2, sinks)


def _group_stats(c1, lo):
    mu = _half_sums(c1, lo) * (1.0 / HEAD_DIM)
    d = c1 - mu
    rstd = lax.rsqrt(_half_sums(d * d, lo) * (1.0 / HEAD_DIM) + EPS)
    return d * rstd, rstd


def _rows(ref, first_row, n):
    return ref[pl.ds(first_row, n, stride=1), :].reshape(n // SUBLANES, SUBLANES, LANES)


def _conv_fwd(cin, cw8, cb, gain, bias):
    s = cin.shape[0]
    tm = _tile(s)
    rc = 64
    nchunk = CONV_WIDTH // LANES
    lead = CONV_HALO - (CONV_KERNEL - 1)

    def body(cin_ref, cw_ref, cb_ref, gain_ref, bias_ref, c3_ref, c1_ref, ext_ref):
        @pl.when(pl.program_id(0) == 0)
        def _():
            ext_ref[:, 0:CONV_HALO, :] = jnp.zeros((nchunk, CONV_HALO, LANES), F32)

        lo = _lo_mask((1, LANES))
        for cc in range(nchunk):
            cols = slice(cc * LANES, (cc + 1) * LANES)
            gcols = slice(CONV_WIDTH + cc * LANES, CONV_WIDTH + (cc + 1) * LANES)
            ext_ref[cc, CONV_HALO:CONV_HALO + tm, :] = cin_ref[:, cols] * _sigmoid(cin_ref[:, gcols])
            ext = ext_ref.at[cc]
            for r in range(tm // rc):
                rows = slice(r * rc, (r + 1) * rc)
                acc = jnp.zeros((rc // SUBLANES, SUBLANES, LANES), F32)
                for k in range(CONV_KERNEL):
                    acc = acc + cw_ref[k * SUBLANES:(k + 1) * SUBLANES, cols][None] * _rows(ext, r * rc + lead + k, rc)
                c1 = acc.reshape(rc, LANES) + cb_ref[:, cols]
                c1_ref[cc, rows, :] = c1
                nrm, _ = _group_stats(c1, lo)
                c2 = nrm * gain_ref[:, cols] + bias_ref[:, cols]
                c3_ref[rows, cols] = (c2 * _sigmoid(c2)).astype(BF16)
        ext_ref[:, 0:CONV_HALO, :] = ext_ref[:, tm:tm + CONV_HALO, :]

    return pl.pallas_call(
        body, grid=(s // tm,),
        in_specs=[pl.BlockSpec((tm, CIN_COLS), lambda i: (i, 0)), _resident((CONV_KERNEL * SUBLANES, CONV_WIDTH)),
                  _resident((1, CONV_WIDTH)), _resident((1, CONV_WIDTH)), _resident((1, CONV_WIDTH))],
        out_specs=[pl.BlockSpec((tm, CONV_WIDTH), lambda i: (i, 0)), pl.BlockSpec((nchunk, tm, LANES), lambda i: (0, i, 0))],
        out_shape=[jax.ShapeDtypeStruct((s, CONV_WIDTH), BF16), jax.ShapeDtypeStruct((nchunk, s, LANES), F32)],
        scratch_shapes=[pltpu.VMEM((nchunk, tm + CONV_HALO, LANES), F32)],
        compiler_params=_cparams("arbitrary"), name="conv_fwd")(cin, cw8, cb, gain, bias)


def _out_proj(x, attn, c3, wo_a, wo_c, b_out, g_ffn):
    s = x.shape[0]
    tm = _tile(s)

    def body(x_ref, a_ref, c_ref, wa_ref, wc_ref, b_ref, g_ref, x2_ref, h2_ref):
        x2 = x_ref[...] + _dot(a_ref[...], wa_ref[...]) + _dot(c_ref[...], wc_ref[...]) + b_ref[...]
        x2_ref[...] = x2
        r = lax.rsqrt(jnp.mean(x2 * x2, axis=-1, keepdims=True) + EPS)
        h2_ref[...] = (x2 * r * g_ref[...]).astype(BF16)

    return pl.pallas_call(
        body, grid=(s // tm,),
        in_specs=[pl.BlockSpec((tm, D_MODEL), lambda i: (i, 0)), pl.BlockSpec((tm, Q_COLS), lambda i: (i, 0)),
                  pl.BlockSpec((tm, CONV_WIDTH), lambda i: (i, 0)),
                  pl.BlockSpec((Q_COLS, D_MODEL), lambda i: (0, 0), pipeline_mode=pl.Buffered(1)),
                  pl.BlockSpec((CONV_WIDTH, D_MODEL), lambda i: (1, 0), pipeline_mode=pl.Buffered(1)),
                  _resident((1, D_MODEL)), _resident((1, D_MODEL))],
        out_specs=[pl.BlockSpec((tm, D_MODEL), lambda i: (i, 0)), pl.BlockSpec((tm, D_MODEL), lambda i: (i, 0))],
        out_shape=[jax.ShapeDtypeStruct((s, D_MODEL), F32), jax.ShapeDtypeStruct((s, D_MODEL), BF16)],
        compiler_params=_cparams("parallel"), name="out_proj")(x, attn, c3, wo_a, wo_c, b_out, g_ffn)


FF_LANE_CHUNKS = -(-FF_CHUNK // LANES)
FF_PADDED = FF_LANE_CHUNKS * LANES


def _tap(ref, first_row, n):
    return ref[pl.ds(first_row, n, stride=1), :]


def _ffn_fwd(h2, x2, target, w_up, fw, fb, w_down):
    s = h2.shape[0]
    tm = _tile(s)
    hal = SUBLANES
    rc = min(128, tm)

    def body(h_ref, x2_ref, t_ref, wu_ref, fw_ref, fb_ref, wd_ref, up0_ref, act_ref, dy_ref, loss_ref,
             ext_ref, carry_ref, act_buf, y_ref):
        i, ci = pl.program_id(0), pl.program_id(1)

        @pl.when((i == 0) & (ci == 0))
        def _():
            carry_ref[...] = jnp.zeros(carry_ref.shape, F32)
            ext_ref[...] = jnp.zeros(ext_ref.shape, F32)
            act_buf[...] = jnp.zeros(act_buf.shape, BF16)
            loss_ref[...] = jnp.zeros((1, 1), F32)

        @pl.when(ci == 0)
        def _():
            y_ref[...] = x2_ref[...]

        h = h_ref[...]
        ws = (fw_ref[ci], fw_ref[ci + N_FF_PAIRS])
        bs = (fb_ref[ci], fb_ref[ci + N_FF_PAIRS])
        down = None
        for lo_c, hi_c in FF_COLS:
            for half in range(2):
                c = ci + half * N_FF_PAIRS
                u0 = _dot(h, wu_ref[c, :, lo_c:hi_c])
                up0_ref[half, 0, :, lo_c:hi_c] = u0.astype(BF16)
                for j in range(lo_c // LANES, -(-hi_c // LANES)):
                    w = min(LANES, hi_c - j * LANES)
                    ext_ref[half, j, 0:hal, 0:w] = carry_ref[c, :, j * LANES:j * LANES + w]
                    ext_ref[half, j, hal:hal + tm, 0:w] = u0[:, j * LANES - lo_c:j * LANES - lo_c + w]
                carry_ref[c, :, lo_c:hi_c] = u0[tm - hal:tm, :]
            for j in range(lo_c // LANES, -(-hi_c // LANES)):
                lanes = slice(j * LANES, (j + 1) * LANES)
                for r in range(tm // rc):
                    base = r * rc
                    ups = []
                    for half in range(2):
                        e, w = ext_ref.at[half, j], ws[half]
                        ups.append(w[0:1, lanes] * _tap(e, base + hal - 2, rc) + w[1:2, lanes] * _tap(e, base + hal - 1, rc)
                                   + w[2:3, lanes] * _tap(e, base + hal, rc) + bs[half][:, lanes])
                    g, u = ups
                    act_buf[base:base + rc, lanes] = (g * _sigmoid(g) * u).astype(BF16)
            act = act_buf[:, lo_c:hi_c]
            act_ref[0, :, lo_c:hi_c] = act
            part = _dot(act, wd_ref[ci, lo_c:hi_c, :])
            down = part if down is None else down + part
        y_ref[...] += down

        @pl.when(ci == N_FF_PAIRS - 1)
        def _():
            e = y_ref[...] - t_ref[...]
            dy_ref[...] = e * (1.0 / D_MODEL)
            loss_ref[...] += (0.5 / D_MODEL) * jnp.sum(e * e).reshape(1, 1)

    tok = lambda i, ci: (i, 0)
    return pl.pallas_call(
        body, grid=(s // tm, N_FF_PAIRS),
        in_specs=[pl.BlockSpec((tm, D_MODEL), tok), pl.BlockSpec((tm, D_MODEL), tok), pl.BlockSpec((tm, D_MODEL), tok),
                  _resident((N_DEV, D_MODEL, FF_CHUNK)), _resident((N_DEV, 3, FF_PADDED)), _resident((N_DEV, 1, FF_PADDED)),
                  _resident((N_FF_PAIRS, FF_CHUNK, D_MODEL))],
        out_specs=[pl.BlockSpec((2, 1, tm, FF_CHUNK), lambda i, ci: (0, ci, i, 0)),
                   pl.BlockSpec((1, tm, FF_CHUNK), lambda i, ci: (ci, i, 0)),
                   pl.BlockSpec((tm, D_MODEL), tok), pl.BlockSpec((1, 1), lambda i, ci: (0, 0))],
        out_shape=[jax.ShapeDtypeStruct((2, N_FF_PAIRS, s, FF_CHUNK), BF16),
                   jax.ShapeDtypeStruct((N_FF_PAIRS, s, FF_CHUNK), BF16), jax.ShapeDtypeStruct((s, D_MODEL), F32),
                   jax.ShapeDtypeStruct((1, 1), F32)],
        scratch_shapes=[pltpu.VMEM((2, FF_LANE_CHUNKS, tm + hal, LANES), F32), pltpu.VMEM((N_DEV, hal, FF_CHUNK), F32),
                        pltpu.VMEM((tm, FF_PADDED), BF16), pltpu.VMEM((tm, D_MODEL), F32)],
        compiler_params=_cparams("arbitrary", "arbitrary"), name="ffn_fwd")(h2, x2, target, w_up, fw, fb, w_down)


def _ffn_bwd(dy, up0, w_up, fw, fb, w_down):
    s = dy.shape[0]
    tm = _tile(s)
    nt = s // tm
    hal = 2 * SUBLANES
    nxt = SUBLANES
    rc = min(128, tm)

    def body(dy_ref, up0_ref, up0h_ref, wu_ref, fw_ref, fb_ref, wd_ref,
             dup0_ref, dh2_ref, dfw_ref, dfb_ref, ext_ref, dext_ref, carry_ref, dact_buf, dup0_buf):
        i, ci = pl.program_id(0), pl.program_id(1)
        t = nt - 1 - i

        @pl.when((i == 0) & (ci == 0))
        def _():
            for ref in (carry_ref, dfw_ref, dfb_ref, ext_ref, dext_ref, dact_buf):
                ref[...] = jnp.zeros(ref.shape, F32)
            dup0_buf[...] = jnp.zeros(dup0_buf.shape, BF16)

        @pl.when(ci == 0)
        def _():
            dh2_ref[...] = jnp.zeros(dh2_ref.shape, F32)

        dyb = dy_ref[...].astype(BF16)
        ws = (fw_ref[ci], fw_ref[ci + N_FF_PAIRS])
        bs = (fb_ref[ci], fb_ref[ci + N_FF_PAIRS])
        fold = lambda v: jnp.sum(v.reshape(rc // SUBLANES, SUBLANES, LANES), axis=0)
        dh2 = None
        for lo_c, hi_c in FF_COLS:
            dact_buf[:, lo_c:hi_c] = _dot_nt(dyb, wd_ref[ci, lo_c:hi_c, :])
            chunks = range(lo_c // LANES, -(-hi_c // LANES))
            for half in range(2):
                c = ci + half * N_FF_PAIRS
                prev = jnp.where(t > 0, up0h_ref[half, 0, :, lo_c:hi_c].astype(F32), 0.0)
                cur = up0_ref[half, 0, :, lo_c:hi_c].astype(F32)
                for j in chunks:
                    w = min(LANES, hi_c - j * LANES)
                    cols = slice(j * LANES - lo_c, j * LANES - lo_c + w)
                    ext_ref[half, j, 0:hal, 0:w] = prev[:, cols]
                    ext_ref[half, j, hal:hal + tm, 0:w] = cur[:, cols]
                    dext_ref[half, j, tm:tm + nxt, 0:w] = carry_ref[c, :, j * LANES:j * LANES + w]
            for j in chunks:
                lanes = slice(j * LANES, (j + 1) * LANES)
                acc = [jnp.zeros((SUBLANES, LANES), F32)] * 8
                for r in range(tm // rc):
                    base = r * rc
                    taps, ups = [], []
                    for half in range(2):
                        e, w = ext_ref.at[half, j], ws[half]
                        x = [_tap(e, base + hal - 2 + k, rc) for k in range(3)]
                        taps.append(x)
                        ups.append(w[0:1, lanes] * x[0] + w[1:2, lanes] * x[1] + w[2:3, lanes] * x[2] + bs[half][:, lanes])
                    g, u = ups
                    sg = _sigmoid(g)
                    dact = dact_buf[base:base + rc, lanes]
                    ds = (dact * u * (sg * (1.0 + g * (1.0 - sg))), dact * (g * sg))
                    for half in range(2):
                        dext_ref[half, j, base:base + rc, :] = ds[half]
                        acc[4 * half] = acc[4 * half] + fold(ds[half])
                        for k in range(3):
                            acc[4 * half + 1 + k] = acc[4 * half + 1 + k] + fold(ds[half] * taps[half][k])
                for half in range(2):
                    c = ci + half * N_FF_PAIRS
                    dfb_ref[c, :, lanes] += jnp.sum(acc[4 * half], axis=0, keepdims=True)
                    dfw_ref[c, :, lanes] += jnp.concatenate(
                        [jnp.sum(acc[4 * half + 1 + k], axis=0, keepdims=True) for k in range(3)], axis=0)
                for r in range(tm // rc):
                    base = r * rc
                    for half in range(2):
                        d, w = dext_ref.at[half, j], ws[half]
                        dup0 = w[2:3, lanes] * _tap(d, base, rc) + w[1:2, lanes] * _tap(d, base + 1, rc) + w[0:1, lanes] * _tap(d, base + 2, rc)
                        dup0_buf[half, base:base + rc, lanes] = dup0.astype(BF16)
            for half in range(2):
                c = ci + half * N_FF_PAIRS
                for j in chunks:
                    w = min(LANES, hi_c - j * LANES)
                    carry_ref[c, :, j * LANES:j * LANES + w] = dext_ref[half, j, 0:nxt, 0:w]
                dup0 = dup0_buf[half, :, lo_c:hi_c]
                dup0_ref[half, 0, :, lo_c:hi_c] = dup0
                part = _dot_nt(dup0, wu_ref[c, :, lo_c:hi_c])
                dh2 = part if dh2 is None else dh2 + part
        dh2_ref[...] += dh2

    tok = lambda i, ci: (nt - 1 - i, 0)
    acc = lambda shape: pl.BlockSpec(shape, lambda i, ci: (0,) * len(shape))
    return pl.pallas_call(
        body, grid=(nt, N_FF_PAIRS),
        in_specs=[pl.BlockSpec((tm, D_MODEL), tok),
                  pl.BlockSpec((2, 1, tm, FF_CHUNK), lambda i, ci: (0, ci, nt - 1 - i, 0)),
                  pl.BlockSpec((2, 1, hal, FF_CHUNK), lambda i, ci: (0, ci, jnp.maximum((nt - 1 - i) * (tm // hal) - 1, 0), 0)),
                  _resident((N_DEV, D_MODEL, FF_CHUNK)), _resident((N_DEV, 3, FF_PADDED)), _resident((N_DEV, 1, FF_PADDED)),
                  _resident((N_FF_PAIRS, FF_CHUNK, D_MODEL))],
        out_specs=[pl.BlockSpec((2, 1, tm, FF_CHUNK), lambda i, ci: (0, ci, nt - 1 - i, 0)),
                   pl.BlockSpec((tm, D_MODEL), tok), acc((N_DEV, 3, FF_PADDED)), acc((N_DEV, 1, FF_PADDED))],
        out_shape=[jax.ShapeDtypeStruct((2, N_FF_PAIRS, s, FF_CHUNK), BF16), jax.ShapeDtypeStruct((s, D_MODEL), F32),
                   jax.ShapeDtypeStruct((N_DEV, 3, FF_PADDED), F32), jax.ShapeDtypeStruct((N_DEV, 1, FF_PADDED), F32)],
        scratch_shapes=[pltpu.VMEM((2, FF_LANE_CHUNKS, tm + hal, LANES), F32), pltpu.VMEM((2, FF_LANE_CHUNKS, tm + nxt, LANES), F32),
                        pltpu.VMEM((N_DEV, nxt, FF_CHUNK), F32), pltpu.VMEM((tm, FF_PADDED), F32),
                        pltpu.VMEM((2, tm, FF_PADDED), BF16)],
        compiler_params=_cparams("arbitrary", "arbitrary"), name="ffn_bwd")(dy, up0, up0, w_up, fw, fb, w_down)


def _ffn_norm_bwd(dh2, dy, x2, g_ffn, w_out):
    s = dy.shape[0]
    tm = _tile(s)

    def body(dh_ref, dy_ref, x2_ref, g_ref, wo_ref, dx2_ref, dmix_ref, dg_ref, dbo_ref):
        @pl.when(pl.program_id(0) == 0)
        def _():
            dg_ref[...] = jnp.zeros(dg_ref.shape, F32)
            dbo_ref[...] = jnp.zeros(dbo_ref.shape, F32)

        x2v = x2_ref[...]
        r = lax.rsqrt(jnp.mean(x2v * x2v, axis=-1, keepdims=True) + EPS)
        n2 = x2v * r
        dh2 = dh_ref[...]
        dg_ref[...] += jnp.sum(dh2 * n2, axis=0, keepdims=True)
        dn = dh2 * g_ref[...]
        dx2 = dy_ref[...] + r * (dn - n2 * jnp.mean(dn * n2, axis=-1, keepdims=True))
        dx2_ref[...] = dx2
        dbo_ref[...] += jnp.sum(dx2, axis=0, keepdims=True)
        dmix_ref[...] = _dot_nt(dx2.astype(BF16), wo_ref[...])

    tok = pl.BlockSpec((tm, D_MODEL), lambda i: (i, 0))
    vec = pl.BlockSpec((1, D_MODEL), lambda i: (0, 0))
    return pl.pallas_call(
        body, grid=(s // tm,),
        in_specs=[tok, tok, tok, _resident((1, D_MODEL)), _resident((D_MODEL, D_MODEL))],
        out_specs=[tok, tok, vec, vec],
        out_shape=[jax.ShapeDtypeStruct((s, D_MODEL), F32), jax.ShapeDtypeStruct((s, D_MODEL), F32),
                   jax.ShapeDtypeStruct((1, D_MODEL), F32), jax.ShapeDtypeStruct((1, D_MODEL), F32)],
        compiler_params=_cparams("arbitrary"), name="ffn_norm_bwd")(dh2, dy, x2, g_ffn, w_out)


def _conv_bwd(dmixed, c1, cin, cw8, gain, bias):
    s = cin.shape[0]
    tm = _tile(s)
    nt = s // tm
    rc = 64
    rn = min(256, tm)
    hal = CONV_HALO
    lead = hal - (CONV_KERNEL - 1)
    nchunk = CONV_WIDTH // LANES

    def body(dc3_ref, dc3n_ref, c1_ref, c1n_ref, cin_ref, cinp_ref, cw_ref, gain_ref, bias_ref,
             dcin_ref, dcw_ref, dcb_ref, dgain_ref, dbias_ref, dbcin_ref, c0_ext, dc1_ext, dcw8):
        i = pl.program_id(0)
        first, last = i == 0, i == nt - 1

        @pl.when(first)
        def _():
            for ref in (dcw8, dcb_ref, dgain_ref, dbias_ref, dbcin_ref):
                ref[...] = jnp.zeros(ref.shape, F32)

        lo = _lo_mask((1, LANES))

        def norm_bwd(dc3, c1v, cols):
            nrm, rstd = _group_stats(c1v, lo)
            c2 = nrm * gain_ref[:, cols] + bias_ref[:, cols]
            sg = _sigmoid(c2)
            dc2 = dc3 * (sg * (1.0 + c2 * (1.0 - sg)))
            dn = dc2 * gain_ref[:, cols]
            inv = 1.0 / HEAD_DIM
            dc1 = rstd * (dn - _half_sums(dn, lo) * inv - nrm * (_half_sums(dn * nrm, lo) * inv))
            return dc1, dc2, nrm

        def row_sum(v):
            return jnp.sum(v, axis=0, keepdims=True)

        for cc in range(nchunk):
            cols = slice(cc * LANES, (cc + 1) * LANES)
            gcols = slice(CONV_WIDTH + cc * LANES, CONV_WIDTH + (cc + 1) * LANES)
            c0e, d1e = c0_ext.at[cc], dc1_ext.at[cc]
            c0e[0:hal, :] = jnp.where(first, 0.0, cinp_ref[:, cols] * _sigmoid(cinp_ref[:, gcols]))
            dc1n, _, _ = norm_bwd(dc3n_ref[:, cols], c1n_ref[cc], cols)
            d1e[tm:tm + hal, :] = jnp.where(last, 0.0, dc1n)

            for r in range(tm // rn):
                rows = slice(r * rn, (r + 1) * rn)
                c0e[hal + r * rn:hal + (r + 1) * rn, :] = cin_ref[rows, cols] * _sigmoid(cin_ref[rows, gcols])
                dc1, dc2, nrm = norm_bwd(dc3_ref[rows, cols], c1_ref[cc, rows, :], cols)
                d1e[rows, :] = dc1
                dgain_ref[:, cols] += row_sum(dc2 * nrm)
                dbias_ref[:, cols] += row_sum(dc2)
                dcb_ref[:, cols] += row_sum(dc1)
            zero = jnp.zeros((1, LANES), F32)

            for k0 in range(0, CONV_KERNEL, SUBLANES):
                taps = range(k0, min(k0 + SUBLANES, CONV_KERNEL))

                def tap_sums(r, acc, taps=taps):
                    d = _rows(d1e, r * rc, rc)
                    return tuple(a + jnp.sum(d * _rows(c0e, r * rc + lead + k, rc), axis=0) for a, k in zip(acc, taps))

                acc = lax.fori_loop(0, tm // rc, tap_sums, tuple(dcw8[k * SUBLANES:(k + 1) * SUBLANES, cols] for k in taps))
                for a, k in zip(acc, taps):
                    dcw8[k * SUBLANES:(k + 1) * SUBLANES, cols] = a

            def input_grad(r, sums):
                rows = pl.ds(pl.multiple_of(r * rc, rc), rc)
                dc0 = jnp.zeros((rc // SUBLANES, SUBLANES, LANES), F32)
                for k in range(CONV_KERNEL):
                    dc0 = dc0 + cw_ref[k * SUBLANES:(k + 1) * SUBLANES, cols][None] * _rows(d1e, r * rc + CONV_KERNEL - 1 - k, rc)
                dc0 = dc0.reshape(rc, LANES)
                sg = _sigmoid(cin_ref[rows, gcols])
                da = dc0 * sg
                dgate = dc0 * cin_ref[rows, cols] * sg * (1.0 - sg)
                dcin_ref[rows, cols] = da.astype(BF16)
                dcin_ref[rows, gcols] = dgate.astype(BF16)
                return sums[0] + row_sum(da), sums[1] + row_sum(dgate)

            sums = lax.fori_loop(0, tm // rc, input_grad, (zero, zero))
            dbcin_ref[:, cols] += sums[0]
            dbcin_ref[:, gcols] += sums[1]

        @pl.when(last)
        def _():
            for k in range(CONV_KERNEL):
                dcw_ref[k:k + 1, :] = jnp.sum(dcw8[k * SUBLANES:(k + 1) * SUBLANES, :], axis=0, keepdims=True)

    nh = tm // hal
    acc = lambda shape: pl.BlockSpec(shape, lambda i: (0,) * len(shape))
    return pl.pallas_call(
        body, grid=(nt,),
        in_specs=[pl.BlockSpec((tm, CONV_WIDTH), lambda i: (i, 1)),
                  pl.BlockSpec((hal, CONV_WIDTH), lambda i: (jnp.minimum((i + 1) * nh, s // hal - 1), 1)),
                  pl.BlockSpec((nchunk, tm, LANES), lambda i: (0, i, 0)),
                  pl.BlockSpec((nchunk, hal, LANES), lambda i: (0, jnp.minimum((i + 1) * nh, s // hal - 1), 0)),
                  pl.BlockSpec((tm, CIN_COLS), lambda i: (i, 0)),
                  pl.BlockSpec((hal, CIN_COLS), lambda i: (jnp.maximum(i * nh - 1, 0), 0)),
                  _resident((CONV_KERNEL * SUBLANES, CONV_WIDTH)), _resident((1, CONV_WIDTH)), _resident((1, CONV_WIDTH))],
        out_specs=[pl.BlockSpec((tm, CIN_COLS), lambda i: (i, 0)), acc((CONV_KERNEL, CONV_WIDTH)), acc((1, CONV_WIDTH)),
                   acc((1, CONV_WIDTH)), acc((1, CONV_WIDTH)), acc((1, CIN_COLS))],
        out_shape=[jax.ShapeDtypeStruct((s, CIN_COLS), BF16), jax.ShapeDtypeStruct((CONV_KERNEL, CONV_WIDTH), F32),
                   jax.ShapeDtypeStruct((1, CONV_WIDTH), F32), jax.ShapeDtypeStruct((1, CONV_WIDTH), F32),
                   jax.ShapeDtypeStruct((1, CONV_WIDTH), F32), jax.ShapeDtypeStruct((1, CIN_COLS), F32)],
        scratch_shapes=[pltpu.VMEM((nchunk, tm + hal, LANES), F32), pltpu.VMEM((nchunk, tm + hal, LANES), F32),
                        pltpu.VMEM((CONV_KERNEL * SUBLANES, CONV_WIDTH), F32)],
        compiler_params=_cparams("arbitrary"), name="conv_bwd")(dmixed, dmixed, c1, c1, cin, cin, cw8, gain, bias)


def _attn_bwd(qkv, dmixed, gq2, gk2, sinks):
    s = qkv.shape[0]
    tq = _tile(s)
    nb = tq // ATT_BLOCK
    nt = s // tq

    def body(q_ref, kv_ref, kvp_ref, do_ref, gq_ref, gk_ref, sink_ref,
             dqkv_ref, dgq_ref, dgk_ref, dsink_ref, dbqkv_ref, dk_acc, dv_acc, carry_k, carry_v):
        i = pl.program_id(0)
        t = nt - 1 - i

        @pl.when(i == 0)
        def _():
            for ref in (carry_k, carry_v, dgq_ref, dgk_ref, dsink_ref, dbqkv_ref):
                ref[...] = jnp.zeros(ref.shape, F32)

        lo = _lo_mask((1, LANES))
        lane_id = lax.broadcasted_iota(jnp.int32, (1, LANES), 1)
        kv_all = jnp.concatenate([kvp_ref[...], kv_ref[...]], axis=0)
        k_lo, k_hi, v_lo, v_hi, kn_pre, rk = _kv_variants(kv_all, gk_ref[...], lo)
        for acc_ref, carry in ((dk_acc, carry_k), (dv_acc, carry_v)):
            acc_ref[:, 0:tq, :] = jnp.zeros((N_KV_HEADS, tq, LANES), F32)
            acc_ref[:, tq:tq + ATT_BLOCK, :] = carry[...]
        dsink = jnp.zeros((1, LANES), F32)
        dgq = jnp.zeros((1, LANES), F32)
        gq = gq_ref[...]
        for b in range(nb):
            rel, valid = _att_consts(t == 0, b)
            rows = slice(b * ATT_BLOCK, (b + 1) * ATT_BLOCK)
            keys = slice(b * ATT_BLOCK, (b + 2) * ATT_BLOCK)
            for kvh in range(N_KV_HEADS):
                pairs = (2 * kvh, 2 * kvh + 1)
                q_raw = jnp.concatenate([q_ref[rows, p * LANES:(p + 1) * LANES] for p in pairs], axis=0)
                qn_pre, rq = _head_norm(q_raw, lo)
                q2 = (qn_pre * gq).astype(BF16)
                do2 = jnp.concatenate([do_ref[rows, p * LANES:(p + 1) * LANES] for p in pairs], axis=0).astype(BF16)
                dq2 = jnp.zeros((2 * ATT_BLOCK, LANES), F32)
                for odd, (k_op, v_op) in enumerate(((k_lo[kvh][keys], v_lo[kvh][keys]), (k_hi[kvh][keys], v_hi[kvh][keys]))):
                    ha, hb = 2 * pairs[0] + odd, 2 * pairs[1] + odd
                    p, p_sink = _probs(q2, k_op, rel, valid, _row_const(SLOPES[ha], SLOPES[hb]),
                                       _row_const(sink_ref[ha], sink_ref[hb]))
                    dp = _dot_nt(do2, v_op)
                    delta = jnp.sum(p * dp, axis=-1, keepdims=True)
                    ds = (p * (dp - delta) * (1.0 / math.sqrt(HEAD_DIM))).astype(BF16)
                    dsk = p_sink * delta
                    dsink = dsink - jnp.where(lane_id == ha, jnp.sum(dsk[0:ATT_BLOCK]), 0.0) \
                        - jnp.where(lane_id == hb, jnp.sum(dsk[ATT_BLOCK:]), 0.0)
                    dq2 = dq2 + _dot(ds, k_op)
                    half = lo if odd == 0 else jnp.logical_not(lo)
                    dk_acc[kvh, keys, :] += jnp.where(half, _dot_tn(ds, q2), 0.0)
                    dv_acc[kvh, keys, :] += jnp.where(half, _dot_tn(p.astype(BF16), do2), 0.0)
                dgq = dgq + jnp.sum(dq2 * qn_pre, axis=0, keepdims=True)
                dq_raw = _head_norm_bwd(dq2 * gq, qn_pre, rq, lo)
                for n, p_ in enumerate(pairs):
                    blk = dq_raw[n * ATT_BLOCK:(n + 1) * ATT_BLOCK]
                    dqkv_ref[rows, p_ * LANES:(p_ + 1) * LANES] = blk.astype(BF16)
                    dbqkv_ref[:, p_ * LANES:(p_ + 1) * LANES] += jnp.sum(blk, axis=0, keepdims=True)
        carry_k[...] = dk_acc[:, 0:ATT_BLOCK, :]
        carry_v[...] = dv_acc[:, 0:ATT_BLOCK, :]

        def fold(acc_ref):
            both = []
            for kvh in range(N_KV_HEADS):
                a = acc_ref[kvh, ATT_BLOCK:ATT_BLOCK + tq, :]
                both.append(a + pltpu.roll(a, HEAD_DIM, 1))
            return jnp.where(lo, both[0], both[1])

        dkn = fold(dk_acc)
        dv = fold(dv_acc)
        kn_c, rk_c = kn_pre[ATT_BLOCK:], rk[ATT_BLOCK:]
        dgk_ref[...] += jnp.sum(dkn * kn_c, axis=0, keepdims=True)
        dk_raw = _head_norm_bwd(dkn * gk_ref[...], kn_c, rk_c, lo)
        dqkv_ref[:, Q_COLS:Q_COLS + KV_COLS] = dk_raw.astype(BF16)
        dqkv_ref[:, Q_COLS + KV_COLS:] = dv.astype(BF16)
        dbqkv_ref[:, Q_COLS:Q_COLS + KV_COLS] += jnp.sum(dk_raw, axis=0, keepdims=True)
        dbqkv_ref[:, Q_COLS + KV_COLS:] += jnp.sum(dv, axis=0, keepdims=True)
        dgq_ref[...] += dgq
        dsink_ref[...] += dsink

        @pl.when(i == nt - 1)
        def _():
            for ref in (dgq_ref, dgk_ref):
                v = ref[...]
                ref[...] = v + pltpu.roll(v, HEAD_DIM, 1)

    acc = lambda shape: pl.BlockSpec(shape, lambda i: (0,) * len(shape))
    return pl.pallas_call(
        body, grid=(nt,),
        in_specs=[pl.BlockSpec((tq, Q_COLS), lambda i: (nt - 1 - i, 0)),
                  pl.BlockSpec((tq, 2 * KV_COLS), lambda i: (nt - 1 - i, 2)),
                  pl.BlockSpec((ATT_BLOCK, 2 * KV_COLS), lambda i: (jnp.maximum((nt - 1 - i) * nb - 1, 0), 2)),
                  pl.BlockSpec((tq, Q_COLS), lambda i: (nt - 1 - i, 0)),
                  _resident((1, LANES)), _resident((1, LANES)), pl.BlockSpec(memory_space=pltpu.SMEM)],
        out_specs=[pl.BlockSpec((tq, QKV_COLS), lambda i: (nt - 1 - i, 0)), acc((1, LANES)), acc((1, LANES)),
                   acc((1, LANES)), acc((1, QKV_COLS))],
        out_shape=[jax.ShapeDtypeStruct((s, QKV_COLS), BF16), jax.ShapeDtypeStruct((1, LANES), F32),
                   jax.ShapeDtypeStruct((1, LANES), F32), jax.ShapeDtypeStruct((1, LANES), F32),
                   jax.ShapeDtypeStruct((1, QKV_COLS), F32)],
        scratch_shapes=[pltpu.VMEM((N_KV_HEADS, tq + ATT_BLOCK, LANES), F32), pltpu.VMEM((N_KV_HEADS, tq + ATT_BLOCK, LANES), F32),
                        pltpu.VMEM((N_KV_HEADS, ATT_BLOCK, LANES), F32), pltpu.VMEM((N_KV_HEADS, ATT_BLOCK, LANES), F32)],
        compiler_params=_cparams("arbitrary"), name="attn_bwd")(qkv, qkv, qkv, dmixed, gq2, gk2, sinks)


def _in_bwd(dqkv, dcin, w_qkv, w_cin, x, dx2, g_mix):
    s = x.shape[0]
    tm = _tile(s)

    def body(dq_ref, dc_ref, wq_ref, wc_ref, x_ref, dx2_ref, g_ref, gx_ref, dg_ref):
        @pl.when(pl.program_id(0) == 0)
        def _():
            dg_ref[...] = jnp.zeros(dg_ref.shape, F32)

        dh = _dot_nt(dq_ref[...], wq_ref[...]) + _dot_nt(dc_ref[...], wc_ref[...])
        xv = x_ref[...]
        r = lax.rsqrt(jnp.mean(xv * xv, axis=-1, keepdims=True) + EPS)
        n = xv * r
        dg_ref[...] += jnp.sum(dh * n, axis=0, keepdims=True)
        dn = dh * g_ref[...]
        gx_ref[...] = dx2_ref[...] + r * (dn - n * jnp.mean(dn * n, axis=-1, keepdims=True))

    return pl.pallas_call(
        body, grid=(s // tm,),
        in_specs=[pl.BlockSpec((tm, QKV_COLS), lambda i: (i, 0)), pl.BlockSpec((tm, CIN_COLS), lambda i: (i, 0)),
                  _resident((D_MODEL, QKV_COLS)), _resident((D_MODEL, CIN_COLS)),
                  pl.BlockSpec((tm, D_MODEL), lambda i: (i, 0)), pl.BlockSpec((tm, D_MODEL), lambda i: (i, 0)),
                  _resident((1, D_MODEL))],
        out_specs=[pl.BlockSpec((tm, D_MODEL), lambda i: (i, 0)), pl.BlockSpec((1, D_MODEL), lambda i: (0, 0))],
        out_shape=[jax.ShapeDtypeStruct((s, D_MODEL), F32), jax.ShapeDtypeStruct((1, D_MODEL), F32)],
        compiler_params=_cparams("arbitrary"), name="in_bwd")(dqkv, dcin, w_qkv, w_cin, x, dx2, g_mix)


def _tn_matmul(a, b, name):
    ga, s, m = a.shape
    gb, _, n = b.shape
    g = max(ga, gb)
    tk = min(TN_TOKENS, s)

    def body(a_ref, b_ref, o_ref):
        @pl.when(pl.program_id(1) == 0)
        def _():
            o_ref[...] = jnp.zeros(o_ref.shape, F32)

        o_ref[0] += _dot_tn(a_ref[0].astype(BF16), b_ref[0].astype(BF16))

    return pl.pallas_call(
        body, grid=(g, s // tk),
        in_specs=[pl.BlockSpec((1, tk, m), (lambda gi, k: (gi, k, 0)) if ga > 1 else (lambda gi, k: (0, k, 0))),
                  pl.BlockSpec((1, tk, n), (lambda gi, k: (gi, k, 0)) if gb > 1 else (lambda gi, k: (0, k, 0)))],
        out_specs=pl.BlockSpec((1, m, n), lambda gi, k: (gi, 0, 0)),
        out_shape=jax.ShapeDtypeStruct((g, m, n), F32),
        compiler_params=_cparams("parallel", "arbitrary"), name=name)(a, b)


def _position():
    return lax.axis_index("x"), lax.axis_index("y"), lax.axis_index("c")


def _dev_index(px, py, pc):
    return 4 * px + 2 * py + pc


def _flip(v, bit):
    return 1 - v if bit else v


OTHER_CHIPS = ((1, 0), (0, 1), (1, 1))


def _allgather(shards, dtypes):
    n = len(shards)
    n_copies = 1 + 2 * len(OTHER_CHIPS)

    def body(*refs):
        ins, outs = refs[:n], refs[n:2 * n]
        send_sems, recv_sems = refs[2 * n:]
        x, y, c = _position()
        me, sibling = (x, y, c), (x, y, 1 - c)
        chips = [(_flip(x, fx), _flip(y, fy)) for fx, fy in OTHER_CHIPS]
        for a in range(n):
            outs[a][_dev_index(*me)] = ins[a][...].astype(dtypes[a])

        def copy(a, k, block, to):
            rows = outs[a].at[_dev_index(*block)]
            return pltpu.make_async_remote_copy(src_ref=rows, dst_ref=rows, send_sem=send_sems.at[a, k],
                                                recv_sem=recv_sems.at[a, k], device_id=to, device_id_type=MESH)

        started = []
        for a in range(n):
            for j, chip in enumerate(chips):
                started.append(copy(a, 1 + j, me, (*chip, c)))
            started.append(copy(a, 0, me, sibling))
        for cp in started:
            cp.start()
        for a in range(n):
            for j, chip in enumerate(chips):
                copy(a, 1 + j, (*chip, c), me).wait_recv()
                fwd = copy(a, 1 + len(chips) + j, (*chip, c), sibling)
                fwd.start()
                started.append(fwd)
        for a in range(n):
            copy(a, 0, sibling, me).wait_recv()
            for j, chip in enumerate(chips):
                copy(a, 1 + len(chips) + j, (*chip, 1 - c), me).wait_recv()
        for cp in started:
            cp.wait_send()

    vmem = pl.BlockSpec(memory_space=pltpu.VMEM)
    return pl.pallas_call(
        body, in_specs=[vmem] * n, out_specs=[vmem] * n,
        out_shape=[jax.ShapeDtypeStruct((N_DEV,) + w.shape, dt) for w, dt in zip(shards, dtypes)],
        scratch_shapes=[pltpu.SemaphoreType.DMA((n, n_copies)), pltpu.SemaphoreType.DMA((n, n_copies))],
        compiler_params=pltpu.CompilerParams(vmem_limit_bytes=VMEM_LIMIT), name="allgather_weights")(*shards)


def _reduce_scatter(g8, name):
    _, r, c_ = g8.shape
    rel = ((0, 0),) + OTHER_CHIPS

    def body(g_ref, out_ref, stage, send_a, recv_a, send_b, recv_b, sa_send, sa_recv, sb_send, sb_recv):
        x, y, c = _position()
        sibling = (x, y, 1 - c)
        chips = [(_flip(x, fx), _flip(y, fy)) for fx, fy in rel]

        def copy_a(j):
            return pltpu.make_async_remote_copy(src_ref=send_a.at[j], dst_ref=recv_a.at[j], send_sem=sa_send.at[j],
                                                recv_sem=sa_recv.at[j], device_id=sibling, device_id_type=MESH)

        def copy_b(j):
            return pltpu.make_async_remote_copy(src_ref=send_b.at[j], dst_ref=recv_b.at[j], send_sem=sb_send.at[j],
                                                recv_sem=sb_recv.at[j], device_id=(*chips[1 + j], c), device_id_type=MESH)

        for j, chip in enumerate(chips):
            pltpu.sync_copy(g_ref.at[_dev_index(*chip, 1 - c)], stage)
            send_a[j] = stage[...].astype(BF16)
            copy_a(j).start()
        for j, chip in enumerate(chips):
            pltpu.sync_copy(g_ref.at[_dev_index(*chip, c)], stage)
            copy_a(j).wait_recv()
            part = stage[...] + recv_a[j].astype(F32)
            if j == 0:
                out_ref[...] = part
            else:
                send_b[j - 1] = part.astype(BF16)
                copy_b(j - 1).start()
        for j in range(len(OTHER_CHIPS)):
            copy_b(j).wait_recv()
            out_ref[...] += recv_b[j].astype(F32)
        for j in range(len(rel)):
            copy_a(j).wait_send()
        for j in range(len(OTHER_CHIPS)):
            copy_b(j).wait_send()

    na, nb = len(rel), len(OTHER_CHIPS)
    return pl.pallas_call(
        body, in_specs=[pl.BlockSpec(memory_space=pl.ANY)], out_specs=pl.BlockSpec(memory_space=pltpu.VMEM),
        out_shape=jax.ShapeDtypeStruct((r, c_), F32),
        scratch_shapes=[pltpu.VMEM((r, c_), F32), pltpu.VMEM((na, r, c_), BF16), pltpu.VMEM((na, r, c_), BF16),
                        pltpu.VMEM((nb, r, c_), BF16), pltpu.VMEM((nb, r, c_), BF16),
                        pltpu.SemaphoreType.DMA((na,)), pltpu.SemaphoreType.DMA((na,)),
                        pltpu.SemaphoreType.DMA((nb,)), pltpu.SemaphoreType.DMA((nb,))],
        compiler_params=pltpu.CompilerParams(vmem_limit_bytes=VMEM_LIMIT), name=name)(g8)


def _small_exchange(v, reduce, name):
    rows = v.shape[0]

    def body(v_ref, out_ref, *scratch):
        gath = scratch[0] if reduce else out_ref
        send_sems, recv_sems = scratch[-2:]
        x, y, c = _position()
        me = _dev_index(x, y, c)
        gath[me] = v_ref[...]
        peers = [(_flip(x, k >> 2 & 1), _flip(y, k >> 1 & 1), _flip(c, k & 1)) for k in range(1, N_DEV)]

        def copy(k, block):
            return pltpu.make_async_remote_copy(src_ref=gath.at[block], dst_ref=gath.at[block], send_sem=send_sems.at[k],
                                                recv_sem=recv_sems.at[k], device_id=peers[k], device_id_type=MESH)

        for k in range(N_DEV - 1):
            copy(k, me).start()
        for k in range(N_DEV - 1):
            copy(k, _dev_index(*peers[k])).wait_recv()
        for k in range(N_DEV - 1):
            copy(k, me).wait_send()
        if reduce:
            total = gath[0]
            for d in range(1, N_DEV):
                total = total + gath[d]
            out_ref[...] = total

    vmem = pl.BlockSpec(memory_space=pltpu.VMEM)
    out_shape = (rows, LANES) if reduce else (N_DEV, rows, LANES)
    return pl.pallas_call(
        body, in_specs=[vmem], out_specs=vmem, out_shape=jax.ShapeDtypeStruct(out_shape, F32),
        scratch_shapes=([pltpu.VMEM((N_DEV, rows, LANES), F32)] if reduce else [])
        + [pltpu.SemaphoreType.DMA((N_DEV - 1,)), pltpu.SemaphoreType.DMA((N_DEV - 1,))],
        compiler_params=pltpu.CompilerParams(vmem_limit_bytes=VMEM_LIMIT), name=name)(v)


def _row_tile(r):
    for n in (8, 4, 2):
        if r % (n * SUBLANES) == 0:
            return r // n
    return r


def _adam_math(wv, gv, mv, vv):
    mn = ADAM_B1 * mv + (1.0 - ADAM_B1) * gv
    vn = ADAM_B2 * vv + (1.0 - ADAM_B2) * (gv * gv)
    m_hat = mn / (1.0 - ADAM_B1 ** ADAM_STEP)
    v_hat = vn / (1.0 - ADAM_B2 ** ADAM_STEP)
    return -ADAM_LR * (m_hat / (jnp.sqrt(v_hat) + ADAM_EPS) + ADAM_WD * wv), mn, vn


def _adamw(w, g, m, v, name):
    r, c_ = w.shape
    tr = _row_tile(r)

    def body(w_ref, g_ref, m_ref, v_ref, d_ref, mo_ref, vo_ref):
        d_ref[...], mo_ref[...], vo_ref[...] = _adam_math(w_ref[...], g_ref[...], m_ref[...], v_ref[...])

    spec = pl.BlockSpec((tr, c_), lambda i: (i, 0))
    return pl.pallas_call(
        body, grid=(r // tr,), in_specs=[spec] * 4, out_specs=[spec] * 3,
        out_shape=[jax.ShapeDtypeStruct((r, c_), F32)] * 3,
        compiler_params=_cparams("parallel"), name=name)(w, g, m, v)


FW_ROWS = 24
CW_ROWS = 32
R_FW = 0
R_FB = R_FW + N_DEV * FW_ROWS
R_CW = R_FB + 48
R_BQKV = R_CW + (CONV_WIDTH // LANES) * CW_ROWS
R_BCIN = R_BQKV + 8
R_GMIX = R_BCIN + 8
R_BOUT = R_GMIX + 8
R_GFFN = R_BOUT + 8
R_CB = R_GFFN + 8
R_CGAIN = R_CB + 8
R_CBIAS = R_CGAIN + 8
R_QKS = R_CBIAS + 8
SMALL_ROWS = R_QKS + 8


def _pack_small(raw):
    def rows(a, n):
        a = a.reshape(-1, LANES)
        return jnp.pad(a, ((0, n - a.shape[0]), (0, 0)))

    fw = jnp.pad(raw["dfw"].reshape(N_DEV, -1, LANES), ((0, 0), (0, FW_ROWS - 3 * FF_LANE_CHUNKS), (0, 0)))
    cw = jnp.pad(raw["dcw"].reshape(CONV_KERNEL, -1, LANES).transpose(1, 0, 2), ((0, 0), (0, CW_ROWS - CONV_KERNEL), (0, 0)))
    qks = jnp.concatenate([raw["dgq"], raw["dgk"], raw["dsink"]], axis=0)
    return jnp.concatenate([
        fw.reshape(-1, LANES), rows(raw["dfb"][:, 0, :FF_CHUNK], 48), cw.reshape(-1, LANES), rows(raw["dbqkv"], 8),
        rows(raw["dbcin"], 8), rows(raw["dg_mix"], 8), rows(raw["db_out"], 8), rows(raw["dg_ffn"], 8), rows(raw["dcb"], 8),
        rows(raw["dcgain"], 8), rows(raw["dcbias"], 8), rows(qks, 8)], axis=0)


def _adamw_small(gpack, w, m, v):
    n = len(SMALL)
    ix = {name: i for i, name in enumerate(SMALL)}

    def body(g_ref, *refs):
        w_refs, m_refs, v_refs, outs = refs[:n], refs[n:2 * n], refs[2 * n:3 * n], refs[3 * n:]
        d = _dev_index(*_position())

        def step(name, idx, gv):
            i = ix[name]
            delta, mn, vn = _adam_math(w_refs[i][idx], gv, m_refs[i][idx], v_refs[i][idx])
            for ref, val in zip(outs[4 * i:4 * i + 4], (gv, delta, mn, vn)):
                ref[idx] = val

        def whole(name, row, nrows):
            step(name, (slice(None), slice(None)), g_ref[row:row + nrows, :])

        whole("mix_norm_gain", R_GMIX, 8)
        whole("b_out", R_BOUT, 8)
        whole("ffn_norm_gain", R_GFFN, 8)
        whole("conv_dw_b", R_CB, 4)
        whole("conv_norm_gain", R_CGAIN, 4)
        whole("conv_norm_bias", R_CBIAS, 4)
        whole("ffn_dw_b", R_FB, 2 * D_FF // LANES)
        nq = QKV_COLS // LANES
        step("b_in", (slice(0, nq), slice(None)), g_ref[R_BQKV:R_BQKV + nq, :])
        step("b_in", (slice(nq, nq + CIN_COLS // LANES), slice(None)), g_ref[R_BCIN:R_BCIN + CIN_COLS // LANES, :])
        step("q_norm_gain", (slice(None), slice(None)), g_ref[R_QKS:R_QKS + 1, 0:HEAD_DIM])
        step("k_norm_gain", (slice(None), slice(None)), g_ref[R_QKS + 1:R_QKS + 2, 0:HEAD_DIM])
        step("attn_sinks", (slice(None), slice(None)), g_ref[R_QKS + 2:R_QKS + 3, 0:N_Q_HEADS])
        blk = g_ref[pl.ds(pl.multiple_of(R_CW + CW_ROWS * lax.shift_right_logical(d, 1), SUBLANES), CW_ROWS), :]
        blk = jnp.where((d & 1) == 1, pltpu.roll(blk, HEAD_DIM, 1), blk)
        step("conv_dw_w", (slice(None), slice(None)), blk[0:CONV_KERNEL, 0:CONV_WIDTH // N_DEV])
        blk = g_ref[pl.ds(pl.multiple_of(R_FW + FW_ROWS * d, SUBLANES), FW_ROWS), :]
        for k in range(3):
            for j in range(FF_LANE_CHUNKS):
                wd = min(LANES, FF_CHUNK - j * LANES)
                row = k * FF_LANE_CHUNKS + j
                step("ffn_dw_w", (slice(k, k + 1), slice(j * LANES, j * LANES + wd)), blk[row:row + 1, 0:wd])

    vmem = pl.BlockSpec(memory_space=pltpu.VMEM)
    args = [gpack] + [d[name] for d in (w, m, v) for name in SMALL]
    outs = pl.pallas_call(
        body, in_specs=[vmem] * len(args), out_specs=[vmem] * (4 * n),
        out_shape=[jax.ShapeDtypeStruct(w[name].shape, F32) for name in SMALL for _ in range(4)],
        compiler_params=pltpu.CompilerParams(vmem_limit_bytes=VMEM_LIMIT), name="adamw_small")(*args)
    return {name: outs[4 * i:4 * i + 4] for i, name in enumerate(SMALL)}


def _local_step(x, target, p):
    s = x.shape[0]
    qkv, cin, h1 = _mix_proj(x, p["g_mix"], p["w_qkv"], p["w_cin"], p["b_qkv"], p["b_cin"])
    attn = _attn_fwd(qkv, p["gq2"], p["gk2"], p["sinks"])
    cw8 = p["cw8"]
    c3, c1 = _conv_fwd(cin, cw8, p["cb"], p["cgain"], p["cbias"])
    x2, h2 = _out_proj(x, attn, c3, p["w_out"], p["w_out"], p["b_out"], p["g_ffn"])
    fw, fb = p["fw"], p["fb"]
    up0, act, dy, loss = _ffn_fwd(h2, x2, target, p["w_up"], fw, fb, p["w_down"])
    dup0, dh2, dfw, dfb = _ffn_bwd(dy, up0, p["w_up"], fw, fb, p["w_down"])
    dx2, dmixed, dg_ffn, db_out = _ffn_norm_bwd(dh2, dy, x2, p["g_ffn"], p["w_out"])
    dcin, dcw, dcb, dcgain, dcbias, dbcin = _conv_bwd(dmixed, c1, cin, cw8, p["cgain"], p["cbias"])
    dqkv, dgq, dgk, dsink, dbqkv = _attn_bwd(qkv, dmixed, p["gq2"], p["gk2"], p["sinks"])
    grad_x, dg_mix = _in_bwd(dqkv, dcin, p["w_qkv"], p["w_cin"], x, dx2, p["g_mix"])
    big = {
        "w_up": _tn_matmul(h2[None], dup0.reshape(N_DEV, s, FF_CHUNK), "dw_up"),
        "w_down": _tn_matmul(act, dy[None], "dw_down"),
        "w_qkv": _tn_matmul(h1[None], dqkv[None], "dw_qkv")[0],
        "w_cin": _tn_matmul(h1[None], dcin[None], "dw_cin")[0],
        "wo_attn": _tn_matmul(attn[None], dx2[None], "dw_out_attn")[0],
        "wo_conv": _tn_matmul(c3[None], dx2[None], "dw_out_conv")[0],
    }
    small = dict(dg_mix=dg_mix, dbqkv=dbqkv, dbcin=dbcin, dgq=dgq, dgk=dgk, dsink=dsink, dcw=dcw, dcb=dcb, dcgain=dcgain,
                 dcbias=dcbias, db_out=db_out, dg_ffn=dg_ffn, dfw=dfw, dfb=dfb)
    return loss, grad_x, big, small


BIG = ("w_in", "w_out", "w_up", "w_down")
SMALL = ("mix_norm_gain", "b_in", "q_norm_gain", "k_norm_gain", "attn_sinks", "conv_dw_w", "conv_dw_b",
         "conv_norm_gain", "conv_norm_bias", "b_out", "ffn_norm_gain", "ffn_dw_w", "ffn_dw_b")
ORDER = ("mix_norm_gain", "w_in", "b_in", "q_norm_gain", "k_norm_gain", "attn_sinks", "conv_dw_w", "conv_dw_b",
         "conv_norm_gain", "conv_norm_bias", "w_out", "b_out", "ffn_norm_gain", "w_up", "ffn_dw_w", "ffn_dw_b", "w_down")


def kernel(x, mix_norm_gain, w_in, b_in, q_norm_gain, k_norm_gain, attn_sinks, conv_dw_w, conv_dw_b, conv_norm_gain, conv_norm_bias, w_out, b_out, ffn_norm_gain, w_up, ffn_dw_w, ffn_dw_b, w_down, loss_target, m_mix_norm_gain, m_w_in, m_b_in, m_q_norm_gain, m_k_norm_gain, m_attn_sinks, m_conv_dw_w, m_conv_dw_b, m_conv_norm_gain, m_conv_norm_bias, m_w_out, m_b_out, m_ffn_norm_gain, m_w_up, m_ffn_dw_w, m_ffn_dw_b, m_w_down, v_mix_norm_gain, v_w_in, v_b_in, v_q_norm_gain, v_k_norm_gain, v_attn_sinks, v_conv_dw_w, v_conv_dw_b, v_conv_norm_gain, v_conv_norm_bias, v_w_out, v_b_out, v_ffn_norm_gain, v_w_up, v_ffn_dw_w, v_ffn_dw_b, v_w_down):
    w = dict(mix_norm_gain=mix_norm_gain, w_in=w_in, b_in=b_in, q_norm_gain=q_norm_gain, k_norm_gain=k_norm_gain,
             attn_sinks=attn_sinks, conv_dw_w=conv_dw_w, conv_dw_b=conv_dw_b, conv_norm_gain=conv_norm_gain,
             conv_norm_bias=conv_norm_bias, w_out=w_out, b_out=b_out, ffn_norm_gain=ffn_norm_gain, w_up=w_up,
             ffn_dw_w=ffn_dw_w, ffn_dw_b=ffn_dw_b, w_down=w_down)
    m = dict(mix_norm_gain=m_mix_norm_gain, w_in=m_w_in, b_in=m_b_in, q_norm_gain=m_q_norm_gain, k_norm_gain=m_k_norm_gain,
             attn_sinks=m_attn_sinks, conv_dw_w=m_conv_dw_w, conv_dw_b=m_conv_dw_b, conv_norm_gain=m_conv_norm_gain,
             conv_norm_bias=m_conv_norm_bias, w_out=m_w_out, b_out=m_b_out, ffn_norm_gain=m_ffn_norm_gain, w_up=m_w_up,
             ffn_dw_w=m_ffn_dw_w, ffn_dw_b=m_ffn_dw_b, w_down=m_w_down)
    v = dict(mix_norm_gain=v_mix_norm_gain, w_in=v_w_in, b_in=v_b_in, q_norm_gain=v_q_norm_gain, k_norm_gain=v_k_norm_gain,
             attn_sinks=v_attn_sinks, conv_dw_w=v_conv_dw_w, conv_dw_b=v_conv_dw_b, conv_norm_gain=v_conv_norm_gain,
             conv_norm_bias=v_conv_norm_bias, w_out=v_w_out, b_out=v_b_out, ffn_norm_gain=v_ffn_norm_gain, w_up=v_w_up,
             ffn_dw_w=v_ffn_dw_w, ffn_dw_b=v_ffn_dw_b, w_down=v_w_down)
    s = x.shape[1]

    wi8, wo8, wu8, wd8, cw8, fw8 = _allgather([w_in, w_out, w_up, w_down, conv_dw_w, ffn_dw_w], [BF16] * 4 + [F32] * 2)
    w_in_full = wi8.transpose(1, 0, 2).reshape(D_MODEL, QKV_COLS + CIN_COLS)
    lane_pad = ((0, 0), (0, 0), (0, FF_PADDED - FF_CHUNK))
    p = {
        "g_mix": mix_norm_gain.reshape(1, -1), "w_qkv": w_in_full[:, :QKV_COLS], "w_cin": w_in_full[:, QKV_COLS:],
        "b_qkv": b_in[:QKV_COLS].reshape(1, -1), "b_cin": b_in[QKV_COLS:].reshape(1, -1),
        "gq2": jnp.tile(q_norm_gain, 2).reshape(1, -1), "gk2": jnp.tile(k_norm_gain, 2).reshape(1, -1), "sinks": attn_sinks,
        "cw8": jnp.repeat(cw8.transpose(1, 0, 2).reshape(CONV_KERNEL, CONV_WIDTH), SUBLANES, axis=0),
        "cb": conv_dw_b.reshape(1, -1), "cgain": conv_norm_gain.reshape(1, -1), "cbias": conv_norm_bias.reshape(1, -1),
        "w_out": wo8.reshape(D_MODEL, D_MODEL), "b_out": b_out.reshape(1, -1), "g_ffn": ffn_norm_gain.reshape(1, -1),
        "w_up": wu8, "fw": jnp.pad(fw8, lane_pad), "fb": jnp.pad(ffn_dw_b.reshape(N_DEV, 1, FF_CHUNK), lane_pad),
        "w_down": wd8.reshape(N_FF_PAIRS, FF_CHUNK, D_MODEL),
    }

    loss, grad_x, big, small = _local_step(x[0], loss_target[0], p)

    dw_in = jnp.concatenate([big["w_qkv"], big["w_cin"]], axis=1)
    dw_out = jnp.concatenate([big["wo_attn"], big["wo_conv"]], axis=0)
    g = {
        "w_in": _reduce_scatter(dw_in.reshape(D_MODEL, N_DEV, -1).transpose(1, 0, 2), "rs_w_in"),
        "w_out": _reduce_scatter(dw_out.reshape(N_DEV, -1, D_MODEL), "rs_w_out"),
        "w_up": _reduce_scatter(big["w_up"], "rs_w_up"),
        "w_down": _reduce_scatter(big["w_down"].reshape(N_DEV, -1, D_MODEL), "rs_w_down"),
    }
    gpack = _small_exchange(_pack_small(small), True, "allreduce_small")

    delta, new_m, new_v = {}, {}, {}
    for n in BIG:
        delta[n], new_m[n], new_v[n] = _adamw(w[n], g[n], m[n], v[n], "adamw_" + n)

    def view(a):
        return a if a.ndim == 2 else (a.reshape(-1, LANES) if a.size % LANES == 0 else a.reshape(1, -1))

    small_out = _adamw_small(gpack, *[{n: view(d[n]) for n in SMALL} for d in (w, m, v)])
    for n in SMALL:
        g[n], delta[n], new_m[n], new_v[n] = [a.reshape(w[n].shape) for a in small_out[n]]

    total = lax.psum(loss[0, 0], ("x", "y", "c"))
    return (total, grad_x.reshape(1, s, D_MODEL), *[g[n] for n in ORDER], *[delta[n] for n in ORDER],
            *[new_m[n] for n in ORDER], *[new_v[n] for n in ORDER])
```

```python
import math

import jax
import jax.numpy as jnp
from jax import lax
from jax.experimental import pallas as pl
from jax.experimental.pallas import tpu as pltpu

F32 = jnp.float32
BF16 = jnp.bfloat16

D_MODEL = 1024
HEAD_DIM = 64
N_Q_HEADS = 8
N_KV_HEADS = 2
Q_COLS = 512
KV_COLS = 128
QKV_COLS = Q_COLS + 2 * KV_COLS
CONV_WIDTH = 512
CIN_COLS = 2 * CONV_WIDTH
CONV_KERNEL = 31
CONV_HALO = 32
D_FF = 2816
N_DEV = 8
FF_CHUNK = 2 * D_FF // N_DEV
N_FF_PAIRS = N_DEV // 2
ATT_BLOCK = 128
EPS = 1e-6
NEG_INF = -1e30
SLOPES = [float(2.0 ** (-8.0 * (h + 1.0) / N_Q_HEADS)) for h in range(N_Q_HEADS)]

ADAM_LR = 0.001
ADAM_B1 = 0.9
ADAM_B2 = 0.999
ADAM_EPS = 1e-08
ADAM_WD = 0.01
ADAM_STEP = 10

LANES = 128
SUBLANES = 8
VMEM_LIMIT = 56 * 1024 * 1024
MESH = pl.DeviceIdType.MESH


def _cparams(*sem, **kw):
    return pltpu.CompilerParams(dimension_semantics=sem or None, vmem_limit_bytes=VMEM_LIMIT, **kw)


def _resident(shape):
    nd = len(shape)
    return pl.BlockSpec(shape, lambda *_: (0,) * nd, pipeline_mode=pl.Buffered(1))


def _dot(a, b):
    return jnp.dot(a, b, preferred_element_type=F32)


def _dot_nt(a, b):
    return lax.dot_general(a, b, (((1,), (1,)), ((), ())), preferred_element_type=F32)


def _dot_tn(a, b):
    return lax.dot_general(a, b, (((0,), (0,)), ((), ())), preferred_element_type=F32)


def _sigmoid(x):
    return 1.0 / (1.0 + jnp.exp(-x))


def _lo_mask(shape):
    return lax.broadcasted_iota(jnp.int32, shape, len(shape) - 1) % LANES < HEAD_DIM


def _half_sums(t, lo):
    s_lo = jnp.sum(jnp.where(lo, t, 0.0), axis=-1, keepdims=True)
    s_hi = jnp.sum(jnp.where(lo, 0.0, t), axis=-1, keepdims=True)
    return jnp.where(lo, s_lo, s_hi)


def _head_norm(t, lo):
    r = lax.rsqrt(_half_sums(t * t, lo) * (1.0 / HEAD_DIM) + EPS)
    return t * r, r


def _head_norm_bwd(dn, n, r, lo):
    return r * (dn - n * (_half_sums(dn * n, lo) * (1.0 / HEAD_DIM)))


def _tile(s):
    return min(512, s)


TN_TOKENS = 2048
FF_COLS = ((0, 256), (256, 512), (512, 704))


def _position():
    return lax.axis_index("x"), lax.axis_index("y"), lax.axis_index("c")


def _dev_index(px, py, pc):
    return 4 * px + 2 * py + pc


def _flip(v, bit):
    return 1 - v if bit else v


OTHER_CHIPS = ((1, 0), (0, 1), (1, 1))
N_GATHER_COPIES = 1 + 2 * len(OTHER_CHIPS)


class _Gather:
    def __init__(self, shard_ref, out_ref, cast_buf, send_sems, recv_sems, local_sem):
        self.shard, self.out, self.buf = shard_ref, out_ref, cast_buf
        self.send_sems, self.recv_sems, self.local_sem = send_sems, recv_sems, local_sem
        x, y, c = _position()
        self.c = c
        self.me, self.sibling = (x, y, c), (x, y, 1 - c)
        self.chips = [(_flip(x, fx), _flip(y, fy)) for fx, fy in OTHER_CHIPS]

    def _copy(self, k, block, to, from_buf=False):
        rows = self.out.at[_dev_index(*block)]
        return pltpu.make_async_remote_copy(src_ref=self.buf if from_buf else rows, dst_ref=rows,
                                            send_sem=self.send_sems.at[k], recv_sem=self.recv_sems.at[k],
                                            device_id=to, device_id_type=MESH)

    def _local(self):
        return pltpu.make_async_copy(self.buf, self.out.at[_dev_index(*self.me)], self.local_sem)

    def start(self):
        self.buf[...] = self.shard[...].astype(self.buf.dtype)
        self._local().start()
        for j, chip in enumerate(self.chips):
            self._copy(1 + j, self.me, (*chip, self.c), from_buf=True).start()
        self._copy(0, self.me, self.sibling, from_buf=True).start()

    def forward(self):
        for j, chip in enumerate(self.chips):
            self._copy(1 + j, (*chip, self.c), self.me).wait_recv()
            self._copy(1 + len(self.chips) + j, (*chip, self.c), self.sibling).start()

    def finish(self):
        self._copy(0, self.sibling, self.me).wait_recv()
        for j, chip in enumerate(self.chips):
            self._copy(1 + len(self.chips) + j, (*chip, 1 - self.c), self.me).wait_recv()
        for k in range(N_GATHER_COPIES):
            self._copy(k, self.me, self.sibling).wait_send()
        self._local().wait()


def _gather_specs(shards):
    whole = [pl.BlockSpec(w.shape, lambda *_, nd=w.ndim: (0,) * nd, pipeline_mode=pl.Buffered(1)) for w in shards]
    outs = [pl.BlockSpec(memory_space=pl.ANY) for _ in shards]
    shapes = [jax.ShapeDtypeStruct((N_DEV,) + w.shape, BF16) for w in shards]
    scratch = []
    for w in shards:
        scratch += [pltpu.VMEM(w.shape, BF16), pltpu.SemaphoreType.DMA((N_GATHER_COPIES,)),
                    pltpu.SemaphoreType.DMA((N_GATHER_COPIES,)), pltpu.SemaphoreType.DMA(())]
    return whole, outs, shapes, scratch


def _run_gathers(gathers, step, n_steps):
    @pl.when(step == 0)
    def _():
        for g in gathers:
            g.start()

    @pl.when(step == n_steps // 2)
    def _():
        for g in gathers:
            g.forward()

    @pl.when(step == n_steps - 1)
    def _():
        for g in gathers:
            g.finish()


class _ReduceScatter:
    def __init__(self, g_ref, out_ref, stage, send_a, recv_a, send_b, recv_b, sa_send, sa_recv, sb_send, sb_recv):
        self.g, self.out, self.stage = g_ref, out_ref, stage
        self.send_a, self.recv_a, self.send_b, self.recv_b = send_a, recv_a, send_b, recv_b
        self.sems = (sa_send, sa_recv, sb_send, sb_recv)
        x, y, c = _position()
        self.c, self.sibling = c, (x, y, 1 - c)
        self.chips = [(x, y)] + [(_flip(x, fx), _flip(y, fy)) for fx, fy in OTHER_CHIPS]

    def _copy_a(self, j):
        return pltpu.make_async_remote_copy(src_ref=self.send_a.at[j], dst_ref=self.recv_a.at[j], send_sem=self.sems[0].at[j],
                                            recv_sem=self.sems[1].at[j], device_id=self.sibling, device_id_type=MESH)

    def _copy_b(self, j):
        return pltpu.make_async_remote_copy(src_ref=self.send_b.at[j], dst_ref=self.recv_b.at[j], send_sem=self.sems[2].at[j],
                                            recv_sem=self.sems[3].at[j], device_id=(*self.chips[1 + j], self.c),
                                            device_id_type=MESH)

    def start(self):
        for j, chip in enumerate(self.chips):
            pltpu.sync_copy(self.g.at[_dev_index(*chip, 1 - self.c)], self.stage)
            self.send_a[j] = self.stage[...].astype(BF16)
            self._copy_a(j).start()

    def middle(self):
        for j, chip in enumerate(self.chips):
            pltpu.sync_copy(self.g.at[_dev_index(*chip, self.c)], self.stage)
            self._copy_a(j).wait_recv()
            part = self.stage[...] + self.recv_a[j].astype(F32)
            if j == 0:
                self.out[...] = part
            else:
                self.send_b[j - 1] = part.astype(BF16)
                self._copy_b(j - 1).start()

    def finish(self):
        for j in range(len(OTHER_CHIPS)):
            self._copy_b(j).wait_recv()
            self.out[...] += self.recv_b[j].astype(F32)
        for j in range(len(self.chips)):
            self._copy_a(j).wait_send()
        for j in range(len(OTHER_CHIPS)):
            self._copy_b(j).wait_send()


N_SCATTER_SCRATCH = 9


def _scatter_specs(g8s):
    na, nb = 1 + len(OTHER_CHIPS), len(OTHER_CHIPS)
    ins = [pl.BlockSpec(memory_space=pl.ANY) for _ in g8s]
    outs = [pl.BlockSpec(g.shape[1:], lambda *_: (0, 0)) for g in g8s]
    shapes = [jax.ShapeDtypeStruct(g.shape[1:], F32) for g in g8s]
    scratch = []
    for g in g8s:
        blk = g.shape[1:]
        scratch += [pltpu.VMEM(blk, F32), pltpu.VMEM((na,) + blk, BF16), pltpu.VMEM((na,) + blk, BF16),
                    pltpu.VMEM((nb,) + blk, BF16), pltpu.VMEM((nb,) + blk, BF16),
                    pltpu.SemaphoreType.DMA((na,)), pltpu.SemaphoreType.DMA((na,)),
                    pltpu.SemaphoreType.DMA((nb,)), pltpu.SemaphoreType.DMA((nb,))]
    return ins, outs, shapes, scratch


def _run_scatters(scatters, step, n_steps):
    @pl.when(step == 0)
    def _():
        for r in scatters:
            r.start()

    @pl.when(step == min(max(1, n_steps // 4), n_steps - 1))
    def _():
        for r in scatters:
            r.middle()

    @pl.when(step == n_steps - 1)
    def _():
        for r in scatters:
            r.finish()


def _mix_proj(x, g_mix, w_qkv, w_cin, b_qkv, b_cin):
    s = x.shape[0]
    tm = _tile(s)

    def body(x_ref, g_ref, wq_ref, wc_ref, bq_ref, bc_ref, qkv_ref, cin_ref, h1_ref):
        xv = x_ref[...]
        r = lax.rsqrt(jnp.mean(xv * xv, axis=-1, keepdims=True) + EPS)
        h = (xv * r * g_ref[...]).astype(BF16)
        h1_ref[...] = h
        qkv_ref[...] = _dot(h, wq_ref[...]) + bq_ref[...]
        cin_ref[...] = _dot(h, wc_ref[...]) + bc_ref[...]

    return pl.pallas_call(
        body, grid=(s // tm,),
        in_specs=[pl.BlockSpec((tm, D_MODEL), lambda i: (i, 0)), _resident((1, D_MODEL)),
                  _resident((D_MODEL, QKV_COLS)), _resident((D_MODEL, CIN_COLS)),
                  _resident((1, QKV_COLS)), _resident((1, CIN_COLS))],
        out_specs=[pl.BlockSpec((tm, QKV_COLS), lambda i: (i, 0)), pl.BlockSpec((tm, CIN_COLS), lambda i: (i, 0)),
                   pl.BlockSpec((tm, D_MODEL), lambda i: (i, 0))],
        out_shape=[jax.ShapeDtypeStruct((s, QKV_COLS), F32), jax.ShapeDtypeStruct((s, CIN_COLS), F32),
                   jax.ShapeDtypeStruct((s, D_MODEL), BF16)],
        compiler_params=_cparams("parallel"), name="mix_proj")(x, g_mix, w_qkv, w_cin, b_qkv, b_cin)


def _kv_variants(kv_all, gk2, lo):
    k_all = kv_all[:, :LANES]
    v_all = kv_all[:, LANES:]
    kn_pre, rk = _head_norm(k_all, lo)
    kn = kn_pre * gk2
    kr = pltpu.roll(kn, HEAD_DIM, 1)
    vr = pltpu.roll(v_all, HEAD_DIM, 1)
    zero = jnp.zeros_like(kn)
    k_lo = [jnp.where(lo, kn, zero).astype(BF16), jnp.where(lo, kr, zero).astype(BF16)]
    k_hi = [jnp.where(lo, zero, kr).astype(BF16), jnp.where(lo, zero, kn).astype(BF16)]
    v_lo = [jnp.where(lo, v_all, zero).astype(BF16), jnp.where(lo, vr, zero).astype(BF16)]
    v_hi = [jnp.where(lo, zero, vr).astype(BF16), jnp.where(lo, zero, v_all).astype(BF16)]
    return k_lo, k_hi, v_lo, v_hi, kn_pre, rk


def _att_consts(first_tile, b):
    rows = 2 * ATT_BLOCK
    qi = lax.broadcasted_iota(jnp.int32, (rows, 2 * ATT_BLOCK), 0) % ATT_BLOCK
    kj = lax.broadcasted_iota(jnp.int32, (rows, 2 * ATT_BLOCK), 1)
    rel = qi + ATT_BLOCK - kj
    valid = (rel >= 0) & (rel < ATT_BLOCK)
    if b == 0:
        valid = valid & ((kj >= ATT_BLOCK) | jnp.logical_not(first_tile))
    return rel.astype(F32), valid


def _row_const(va, vb):
    top = lax.broadcasted_iota(jnp.int32, (2 * ATT_BLOCK, 1), 0) < ATT_BLOCK
    return jnp.where(top, va, vb)


def _probs(q2, k_op, rel, valid, slope, sink):
    sc = _dot_nt(q2, k_op) * (1.0 / math.sqrt(HEAD_DIM)) - slope * rel
    sc = jnp.where(valid, sc, NEG_INF)
    m = jnp.maximum(jnp.max(sc, axis=-1, keepdims=True), sink)
    p = jnp.exp(sc - m)
    e_sink = jnp.exp(sink - m)
    inv = 1.0 / (jnp.sum(p, axis=-1, keepdims=True) + e_sink)
    return p * inv, e_sink * inv


def _attn_fwd(qkv, gq2, gk2, sinks, shards):
    s = qkv.shape[0]
    tq = _tile(s)
    nb = tq // ATT_BLOCK
    ng = len(shards)
    g_in, g_out, g_shape, g_scratch = _gather_specs(shards)

    def body(q_ref, kv_ref, kvp_ref, gq_ref, gk_ref, sink_ref, *rest):
        out_ref = rest[ng]
        i = pl.program_id(0)
        _run_gathers([_Gather(rest[a], rest[ng + 1 + a], *rest[2 * ng + 1 + 4 * a:2 * ng + 5 + 4 * a]) for a in range(ng)],
                     i, s // tq)
        lo = _lo_mask((1, LANES))
        kv_all = jnp.concatenate([kvp_ref[...], kv_ref[...]], axis=0)
        k_lo, k_hi, v_lo, v_hi, _, _ = _kv_variants(kv_all, gk_ref[...], lo)
        for b in range(nb):
            rel, valid = _att_consts(i == 0, b)
            rows = slice(b * ATT_BLOCK, (b + 1) * ATT_BLOCK)
            keys = slice(b * ATT_BLOCK, (b + 2) * ATT_BLOCK)
            for kvh in range(N_KV_HEADS):
                pairs = (2 * kvh, 2 * kvh + 1)
                q2 = jnp.concatenate([q_ref[rows, p * LANES:(p + 1) * LANES] for p in pairs], axis=0)
                qn, _ = _head_norm(q2, lo)
                q2 = (qn * gq_ref[...]).astype(BF16)
                out = None
                for odd, (k_op, v_op) in enumerate(((k_lo[kvh][keys], v_lo[kvh][keys]), (k_hi[kvh][keys], v_hi[kvh][keys]))):
                    ha, hb = 2 * pairs[0] + odd, 2 * pairs[1] + odd
                    p, _ = _probs(q2, k_op, rel, valid, _row_const(SLOPES[ha], SLOPES[hb]),
                                  _row_const(sink_ref[ha], sink_ref[hb]))
                    o = _dot(p.astype(BF16), v_op)
                    out = o if out is None else out + o
                for n, p in enumerate(pairs):
                    out_ref[rows, p * LANES:(p + 1) * LANES] = out[n * ATT_BLOCK:(n + 1) * ATT_BLOCK].astype(BF16)

    return pl.pallas_call(
        body, grid=(s // tq,),
        in_specs=[pl.BlockSpec((tq, Q_COLS), lambda i: (i, 0)),
                  pl.BlockSpec((tq, 2 * KV_COLS), lambda i: (i, 2)),
                  pl.BlockSpec((ATT_BLOCK, 2 * KV_COLS), lambda i: (jnp.maximum(i * nb - 1, 0), 2)),
                  _resident((1, LANES)), _resident((1, LANES)),
                  pl.BlockSpec(memory_space=pltpu.SMEM)] + g_in,
        out_specs=[pl.BlockSpec((tq, Q_COLS), lambda i: (i, 0))] + g_out,
        out_shape=[jax.ShapeDtypeStruct((s, Q_COLS), BF16)] + g_shape,
        scratch_shapes=g_scratch,
        compiler_params=_cparams("arbitrary"), name="attn_fwd")(qkv, qkv, qkv, gq2, gk2, sinks, *shards)


def _group_stats(c1, lo):
    mu = _half_sums(c1, lo) * (1.0 / HEAD_DIM)
    d = c1 - mu
    rstd = lax.rsqrt(_half_sums(d * d, lo) * (1.0 / HEAD_DIM) + EPS)
    return d * rstd, rstd


def _rows(ref, first_row, n):
    return ref[pl.ds(first_row, n, stride=1), :].reshape(n // SUBLANES, SUBLANES, LANES)


def _conv_fwd(cin, cw8, cb, gain, bias, shards):
    s = cin.shape[0]
    tm = _tile(s)
    rc = 64
    nchunk = CONV_WIDTH // LANES
    lead = CONV_HALO - (CONV_KERNEL - 1)
    ng = len(shards)
    g_in, g_out, g_shape, g_scratch = _gather_specs(shards)

    def body(cin_ref, cw_ref, cb_ref, gain_ref, bias_ref, *rest):
        c3_ref, c1_ref, ext_ref = rest[ng], rest[ng + 1], rest[2 * ng + 2]
        _run_gathers([_Gather(rest[a], rest[ng + 2 + a], *rest[2 * ng + 3 + 4 * a:2 * ng + 7 + 4 * a]) for a in range(ng)],
                     pl.program_id(0), s // tm)

        @pl.when(pl.program_id(0) == 0)
        def _():
            ext_ref[:, 0:CONV_HALO, :] = jnp.zeros((nchunk, CONV_HALO, LANES), F32)

        lo = _lo_mask((1, LANES))
        for cc in range(nchunk):
            cols = slice(cc * LANES, (cc + 1) * LANES)
            gcols = slice(CONV_WIDTH + cc * LANES, CONV_WIDTH + (cc + 1) * LANES)
            ext_ref[cc, CONV_HALO:CONV_HALO + tm, :] = cin_ref[:, cols] * _sigmoid(cin_ref[:, gcols])
            ext = ext_ref.at[cc]
            for r in range(tm // rc):
                rows = slice(r * rc, (r + 1) * rc)
                acc = jnp.zeros((rc // SUBLANES, SUBLANES, LANES), F32)
                for k in range(CONV_KERNEL):
                    acc = acc + cw_ref[k * SUBLANES:(k + 1) * SUBLANES, cols][None] * _rows(ext, r * rc + lead + k, rc)
                c1 = acc.reshape(rc, LANES) + cb_ref[:, cols]
                c1_ref[cc, rows, :] = c1
                nrm, _ = _group_stats(c1, lo)
                c2 = nrm * gain_ref[:, cols] + bias_ref[:, cols]
                c3_ref[rows, cols] = (c2 * _sigmoid(c2)).astype(BF16)
        ext_ref[:, 0:CONV_HALO, :] = ext_ref[:, tm:tm + CONV_HALO, :]

    return pl.pallas_call(
        body, grid=(s // tm,),
        in_specs=[pl.BlockSpec((tm, CIN_COLS), lambda i: (i, 0)), _resident((CONV_KERNEL * SUBLANES, CONV_WIDTH)),
                  _resident((1, CONV_WIDTH)), _resident((1, CONV_WIDTH)), _resident((1, CONV_WIDTH))] + g_in,
        out_specs=[pl.BlockSpec((tm, CONV_WIDTH), lambda i: (i, 0)), pl.BlockSpec((nchunk, tm, LANES), lambda i: (0, i, 0))] + g_out,
        out_shape=[jax.ShapeDtypeStruct((s, CONV_WIDTH), BF16), jax.ShapeDtypeStruct((nchunk, s, LANES), F32)] + g_shape,
        scratch_shapes=[pltpu.VMEM((nchunk, tm + CONV_HALO, LANES), F32)] + g_scratch,
        compiler_params=_cparams("arbitrary"), name="conv_fwd")(cin, cw8, cb, gain, bias, *shards)


def _out_proj(x, attn, c3, wo_a, wo_c, b_out, g_ffn):
    s = x.shape[0]
    tm = _tile(s)

    def body(x_ref, a_ref, c_ref, wa_ref, wc_ref, b_ref, g_ref, x2_ref, h2_ref):
        x2 = x_ref[...] + _dot(a_ref[...], wa_ref[...]) + _dot(c_ref[...], wc_ref[...]) + b_ref[...]
        x2_ref[...] = x2
        r = lax.rsqrt(jnp.mean(x2 * x2, axis=-1, keepdims=True) + EPS)
        h2_ref[...] = (x2 * r * g_ref[...]).astype(BF16)

    return pl.pallas_call(
        body, grid=(s // tm,),
        in_specs=[pl.BlockSpec((tm, D_MODEL), lambda i: (i, 0)), pl.BlockSpec((tm, Q_COLS), lambda i: (i, 0)),
                  pl.BlockSpec((tm, CONV_WIDTH), lambda i: (i, 0)),
                  pl.BlockSpec((Q_COLS, D_MODEL), lambda i: (0, 0), pipeline_mode=pl.Buffered(1)),
                  pl.BlockSpec((CONV_WIDTH, D_MODEL), lambda i: (1, 0), pipeline_mode=pl.Buffered(1)),
                  _resident((1, D_MODEL)), _resident((1, D_MODEL))],
        out_specs=[pl.BlockSpec((tm, D_MODEL), lambda i: (i, 0)), pl.BlockSpec((tm, D_MODEL), lambda i: (i, 0))],
        out_shape=[jax.ShapeDtypeStruct((s, D_MODEL), F32), jax.ShapeDtypeStruct((s, D_MODEL), BF16)],
        compiler_params=_cparams("parallel"), name="out_proj")(x, attn, c3, wo_a, wo_c, b_out, g_ffn)


FF_LANE_CHUNKS = -(-FF_CHUNK // LANES)
FF_PADDED = FF_LANE_CHUNKS * LANES


def _tap(ref, first_row, n):
    return ref[pl.ds(first_row, n, stride=1), :]


def _ffn_fwd(h2, x2, target, w_up, fw, fb, w_down):
    s = h2.shape[0]
    tm = _tile(s)
    hal = SUBLANES
    rc = min(128, tm)

    def body(h_ref, x2_ref, t_ref, wu_ref, fw_ref, fb_ref, wd_ref, up0_ref, act_ref, dy_ref, loss_ref,
             ext_ref, carry_ref, act_buf, y_ref):
        i, ci = pl.program_id(0), pl.program_id(1)

        @pl.when((i == 0) & (ci == 0))
        def _():
            carry_ref[...] = jnp.zeros(carry_ref.shape, F32)
            ext_ref[...] = jnp.zeros(ext_ref.shape, F32)
            act_buf[...] = jnp.zeros(act_buf.shape, BF16)
            loss_ref[...] = jnp.zeros((1, 1), F32)

        @pl.when(ci == 0)
        def _():
            y_ref[...] = x2_ref[...]

        h = h_ref[...]
        ws = (fw_ref[ci], fw_ref[ci + N_FF_PAIRS])
        bs = (fb_ref[ci], fb_ref[ci + N_FF_PAIRS])
        down = None
        for lo_c, hi_c in FF_COLS:
            for half in range(2):
                c = ci + half * N_FF_PAIRS
                u0 = _dot(h, wu_ref[c, :, lo_c:hi_c])
                up0_ref[half, 0, :, lo_c:hi_c] = u0.astype(BF16)
                for j in range(lo_c // LANES, -(-hi_c // LANES)):
                    w = min(LANES, hi_c - j * LANES)
                    ext_ref[half, j, 0:hal, 0:w] = carry_ref[c, :, j * LANES:j * LANES + w]
                    ext_ref[half, j, hal:hal + tm, 0:w] = u0[:, j * LANES - lo_c:j * LANES - lo_c + w]
                carry_ref[c, :, lo_c:hi_c] = u0[tm - hal:tm, :]
            for j in range(lo_c // LANES, -(-hi_c // LANES)):
                lanes = slice(j * LANES, (j + 1) * LANES)
                for r in range(tm // rc):
                    base = r * rc
                    ups = []
                    for half in range(2):
                        e, w = ext_ref.at[half, j], ws[half]
                        ups.append(w[0:1, lanes] * _tap(e, base + hal - 2, rc) + w[1:2, lanes] * _tap(e, base + hal - 1, rc)
                                   + w[2:3, lanes] * _tap(e, base + hal, rc) + bs[half][:, lanes])
                    g, u = ups
                    act_buf[base:base + rc, lanes] = (g * _sigmoid(g) * u).astype(BF16)
            act = act_buf[:, lo_c:hi_c]
            act_ref[0, :, lo_c:hi_c] = act
            part = _dot(act, wd_ref[ci, lo_c:hi_c, :])
            down = part if down is None else down + part
        y_ref[...] += down

        @pl.when(ci == N_FF_PAIRS - 1)
        def _():
            e = y_ref[...] - t_ref[...]
            dy_ref[...] = e * (1.0 / D_MODEL)
            loss_ref[...] += (0.5 / D_MODEL) * jnp.sum(e * e).reshape(1, 1)

    tok = lambda i, ci: (i, 0)
    return pl.pallas_call(
        body, grid=(s // tm, N_FF_PAIRS),
        in_specs=[pl.BlockSpec((tm, D_MODEL), tok), pl.BlockSpec((tm, D_MODEL), tok), pl.BlockSpec((tm, D_MODEL), tok),
                  _resident((N_DEV, D_MODEL, FF_CHUNK)), _resident((N_DEV, 3, FF_PADDED)), _resident((N_DEV, 1, FF_PADDED)),
                  _resident((N_FF_PAIRS, FF_CHUNK, D_MODEL))],
        out_specs=[pl.BlockSpec((2, 1, tm, FF_CHUNK), lambda i, ci: (0, ci, i, 0)),
                   pl.BlockSpec((1, tm, FF_CHUNK), lambda i, ci: (ci, i, 0)),
                   pl.BlockSpec((tm, D_MODEL), tok), pl.BlockSpec((1, 1), lambda i, ci: (0, 0))],
        out_shape=[jax.ShapeDtypeStruct((2, N_FF_PAIRS, s, FF_CHUNK), BF16),
                   jax.ShapeDtypeStruct((N_FF_PAIRS, s, FF_CHUNK), BF16), jax.ShapeDtypeStruct((s, D_MODEL), F32),
                   jax.ShapeDtypeStruct((1, 1), F32)],
        scratch_shapes=[pltpu.VMEM((2, FF_LANE_CHUNKS, tm + hal, LANES), F32), pltpu.VMEM((N_DEV, hal, FF_CHUNK), F32),
                        pltpu.VMEM((tm, FF_PADDED), BF16), pltpu.VMEM((tm, D_MODEL), F32)],
        compiler_params=_cparams("arbitrary", "arbitrary"), name="ffn_fwd")(h2, x2, target, w_up, fw, fb, w_down)


def _ffn_bwd(dy, up0, w_up, fw, fb, w_down):
    s = dy.shape[0]
    tm = _tile(s)
    nt = s // tm
    hal = 2 * SUBLANES
    nxt = SUBLANES
    rc = min(128, tm)

    def body(dy_ref, up0_ref, up0h_ref, wu_ref, fw_ref, fb_ref, wd_ref,
             dup0_ref, dh2_ref, dfw_ref, dfb_ref, ext_ref, dext_ref, carry_ref, dact_buf, dup0_buf):
        i, ci = pl.program_id(0), pl.program_id(1)
        t = nt - 1 - i

        @pl.when((i == 0) & (ci == 0))
        def _():
            for ref in (carry_ref, dfw_ref, dfb_ref, ext_ref, dext_ref, dact_buf):
                ref[...] = jnp.zeros(ref.shape, F32)
            dup0_buf[...] = jnp.zeros(dup0_buf.shape, BF16)

        @pl.when(ci == 0)
        def _():
            dh2_ref[...] = jnp.zeros(dh2_ref.shape, F32)

        dyb = dy_ref[...].astype(BF16)
        ws = (fw_ref[ci], fw_ref[ci + N_FF_PAIRS])
        bs = (fb_ref[ci], fb_ref[ci + N_FF_PAIRS])
        fold = lambda v: jnp.sum(v.reshape(rc // SUBLANES, SUBLANES, LANES), axis=0)
        dh2 = None
        for lo_c, hi_c in FF_COLS:
            dact_buf[:, lo_c:hi_c] = _dot_nt(dyb, wd_ref[ci, lo_c:hi_c, :])
            chunks = range(lo_c // LANES, -(-hi_c // LANES))
            for half in range(2):
                c = ci + half * N_FF_PAIRS
                prev = jnp.where(t > 0, up0h_ref[half, 0, :, lo_c:hi_c].astype(F32), 0.0)
                cur = up0_ref[half, 0, :, lo_c:hi_c].astype(F32)
                for j in chunks:
                    w = min(LANES, hi_c - j * LANES)
                    cols = slice(j * LANES - lo_c, j * LANES - lo_c + w)
                    ext_ref[half, j, 0:hal, 0:w] = prev[:, cols]
                    ext_ref[half, j, hal:hal + tm, 0:w] = cur[:, cols]
                    dext_ref[half, j, tm:tm + nxt, 0:w] = carry_ref[c, :, j * LANES:j * LANES + w]
            for j in chunks:
                lanes = slice(j * LANES, (j + 1) * LANES)
                acc = [jnp.zeros((SUBLANES, LANES), F32)] * 8
                for r in range(tm // rc):
                    base = r * rc
                    taps, ups = [], []
                    for half in range(2):
                        e, w = ext_ref.at[half, j], ws[half]
                        x = [_tap(e, base + hal - 2 + k, rc) for k in range(3)]
                        taps.append(x)
                        ups.append(w[0:1, lanes] * x[0] + w[1:2, lanes] * x[1] + w[2:3, lanes] * x[2] + bs[half][:, lanes])
                    g, u = ups
                    sg = _sigmoid(g)
                    dact = dact_buf[base:base + rc, lanes]
                    ds = (dact * u * (sg * (1.0 + g * (1.0 - sg))), dact * (g * sg))
                    for half in range(2):
                        dext_ref[half, j, base:base + rc, :] = ds[half]
                        acc[4 * half] = acc[4 * half] + fold(ds[half])
                        for k in range(3):
                            acc[4 * half + 1 + k] = acc[4 * half + 1 + k] + fold(ds[half] * taps[half][k])
                for half in range(2):
                    c = ci + half * N_FF_PAIRS
                    dfb_ref[c, :, lanes] += jnp.sum(acc[4 * half], axis=0, keepdims=True)
                    dfw_ref[c, :, lanes] += jnp.concatenate(
                        [jnp.sum(acc[4 * half + 1 + k], axis=0, keepdims=True) for k in range(3)], axis=0)
                for r in range(tm // rc):
                    base = r * rc
                    for half in range(2):
                        d, w = dext_ref.at[half, j], ws[half]
                        dup0 = w[2:3, lanes] * _tap(d, base, rc) + w[1:2, lanes] * _tap(d, base + 1, rc) + w[0:1, lanes] * _tap(d, base + 2, rc)
                        dup0_buf[half, base:base + rc, lanes] = dup0.astype(BF16)
            for half in range(2):
                c = ci + half * N_FF_PAIRS
                for j in chunks:
                    w = min(LANES, hi_c - j * LANES)
                    carry_ref[c, :, j * LANES:j * LANES + w] = dext_ref[half, j, 0:nxt, 0:w]
                dup0 = dup0_buf[half, :, lo_c:hi_c]
                dup0_ref[half, 0, :, lo_c:hi_c] = dup0
                part = _dot_nt(dup0, wu_ref[c, :, lo_c:hi_c])
                dh2 = part if dh2 is None else dh2 + part
        dh2_ref[...] += dh2

    tok = lambda i, ci: (nt - 1 - i, 0)
    acc = lambda shape: pl.BlockSpec(shape, lambda i, ci: (0,) * len(shape))
    return pl.pallas_call(
        body, grid=(nt, N_FF_PAIRS),
        in_specs=[pl.BlockSpec((tm, D_MODEL), tok),
                  pl.BlockSpec((2, 1, tm, FF_CHUNK), lambda i, ci: (0, ci, nt - 1 - i, 0)),
                  pl.BlockSpec((2, 1, hal, FF_CHUNK), lambda i, ci: (0, ci, jnp.maximum((nt - 1 - i) * (tm // hal) - 1, 0), 0)),
                  _resident((N_DEV, D_MODEL, FF_CHUNK)), _resident((N_DEV, 3, FF_PADDED)), _resident((N_DEV, 1, FF_PADDED)),
                  _resident((N_FF_PAIRS, FF_CHUNK, D_MODEL))],
        out_specs=[pl.BlockSpec((2, 1, tm, FF_CHUNK), lambda i, ci: (0, ci, nt - 1 - i, 0)),
                   pl.BlockSpec((tm, D_MODEL), tok), acc((N_DEV, 3, FF_PADDED)), acc((N_DEV, 1, FF_PADDED))],
        out_shape=[jax.ShapeDtypeStruct((2, N_FF_PAIRS, s, FF_CHUNK), BF16), jax.ShapeDtypeStruct((s, D_MODEL), F32),
                   jax.ShapeDtypeStruct((N_DEV, 3, FF_PADDED), F32), jax.ShapeDtypeStruct((N_DEV, 1, FF_PADDED), F32)],
        scratch_shapes=[pltpu.VMEM((2, FF_LANE_CHUNKS, tm + hal, LANES), F32), pltpu.VMEM((2, FF_LANE_CHUNKS, tm + nxt, LANES), F32),
                        pltpu.VMEM((N_DEV, nxt, FF_CHUNK), F32), pltpu.VMEM((tm, FF_PADDED), F32),
                        pltpu.VMEM((2, tm, FF_PADDED), BF16)],
        compiler_params=_cparams("arbitrary", "arbitrary"), name="ffn_bwd")(dy, up0, up0, w_up, fw, fb, w_down)


def _ffn_norm_bwd(dh2, dy, x2, g_ffn, w_out):
    s = dy.shape[0]
    tm = _tile(s)

    def body(dh_ref, dy_ref, x2_ref, g_ref, wo_ref, dx2_ref, dmix_ref, dg_ref, dbo_ref):
        @pl.when(pl.program_id(0) == 0)
        def _():
            dg_ref[...] = jnp.zeros(dg_ref.shape, F32)
            dbo_ref[...] = jnp.zeros(dbo_ref.shape, F32)

        x2v = x2_ref[...]
        r = lax.rsqrt(jnp.mean(x2v * x2v, axis=-1, keepdims=True) + EPS)
        n2 = x2v * r
        dh2 = dh_ref[...]
        dg_ref[...] += jnp.sum(dh2 * n2, axis=0, keepdims=True)
        dn = dh2 * g_ref[...]
        dx2 = dy_ref[...] + r * (dn - n2 * jnp.mean(dn * n2, axis=-1, keepdims=True))
        dx2_ref[...] = dx2
        dbo_ref[...] += jnp.sum(dx2, axis=0, keepdims=True)
        dmix_ref[...] = _dot_nt(dx2.astype(BF16), wo_ref[...])

    tok = pl.BlockSpec((tm, D_MODEL), lambda i: (i, 0))
    vec = pl.BlockSpec((1, D_MODEL), lambda i: (0, 0))
    return pl.pallas_call(
        body, grid=(s // tm,),
        in_specs=[tok, tok, tok, _resident((1, D_MODEL)), _resident((D_MODEL, D_MODEL))],
        out_specs=[tok, tok, vec, vec],
        out_shape=[jax.ShapeDtypeStruct((s, D_MODEL), F32), jax.ShapeDtypeStruct((s, D_MODEL), F32),
                   jax.ShapeDtypeStruct((1, D_MODEL), F32), jax.ShapeDtypeStruct((1, D_MODEL), F32)],
        compiler_params=_cparams("arbitrary"), name="ffn_norm_bwd")(dh2, dy, x2, g_ffn, w_out)


def _conv_bwd(dmixed, c1, cin, cw8, gain, bias, g8s):
    ns = len(g8s)
    s_in, s_out, s_shape, s_scratch = _scatter_specs(g8s)
    s = cin.shape[0]
    tm = _tile(s)
    nt = s // tm
    rc = 64
    rn = min(256, tm)
    hal = CONV_HALO
    lead = hal - (CONV_KERNEL - 1)
    nchunk = CONV_WIDTH // LANES

    def body(dc3_ref, dc3n_ref, c1_ref, c1n_ref, cin_ref, cinp_ref, cw_ref, gain_ref, bias_ref, *rest):
        dcin_ref, dcw_ref, dcb_ref, dgain_ref, dbias_ref, dbcin_ref = rest[ns:ns + 6]
        c0_ext, dc1_ext, dcw8 = rest[2 * ns + 6:2 * ns + 9]
        i = pl.program_id(0)
        first, last = i == 0, i == nt - 1
        own = rest[2 * ns + 9:]
        _run_scatters([_ReduceScatter(rest[a], rest[ns + 6 + a], *own[N_SCATTER_SCRATCH * a:N_SCATTER_SCRATCH * (a + 1)])
                       for a in range(ns)], i, nt)

        @pl.when(first)
        def _():
            for ref in (dcw8, dcb_ref, dgain_ref, dbias_ref, dbcin_ref):
                ref[...] = jnp.zeros(ref.shape, F32)

        lo = _lo_mask((1, LANES))

        def norm_bwd(dc3, c1v, cols):
            nrm, rstd = _group_stats(c1v, lo)
            c2 = nrm * gain_ref[:, cols] + bias_ref[:, cols]
            sg = _sigmoid(c2)
            dc2 = dc3 * (sg * (1.0 + c2 * (1.0 - sg)))
            dn = dc2 * gain_ref[:, cols]
            inv = 1.0 / HEAD_DIM
            dc1 = rstd * (dn - _half_sums(dn, lo) * inv - nrm * (_half_sums(dn * nrm, lo) * inv))
            return dc1, dc2, nrm

        def row_sum(v):
            return jnp.sum(v, axis=0, keepdims=True)

        for cc in range(nchunk):
            cols = slice(cc * LANES, (cc + 1) * LANES)
            gcols = slice(CONV_WIDTH + cc * LANES, CONV_WIDTH + (cc + 1) * LANES)
            c0e, d1e = c0_ext.at[cc], dc1_ext.at[cc]
            c0e[0:hal, :] = jnp.where(first, 0.0, cinp_ref[:, cols] * _sigmoid(cinp_ref[:, gcols]))
            dc1n, _, _ = norm_bwd(dc3n_ref[:, cols], c1n_ref[cc], cols)
            d1e[tm:tm + hal, :] = jnp.where(last, 0.0, dc1n)

            for r in range(tm // rn):
                rows = slice(r * rn, (r + 1) * rn)
                c0e[hal + r * rn:hal + (r + 1) * rn, :] = cin_ref[rows, cols] * _sigmoid(cin_ref[rows, gcols])
                dc1, dc2, nrm = norm_bwd(dc3_ref[rows, cols], c1_ref[cc, rows, :], cols)
                d1e[rows, :] = dc1
                dgain_ref[:, cols] += row_sum(dc2 * nrm)
                dbias_ref[:, cols] += row_sum(dc2)
                dcb_ref[:, cols] += row_sum(dc1)
            zero = jnp.zeros((1, LANES), F32)

            for k0 in range(0, CONV_KERNEL, SUBLANES):
                taps = range(k0, min(k0 + SUBLANES, CONV_KERNEL))

                def tap_sums(r, acc, taps=taps):
                    d = _rows(d1e, r * rc, rc)
                    return tuple(a + jnp.sum(d * _rows(c0e, r * rc + lead + k, rc), axis=0) for a, k in zip(acc, taps))

                acc = lax.fori_loop(0, tm // rc, tap_sums, tuple(dcw8[k * SUBLANES:(k + 1) * SUBLANES, cols] for k in taps))
                for a, k in zip(acc, taps):
                    dcw8[k * SUBLANES:(k + 1) * SUBLANES, cols] = a

            def input_grad(r, sums):
                rows = pl.ds(pl.multiple_of(r * rc, rc), rc)
                dc0 = jnp.zeros((rc // SUBLANES, SUBLANES, LANES), F32)
                for k in range(CONV_KERNEL):
                    dc0 = dc0 + cw_ref[k * SUBLANES:(k + 1) * SUBLANES, cols][None] * _rows(d1e, r * rc + CONV_KERNEL - 1 - k, rc)
                dc0 = dc0.reshape(rc, LANES)
                sg = _sigmoid(cin_ref[rows, gcols])
                da = dc0 * sg
                dgate = dc0 * cin_ref[rows, cols] * sg * (1.0 - sg)
                dcin_ref[rows, cols] = da.astype(BF16)
                dcin_ref[rows, gcols] = dgate.astype(BF16)
                return sums[0] + row_sum(da), sums[1] + row_sum(dgate)

            sums = lax.fori_loop(0, tm // rc, input_grad, (zero, zero))
            dbcin_ref[:, cols] += sums[0]
            dbcin_ref[:, gcols] += sums[1]

        @pl.when(last)
        def _():
            for k in range(CONV_KERNEL):
                dcw_ref[k:k + 1, :] = jnp.sum(dcw8[k * SUBLANES:(k + 1) * SUBLANES, :], axis=0, keepdims=True)

    nh = tm // hal
    acc = lambda shape: pl.BlockSpec(shape, lambda i: (0,) * len(shape))
    return pl.pallas_call(
        body, grid=(nt,),
        in_specs=[pl.BlockSpec((tm, CONV_WIDTH), lambda i: (i, 1)),
                  pl.BlockSpec((hal, CONV_WIDTH), lambda i: (jnp.minimum((i + 1) * nh, s // hal - 1), 1)),
                  pl.BlockSpec((nchunk, tm, LANES), lambda i: (0, i, 0)),
                  pl.BlockSpec((nchunk, hal, LANES), lambda i: (0, jnp.minimum((i + 1) * nh, s // hal - 1), 0)),
                  pl.BlockSpec((tm, CIN_COLS), lambda i: (i, 0)),
                  pl.BlockSpec((hal, CIN_COLS), lambda i: (jnp.maximum(i * nh - 1, 0), 0)),
                  _resident((CONV_KERNEL * SUBLANES, CONV_WIDTH)), _resident((1, CONV_WIDTH)), _resident((1, CONV_WIDTH))] + s_in,
        out_specs=[pl.BlockSpec((tm, CIN_COLS), lambda i: (i, 0)), acc((CONV_KERNEL, CONV_WIDTH)), acc((1, CONV_WIDTH)),
                   acc((1, CONV_WIDTH)), acc((1, CONV_WIDTH)), acc((1, CIN_COLS))] + s_out,
        out_shape=[jax.ShapeDtypeStruct((s, CIN_COLS), BF16), jax.ShapeDtypeStruct((CONV_KERNEL, CONV_WIDTH), F32),
                   jax.ShapeDtypeStruct((1, CONV_WIDTH), F32), jax.ShapeDtypeStruct((1, CONV_WIDTH), F32),
                   jax.ShapeDtypeStruct((1, CONV_WIDTH), F32), jax.ShapeDtypeStruct((1, CIN_COLS), F32)] + s_shape,
        scratch_shapes=[pltpu.VMEM((nchunk, tm + hal, LANES), F32), pltpu.VMEM((nchunk, tm + hal, LANES), F32),
                        pltpu.VMEM((CONV_KERNEL * SUBLANES, CONV_WIDTH), F32)] + s_scratch,
        compiler_params=_cparams("arbitrary"), name="conv_bwd")(dmixed, dmixed, c1, c1, cin, cin, cw8, gain, bias, *g8s)


def _attn_bwd(qkv, dmixed, gq2, gk2, sinks, g8s):
    ns = len(g8s)
    s_in, s_out, s_shape, s_scratch = _scatter_specs(g8s)
    s = qkv.shape[0]
    tq = _tile(s)
    nb = tq // ATT_BLOCK
    nt = s // tq

    def body(q_ref, kv_ref, kvp_ref, do_ref, gq_ref, gk_ref, sink_ref, *rest):
        dqkv_ref, dgq_ref, dgk_ref, dsink_ref, dbqkv_ref = rest[ns:ns + 5]
        dk_acc, dv_acc, carry_k, carry_v = rest[2 * ns + 5:2 * ns + 9]
        i = pl.program_id(0)
        t = nt - 1 - i
        own = rest[2 * ns + 9:]
        _run_scatters([_ReduceScatter(rest[a], rest[ns + 5 + a], *own[N_SCATTER_SCRATCH * a:N_SCATTER_SCRATCH * (a + 1)])
                       for a in range(ns)], i, nt)

        @pl.when(i == 0)
        def _():
            for ref in (carry_k, carry_v, dgq_ref, dgk_ref, dsink_ref, dbqkv_ref):
                ref[...] = jnp.zeros(ref.shape, F32)

        lo = _lo_mask((1, LANES))
        lane_id = lax.broadcasted_iota(jnp.int32, (1, LANES), 1)
        kv_all = jnp.concatenate([kvp_ref[...], kv_ref[...]], axis=0)
        k_lo, k_hi, v_lo, v_hi, kn_pre, rk = _kv_variants(kv_all, gk_ref[...], lo)
        for acc_ref, carry in ((dk_acc, carry_k), (dv_acc, carry_v)):
            acc_ref[:, 0:tq, :] = jnp.zeros((N_KV_HEADS, tq, LANES), F32)
            acc_ref[:, tq:tq + ATT_BLOCK, :] = carry[...]
        dsink = jnp.zeros((1, LANES), F32)
        dgq = jnp.zeros((1, LANES), F32)
        gq = gq_ref[...]
        for b in range(nb):
            rel, valid = _att_consts(t == 0, b)
            rows = slice(b * ATT_BLOCK, (b + 1) * ATT_BLOCK)
            keys = slice(b * ATT_BLOCK, (b + 2) * ATT_BLOCK)
            for kvh in range(N_KV_HEADS):
                pairs = (2 * kvh, 2 * kvh + 1)
                q_raw = jnp.concatenate([q_ref[rows, p * LANES:(p + 1) * LANES] for p in pairs], axis=0)
                qn_pre, rq = _head_norm(q_raw, lo)
                q2 = (qn_pre * gq).astype(BF16)
                do2 = jnp.concatenate([do_ref[rows, p * LANES:(p + 1) * LANES] for p in pairs], axis=0).astype(BF16)
                dq2 = jnp.zeros((2 * ATT_BLOCK, LANES), F32)
                for odd, (k_op, v_op) in enumerate(((k_lo[kvh][keys], v_lo[kvh][keys]), (k_hi[kvh][keys], v_hi[kvh][keys]))):
                    ha, hb = 2 * pairs[0] + odd, 2 * pairs[1] + odd
                    p, p_sink = _probs(q2, k_op, rel, valid, _row_const(SLOPES[ha], SLOPES[hb]),
                                       _row_const(sink_ref[ha], sink_ref[hb]))
                    dp = _dot_nt(do2, v_op)
                    delta = jnp.sum(p * dp, axis=-1, keepdims=True)
                    ds = (p * (dp - delta) * (1.0 / math.sqrt(HEAD_DIM))).astype(BF16)
                    dsk = p_sink * delta
                    dsink = dsink - jnp.where(lane_id == ha, jnp.sum(dsk[0:ATT_BLOCK]), 0.0) \
                        - jnp.where(lane_id == hb, jnp.sum(dsk[ATT_BLOCK:]), 0.0)
                    dq2 = dq2 + _dot(ds, k_op)
                    half = lo if odd == 0 else jnp.logical_not(lo)
                    dk_acc[kvh, keys, :] += jnp.where(half, _dot_tn(ds, q2), 0.0)
                    dv_acc[kvh, keys, :] += jnp.where(half, _dot_tn(p.astype(BF16), do2), 0.0)
                dgq = dgq + jnp.sum(dq2 * qn_pre, axis=0, keepdims=True)
                dq_raw = _head_norm_bwd(dq2 * gq, qn_pre, rq, lo)
                for n, p_ in enumerate(pairs):
                    blk = dq_raw[n * ATT_BLOCK:(n + 1) * ATT_BLOCK]
                    dqkv_ref[rows, p_ * LANES:(p_ + 1) * LANES] = blk.astype(BF16)
                    dbqkv_ref[:, p_ * LANES:(p_ + 1) * LANES] += jnp.sum(blk, axis=0, keepdims=True)
        carry_k[...] = dk_acc[:, 0:ATT_BLOCK, :]
        carry_v[...] = dv_acc[:, 0:ATT_BLOCK, :]

        def fold(acc_ref):
            both = []
            for kvh in range(N_KV_HEADS):
                a = acc_ref[kvh, ATT_BLOCK:ATT_BLOCK + tq, :]
                both.append(a + pltpu.roll(a, HEAD_DIM, 1))
            return jnp.where(lo, both[0], both[1])

        dkn = fold(dk_acc)
        dv = fold(dv_acc)
        kn_c, rk_c = kn_pre[ATT_BLOCK:], rk[ATT_BLOCK:]
        dgk_ref[...] += jnp.sum(dkn * kn_c, axis=0, keepdims=True)
        dk_raw = _head_norm_bwd(dkn * gk_ref[...], kn_c, rk_c, lo)
        dqkv_ref[:, Q_COLS:Q_COLS + KV_COLS] = dk_raw.astype(BF16)
        dqkv_ref[:, Q_COLS + KV_COLS:] = dv.astype(BF16)
        dbqkv_ref[:, Q_COLS:Q_COLS + KV_COLS] += jnp.sum(dk_raw, axis=0, keepdims=True)
        dbqkv_ref[:, Q_COLS + KV_COLS:] += jnp.sum(dv, axis=0, keepdims=True)
        dgq_ref[...] += dgq
        dsink_ref[...] += dsink

        @pl.when(i == nt - 1)
        def _():
            for ref in (dgq_ref, dgk_ref):
                v = ref[...]
                ref[...] = v + pltpu.roll(v, HEAD_DIM, 1)

    acc = lambda shape: pl.BlockSpec(shape, lambda i: (0,) * len(shape))
    return pl.pallas_call(
        body, grid=(nt,),
        in_specs=[pl.BlockSpec((tq, Q_COLS), lambda i: (nt - 1 - i, 0)),
                  pl.BlockSpec((tq, 2 * KV_COLS), lambda i: (nt - 1 - i, 2)),
                  pl.BlockSpec((ATT_BLOCK, 2 * KV_COLS), lambda i: (jnp.maximum((nt - 1 - i) * nb - 1, 0), 2)),
                  pl.BlockSpec((tq, Q_COLS), lambda i: (nt - 1 - i, 0)),
                  _resident((1, LANES)), _resident((1, LANES)), pl.BlockSpec(memory_space=pltpu.SMEM)] + s_in,
        out_specs=[pl.BlockSpec((tq, QKV_COLS), lambda i: (nt - 1 - i, 0)), acc((1, LANES)), acc((1, LANES)),
                   acc((1, LANES)), acc((1, QKV_COLS))] + s_out,
        out_shape=[jax.ShapeDtypeStruct((s, QKV_COLS), BF16), jax.ShapeDtypeStruct((1, LANES), F32),
                   jax.ShapeDtypeStruct((1, LANES), F32), jax.ShapeDtypeStruct((1, LANES), F32),
                   jax.ShapeDtypeStruct((1, QKV_COLS), F32)] + s_shape,
        scratch_shapes=[pltpu.VMEM((N_KV_HEADS, tq + ATT_BLOCK, LANES), F32), pltpu.VMEM((N_KV_HEADS, tq + ATT_BLOCK, LANES), F32),
                        pltpu.VMEM((N_KV_HEADS, ATT_BLOCK, LANES), F32), pltpu.VMEM((N_KV_HEADS, ATT_BLOCK, LANES), F32)] + s_scratch,
        compiler_params=_cparams("arbitrary"), name="attn_bwd")(qkv, qkv, qkv, dmixed, gq2, gk2, sinks, *g8s)


def _in_bwd(dqkv, dcin, w_qkv, w_cin, x, dx2, g_mix, g8s):
    ns = len(g8s)
    s_in, s_out, s_shape, s_scratch = _scatter_specs(g8s)
    s = x.shape[0]
    tm = _tile(s)

    def body(dq_ref, dc_ref, wq_ref, wc_ref, x_ref, dx2_ref, g_ref, *rest):
        gx_ref, dg_ref = rest[ns:ns + 2]
        own = rest[2 * ns + 2:]
        _run_scatters([_ReduceScatter(rest[a], rest[ns + 2 + a], *own[N_SCATTER_SCRATCH * a:N_SCATTER_SCRATCH * (a + 1)])
                       for a in range(ns)], pl.program_id(0), s // tm)

        @pl.when(pl.program_id(0) == 0)
        def _():
            dg_ref[...] = jnp.zeros(dg_ref.shape, F32)

        dh = _dot_nt(dq_ref[...], wq_ref[...]) + _dot_nt(dc_ref[...], wc_ref[...])
        xv = x_ref[...]
        r = lax.rsqrt(jnp.mean(xv * xv, axis=-1, keepdims=True) + EPS)
        n = xv * r
        dg_ref[...] += jnp.sum(dh * n, axis=0, keepdims=True)
        dn = dh * g_ref[...]
        gx_ref[...] = dx2_ref[...] + r * (dn - n * jnp.mean(dn * n, axis=-1, keepdims=True))

    return pl.pallas_call(
        body, grid=(s // tm,),
        in_specs=[pl.BlockSpec((tm, QKV_COLS), lambda i: (i, 0)), pl.BlockSpec((tm, CIN_COLS), lambda i: (i, 0)),
                  _resident((D_MODEL, QKV_COLS)), _resident((D_MODEL, CIN_COLS)),
                  pl.BlockSpec((tm, D_MODEL), lambda i: (i, 0)), pl.BlockSpec((tm, D_MODEL), lambda i: (i, 0)),
                  _resident((1, D_MODEL))] + s_in,
        out_specs=[pl.BlockSpec((tm, D_MODEL), lambda i: (i, 0)), pl.BlockSpec((1, D_MODEL), lambda i: (0, 0))] + s_out,
        out_shape=[jax.ShapeDtypeStruct((s, D_MODEL), F32), jax.ShapeDtypeStruct((1, D_MODEL), F32)] + s_shape,
        scratch_shapes=s_scratch,
        compiler_params=_cparams("arbitrary"), name="in_bwd")(dqkv, dcin, w_qkv, w_cin, x, dx2, g_mix, *g8s)


def _tn_matmul(a, b, name):
    ga, s, m = a.shape
    gb, _, n = b.shape
    g = max(ga, gb)
    tk = min(TN_TOKENS, s)

    def body(a_ref, b_ref, o_ref):
        @pl.when(pl.program_id(1) == 0)
        def _():
            o_ref[...] = jnp.zeros(o_ref.shape, F32)

        o_ref[0] += _dot_tn(a_ref[0].astype(BF16), b_ref[0].astype(BF16))

    return pl.pallas_call(
        body, grid=(g, s // tk),
        in_specs=[pl.BlockSpec((1, tk, m), (lambda gi, k: (gi, k, 0)) if ga > 1 else (lambda gi, k: (0, k, 0))),
                  pl.BlockSpec((1, tk, n), (lambda gi, k: (gi, k, 0)) if gb > 1 else (lambda gi, k: (0, k, 0)))],
        out_specs=pl.BlockSpec((1, m, n), lambda gi, k: (gi, 0, 0)),
        out_shape=jax.ShapeDtypeStruct((g, m, n), F32),
        compiler_params=_cparams("parallel", "arbitrary"), name=name)(a, b)


def _allgather(shards, dtypes):
    n = len(shards)
    n_copies = 1 + 2 * len(OTHER_CHIPS)

    def body(*refs):
        ins, outs = refs[:n], refs[n:2 * n]
        send_sems, recv_sems = refs[2 * n:]
        x, y, c = _position()
        me, sibling = (x, y, c), (x, y, 1 - c)
        chips = [(_flip(x, fx), _flip(y, fy)) for fx, fy in OTHER_CHIPS]
        for a in range(n):
            outs[a][_dev_index(*me)] = ins[a][...].astype(dtypes[a])

        def copy(a, k, block, to):
            rows = outs[a].at[_dev_index(*block)]
            return pltpu.make_async_remote_copy(src_ref=rows, dst_ref=rows, send_sem=send_sems.at[a, k],
                                                recv_sem=recv_sems.at[a, k], device_id=to, device_id_type=MESH)

        started = []
        for a in range(n):
            for j, chip in enumerate(chips):
                started.append(copy(a, 1 + j, me, (*chip, c)))
            started.append(copy(a, 0, me, sibling))
        for cp in started:
            cp.start()
        for a in range(n):
            for j, chip in enumerate(chips):
                copy(a, 1 + j, (*chip, c), me).wait_recv()
                fwd = copy(a, 1 + len(chips) + j, (*chip, c), sibling)
                fwd.start()
                started.append(fwd)
        for a in range(n):
            copy(a, 0, sibling, me).wait_recv()
            for j, chip in enumerate(chips):
                copy(a, 1 + len(chips) + j, (*chip, 1 - c), me).wait_recv()
        for cp in started:
            cp.wait_send()

    vmem = pl.BlockSpec(memory_space=pltpu.VMEM)
    return pl.pallas_call(
        body, in_specs=[vmem] * n, out_specs=[vmem] * n,
        out_shape=[jax.ShapeDtypeStruct((N_DEV,) + w.shape, dt) for w, dt in zip(shards, dtypes)],
        scratch_shapes=[pltpu.SemaphoreType.DMA((n, n_copies)), pltpu.SemaphoreType.DMA((n, n_copies))],
        compiler_params=pltpu.CompilerParams(vmem_limit_bytes=VMEM_LIMIT), name="allgather_weights")(*shards)


def _small_exchange(v, reduce, name):
    rows = v.shape[0]

    def body(v_ref, out_ref, *scratch):
        gath = scratch[0] if reduce else out_ref
        send_sems, recv_sems = scratch[-2:]
        x, y, c = _position()
        me = _dev_index(x, y, c)
        gath[me] = v_ref[...]
        peers = [(_flip(x, k >> 2 & 1), _flip(y, k >> 1 & 1), _flip(c, k & 1)) for k in range(1, N_DEV)]

        def copy(k, block):
            return pltpu.make_async_remote_copy(src_ref=gath.at[block], dst_ref=gath.at[block], send_sem=send_sems.at[k],
                                                recv_sem=recv_sems.at[k], device_id=peers[k], device_id_type=MESH)

        for k in range(N_DEV - 1):
            copy(k, me).start()
        for k in range(N_DEV - 1):
            copy(k, _dev_index(*peers[k])).wait_recv()
        for k in range(N_DEV - 1):
            copy(k, me).wait_send()
        if reduce:
            total = gath[0]
            for d in range(1, N_DEV):
                total = total + gath[d]
            out_ref[...] = total

    vmem = pl.BlockSpec(memory_space=pltpu.VMEM)
    out_shape = (rows, LANES) if reduce else (N_DEV, rows, LANES)
    return pl.pallas_call(
        body, in_specs=[vmem], out_specs=vmem, out_shape=jax.ShapeDtypeStruct(out_shape, F32),
        scratch_shapes=([pltpu.VMEM((N_DEV, rows, LANES), F32)] if reduce else [])
        + [pltpu.SemaphoreType.DMA((N_DEV - 1,)), pltpu.SemaphoreType.DMA((N_DEV - 1,))],
        compiler_params=pltpu.CompilerParams(vmem_limit_bytes=VMEM_LIMIT), name=name)(v)


def _row_tile(r):
    for n in (8, 4, 2):
        if r % (n * SUBLANES) == 0:
            return r // n
    return r


def _adam_math(wv, gv, mv, vv):
    mn = ADAM_B1 * mv + (1.0 - ADAM_B1) * gv
    vn = ADAM_B2 * vv + (1.0 - ADAM_B2) * (gv * gv)
    m_hat = mn / (1.0 - ADAM_B1 ** ADAM_STEP)
    v_hat = vn / (1.0 - ADAM_B2 ** ADAM_STEP)
    return -ADAM_LR * (m_hat / (jnp.sqrt(v_hat) + ADAM_EPS) + ADAM_WD * wv), mn, vn


def _adamw(w, g, m, v, name):
    r, c_ = w.shape
    tr = _row_tile(r)

    def body(w_ref, g_ref, m_ref, v_ref, d_ref, mo_ref, vo_ref):
        d_ref[...], mo_ref[...], vo_ref[...] = _adam_math(w_ref[...], g_ref[...], m_ref[...], v_ref[...])

    spec = pl.BlockSpec((tr, c_), lambda i: (i, 0))
    return pl.pallas_call(
        body, grid=(r // tr,), in_specs=[spec] * 4, out_specs=[spec] * 3,
        out_shape=[jax.ShapeDtypeStruct((r, c_), F32)] * 3,
        compiler_params=_cparams("parallel"), name=name)(w, g, m, v)


FW_ROWS = 24
CW_ROWS = 32
R_FW = 0
R_FB = R_FW + N_DEV * FW_ROWS
R_CW = R_FB + 48
R_BQKV = R_CW + (CONV_WIDTH // LANES) * CW_ROWS
R_BCIN = R_BQKV + 8
R_GMIX = R_BCIN + 8
R_BOUT = R_GMIX + 8
R_GFFN = R_BOUT + 8
R_CB = R_GFFN + 8
R_CGAIN = R_CB + 8
R_CBIAS = R_CGAIN + 8
R_QKS = R_CBIAS + 8
SMALL_ROWS = R_QKS + 8


def _pack_small(raw):
    def rows(a, n):
        a = a.reshape(-1, LANES)
        return jnp.pad(a, ((0, n - a.shape[0]), (0, 0)))

    fw = jnp.pad(raw["dfw"].reshape(N_DEV, -1, LANES), ((0, 0), (0, FW_ROWS - 3 * FF_LANE_CHUNKS), (0, 0)))
    cw = jnp.pad(raw["dcw"].reshape(CONV_KERNEL, -1, LANES).transpose(1, 0, 2), ((0, 0), (0, CW_ROWS - CONV_KERNEL), (0, 0)))
    qks = jnp.concatenate([raw["dgq"], raw["dgk"], raw["dsink"]], axis=0)
    return jnp.concatenate([
        fw.reshape(-1, LANES), rows(raw["dfb"][:, 0, :FF_CHUNK], 48), cw.reshape(-1, LANES), rows(raw["dbqkv"], 8),
        rows(raw["dbcin"], 8), rows(raw["dg_mix"], 8), rows(raw["db_out"], 8), rows(raw["dg_ffn"], 8), rows(raw["dcb"], 8),
        rows(raw["dcgain"], 8), rows(raw["dcbias"], 8), rows(qks, 8)], axis=0)


def _adamw_small(gpack, w, m, v):
    n = len(SMALL)
    ix = {name: i for i, name in enumerate(SMALL)}

    def body(g_ref, *refs):
        w_refs, m_refs, v_refs, outs = refs[:n], refs[n:2 * n], refs[2 * n:3 * n], refs[3 * n:]
        d = _dev_index(*_position())

        def step(name, idx, gv):
            i = ix[name]
            delta, mn, vn = _adam_math(w_refs[i][idx], gv, m_refs[i][idx], v_refs[i][idx])
            for ref, val in zip(outs[4 * i:4 * i + 4], (gv, delta, mn, vn)):
                ref[idx] = val

        def whole(name, row, nrows):
            step(name, (slice(None), slice(None)), g_ref[row:row + nrows, :])

        whole("mix_norm_gain", R_GMIX, 8)
        whole("b_out", R_BOUT, 8)
        whole("ffn_norm_gain", R_GFFN, 8)
        whole("conv_dw_b", R_CB, 4)
        whole("conv_norm_gain", R_CGAIN, 4)
        whole("conv_norm_bias", R_CBIAS, 4)
        whole("ffn_dw_b", R_FB, 2 * D_FF // LANES)
        nq = QKV_COLS // LANES
        step("b_in", (slice(0, nq), slice(None)), g_ref[R_BQKV:R_BQKV + nq, :])
        step("b_in", (slice(nq, nq + CIN_COLS // LANES), slice(None)), g_ref[R_BCIN:R_BCIN + CIN_COLS // LANES, :])
        step("q_norm_gain", (slice(None), slice(None)), g_ref[R_QKS:R_QKS + 1, 0:HEAD_DIM])
        step("k_norm_gain", (slice(None), slice(None)), g_ref[R_QKS + 1:R_QKS + 2, 0:HEAD_DIM])
        step("attn_sinks", (slice(None), slice(None)), g_ref[R_QKS + 2:R_QKS + 3, 0:N_Q_HEADS])
        blk = g_ref[pl.ds(pl.multiple_of(R_CW + CW_ROWS * lax.shift_right_logical(d, 1), SUBLANES), CW_ROWS), :]
        blk = jnp.where((d & 1) == 1, pltpu.roll(blk, HEAD_DIM, 1), blk)
        step("conv_dw_w", (slice(None), slice(None)), blk[0:CONV_KERNEL, 0:CONV_WIDTH // N_DEV])
        blk = g_ref[pl.ds(pl.multiple_of(R_FW + FW_ROWS * d, SUBLANES), FW_ROWS), :]
        for k in range(3):
            for j in range(FF_LANE_CHUNKS):
                wd = min(LANES, FF_CHUNK - j * LANES)
                row = k * FF_LANE_CHUNKS + j
                step("ffn_dw_w", (slice(k, k + 1), slice(j * LANES, j * LANES + wd)), blk[row:row + 1, 0:wd])

    vmem = pl.BlockSpec(memory_space=pltpu.VMEM)
    args = [gpack] + [d[name] for d in (w, m, v) for name in SMALL]
    outs = pl.pallas_call(
        body, in_specs=[vmem] * len(args), out_specs=[vmem] * (4 * n),
        out_shape=[jax.ShapeDtypeStruct(w[name].shape, F32) for name in SMALL for _ in range(4)],
        compiler_params=pltpu.CompilerParams(vmem_limit_bytes=VMEM_LIMIT), name="adamw_small")(*args)
    return {name: outs[4 * i:4 * i + 4] for i, name in enumerate(SMALL)}


def _token_mixing(x, p, attn_shards, conv_shards):
    qkv, cin, h1 = _mix_proj(x, p["g_mix"], p["w_qkv"], p["w_cin"], p["b_qkv"], p["b_cin"])
    attn, *from_attn = _attn_fwd(qkv, p["gq2"], p["gk2"], p["sinks"], attn_shards)
    c3, c1, *from_conv = _conv_fwd(cin, p["cw8"], p["cb"], p["cgain"], p["cbias"], conv_shards)
    return (qkv, cin, h1, attn, c3, c1), from_attn, from_conv


def _rest_of_step(x, target, p, saved, scatter):
    s = x.shape[0]
    qkv, cin, h1, attn, c3, c1 = saved
    cw8, w_out, w_up, w_down = p["cw8"], p["w_out"], p["w_up"], p["w_down"]
    x2, h2 = _out_proj(x, attn, c3, w_out, w_out, p["b_out"], p["g_ffn"])
    fw, fb = p["fw"], p["fb"]
    up0, act, dy, loss = _ffn_fwd(h2, x2, target, w_up, fw, fb, w_down)
    dup0, dh2, dfw, dfb = _ffn_bwd(dy, up0, w_up, fw, fb, w_down)
    dx2, dmixed, dg_ffn, db_out = _ffn_norm_bwd(dh2, dy, x2, p["g_ffn"], w_out)
    dw_up = _tn_matmul(h2[None], dup0.reshape(N_DEV, s, FF_CHUNK), "dw_up")
    dw_down = _tn_matmul(act, dy[None], "dw_down").reshape(N_DEV, -1, D_MODEL)
    dw_out = jnp.concatenate([_tn_matmul(attn[None], dx2[None], "dw_out_attn")[0],
                              _tn_matmul(c3[None], dx2[None], "dw_out_conv")[0]], axis=0).reshape(N_DEV, -1, D_MODEL)
    dcin, dcw, dcb, dcgain, dcbias, dbcin, *g_up = _conv_bwd(dmixed, c1, cin, cw8, p["cgain"], p["cbias"], [dw_up] if scatter else [])
    dqkv, dgq, dgk, dsink, dbqkv, *g_down = _attn_bwd(qkv, dmixed, p["gq2"], p["gk2"], p["sinks"], [dw_down] if scatter else [])
    dw_in = jnp.concatenate([_tn_matmul(h1[None], dqkv[None], "dw_qkv")[0], _tn_matmul(h1[None], dcin[None], "dw_cin")[0]], axis=1)
    dw_in = dw_in.reshape(D_MODEL, N_DEV, -1).transpose(1, 0, 2)
    grad_x, dg_mix, *g_in_out = _in_bwd(dqkv, dcin, p["w_qkv"], p["w_cin"], x, dx2, p["g_mix"], [dw_in, dw_out] if scatter else [])
    if scatter:
        big = {"w_up": g_up[0], "w_down": g_down[0], "w_in": g_in_out[0], "w_out": g_in_out[1]}
    else:
        big = {"w_up": dw_up, "w_down": dw_down, "w_in": dw_in, "w_out": dw_out}
    small = dict(dg_mix=dg_mix, dbqkv=dbqkv, dbcin=dbcin, dgq=dgq, dgk=dgk, dsink=dsink, dcw=dcw, dcb=dcb, dcgain=dcgain,
                 dcbias=dcbias, db_out=db_out, dg_ffn=dg_ffn, dfw=dfw, dfb=dfb)
    return loss, grad_x, big, small


BIG = ("w_in", "w_out", "w_up", "w_down")
SMALL = ("mix_norm_gain", "b_in", "q_norm_gain", "k_norm_gain", "attn_sinks", "conv_dw_w", "conv_dw_b",
         "conv_norm_gain", "conv_norm_bias", "b_out", "ffn_norm_gain", "ffn_dw_w", "ffn_dw_b")
ORDER = ("mix_norm_gain", "w_in", "b_in", "q_norm_gain", "k_norm_gain", "attn_sinks", "conv_dw_w", "conv_dw_b",
         "conv_norm_gain", "conv_norm_bias", "w_out", "b_out", "ffn_norm_gain", "w_up", "ffn_dw_w", "ffn_dw_b", "w_down")


def kernel(x, mix_norm_gain, w_in, b_in, q_norm_gain, k_norm_gain, attn_sinks, conv_dw_w, conv_dw_b, conv_norm_gain, conv_norm_bias, w_out, b_out, ffn_norm_gain, w_up, ffn_dw_w, ffn_dw_b, w_down, loss_target, m_mix_norm_gain, m_w_in, m_b_in, m_q_norm_gain, m_k_norm_gain, m_attn_sinks, m_conv_dw_w, m_conv_dw_b, m_conv_norm_gain, m_conv_norm_bias, m_w_out, m_b_out, m_ffn_norm_gain, m_w_up, m_ffn_dw_w, m_ffn_dw_b, m_w_down, v_mix_norm_gain, v_w_in, v_b_in, v_q_norm_gain, v_k_norm_gain, v_attn_sinks, v_conv_dw_w, v_conv_dw_b, v_conv_norm_gain, v_conv_norm_bias, v_w_out, v_b_out, v_ffn_norm_gain, v_w_up, v_ffn_dw_w, v_ffn_dw_b, v_w_down):
    w = dict(mix_norm_gain=mix_norm_gain, w_in=w_in, b_in=b_in, q_norm_gain=q_norm_gain, k_norm_gain=k_norm_gain,
             attn_sinks=attn_sinks, conv_dw_w=conv_dw_w, conv_dw_b=conv_dw_b, conv_norm_gain=conv_norm_gain,
             conv_norm_bias=conv_norm_bias, w_out=w_out, b_out=b_out, ffn_norm_gain=ffn_norm_gain, w_up=w_up,
             ffn_dw_w=ffn_dw_w, ffn_dw_b=ffn_dw_b, w_down=w_down)
    m = dict(mix_norm_gain=m_mix_norm_gain, w_in=m_w_in, b_in=m_b_in, q_norm_gain=m_q_norm_gain, k_norm_gain=m_k_norm_gain,
             attn_sinks=m_attn_sinks, conv_dw_w=m_conv_dw_w, conv_dw_b=m_conv_dw_b, conv_norm_gain=m_conv_norm_gain,
             conv_norm_bias=m_conv_norm_bias, w_out=m_w_out, b_out=m_b_out, ffn_norm_gain=m_ffn_norm_gain, w_up=m_w_up,
             ffn_dw_w=m_ffn_dw_w, ffn_dw_b=m_ffn_dw_b, w_down=m_w_down)
    v = dict(mix_norm_gain=v_mix_norm_gain, w_in=v_w_in, b_in=v_b_in, q_norm_gain=v_q_norm_gain, k_norm_gain=v_k_norm_gain,
             attn_sinks=v_attn_sinks, conv_dw_w=v_conv_dw_w, conv_dw_b=v_conv_dw_b, conv_norm_gain=v_conv_norm_gain,
             conv_norm_bias=v_conv_norm_bias, w_out=v_w_out, b_out=v_b_out, ffn_norm_gain=v_ffn_norm_gain, w_up=v_w_up,
             ffn_dw_w=v_ffn_dw_w, ffn_dw_b=v_ffn_dw_b, w_down=v_w_down)
    s = x.shape[1]

    wi8, cw8, fw8 = _allgather([w_in, conv_dw_w, ffn_dw_w], [BF16, F32, F32])
    w_in_full = wi8.transpose(1, 0, 2).reshape(D_MODEL, QKV_COLS + CIN_COLS)
    lane_pad = ((0, 0), (0, 0), (0, FF_PADDED - FF_CHUNK))
    p = {
        "g_mix": mix_norm_gain.reshape(1, -1), "w_qkv": w_in_full[:, :QKV_COLS], "w_cin": w_in_full[:, QKV_COLS:],
        "b_qkv": b_in[:QKV_COLS].reshape(1, -1), "b_cin": b_in[QKV_COLS:].reshape(1, -1),
        "gq2": jnp.tile(q_norm_gain, 2).reshape(1, -1), "gk2": jnp.tile(k_norm_gain, 2).reshape(1, -1), "sinks": attn_sinks,
        "cw8": jnp.repeat(cw8.transpose(1, 0, 2).reshape(CONV_KERNEL, CONV_WIDTH), SUBLANES, axis=0),
        "cb": conv_dw_b.reshape(1, -1), "cgain": conv_norm_gain.reshape(1, -1), "cbias": conv_norm_bias.reshape(1, -1),
        "b_out": b_out.reshape(1, -1), "g_ffn": ffn_norm_gain.reshape(1, -1),
        "fw": jnp.pad(fw8, lane_pad), "fb": jnp.pad(ffn_dw_b.reshape(N_DEV, 1, FF_CHUNK), lane_pad),
    }

    saved, (wu8,), (wo8, wd8) = _token_mixing(x[0], p, [w_up], [w_out, w_down])
    p.update(w_out=wo8.reshape(D_MODEL, D_MODEL), w_up=wu8, w_down=wd8.reshape(N_FF_PAIRS, FF_CHUNK, D_MODEL))
    loss, grad_x, big, small = _rest_of_step(x[0], loss_target[0], p, saved, True)

    g = dict(big)
    gpack = _small_exchange(_pack_small(small), True, "allreduce_small")

    delta, new_m, new_v = {}, {}, {}
    for n in BIG:
        delta[n], new_m[n], new_v[n] = _adamw(w[n], g[n], m[n], v[n], "adamw_" + n)

    def view(a):
        return a if a.ndim == 2 else (a.reshape(-1, LANES) if a.size % LANES == 0 else a.reshape(1, -1))

    small_out = _adamw_small(gpack, *[{n: view(d[n]) for n in SMALL} for d in (w, m, v)])
    for n in SMALL:
        g[n], delta[n], new_m[n], new_v[n] = [a.reshape(w[n].shape) for a in small_out[n]]

    total = lax.psum(loss[0, 0], ("x", "y", "c"))
    return (total, grad_x.reshape(1, s, D_MODEL), *[g[n] for n in ORDER], *[delta[n] for n in ORDER],
            *[new_m[n] for n in ORDER], *[new_v[n] for n in ORDER])
```

```python
import functools
import math

import jax
import jax.numpy as jnp
from jax import lax
from jax.experimental import pallas as pl
from jax.experimental.pallas import tpu as pltpu

F32 = jnp.float32
BF16 = jnp.bfloat16

D_MODEL = 1024
HEAD_DIM = 64
N_Q_HEADS = 8
N_KV_HEADS = 2
Q_COLS = 512
KV_COLS = 128
QKV_COLS = Q_COLS + 2 * KV_COLS
CONV_WIDTH = 512
CIN_COLS = 2 * CONV_WIDTH
CONV_KERNEL = 31
CONV_HALO = 32
D_FF = 2816
N_DEV = 8
FF_CHUNK = 2 * D_FF // N_DEV
N_FF_PAIRS = N_DEV // 2
ATT_BLOCK = 128
EPS = 1e-6
NEG_INF = -1e30
SLOPES = [float(2.0 ** (-8.0 * (h + 1.0) / N_Q_HEADS)) for h in range(N_Q_HEADS)]

ADAM_LR = 0.001
ADAM_B1 = 0.9
ADAM_B2 = 0.999
ADAM_EPS = 1e-08
ADAM_WD = 0.01
ADAM_STEP = 10

LANES = 128
SUBLANES = 8
VMEM_LIMIT = 56 * 1024 * 1024
MESH = pl.DeviceIdType.MESH


def _cparams(*sem, **kw):
    return pltpu.CompilerParams(dimension_semantics=sem or None, vmem_limit_bytes=VMEM_LIMIT, **kw)


def _resident(shape):
    nd = len(shape)
    return pl.BlockSpec(shape, lambda *_: (0,) * nd, pipeline_mode=pl.Buffered(1))


def _dot(a, b):
    return jnp.dot(a, b, preferred_element_type=F32)


def _dot_nt(a, b):
    return lax.dot_general(a, b, (((1,), (1,)), ((), ())), preferred_element_type=F32)


def _dot_tn(a, b):
    return lax.dot_general(a, b, (((0,), (0,)), ((), ())), preferred_element_type=F32)


def _sigmoid(x):
    return 1.0 / (1.0 + jnp.exp(-x))


def _lo_mask(shape):
    return lax.broadcasted_iota(jnp.int32, shape, len(shape) - 1) % LANES < HEAD_DIM


def _half_sums(t, lo):
    s_lo = jnp.sum(jnp.where(lo, t, 0.0), axis=-1, keepdims=True)
    s_hi = jnp.sum(jnp.where(lo, 0.0, t), axis=-1, keepdims=True)
    return jnp.where(lo, s_lo, s_hi)


def _head_norm(t, lo):
    r = lax.rsqrt(_half_sums(t * t, lo) * (1.0 / HEAD_DIM) + EPS)
    return t * r, r


def _head_norm_bwd(dn, n, r, lo):
    return r * (dn - n * (_half_sums(dn * n, lo) * (1.0 / HEAD_DIM)))


def _tile(s):
    return min(512, s)


TN_TOKENS = 2048
FF_COLS = ((0, 256), (256, 512), (512, 704))


def _position():
    return lax.axis_index("x"), lax.axis_index("y"), lax.axis_index("c")


def _dev_index(px, py, pc):
    return 4 * px + 2 * py + pc


def _flip(v, bit):
    return 1 - v if bit else v


OTHER_CHIPS = ((1, 0), (0, 1), (1, 1))
N_GATHER_COPIES = 1 + 2 * len(OTHER_CHIPS)


class _Gather:
    def __init__(self, shard_ref, out_ref, cast_buf, send_sems, recv_sems, local_sem):
        self.shard, self.out, self.buf = shard_ref, out_ref, cast_buf
        self.send_sems, self.recv_sems, self.local_sem = send_sems, recv_sems, local_sem
        x, y, c = _position()
        self.c = c
        self.me, self.sibling = (x, y, c), (x, y, 1 - c)
        self.chips = [(_flip(x, fx), _flip(y, fy)) for fx, fy in OTHER_CHIPS]

    def _copy(self, k, block, to, from_buf=False):
        rows = self.out.at[_dev_index(*block)]
        return pltpu.make_async_remote_copy(src_ref=self.buf if from_buf else rows, dst_ref=rows,
                                            send_sem=self.send_sems.at[k], recv_sem=self.recv_sems.at[k],
                                            device_id=to, device_id_type=MESH)

    def _local(self):
        return pltpu.make_async_copy(self.buf, self.out.at[_dev_index(*self.me)], self.local_sem)

    def start(self):
        self.buf[...] = self.shard[...].astype(self.buf.dtype)
        self._local().start()
        for j, chip in enumerate(self.chips):
            self._copy(1 + j, self.me, (*chip, self.c), from_buf=True).start()
        self._copy(0, self.me, self.sibling, from_buf=True).start()

    def forward(self):
        for j, chip in enumerate(self.chips):
            self._copy(1 + j, (*chip, self.c), self.me).wait_recv()
            self._copy(1 + len(self.chips) + j, (*chip, self.c), self.sibling).start()

    def finish(self):
        self._copy(0, self.sibling, self.me).wait_recv()
        for j, chip in enumerate(self.chips):
            self._copy(1 + len(self.chips) + j, (*chip, 1 - self.c), self.me).wait_recv()
        for k in range(N_GATHER_COPIES):
            self._copy(k, self.me, self.sibling).wait_send()
        self._local().wait()


def _gather_specs(shards):
    whole = [pl.BlockSpec(w.shape, lambda *_, nd=w.ndim: (0,) * nd, pipeline_mode=pl.Buffered(1)) for w in shards]
    outs = [pl.BlockSpec(memory_space=pl.ANY) for _ in shards]
    shapes = [jax.ShapeDtypeStruct((N_DEV,) + w.shape, BF16) for w in shards]
    scratch = []
    for w in shards:
        scratch += [pltpu.VMEM(w.shape, BF16), pltpu.SemaphoreType.DMA((N_GATHER_COPIES,)),
                    pltpu.SemaphoreType.DMA((N_GATHER_COPIES,)), pltpu.SemaphoreType.DMA(())]
    return whole, outs, shapes, scratch


def _run_gathers(gathers, step, n_steps):
    @pl.when(step == 0)
    def _():
        for g in gathers:
            g.start()

    @pl.when(step == 3 * n_steps // 4)
    def _():
        for g in gathers:
            g.forward()

    @pl.when(step == n_steps - 1)
    def _():
        for g in gathers:
            g.finish()


class _ReduceScatter:
    def __init__(self, g_ref, out_ref, stage, send_a, recv_a, send_b, recv_b, sa_send, sa_recv, sb_send, sb_recv):
        self.g, self.out, self.stage = g_ref, out_ref, stage
        self.send_a, self.recv_a, self.send_b, self.recv_b = send_a, recv_a, send_b, recv_b
        self.sems = (sa_send, sa_recv, sb_send, sb_recv)
        x, y, c = _position()
        self.c, self.sibling = c, (x, y, 1 - c)
        self.chips = [(x, y)] + [(_flip(x, fx), _flip(y, fy)) for fx, fy in OTHER_CHIPS]

    def _copy_a(self, j):
        return pltpu.make_async_remote_copy(src_ref=self.send_a.at[j], dst_ref=self.recv_a.at[j], send_sem=self.sems[0].at[j],
                                            recv_sem=self.sems[1].at[j], device_id=self.sibling, device_id_type=MESH)

    def _copy_b(self, j):
        return pltpu.make_async_remote_copy(src_ref=self.send_b.at[j], dst_ref=self.recv_b.at[j], send_sem=self.sems[2].at[j],
                                            recv_sem=self.sems[3].at[j], device_id=(*self.chips[1 + j], self.c),
                                            device_id_type=MESH)

    def start(self):
        for j, chip in enumerate(self.chips):
            pltpu.sync_copy(self.g.at[_dev_index(*chip, 1 - self.c)], self.stage)
            self.send_a[j] = self.stage[...].astype(BF16)
            self._copy_a(j).start()

    def middle(self):
        for j, chip in enumerate(self.chips):
            pltpu.sync_copy(self.g.at[_dev_index(*chip, self.c)], self.stage)
            self._copy_a(j).wait_recv()
            part = self.stage[...] + self.recv_a[j].astype(F32)
            if j == 0:
                self.out[...] = part
            else:
                self.send_b[j - 1] = part.astype(BF16)
                self._copy_b(j - 1).start()

    def finish(self):
        for j in range(len(OTHER_CHIPS)):
            self._copy_b(j).wait_recv()
            self.out[...] += self.recv_b[j].astype(F32)
        for j in range(len(self.chips)):
            self._copy_a(j).wait_send()
        for j in range(len(OTHER_CHIPS)):
            self._copy_b(j).wait_send()


N_SCATTER_SCRATCH = 9


def _scatter_specs(g8s):
    na, nb = 1 + len(OTHER_CHIPS), len(OTHER_CHIPS)
    ins = [pl.BlockSpec(memory_space=pl.ANY) for _ in g8s]
    outs = [pl.BlockSpec(g.shape[1:], lambda *_: (0, 0)) for g in g8s]
    shapes = [jax.ShapeDtypeStruct(g.shape[1:], F32) for g in g8s]
    scratch = []
    for g in g8s:
        blk = g.shape[1:]
        scratch += [pltpu.VMEM(blk, F32), pltpu.VMEM((na,) + blk, BF16), pltpu.VMEM((na,) + blk, BF16),
                    pltpu.VMEM((nb,) + blk, BF16), pltpu.VMEM((nb,) + blk, BF16),
                    pltpu.SemaphoreType.DMA((na,)), pltpu.SemaphoreType.DMA((na,)),
                    pltpu.SemaphoreType.DMA((nb,)), pltpu.SemaphoreType.DMA((nb,))]
    return ins, outs, shapes, scratch


def _run_scatters(scatters, step, n_steps):
    @pl.when(step == 0)
    def _():
        for r in scatters:
            r.start()

    @pl.when(step == min(max(1, n_steps // 4), n_steps - 1))
    def _():
        for r in scatters:
            r.middle()

    @pl.when(step == n_steps - 1)
    def _():
        for r in scatters:
            r.finish()


def _mix_proj(x, g_mix, w_qkv, w_cin, b_qkv, b_cin):
    s = x.shape[0]
    tm = _tile(s)

    def body(x_ref, g_ref, wq_ref, wc_ref, bq_ref, bc_ref, qkv_ref, cin_ref, h1_ref):
        xv = x_ref[...]
        r = lax.rsqrt(jnp.mean(xv * xv, axis=-1, keepdims=True) + EPS)
        h = (xv * r * g_ref[...]).astype(BF16)
        h1_ref[...] = h
        qkv_ref[...] = _dot(h, wq_ref[...]) + bq_ref[...]
        cin_ref[...] = _dot(h, wc_ref[...]) + bc_ref[...]

    return pl.pallas_call(
        body, grid=(s // tm,),
        in_specs=[pl.BlockSpec((tm, D_MODEL), lambda i: (i, 0)), _resident((1, D_MODEL)),
                  _resident((D_MODEL, QKV_COLS)), _resident((D_MODEL, CIN_COLS)),
                  _resident((1, QKV_COLS)), _resident((1, CIN_COLS))],
        out_specs=[pl.BlockSpec((tm, QKV_COLS), lambda i: (i, 0)), pl.BlockSpec((tm, CIN_COLS), lambda i: (i, 0)),
                   pl.BlockSpec((tm, D_MODEL), lambda i: (i, 0))],
        out_shape=[jax.ShapeDtypeStruct((s, QKV_COLS), F32), jax.ShapeDtypeStruct((s, CIN_COLS), F32),
                   jax.ShapeDtypeStruct((s, D_MODEL), BF16)],
        compiler_params=_cparams("parallel"), name="mix_proj")(x, g_mix, w_qkv, w_cin, b_qkv, b_cin)


def _kv_variants(kv_all, gk2, lo):
    k_all = kv_all[:, :LANES]
    v_all = kv_all[:, LANES:]
    kn_pre, rk = _head_norm(k_all, lo)
    kn = kn_pre * gk2
    kr = pltpu.roll(kn, HEAD_DIM, 1)
    vr = pltpu.roll(v_all, HEAD_DIM, 1)
    zero = jnp.zeros_like(kn)
    k_lo = [jnp.where(lo, kn, zero).astype(BF16), jnp.where(lo, kr, zero).astype(BF16)]
    k_hi = [jnp.where(lo, zero, kr).astype(BF16), jnp.where(lo, zero, kn).astype(BF16)]
    v_lo = [jnp.where(lo, v_all, zero).astype(BF16), jnp.where(lo, vr, zero).astype(BF16)]
    v_hi = [jnp.where(lo, zero, vr).astype(BF16), jnp.where(lo, zero, v_all).astype(BF16)]
    return k_lo, k_hi, v_lo, v_hi, kn_pre, rk


def _att_consts(first_tile, b):
    rows = 2 * ATT_BLOCK
    qi = lax.broadcasted_iota(jnp.int32, (rows, 2 * ATT_BLOCK), 0) % ATT_BLOCK
    kj = lax.broadcasted_iota(jnp.int32, (rows, 2 * ATT_BLOCK), 1)
    rel = qi + ATT_BLOCK - kj
    valid = (rel >= 0) & (rel < ATT_BLOCK)
    if b == 0:
        valid = valid & ((kj >= ATT_BLOCK) | jnp.logical_not(first_tile))
    return rel.astype(F32), valid


def _row_const(va, vb):
    top = lax.broadcasted_iota(jnp.int32, (2 * ATT_BLOCK, 1), 0) < ATT_BLOCK
    return jnp.where(top, va, vb)


def _probs(q2, k_op, rel, valid, slope, sink):
    sc = _dot_nt(q2, k_op) * (1.0 / math.sqrt(HEAD_DIM)) - slope * rel
    sc = jnp.where(valid, sc, NEG_INF)
    m = jnp.maximum(jnp.max(sc, axis=-1, keepdims=True), sink)
    p = jnp.exp(sc - m)
    e_sink = jnp.exp(sink - m)
    inv = 1.0 / (jnp.sum(p, axis=-1, keepdims=True) + e_sink)
    return p * inv, e_sink * inv


def _attn_fwd(qkv, gq2, gk2, sinks, shards):
    s = qkv.shape[0]
    tq = _tile(s)
    nb = tq // ATT_BLOCK
    ng = len(shards)
    g_in, g_out, g_shape, g_scratch = _gather_specs(shards)

    def body(q_ref, kv_ref, kvp_ref, gq_ref, gk_ref, sink_ref, *rest):
        out_ref = rest[ng]
        i = pl.program_id(0)
        _run_gathers([_Gather(rest[a], rest[ng + 1 + a], *rest[2 * ng + 1 + 4 * a:2 * ng + 5 + 4 * a]) for a in range(ng)],
                     i, s // tq)
        lo = _lo_mask((1, LANES))
        kv_all = jnp.concatenate([kvp_ref[...], kv_ref[...]], axis=0)
        k_lo, k_hi, v_lo, v_hi, _, _ = _kv_variants(kv_all, gk_ref[...], lo)
        for b in range(nb):
            rel, valid = _att_consts(i == 0, b)
            rows = slice(b * ATT_BLOCK, (b + 1) * ATT_BLOCK)
            keys = slice(b * ATT_BLOCK, (b + 2) * ATT_BLOCK)
            for kvh in range(N_KV_HEADS):
                pairs = (2 * kvh, 2 * kvh + 1)
                q2 = jnp.concatenate([q_ref[rows, p * LANES:(p + 1) * LANES] for p in pairs], axis=0)
                qn, _ = _head_norm(q2, lo)
                q2 = (qn * gq_ref[...]).astype(BF16)
                out = None
                for odd, (k_op, v_op) in enumerate(((k_lo[kvh][keys], v_lo[kvh][keys]), (k_hi[kvh][keys], v_hi[kvh][keys]))):
                    ha, hb = 2 * pairs[0] + odd, 2 * pairs[1] + odd
                    p, _ = _probs(q2, k_op, rel, valid, _row_const(SLOPES[ha], SLOPES[hb]),
                                  _row_const(sink_ref[ha], sink_ref[hb]))
                    o = _dot(p.astype(BF16), v_op)
                    out = o if out is None else out + o
                for n, p in enumerate(pairs):
                    out_ref[rows, p * LANES:(p + 1) * LANES] = out[n * ATT_BLOCK:(n + 1) * ATT_BLOCK].astype(BF16)

    return pl.pallas_call(
        body, grid=(s // tq,),
        in_specs=[pl.BlockSpec((tq, Q_COLS), lambda i: (i, 0)),
                  pl.BlockSpec((tq, 2 * KV_COLS), lambda i: (i, 2)),
                  pl.BlockSpec((ATT_BLOCK, 2 * KV_COLS), lambda i: (jnp.maximum(i * nb - 1, 0), 2)),
                  _resident((1, LANES)), _resident((1, LANES)),
                  pl.BlockSpec(memory_space=pltpu.SMEM)] + g_in,
        out_specs=[pl.BlockSpec((tq, Q_COLS), lambda i: (i, 0))] + g_out,
        out_shape=[jax.ShapeDtypeStruct((s, Q_COLS), BF16)] + g_shape,
        scratch_shapes=g_scratch,
        compiler_params=_cparams("arbitrary"), name="attn_fwd")(qkv, qkv, qkv, gq2, gk2, sinks, *shards)


def _group_stats(c1, lo):
    mu = _half_sums(c1, lo) * (1.0 / HEAD_DIM)
    d = c1 - mu
    rstd = lax.rsqrt(_half_sums(d * d, lo) * (1.0 / HEAD_DIM) + EPS)
    return d * rstd, rstd


def _rows(ref, first_row, n):
    return ref[pl.ds(first_row, n, stride=1), :].reshape(n // SUBLANES, SUBLANES, LANES)


def _conv_fwd(cin, cw8, cb, gain, bias, shards):
    s = cin.shape[0]
    tm = _tile(s)
    rc = 64
    nchunk = CONV_WIDTH // LANES
    lead = CONV_HALO - (CONV_KERNEL - 1)
    ng = len(shards)
    g_in, g_out, g_shape, g_scratch = _gather_specs(shards)

    def body(cin_ref, cw_ref, cb_ref, gain_ref, bias_ref, *rest):
        c3_ref, c1_ref, ext_ref = rest[ng], rest[ng + 1], rest[2 * ng + 2]
        _run_gathers([_Gather(rest[a], rest[ng + 2 + a], *rest[2 * ng + 3 + 4 * a:2 * ng + 7 + 4 * a]) for a in range(ng)],
                     pl.program_id(0), s // tm)

        @pl.when(pl.program_id(0) == 0)
        def _():
            ext_ref[:, 0:CONV_HALO, :] = jnp.zeros((nchunk, CONV_HALO, LANES), F32)

        lo = _lo_mask((1, LANES))
        for cc in range(nchunk):
            cols = slice(cc * LANES, (cc + 1) * LANES)
            gcols = slice(CONV_WIDTH + cc * LANES, CONV_WIDTH + (cc + 1) * LANES)
            ext_ref[cc, CONV_HALO:CONV_HALO + tm, :] = cin_ref[:, cols] * _sigmoid(cin_ref[:, gcols])
            ext = ext_ref.at[cc]
            for r in range(tm // rc):
                rows = slice(r * rc, (r + 1) * rc)
                acc = jnp.zeros((rc // SUBLANES, SUBLANES, LANES), F32)
                for k in range(CONV_KERNEL):
                    acc = acc + cw_ref[k * SUBLANES:(k + 1) * SUBLANES, cols][None] * _rows(ext, r * rc + lead + k, rc)
                c1 = acc.reshape(rc, LANES) + cb_ref[:, cols]
                c1_ref[cc, rows, :] = c1
                nrm, _ = _group_stats(c1, lo)
                c2 = nrm * gain_ref[:, cols] + bias_ref[:, cols]
                c3_ref[rows, cols] = (c2 * _sigmoid(c2)).astype(BF16)
        ext_ref[:, 0:CONV_HALO, :] = ext_ref[:, tm:tm + CONV_HALO, :]

    return pl.pallas_call(
        body, grid=(s // tm,),
        in_specs=[pl.BlockSpec((tm, CIN_COLS), lambda i: (i, 0)), _resident((CONV_KERNEL * SUBLANES, CONV_WIDTH)),
                  _resident((1, CONV_WIDTH)), _resident((1, CONV_WIDTH)), _resident((1, CONV_WIDTH))] + g_in,
        out_specs=[pl.BlockSpec((tm, CONV_WIDTH), lambda i: (i, 0)), pl.BlockSpec((nchunk, tm, LANES), lambda i: (0, i, 0))] + g_out,
        out_shape=[jax.ShapeDtypeStruct((s, CONV_WIDTH), BF16), jax.ShapeDtypeStruct((nchunk, s, LANES), F32)] + g_shape,
        scratch_shapes=[pltpu.VMEM((nchunk, tm + CONV_HALO, LANES), F32)] + g_scratch,
        compiler_params=_cparams("arbitrary"), name="conv_fwd")(cin, cw8, cb, gain, bias, *shards)


def _out_proj(x, attn, c3, wo_a, wo_c, b_out, g_ffn):
    s = x.shape[0]
    tm = _tile(s)

    def body(x_ref, a_ref, c_ref, wa_ref, wc_ref, b_ref, g_ref, x2_ref, h2_ref):
        x2 = x_ref[...] + _dot(a_ref[...], wa_ref[...]) + _dot(c_ref[...], wc_ref[...]) + b_ref[...]
        x2_ref[...] = x2
        r = lax.rsqrt(jnp.mean(x2 * x2, axis=-1, keepdims=True) + EPS)
        h2_ref[...] = (x2 * r * g_ref[...]).astype(BF16)

    return pl.pallas_call(
        body, grid=(s // tm,),
        in_specs=[pl.BlockSpec((tm, D_MODEL), lambda i: (i, 0)), pl.BlockSpec((tm, Q_COLS), lambda i: (i, 0)),
                  pl.BlockSpec((tm, CONV_WIDTH), lambda i: (i, 0)),
                  pl.BlockSpec((Q_COLS, D_MODEL), lambda i: (0, 0), pipeline_mode=pl.Buffered(1)),
                  pl.BlockSpec((CONV_WIDTH, D_MODEL), lambda i: (1, 0), pipeline_mode=pl.Buffered(1)),
                  _resident((1, D_MODEL)), _resident((1, D_MODEL))],
        out_specs=[pl.BlockSpec((tm, D_MODEL), lambda i: (i, 0)), pl.BlockSpec((tm, D_MODEL), lambda i: (i, 0))],
        out_shape=[jax.ShapeDtypeStruct((s, D_MODEL), F32), jax.ShapeDtypeStruct((s, D_MODEL), BF16)],
        compiler_params=_cparams("parallel"), name="out_proj")(x, attn, c3, wo_a, wo_c, b_out, g_ffn)


FF_LANE_CHUNKS = -(-FF_CHUNK // LANES)
FF_PADDED = FF_LANE_CHUNKS * LANES


def _tap(ref, first_row, n):
    return ref[pl.ds(first_row, n, stride=1), :]


def _ffn_fwd(h2, x2, target, w_up, fw, fb, w_down):
    s = h2.shape[0]
    tm = _tile(s)
    hal = SUBLANES
    rc = min(128, tm)

    def body(h_ref, x2_ref, t_ref, wu_ref, fw_ref, fb_ref, wd_ref, up0_ref, act_ref, dy_ref, loss_ref,
             ext_ref, carry_ref, act_buf, y_ref):
        i, ci = pl.program_id(0), pl.program_id(1)

        @pl.when((i == 0) & (ci == 0))
        def _():
            carry_ref[...] = jnp.zeros(carry_ref.shape, F32)
            ext_ref[...] = jnp.zeros(ext_ref.shape, F32)
            act_buf[...] = jnp.zeros(act_buf.shape, BF16)
            loss_ref[...] = jnp.zeros((1, 1), F32)

        @pl.when(ci == 0)
        def _():
            y_ref[...] = x2_ref[...]

        h = h_ref[...]
        ws = (fw_ref[ci], fw_ref[ci + N_FF_PAIRS])
        bs = (fb_ref[ci], fb_ref[ci + N_FF_PAIRS])
        down = None
        for lo_c, hi_c in FF_COLS:
            for half in range(2):
                c = ci + half * N_FF_PAIRS
                u0 = _dot(h, wu_ref[c, :, lo_c:hi_c])
                up0_ref[half, 0, :, lo_c:hi_c] = u0.astype(BF16)
                for j in range(lo_c // LANES, -(-hi_c // LANES)):
                    w = min(LANES, hi_c - j * LANES)
                    ext_ref[half, j, 0:hal, 0:w] = carry_ref[c, :, j * LANES:j * LANES + w]
                    ext_ref[half, j, hal:hal + tm, 0:w] = u0[:, j * LANES - lo_c:j * LANES - lo_c + w]
                carry_ref[c, :, lo_c:hi_c] = u0[tm - hal:tm, :]
            for j in range(lo_c // LANES, -(-hi_c // LANES)):
                lanes = slice(j * LANES, (j + 1) * LANES)
                for r in range(tm // rc):
                    base = r * rc
                    ups = []
                    for half in range(2):
                        e, w = ext_ref.at[half, j], ws[half]
                        ups.append(w[0:1, lanes] * _tap(e, base + hal - 2, rc) + w[1:2, lanes] * _tap(e, base + hal - 1, rc)
                                   + w[2:3, lanes] * _tap(e, base + hal, rc) + bs[half][:, lanes])
                    g, u = ups
                    act_buf[base:base + rc, lanes] = (g * _sigmoid(g) * u).astype(BF16)
            act = act_buf[:, lo_c:hi_c]
            act_ref[0, :, lo_c:hi_c] = act
            part = _dot(act, wd_ref[ci, lo_c:hi_c, :])
            down = part if down is None else down + part
        y_ref[...] += down

        @pl.when(ci == N_FF_PAIRS - 1)
        def _():
            e = y_ref[...] - t_ref[...]
            dy_ref[...] = e * (1.0 / D_MODEL)
            loss_ref[...] += (0.5 / D_MODEL) * jnp.sum(e * e).reshape(1, 1)

    tok = lambda i, ci: (i, 0)
    return pl.pallas_call(
        body, grid=(s // tm, N_FF_PAIRS),
        in_specs=[pl.BlockSpec((tm, D_MODEL), tok), pl.BlockSpec((tm, D_MODEL), tok), pl.BlockSpec((tm, D_MODEL), tok),
                  _resident((N_DEV, D_MODEL, FF_CHUNK)), _resident((N_DEV, 3, FF_PADDED)), _resident((N_DEV, 1, FF_PADDED)),
                  _resident((N_FF_PAIRS, FF_CHUNK, D_MODEL))],
        out_specs=[pl.BlockSpec((2, 1, tm, FF_CHUNK), lambda i, ci: (0, ci, i, 0)),
                   pl.BlockSpec((1, tm, FF_CHUNK), lambda i, ci: (ci, i, 0)),
                   pl.BlockSpec((tm, D_MODEL), tok), pl.BlockSpec((1, 1), lambda i, ci: (0, 0))],
        out_shape=[jax.ShapeDtypeStruct((2, N_FF_PAIRS, s, FF_CHUNK), BF16),
                   jax.ShapeDtypeStruct((N_FF_PAIRS, s, FF_CHUNK), BF16), jax.ShapeDtypeStruct((s, D_MODEL), F32),
                   jax.ShapeDtypeStruct((1, 1), F32)],
        scratch_shapes=[pltpu.VMEM((2, FF_LANE_CHUNKS, tm + hal, LANES), F32), pltpu.VMEM((N_DEV, hal, FF_CHUNK), F32),
                        pltpu.VMEM((tm, FF_PADDED), BF16), pltpu.VMEM((tm, D_MODEL), F32)],
        compiler_params=_cparams("arbitrary", "arbitrary"), name="ffn_fwd")(h2, x2, target, w_up, fw, fb, w_down)


def _ffn_bwd(dy, up0, w_up, fw, fb, w_down):
    s = dy.shape[0]
    tm = _tile(s)
    nt = s // tm
    hal = 2 * SUBLANES
    nxt = SUBLANES
    rc = min(128, tm)

    def body(dy_ref, up0_ref, up0h_ref, wu_ref, fw_ref, fb_ref, wd_ref,
             dup0_ref, dh2_ref, dfw_ref, dfb_ref, ext_ref, dext_ref, carry_ref, dact_buf, dup0_buf):
        i, ci = pl.program_id(0), pl.program_id(1)
        t = nt - 1 - i

        @pl.when((i == 0) & (ci == 0))
        def _():
            for ref in (carry_ref, dfw_ref, dfb_ref, ext_ref, dext_ref, dact_buf):
                ref[...] = jnp.zeros(ref.shape, F32)
            dup0_buf[...] = jnp.zeros(dup0_buf.shape, BF16)

        @pl.when(ci == 0)
        def _():
            dh2_ref[...] = jnp.zeros(dh2_ref.shape, F32)

        ws = (fw_ref[ci], fw_ref[ci + N_FF_PAIRS])
        bs = (fb_ref[ci], fb_ref[ci + N_FF_PAIRS])
        fold = lambda v: jnp.sum(v.reshape(rc // SUBLANES, SUBLANES, LANES), axis=0)
        half_rows = (slice(0, tm // 2), slice(tm // 2, tm))
        n_grp = len(FF_COLS)

        def dact_slices(grp, pair):
            lo_c, hi_c = FF_COLS[grp]

            def make(rows):
                def run():
                    dact_buf[rows, lo_c:hi_c] = _dot_nt(dy_ref[rows, :].astype(BF16), wd_ref[pair, lo_c:hi_c, :])
                return run
            return [make(rows) for rows in half_rows]

        def dh2_slices(grp, pair):
            lo_c, hi_c = FF_COLS[grp]

            def make(half, rows):
                def run():
                    c = pair + half * N_FF_PAIRS
                    dh2_ref[rows, :] += _dot_nt(dup0_buf[half, rows, lo_c:hi_c], wu_ref[c, :, lo_c:hi_c])
                return run
            return [make(half, rows) for half in range(2) for rows in half_rows]

        def vector_blocks(grp):
            lo_c, hi_c = FF_COLS[grp]
            chunks = range(lo_c // LANES, -(-hi_c // LANES))
            blocks = []

            def stage():
                for half in range(2):
                    c = ci + half * N_FF_PAIRS
                    prev = jnp.where(t > 0, up0h_ref[half, 0, :, lo_c:hi_c].astype(F32), 0.0)
                    cur = up0_ref[half, 0, :, lo_c:hi_c].astype(F32)
                    for j in chunks:
                        w = min(LANES, hi_c - j * LANES)
                        cols = slice(j * LANES - lo_c, j * LANES - lo_c + w)
                        ext_ref[half, j, 0:hal, 0:w] = prev[:, cols]
                        ext_ref[half, j, hal:hal + tm, 0:w] = cur[:, cols]
                        dext_ref[half, j, tm:tm + nxt, 0:w] = carry_ref[c, :, j * LANES:j * LANES + w]
            blocks.append(stage)
            for j in chunks:
                lanes = slice(j * LANES, (j + 1) * LANES)
                acc = [jnp.zeros((SUBLANES, LANES), F32)] * 8

                def grads(r, lanes=lanes, j=j, acc=acc):
                    base = r * rc
                    taps, ups = [], []
                    for half in range(2):
                        e, w = ext_ref.at[half, j], ws[half]
                        x = [_tap(e, base + hal - 2 + k, rc) for k in range(3)]
                        taps.append(x)
                        ups.append(w[0:1, lanes] * x[0] + w[1:2, lanes] * x[1] + w[2:3, lanes] * x[2] + bs[half][:, lanes])
                    g, u = ups
                    sg = _sigmoid(g)
                    dact = dact_buf[base:base + rc, lanes]
                    ds = (dact * u * (sg * (1.0 + g * (1.0 - sg))), dact * (g * sg))
                    for half in range(2):
                        dext_ref[half, j, base:base + rc, :] = ds[half]
                        acc[4 * half] = acc[4 * half] + fold(ds[half])
                        for k in range(3):
                            acc[4 * half + 1 + k] = acc[4 * half + 1 + k] + fold(ds[half] * taps[half][k])

                def sums(lanes=lanes, acc=acc):
                    for half in range(2):
                        c = ci + half * N_FF_PAIRS
                        dfb_ref[c, :, lanes] += jnp.sum(acc[4 * half], axis=0, keepdims=True)
                        dfw_ref[c, :, lanes] += jnp.concatenate(
                            [jnp.sum(acc[4 * half + 1 + k], axis=0, keepdims=True) for k in range(3)], axis=0)

                def conv_back(r, lanes=lanes, j=j):
                    base = r * rc
                    for half in range(2):
                        d, w = dext_ref.at[half, j], ws[half]
                        dup0 = w[2:3, lanes] * _tap(d, base, rc) + w[1:2, lanes] * _tap(d, base + 1, rc) + w[0:1, lanes] * _tap(d, base + 2, rc)
                        dup0_buf[half, base:base + rc, lanes] = dup0.astype(BF16)

                blocks += [functools.partial(grads, r) for r in range(tm // rc)] + [sums]
                blocks += [functools.partial(conv_back, r) for r in range(tm // rc)]

            def finish():
                for half in range(2):
                    c = ci + half * N_FF_PAIRS
                    for j in chunks:
                        w = min(LANES, hi_c - j * LANES)
                        carry_ref[c, :, j * LANES:j * LANES + w] = dext_ref[half, j, 0:nxt, 0:w]
                    dup0_ref[half, 0, :, lo_c:hi_c] = dup0_buf[half, :, lo_c:hi_c]
            blocks.append(finish)
            return blocks

        for run in dact_slices(0, ci):
            run()
        for grp in range(n_grp):
            matmuls = (dact_slices(grp + 1, ci) if grp + 1 < n_grp else []) + (dh2_slices(grp - 1, ci) if grp > 0 else [])
            blocks = vector_blocks(grp)
            every = max(1, len(blocks) // (len(matmuls) + 1))
            for n, run in enumerate(blocks):
                run()
                if n % every == every - 1 and matmuls:
                    matmuls.pop(0)()
            for run in matmuls:
                run()
        for run in dh2_slices(n_grp - 1, ci):
            run()

    tok = lambda i, ci: (nt - 1 - i, 0)
    acc = lambda shape: pl.BlockSpec(shape, lambda i, ci: (0,) * len(shape))
    return pl.pallas_call(
        body, grid=(nt, N_FF_PAIRS),
        in_specs=[pl.BlockSpec((tm, D_MODEL), tok),
                  pl.BlockSpec((2, 1, tm, FF_CHUNK), lambda i, ci: (0, ci, nt - 1 - i, 0)),
                  pl.BlockSpec((2, 1, hal, FF_CHUNK), lambda i, ci: (0, ci, jnp.maximum((nt - 1 - i) * (tm // hal) - 1, 0), 0)),
                  _resident((N_DEV, D_MODEL, FF_CHUNK)), _resident((N_DEV, 3, FF_PADDED)), _resident((N_DEV, 1, FF_PADDED)),
                  _resident((N_FF_PAIRS, FF_CHUNK, D_MODEL))],
        out_specs=[pl.BlockSpec((2, 1, tm, FF_CHUNK), lambda i, ci: (0, ci, nt - 1 - i, 0)),
                   pl.BlockSpec((tm, D_MODEL), tok), acc((N_DEV, 3, FF_PADDED)), acc((N_DEV, 1, FF_PADDED))],
        out_shape=[jax.ShapeDtypeStruct((2, N_FF_PAIRS, s, FF_CHUNK), BF16), jax.ShapeDtypeStruct((s, D_MODEL), F32),
                   jax.ShapeDtypeStruct((N_DEV, 3, FF_PADDED), F32), jax.ShapeDtypeStruct((N_DEV, 1, FF_PADDED), F32)],
        scratch_shapes=[pltpu.VMEM((2, FF_LANE_CHUNKS, tm + hal, LANES), F32), pltpu.VMEM((2, FF_LANE_CHUNKS, tm + nxt, LANES), F32),
                        pltpu.VMEM((N_DEV, nxt, FF_CHUNK), F32), pltpu.VMEM((tm, FF_PADDED), F32),
                        pltpu.VMEM((2, tm, FF_PADDED), BF16)],
        compiler_params=_cparams("arbitrary", "arbitrary"), name="ffn_bwd")(dy, up0, up0, w_up, fw, fb, w_down)


def _ffn_norm_bwd(dh2, dy, x2, g_ffn, w_out):
    s = dy.shape[0]
    tm = _tile(s)

    def body(dh_ref, dy_ref, x2_ref, g_ref, wo_ref, dx2_ref, dmix_ref, dg_ref, dbo_ref):
        @pl.when(pl.program_id(0) == 0)
        def _():
            dg_ref[...] = jnp.zeros(dg_ref.shape, F32)
            dbo_ref[...] = jnp.zeros(dbo_ref.shape, F32)

        x2v = x2_ref[...]
        r = lax.rsqrt(jnp.mean(x2v * x2v, axis=-1, keepdims=True) + EPS)
        n2 = x2v * r
        dh2 = dh_ref[...]
        dg_ref[...] += jnp.sum(dh2 * n2, axis=0, keepdims=True)
        dn = dh2 * g_ref[...]
        dx2 = dy_ref[...] + r * (dn - n2 * jnp.mean(dn * n2, axis=-1, keepdims=True))
        dx2_ref[...] = dx2
        dbo_ref[...] += jnp.sum(dx2, axis=0, keepdims=True)
        dmix_ref[...] = _dot_nt(dx2.astype(BF16), wo_ref[...])

    tok = pl.BlockSpec((tm, D_MODEL), lambda i: (i, 0))
    vec = pl.BlockSpec((1, D_MODEL), lambda i: (0, 0))
    return pl.pallas_call(
        body, grid=(s // tm,),
        in_specs=[tok, tok, tok, _resident((1, D_MODEL)), _resident((D_MODEL, D_MODEL))],
        out_specs=[tok, tok, vec, vec],
        out_shape=[jax.ShapeDtypeStruct((s, D_MODEL), F32), jax.ShapeDtypeStruct((s, D_MODEL), F32),
                   jax.ShapeDtypeStruct((1, D_MODEL), F32), jax.ShapeDtypeStruct((1, D_MODEL), F32)],
        compiler_params=_cparams("arbitrary"), name="ffn_norm_bwd")(dh2, dy, x2, g_ffn, w_out)


def _conv_bwd(dmixed, c1, cin, cw8, gain, bias, g8s):
    ns = len(g8s)
    s_in, s_out, s_shape, s_scratch = _scatter_specs(g8s)
    s = cin.shape[0]
    tm = _tile(s)
    nt = s // tm
    rc = 64
    rn = min(256, tm)
    hal = CONV_HALO
    lead = hal - (CONV_KERNEL - 1)
    nchunk = CONV_WIDTH // LANES

    def body(dc3_ref, dc3n_ref, c1_ref, c1n_ref, cin_ref, cinp_ref, cw_ref, gain_ref, bias_ref, *rest):
        dcin_ref, dcw_ref, dcb_ref, dgain_ref, dbias_ref, dbcin_ref = rest[ns:ns + 6]
        c0_ext, dc1_ext, dcw8 = rest[2 * ns + 6:2 * ns + 9]
        i = pl.program_id(0)
        first, last = i == 0, i == nt - 1
        own = rest[2 * ns + 9:]
        _run_scatters([_ReduceScatter(rest[a], rest[ns + 6 + a], *own[N_SCATTER_SCRATCH * a:N_SCATTER_SCRATCH * (a + 1)])
                       for a in range(ns)], i, nt)

        @pl.when(first)
        def _():
            for ref in (dcw8, dcb_ref, dgain_ref, dbias_ref, dbcin_ref):
                ref[...] = jnp.zeros(ref.shape, F32)

        lo = _lo_mask((1, LANES))

        def norm_bwd(dc3, c1v, cols):
            nrm, rstd = _group_stats(c1v, lo)
            c2 = nrm * gain_ref[:, cols] + bias_ref[:, cols]
            sg = _sigmoid(c2)
            dc2 = dc3 * (sg * (1.0 + c2 * (1.0 - sg)))
            dn = dc2 * gain_ref[:, cols]
            inv = 1.0 / HEAD_DIM
            dc1 = rstd * (dn - _half_sums(dn, lo) * inv - nrm * (_half_sums(dn * nrm, lo) * inv))
            return dc1, dc2, nrm

        def row_sum(v):
            return jnp.sum(v, axis=0, keepdims=True)

        for cc in range(nchunk):
            cols = slice(cc * LANES, (cc + 1) * LANES)
            gcols = slice(CONV_WIDTH + cc * LANES, CONV_WIDTH + (cc + 1) * LANES)
            c0e, d1e = c0_ext.at[cc], dc1_ext.at[cc]
            c0e[0:hal, :] = jnp.where(first, 0.0, cinp_ref[:, cols] * _sigmoid(cinp_ref[:, gcols]))
            dc1n, _, _ = norm_bwd(dc3n_ref[:, cols], c1n_ref[cc], cols)
            d1e[tm:tm + hal, :] = jnp.where(last, 0.0, dc1n)

            for r in range(tm // rn):
                rows = slice(r * rn, (r + 1) * rn)
                c0e[hal + r * rn:hal + (r + 1) * rn, :] = cin_ref[rows, cols] * _sigmoid(cin_ref[rows, gcols])
                dc1, dc2, nrm = norm_bwd(dc3_ref[rows, cols], c1_ref[cc, rows, :], cols)
                d1e[rows, :] = dc1
                dgain_ref[:, cols] += row_sum(dc2 * nrm)
                dbias_ref[:, cols] += row_sum(dc2)
                dcb_ref[:, cols] += row_sum(dc1)
            zero = jnp.zeros((1, LANES), F32)

            for k0 in range(0, CONV_KERNEL, SUBLANES):
                taps = range(k0, min(k0 + SUBLANES, CONV_KERNEL))

                def tap_sums(r, acc, taps=taps):
                    d = _rows(d1e, r * rc, rc)
                    return tuple(a + jnp.sum(d * _rows(c0e, r * rc + lead + k, rc), axis=0) for a, k in zip(acc, taps))

                acc = lax.fori_loop(0, tm // rc, tap_sums, tuple(dcw8[k * SUBLANES:(k + 1) * SUBLANES, cols] for k in taps))
                for a, k in zip(acc, taps):
                    dcw8[k * SUBLANES:(k + 1) * SUBLANES, cols] = a

            def input_grad(r, sums):
                rows = pl.ds(pl.multiple_of(r * rc, rc), rc)
                dc0 = jnp.zeros((rc // SUBLANES, SUBLANES, LANES), F32)
                for k in range(CONV_KERNEL):
                    dc0 = dc0 + cw_ref[k * SUBLANES:(k + 1) * SUBLANES, cols][None] * _rows(d1e, r * rc + CONV_KERNEL - 1 - k, rc)
                dc0 = dc0.reshape(rc, LANES)
                sg = _sigmoid(cin_ref[rows, gcols])
                da = dc0 * sg
                dgate = dc0 * cin_ref[rows, cols] * sg * (1.0 - sg)
                dcin_ref[rows, cols] = da.astype(BF16)
                dcin_ref[rows, gcols] = dgate.astype(BF16)
                return sums[0] + row_sum(da), sums[1] + row_sum(dgate)

            sums = lax.fori_loop(0, tm // rc, input_grad, (zero, zero))
            dbcin_ref[:, cols] += sums[0]
            dbcin_ref[:, gcols] += sums[1]

        @pl.when(last)
        def _():
            for k in range(CONV_KERNEL):
                dcw_ref[k:k + 1, :] = jnp.sum(dcw8[k * SUBLANES:(k + 1) * SUBLANES, :], axis=0, keepdims=True)

    nh = tm // hal
    acc = lambda shape: pl.BlockSpec(shape, lambda i: (0,) * len(shape))
    return pl.pallas_call(
        body, grid=(nt,),
        in_specs=[pl.BlockSpec((tm, CONV_WIDTH), lambda i: (i, 1)),
                  pl.BlockSpec((hal, CONV_WIDTH), lambda i: (jnp.minimum((i + 1) * nh, s // hal - 1), 1)),
                  pl.BlockSpec((nchunk, tm, LANES), lambda i: (0, i, 0)),
                  pl.BlockSpec((nchunk, hal, LANES), lambda i: (0, jnp.minimum((i + 1) * nh, s // hal - 1), 0)),
                  pl.BlockSpec((tm, CIN_COLS), lambda i: (i, 0)),
                  pl.BlockSpec((hal, CIN_COLS), lambda i: (jnp.maximum(i * nh - 1, 0), 0)),
                  _resident((CONV_KERNEL * SUBLANES, CONV_WIDTH)), _resident((1, CONV_WIDTH)), _resident((1, CONV_WIDTH))] + s_in,
        out_specs=[pl.BlockSpec((tm, CIN_COLS), lambda i: (i, 0)), acc((CONV_KERNEL, CONV_WIDTH)), acc((1, CONV_WIDTH)),
                   acc((1, CONV_WIDTH)), acc((1, CONV_WIDTH)), acc((1, CIN_COLS))] + s_out,
        out_shape=[jax.ShapeDtypeStruct((s, CIN_COLS), BF16), jax.ShapeDtypeStruct((CONV_KERNEL, CONV_WIDTH), F32),
                   jax.ShapeDtypeStruct((1, CONV_WIDTH), F32), jax.ShapeDtypeStruct((1, CONV_WIDTH), F32),
                   jax.ShapeDtypeStruct((1, CONV_WIDTH), F32), jax.ShapeDtypeStruct((1, CIN_COLS), F32)] + s_shape,
        scratch_shapes=[pltpu.VMEM((nchunk, tm + hal, LANES), F32), pltpu.VMEM((nchunk, tm + hal, LANES), F32),
                        pltpu.VMEM((CONV_KERNEL * SUBLANES, CONV_WIDTH), F32)] + s_scratch,
        compiler_params=_cparams("arbitrary"), name="conv_bwd")(dmixed, dmixed, c1, c1, cin, cin, cw8, gain, bias, *g8s)


def _attn_bwd(qkv, dmixed, gq2, gk2, sinks, g8s):
    ns = len(g8s)
    s_in, s_out, s_shape, s_scratch = _scatter_specs(g8s)
    s = qkv.shape[0]
    tq = _tile(s)
    nb = tq // ATT_BLOCK
    nt = s // tq

    def body(q_ref, kv_ref, kvp_ref, do_ref, gq_ref, gk_ref, sink_ref, *rest):
        dqkv_ref, dgq_ref, dgk_ref, dsink_ref, dbqkv_ref = rest[ns:ns + 5]
        dk_acc, dv_acc, carry_k, carry_v = rest[2 * ns + 5:2 * ns + 9]
        i = pl.program_id(0)
        t = nt - 1 - i
        own = rest[2 * ns + 9:]
        _run_scatters([_ReduceScatter(rest[a], rest[ns + 5 + a], *own[N_SCATTER_SCRATCH * a:N_SCATTER_SCRATCH * (a + 1)])
                       for a in range(ns)], i, nt)

        @pl.when(i == 0)
        def _():
            for ref in (carry_k, carry_v, dgq_ref, dgk_ref, dsink_ref, dbqkv_ref):
                ref[...] = jnp.zeros(ref.shape, F32)

        lo = _lo_mask((1, LANES))
        lane_id = lax.broadcasted_iota(jnp.int32, (1, LANES), 1)
        kv_all = jnp.concatenate([kvp_ref[...], kv_ref[...]], axis=0)
        k_lo, k_hi, v_lo, v_hi, kn_pre, rk = _kv_variants(kv_all, gk_ref[...], lo)
        for acc_ref, carry in ((dk_acc, carry_k), (dv_acc, carry_v)):
            acc_ref[:, 0:tq, :] = jnp.zeros((N_KV_HEADS, tq, LANES), F32)
            acc_ref[:, tq:tq + ATT_BLOCK, :] = carry[...]
        dsink = jnp.zeros((1, LANES), F32)
        dgq = jnp.zeros((1, LANES), F32)
        gq = gq_ref[...]
        for b in range(nb):
            rel, valid = _att_consts(t == 0, b)
            rows = slice(b * ATT_BLOCK, (b + 1) * ATT_BLOCK)
            keys = slice(b * ATT_BLOCK, (b + 2) * ATT_BLOCK)
            for kvh in range(N_KV_HEADS):
                pairs = (2 * kvh, 2 * kvh + 1)
                q_raw = jnp.concatenate([q_ref[rows, p * LANES:(p + 1) * LANES] for p in pairs], axis=0)
                qn_pre, rq = _head_norm(q_raw, lo)
                q2 = (qn_pre * gq).astype(BF16)
                do2 = jnp.concatenate([do_ref[rows, p * LANES:(p + 1) * LANES] for p in pairs], axis=0).astype(BF16)
                dq2 = jnp.zeros((2 * ATT_BLOCK, LANES), F32)
                for odd, (k_op, v_op) in enumerate(((k_lo[kvh][keys], v_lo[kvh][keys]), (k_hi[kvh][keys], v_hi[kvh][keys]))):
                    ha, hb = 2 * pairs[0] + odd, 2 * pairs[1] + odd
                    p, p_sink = _probs(q2, k_op, rel, valid, _row_const(SLOPES[ha], SLOPES[hb]),
                                       _row_const(sink_ref[ha], sink_ref[hb]))
                    dp = _dot_nt(do2, v_op)
                    delta = jnp.sum(p * dp, axis=-1, keepdims=True)
                    ds = (p * (dp - delta) * (1.0 / math.sqrt(HEAD_DIM))).astype(BF16)
                    dsk = p_sink * delta
                    dsink = dsink - jnp.where(lane_id == ha, jnp.sum(dsk[0:ATT_BLOCK]), 0.0) \
                        - jnp.where(lane_id == hb, jnp.sum(dsk[ATT_BLOCK:]), 0.0)
                    dq2 = dq2 + _dot(ds, k_op)
                    half = lo if odd == 0 else jnp.logical_not(lo)
                    dk_acc[kvh, keys, :] += jnp.where(half, _dot_tn(ds, q2), 0.0)
                    dv_acc[kvh, keys, :] += jnp.where(half, _dot_tn(p.astype(BF16), do2), 0.0)
                dgq = dgq + jnp.sum(dq2 * qn_pre, axis=0, keepdims=True)
                dq_raw = _head_norm_bwd(dq2 * gq, qn_pre, rq, lo)
                for n, p_ in enumerate(pairs):
                    blk = dq_raw[n * ATT_BLOCK:(n + 1) * ATT_BLOCK]
                    dqkv_ref[rows, p_ * LANES:(p_ + 1) * LANES] = blk.astype(BF16)
                    dbqkv_ref[:, p_ * LANES:(p_ + 1) * LANES] += jnp.sum(blk, axis=0, keepdims=True)
        carry_k[...] = dk_acc[:, 0:ATT_BLOCK, :]
        carry_v[...] = dv_acc[:, 0:ATT_BLOCK, :]

        def fold(acc_ref):
            both = []
            for kvh in range(N_KV_HEADS):
                a = acc_ref[kvh, ATT_BLOCK:ATT_BLOCK + tq, :]
                both.append(a + pltpu.roll(a, HEAD_DIM, 1))
            return jnp.where(lo, both[0], both[1])

        dkn = fold(dk_acc)
        dv = fold(dv_acc)
        kn_c, rk_c = kn_pre[ATT_BLOCK:], rk[ATT_BLOCK:]
        dgk_ref[...] += jnp.sum(dkn * kn_c, axis=0, keepdims=True)
        dk_raw = _head_norm_bwd(dkn * gk_ref[...], kn_c, rk_c, lo)
        dqkv_ref[:, Q_COLS:Q_COLS + KV_COLS] = dk_raw.astype(BF16)
        dqkv_ref[:, Q_COLS + KV_COLS:] = dv.astype(BF16)
        dbqkv_ref[:, Q_COLS:Q_COLS + KV_COLS] += jnp.sum(dk_raw, axis=0, keepdims=True)
        dbqkv_ref[:, Q_COLS + KV_COLS:] += jnp.sum(dv, axis=0, keepdims=True)
        dgq_ref[...] += dgq
        dsink_ref[...] += dsink

        @pl.when(i == nt - 1)
        def _():
            for ref in (dgq_ref, dgk_ref):
                v = ref[...]
                ref[...] = v + pltpu.roll(v, HEAD_DIM, 1)

    acc = lambda shape: pl.BlockSpec(shape, lambda i: (0,) * len(shape))
    return pl.pallas_call(
        body, grid=(nt,),
        in_specs=[pl.BlockSpec((tq, Q_COLS), lambda i: (nt - 1 - i, 0)),
                  pl.BlockSpec((tq, 2 * KV_COLS), lambda i: (nt - 1 - i, 2)),
                  pl.BlockSpec((ATT_BLOCK, 2 * KV_COLS), lambda i: (jnp.maximum((nt - 1 - i) * nb - 1, 0), 2)),
                  pl.BlockSpec((tq, Q_COLS), lambda i: (nt - 1 - i, 0)),
                  _resident((1, LANES)), _resident((1, LANES)), pl.BlockSpec(memory_space=pltpu.SMEM)] + s_in,
        out_specs=[pl.BlockSpec((tq, QKV_COLS), lambda i: (nt - 1 - i, 0)), acc((1, LANES)), acc((1, LANES)),
                   acc((1, LANES)), acc((1, QKV_COLS))] + s_out,
        out_shape=[jax.ShapeDtypeStruct((s, QKV_COLS), BF16), jax.ShapeDtypeStruct((1, LANES), F32),
                   jax.ShapeDtypeStruct((1, LANES), F32), jax.ShapeDtypeStruct((1, LANES), F32),
                   jax.ShapeDtypeStruct((1, QKV_COLS), F32)] + s_shape,
        scratch_shapes=[pltpu.VMEM((N_KV_HEADS, tq + ATT_BLOCK, LANES), F32), pltpu.VMEM((N_KV_HEADS, tq + ATT_BLOCK, LANES), F32),
                        pltpu.VMEM((N_KV_HEADS, ATT_BLOCK, LANES), F32), pltpu.VMEM((N_KV_HEADS, ATT_BLOCK, LANES), F32)] + s_scratch,
        compiler_params=_cparams("arbitrary"), name="attn_bwd")(qkv, qkv, qkv, dmixed, gq2, gk2, sinks, *g8s)


def _in_bwd(dqkv, dcin, w_qkv, w_cin, x, dx2, g_mix):
    s = x.shape[0]
    tm = _tile(s)

    def body(dq_ref, dc_ref, wq_ref, wc_ref, x_ref, dx2_ref, g_ref, gx_ref, dg_ref):
        @pl.when(pl.program_id(0) == 0)
        def _():
            dg_ref[...] = jnp.zeros(dg_ref.shape, F32)

        dh = _dot_nt(dq_ref[...], wq_ref[...]) + _dot_nt(dc_ref[...], wc_ref[...])
        xv = x_ref[...]
        r = lax.rsqrt(jnp.mean(xv * xv, axis=-1, keepdims=True) + EPS)
        n = xv * r
        dg_ref[...] += jnp.sum(dh * n, axis=0, keepdims=True)
        dn = dh * g_ref[...]
        gx_ref[...] = dx2_ref[...] + r * (dn - n * jnp.mean(dn * n, axis=-1, keepdims=True))

    return pl.pallas_call(
        body, grid=(s // tm,),
        in_specs=[pl.BlockSpec((tm, QKV_COLS), lambda i: (i, 0)), pl.BlockSpec((tm, CIN_COLS), lambda i: (i, 0)),
                  _resident((D_MODEL, QKV_COLS)), _resident((D_MODEL, CIN_COLS)),
                  pl.BlockSpec((tm, D_MODEL), lambda i: (i, 0)), pl.BlockSpec((tm, D_MODEL), lambda i: (i, 0)),
                  _resident((1, D_MODEL))],
        out_specs=[pl.BlockSpec((tm, D_MODEL), lambda i: (i, 0)), pl.BlockSpec((1, D_MODEL), lambda i: (0, 0))],
        out_shape=[jax.ShapeDtypeStruct((s, D_MODEL), F32), jax.ShapeDtypeStruct((1, D_MODEL), F32)],
        compiler_params=_cparams("arbitrary"), name="in_bwd")(dqkv, dcin, w_qkv, w_cin, x, dx2, g_mix)


def _tn_matmul(a, b, name):
    ga, s, m = a.shape
    gb, _, n = b.shape
    g = max(ga, gb)
    tk = min(TN_TOKENS, s)

    def body(a_ref, b_ref, o_ref):
        @pl.when(pl.program_id(1) == 0)
        def _():
            o_ref[...] = jnp.zeros(o_ref.shape, F32)

        o_ref[0] += _dot_tn(a_ref[0].astype(BF16), b_ref[0].astype(BF16))

    return pl.pallas_call(
        body, grid=(g, s // tk),
        in_specs=[pl.BlockSpec((1, tk, m), (lambda gi, k: (gi, k, 0)) if ga > 1 else (lambda gi, k: (0, k, 0))),
                  pl.BlockSpec((1, tk, n), (lambda gi, k: (gi, k, 0)) if gb > 1 else (lambda gi, k: (0, k, 0)))],
        out_specs=pl.BlockSpec((1, m, n), lambda gi, k: (gi, 0, 0)),
        out_shape=jax.ShapeDtypeStruct((g, m, n), F32),
        compiler_params=_cparams("parallel", "arbitrary"), name=name)(a, b)


def _allgather(shards, dtypes):
    n = len(shards)
    n_copies = 1 + 2 * len(OTHER_CHIPS)

    def body(*refs):
        ins, outs = refs[:n], refs[n:2 * n]
        send_sems, recv_sems = refs[2 * n:]
        x, y, c = _position()
        me, sibling = (x, y, c), (x, y, 1 - c)
        chips = [(_flip(x, fx), _flip(y, fy)) for fx, fy in OTHER_CHIPS]
        for a in range(n):
            outs[a][_dev_index(*me)] = ins[a][...].astype(dtypes[a])

        def copy(a, k, block, to):
            rows = outs[a].at[_dev_index(*block)]
            return pltpu.make_async_remote_copy(src_ref=rows, dst_ref=rows, send_sem=send_sems.at[a, k],
                                                recv_sem=recv_sems.at[a, k], device_id=to, device_id_type=MESH)

        started = []
        for a in range(n):
            for j, chip in enumerate(chips):
                started.append(copy(a, 1 + j, me, (*chip, c)))
            started.append(copy(a, 0, me, sibling))
        for cp in started:
            cp.start()
        for a in range(n):
            for j, chip in enumerate(chips):
                copy(a, 1 + j, (*chip, c), me).wait_recv()
                fwd = copy(a, 1 + len(chips) + j, (*chip, c), sibling)
                fwd.start()
                started.append(fwd)
        for a in range(n):
            copy(a, 0, sibling, me).wait_recv()
            for j, chip in enumerate(chips):
                copy(a, 1 + len(chips) + j, (*chip, 1 - c), me).wait_recv()
        for cp in started:
            cp.wait_send()

    vmem = pl.BlockSpec(memory_space=pltpu.VMEM)
    return pl.pallas_call(
        body, in_specs=[vmem] * n, out_specs=[vmem] * n,
        out_shape=[jax.ShapeDtypeStruct((N_DEV,) + w.shape, dt) for w, dt in zip(shards, dtypes)],
        scratch_shapes=[pltpu.SemaphoreType.DMA((n, n_copies)), pltpu.SemaphoreType.DMA((n, n_copies))],
        compiler_params=pltpu.CompilerParams(vmem_limit_bytes=VMEM_LIMIT), name="allgather_weights")(*shards)


def _final_exchange(g8, v):
    rows = v.shape[0]
    _, _, s_shape, s_scratch = _scatter_specs([g8])

    def body(g_ref, v_ref, gout_ref, vout_ref, gath, send_sems, recv_sems, *rs_scratch):
        scatter = _ReduceScatter(g_ref, gout_ref, *rs_scratch)
        x, y, c = _position()
        me = _dev_index(x, y, c)
        peers = [(_flip(x, k >> 2 & 1), _flip(y, k >> 1 & 1), _flip(c, k & 1)) for k in range(1, N_DEV)]

        def copy(k, block):
            return pltpu.make_async_remote_copy(src_ref=gath.at[block], dst_ref=gath.at[block], send_sem=send_sems.at[k],
                                                recv_sem=recv_sems.at[k], device_id=peers[k], device_id_type=MESH)

        scatter.start()
        gath[me] = v_ref[...]
        for k in range(N_DEV - 1):
            copy(k, me).start()
        scatter.middle()
        for k in range(N_DEV - 1):
            copy(k, _dev_index(*peers[k])).wait_recv()
        for k in range(N_DEV - 1):
            copy(k, me).wait_send()
        total = gath[0]
        for d in range(1, N_DEV):
            total = total + gath[d]
        vout_ref[...] = total
        scatter.finish()

    vmem = pl.BlockSpec(memory_space=pltpu.VMEM)
    return pl.pallas_call(
        body, in_specs=[pl.BlockSpec(memory_space=pl.ANY), vmem], out_specs=[vmem, vmem],
        out_shape=s_shape + [jax.ShapeDtypeStruct((rows, LANES), F32)],
        scratch_shapes=[pltpu.VMEM((N_DEV, rows, LANES), F32), pltpu.SemaphoreType.DMA((N_DEV - 1,)),
                        pltpu.SemaphoreType.DMA((N_DEV - 1,))] + s_scratch,
        compiler_params=pltpu.CompilerParams(vmem_limit_bytes=VMEM_LIMIT), name="final_exchange")(g8, v)


def _row_tile(r):
    for n in (8, 4, 2):
        if r % (n * SUBLANES) == 0:
            return r // n
    return r


def _adam_math(wv, gv, mv, vv):
    mn = ADAM_B1 * mv + (1.0 - ADAM_B1) * gv
    vn = ADAM_B2 * vv + (1.0 - ADAM_B2) * (gv * gv)
    m_hat = mn / (1.0 - ADAM_B1 ** ADAM_STEP)
    v_hat = vn / (1.0 - ADAM_B2 ** ADAM_STEP)
    return -ADAM_LR * (m_hat / (jnp.sqrt(v_hat) + ADAM_EPS) + ADAM_WD * wv), mn, vn


def _adamw(w, g, m, v, name):
    r, c_ = w.shape
    tr = _row_tile(r)

    def body(w_ref, g_ref, m_ref, v_ref, d_ref, mo_ref, vo_ref):
        d_ref[...], mo_ref[...], vo_ref[...] = _adam_math(w_ref[...], g_ref[...], m_ref[...], v_ref[...])

    spec = pl.BlockSpec((tr, c_), lambda i: (i, 0))
    return pl.pallas_call(
        body, grid=(r // tr,), in_specs=[spec] * 4, out_specs=[spec] * 3,
        out_shape=[jax.ShapeDtypeStruct((r, c_), F32)] * 3,
        compiler_params=_cparams("parallel"), name=name)(w, g, m, v)


FW_ROWS = 24
CW_ROWS = 32
R_FW = 0
R_FB = R_FW + N_DEV * FW_ROWS
R_CW = R_FB + 48
R_BQKV = R_CW + (CONV_WIDTH // LANES) * CW_ROWS
R_BCIN = R_BQKV + 8
R_GMIX = R_BCIN + 8
R_BOUT = R_GMIX + 8
R_GFFN = R_BOUT + 8
R_CB = R_GFFN + 8
R_CGAIN = R_CB + 8
R_CBIAS = R_CGAIN + 8
R_QKS = R_CBIAS + 8
SMALL_ROWS = R_QKS + 8


def _pack_small(raw):
    def rows(a, n):
        a = a.reshape(-1, LANES)
        return jnp.pad(a, ((0, n - a.shape[0]), (0, 0)))

    fw = jnp.pad(raw["dfw"].reshape(N_DEV, -1, LANES), ((0, 0), (0, FW_ROWS - 3 * FF_LANE_CHUNKS), (0, 0)))
    cw = jnp.pad(raw["dcw"].reshape(CONV_KERNEL, -1, LANES).transpose(1, 0, 2), ((0, 0), (0, CW_ROWS - CONV_KERNEL), (0, 0)))
    qks = jnp.concatenate([raw["dgq"], raw["dgk"], raw["dsink"], jnp.pad(raw["loss"], ((0, 0), (0, LANES - 1)))], axis=0)
    return jnp.concatenate([
        fw.reshape(-1, LANES), rows(raw["dfb"][:, 0, :FF_CHUNK], 48), cw.reshape(-1, LANES), rows(raw["dbqkv"], 8),
        rows(raw["dbcin"], 8), rows(raw["dg_mix"], 8), rows(raw["db_out"], 8), rows(raw["dg_ffn"], 8), rows(raw["dcb"], 8),
        rows(raw["dcgain"], 8), rows(raw["dcbias"], 8), rows(qks, 8)], axis=0)


def _adamw_small(gpack, w, m, v):
    n = len(SMALL)
    ix = {name: i for i, name in enumerate(SMALL)}

    def body(g_ref, *refs):
        w_refs, m_refs, v_refs, outs = refs[:n], refs[n:2 * n], refs[2 * n:3 * n], refs[3 * n:]
        d = _dev_index(*_position())

        def step(name, idx, gv):
            i = ix[name]
            delta, mn, vn = _adam_math(w_refs[i][idx], gv, m_refs[i][idx], v_refs[i][idx])
            for ref, val in zip(outs[4 * i:4 * i + 4], (gv, delta, mn, vn)):
                ref[idx] = val

        def whole(name, row, nrows):
            step(name, (slice(None), slice(None)), g_ref[row:row + nrows, :])

        whole("mix_norm_gain", R_GMIX, 8)
        whole("b_out", R_BOUT, 8)
        whole("ffn_norm_gain", R_GFFN, 8)
        whole("conv_dw_b", R_CB, 4)
        whole("conv_norm_gain", R_CGAIN, 4)
        whole("conv_norm_bias", R_CBIAS, 4)
        whole("ffn_dw_b", R_FB, 2 * D_FF // LANES)
        nq = QKV_COLS // LANES
        step("b_in", (slice(0, nq), slice(None)), g_ref[R_BQKV:R_BQKV + nq, :])
        step("b_in", (slice(nq, nq + CIN_COLS // LANES), slice(None)), g_ref[R_BCIN:R_BCIN + CIN_COLS // LANES, :])
        step("q_norm_gain", (slice(None), slice(None)), g_ref[R_QKS:R_QKS + 1, 0:HEAD_DIM])
        step("k_norm_gain", (slice(None), slice(None)), g_ref[R_QKS + 1:R_QKS + 2, 0:HEAD_DIM])
        step("attn_sinks", (slice(None), slice(None)), g_ref[R_QKS + 2:R_QKS + 3, 0:N_Q_HEADS])
        blk = g_ref[pl.ds(pl.multiple_of(R_CW + CW_ROWS * lax.shift_right_logical(d, 1), SUBLANES), CW_ROWS), :]
        blk = jnp.where((d & 1) == 1, pltpu.roll(blk, HEAD_DIM, 1), blk)
        step("conv_dw_w", (slice(None), slice(None)), blk[0:CONV_KERNEL, 0:CONV_WIDTH // N_DEV])
        blk = g_ref[pl.ds(pl.multiple_of(R_FW + FW_ROWS * d, SUBLANES), FW_ROWS), :]
        for k in range(3):
            for j in range(FF_LANE_CHUNKS):
                wd = min(LANES, FF_CHUNK - j * LANES)
                row = k * FF_LANE_CHUNKS + j
                step("ffn_dw_w", (slice(k, k + 1), slice(j * LANES, j * LANES + wd)), blk[row:row + 1, 0:wd])

    vmem = pl.BlockSpec(memory_space=pltpu.VMEM)
    args = [gpack] + [d[name] for d in (w, m, v) for name in SMALL]
    outs = pl.pallas_call(
        body, in_specs=[vmem] * len(args), out_specs=[vmem] * (4 * n),
        out_shape=[jax.ShapeDtypeStruct(w[name].shape, F32) for name in SMALL for _ in range(4)],
        compiler_params=pltpu.CompilerParams(vmem_limit_bytes=VMEM_LIMIT), name="adamw_small")(*args)
    return {name: outs[4 * i:4 * i + 4] for i, name in enumerate(SMALL)}


def _token_mixing(x, p, attn_shards, conv_shards):
    qkv, cin, h1 = _mix_proj(x, p["g_mix"], p["w_qkv"], p["w_cin"], p["b_qkv"], p["b_cin"])
    attn, *from_attn = _attn_fwd(qkv, p["gq2"], p["gk2"], p["sinks"], attn_shards)
    c3, c1, *from_conv = _conv_fwd(cin, p["cw8"], p["cb"], p["cgain"], p["cbias"], conv_shards)
    return (qkv, cin, h1, attn, c3, c1), from_attn, from_conv


def _rest_of_step(x, target, p, saved, scatter):
    s = x.shape[0]
    qkv, cin, h1, attn, c3, c1 = saved
    cw8, w_out, w_up, w_down = p["cw8"], p["w_out"], p["w_up"], p["w_down"]
    x2, h2 = _out_proj(x, attn, c3, w_out, w_out, p["b_out"], p["g_ffn"])
    fw, fb = p["fw"], p["fb"]
    up0, act, dy, loss = _ffn_fwd(h2, x2, target, w_up, fw, fb, w_down)
    dup0, dh2, dfw, dfb = _ffn_bwd(dy, up0, w_up, fw, fb, w_down)
    dx2, dmixed, dg_ffn, db_out = _ffn_norm_bwd(dh2, dy, x2, p["g_ffn"], w_out)
    dw_up = _tn_matmul(h2[None], dup0.reshape(N_DEV, s, FF_CHUNK), "dw_up")
    dw_down = _tn_matmul(act, dy[None], "dw_down").reshape(N_DEV, -1, D_MODEL)
    dw_out = jnp.concatenate([_tn_matmul(attn[None], dx2[None], "dw_out_attn")[0],
                              _tn_matmul(c3[None], dx2[None], "dw_out_conv")[0]], axis=0).reshape(N_DEV, -1, D_MODEL)
    dcin, dcw, dcb, dcgain, dcbias, dbcin, *g_up = _conv_bwd(dmixed, c1, cin, cw8, p["cgain"], p["cbias"], [dw_up] if scatter else [])
    dqkv, dgq, dgk, dsink, dbqkv, *g_down_out = _attn_bwd(qkv, dmixed, p["gq2"], p["gk2"], p["sinks"],
                                                          [dw_down, dw_out] if scatter else [])
    dw_in = jnp.concatenate([_tn_matmul(h1[None], dqkv[None], "dw_qkv")[0], _tn_matmul(h1[None], dcin[None], "dw_cin")[0]], axis=1)
    dw_in = dw_in.reshape(D_MODEL, N_DEV, -1).transpose(1, 0, 2)
    grad_x, dg_mix = _in_bwd(dqkv, dcin, p["w_qkv"], p["w_cin"], x, dx2, p["g_mix"])
    if scatter:
        big = {"w_up": g_up[0], "w_down": g_down_out[0], "w_in": dw_in, "w_out": g_down_out[1]}
    else:
        big = {"w_up": dw_up, "w_down": dw_down, "w_in": dw_in, "w_out": dw_out}
    small = dict(dg_mix=dg_mix, dbqkv=dbqkv, dbcin=dbcin, dgq=dgq, dgk=dgk, dsink=dsink, dcw=dcw, dcb=dcb, dcgain=dcgain,
                 dcbias=dcbias, db_out=db_out, dg_ffn=dg_ffn, dfw=dfw, dfb=dfb, loss=loss)
    return loss, grad_x, big, small


BIG = ("w_in", "w_out", "w_up", "w_down")
SMALL = ("mix_norm_gain", "b_in", "q_norm_gain", "k_norm_gain", "attn_sinks", "conv_dw_w", "conv_dw_b",
         "conv_norm_gain", "conv_norm_bias", "b_out", "ffn_norm_gain", "ffn_dw_w", "ffn_dw_b")
ORDER = ("mix_norm_gain", "w_in", "b_in", "q_norm_gain", "k_norm_gain", "attn_sinks", "conv_dw_w", "conv_dw_b",
         "conv_norm_gain", "conv_norm_bias", "w_out", "b_out", "ffn_norm_gain", "w_up", "ffn_dw_w", "ffn_dw_b", "w_down")


def kernel(x, mix_norm_gain, w_in, b_in, q_norm_gain, k_norm_gain, attn_sinks, conv_dw_w, conv_dw_b, conv_norm_gain, conv_norm_bias, w_out, b_out, ffn_norm_gain, w_up, ffn_dw_w, ffn_dw_b, w_down, loss_target, m_mix_norm_gain, m_w_in, m_b_in, m_q_norm_gain, m_k_norm_gain, m_attn_sinks, m_conv_dw_w, m_conv_dw_b, m_conv_norm_gain, m_conv_norm_bias, m_w_out, m_b_out, m_ffn_norm_gain, m_w_up, m_ffn_dw_w, m_ffn_dw_b, m_w_down, v_mix_norm_gain, v_w_in, v_b_in, v_q_norm_gain, v_k_norm_gain, v_attn_sinks, v_conv_dw_w, v_conv_dw_b, v_conv_norm_gain, v_conv_norm_bias, v_w_out, v_b_out, v_ffn_norm_gain, v_w_up, v_ffn_dw_w, v_ffn_dw_b, v_w_down):
    w = dict(mix_norm_gain=mix_norm_gain, w_in=w_in, b_in=b_in, q_norm_gain=q_norm_gain, k_norm_gain=k_norm_gain,
             attn_sinks=attn_sinks, conv_dw_w=conv_dw_w, conv_dw_b=conv_dw_b, conv_norm_gain=conv_norm_gain,
             conv_norm_bias=conv_norm_bias, w_out=w_out, b_out=b_out, ffn_norm_gain=ffn_norm_gain, w_up=w_up,
             ffn_dw_w=ffn_dw_w, ffn_dw_b=ffn_dw_b, w_down=w_down)
    m = dict(mix_norm_gain=m_mix_norm_gain, w_in=m_w_in, b_in=m_b_in, q_norm_gain=m_q_norm_gain, k_norm_gain=m_k_norm_gain,
             attn_sinks=m_attn_sinks, conv_dw_w=m_conv_dw_w, conv_dw_b=m_conv_dw_b, conv_norm_gain=m_conv_norm_gain,
             conv_norm_bias=m_conv_norm_bias, w_out=m_w_out, b_out=m_b_out, ffn_norm_gain=m_ffn_norm_gain, w_up=m_w_up,
             ffn_dw_w=m_ffn_dw_w, ffn_dw_b=m_ffn_dw_b, w_down=m_w_down)
    v = dict(mix_norm_gain=v_mix_norm_gain, w_in=v_w_in, b_in=v_b_in, q_norm_gain=v_q_norm_gain, k_norm_gain=v_k_norm_gain,
             attn_sinks=v_attn_sinks, conv_dw_w=v_conv_dw_w, conv_dw_b=v_conv_dw_b, conv_norm_gain=v_conv_norm_gain,
             conv_norm_bias=v_conv_norm_bias, w_out=v_w_out, b_out=v_b_out, ffn_norm_gain=v_ffn_norm_gain, w_up=v_w_up,
             ffn_dw_w=v_ffn_dw_w, ffn_dw_b=v_ffn_dw_b, w_down=v_w_down)
    s = x.shape[1]

    wi8, cw8, fw8 = _allgather([w_in, conv_dw_w, ffn_dw_w], [BF16, F32, F32])
    w_in_full = wi8.transpose(1, 0, 2).reshape(D_MODEL, QKV_COLS + CIN_COLS)
    lane_pad = ((0, 0), (0, 0), (0, FF_PADDED - FF_CHUNK))
    p = {
        "g_mix": mix_norm_gain.reshape(1, -1), "w_qkv": w_in_full[:, :QKV_COLS], "w_cin": w_in_full[:, QKV_COLS:],
        "b_qkv": b_in[:QKV_COLS].reshape(1, -1), "b_cin": b_in[QKV_COLS:].reshape(1, -1),
        "gq2": jnp.tile(q_norm_gain, 2).reshape(1, -1), "gk2": jnp.tile(k_norm_gain, 2).reshape(1, -1), "sinks": attn_sinks,
        "cw8": jnp.repeat(cw8.transpose(1, 0, 2).reshape(CONV_KERNEL, CONV_WIDTH), SUBLANES, axis=0),
        "cb": conv_dw_b.reshape(1, -1), "cgain": conv_norm_gain.reshape(1, -1), "cbias": conv_norm_bias.reshape(1, -1),
        "b_out": b_out.reshape(1, -1), "g_ffn": ffn_norm_gain.reshape(1, -1),
        "fw": jnp.pad(fw8, lane_pad), "fb": jnp.pad(ffn_dw_b.reshape(N_DEV, 1, FF_CHUNK), lane_pad),
    }

    saved, (wu8,), (wo8, wd8) = _token_mixing(x[0], p, [w_up], [w_out, w_down])
    p.update(w_out=wo8.reshape(D_MODEL, D_MODEL), w_up=wu8, w_down=wd8.reshape(N_FF_PAIRS, FF_CHUNK, D_MODEL))
    loss, grad_x, big, small = _rest_of_step(x[0], loss_target[0], p, saved, True)

    g = dict(big)
    g["w_in"], gpack = _final_exchange(big["w_in"], _pack_small(small))

    delta, new_m, new_v = {}, {}, {}
    for n in BIG:
        delta[n], new_m[n], new_v[n] = _adamw(w[n], g[n], m[n], v[n], "adamw_" + n)

    def view(a):
        return a if a.ndim == 2 else (a.reshape(-1, LANES) if a.size % LANES == 0 else a.reshape(1, -1))

    small_out = _adamw_small(gpack, *[{n: view(d[n]) for n in SMALL} for d in (w, m, v)])
    for n in SMALL:
        g[n], delta[n], new_m[n], new_v[n] = [a.reshape(w[n].shape) for a in small_out[n]]

    total = gpack[R_QKS + 3, 0]
    return (total, grad_x.reshape(1, s, D_MODEL), *[g[n] for n in ORDER], *[delta[n] for n in ORDER],
            *[new_m[n] for n in ORDER], *[new_v[n] for n in ORDER])
```

```python
import functools
import math

import jax
import jax.numpy as jnp
from jax import lax
from jax.experimental import pallas as pl
from jax.experimental.pallas import tpu as pltpu

F32 = jnp.float32
BF16 = jnp.bfloat16

D_MODEL = 1024
HEAD_DIM = 64
N_Q_HEADS = 8
N_KV_HEADS = 2
Q_COLS = 512
KV_COLS = 128
QKV_COLS = Q_COLS + 2 * KV_COLS
CONV_WIDTH = 512
CIN_COLS = 2 * CONV_WIDTH
CONV_KERNEL = 31
CONV_HALO = 32
D_FF = 2816
N_DEV = 8
FF_CHUNK = 2 * D_FF // N_DEV
N_FF_PAIRS = N_DEV // 2
ATT_BLOCK = 128
EPS = 1e-6
NEG_INF = -1e30
SLOPES = [float(2.0 ** (-8.0 * (h + 1.0) / N_Q_HEADS)) for h in range(N_Q_HEADS)]

ADAM_LR = 0.001
ADAM_B1 = 0.9
ADAM_B2 = 0.999
ADAM_EPS = 1e-08
ADAM_WD = 0.01
ADAM_STEP = 10

LANES = 128
SUBLANES = 8
VMEM_LIMIT = 56 * 1024 * 1024
MESH = pl.DeviceIdType.MESH


def _cparams(*sem, **kw):
    return pltpu.CompilerParams(dimension_semantics=sem or None, vmem_limit_bytes=VMEM_LIMIT, **kw)


def _resident(shape):
    nd = len(shape)
    return pl.BlockSpec(shape, lambda *_: (0,) * nd, pipeline_mode=pl.Buffered(1))


def _dot(a, b):
    return jnp.dot(a, b, preferred_element_type=F32)


def _dot_nt(a, b):
    return lax.dot_general(a, b, (((1,), (1,)), ((), ())), preferred_element_type=F32)


def _dot_tn(a, b):
    return lax.dot_general(a, b, (((0,), (0,)), ((), ())), preferred_element_type=F32)


def _sigmoid(x):
    return 1.0 / (1.0 + jnp.exp(-x))


def _lo_mask(shape):
    return lax.broadcasted_iota(jnp.int32, shape, len(shape) - 1) % LANES < HEAD_DIM


def _half_sums(t, lo):
    s_lo = jnp.sum(jnp.where(lo, t, 0.0), axis=-1, keepdims=True)
    s_hi = jnp.sum(jnp.where(lo, 0.0, t), axis=-1, keepdims=True)
    return jnp.where(lo, s_lo, s_hi)


def _head_norm(t, lo):
    r = lax.rsqrt(_half_sums(t * t, lo) * (1.0 / HEAD_DIM) + EPS)
    return t * r, r


def _head_norm_bwd(dn, n, r, lo):
    return r * (dn - n * (_half_sums(dn * n, lo) * (1.0 / HEAD_DIM)))


def _tile(s):
    return min(512, s)


TN_TOKENS = 2048
FF_COLS = ((0, 256), (256, 512), (512, 704))


def _position():
    return lax.axis_index("x"), lax.axis_index("y"), lax.axis_index("c")


def _dev_index(px, py, pc):
    return 4 * px + 2 * py + pc


def _flip(v, bit):
    return 1 - v if bit else v


OTHER_CHIPS = ((1, 0), (0, 1), (1, 1))
N_GATHER_COPIES = 1 + 2 * len(OTHER_CHIPS)


class _Gather:
    def __init__(self, shard_ref, out_ref, cast_buf, send_sems, recv_sems, local_sem):
        self.shard, self.out, self.buf = shard_ref, out_ref, cast_buf
        self.send_sems, self.recv_sems, self.local_sem = send_sems, recv_sems, local_sem
        x, y, c = _position()
        self.c = c
        self.me, self.sibling = (x, y, c), (x, y, 1 - c)
        self.chips = [(_flip(x, fx), _flip(y, fy)) for fx, fy in OTHER_CHIPS]

    def _copy(self, k, block, to, from_buf=False):
        rows = self.out.at[_dev_index(*block)]
        return pltpu.make_async_remote_copy(src_ref=self.buf if from_buf else rows, dst_ref=rows,
                                            send_sem=self.send_sems.at[k], recv_sem=self.recv_sems.at[k],
                                            device_id=to, device_id_type=MESH)

    def _local(self):
        return pltpu.make_async_copy(self.buf, self.out.at[_dev_index(*self.me)], self.local_sem)

    def start(self):
        self.buf[...] = self.shard[...].astype(self.buf.dtype)
        self._local().start()
        for j, chip in enumerate(self.chips):
            self._copy(1 + j, self.me, (*chip, self.c), from_buf=True).start()
        self._copy(0, self.me, self.sibling, from_buf=True).start()

    def forward(self):
        for j, chip in enumerate(self.chips):
            self._copy(1 + j, (*chip, self.c), self.me).wait_recv()
            self._copy(1 + len(self.chips) + j, (*chip, self.c), self.sibling).start()

    def finish(self):
        self._copy(0, self.sibling, self.me).wait_recv()
        for j, chip in enumerate(self.chips):
            self._copy(1 + len(self.chips) + j, (*chip, 1 - self.c), self.me).wait_recv()
        for k in range(N_GATHER_COPIES):
            self._copy(k, self.me, self.sibling).wait_send()
        self._local().wait()


def _gather_specs(shards):
    whole = [pl.BlockSpec(w.shape, lambda *_, nd=w.ndim: (0,) * nd, pipeline_mode=pl.Buffered(1)) for w in shards]
    outs = [pl.BlockSpec(memory_space=pl.ANY) for _ in shards]
    shapes = [jax.ShapeDtypeStruct((N_DEV,) + w.shape, BF16) for w in shards]
    scratch = []
    for w in shards:
        scratch += [pltpu.VMEM(w.shape, BF16), pltpu.SemaphoreType.DMA((N_GATHER_COPIES,)),
                    pltpu.SemaphoreType.DMA((N_GATHER_COPIES,)), pltpu.SemaphoreType.DMA(())]
    return whole, outs, shapes, scratch


def _run_gathers(gathers, step, n_steps):
    @pl.when(step == 0)
    def _():
        for g in gathers:
            g.start()

    @pl.when(step == 3 * n_steps // 4)
    def _():
        for g in gathers:
            g.forward()

    @pl.when(step == n_steps - 1)
    def _():
        for g in gathers:
            g.finish()


class _ReduceScatter:
    def __init__(self, g_ref, out_ref, stage, load_sems, send_a, recv_a, send_b, recv_b, sa_send, sa_recv, sb_send, sb_recv):
        self.g, self.out, self.stage, self.load_sems = g_ref, out_ref, stage, load_sems
        self.send_a, self.recv_a, self.send_b, self.recv_b = send_a, recv_a, send_b, recv_b
        self.sems = (sa_send, sa_recv, sb_send, sb_recv)
        x, y, c = _position()
        self.c, self.sibling = c, (x, y, 1 - c)
        self.chips = [(x, y)] + [(_flip(x, fx), _flip(y, fy)) for fx, fy in OTHER_CHIPS]

    def _copy_a(self, j):
        return pltpu.make_async_remote_copy(src_ref=self.send_a.at[j], dst_ref=self.recv_a.at[j], send_sem=self.sems[0].at[j],
                                            recv_sem=self.sems[1].at[j], device_id=self.sibling, device_id_type=MESH)

    def _copy_b(self, j):
        return pltpu.make_async_remote_copy(src_ref=self.send_b.at[j], dst_ref=self.recv_b.at[j], send_sem=self.sems[2].at[j],
                                            recv_sem=self.sems[3].at[j], device_id=(*self.chips[1 + j], self.c),
                                            device_id_type=MESH)

    def _load(self, j, core):
        return pltpu.make_async_copy(self.g.at[_dev_index(*self.chips[j], core)], self.stage.at[j % 2], self.load_sems.at[j % 2])

    def start(self):
        self._load(0, 1 - self.c).start()
        for j in range(len(self.chips)):
            self._load(j, 1 - self.c).wait()
            if j + 1 < len(self.chips):
                self._load(j + 1, 1 - self.c).start()
            self.send_a[j] = self.stage[j % 2].astype(BF16)
            self._copy_a(j).start()

    def middle(self):
        self._load(0, self.c).start()
        for j in range(len(self.chips)):
            self._load(j, self.c).wait()
            if j + 1 < len(self.chips):
                self._load(j + 1, self.c).start()
            self._copy_a(j).wait_recv()
            part = self.stage[j % 2] + self.recv_a[j].astype(F32)
            if j == 0:
                self.out[...] = part
            else:
                self.send_b[j - 1] = part.astype(BF16)
                self._copy_b(j - 1).start()

    def finish(self):
        for j in range(len(OTHER_CHIPS)):
            self._copy_b(j).wait_recv()
            self.out[...] += self.recv_b[j].astype(F32)
        for j in range(len(self.chips)):
            self._copy_a(j).wait_send()
        for j in range(len(OTHER_CHIPS)):
            self._copy_b(j).wait_send()


N_SCATTER_SCRATCH = 10


def _scatter_specs(g8s):
    na, nb = 1 + len(OTHER_CHIPS), len(OTHER_CHIPS)
    ins = [pl.BlockSpec(memory_space=pl.ANY) for _ in g8s]
    outs = [pl.BlockSpec(g.shape[1:], lambda *_: (0, 0)) for g in g8s]
    shapes = [jax.ShapeDtypeStruct(g.shape[1:], F32) for g in g8s]
    scratch = []
    for g in g8s:
        blk = g.shape[1:]
        scratch += [pltpu.VMEM((2,) + blk, F32), pltpu.SemaphoreType.DMA((2,)), pltpu.VMEM((na,) + blk, BF16), pltpu.VMEM((na,) + blk, BF16),
                    pltpu.VMEM((nb,) + blk, BF16), pltpu.VMEM((nb,) + blk, BF16),
                    pltpu.SemaphoreType.DMA((na,)), pltpu.SemaphoreType.DMA((na,)),
                    pltpu.SemaphoreType.DMA((nb,)), pltpu.SemaphoreType.DMA((nb,))]
    return ins, outs, shapes, scratch


def _run_scatters(scatters, step, n_steps):
    @pl.when(step == 0)
    def _():
        for r in scatters:
            r.start()

    @pl.when(step == min(max(1, n_steps // 4), n_steps - 1))
    def _():
        for r in scatters:
            r.middle()

    @pl.when(step == n_steps - 1)
    def _():
        for r in scatters:
            r.finish()


def _mix_proj(x, g_mix, w_qkv, w_cin, b_qkv, b_cin):
    s = x.shape[0]
    tm = _tile(s)

    def body(x_ref, g_ref, wq_ref, wc_ref, bq_ref, bc_ref, qkv_ref, cin_ref, h1_ref):
        xv = x_ref[...]
        r = lax.rsqrt(jnp.mean(xv * xv, axis=-1, keepdims=True) + EPS)
        h = (xv * r * g_ref[...]).astype(BF16)
        h1_ref[...] = h
        qkv_ref[...] = _dot(h, wq_ref[...]) + bq_ref[...]
        cin_ref[...] = _dot(h, wc_ref[...]) + bc_ref[...]

    return pl.pallas_call(
        body, grid=(s // tm,),
        in_specs=[pl.BlockSpec((tm, D_MODEL), lambda i: (i, 0)), _resident((1, D_MODEL)),
                  _resident((D_MODEL, QKV_COLS)), _resident((D_MODEL, CIN_COLS)),
                  _resident((1, QKV_COLS)), _resident((1, CIN_COLS))],
        out_specs=[pl.BlockSpec((tm, QKV_COLS), lambda i: (i, 0)), pl.BlockSpec((tm, CIN_COLS), lambda i: (i, 0)),
                   pl.BlockSpec((tm, D_MODEL), lambda i: (i, 0))],
        out_shape=[jax.ShapeDtypeStruct((s, QKV_COLS), F32), jax.ShapeDtypeStruct((s, CIN_COLS), F32),
                   jax.ShapeDtypeStruct((s, D_MODEL), BF16)],
        compiler_params=_cparams("parallel"), name="mix_proj")(x, g_mix, w_qkv, w_cin, b_qkv, b_cin)


def _kv_variants(kv_all, gk2, lo):
    k_all = kv_all[:, :LANES]
    v_all = kv_all[:, LANES:]
    kn_pre, rk = _head_norm(k_all, lo)
    kn = kn_pre * gk2
    kr = pltpu.roll(kn, HEAD_DIM, 1)
    vr = pltpu.roll(v_all, HEAD_DIM, 1)
    zero = jnp.zeros_like(kn)
    k_lo = [jnp.where(lo, kn, zero).astype(BF16), jnp.where(lo, kr, zero).astype(BF16)]
    k_hi = [jnp.where(lo, zero, kr).astype(BF16), jnp.where(lo, zero, kn).astype(BF16)]
    v_lo = [jnp.where(lo, v_all, zero).astype(BF16), jnp.where(lo, vr, zero).astype(BF16)]
    v_hi = [jnp.where(lo, zero, vr).astype(BF16), jnp.where(lo, zero, v_all).astype(BF16)]
    return k_lo, k_hi, v_lo, v_hi, kn_pre, rk


def _att_consts(first_tile, b):
    rows = 2 * ATT_BLOCK
    qi = lax.broadcasted_iota(jnp.int32, (rows, 2 * ATT_BLOCK), 0) % ATT_BLOCK
    kj = lax.broadcasted_iota(jnp.int32, (rows, 2 * ATT_BLOCK), 1)
    rel = qi + ATT_BLOCK - kj
    valid = (rel >= 0) & (rel < ATT_BLOCK)
    if b == 0:
        valid = valid & ((kj >= ATT_BLOCK) | jnp.logical_not(first_tile))
    return rel.astype(F32), valid


def _row_const(va, vb):
    top = lax.broadcasted_iota(jnp.int32, (2 * ATT_BLOCK, 1), 0) < ATT_BLOCK
    return jnp.where(top, va, vb)


def _probs(q2, k_op, rel, valid, slope, sink):
    sc = _dot_nt(q2, k_op) * (1.0 / math.sqrt(HEAD_DIM)) - slope * rel
    sc = jnp.where(valid, sc, NEG_INF)
    m = jnp.maximum(jnp.max(sc, axis=-1, keepdims=True), sink)
    p = jnp.exp(sc - m)
    e_sink = jnp.exp(sink - m)
    inv = 1.0 / (jnp.sum(p, axis=-1, keepdims=True) + e_sink)
    return p * inv, e_sink * inv


def _attn_fwd(qkv, gq2, gk2, sinks, shards):
    s = qkv.shape[0]
    tq = _tile(s)
    nb = tq // ATT_BLOCK
    ng = len(shards)
    g_in, g_out, g_shape, g_scratch = _gather_specs(shards)

    def body(q_ref, kv_ref, kvp_ref, gq_ref, gk_ref, sink_ref, *rest):
        out_ref = rest[ng]
        i = pl.program_id(0)
        _run_gathers([_Gather(rest[a], rest[ng + 1 + a], *rest[2 * ng + 1 + 4 * a:2 * ng + 5 + 4 * a]) for a in range(ng)],
                     i, s // tq)
        lo = _lo_mask((1, LANES))
        kv_all = jnp.concatenate([kvp_ref[...], kv_ref[...]], axis=0)
        k_lo, k_hi, v_lo, v_hi, _, _ = _kv_variants(kv_all, gk_ref[...], lo)
        for b in range(nb):
            rel, valid = _att_consts(i == 0, b)
            rows = slice(b * ATT_BLOCK, (b + 1) * ATT_BLOCK)
            keys = slice(b * ATT_BLOCK, (b + 2) * ATT_BLOCK)
            for kvh in range(N_KV_HEADS):
                pairs = (2 * kvh, 2 * kvh + 1)
                q2 = jnp.concatenate([q_ref[rows, p * LANES:(p + 1) * LANES] for p in pairs], axis=0)
                qn, _ = _head_norm(q2, lo)
                q2 = (qn * gq_ref[...]).astype(BF16)
                out = None
                for odd, (k_op, v_op) in enumerate(((k_lo[kvh][keys], v_lo[kvh][keys]), (k_hi[kvh][keys], v_hi[kvh][keys]))):
                    ha, hb = 2 * pairs[0] + odd, 2 * pairs[1] + odd
                    p, _ = _probs(q2, k_op, rel, valid, _row_const(SLOPES[ha], SLOPES[hb]),
                                  _row_const(sink_ref[ha], sink_ref[hb]))
                    o = _dot(p.astype(BF16), v_op)
                    out = o if out is None else out + o
                for n, p in enumerate(pairs):
                    out_ref[rows, p * LANES:(p + 1) * LANES] = out[n * ATT_BLOCK:(n + 1) * ATT_BLOCK].astype(BF16)

    return pl.pallas_call(
        body, grid=(s // tq,),
        in_specs=[pl.BlockSpec((tq, Q_COLS), lambda i: (i, 0)),
                  pl.BlockSpec((tq, 2 * KV_COLS), lambda i: (i, 2)),
                  pl.BlockSpec((ATT_BLOCK, 2 * KV_COLS), lambda i: (jnp.maximum(i * nb - 1, 0), 2)),
                  _resident((1, LANES)), _resident((1, LANES)),
                  pl.BlockSpec(memory_space=pltpu.SMEM)] + g_in,
        out_specs=[pl.BlockSpec((tq, Q_COLS), lambda i: (i, 0))] + g_out,
        out_shape=[jax.ShapeDtypeStruct((s, Q_COLS), BF16)] + g_shape,
        scratch_shapes=g_scratch,
        compiler_params=_cparams("arbitrary"), name="attn_fwd")(qkv, qkv, qkv, gq2, gk2, sinks, *shards)


def _group_stats(c1, lo):
    mu = _half_sums(c1, lo) * (1.0 / HEAD_DIM)
    d = c1 - mu
    rstd = lax.rsqrt(_half_sums(d * d, lo) * (1.0 / HEAD_DIM) + EPS)
    return d * rstd, rstd


def _rows(ref, first_row, n):
    return ref[pl.ds(first_row, n, stride=1), :].reshape(n // SUBLANES, SUBLANES, LANES)


def _conv_fwd(cin, cw8, cb, gain, bias, shards):
    s = cin.shape[0]
    tm = _tile(s)
    rc = 64
    nchunk = CONV_WIDTH // LANES
    lead = CONV_HALO - (CONV_KERNEL - 1)
    ng = len(shards)
    g_in, g_out, g_shape, g_scratch = _gather_specs(shards)

    def body(cin_ref, cw_ref, cb_ref, gain_ref, bias_ref, *rest):
        c3_ref, c1_ref, ext_ref = rest[ng], rest[ng + 1], rest[2 * ng + 2]
        _run_gathers([_Gather(rest[a], rest[ng + 2 + a], *rest[2 * ng + 3 + 4 * a:2 * ng + 7 + 4 * a]) for a in range(ng)],
                     pl.program_id(0), s // tm)

        @pl.when(pl.program_id(0) == 0)
        def _():
            ext_ref[:, 0:CONV_HALO, :] = jnp.zeros((nchunk, CONV_HALO, LANES), F32)

        lo = _lo_mask((1, LANES))
        for cc in range(nchunk):
            cols = slice(cc * LANES, (cc + 1) * LANES)
            gcols = slice(CONV_WIDTH + cc * LANES, CONV_WIDTH + (cc + 1) * LANES)
            ext_ref[cc, CONV_HALO:CONV_HALO + tm, :] = cin_ref[:, cols] * _sigmoid(cin_ref[:, gcols])
            ext = ext_ref.at[cc]
            for r in range(tm // rc):
                rows = slice(r * rc, (r + 1) * rc)
                acc = jnp.zeros((rc // SUBLANES, SUBLANES, LANES), F32)
                for k in range(CONV_KERNEL):
                    acc = acc + cw_ref[k * SUBLANES:(k + 1) * SUBLANES, cols][None] * _rows(ext, r * rc + lead + k, rc)
                c1 = acc.reshape(rc, LANES) + cb_ref[:, cols]
                c1_ref[cc, rows, :] = c1
                nrm, _ = _group_stats(c1, lo)
                c2 = nrm * gain_ref[:, cols] + bias_ref[:, cols]
                c3_ref[rows, cols] = (c2 * _sigmoid(c2)).astype(BF16)
        ext_ref[:, 0:CONV_HALO, :] = ext_ref[:, tm:tm + CONV_HALO, :]

    return pl.pallas_call(
        body, grid=(s // tm,),
        in_specs=[pl.BlockSpec((tm, CIN_COLS), lambda i: (i, 0)), _resident((CONV_KERNEL * SUBLANES, CONV_WIDTH)),
                  _resident((1, CONV_WIDTH)), _resident((1, CONV_WIDTH)), _resident((1, CONV_WIDTH))] + g_in,
        out_specs=[pl.BlockSpec((tm, CONV_WIDTH), lambda i: (i, 0)), pl.BlockSpec((nchunk, tm, LANES), lambda i: (0, i, 0))] + g_out,
        out_shape=[jax.ShapeDtypeStruct((s, CONV_WIDTH), BF16), jax.ShapeDtypeStruct((nchunk, s, LANES), F32)] + g_shape,
        scratch_shapes=[pltpu.VMEM((nchunk, tm + CONV_HALO, LANES), F32)] + g_scratch,
        compiler_params=_cparams("arbitrary"), name="conv_fwd")(cin, cw8, cb, gain, bias, *shards)


def _out_proj(x, attn, c3, wo_a, wo_c, b_out, g_ffn):
    s = x.shape[0]
    tm = _tile(s)

    def body(x_ref, a_ref, c_ref, wa_ref, wc_ref, b_ref, g_ref, x2_ref, h2_ref):
        x2 = x_ref[...] + _dot(a_ref[...], wa_ref[...]) + _dot(c_ref[...], wc_ref[...]) + b_ref[...]
        x2_ref[...] = x2
        r = lax.rsqrt(jnp.mean(x2 * x2, axis=-1, keepdims=True) + EPS)
        h2_ref[...] = (x2 * r * g_ref[...]).astype(BF16)

    return pl.pallas_call(
        body, grid=(s // tm,),
        in_specs=[pl.BlockSpec((tm, D_MODEL), lambda i: (i, 0)), pl.BlockSpec((tm, Q_COLS), lambda i: (i, 0)),
                  pl.BlockSpec((tm, CONV_WIDTH), lambda i: (i, 0)),
                  pl.BlockSpec((Q_COLS, D_MODEL), lambda i: (0, 0), pipeline_mode=pl.Buffered(1)),
                  pl.BlockSpec((CONV_WIDTH, D_MODEL), lambda i: (1, 0), pipeline_mode=pl.Buffered(1)),
                  _resident((1, D_MODEL)), _resident((1, D_MODEL))],
        out_specs=[pl.BlockSpec((tm, D_MODEL), lambda i: (i, 0)), pl.BlockSpec((tm, D_MODEL), lambda i: (i, 0))],
        out_shape=[jax.ShapeDtypeStruct((s, D_MODEL), F32), jax.ShapeDtypeStruct((s, D_MODEL), BF16)],
        compiler_params=_cparams("parallel"), name="out_proj")(x, attn, c3, wo_a, wo_c, b_out, g_ffn)


FF_LANE_CHUNKS = -(-FF_CHUNK // LANES)
FF_PADDED = FF_LANE_CHUNKS * LANES


def _tap(ref, first_row, n):
    return ref[pl.ds(first_row, n, stride=1), :]


def _ffn_fwd(h2, x2, target, w_up, fw, fb, w_down):
    s = h2.shape[0]
    tm = _tile(s)
    hal = SUBLANES
    rc = min(128, tm)

    def body(h_ref, x2_ref, t_ref, wu_ref, fw_ref, fb_ref, wd_ref, up0_ref, act_ref, dy_ref, loss_ref,
             ext_ref, carry_ref, act_buf, y_ref):
        i, ci = pl.program_id(0), pl.program_id(1)

        @pl.when((i == 0) & (ci == 0))
        def _():
            carry_ref[...] = jnp.zeros(carry_ref.shape, F32)
            ext_ref[...] = jnp.zeros(ext_ref.shape, F32)
            act_buf[...] = jnp.zeros(act_buf.shape, BF16)
            loss_ref[...] = jnp.zeros((1, 1), F32)

        @pl.when(ci == 0)
        def _():
            y_ref[...] = x2_ref[...]

        ws = (fw_ref[ci], fw_ref[ci + N_FF_PAIRS])
        bs = (fb_ref[ci], fb_ref[ci + N_FF_PAIRS])
        half_rows = (slice(0, tm // 2), slice(tm // 2, tm))
        n_grp = len(FF_COLS)

        def up_slices(grp):
            lo_c, hi_c = FF_COLS[grp]
            chunks = range(lo_c // LANES, -(-hi_c // LANES))

            def make(half, n, rows):
                def run():
                    c = ci + half * N_FF_PAIRS
                    u0 = _dot(h_ref[rows, :], wu_ref[c, :, lo_c:hi_c])
                    up0_ref[half, 0, rows, lo_c:hi_c] = u0.astype(BF16)
                    for j in chunks:
                        w = min(LANES, hi_c - j * LANES)
                        if n == 0:
                            ext_ref[half, j, 0:hal, 0:w] = carry_ref[c, :, j * LANES:j * LANES + w]
                        ext_ref[half, j, hal + rows.start:hal + rows.stop, 0:w] = u0[:, j * LANES - lo_c:j * LANES - lo_c + w]
                    if n == len(half_rows) - 1:
                        carry_ref[c, :, lo_c:hi_c] = u0[u0.shape[0] - hal:, :]
                return run
            return [make(half, n, rows) for half in range(2) for n, rows in enumerate(half_rows)]

        def down_slices(grp):
            lo_c, hi_c = FF_COLS[grp]

            def make(rows):
                def run():
                    y_ref[rows, :] += _dot(act_buf[rows, lo_c:hi_c], wd_ref[ci, lo_c:hi_c, :])
                return run
            return [make(rows) for rows in half_rows]

        def vector_blocks(grp):
            lo_c, hi_c = FF_COLS[grp]
            blocks = []
            for j in range(lo_c // LANES, -(-hi_c // LANES)):
                lanes = slice(j * LANES, (j + 1) * LANES)

                def gate(r, lanes=lanes, j=j):
                    base = r * rc
                    ups = []
                    for half in range(2):
                        e, w = ext_ref.at[half, j], ws[half]
                        ups.append(w[0:1, lanes] * _tap(e, base + hal - 2, rc) + w[1:2, lanes] * _tap(e, base + hal - 1, rc)
                                   + w[2:3, lanes] * _tap(e, base + hal, rc) + bs[half][:, lanes])
                    g, u = ups
                    act_buf[base:base + rc, lanes] = (g * _sigmoid(g) * u).astype(BF16)

                blocks += [functools.partial(gate, r) for r in range(tm // rc)]

            def finish():
                act_ref[0, :, lo_c:hi_c] = act_buf[:, lo_c:hi_c]
            blocks.append(finish)
            return blocks

        for run in up_slices(0):
            run()
        for grp in range(n_grp):
            matmuls = (up_slices(grp + 1) if grp + 1 < n_grp else []) + (down_slices(grp - 1) if grp > 0 else [])
            blocks = vector_blocks(grp)
            every = max(1, len(blocks) // (len(matmuls) + 1))
            for n, run in enumerate(blocks):
                run()
                if n % every == every - 1 and matmuls:
                    matmuls.pop(0)()
            for run in matmuls:
                run()
        for run in down_slices(n_grp - 1):
            run()

        @pl.when(ci == N_FF_PAIRS - 1)
        def _():
            e = y_ref[...] - t_ref[...]
            dy_ref[...] = e * (1.0 / D_MODEL)
            loss_ref[...] += (0.5 / D_MODEL) * jnp.sum(e * e).reshape(1, 1)

    tok = lambda i, ci: (i, 0)
    return pl.pallas_call(
        body, grid=(s // tm, N_FF_PAIRS),
        in_specs=[pl.BlockSpec((tm, D_MODEL), tok), pl.BlockSpec((tm, D_MODEL), tok), pl.BlockSpec((tm, D_MODEL), tok),
                  _resident((N_DEV, D_MODEL, FF_CHUNK)), _resident((N_DEV, 3, FF_PADDED)), _resident((N_DEV, 1, FF_PADDED)),
                  _resident((N_FF_PAIRS, FF_CHUNK, D_MODEL))],
        out_specs=[pl.BlockSpec((2, 1, tm, FF_CHUNK), lambda i, ci: (0, ci, i, 0)),
                   pl.BlockSpec((1, tm, FF_CHUNK), lambda i, ci: (ci, i, 0)),
                   pl.BlockSpec((tm, D_MODEL), tok), pl.BlockSpec((1, 1), lambda i, ci: (0, 0))],
        out_shape=[jax.ShapeDtypeStruct((2, N_FF_PAIRS, s, FF_CHUNK), BF16),
                   jax.ShapeDtypeStruct((N_FF_PAIRS, s, FF_CHUNK), BF16), jax.ShapeDtypeStruct((s, D_MODEL), F32),
                   jax.ShapeDtypeStruct((1, 1), F32)],
        scratch_shapes=[pltpu.VMEM((2, FF_LANE_CHUNKS, tm + hal, LANES), F32), pltpu.VMEM((N_DEV, hal, FF_CHUNK), F32),
                        pltpu.VMEM((tm, FF_PADDED), BF16), pltpu.VMEM((tm, D_MODEL), F32)],
        compiler_params=_cparams("arbitrary", "arbitrary"), name="ffn_fwd")(h2, x2, target, w_up, fw, fb, w_down)


def _ffn_bwd(dy, up0, w_up, fw, fb, w_down):
    s = dy.shape[0]
    tm = _tile(s)
    nt = s // tm
    hal = 2 * SUBLANES
    nxt = SUBLANES
    rc = min(128, tm)

    def body(dy_ref, up0_ref, up0h_ref, wu_ref, fw_ref, fb_ref, wd_ref,
             dup0_ref, dh2_ref, dfw_ref, dfb_ref, ext_ref, dext_ref, carry_ref, dact_buf, dup0_buf):
        i, ci = pl.program_id(0), pl.program_id(1)
        t = nt - 1 - i

        @pl.when((i == 0) & (ci == 0))
        def _():
            for ref in (carry_ref, dfw_ref, dfb_ref, ext_ref, dext_ref, dact_buf):
                ref[...] = jnp.zeros(ref.shape, F32)
            dup0_buf[...] = jnp.zeros(dup0_buf.shape, BF16)

        @pl.when(ci == 0)
        def _():
            dh2_ref[...] = jnp.zeros(dh2_ref.shape, F32)

        ws = (fw_ref[ci], fw_ref[ci + N_FF_PAIRS])
        bs = (fb_ref[ci], fb_ref[ci + N_FF_PAIRS])
        fold = lambda v: jnp.sum(v.reshape(rc // SUBLANES, SUBLANES, LANES), axis=0)
        half_rows = (slice(0, tm // 2), slice(tm // 2, tm))
        n_grp = len(FF_COLS)

        def dact_slices(grp, pair):
            lo_c, hi_c = FF_COLS[grp]

            def make(rows):
                def run():
                    dact_buf[rows, lo_c:hi_c] = _dot_nt(dy_ref[rows, :].astype(BF16), wd_ref[pair, lo_c:hi_c, :])
                return run
            return [make(rows) for rows in half_rows]

        def dh2_slices(grp, pair):
            lo_c, hi_c = FF_COLS[grp]

            def make(half, rows):
                def run():
                    c = pair + half * N_FF_PAIRS
                    dh2_ref[rows, :] += _dot_nt(dup0_buf[half, rows, lo_c:hi_c], wu_ref[c, :, lo_c:hi_c])
                return run
            return [make(half, rows) for half in range(2) for rows in half_rows]

        def vector_blocks(grp):
            lo_c, hi_c = FF_COLS[grp]
            chunks = range(lo_c // LANES, -(-hi_c // LANES))
            blocks = []

            def stage():
                for half in range(2):
                    c = ci + half * N_FF_PAIRS
                    prev = jnp.where(t > 0, up0h_ref[half, 0, :, lo_c:hi_c].astype(F32), 0.0)
                    cur = up0_ref[half, 0, :, lo_c:hi_c].astype(F32)
                    for j in chunks:
                        w = min(LANES, hi_c - j * LANES)
                        cols = slice(j * LANES - lo_c, j * LANES - lo_c + w)
                        ext_ref[half, j, 0:hal, 0:w] = prev[:, cols]
                        ext_ref[half, j, hal:hal + tm, 0:w] = cur[:, cols]
                        dext_ref[half, j, tm:tm + nxt, 0:w] = carry_ref[c, :, j * LANES:j * LANES + w]
            blocks.append(stage)
            for j in chunks:
                lanes = slice(j * LANES, (j + 1) * LANES)
                acc = [jnp.zeros((SUBLANES, LANES), F32)] * 8

                def grads(r, lanes=lanes, j=j, acc=acc):
                    base = r * rc
                    taps, ups = [], []
                    for half in range(2):
                        e, w = ext_ref.at[half, j], ws[half]
                        x = [_tap(e, base + hal - 2 + k, rc) for k in range(3)]
                        taps.append(x)
                        ups.append(w[0:1, lanes] * x[0] + w[1:2, lanes] * x[1] + w[2:3, lanes] * x[2] + bs[half][:, lanes])
                    g, u = ups
                    sg = _sigmoid(g)
                    dact = dact_buf[base:base + rc, lanes]
                    ds = (dact * u * (sg * (1.0 + g * (1.0 - sg))), dact * (g * sg))
                    for half in range(2):
                        dext_ref[half, j, base:base + rc, :] = ds[half]
                        acc[4 * half] = acc[4 * half] + fold(ds[half])
                        for k in range(3):
                            acc[4 * half + 1 + k] = acc[4 * half + 1 + k] + fold(ds[half] * taps[half][k])

                def sums(lanes=lanes, acc=acc):
                    for half in range(2):
                        c = ci + half * N_FF_PAIRS
                        dfb_ref[c, :, lanes] += jnp.sum(acc[4 * half], axis=0, keepdims=True)
                        dfw_ref[c, :, lanes] += jnp.concatenate(
                            [jnp.sum(acc[4 * half + 1 + k], axis=0, keepdims=True) for k in range(3)], axis=0)

                def conv_back(r, lanes=lanes, j=j):
                    base = r * rc
                    for half in range(2):
                        d, w = dext_ref.at[half, j], ws[half]
                        dup0 = w[2:3, lanes] * _tap(d, base, rc) + w[1:2, lanes] * _tap(d, base + 1, rc) + w[0:1, lanes] * _tap(d, base + 2, rc)
                        dup0_buf[half, base:base + rc, lanes] = dup0.astype(BF16)

                blocks += [functools.partial(grads, r) for r in range(tm // rc)] + [sums]
                blocks += [functools.partial(conv_back, r) for r in range(tm // rc)]

            def finish():
                for half in range(2):
                    c = ci + half * N_FF_PAIRS
                    for j in chunks:
                        w = min(LANES, hi_c - j * LANES)
                        carry_ref[c, :, j * LANES:j * LANES + w] = dext_ref[half, j, 0:nxt, 0:w]
                    dup0_ref[half, 0, :, lo_c:hi_c] = dup0_buf[half, :, lo_c:hi_c]
            blocks.append(finish)
            return blocks

        for run in dact_slices(0, ci):
            run()
        for grp in range(n_grp):
            matmuls = (dact_slices(grp + 1, ci) if grp + 1 < n_grp else []) + (dh2_slices(grp - 1, ci) if grp > 0 else [])
            blocks = vector_blocks(grp)
            every = max(1, len(blocks) // (len(matmuls) + 1))
            for n, run in enumerate(blocks):
                run()
                if n % every == every - 1 and matmuls:
                    matmuls.pop(0)()
            for run in matmuls:
                run()
        for run in dh2_slices(n_grp - 1, ci):
            run()

    tok = lambda i, ci: (nt - 1 - i, 0)
    acc = lambda shape: pl.BlockSpec(shape, lambda i, ci: (0,) * len(shape))
    return pl.pallas_call(
        body, grid=(nt, N_FF_PAIRS),
        in_specs=[pl.BlockSpec((tm, D_MODEL), tok),
                  pl.BlockSpec((2, 1, tm, FF_CHUNK), lambda i, ci: (0, ci, nt - 1 - i, 0)),
                  pl.BlockSpec((2, 1, hal, FF_CHUNK), lambda i, ci: (0, ci, jnp.maximum((nt - 1 - i) * (tm // hal) - 1, 0), 0)),
                  _resident((N_DEV, D_MODEL, FF_CHUNK)), _resident((N_DEV, 3, FF_PADDED)), _resident((N_DEV, 1, FF_PADDED)),
                  _resident((N_FF_PAIRS, FF_CHUNK, D_MODEL))],
        out_specs=[pl.BlockSpec((2, 1, tm, FF_CHUNK), lambda i, ci: (0, ci, nt - 1 - i, 0)),
                   pl.BlockSpec((tm, D_MODEL), tok), acc((N_DEV, 3, FF_PADDED)), acc((N_DEV, 1, FF_PADDED))],
        out_shape=[jax.ShapeDtypeStruct((2, N_FF_PAIRS, s, FF_CHUNK), BF16), jax.ShapeDtypeStruct((s, D_MODEL), F32),
                   jax.ShapeDtypeStruct((N_DEV, 3, FF_PADDED), F32), jax.ShapeDtypeStruct((N_DEV, 1, FF_PADDED), F32)],
        scratch_shapes=[pltpu.VMEM((2, FF_LANE_CHUNKS, tm + hal, LANES), F32), pltpu.VMEM((2, FF_LANE_CHUNKS, tm + nxt, LANES), F32),
                        pltpu.VMEM((N_DEV, nxt, FF_CHUNK), F32), pltpu.VMEM((tm, FF_PADDED), F32),
                        pltpu.VMEM((2, tm, FF_PADDED), BF16)],
        compiler_params=_cparams("arbitrary", "arbitrary"), name="ffn_bwd")(dy, up0, up0, w_up, fw, fb, w_down)


def _ffn_norm_bwd(dh2, dy, x2, g_ffn, w_out):
    s = dy.shape[0]
    tm = _tile(s)

    def body(dh_ref, dy_ref, x2_ref, g_ref, wo_ref, dx2_ref, dmix_ref, dg_ref, dbo_ref):
        @pl.when(pl.program_id(0) == 0)
        def _():
            dg_ref[...] = jnp.zeros(dg_ref.shape, F32)
            dbo_ref[...] = jnp.zeros(dbo_ref.shape, F32)

        x2v = x2_ref[...]
        r = lax.rsqrt(jnp.mean(x2v * x2v, axis=-1, keepdims=True) + EPS)
        n2 = x2v * r
        dh2 = dh_ref[...]
        dg_ref[...] += jnp.sum(dh2 * n2, axis=0, keepdims=True)
        dn = dh2 * g_ref[...]
        dx2 = dy_ref[...] + r * (dn - n2 * jnp.mean(dn * n2, axis=-1, keepdims=True))
        dx2_ref[...] = dx2
        dbo_ref[...] += jnp.sum(dx2, axis=0, keepdims=True)
        dmix_ref[...] = _dot_nt(dx2.astype(BF16), wo_ref[...])

    tok = pl.BlockSpec((tm, D_MODEL), lambda i: (i, 0))
    vec = pl.BlockSpec((1, D_MODEL), lambda i: (0, 0))
    return pl.pallas_call(
        body, grid=(s // tm,),
        in_specs=[tok, tok, tok, _resident((1, D_MODEL)), _resident((D_MODEL, D_MODEL))],
        out_specs=[tok, tok, vec, vec],
        out_shape=[jax.ShapeDtypeStruct((s, D_MODEL), F32), jax.ShapeDtypeStruct((s, D_MODEL), F32),
                   jax.ShapeDtypeStruct((1, D_MODEL), F32), jax.ShapeDtypeStruct((1, D_MODEL), F32)],
        compiler_params=_cparams("arbitrary"), name="ffn_norm_bwd")(dh2, dy, x2, g_ffn, w_out)


def _conv_bwd(dmixed, c1, cin, cw8, gain, bias, g8s):
    ns = len(g8s)
    s_in, s_out, s_shape, s_scratch = _scatter_specs(g8s)
    s = cin.shape[0]
    tm = _tile(s)
    nt = s // tm
    rc = 64
    rn = min(256, tm)
    hal = CONV_HALO
    lead = hal - (CONV_KERNEL - 1)
    nchunk = CONV_WIDTH // LANES

    def body(dc3_ref, dc3n_ref, c1_ref, c1n_ref, cin_ref, cinp_ref, cw_ref, gain_ref, bias_ref, *rest):
        dcin_ref, dcw_ref, dcb_ref, dgain_ref, dbias_ref, dbcin_ref = rest[ns:ns + 6]
        c0_ext, dc1_ext, dcw8 = rest[2 * ns + 6:2 * ns + 9]
        i = pl.program_id(0)
        first, last = i == 0, i == nt - 1
        own = rest[2 * ns + 9:]
        _run_scatters([_ReduceScatter(rest[a], rest[ns + 6 + a], *own[N_SCATTER_SCRATCH * a:N_SCATTER_SCRATCH * (a + 1)])
                       for a in range(ns)], i, nt)

        @pl.when(first)
        def _():
            for ref in (dcw8, dcb_ref, dgain_ref, dbias_ref, dbcin_ref):
                ref[...] = jnp.zeros(ref.shape, F32)

        lo = _lo_mask((1, LANES))

        def norm_bwd(dc3, c1v, cols):
            nrm, rstd = _group_stats(c1v, lo)
            c2 = nrm * gain_ref[:, cols] + bias_ref[:, cols]
            sg = _sigmoid(c2)
            dc2 = dc3 * (sg * (1.0 + c2 * (1.0 - sg)))
            dn = dc2 * gain_ref[:, cols]
            inv = 1.0 / HEAD_DIM
            dc1 = rstd * (dn - _half_sums(dn, lo) * inv - nrm * (_half_sums(dn * nrm, lo) * inv))
            return dc1, dc2, nrm

        def row_sum(v):
            return jnp.sum(v, axis=0, keepdims=True)

        for cc in range(nchunk):
            cols = slice(cc * LANES, (cc + 1) * LANES)
            gcols = slice(CONV_WIDTH + cc * LANES, CONV_WIDTH + (cc + 1) * LANES)
            c0e, d1e = c0_ext.at[cc], dc1_ext.at[cc]
            c0e[0:hal, :] = jnp.where(first, 0.0, cinp_ref[:, cols] * _sigmoid(cinp_ref[:, gcols]))
            dc1n, _, _ = norm_bwd(dc3n_ref[:, cols], c1n_ref[cc], cols)
            d1e[tm:tm + hal, :] = jnp.where(last, 0.0, dc1n)

            for r in range(tm // rn):
                rows = slice(r * rn, (r + 1) * rn)
                c0e[hal + r * rn:hal + (r + 1) * rn, :] = cin_ref[rows, cols] * _sigmoid(cin_ref[rows, gcols])
                dc1, dc2, nrm = norm_bwd(dc3_ref[rows, cols], c1_ref[cc, rows, :], cols)
                d1e[rows, :] = dc1
                dgain_ref[:, cols] += row_sum(dc2 * nrm)
                dbias_ref[:, cols] += row_sum(dc2)
                dcb_ref[:, cols] += row_sum(dc1)
            zero = jnp.zeros((1, LANES), F32)

            for k0 in range(0, CONV_KERNEL, SUBLANES):
                taps = range(k0, min(k0 + SUBLANES, CONV_KERNEL))

                def tap_sums(r, acc, taps=taps):
                    d = _rows(d1e, r * rc, rc)
                    return tuple(a + jnp.sum(d * _rows(c0e, r * rc + lead + k, rc), axis=0) for a, k in zip(acc, taps))

                acc = lax.fori_loop(0, tm // rc, tap_sums, tuple(dcw8[k * SUBLANES:(k + 1) * SUBLANES, cols] for k in taps))
                for a, k in zip(acc, taps):
                    dcw8[k * SUBLANES:(k + 1) * SUBLANES, cols] = a

            def input_grad(r, sums):
                rows = pl.ds(pl.multiple_of(r * rc, rc), rc)
                dc0 = jnp.zeros((rc // SUBLANES, SUBLANES, LANES), F32)
                for k in range(CONV_KERNEL):
                    dc0 = dc0 + cw_ref[k * SUBLANES:(k + 1) * SUBLANES, cols][None] * _rows(d1e, r * rc + CONV_KERNEL - 1 - k, rc)
                dc0 = dc0.reshape(rc, LANES)
                sg = _sigmoid(cin_ref[rows, gcols])
                da = dc0 * sg
                dgate = dc0 * cin_ref[rows, cols] * sg * (1.0 - sg)
                dcin_ref[rows, cols] = da.astype(BF16)
                dcin_ref[rows, gcols] = dgate.astype(BF16)
                return sums[0] + row_sum(da), sums[1] + row_sum(dgate)

            sums = lax.fori_loop(0, tm // rc, input_grad, (zero, zero))
            dbcin_ref[:, cols] += sums[0]
            dbcin_ref[:, gcols] += sums[1]

        @pl.when(last)
        def _():
            for k in range(CONV_KERNEL):
                dcw_ref[k:k + 1, :] = jnp.sum(dcw8[k * SUBLANES:(k + 1) * SUBLANES, :], axis=0, keepdims=True)

    nh = tm // hal
    acc = lambda shape: pl.BlockSpec(shape, lambda i: (0,) * len(shape))
    return pl.pallas_call(
        body, grid=(nt,),
        in_specs=[pl.BlockSpec((tm, CONV_WIDTH), lambda i: (i, 1)),
                  pl.BlockSpec((hal, CONV_WIDTH), lambda i: (jnp.minimum((i + 1) * nh, s // hal - 1), 1)),
                  pl.BlockSpec((nchunk, tm, LANES), lambda i: (0, i, 0)),
                  pl.BlockSpec((nchunk, hal, LANES), lambda i: (0, jnp.minimum((i + 1) * nh, s // hal - 1), 0)),
                  pl.BlockSpec((tm, CIN_COLS), lambda i: (i, 0)),
                  pl.BlockSpec((hal, CIN_COLS), lambda i: (jnp.maximum(i * nh - 1, 0), 0)),
                  _resident((CONV_KERNEL * SUBLANES, CONV_WIDTH)), _resident((1, CONV_WIDTH)), _resident((1, CONV_WIDTH))] + s_in,
        out_specs=[pl.BlockSpec((tm, CIN_COLS), lambda i: (i, 0)), acc((CONV_KERNEL, CONV_WIDTH)), acc((1, CONV_WIDTH)),
                   acc((1, CONV_WIDTH)), acc((1, CONV_WIDTH)), acc((1, CIN_COLS))] + s_out,
        out_shape=[jax.ShapeDtypeStruct((s, CIN_COLS), BF16), jax.ShapeDtypeStruct((CONV_KERNEL, CONV_WIDTH), F32),
                   jax.ShapeDtypeStruct((1, CONV_WIDTH), F32), jax.ShapeDtypeStruct((1, CONV_WIDTH), F32),
                   jax.ShapeDtypeStruct((1, CONV_WIDTH), F32), jax.ShapeDtypeStruct((1, CIN_COLS), F32)] + s_shape,
        scratch_shapes=[pltpu.VMEM((nchunk, tm + hal, LANES), F32), pltpu.VMEM((nchunk, tm + hal, LANES), F32),
                        pltpu.VMEM((CONV_KERNEL * SUBLANES, CONV_WIDTH), F32)] + s_scratch,
        compiler_params=_cparams("arbitrary"), name="conv_bwd")(dmixed, dmixed, c1, c1, cin, cin, cw8, gain, bias, *g8s)


def _attn_bwd(qkv, dmixed, gq2, gk2, sinks, g8s):
    ns = len(g8s)
    s_in, s_out, s_shape, s_scratch = _scatter_specs(g8s)
    s = qkv.shape[0]
    tq = _tile(s)
    nb = tq // ATT_BLOCK
    nt = s // tq

    def body(q_ref, kv_ref, kvp_ref, do_ref, gq_ref, gk_ref, sink_ref, *rest):
        dqkv_ref, dgq_ref, dgk_ref, dsink_ref, dbqkv_ref = rest[ns:ns + 5]
        dk_acc, dv_acc, carry_k, carry_v = rest[2 * ns + 5:2 * ns + 9]
        i = pl.program_id(0)
        t = nt - 1 - i
        own = rest[2 * ns + 9:]
        _run_scatters([_ReduceScatter(rest[a], rest[ns + 5 + a], *own[N_SCATTER_SCRATCH * a:N_SCATTER_SCRATCH * (a + 1)])
                       for a in range(ns)], i, nt)

        @pl.when(i == 0)
        def _():
            for ref in (carry_k, carry_v, dgq_ref, dgk_ref, dsink_ref, dbqkv_ref):
                ref[...] = jnp.zeros(ref.shape, F32)

        lo = _lo_mask((1, LANES))
        lane_id = lax.broadcasted_iota(jnp.int32, (1, LANES), 1)
        kv_all = jnp.concatenate([kvp_ref[...], kv_ref[...]], axis=0)
        k_lo, k_hi, v_lo, v_hi, kn_pre, rk = _kv_variants(kv_all, gk_ref[...], lo)
        for acc_ref, carry in ((dk_acc, carry_k), (dv_acc, carry_v)):
            acc_ref[:, 0:tq, :] = jnp.zeros((N_KV_HEADS, tq, LANES), F32)
            acc_ref[:, tq:tq + ATT_BLOCK, :] = carry[...]
        dsink = jnp.zeros((1, LANES), F32)
        dgq = jnp.zeros((1, LANES), F32)
        gq = gq_ref[...]
        for b in range(nb):
            rel, valid = _att_consts(t == 0, b)
            rows = slice(b * ATT_BLOCK, (b + 1) * ATT_BLOCK)
            keys = slice(b * ATT_BLOCK, (b + 2) * ATT_BLOCK)
            for kvh in range(N_KV_HEADS):
                pairs = (2 * kvh, 2 * kvh + 1)
                q_raw = jnp.concatenate([q_ref[rows, p * LANES:(p + 1) * LANES] for p in pairs], axis=0)
                qn_pre, rq = _head_norm(q_raw, lo)
                q2 = (qn_pre * gq).astype(BF16)
                do2 = jnp.concatenate([do_ref[rows, p * LANES:(p + 1) * LANES] for p in pairs], axis=0).astype(BF16)
                dq2 = jnp.zeros((2 * ATT_BLOCK, LANES), F32)
                for odd, (k_op, v_op) in enumerate(((k_lo[kvh][keys], v_lo[kvh][keys]), (k_hi[kvh][keys], v_hi[kvh][keys]))):
                    ha, hb = 2 * pairs[0] + odd, 2 * pairs[1] + odd
                    p, p_sink = _probs(q2, k_op, rel, valid, _row_const(SLOPES[ha], SLOPES[hb]),
                                       _row_const(sink_ref[ha], sink_ref[hb]))
                    dp = _dot_nt(do2, v_op)
                    delta = jnp.sum(p * dp, axis=-1, keepdims=True)
                    ds = (p * (dp - delta) * (1.0 / math.sqrt(HEAD_DIM))).astype(BF16)
                    dsk = p_sink * delta
                    dsink = dsink - jnp.where(lane_id == ha, jnp.sum(dsk[0:ATT_BLOCK]), 0.0) \
                        - jnp.where(lane_id == hb, jnp.sum(dsk[ATT_BLOCK:]), 0.0)
                    dq2 = dq2 + _dot(ds, k_op)
                    half = lo if odd == 0 else jnp.logical_not(lo)
                    dk_acc[kvh, keys, :] += jnp.where(half, _dot_tn(ds, q2), 0.0)
                    dv_acc[kvh, keys, :] += jnp.where(half, _dot_tn(p.astype(BF16), do2), 0.0)
                dgq = dgq + jnp.sum(dq2 * qn_pre, axis=0, keepdims=True)
                dq_raw = _head_norm_bwd(dq2 * gq, qn_pre, rq, lo)
                for n, p_ in enumerate(pairs):
                    blk = dq_raw[n * ATT_BLOCK:(n + 1) * ATT_BLOCK]
                    dqkv_ref[rows, p_ * LANES:(p_ + 1) * LANES] = blk.astype(BF16)
                    dbqkv_ref[:, p_ * LANES:(p_ + 1) * LANES] += jnp.sum(blk, axis=0, keepdims=True)
        carry_k[...] = dk_acc[:, 0:ATT_BLOCK, :]
        carry_v[...] = dv_acc[:, 0:ATT_BLOCK, :]

        def fold(acc_ref):
            both = []
            for kvh in range(N_KV_HEADS):
                a = acc_ref[kvh, ATT_BLOCK:ATT_BLOCK + tq, :]
                both.append(a + pltpu.roll(a, HEAD_DIM, 1))
            return jnp.where(lo, both[0], both[1])

        dkn = fold(dk_acc)
        dv = fold(dv_acc)
        kn_c, rk_c = kn_pre[ATT_BLOCK:], rk[ATT_BLOCK:]
        dgk_ref[...] += jnp.sum(dkn * kn_c, axis=0, keepdims=True)
        dk_raw = _head_norm_bwd(dkn * gk_ref[...], kn_c, rk_c, lo)
        dqkv_ref[:, Q_COLS:Q_COLS + KV_COLS] = dk_raw.astype(BF16)
        dqkv_ref[:, Q_COLS + KV_COLS:] = dv.astype(BF16)
        dbqkv_ref[:, Q_COLS:Q_COLS + KV_COLS] += jnp.sum(dk_raw, axis=0, keepdims=True)
        dbqkv_ref[:, Q_COLS + KV_COLS:] += jnp.sum(dv, axis=0, keepdims=True)
        dgq_ref[...] += dgq
        dsink_ref[...] += dsink

        @pl.when(i == nt - 1)
        def _():
            for ref in (dgq_ref, dgk_ref):
                v = ref[...]
                ref[...] = v + pltpu.roll(v, HEAD_DIM, 1)

    acc = lambda shape: pl.BlockSpec(shape, lambda i: (0,) * len(shape))
    return pl.pallas_call(
        body, grid=(nt,),
        in_specs=[pl.BlockSpec((tq, Q_COLS), lambda i: (nt - 1 - i, 0)),
                  pl.BlockSpec((tq, 2 * KV_COLS), lambda i: (nt - 1 - i, 2)),
                  pl.BlockSpec((ATT_BLOCK, 2 * KV_COLS), lambda i: (jnp.maximum((nt - 1 - i) * nb - 1, 0), 2)),
                  pl.BlockSpec((tq, Q_COLS), lambda i: (nt - 1 - i, 0)),
                  _resident((1, LANES)), _resident((1, LANES)), pl.BlockSpec(memory_space=pltpu.SMEM)] + s_in,
        out_specs=[pl.BlockSpec((tq, QKV_COLS), lambda i: (nt - 1 - i, 0)), acc((1, LANES)), acc((1, LANES)),
                   acc((1, LANES)), acc((1, QKV_COLS))] + s_out,
        out_shape=[jax.ShapeDtypeStruct((s, QKV_COLS), BF16), jax.ShapeDtypeStruct((1, LANES), F32),
                   jax.ShapeDtypeStruct((1, LANES), F32), jax.ShapeDtypeStruct((1, LANES), F32),
                   jax.ShapeDtypeStruct((1, QKV_COLS), F32)] + s_shape,
        scratch_shapes=[pltpu.VMEM((N_KV_HEADS, tq + ATT_BLOCK, LANES), F32), pltpu.VMEM((N_KV_HEADS, tq + ATT_BLOCK, LANES), F32),
                        pltpu.VMEM((N_KV_HEADS, ATT_BLOCK, LANES), F32), pltpu.VMEM((N_KV_HEADS, ATT_BLOCK, LANES), F32)] + s_scratch,
        compiler_params=_cparams("arbitrary"), name="attn_bwd")(qkv, qkv, qkv, dmixed, gq2, gk2, sinks, *g8s)


def _in_bwd(dqkv, dcin, w_qkv, w_cin, x, dx2, g_mix):
    s = x.shape[0]
    tm = _tile(s)

    def body(dq_ref, dc_ref, wq_ref, wc_ref, x_ref, dx2_ref, g_ref, gx_ref, dg_ref):
        @pl.when(pl.program_id(0) == 0)
        def _():
            dg_ref[...] = jnp.zeros(dg_ref.shape, F32)

        dh = _dot_nt(dq_ref[...], wq_ref[...]) + _dot_nt(dc_ref[...], wc_ref[...])
        xv = x_ref[...]
        r = lax.rsqrt(jnp.mean(xv * xv, axis=-1, keepdims=True) + EPS)
        n = xv * r
        dg_ref[...] += jnp.sum(dh * n, axis=0, keepdims=True)
        dn = dh * g_ref[...]
        gx_ref[...] = dx2_ref[...] + r * (dn - n * jnp.mean(dn * n, axis=-1, keepdims=True))

    return pl.pallas_call(
        body, grid=(s // tm,),
        in_specs=[pl.BlockSpec((tm, QKV_COLS), lambda i: (i, 0)), pl.BlockSpec((tm, CIN_COLS), lambda i: (i, 0)),
                  _resident((D_MODEL, QKV_COLS)), _resident((D_MODEL, CIN_COLS)),
                  pl.BlockSpec((tm, D_MODEL), lambda i: (i, 0)), pl.BlockSpec((tm, D_MODEL), lambda i: (i, 0)),
                  _resident((1, D_MODEL))],
        out_specs=[pl.BlockSpec((tm, D_MODEL), lambda i: (i, 0)), pl.BlockSpec((1, D_MODEL), lambda i: (0, 0))],
        out_shape=[jax.ShapeDtypeStruct((s, D_MODEL), F32), jax.ShapeDtypeStruct((1, D_MODEL), F32)],
        compiler_params=_cparams("arbitrary"), name="in_bwd")(dqkv, dcin, w_qkv, w_cin, x, dx2, g_mix)


def _tn_matmul(a, b, name):
    ga, s, m = a.shape
    gb, _, n = b.shape
    g = max(ga, gb)
    tk = min(TN_TOKENS, s)

    def body(a_ref, b_ref, o_ref):
        @pl.when(pl.program_id(1) == 0)
        def _():
            o_ref[...] = jnp.zeros(o_ref.shape, F32)

        o_ref[0] += _dot_tn(a_ref[0].astype(BF16), b_ref[0].astype(BF16))

    return pl.pallas_call(
        body, grid=(g, s // tk),
        in_specs=[pl.BlockSpec((1, tk, m), (lambda gi, k: (gi, k, 0)) if ga > 1 else (lambda gi, k: (0, k, 0))),
                  pl.BlockSpec((1, tk, n), (lambda gi, k: (gi, k, 0)) if gb > 1 else (lambda gi, k: (0, k, 0)))],
        out_specs=pl.BlockSpec((1, m, n), lambda gi, k: (gi, 0, 0)),
        out_shape=jax.ShapeDtypeStruct((g, m, n), F32),
        compiler_params=_cparams("parallel", "arbitrary"), name=name)(a, b)


def _allgather(shards, dtypes):
    n = len(shards)
    n_copies = 1 + 2 * len(OTHER_CHIPS)

    def body(*refs):
        ins, outs = refs[:n], refs[n:2 * n]
        send_sems, recv_sems = refs[2 * n:]
        x, y, c = _position()
        me, sibling = (x, y, c), (x, y, 1 - c)
        chips = [(_flip(x, fx), _flip(y, fy)) for fx, fy in OTHER_CHIPS]
        for a in range(n):
            outs[a][_dev_index(*me)] = ins[a][...].astype(dtypes[a])

        def copy(a, k, block, to):
            rows = outs[a].at[_dev_index(*block)]
            return pltpu.make_async_remote_copy(src_ref=rows, dst_ref=rows, send_sem=send_sems.at[a, k],
                                                recv_sem=recv_sems.at[a, k], device_id=to, device_id_type=MESH)

        started = []
        for a in range(n):
            for j, chip in enumerate(chips):
                started.append(copy(a, 1 + j, me, (*chip, c)))
            started.append(copy(a, 0, me, sibling))
        for cp in started:
            cp.start()
        for a in range(n):
            for j, chip in enumerate(chips):
                copy(a, 1 + j, (*chip, c), me).wait_recv()
                fwd = copy(a, 1 + len(chips) + j, (*chip, c), sibling)
                fwd.start()
                started.append(fwd)
        for a in range(n):
            copy(a, 0, sibling, me).wait_recv()
            for j, chip in enumerate(chips):
                copy(a, 1 + len(chips) + j, (*chip, 1 - c), me).wait_recv()
        for cp in started:
            cp.wait_send()

    vmem = pl.BlockSpec(memory_space=pltpu.VMEM)
    return pl.pallas_call(
        body, in_specs=[vmem] * n, out_specs=[vmem] * n,
        out_shape=[jax.ShapeDtypeStruct((N_DEV,) + w.shape, dt) for w, dt in zip(shards, dtypes)],
        scratch_shapes=[pltpu.SemaphoreType.DMA((n, n_copies)), pltpu.SemaphoreType.DMA((n, n_copies))],
        compiler_params=pltpu.CompilerParams(vmem_limit_bytes=VMEM_LIMIT), name="allgather_weights")(*shards)


def _final_exchange(g8, v):
    rows = v.shape[0]
    _, _, s_shape, s_scratch = _scatter_specs([g8])

    def body(g_ref, v_ref, gout_ref, vout_ref, gath, send_sems, recv_sems, *rs_scratch):
        scatter = _ReduceScatter(g_ref, gout_ref, *rs_scratch)
        x, y, c = _position()
        me = _dev_index(x, y, c)
        peers = [(_flip(x, k >> 2 & 1), _flip(y, k >> 1 & 1), _flip(c, k & 1)) for k in range(1, N_DEV)]

        def copy(k, block):
            return pltpu.make_async_remote_copy(src_ref=gath.at[block], dst_ref=gath.at[block], send_sem=send_sems.at[k],
                                                recv_sem=recv_sems.at[k], device_id=peers[k], device_id_type=MESH)

        scatter.start()
        gath[me] = v_ref[...]
        for k in range(N_DEV - 1):
            copy(k, me).start()
        scatter.middle()
        for k in range(N_DEV - 1):
            copy(k, _dev_index(*peers[k])).wait_recv()
        for k in range(N_DEV - 1):
            copy(k, me).wait_send()
        total = gath[0]
        for d in range(1, N_DEV):
            total = total + gath[d]
        vout_ref[...] = total
        scatter.finish()

    vmem = pl.BlockSpec(memory_space=pltpu.VMEM)
    return pl.pallas_call(
        body, in_specs=[pl.BlockSpec(memory_space=pl.ANY), vmem], out_specs=[vmem, vmem],
        out_shape=s_shape + [jax.ShapeDtypeStruct((rows, LANES), F32)],
        scratch_shapes=[pltpu.VMEM((N_DEV, rows, LANES), F32), pltpu.SemaphoreType.DMA((N_DEV - 1,)),
                        pltpu.SemaphoreType.DMA((N_DEV - 1,))] + s_scratch,
        compiler_params=pltpu.CompilerParams(vmem_limit_bytes=VMEM_LIMIT), name="final_exchange")(g8, v)


def _row_tile(r):
    for n in (8, 4, 2):
        if r % (n * SUBLANES) == 0:
            return r // n
    return r


def _adam_math(wv, gv, mv, vv):
    mn = ADAM_B1 * mv + (1.0 - ADAM_B1) * gv
    vn = ADAM_B2 * vv + (1.0 - ADAM_B2) * (gv * gv)
    m_hat = mn / (1.0 - ADAM_B1 ** ADAM_STEP)
    v_hat = vn / (1.0 - ADAM_B2 ** ADAM_STEP)
    return -ADAM_LR * (m_hat / (jnp.sqrt(v_hat) + ADAM_EPS) + ADAM_WD * wv), mn, vn


def _adamw(w, g, m, v, name):
    r, c_ = w.shape
    tr = _row_tile(r)

    def body(w_ref, g_ref, m_ref, v_ref, d_ref, mo_ref, vo_ref):
        d_ref[...], mo_ref[...], vo_ref[...] = _adam_math(w_ref[...], g_ref[...], m_ref[...], v_ref[...])

    spec = pl.BlockSpec((tr, c_), lambda i: (i, 0))
    return pl.pallas_call(
        body, grid=(r // tr,), in_specs=[spec] * 4, out_specs=[spec] * 3,
        out_shape=[jax.ShapeDtypeStruct((r, c_), F32)] * 3,
        compiler_params=_cparams("parallel"), name=name)(w, g, m, v)


FW_ROWS = 24
CW_ROWS = 32
R_FW = 0
R_FB = R_FW + N_DEV * FW_ROWS
R_CW = R_FB + 48
R_BQKV = R_CW + (CONV_WIDTH // LANES) * CW_ROWS
R_BCIN = R_BQKV + 8
R_GMIX = R_BCIN + 8
R_BOUT = R_GMIX + 8
R_GFFN = R_BOUT + 8
R_CB = R_GFFN + 8
R_CGAIN = R_CB + 8
R_CBIAS = R_CGAIN + 8
R_QKS = R_CBIAS + 8
SMALL_ROWS = R_QKS + 8


def _pack_small(raw):
    def rows(a, n):
        a = a.reshape(-1, LANES)
        return jnp.pad(a, ((0, n - a.shape[0]), (0, 0)))

    fw = jnp.pad(raw["dfw"].reshape(N_DEV, -1, LANES), ((0, 0), (0, FW_ROWS - 3 * FF_LANE_CHUNKS), (0, 0)))
    cw = jnp.pad(raw["dcw"].reshape(CONV_KERNEL, -1, LANES).transpose(1, 0, 2), ((0, 0), (0, CW_ROWS - CONV_KERNEL), (0, 0)))
    qks = jnp.concatenate([raw["dgq"], raw["dgk"], raw["dsink"], jnp.pad(raw["loss"], ((0, 0), (0, LANES - 1)))], axis=0)
    return jnp.concatenate([
        fw.reshape(-1, LANES), rows(raw["dfb"][:, 0, :FF_CHUNK], 48), cw.reshape(-1, LANES), rows(raw["dbqkv"], 8),
        rows(raw["dbcin"], 8), rows(raw["dg_mix"], 8), rows(raw["db_out"], 8), rows(raw["dg_ffn"], 8), rows(raw["dcb"], 8),
        rows(raw["dcgain"], 8), rows(raw["dcbias"], 8), rows(qks, 8)], axis=0)


def _adamw_small(gpack, w, m, v):
    n = len(SMALL)
    ix = {name: i for i, name in enumerate(SMALL)}

    def body(g_ref, *refs):
        w_refs, m_refs, v_refs, outs = refs[:n], refs[n:2 * n], refs[2 * n:3 * n], refs[3 * n:]
        d = _dev_index(*_position())

        def step(name, idx, gv):
            i = ix[name]
            delta, mn, vn = _adam_math(w_refs[i][idx], gv, m_refs[i][idx], v_refs[i][idx])
            for ref, val in zip(outs[4 * i:4 * i + 4], (gv, delta, mn, vn)):
                ref[idx] = val

        def whole(name, row, nrows):
            step(name, (slice(None), slice(None)), g_ref[row:row + nrows, :])

        whole("mix_norm_gain", R_GMIX, 8)
        whole("b_out", R_BOUT, 8)
        whole("ffn_norm_gain", R_GFFN, 8)
        whole("conv_dw_b", R_CB, 4)
        whole("conv_norm_gain", R_CGAIN, 4)
        whole("conv_norm_bias", R_CBIAS, 4)
        whole("ffn_dw_b", R_FB, 2 * D_FF // LANES)
        nq = QKV_COLS // LANES
        step("b_in", (slice(0, nq), slice(None)), g_ref[R_BQKV:R_BQKV + nq, :])
        step("b_in", (slice(nq, nq + CIN_COLS // LANES), slice(None)), g_ref[R_BCIN:R_BCIN + CIN_COLS // LANES, :])
        step("q_norm_gain", (slice(None), slice(None)), g_ref[R_QKS:R_QKS + 1, 0:HEAD_DIM])
        step("k_norm_gain", (slice(None), slice(None)), g_ref[R_QKS + 1:R_QKS + 2, 0:HEAD_DIM])
        step("attn_sinks", (slice(None), slice(None)), g_ref[R_QKS + 2:R_QKS + 3, 0:N_Q_HEADS])
        blk = g_ref[pl.ds(pl.multiple_of(R_CW + CW_ROWS * lax.shift_right_logical(d, 1), SUBLANES), CW_ROWS), :]
        blk = jnp.where((d & 1) == 1, pltpu.roll(blk, HEAD_DIM, 1), blk)
        step("conv_dw_w", (slice(None), slice(None)), blk[0:CONV_KERNEL, 0:CONV_WIDTH // N_DEV])
        blk = g_ref[pl.ds(pl.multiple_of(R_FW + FW_ROWS * d, SUBLANES), FW_ROWS), :]
        for k in range(3):
            for j in range(FF_LANE_CHUNKS):
                wd = min(LANES, FF_CHUNK - j * LANES)
                row = k * FF_LANE_CHUNKS + j
                step("ffn_dw_w", (slice(k, k + 1), slice(j * LANES, j * LANES + wd)), blk[row:row + 1, 0:wd])

    vmem = pl.BlockSpec(memory_space=pltpu.VMEM)
    args = [gpack] + [d[name] for d in (w, m, v) for name in SMALL]
    outs = pl.pallas_call(
        body, in_specs=[vmem] * len(args), out_specs=[vmem] * (4 * n),
        out_shape=[jax.ShapeDtypeStruct(w[name].shape, F32) for name in SMALL for _ in range(4)],
        compiler_params=pltpu.CompilerParams(vmem_limit_bytes=VMEM_LIMIT), name="adamw_small")(*args)
    return {name: outs[4 * i:4 * i + 4] for i, name in enumerate(SMALL)}


def _token_mixing(x, p, attn_shards, conv_shards):
    qkv, cin, h1 = _mix_proj(x, p["g_mix"], p["w_qkv"], p["w_cin"], p["b_qkv"], p["b_cin"])
    attn, *from_attn = _attn_fwd(qkv, p["gq2"], p["gk2"], p["sinks"], attn_shards)
    c3, c1, *from_conv = _conv_fwd(cin, p["cw8"], p["cb"], p["cgain"], p["cbias"], conv_shards)
    return (qkv, cin, h1, attn, c3, c1), from_attn, from_conv


def _rest_of_step(x, target, p, saved, scatter):
    s = x.shape[0]
    qkv, cin, h1, attn, c3, c1 = saved
    cw8, w_out, w_up, w_down = p["cw8"], p["w_out"], p["w_up"], p["w_down"]
    x2, h2 = _out_proj(x, attn, c3, w_out, w_out, p["b_out"], p["g_ffn"])
    fw, fb = p["fw"], p["fb"]
    up0, act, dy, loss = _ffn_fwd(h2, x2, target, w_up, fw, fb, w_down)
    dup0, dh2, dfw, dfb = _ffn_bwd(dy, up0, w_up, fw, fb, w_down)
    dx2, dmixed, dg_ffn, db_out = _ffn_norm_bwd(dh2, dy, x2, p["g_ffn"], w_out)
    dw_up = _tn_matmul(h2[None], dup0.reshape(N_DEV, s, FF_CHUNK), "dw_up")
    dw_down = _tn_matmul(act, dy[None], "dw_down").reshape(N_DEV, -1, D_MODEL)
    dw_out = jnp.concatenate([_tn_matmul(attn[None], dx2[None], "dw_out_attn")[0],
                              _tn_matmul(c3[None], dx2[None], "dw_out_conv")[0]], axis=0).reshape(N_DEV, -1, D_MODEL)
    dcin, dcw, dcb, dcgain, dcbias, dbcin, *g_up = _conv_bwd(dmixed, c1, cin, cw8, p["cgain"], p["cbias"], [dw_up] if scatter else [])
    dqkv, dgq, dgk, dsink, dbqkv, *g_down_out = _attn_bwd(qkv, dmixed, p["gq2"], p["gk2"], p["sinks"],
                                                          [dw_down, dw_out] if scatter else [])
    dw_in = jnp.concatenate([_tn_matmul(h1[None], dqkv[None], "dw_qkv")[0], _tn_matmul(h1[None], dcin[None], "dw_cin")[0]], axis=1)
    dw_in = dw_in.reshape(D_MODEL, N_DEV, -1).transpose(1, 0, 2)
    grad_x, dg_mix = _in_bwd(dqkv, dcin, p["w_qkv"], p["w_cin"], x, dx2, p["g_mix"])
    if scatter:
        big = {"w_up": g_up[0], "w_down": g_down_out[0], "w_in": dw_in, "w_out": g_down_out[1]}
    else:
        big = {"w_up": dw_up, "w_down": dw_down, "w_in": dw_in, "w_out": dw_out}
    small = dict(dg_mix=dg_mix, dbqkv=dbqkv, dbcin=dbcin, dgq=dgq, dgk=dgk, dsink=dsink, dcw=dcw, dcb=dcb, dcgain=dcgain,
                 dcbias=dcbias, db_out=db_out, dg_ffn=dg_ffn, dfw=dfw, dfb=dfb, loss=loss)
    return loss, grad_x, big, small


BIG = ("w_in", "w_out", "w_up", "w_down")
SMALL = ("mix_norm_gain", "b_in", "q_norm_gain", "k_norm_gain", "attn_sinks", "conv_dw_w", "conv_dw_b",
         "conv_norm_gain", "conv_norm_bias", "b_out", "ffn_norm_gain", "ffn_dw_w", "ffn_dw_b")
ORDER = ("mix_norm_gain", "w_in", "b_in", "q_norm_gain", "k_norm_gain", "attn_sinks", "conv_dw_w", "conv_dw_b",
         "conv_norm_gain", "conv_norm_bias", "w_out", "b_out", "ffn_norm_gain", "w_up", "ffn_dw_w", "ffn_dw_b", "w_down")


def kernel(x, mix_norm_gain, w_in, b_in, q_norm_gain, k_norm_gain, attn_sinks, conv_dw_w, conv_dw_b, conv_norm_gain, conv_norm_bias, w_out, b_out, ffn_norm_gain, w_up, ffn_dw_w, ffn_dw_b, w_down, loss_target, m_mix_norm_gain, m_w_in, m_b_in, m_q_norm_gain, m_k_norm_gain, m_attn_sinks, m_conv_dw_w, m_conv_dw_b, m_conv_norm_gain, m_conv_norm_bias, m_w_out, m_b_out, m_ffn_norm_gain, m_w_up, m_ffn_dw_w, m_ffn_dw_b, m_w_down, v_mix_norm_gain, v_w_in, v_b_in, v_q_norm_gain, v_k_norm_gain, v_attn_sinks, v_conv_dw_w, v_conv_dw_b, v_conv_norm_gain, v_conv_norm_bias, v_w_out, v_b_out, v_ffn_norm_gain, v_w_up, v_ffn_dw_w, v_ffn_dw_b, v_w_down):
    w = dict(mix_norm_gain=mix_norm_gain, w_in=w_in, b_in=b_in, q_norm_gain=q_norm_gain, k_norm_gain=k_norm_gain,
             attn_sinks=attn_sinks, conv_dw_w=conv_dw_w, conv_dw_b=conv_dw_b, conv_norm_gain=conv_norm_gain,
             conv_norm_bias=conv_norm_bias, w_out=w_out, b_out=b_out, ffn_norm_gain=ffn_norm_gain, w_up=w_up,
             ffn_dw_w=ffn_dw_w, ffn_dw_b=ffn_dw_b, w_down=w_down)
    m = dict(mix_norm_gain=m_mix_norm_gain, w_in=m_w_in, b_in=m_b_in, q_norm_gain=m_q_norm_gain, k_norm_gain=m_k_norm_gain,
             attn_sinks=m_attn_sinks, conv_dw_w=m_conv_dw_w, conv_dw_b=m_conv_dw_b, conv_norm_gain=m_conv_norm_gain,
             conv_norm_bias=m_conv_norm_bias, w_out=m_w_out, b_out=m_b_out, ffn_norm_gain=m_ffn_norm_gain, w_up=m_w_up,
             ffn_dw_w=m_ffn_dw_w, ffn_dw_b=m_ffn_dw_b, w_down=m_w_down)
    v = dict(mix_norm_gain=v_mix_norm_gain, w_in=v_w_in, b_in=v_b_in, q_norm_gain=v_q_norm_gain, k_norm_gain=v_k_norm_gain,
             attn_sinks=v_attn_sinks, conv_dw_w=v_conv_dw_w, conv_dw_b=v_conv_dw_b, conv_norm_gain=v_conv_norm_gain,
             conv_norm_bias=v_conv_norm_bias, w_out=v_w_out, b_out=v_b_out, ffn_norm_gain=v_ffn_norm_gain, w_up=v_w_up,
             ffn_dw_w=v_ffn_dw_w, ffn_dw_b=v_ffn_dw_b, w_down=v_w_down)
    s = x.shape[1]

    wi8, cw8, fw8 = _allgather([w_in, conv_dw_w, ffn_dw_w], [BF16, F32, F32])
    w_in_full = wi8.transpose(1, 0, 2).reshape(D_MODEL, QKV_COLS + CIN_COLS)
    lane_pad = ((0, 0), (0, 0), (0, FF_PADDED - FF_CHUNK))
    p = {
        "g_mix": mix_norm_gain.reshape(1, -1), "w_qkv": w_in_full[:, :QKV_COLS], "w_cin": w_in_full[:, QKV_COLS:],
        "b_qkv": b_in[:QKV_COLS].reshape(1, -1), "b_cin": b_in[QKV_COLS:].reshape(1, -1),
        "gq2": jnp.tile(q_norm_gain, 2).reshape(1, -1), "gk2": jnp.tile(k_norm_gain, 2).reshape(1, -1), "sinks": attn_sinks,
        "cw8": jnp.repeat(cw8.transpose(1, 0, 2).reshape(CONV_KERNEL, CONV_WIDTH), SUBLANES, axis=0),
        "cb": conv_dw_b.reshape(1, -1), "cgain": conv_norm_gain.reshape(1, -1), "cbias": conv_norm_bias.reshape(1, -1),
        "b_out": b_out.reshape(1, -1), "g_ffn": ffn_norm_gain.reshape(1, -1),
        "fw": jnp.pad(fw8, lane_pad), "fb": jnp.pad(ffn_dw_b.reshape(N_DEV, 1, FF_CHUNK), lane_pad),
    }

    saved, (wu8,), (wo8, wd8) = _token_mixing(x[0], p, [w_up], [w_out, w_down])
    p.update(w_out=wo8.reshape(D_MODEL, D_MODEL), w_up=wu8, w_down=wd8.reshape(N_FF_PAIRS, FF_CHUNK, D_MODEL))
    loss, grad_x, big, small = _rest_of_step(x[0], loss_target[0], p, saved, True)

    g = dict(big)
    g["w_in"], gpack = _final_exchange(big["w_in"], _pack_small(small))

    delta, new_m, new_v = {}, {}, {}
    for n in BIG:
        delta[n], new_m[n], new_v[n] = _adamw(w[n], g[n], m[n], v[n], "adamw_" + n)

    def view(a):
        return a if a.ndim == 2 else (a.reshape(-1, LANES) if a.size % LANES == 0 else a.reshape(1, -1))

    small_out = _adamw_small(gpack, *[{n: view(d[n]) for n in SMALL} for d in (w, m, v)])
    for n in SMALL:
        g[n], delta[n], new_m[n], new_v[n] = [a.reshape(w[n].shape) for a in small_out[n]]

    total = gpack[R_QKS + 3, 0]
    return (total, grad_x.reshape(1, s, D_MODEL), *[g[n] for n in ORDER], *[delta[n] for n in ORDER],
            *[new_m[n] for n in ORDER], *[new_v[n] for n in ORDER])
```

```python
import functools
import math

import jax
import jax.numpy as jnp
from jax import lax
from jax.experimental import pallas as pl
from jax.experimental.pallas import tpu as pltpu

F32 = jnp.float32
BF16 = jnp.bfloat16

D_MODEL = 1024
HEAD_DIM = 64
N_Q_HEADS = 8
N_KV_HEADS = 2
Q_COLS = 512
KV_COLS = 128
QKV_COLS = Q_COLS + 2 * KV_COLS
CONV_WIDTH = 512
CIN_COLS = 2 * CONV_WIDTH
CONV_KERNEL = 31
CONV_HALO = 32
D_FF = 2816
N_DEV = 8
FF_CHUNK = 2 * D_FF // N_DEV
N_FF_PAIRS = N_DEV // 2
ATT_BLOCK = 128
EPS = 1e-6
NEG_INF = -1e30
SLOPES = [float(2.0 ** (-8.0 * (h + 1.0) / N_Q_HEADS)) for h in range(N_Q_HEADS)]

ADAM_LR = 0.001
ADAM_B1 = 0.9
ADAM_B2 = 0.999
ADAM_EPS = 1e-08
ADAM_WD = 0.01
ADAM_STEP = 10

LANES = 128
SUBLANES = 8
VMEM_LIMIT = 56 * 1024 * 1024
MESH = pl.DeviceIdType.MESH


def _cparams(*sem, **kw):
    return pltpu.CompilerParams(dimension_semantics=sem or None, vmem_limit_bytes=VMEM_LIMIT, **kw)


def _resident(shape):
    nd = len(shape)
    return pl.BlockSpec(shape, lambda *_: (0,) * nd, pipeline_mode=pl.Buffered(1))


def _dot(a, b):
    return jnp.dot(a, b, preferred_element_type=F32)


def _dot_nt(a, b):
    return lax.dot_general(a, b, (((1,), (1,)), ((), ())), preferred_element_type=F32)


def _dot_tn(a, b):
    return lax.dot_general(a, b, (((0,), (0,)), ((), ())), preferred_element_type=F32)


def _sigmoid(x):
    return 1.0 / (1.0 + jnp.exp(-x))


def _lo_mask(shape):
    return lax.broadcasted_iota(jnp.int32, shape, len(shape) - 1) % LANES < HEAD_DIM


def _half_sums(t, lo):
    s_lo = jnp.sum(jnp.where(lo, t, 0.0), axis=-1, keepdims=True)
    s_hi = jnp.sum(jnp.where(lo, 0.0, t), axis=-1, keepdims=True)
    return jnp.where(lo, s_lo, s_hi)


def _head_norm(t, lo):
    r = lax.rsqrt(_half_sums(t * t, lo) * (1.0 / HEAD_DIM) + EPS)
    return t * r, r


def _head_norm_bwd(dn, n, r, lo):
    return r * (dn - n * (_half_sums(dn * n, lo) * (1.0 / HEAD_DIM)))


def _tile(s):
    return min(512, s)


TN_TOKENS = 2048
FF_COLS = ((0, 256), (256, 512), (512, 704))


def _position():
    return lax.axis_index("x"), lax.axis_index("y"), lax.axis_index("c")


def _dev_index(px, py, pc):
    return 4 * px + 2 * py + pc


def _flip(v, bit):
    return 1 - v if bit else v


OTHER_CHIPS = ((1, 0), (0, 1), (1, 1))
N_GATHER_COPIES = 1 + 2 * len(OTHER_CHIPS)


class _Gather:
    def __init__(self, shard_ref, out_ref, cast_buf, send_sems, recv_sems, local_sem):
        self.shard, self.out, self.buf = shard_ref, out_ref, cast_buf
        self.send_sems, self.recv_sems, self.local_sem = send_sems, recv_sems, local_sem
        x, y, c = _position()
        self.c = c
        self.me, self.sibling = (x, y, c), (x, y, 1 - c)
        self.chips = [(_flip(x, fx), _flip(y, fy)) for fx, fy in OTHER_CHIPS]

    def _copy(self, k, block, to, from_buf=False):
        rows = self.out.at[_dev_index(*block)]
        return pltpu.make_async_remote_copy(src_ref=self.buf if from_buf else rows, dst_ref=rows,
                                            send_sem=self.send_sems.at[k], recv_sem=self.recv_sems.at[k],
                                            device_id=to, device_id_type=MESH)

    def _local(self):
        return pltpu.make_async_copy(self.buf, self.out.at[_dev_index(*self.me)], self.local_sem)

    def start(self):
        self.buf[...] = self.shard[...].astype(self.buf.dtype)
        self._local().start()
        for j, chip in enumerate(self.chips):
            self._copy(1 + j, self.me, (*chip, self.c), from_buf=True).start()
        self._copy(0, self.me, self.sibling, from_buf=True).start()

    def forward(self):
        for j, chip in enumerate(self.chips):
            self._copy(1 + j, (*chip, self.c), self.me).wait_recv()
            self._copy(1 + len(self.chips) + j, (*chip, self.c), self.sibling).start()

    def finish(self):
        self._copy(0, self.sibling, self.me).wait_recv()
        for j, chip in enumerate(self.chips):
            self._copy(1 + len(self.chips) + j, (*chip, 1 - self.c), self.me).wait_recv()
        for k in range(N_GATHER_COPIES):
            self._copy(k, self.me, self.sibling).wait_send()
        self._local().wait()


def _gather_specs(shards):
    whole = [pl.BlockSpec(w.shape, lambda *_, nd=w.ndim: (0,) * nd, pipeline_mode=pl.Buffered(1)) for w in shards]
    outs = [pl.BlockSpec(memory_space=pl.ANY) for _ in shards]
    shapes = [jax.ShapeDtypeStruct((N_DEV,) + w.shape, BF16) for w in shards]
    scratch = []
    for w in shards:
        scratch += [pltpu.VMEM(w.shape, BF16), pltpu.SemaphoreType.DMA((N_GATHER_COPIES,)),
                    pltpu.SemaphoreType.DMA((N_GATHER_COPIES,)), pltpu.SemaphoreType.DMA(())]
    return whole, outs, shapes, scratch


def _run_gathers(gathers, step, n_steps):
    @pl.when(step == 0)
    def _():
        for g in gathers:
            g.start()

    @pl.when(step == 3 * n_steps // 4)
    def _():
        for g in gathers:
            g.forward()

    @pl.when(step == n_steps - 1)
    def _():
        for g in gathers:
            g.finish()


class _ReduceScatter:
    def __init__(self, g_ref, out_ref, stage, load_sems, send_a, recv_a, send_b, recv_b, sa_send, sa_recv, sb_send, sb_recv):
        self.g, self.out, self.stage, self.load_sems = g_ref, out_ref, stage, load_sems
        self.send_a, self.recv_a, self.send_b, self.recv_b = send_a, recv_a, send_b, recv_b
        self.sems = (sa_send, sa_recv, sb_send, sb_recv)
        x, y, c = _position()
        self.c, self.sibling = c, (x, y, 1 - c)
        self.chips = [(x, y)] + [(_flip(x, fx), _flip(y, fy)) for fx, fy in OTHER_CHIPS]

    def _copy_a(self, j):
        return pltpu.make_async_remote_copy(src_ref=self.send_a.at[j], dst_ref=self.recv_a.at[j], send_sem=self.sems[0].at[j],
                                            recv_sem=self.sems[1].at[j], device_id=self.sibling, device_id_type=MESH)

    def _copy_b(self, j):
        return pltpu.make_async_remote_copy(src_ref=self.send_b.at[j], dst_ref=self.recv_b.at[j], send_sem=self.sems[2].at[j],
                                            recv_sem=self.sems[3].at[j], device_id=(*self.chips[1 + j], self.c),
                                            device_id_type=MESH)

    def _load(self, j, core):
        return pltpu.make_async_copy(self.g.at[_dev_index(*self.chips[j], core)], self.stage.at[j % 2], self.load_sems.at[j % 2])

    def start(self):
        self._load(0, 1 - self.c).start()
        for j in range(len(self.chips)):
            self._load(j, 1 - self.c).wait()
            if j + 1 < len(self.chips):
                self._load(j + 1, 1 - self.c).start()
            self.send_a[j] = self.stage[j % 2].astype(BF16)
            self._copy_a(j).start()

    def middle(self):
        self._load(0, self.c).start()
        for j in range(len(self.chips)):
            self._load(j, self.c).wait()
            if j + 1 < len(self.chips):
                self._load(j + 1, self.c).start()
            self._copy_a(j).wait_recv()
            part = self.stage[j % 2] + self.recv_a[j].astype(F32)
            if j == 0:
                self.out[...] = part
            else:
                self.send_b[j - 1] = part.astype(BF16)
                self._copy_b(j - 1).start()

    def finish(self):
        for j in range(len(OTHER_CHIPS)):
            self._copy_b(j).wait_recv()
            self.out[...] += self.recv_b[j].astype(F32)
        for j in range(len(self.chips)):
            self._copy_a(j).wait_send()
        for j in range(len(OTHER_CHIPS)):
            self._copy_b(j).wait_send()


N_SCATTER_SCRATCH = 10


def _scatter_specs(g8s):
    na, nb = 1 + len(OTHER_CHIPS), len(OTHER_CHIPS)
    ins = [pl.BlockSpec(memory_space=pl.ANY) for _ in g8s]
    outs = [pl.BlockSpec(g.shape[1:], lambda *_: (0, 0)) for g in g8s]
    shapes = [jax.ShapeDtypeStruct(g.shape[1:], F32) for g in g8s]
    scratch = []
    for g in g8s:
        blk = g.shape[1:]
        scratch += [pltpu.VMEM((2,) + blk, F32), pltpu.SemaphoreType.DMA((2,)), pltpu.VMEM((na,) + blk, BF16), pltpu.VMEM((na,) + blk, BF16),
                    pltpu.VMEM((nb,) + blk, BF16), pltpu.VMEM((nb,) + blk, BF16),
                    pltpu.SemaphoreType.DMA((na,)), pltpu.SemaphoreType.DMA((na,)),
                    pltpu.SemaphoreType.DMA((nb,)), pltpu.SemaphoreType.DMA((nb,))]
    return ins, outs, shapes, scratch


def _run_scatters(scatters, step, n_steps):
    @pl.when(step == 0)
    def _():
        for r in scatters:
            r.start()

    @pl.when(step == min(max(1, n_steps // 4), n_steps - 1))
    def _():
        for r in scatters:
            r.middle()

    @pl.when(step == n_steps - 1)
    def _():
        for r in scatters:
            r.finish()


def _mix_proj(x, g_mix, w_in_t, b_qkv, b_cin):
    s = x.shape[0]
    tm = _tile(s)

    def body(x_ref, g_ref, w_ref, bq_ref, bc_ref, qkv_ref, cin_ref, h1_ref):
        xv = x_ref[...]
        r = lax.rsqrt(jnp.mean(xv * xv, axis=-1, keepdims=True) + EPS)
        h = (xv * r * g_ref[...]).astype(BF16)
        h1_ref[...] = h
        qkv_ref[...] = _dot_nt(h, w_ref[0:QKV_COLS, :]) + bq_ref[...]
        cin_ref[...] = _dot_nt(h, w_ref[QKV_COLS:, :]) + bc_ref[...]

    return pl.pallas_call(
        body, grid=(s // tm,),
        in_specs=[pl.BlockSpec((tm, D_MODEL), lambda i: (i, 0)), _resident((1, D_MODEL)),
                  _resident((QKV_COLS + CIN_COLS, D_MODEL)), _resident((1, QKV_COLS)), _resident((1, CIN_COLS))],
        out_specs=[pl.BlockSpec((tm, QKV_COLS), lambda i: (i, 0)), pl.BlockSpec((tm, CIN_COLS), lambda i: (i, 0)),
                   pl.BlockSpec((tm, D_MODEL), lambda i: (i, 0))],
        out_shape=[jax.ShapeDtypeStruct((s, QKV_COLS), F32), jax.ShapeDtypeStruct((s, CIN_COLS), F32),
                   jax.ShapeDtypeStruct((s, D_MODEL), BF16)],
        compiler_params=_cparams("parallel"), name="mix_proj")(x, g_mix, w_in_t, b_qkv, b_cin)


def _kv_variants(kv_all, gk2, lo):
    k_all = kv_all[:, :LANES]
    v_all = kv_all[:, LANES:]
    kn_pre, rk = _head_norm(k_all, lo)
    kn = kn_pre * gk2
    kr = pltpu.roll(kn, HEAD_DIM, 1)
    vr = pltpu.roll(v_all, HEAD_DIM, 1)
    zero = jnp.zeros_like(kn)
    k_lo = [jnp.where(lo, kn, zero).astype(BF16), jnp.where(lo, kr, zero).astype(BF16)]
    k_hi = [jnp.where(lo, zero, kr).astype(BF16), jnp.where(lo, zero, kn).astype(BF16)]
    v_lo = [jnp.where(lo, v_all, zero).astype(BF16), jnp.where(lo, vr, zero).astype(BF16)]
    v_hi = [jnp.where(lo, zero, vr).astype(BF16), jnp.where(lo, zero, v_all).astype(BF16)]
    return k_lo, k_hi, v_lo, v_hi, kn_pre, rk


def _att_consts(first_tile, b):
    rows = 2 * ATT_BLOCK
    qi = lax.broadcasted_iota(jnp.int32, (rows, 2 * ATT_BLOCK), 0) % ATT_BLOCK
    kj = lax.broadcasted_iota(jnp.int32, (rows, 2 * ATT_BLOCK), 1)
    rel = qi + ATT_BLOCK - kj
    valid = (rel >= 0) & (rel < ATT_BLOCK)
    if b == 0:
        valid = valid & ((kj >= ATT_BLOCK) | jnp.logical_not(first_tile))
    return rel.astype(F32), valid


def _row_const(va, vb):
    top = lax.broadcasted_iota(jnp.int32, (2 * ATT_BLOCK, 1), 0) < ATT_BLOCK
    return jnp.where(top, va, vb)


def _probs(q2, k_op, rel, valid, slope, sink):
    sc = _dot_nt(q2, k_op) * (1.0 / math.sqrt(HEAD_DIM)) - slope * rel
    sc = jnp.where(valid, sc, NEG_INF)
    m = jnp.maximum(jnp.max(sc, axis=-1, keepdims=True), sink)
    p = jnp.exp(sc - m)
    e_sink = jnp.exp(sink - m)
    inv = 1.0 / (jnp.sum(p, axis=-1, keepdims=True) + e_sink)
    return p * inv, e_sink * inv


def _attn_fwd(qkv, gq2, gk2, sinks, shards):
    s = qkv.shape[0]
    tq = _tile(s)
    nb = tq // ATT_BLOCK
    ng = len(shards)
    g_in, g_out, g_shape, g_scratch = _gather_specs(shards)

    def body(q_ref, kv_ref, kvp_ref, gq_ref, gk_ref, sink_ref, *rest):
        out_ref = rest[ng]
        i = pl.program_id(0)
        _run_gathers([_Gather(rest[a], rest[ng + 1 + a], *rest[2 * ng + 1 + 4 * a:2 * ng + 5 + 4 * a]) for a in range(ng)],
                     i, s // tq)
        lo = _lo_mask((1, LANES))
        kv_all = jnp.concatenate([kvp_ref[...], kv_ref[...]], axis=0)
        k_lo, k_hi, v_lo, v_hi, _, _ = _kv_variants(kv_all, gk_ref[...], lo)
        for b in range(nb):
            rel, valid = _att_consts(i == 0, b)
            rows = slice(b * ATT_BLOCK, (b + 1) * ATT_BLOCK)
            keys = slice(b * ATT_BLOCK, (b + 2) * ATT_BLOCK)
            for kvh in range(N_KV_HEADS):
                pairs = (2 * kvh, 2 * kvh + 1)
                q2 = jnp.concatenate([q_ref[rows, p * LANES:(p + 1) * LANES] for p in pairs], axis=0)
                qn, _ = _head_norm(q2, lo)
                q2 = (qn * gq_ref[...]).astype(BF16)
                out = None
                for odd, (k_op, v_op) in enumerate(((k_lo[kvh][keys], v_lo[kvh][keys]), (k_hi[kvh][keys], v_hi[kvh][keys]))):
                    ha, hb = 2 * pairs[0] + odd, 2 * pairs[1] + odd
                    p, _ = _probs(q2, k_op, rel, valid, _row_const(SLOPES[ha], SLOPES[hb]),
                                  _row_const(sink_ref[ha], sink_ref[hb]))
                    o = _dot(p.astype(BF16), v_op)
                    out = o if out is None else out + o
                for n, p in enumerate(pairs):
                    out_ref[rows, p * LANES:(p + 1) * LANES] = out[n * ATT_BLOCK:(n + 1) * ATT_BLOCK].astype(BF16)

    return pl.pallas_call(
        body, grid=(s // tq,),
        in_specs=[pl.BlockSpec((tq, Q_COLS), lambda i: (i, 0)),
                  pl.BlockSpec((tq, 2 * KV_COLS), lambda i: (i, 2)),
                  pl.BlockSpec((ATT_BLOCK, 2 * KV_COLS), lambda i: (jnp.maximum(i * nb - 1, 0), 2)),
                  _resident((1, LANES)), _resident((1, LANES)),
                  pl.BlockSpec(memory_space=pltpu.SMEM)] + g_in,
        out_specs=[pl.BlockSpec((tq, Q_COLS), lambda i: (i, 0))] + g_out,
        out_shape=[jax.ShapeDtypeStruct((s, Q_COLS), BF16)] + g_shape,
        scratch_shapes=g_scratch,
        compiler_params=_cparams("arbitrary"), name="attn_fwd")(qkv, qkv, qkv, gq2, gk2, sinks, *shards)


def _group_stats(c1, lo):
    mu = _half_sums(c1, lo) * (1.0 / HEAD_DIM)
    d = c1 - mu
    rstd = lax.rsqrt(_half_sums(d * d, lo) * (1.0 / HEAD_DIM) + EPS)
    return d * rstd, rstd


def _rows(ref, first_row, n):
    return ref[pl.ds(first_row, n, stride=1), :].reshape(n // SUBLANES, SUBLANES, LANES)


def _conv_fwd(cin, cw8, cb, gain, bias, shards):
    s = cin.shape[0]
    tm = _tile(s)
    rc = 64
    nchunk = CONV_WIDTH // LANES
    lead = CONV_HALO - (CONV_KERNEL - 1)
    ng = len(shards)
    g_in, g_out, g_shape, g_scratch = _gather_specs(shards)

    def body(cin_ref, cw_ref, cb_ref, gain_ref, bias_ref, *rest):
        c3_ref, c1_ref, ext_ref = rest[ng], rest[ng + 1], rest[2 * ng + 2]
        _run_gathers([_Gather(rest[a], rest[ng + 2 + a], *rest[2 * ng + 3 + 4 * a:2 * ng + 7 + 4 * a]) for a in range(ng)],
                     pl.program_id(0), s // tm)

        @pl.when(pl.program_id(0) == 0)
        def _():
            ext_ref[:, 0:CONV_HALO, :] = jnp.zeros((nchunk, CONV_HALO, LANES), F32)

        lo = _lo_mask((1, LANES))
        for cc in range(nchunk):
            cols = slice(cc * LANES, (cc + 1) * LANES)
            gcols = slice(CONV_WIDTH + cc * LANES, CONV_WIDTH + (cc + 1) * LANES)
            ext_ref[cc, CONV_HALO:CONV_HALO + tm, :] = cin_ref[:, cols] * _sigmoid(cin_ref[:, gcols])
            ext = ext_ref.at[cc]
            for r in range(tm // rc):
                rows = slice(r * rc, (r + 1) * rc)
                acc = jnp.zeros((rc // SUBLANES, SUBLANES, LANES), F32)
                for k in range(CONV_KERNEL):
                    acc = acc + cw_ref[k * SUBLANES:(k + 1) * SUBLANES, cols][None] * _rows(ext, r * rc + lead + k, rc)
                c1 = acc.reshape(rc, LANES) + cb_ref[:, cols]
                c1_ref[cc, rows, :] = c1
                nrm, _ = _group_stats(c1, lo)
                c2 = nrm * gain_ref[:, cols] + bias_ref[:, cols]
                c3_ref[rows, cols] = (c2 * _sigmoid(c2)).astype(BF16)
        ext_ref[:, 0:CONV_HALO, :] = ext_ref[:, tm:tm + CONV_HALO, :]

    return pl.pallas_call(
        body, grid=(s // tm,),
        in_specs=[pl.BlockSpec((tm, CIN_COLS), lambda i: (i, 0)), _resident((CONV_KERNEL * SUBLANES, CONV_WIDTH)),
                  _resident((1, CONV_WIDTH)), _resident((1, CONV_WIDTH)), _resident((1, CONV_WIDTH))] + g_in,
        out_specs=[pl.BlockSpec((tm, CONV_WIDTH), lambda i: (i, 0)), pl.BlockSpec((nchunk, tm, LANES), lambda i: (0, i, 0))] + g_out,
        out_shape=[jax.ShapeDtypeStruct((s, CONV_WIDTH), BF16), jax.ShapeDtypeStruct((nchunk, s, LANES), F32)] + g_shape,
        scratch_shapes=[pltpu.VMEM((nchunk, tm + CONV_HALO, LANES), F32)] + g_scratch,
        compiler_params=_cparams("arbitrary"), name="conv_fwd")(cin, cw8, cb, gain, bias, *shards)


def _out_proj(x, attn, c3, wo_a, wo_c, b_out, g_ffn):
    s = x.shape[0]
    tm = _tile(s)

    def body(x_ref, a_ref, c_ref, wa_ref, wc_ref, b_ref, g_ref, x2_ref, h2_ref):
        x2 = x_ref[...] + _dot(a_ref[...], wa_ref[...]) + _dot(c_ref[...], wc_ref[...]) + b_ref[...]
        x2_ref[...] = x2
        r = lax.rsqrt(jnp.mean(x2 * x2, axis=-1, keepdims=True) + EPS)
        h2_ref[...] = (x2 * r * g_ref[...]).astype(BF16)

    return pl.pallas_call(
        body, grid=(s // tm,),
        in_specs=[pl.BlockSpec((tm, D_MODEL), lambda i: (i, 0)), pl.BlockSpec((tm, Q_COLS), lambda i: (i, 0)),
                  pl.BlockSpec((tm, CONV_WIDTH), lambda i: (i, 0)),
                  pl.BlockSpec((Q_COLS, D_MODEL), lambda i: (0, 0), pipeline_mode=pl.Buffered(1)),
                  pl.BlockSpec((CONV_WIDTH, D_MODEL), lambda i: (1, 0), pipeline_mode=pl.Buffered(1)),
                  _resident((1, D_MODEL)), _resident((1, D_MODEL))],
        out_specs=[pl.BlockSpec((tm, D_MODEL), lambda i: (i, 0)), pl.BlockSpec((tm, D_MODEL), lambda i: (i, 0))],
        out_shape=[jax.ShapeDtypeStruct((s, D_MODEL), F32), jax.ShapeDtypeStruct((s, D_MODEL), BF16)],
        compiler_params=_cparams("parallel"), name="out_proj")(x, attn, c3, wo_a, wo_c, b_out, g_ffn)


FF_LANE_CHUNKS = -(-FF_CHUNK // LANES)
FF_PADDED = FF_LANE_CHUNKS * LANES


def _tap(ref, first_row, n):
    return ref[pl.ds(first_row, n, stride=1), :]


def _ffn_fwd(h2, x2, target, w_up, fw, fb, w_down):
    s = h2.shape[0]
    tm = _tile(s)
    hal = SUBLANES
    rc = min(128, tm)

    def body(h_ref, x2_ref, t_ref, wu_ref, fw_ref, fb_ref, wd_ref, up0_ref, act_ref, dy_ref, loss_ref,
             ext_ref, carry_ref, act_buf, y_ref):
        i, ci = pl.program_id(0), pl.program_id(1)

        @pl.when((i == 0) & (ci == 0))
        def _():
            carry_ref[...] = jnp.zeros(carry_ref.shape, F32)
            ext_ref[...] = jnp.zeros(ext_ref.shape, F32)
            act_buf[...] = jnp.zeros(act_buf.shape, BF16)
            loss_ref[...] = jnp.zeros((1, 1), F32)

        @pl.when(ci == 0)
        def _():
            y_ref[...] = x2_ref[...]

        ws = (fw_ref[ci], fw_ref[ci + N_FF_PAIRS])
        bs = (fb_ref[ci], fb_ref[ci + N_FF_PAIRS])
        half_rows = (slice(0, tm // 2), slice(tm // 2, tm))
        n_grp = len(FF_COLS)

        def up_slices(grp):
            lo_c, hi_c = FF_COLS[grp]
            chunks = range(lo_c // LANES, -(-hi_c // LANES))

            def make(half, n, rows):
                def run():
                    c = ci + half * N_FF_PAIRS
                    u0 = _dot_nt(h_ref[rows, :], wu_ref[c, lo_c:hi_c, :])
                    up0_ref[half, 0, rows, lo_c:hi_c] = u0.astype(BF16)
                    for j in chunks:
                        w = min(LANES, hi_c - j * LANES)
                        if n == 0:
                            ext_ref[half, j, 0:hal, 0:w] = carry_ref[c, :, j * LANES:j * LANES + w]
                        ext_ref[half, j, hal + rows.start:hal + rows.stop, 0:w] = u0[:, j * LANES - lo_c:j * LANES - lo_c + w]
                    if n == len(half_rows) - 1:
                        carry_ref[c, :, lo_c:hi_c] = u0[u0.shape[0] - hal:, :]
                return run
            return [make(half, n, rows) for half in range(2) for n, rows in enumerate(half_rows)]

        def down_slices(grp):
            lo_c, hi_c = FF_COLS[grp]

            def make(rows):
                def run():
                    y_ref[rows, :] += _dot(act_buf[rows, lo_c:hi_c], wd_ref[ci, lo_c:hi_c, :])
                return run
            return [make(rows) for rows in half_rows]

        def vector_blocks(grp):
            lo_c, hi_c = FF_COLS[grp]
            blocks = []
            for j in range(lo_c // LANES, -(-hi_c // LANES)):
                lanes = slice(j * LANES, (j + 1) * LANES)

                def gate(r, lanes=lanes, j=j):
                    base = r * rc
                    ups = []
                    for half in range(2):
                        e, w = ext_ref.at[half, j], ws[half]
                        ups.append(w[0:1, lanes] * _tap(e, base + hal - 2, rc) + w[1:2, lanes] * _tap(e, base + hal - 1, rc)
                                   + w[2:3, lanes] * _tap(e, base + hal, rc) + bs[half][:, lanes])
                    g, u = ups
                    act_buf[base:base + rc, lanes] = (g * _sigmoid(g) * u).astype(BF16)

                blocks += [functools.partial(gate, r) for r in range(tm // rc)]

            def finish():
                act_ref[0, :, lo_c:hi_c] = act_buf[:, lo_c:hi_c]
            blocks.append(finish)
            return blocks

        for run in up_slices(0):
            run()
        for grp in range(n_grp):
            matmuls = (up_slices(grp + 1) if grp + 1 < n_grp else []) + (down_slices(grp - 1) if grp > 0 else [])
            blocks = vector_blocks(grp)
            every = max(1, len(blocks) // (len(matmuls) + 1))
            for n, run in enumerate(blocks):
                run()
                if n % every == every - 1 and matmuls:
                    matmuls.pop(0)()
            for run in matmuls:
                run()
        for run in down_slices(n_grp - 1):
            run()

        @pl.when(ci == N_FF_PAIRS - 1)
        def _():
            e = y_ref[...] - t_ref[...]
            dy_ref[...] = e * (1.0 / D_MODEL)
            loss_ref[...] += (0.5 / D_MODEL) * jnp.sum(e * e).reshape(1, 1)

    tok = lambda i, ci: (i, 0)
    return pl.pallas_call(
        body, grid=(s // tm, N_FF_PAIRS),
        in_specs=[pl.BlockSpec((tm, D_MODEL), tok), pl.BlockSpec((tm, D_MODEL), tok), pl.BlockSpec((tm, D_MODEL), tok),
                  _resident((N_DEV, FF_CHUNK, D_MODEL)), _resident((N_DEV, 3, FF_PADDED)), _resident((N_DEV, 1, FF_PADDED)),
                  _resident((N_FF_PAIRS, FF_CHUNK, D_MODEL))],
        out_specs=[pl.BlockSpec((2, 1, tm, FF_CHUNK), lambda i, ci: (0, ci, i, 0)),
                   pl.BlockSpec((1, tm, FF_CHUNK), lambda i, ci: (ci, i, 0)),
                   pl.BlockSpec((tm, D_MODEL), tok), pl.BlockSpec((1, 1), lambda i, ci: (0, 0))],
        out_shape=[jax.ShapeDtypeStruct((2, N_FF_PAIRS, s, FF_CHUNK), BF16),
                   jax.ShapeDtypeStruct((N_FF_PAIRS, s, FF_CHUNK), BF16), jax.ShapeDtypeStruct((s, D_MODEL), F32),
                   jax.ShapeDtypeStruct((1, 1), F32)],
        scratch_shapes=[pltpu.VMEM((2, FF_LANE_CHUNKS, tm + hal, LANES), F32), pltpu.VMEM((N_DEV, hal, FF_CHUNK), F32),
                        pltpu.VMEM((tm, FF_PADDED), BF16), pltpu.VMEM((tm, D_MODEL), F32)],
        compiler_params=_cparams("arbitrary", "arbitrary"), name="ffn_fwd")(h2, x2, target, w_up, fw, fb, w_down)


def _ffn_bwd(dy, up0, w_up, fw, fb, w_down):
    s = dy.shape[0]
    tm = _tile(s)
    nt = s // tm
    hal = 2 * SUBLANES
    nxt = SUBLANES
    rc = min(128, tm)

    def body(dy_ref, up0_ref, up0h_ref, wu_ref, fw_ref, fb_ref, wd_ref,
             dup0_ref, dh2_ref, dfw_ref, dfb_ref, ext_ref, dext_ref, carry_ref, dact_buf, dup0_buf):
        i, ci = pl.program_id(0), pl.program_id(1)
        t = nt - 1 - i

        @pl.when((i == 0) & (ci == 0))
        def _():
            for ref in (carry_ref, dfw_ref, dfb_ref, ext_ref, dext_ref, dact_buf):
                ref[...] = jnp.zeros(ref.shape, F32)
            dup0_buf[...] = jnp.zeros(dup0_buf.shape, BF16)

        @pl.when(ci == 0)
        def _():
            dh2_ref[...] = jnp.zeros(dh2_ref.shape, F32)

        ws = (fw_ref[ci], fw_ref[ci + N_FF_PAIRS])
        bs = (fb_ref[ci], fb_ref[ci + N_FF_PAIRS])
        fold = lambda v: jnp.sum(v.reshape(rc // SUBLANES, SUBLANES, LANES), axis=0)
        half_rows = (slice(0, tm // 2), slice(tm // 2, tm))
        n_grp = len(FF_COLS)

        def dact_slices(grp, pair):
            lo_c, hi_c = FF_COLS[grp]

            def make(rows):
                def run():
                    dact_buf[rows, lo_c:hi_c] = _dot_nt(dy_ref[rows, :].astype(BF16), wd_ref[pair, lo_c:hi_c, :])
                return run
            return [make(rows) for rows in half_rows]

        def dh2_slices(grp, pair):
            lo_c, hi_c = FF_COLS[grp]

            def make(half, rows):
                def run():
                    c = pair + half * N_FF_PAIRS
                    dh2_ref[rows, :] += _dot(dup0_buf[half, rows, lo_c:hi_c], wu_ref[c, lo_c:hi_c, :])
                return run
            return [make(half, rows) for half in range(2) for rows in half_rows]

        def vector_blocks(grp):
            lo_c, hi_c = FF_COLS[grp]
            chunks = range(lo_c // LANES, -(-hi_c // LANES))
            blocks = []

            def stage():
                for half in range(2):
                    c = ci + half * N_FF_PAIRS
                    prev = jnp.where(t > 0, up0h_ref[half, 0, :, lo_c:hi_c].astype(F32), 0.0)
                    cur = up0_ref[half, 0, :, lo_c:hi_c].astype(F32)
                    for j in chunks:
                        w = min(LANES, hi_c - j * LANES)
                        cols = slice(j * LANES - lo_c, j * LANES - lo_c + w)
                        ext_ref[half, j, 0:hal, 0:w] = prev[:, cols]
                        ext_ref[half, j, hal:hal + tm, 0:w] = cur[:, cols]
                        dext_ref[half, j, tm:tm + nxt, 0:w] = carry_ref[c, :, j * LANES:j * LANES + w]
            blocks.append(stage)
            for j in chunks:
                lanes = slice(j * LANES, (j + 1) * LANES)
                acc = [jnp.zeros((SUBLANES, LANES), F32)] * 8

                def grads(r, lanes=lanes, j=j, acc=acc):
                    base = r * rc
                    taps, ups = [], []
                    for half in range(2):
                        e, w = ext_ref.at[half, j], ws[half]
                        x = [_tap(e, base + hal - 2 + k, rc) for k in range(3)]
                        taps.append(x)
                        ups.append(w[0:1, lanes] * x[0] + w[1:2, lanes] * x[1] + w[2:3, lanes] * x[2] + bs[half][:, lanes])
                    g, u = ups
                    sg = _sigmoid(g)
                    dact = dact_buf[base:base + rc, lanes]
                    ds = (dact * u * (sg * (1.0 + g * (1.0 - sg))), dact * (g * sg))
                    for half in range(2):
                        dext_ref[half, j, base:base + rc, :] = ds[half]
                        acc[4 * half] = acc[4 * half] + fold(ds[half])
                        for k in range(3):
                            acc[4 * half + 1 + k] = acc[4 * half + 1 + k] + fold(ds[half] * taps[half][k])

                def sums(lanes=lanes, acc=acc):
                    for half in range(2):
                        c = ci + half * N_FF_PAIRS
                        dfb_ref[c, :, lanes] += jnp.sum(acc[4 * half], axis=0, keepdims=True)
                        dfw_ref[c, :, lanes] += jnp.concatenate(
                            [jnp.sum(acc[4 * half + 1 + k], axis=0, keepdims=True) for k in range(3)], axis=0)

                def conv_back(r, lanes=lanes, j=j):
                    base = r * rc
                    for half in range(2):
                        d, w = dext_ref.at[half, j], ws[half]
                        dup0 = w[2:3, lanes] * _tap(d, base, rc) + w[1:2, lanes] * _tap(d, base + 1, rc) + w[0:1, lanes] * _tap(d, base + 2, rc)
                        dup0_buf[half, base:base + rc, lanes] = dup0.astype(BF16)

                blocks += [functools.partial(grads, r) for r in range(tm // rc)] + [sums]
                blocks += [functools.partial(conv_back, r) for r in range(tm // rc)]

            def finish():
                for half in range(2):
                    c = ci + half * N_FF_PAIRS
                    for j in chunks:
                        w = min(LANES, hi_c - j * LANES)
                        carry_ref[c, :, j * LANES:j * LANES + w] = dext_ref[half, j, 0:nxt, 0:w]
                    dup0_ref[half, 0, :, lo_c:hi_c] = dup0_buf[half, :, lo_c:hi_c]
            blocks.append(finish)
            return blocks

        for run in dact_slices(0, ci):
            run()
        for grp in range(n_grp):
            matmuls = (dact_slices(grp + 1, ci) if grp + 1 < n_grp else []) + (dh2_slices(grp - 1, ci) if grp > 0 else [])
            blocks = vector_blocks(grp)
            every = max(1, len(blocks) // (len(matmuls) + 1))
            for n, run in enumerate(blocks):
                run()
                if n % every == every - 1 and matmuls:
                    matmuls.pop(0)()
            for run in matmuls:
                run()
        for run in dh2_slices(n_grp - 1, ci):
            run()

    tok = lambda i, ci: (nt - 1 - i, 0)
    acc = lambda shape: pl.BlockSpec(shape, lambda i, ci: (0,) * len(shape))
    return pl.pallas_call(
        body, grid=(nt, N_FF_PAIRS),
        in_specs=[pl.BlockSpec((tm, D_MODEL), tok),
                  pl.BlockSpec((2, 1, tm, FF_CHUNK), lambda i, ci: (0, ci, nt - 1 - i, 0)),
                  pl.BlockSpec((2, 1, hal, FF_CHUNK), lambda i, ci: (0, ci, jnp.maximum((nt - 1 - i) * (tm // hal) - 1, 0), 0)),
                  _resident((N_DEV, FF_CHUNK, D_MODEL)), _resident((N_DEV, 3, FF_PADDED)), _resident((N_DEV, 1, FF_PADDED)),
                  _resident((N_FF_PAIRS, FF_CHUNK, D_MODEL))],
        out_specs=[pl.BlockSpec((2, 1, tm, FF_CHUNK), lambda i, ci: (0, ci, nt - 1 - i, 0)),
                   pl.BlockSpec((tm, D_MODEL), tok), acc((N_DEV, 3, FF_PADDED)), acc((N_DEV, 1, FF_PADDED))],
        out_shape=[jax.ShapeDtypeStruct((2, N_FF_PAIRS, s, FF_CHUNK), BF16), jax.ShapeDtypeStruct((s, D_MODEL), F32),
                   jax.ShapeDtypeStruct((N_DEV, 3, FF_PADDED), F32), jax.ShapeDtypeStruct((N_DEV, 1, FF_PADDED), F32)],
        scratch_shapes=[pltpu.VMEM((2, FF_LANE_CHUNKS, tm + hal, LANES), F32), pltpu.VMEM((2, FF_LANE_CHUNKS, tm + nxt, LANES), F32),
                        pltpu.VMEM((N_DEV, nxt, FF_CHUNK), F32), pltpu.VMEM((tm, FF_PADDED), F32),
                        pltpu.VMEM((2, tm, FF_PADDED), BF16)],
        compiler_params=_cparams("arbitrary", "arbitrary"), name="ffn_bwd")(dy, up0, up0, w_up, fw, fb, w_down)


def _ffn_norm_bwd(dh2, dy, x2, g_ffn, w_out):
    s = dy.shape[0]
    tm = _tile(s)

    def body(dh_ref, dy_ref, x2_ref, g_ref, wo_ref, dx2_ref, dmix_ref, dg_ref, dbo_ref):
        @pl.when(pl.program_id(0) == 0)
        def _():
            dg_ref[...] = jnp.zeros(dg_ref.shape, F32)
            dbo_ref[...] = jnp.zeros(dbo_ref.shape, F32)

        x2v = x2_ref[...]
        r = lax.rsqrt(jnp.mean(x2v * x2v, axis=-1, keepdims=True) + EPS)
        n2 = x2v * r
        dh2 = dh_ref[...]
        dg_ref[...] += jnp.sum(dh2 * n2, axis=0, keepdims=True)
        dn = dh2 * g_ref[...]
        dx2 = dy_ref[...] + r * (dn - n2 * jnp.mean(dn * n2, axis=-1, keepdims=True))
        dx2_ref[...] = dx2
        dbo_ref[...] += jnp.sum(dx2, axis=0, keepdims=True)
        dmix_ref[...] = _dot_nt(dx2.astype(BF16), wo_ref[...])

    tok = pl.BlockSpec((tm, D_MODEL), lambda i: (i, 0))
    vec = pl.BlockSpec((1, D_MODEL), lambda i: (0, 0))
    return pl.pallas_call(
        body, grid=(s // tm,),
        in_specs=[tok, tok, tok, _resident((1, D_MODEL)), _resident((D_MODEL, D_MODEL))],
        out_specs=[tok, tok, vec, vec],
        out_shape=[jax.ShapeDtypeStruct((s, D_MODEL), F32), jax.ShapeDtypeStruct((s, D_MODEL), F32),
                   jax.ShapeDtypeStruct((1, D_MODEL), F32), jax.ShapeDtypeStruct((1, D_MODEL), F32)],
        compiler_params=_cparams("arbitrary"), name="ffn_norm_bwd")(dh2, dy, x2, g_ffn, w_out)


def _conv_bwd(dmixed, c1, cin, cw8, gain, bias, g8s):
    ns = len(g8s)
    s_in, s_out, s_shape, s_scratch = _scatter_specs(g8s)
    s = cin.shape[0]
    tm = _tile(s)
    nt = s // tm
    rc = 64
    rn = min(256, tm)
    hal = CONV_HALO
    lead = hal - (CONV_KERNEL - 1)
    nchunk = CONV_WIDTH // LANES

    def body(dc3_ref, dc3n_ref, c1_ref, c1n_ref, cin_ref, cinp_ref, cw_ref, gain_ref, bias_ref, *rest):
        dcin_ref, dcw_ref, dcb_ref, dgain_ref, dbias_ref, dbcin_ref = rest[ns:ns + 6]
        c0_ext, dc1_ext, dcw8 = rest[2 * ns + 6:2 * ns + 9]
        i = pl.program_id(0)
        first, last = i == 0, i == nt - 1
        own = rest[2 * ns + 9:]
        _run_scatters([_ReduceScatter(rest[a], rest[ns + 6 + a], *own[N_SCATTER_SCRATCH * a:N_SCATTER_SCRATCH * (a + 1)])
                       for a in range(ns)], i, nt)

        @pl.when(first)
        def _():
            for ref in (dcw8, dcb_ref, dgain_ref, dbias_ref, dbcin_ref):
                ref[...] = jnp.zeros(ref.shape, F32)

        lo = _lo_mask((1, LANES))

        def norm_bwd(dc3, c1v, cols):
            nrm, rstd = _group_stats(c1v, lo)
            c2 = nrm * gain_ref[:, cols] + bias_ref[:, cols]
            sg = _sigmoid(c2)
            dc2 = dc3 * (sg * (1.0 + c2 * (1.0 - sg)))
            dn = dc2 * gain_ref[:, cols]
            inv = 1.0 / HEAD_DIM
            dc1 = rstd * (dn - _half_sums(dn, lo) * inv - nrm * (_half_sums(dn * nrm, lo) * inv))
            return dc1, dc2, nrm

        def row_sum(v):
            return jnp.sum(v, axis=0, keepdims=True)

        for cc in range(nchunk):
            cols = slice(cc * LANES, (cc + 1) * LANES)
            gcols = slice(CONV_WIDTH + cc * LANES, CONV_WIDTH + (cc + 1) * LANES)
            c0e, d1e = c0_ext.at[cc], dc1_ext.at[cc]
            c0e[0:hal, :] = jnp.where(first, 0.0, cinp_ref[:, cols] * _sigmoid(cinp_ref[:, gcols]))
            dc1n, _, _ = norm_bwd(dc3n_ref[:, cols], c1n_ref[cc], cols)
            d1e[tm:tm + hal, :] = jnp.where(last, 0.0, dc1n)

            for r in range(tm // rn):
                rows = slice(r * rn, (r + 1) * rn)
                c0e[hal + r * rn:hal + (r + 1) * rn, :] = cin_ref[rows, cols] * _sigmoid(cin_ref[rows, gcols])
                dc1, dc2, nrm = norm_bwd(dc3_ref[rows, cols], c1_ref[cc, rows, :], cols)
                d1e[rows, :] = dc1
                dgain_ref[:, cols] += row_sum(dc2 * nrm)
                dbias_ref[:, cols] += row_sum(dc2)
                dcb_ref[:, cols] += row_sum(dc1)
            zero = jnp.zeros((1, LANES), F32)

            for k0 in range(0, CONV_KERNEL, SUBLANES):
                taps = range(k0, min(k0 + SUBLANES, CONV_KERNEL))

                def tap_sums(r, acc, taps=taps):
                    d = _rows(d1e, r * rc, rc)
                    return tuple(a + jnp.sum(d * _rows(c0e, r * rc + lead + k, rc), axis=0) for a, k in zip(acc, taps))

                acc = lax.fori_loop(0, tm // rc, tap_sums, tuple(dcw8[k * SUBLANES:(k + 1) * SUBLANES, cols] for k in taps))
                for a, k in zip(acc, taps):
                    dcw8[k * SUBLANES:(k + 1) * SUBLANES, cols] = a

            def input_grad(r, sums):
                rows = pl.ds(pl.multiple_of(r * rc, rc), rc)
                dc0 = jnp.zeros((rc // SUBLANES, SUBLANES, LANES), F32)
                for k in range(CONV_KERNEL):
                    dc0 = dc0 + cw_ref[k * SUBLANES:(k + 1) * SUBLANES, cols][None] * _rows(d1e, r * rc + CONV_KERNEL - 1 - k, rc)
                dc0 = dc0.reshape(rc, LANES)
                sg = _sigmoid(cin_ref[rows, gcols])
                da = dc0 * sg
                dgate = dc0 * cin_ref[rows, cols] * sg * (1.0 - sg)
                dcin_ref[rows, cols] = da.astype(BF16)
                dcin_ref[rows, gcols] = dgate.astype(BF16)
                return sums[0] + row_sum(da), sums[1] + row_sum(dgate)

            sums = lax.fori_loop(0, tm // rc, input_grad, (zero, zero))
            dbcin_ref[:, cols] += sums[0]
            dbcin_ref[:, gcols] += sums[1]

        @pl.when(last)
        def _():
            for k in range(CONV_KERNEL):
                dcw_ref[k:k + 1, :] = jnp.sum(dcw8[k * SUBLANES:(k + 1) * SUBLANES, :], axis=0, keepdims=True)

    nh = tm // hal
    acc = lambda shape: pl.BlockSpec(shape, lambda i: (0,) * len(shape))
    return pl.pallas_call(
        body, grid=(nt,),
        in_specs=[pl.BlockSpec((tm, CONV_WIDTH), lambda i: (i, 1)),
                  pl.BlockSpec((hal, CONV_WIDTH), lambda i: (jnp.minimum((i + 1) * nh, s // hal - 1), 1)),
                  pl.BlockSpec((nchunk, tm, LANES), lambda i: (0, i, 0)),
                  pl.BlockSpec((nchunk, hal, LANES), lambda i: (0, jnp.minimum((i + 1) * nh, s // hal - 1), 0)),
                  pl.BlockSpec((tm, CIN_COLS), lambda i: (i, 0)),
                  pl.BlockSpec((hal, CIN_COLS), lambda i: (jnp.maximum(i * nh - 1, 0), 0)),
                  _resident((CONV_KERNEL * SUBLANES, CONV_WIDTH)), _resident((1, CONV_WIDTH)), _resident((1, CONV_WIDTH))] + s_in,
        out_specs=[pl.BlockSpec((tm, CIN_COLS), lambda i: (i, 0)), acc((CONV_KERNEL, CONV_WIDTH)), acc((1, CONV_WIDTH)),
                   acc((1, CONV_WIDTH)), acc((1, CONV_WIDTH)), acc((1, CIN_COLS))] + s_out,
        out_shape=[jax.ShapeDtypeStruct((s, CIN_COLS), BF16), jax.ShapeDtypeStruct((CONV_KERNEL, CONV_WIDTH), F32),
                   jax.ShapeDtypeStruct((1, CONV_WIDTH), F32), jax.ShapeDtypeStruct((1, CONV_WIDTH), F32),
                   jax.ShapeDtypeStruct((1, CONV_WIDTH), F32), jax.ShapeDtypeStruct((1, CIN_COLS), F32)] + s_shape,
        scratch_shapes=[pltpu.VMEM((nchunk, tm + hal, LANES), F32), pltpu.VMEM((nchunk, tm + hal, LANES), F32),
                        pltpu.VMEM((CONV_KERNEL * SUBLANES, CONV_WIDTH), F32)] + s_scratch,
        compiler_params=_cparams("arbitrary"), name="conv_bwd")(dmixed, dmixed, c1, c1, cin, cin, cw8, gain, bias, *g8s)


def _attn_bwd(qkv, dmixed, gq2, gk2, sinks, g8s):
    ns = len(g8s)
    s_in, s_out, s_shape, s_scratch = _scatter_specs(g8s)
    s = qkv.shape[0]
    tq = _tile(s)
    nb = tq // ATT_BLOCK
    nt = s // tq

    def body(q_ref, kv_ref, kvp_ref, do_ref, gq_ref, gk_ref, sink_ref, *rest):
        dqkv_ref, dgq_ref, dgk_ref, dsink_ref, dbqkv_ref = rest[ns:ns + 5]
        dk_acc, dv_acc, carry_k, carry_v = rest[2 * ns + 5:2 * ns + 9]
        i = pl.program_id(0)
        t = nt - 1 - i
        own = rest[2 * ns + 9:]
        _run_scatters([_ReduceScatter(rest[a], rest[ns + 5 + a], *own[N_SCATTER_SCRATCH * a:N_SCATTER_SCRATCH * (a + 1)])
                       for a in range(ns)], i, nt)

        @pl.when(i == 0)
        def _():
            for ref in (carry_k, carry_v, dgq_ref, dgk_ref, dsink_ref, dbqkv_ref):
                ref[...] = jnp.zeros(ref.shape, F32)

        lo = _lo_mask((1, LANES))
        lane_id = lax.broadcasted_iota(jnp.int32, (1, LANES), 1)
        kv_all = jnp.concatenate([kvp_ref[...], kv_ref[...]], axis=0)
        k_lo, k_hi, v_lo, v_hi, kn_pre, rk = _kv_variants(kv_all, gk_ref[...], lo)
        for acc_ref, carry in ((dk_acc, carry_k), (dv_acc, carry_v)):
            acc_ref[:, 0:tq, :] = jnp.zeros((N_KV_HEADS, tq, LANES), F32)
            acc_ref[:, tq:tq + ATT_BLOCK, :] = carry[...]
        dsink = jnp.zeros((1, LANES), F32)
        dgq = jnp.zeros((1, LANES), F32)
        gq = gq_ref[...]
        for b in range(nb):
            rel, valid = _att_consts(t == 0, b)
            rows = slice(b * ATT_BLOCK, (b + 1) * ATT_BLOCK)
            keys = slice(b * ATT_BLOCK, (b + 2) * ATT_BLOCK)
            for kvh in range(N_KV_HEADS):
                pairs = (2 * kvh, 2 * kvh + 1)
                q_raw = jnp.concatenate([q_ref[rows, p * LANES:(p + 1) * LANES] for p in pairs], axis=0)
                qn_pre, rq = _head_norm(q_raw, lo)
                q2 = (qn_pre * gq).astype(BF16)
                do2 = jnp.concatenate([do_ref[rows, p * LANES:(p + 1) * LANES] for p in pairs], axis=0).astype(BF16)
                dq2 = jnp.zeros((2 * ATT_BLOCK, LANES), F32)
                for odd, (k_op, v_op) in enumerate(((k_lo[kvh][keys], v_lo[kvh][keys]), (k_hi[kvh][keys], v_hi[kvh][keys]))):
                    ha, hb = 2 * pairs[0] + odd, 2 * pairs[1] + odd
                    p, p_sink = _probs(q2, k_op, rel, valid, _row_const(SLOPES[ha], SLOPES[hb]),
                                       _row_const(sink_ref[ha], sink_ref[hb]))
                    dp = _dot_nt(do2, v_op)
                    delta = jnp.sum(p * dp, axis=-1, keepdims=True)
                    ds = (p * (dp - delta) * (1.0 / math.sqrt(HEAD_DIM))).astype(BF16)
                    dsk = p_sink * delta
                    dsink = dsink - jnp.where(lane_id == ha, jnp.sum(dsk[0:ATT_BLOCK]), 0.0) \
                        - jnp.where(lane_id == hb, jnp.sum(dsk[ATT_BLOCK:]), 0.0)
                    dq2 = dq2 + _dot(ds, k_op)
                    half = lo if odd == 0 else jnp.logical_not(lo)
                    dk_acc[kvh, keys, :] += jnp.where(half, _dot_tn(ds, q2), 0.0)
                    dv_acc[kvh, keys, :] += jnp.where(half, _dot_tn(p.astype(BF16), do2), 0.0)
                dgq = dgq + jnp.sum(dq2 * qn_pre, axis=0, keepdims=True)
                dq_raw = _head_norm_bwd(dq2 * gq, qn_pre, rq, lo)
                for n, p_ in enumerate(pairs):
                    blk = dq_raw[n * ATT_BLOCK:(n + 1) * ATT_BLOCK]
                    dqkv_ref[rows, p_ * LANES:(p_ + 1) * LANES] = blk.astype(BF16)
                    dbqkv_ref[:, p_ * LANES:(p_ + 1) * LANES] += jnp.sum(blk, axis=0, keepdims=True)
        carry_k[...] = dk_acc[:, 0:ATT_BLOCK, :]
        carry_v[...] = dv_acc[:, 0:ATT_BLOCK, :]

        def fold(acc_ref):
            both = []
            for kvh in range(N_KV_HEADS):
                a = acc_ref[kvh, ATT_BLOCK:ATT_BLOCK + tq, :]
                both.append(a + pltpu.roll(a, HEAD_DIM, 1))
            return jnp.where(lo, both[0], both[1])

        dkn = fold(dk_acc)
        dv = fold(dv_acc)
        kn_c, rk_c = kn_pre[ATT_BLOCK:], rk[ATT_BLOCK:]
        dgk_ref[...] += jnp.sum(dkn * kn_c, axis=0, keepdims=True)
        dk_raw = _head_norm_bwd(dkn * gk_ref[...], kn_c, rk_c, lo)
        dqkv_ref[:, Q_COLS:Q_COLS + KV_COLS] = dk_raw.astype(BF16)
        dqkv_ref[:, Q_COLS + KV_COLS:] = dv.astype(BF16)
        dbqkv_ref[:, Q_COLS:Q_COLS + KV_COLS] += jnp.sum(dk_raw, axis=0, keepdims=True)
        dbqkv_ref[:, Q_COLS + KV_COLS:] += jnp.sum(dv, axis=0, keepdims=True)
        dgq_ref[...] += dgq
        dsink_ref[...] += dsink

        @pl.when(i == nt - 1)
        def _():
            for ref in (dgq_ref, dgk_ref):
                v = ref[...]
                ref[...] = v + pltpu.roll(v, HEAD_DIM, 1)

    acc = lambda shape: pl.BlockSpec(shape, lambda i: (0,) * len(shape))
    return pl.pallas_call(
        body, grid=(nt,),
        in_specs=[pl.BlockSpec((tq, Q_COLS), lambda i: (nt - 1 - i, 0)),
                  pl.BlockSpec((tq, 2 * KV_COLS), lambda i: (nt - 1 - i, 2)),
                  pl.BlockSpec((ATT_BLOCK, 2 * KV_COLS), lambda i: (jnp.maximum((nt - 1 - i) * nb - 1, 0), 2)),
                  pl.BlockSpec((tq, Q_COLS), lambda i: (nt - 1 - i, 0)),
                  _resident((1, LANES)), _resident((1, LANES)), pl.BlockSpec(memory_space=pltpu.SMEM)] + s_in,
        out_specs=[pl.BlockSpec((tq, QKV_COLS), lambda i: (nt - 1 - i, 0)), acc((1, LANES)), acc((1, LANES)),
                   acc((1, LANES)), acc((1, QKV_COLS))] + s_out,
        out_shape=[jax.ShapeDtypeStruct((s, QKV_COLS), BF16), jax.ShapeDtypeStruct((1, LANES), F32),
                   jax.ShapeDtypeStruct((1, LANES), F32), jax.ShapeDtypeStruct((1, LANES), F32),
                   jax.ShapeDtypeStruct((1, QKV_COLS), F32)] + s_shape,
        scratch_shapes=[pltpu.VMEM((N_KV_HEADS, tq + ATT_BLOCK, LANES), F32), pltpu.VMEM((N_KV_HEADS, tq + ATT_BLOCK, LANES), F32),
                        pltpu.VMEM((N_KV_HEADS, ATT_BLOCK, LANES), F32), pltpu.VMEM((N_KV_HEADS, ATT_BLOCK, LANES), F32)] + s_scratch,
        compiler_params=_cparams("arbitrary"), name="attn_bwd")(qkv, qkv, qkv, dmixed, gq2, gk2, sinks, *g8s)


def _in_bwd(dqkv, dcin, w_in_t, x, dx2, g_mix):
    s = x.shape[0]
    tm = _tile(s)

    def body(dq_ref, dc_ref, w_ref, x_ref, dx2_ref, g_ref, gx_ref, dg_ref):
        @pl.when(pl.program_id(0) == 0)
        def _():
            dg_ref[...] = jnp.zeros(dg_ref.shape, F32)

        dh = _dot(dq_ref[...], w_ref[0:QKV_COLS, :]) + _dot(dc_ref[...], w_ref[QKV_COLS:, :])
        xv = x_ref[...]
        r = lax.rsqrt(jnp.mean(xv * xv, axis=-1, keepdims=True) + EPS)
        n = xv * r
        dg_ref[...] += jnp.sum(dh * n, axis=0, keepdims=True)
        dn = dh * g_ref[...]
        gx_ref[...] = dx2_ref[...] + r * (dn - n * jnp.mean(dn * n, axis=-1, keepdims=True))

    return pl.pallas_call(
        body, grid=(s // tm,),
        in_specs=[pl.BlockSpec((tm, QKV_COLS), lambda i: (i, 0)), pl.BlockSpec((tm, CIN_COLS), lambda i: (i, 0)),
                  _resident((QKV_COLS + CIN_COLS, D_MODEL)),
                  pl.BlockSpec((tm, D_MODEL), lambda i: (i, 0)), pl.BlockSpec((tm, D_MODEL), lambda i: (i, 0)),
                  _resident((1, D_MODEL))],
        out_specs=[pl.BlockSpec((tm, D_MODEL), lambda i: (i, 0)), pl.BlockSpec((1, D_MODEL), lambda i: (0, 0))],
        out_shape=[jax.ShapeDtypeStruct((s, D_MODEL), F32), jax.ShapeDtypeStruct((1, D_MODEL), F32)],
        compiler_params=_cparams("arbitrary"), name="in_bwd")(dqkv, dcin, w_in_t, x, dx2, g_mix)


def _tn_matmul(a, b, name):
    ga, s, m = a.shape
    gb, _, n = b.shape
    g = max(ga, gb)
    tk = min(TN_TOKENS, s)

    def body(a_ref, b_ref, o_ref):
        @pl.when(pl.program_id(1) == 0)
        def _():
            o_ref[...] = jnp.zeros(o_ref.shape, F32)

        o_ref[0] += _dot_tn(a_ref[0].astype(BF16), b_ref[0].astype(BF16))

    return pl.pallas_call(
        body, grid=(g, s // tk),
        in_specs=[pl.BlockSpec((1, tk, m), (lambda gi, k: (gi, k, 0)) if ga > 1 else (lambda gi, k: (0, k, 0))),
                  pl.BlockSpec((1, tk, n), (lambda gi, k: (gi, k, 0)) if gb > 1 else (lambda gi, k: (0, k, 0)))],
        out_specs=pl.BlockSpec((1, m, n), lambda gi, k: (gi, 0, 0)),
        out_shape=jax.ShapeDtypeStruct((g, m, n), F32),
        compiler_params=_cparams("parallel", "arbitrary"), name=name)(a, b)


def _allgather(shards, dtypes):
    n = len(shards)
    n_copies = 1 + 2 * len(OTHER_CHIPS)

    def body(*refs):
        ins, outs = refs[:n], refs[n:2 * n]
        send_sems, recv_sems = refs[2 * n:]
        x, y, c = _position()
        me, sibling = (x, y, c), (x, y, 1 - c)
        chips = [(_flip(x, fx), _flip(y, fy)) for fx, fy in OTHER_CHIPS]
        for a in range(n):
            outs[a][_dev_index(*me)] = ins[a][...].astype(dtypes[a])

        def copy(a, k, block, to):
            rows = outs[a].at[_dev_index(*block)]
            return pltpu.make_async_remote_copy(src_ref=rows, dst_ref=rows, send_sem=send_sems.at[a, k],
                                                recv_sem=recv_sems.at[a, k], device_id=to, device_id_type=MESH)

        started = []
        for a in range(n):
            for j, chip in enumerate(chips):
                started.append(copy(a, 1 + j, me, (*chip, c)))
            started.append(copy(a, 0, me, sibling))
        for cp in started:
            cp.start()
        for a in range(n):
            for j, chip in enumerate(chips):
                copy(a, 1 + j, (*chip, c), me).wait_recv()
                fwd = copy(a, 1 + len(chips) + j, (*chip, c), sibling)
                fwd.start()
                started.append(fwd)
        for a in range(n):
            copy(a, 0, sibling, me).wait_recv()
            for j, chip in enumerate(chips):
                copy(a, 1 + len(chips) + j, (*chip, 1 - c), me).wait_recv()
        for cp in started:
            cp.wait_send()

    vmem = pl.BlockSpec(memory_space=pltpu.VMEM)
    return pl.pallas_call(
        body, in_specs=[vmem] * n, out_specs=[vmem] * n,
        out_shape=[jax.ShapeDtypeStruct((N_DEV,) + w.shape, dt) for w, dt in zip(shards, dtypes)],
        scratch_shapes=[pltpu.SemaphoreType.DMA((n, n_copies)), pltpu.SemaphoreType.DMA((n, n_copies))],
        compiler_params=pltpu.CompilerParams(vmem_limit_bytes=VMEM_LIMIT), name="allgather_weights")(*shards)


def _final_exchange(g8, v):
    rows = v.shape[0]
    _, _, s_shape, s_scratch = _scatter_specs([g8])

    def body(g_ref, v_ref, gout_ref, vout_ref, gath, send_sems, recv_sems, *rs_scratch):
        scatter = _ReduceScatter(g_ref, gout_ref, *rs_scratch)
        x, y, c = _position()
        me = _dev_index(x, y, c)
        peers = [(_flip(x, k >> 2 & 1), _flip(y, k >> 1 & 1), _flip(c, k & 1)) for k in range(1, N_DEV)]

        def copy(k, block):
            return pltpu.make_async_remote_copy(src_ref=gath.at[block], dst_ref=gath.at[block], send_sem=send_sems.at[k],
                                                recv_sem=recv_sems.at[k], device_id=peers[k], device_id_type=MESH)

        scatter.start()
        gath[me] = v_ref[...]
        for k in range(N_DEV - 1):
            copy(k, me).start()
        scatter.middle()
        for k in range(N_DEV - 1):
            copy(k, _dev_index(*peers[k])).wait_recv()
        for k in range(N_DEV - 1):
            copy(k, me).wait_send()
        total = gath[0]
        for d in range(1, N_DEV):
            total = total + gath[d]
        vout_ref[...] = total
        scatter.finish()

    vmem = pl.BlockSpec(memory_space=pltpu.VMEM)
    return pl.pallas_call(
        body, in_specs=[pl.BlockSpec(memory_space=pl.ANY), vmem], out_specs=[vmem, vmem],
        out_shape=s_shape + [jax.ShapeDtypeStruct((rows, LANES), F32)],
        scratch_shapes=[pltpu.VMEM((N_DEV, rows, LANES), F32), pltpu.SemaphoreType.DMA((N_DEV - 1,)),
                        pltpu.SemaphoreType.DMA((N_DEV - 1,))] + s_scratch,
        compiler_params=pltpu.CompilerParams(vmem_limit_bytes=VMEM_LIMIT), name="final_exchange")(g8, v)


def _row_tile(r):
    for n in (8, 4, 2):
        if r % (n * SUBLANES) == 0:
            return r // n
    return r


def _adam_math(wv, gv, mv, vv):
    mn = ADAM_B1 * mv + (1.0 - ADAM_B1) * gv
    vn = ADAM_B2 * vv + (1.0 - ADAM_B2) * (gv * gv)
    m_hat = mn / (1.0 - ADAM_B1 ** ADAM_STEP)
    v_hat = vn / (1.0 - ADAM_B2 ** ADAM_STEP)
    return -ADAM_LR * (m_hat / (jnp.sqrt(v_hat) + ADAM_EPS) + ADAM_WD * wv), mn, vn


def _adamw(w, g, m, v, name):
    r, c_ = w.shape
    tr = _row_tile(r)

    def body(w_ref, g_ref, m_ref, v_ref, d_ref, mo_ref, vo_ref):
        d_ref[...], mo_ref[...], vo_ref[...] = _adam_math(w_ref[...], g_ref[...], m_ref[...], v_ref[...])

    spec = pl.BlockSpec((tr, c_), lambda i: (i, 0))
    return pl.pallas_call(
        body, grid=(r // tr,), in_specs=[spec] * 4, out_specs=[spec] * 3,
        out_shape=[jax.ShapeDtypeStruct((r, c_), F32)] * 3,
        compiler_params=_cparams("parallel"), name=name)(w, g, m, v)


FW_ROWS = 24
CW_ROWS = 32
R_FW = 0
R_FB = R_FW + N_DEV * FW_ROWS
R_CW = R_FB + 48
R_BQKV = R_CW + (CONV_WIDTH // LANES) * CW_ROWS
R_BCIN = R_BQKV + 8
R_GMIX = R_BCIN + 8
R_BOUT = R_GMIX + 8
R_GFFN = R_BOUT + 8
R_CB = R_GFFN + 8
R_CGAIN = R_CB + 8
R_CBIAS = R_CGAIN + 8
R_QKS = R_CBIAS + 8
SMALL_ROWS = R_QKS + 8


def _pack_small(raw):
    def rows(a, n):
        a = a.reshape(-1, LANES)
        return jnp.pad(a, ((0, n - a.shape[0]), (0, 0)))

    fw = jnp.pad(raw["dfw"].reshape(N_DEV, -1, LANES), ((0, 0), (0, FW_ROWS - 3 * FF_LANE_CHUNKS), (0, 0)))
    cw = jnp.pad(raw["dcw"].reshape(CONV_KERNEL, -1, LANES).transpose(1, 0, 2), ((0, 0), (0, CW_ROWS - CONV_KERNEL), (0, 0)))
    qks = jnp.concatenate([raw["dgq"], raw["dgk"], raw["dsink"], jnp.pad(raw["loss"], ((0, 0), (0, LANES - 1)))], axis=0)
    return jnp.concatenate([
        fw.reshape(-1, LANES), rows(raw["dfb"][:, 0, :FF_CHUNK], 48), cw.reshape(-1, LANES), rows(raw["dbqkv"], 8),
        rows(raw["dbcin"], 8), rows(raw["dg_mix"], 8), rows(raw["db_out"], 8), rows(raw["dg_ffn"], 8), rows(raw["dcb"], 8),
        rows(raw["dcgain"], 8), rows(raw["dcbias"], 8), rows(qks, 8)], axis=0)


def _adamw_small(gpack, w, m, v):
    n = len(SMALL)
    ix = {name: i for i, name in enumerate(SMALL)}

    def body(g_ref, *refs):
        w_refs, m_refs, v_refs, outs = refs[:n], refs[n:2 * n], refs[2 * n:3 * n], refs[3 * n:]
        d = _dev_index(*_position())

        def step(name, idx, gv):
            i = ix[name]
            delta, mn, vn = _adam_math(w_refs[i][idx], gv, m_refs[i][idx], v_refs[i][idx])
            for ref, val in zip(outs[4 * i:4 * i + 4], (gv, delta, mn, vn)):
                ref[idx] = val

        def whole(name, row, nrows):
            step(name, (slice(None), slice(None)), g_ref[row:row + nrows, :])

        whole("mix_norm_gain", R_GMIX, 8)
        whole("b_out", R_BOUT, 8)
        whole("ffn_norm_gain", R_GFFN, 8)
        whole("conv_dw_b", R_CB, 4)
        whole("conv_norm_gain", R_CGAIN, 4)
        whole("conv_norm_bias", R_CBIAS, 4)
        whole("ffn_dw_b", R_FB, 2 * D_FF // LANES)
        nq = QKV_COLS // LANES
        step("b_in", (slice(0, nq), slice(None)), g_ref[R_BQKV:R_BQKV + nq, :])
        step("b_in", (slice(nq, nq + CIN_COLS // LANES), slice(None)), g_ref[R_BCIN:R_BCIN + CIN_COLS // LANES, :])
        step("q_norm_gain", (slice(None), slice(None)), g_ref[R_QKS:R_QKS + 1, 0:HEAD_DIM])
        step("k_norm_gain", (slice(None), slice(None)), g_ref[R_QKS + 1:R_QKS + 2, 0:HEAD_DIM])
        step("attn_sinks", (slice(None), slice(None)), g_ref[R_QKS + 2:R_QKS + 3, 0:N_Q_HEADS])
        blk = g_ref[pl.ds(pl.multiple_of(R_CW + CW_ROWS * lax.shift_right_logical(d, 1), SUBLANES), CW_ROWS), :]
        blk = jnp.where((d & 1) == 1, pltpu.roll(blk, HEAD_DIM, 1), blk)
        step("conv_dw_w", (slice(None), slice(None)), blk[0:CONV_KERNEL, 0:CONV_WIDTH // N_DEV])
        blk = g_ref[pl.ds(pl.multiple_of(R_FW + FW_ROWS * d, SUBLANES), FW_ROWS), :]
        for k in range(3):
            for j in range(FF_LANE_CHUNKS):
                wd = min(LANES, FF_CHUNK - j * LANES)
                row = k * FF_LANE_CHUNKS + j
                step("ffn_dw_w", (slice(k, k + 1), slice(j * LANES, j * LANES + wd)), blk[row:row + 1, 0:wd])

    vmem = pl.BlockSpec(memory_space=pltpu.VMEM)
    args = [gpack] + [d[name] for d in (w, m, v) for name in SMALL]
    outs = pl.pallas_call(
        body, in_specs=[vmem] * len(args), out_specs=[vmem] * (4 * n),
        out_shape=[jax.ShapeDtypeStruct(w[name].shape, F32) for name in SMALL for _ in range(4)],
        compiler_params=pltpu.CompilerParams(vmem_limit_bytes=VMEM_LIMIT), name="adamw_small")(*args)
    return {name: outs[4 * i:4 * i + 4] for i, name in enumerate(SMALL)}


def _token_mixing(x, p, attn_shards, conv_shards):
    qkv, cin, h1 = _mix_proj(x, p["g_mix"], p["w_in_t"], p["b_qkv"], p["b_cin"])
    attn, *from_attn = _attn_fwd(qkv, p["gq2"], p["gk2"], p["sinks"], attn_shards)
    c3, c1, *from_conv = _conv_fwd(cin, p["cw8"], p["cb"], p["cgain"], p["cbias"], conv_shards)
    return (qkv, cin, h1, attn, c3, c1), from_attn, from_conv


def _rest_of_step(x, target, p, saved, scatter):
    s = x.shape[0]
    qkv, cin, h1, attn, c3, c1 = saved
    cw8, w_out, w_up, w_down = p["cw8"], p["w_out"], p["w_up"], p["w_down"]
    x2, h2 = _out_proj(x, attn, c3, w_out, w_out, p["b_out"], p["g_ffn"])
    fw, fb = p["fw"], p["fb"]
    up0, act, dy, loss = _ffn_fwd(h2, x2, target, w_up, fw, fb, w_down)
    dup0, dh2, dfw, dfb = _ffn_bwd(dy, up0, w_up, fw, fb, w_down)
    dx2, dmixed, dg_ffn, db_out = _ffn_norm_bwd(dh2, dy, x2, p["g_ffn"], w_out)
    dw_up = _tn_matmul(dup0.reshape(N_DEV, s, FF_CHUNK), h2[None], "dw_up")
    dw_down = _tn_matmul(act, dy[None], "dw_down").reshape(N_DEV, -1, D_MODEL)
    dw_out = jnp.concatenate([_tn_matmul(attn[None], dx2[None], "dw_out_attn")[0],
                              _tn_matmul(c3[None], dx2[None], "dw_out_conv")[0]], axis=0).reshape(N_DEV, -1, D_MODEL)
    dcin, dcw, dcb, dcgain, dcbias, dbcin, *g_up = _conv_bwd(dmixed, c1, cin, cw8, p["cgain"], p["cbias"], [dw_up] if scatter else [])
    dqkv, dgq, dgk, dsink, dbqkv, *g_down_out = _attn_bwd(qkv, dmixed, p["gq2"], p["gk2"], p["sinks"],
                                                          [dw_down, dw_out] if scatter else [])
    dw_in = jnp.concatenate([_tn_matmul(dqkv[None], h1[None], "dw_qkv")[0], _tn_matmul(dcin[None], h1[None], "dw_cin")[0]], axis=0)
    dw_in = dw_in.reshape(N_DEV, -1, D_MODEL)
    grad_x, dg_mix = _in_bwd(dqkv, dcin, p["w_in_t"], x, dx2, p["g_mix"])
    if scatter:
        big = {"w_up": g_up[0], "w_down": g_down_out[0], "w_in": dw_in, "w_out": g_down_out[1]}
    else:
        big = {"w_up": dw_up, "w_down": dw_down, "w_in": dw_in, "w_out": dw_out}
    small = dict(dg_mix=dg_mix, dbqkv=dbqkv, dbcin=dbcin, dgq=dgq, dgk=dgk, dsink=dsink, dcw=dcw, dcb=dcb, dcgain=dcgain,
                 dcbias=dcbias, db_out=db_out, dg_ffn=dg_ffn, dfw=dfw, dfb=dfb, loss=loss)
    return loss, grad_x, big, small


BIG = ("w_in", "w_out", "w_up", "w_down")
SMALL = ("mix_norm_gain", "b_in", "q_norm_gain", "k_norm_gain", "attn_sinks", "conv_dw_w", "conv_dw_b",
         "conv_norm_gain", "conv_norm_bias", "b_out", "ffn_norm_gain", "ffn_dw_w", "ffn_dw_b")
ORDER = ("mix_norm_gain", "w_in", "b_in", "q_norm_gain", "k_norm_gain", "attn_sinks", "conv_dw_w", "conv_dw_b",
         "conv_norm_gain", "conv_norm_bias", "w_out", "b_out", "ffn_norm_gain", "w_up", "ffn_dw_w", "ffn_dw_b", "w_down")


def kernel(x, mix_norm_gain, w_in, b_in, q_norm_gain, k_norm_gain, attn_sinks, conv_dw_w, conv_dw_b, conv_norm_gain, conv_norm_bias, w_out, b_out, ffn_norm_gain, w_up, ffn_dw_w, ffn_dw_b, w_down, loss_target, m_mix_norm_gain, m_w_in, m_b_in, m_q_norm_gain, m_k_norm_gain, m_attn_sinks, m_conv_dw_w, m_conv_dw_b, m_conv_norm_gain, m_conv_norm_bias, m_w_out, m_b_out, m_ffn_norm_gain, m_w_up, m_ffn_dw_w, m_ffn_dw_b, m_w_down, v_mix_norm_gain, v_w_in, v_b_in, v_q_norm_gain, v_k_norm_gain, v_attn_sinks, v_conv_dw_w, v_conv_dw_b, v_conv_norm_gain, v_conv_norm_bias, v_w_out, v_b_out, v_ffn_norm_gain, v_w_up, v_ffn_dw_w, v_ffn_dw_b, v_w_down):
    w = dict(mix_norm_gain=mix_norm_gain, w_in=w_in, b_in=b_in, q_norm_gain=q_norm_gain, k_norm_gain=k_norm_gain,
             attn_sinks=attn_sinks, conv_dw_w=conv_dw_w, conv_dw_b=conv_dw_b, conv_norm_gain=conv_norm_gain,
             conv_norm_bias=conv_norm_bias, w_out=w_out, b_out=b_out, ffn_norm_gain=ffn_norm_gain, w_up=w_up,
             ffn_dw_w=ffn_dw_w, ffn_dw_b=ffn_dw_b, w_down=w_down)
    m = dict(mix_norm_gain=m_mix_norm_gain, w_in=m_w_in, b_in=m_b_in, q_norm_gain=m_q_norm_gain, k_norm_gain=m_k_norm_gain,
             attn_sinks=m_attn_sinks, conv_dw_w=m_conv_dw_w, conv_dw_b=m_conv_dw_b, conv_norm_gain=m_conv_norm_gain,
             conv_norm_bias=m_conv_norm_bias, w_out=m_w_out, b_out=m_b_out, ffn_norm_gain=m_ffn_norm_gain, w_up=m_w_up,
             ffn_dw_w=m_ffn_dw_w, ffn_dw_b=m_ffn_dw_b, w_down=m_w_down)
    v = dict(mix_norm_gain=v_mix_norm_gain, w_in=v_w_in, b_in=v_b_in, q_norm_gain=v_q_norm_gain, k_norm_gain=v_k_norm_gain,
             attn_sinks=v_attn_sinks, conv_dw_w=v_conv_dw_w, conv_dw_b=v_conv_dw_b, conv_norm_gain=v_conv_norm_gain,
             conv_norm_bias=v_conv_norm_bias, w_out=v_w_out, b_out=v_b_out, ffn_norm_gain=v_ffn_norm_gain, w_up=v_w_up,
             ffn_dw_w=v_ffn_dw_w, ffn_dw_b=v_ffn_dw_b, w_down=v_w_down)
    s = x.shape[1]

    wi8, cw8, fw8 = _allgather([w_in.T, conv_dw_w, ffn_dw_w], [BF16, F32, F32])
    lane_pad = ((0, 0), (0, 0), (0, FF_PADDED - FF_CHUNK))
    p = {
        "g_mix": mix_norm_gain.reshape(1, -1), "w_in_t": wi8.reshape(QKV_COLS + CIN_COLS, D_MODEL),
        "b_qkv": b_in[:QKV_COLS].reshape(1, -1), "b_cin": b_in[QKV_COLS:].reshape(1, -1),
        "gq2": jnp.tile(q_norm_gain, 2).reshape(1, -1), "gk2": jnp.tile(k_norm_gain, 2).reshape(1, -1), "sinks": attn_sinks,
        "cw8": jnp.repeat(cw8.transpose(1, 0, 2).reshape(CONV_KERNEL, CONV_WIDTH), SUBLANES, axis=0),
        "cb": conv_dw_b.reshape(1, -1), "cgain": conv_norm_gain.reshape(1, -1), "cbias": conv_norm_bias.reshape(1, -1),
        "b_out": b_out.reshape(1, -1), "g_ffn": ffn_norm_gain.reshape(1, -1),
        "fw": jnp.pad(fw8, lane_pad), "fb": jnp.pad(ffn_dw_b.reshape(N_DEV, 1, FF_CHUNK), lane_pad),
    }

    saved, (wu8,), (wo8, wd8) = _token_mixing(x[0], p, [w_up.T], [w_out, w_down])
    p.update(w_out=wo8.reshape(D_MODEL, D_MODEL), w_up=wu8, w_down=wd8.reshape(N_FF_PAIRS, FF_CHUNK, D_MODEL))
    loss, grad_x, big, small = _rest_of_step(x[0], loss_target[0], p, saved, True)

    g = dict(big)
    g["w_in"], gpack = _final_exchange(big["w_in"], _pack_small(small))

    delta, new_m, new_v = {}, {}, {}
    for n in BIG:
        if n in ("w_in", "w_up"):
            outs = _adamw(w[n].T, g[n], m[n].T, v[n].T, "adamw_" + n)
            g[n], delta[n], new_m[n], new_v[n] = g[n].T, *[o.T for o in outs]
        else:
            delta[n], new_m[n], new_v[n] = _adamw(w[n], g[n], m[n], v[n], "adamw_" + n)

    def view(a):
        return a if a.ndim == 2 else (a.reshape(-1, LANES) if a.size % LANES == 0 else a.reshape(1, -1))

    small_out = _adamw_small(gpack, *[{n: view(d[n]) for n in SMALL} for d in (w, m, v)])
    for n in SMALL:
        g[n], delta[n], new_m[n], new_v[n] = [a.reshape(w[n].shape) for a in small_out[n]]

    total = gpack[R_QKS + 3, 0]
    return (total, grad_x.reshape(1, s, D_MODEL), *[g[n] for n in ORDER], *[delta[n] for n in ORDER],
            *[new_m[n] for n in ORDER], *[new_v[n] for n in ORDER])
```

```python
import functools
import math

import jax
import jax.numpy as jnp
from jax import lax
from jax.experimental import pallas as pl
from jax.experimental.pallas import tpu as pltpu

F32 = jnp.float32
BF16 = jnp.bfloat16

D_MODEL = 1024
HEAD_DIM = 64
N_Q_HEADS = 8
N_KV_HEADS = 2
Q_COLS = 512
KV_COLS = 128
QKV_COLS = Q_COLS + 2 * KV_COLS
CONV_WIDTH = 512
CIN_COLS = 2 * CONV_WIDTH
CONV_KERNEL = 31
CONV_HALO = 32
D_FF = 2816
N_DEV = 8
FF_CHUNK = 2 * D_FF // N_DEV
N_FF_PAIRS = N_DEV // 2
ATT_BLOCK = 128
EPS = 1e-6
NEG_INF = -1e30
SLOPES = [float(2.0 ** (-8.0 * (h + 1.0) / N_Q_HEADS)) for h in range(N_Q_HEADS)]

ADAM_LR = 0.001
ADAM_B1 = 0.9
ADAM_B2 = 0.999
ADAM_EPS = 1e-08
ADAM_WD = 0.01
ADAM_STEP = 10

LANES = 128
SUBLANES = 8
VMEM_LIMIT = 56 * 1024 * 1024
MESH = pl.DeviceIdType.MESH


def _cparams(*sem, **kw):
    return pltpu.CompilerParams(dimension_semantics=sem or None, vmem_limit_bytes=VMEM_LIMIT, **kw)


def _resident(shape):
    nd = len(shape)
    return pl.BlockSpec(shape, lambda *_: (0,) * nd, pipeline_mode=pl.Buffered(1))


def _dot(a, b):
    return jnp.dot(a, b, preferred_element_type=F32)


def _dot_nt(a, b):
    return lax.dot_general(a, b, (((1,), (1,)), ((), ())), preferred_element_type=F32)


def _dot_tn(a, b):
    return lax.dot_general(a, b, (((0,), (0,)), ((), ())), preferred_element_type=F32)


def _sigmoid(x):
    return 1.0 / (1.0 + jnp.exp(-x))


def _lo_mask(shape):
    return lax.broadcasted_iota(jnp.int32, shape, len(shape) - 1) % LANES < HEAD_DIM


def _half_sums(t, lo):
    s_lo = jnp.sum(jnp.where(lo, t, 0.0), axis=-1, keepdims=True)
    s_hi = jnp.sum(jnp.where(lo, 0.0, t), axis=-1, keepdims=True)
    return jnp.where(lo, s_lo, s_hi)


def _head_norm(t, lo):
    r = lax.rsqrt(_half_sums(t * t, lo) * (1.0 / HEAD_DIM) + EPS)
    return t * r, r


def _head_norm_bwd(dn, n, r, lo):
    return r * (dn - n * (_half_sums(dn * n, lo) * (1.0 / HEAD_DIM)))


def _tile(s):
    return min(512, s)


TN_TOKENS = 2048
FF_COLS = ((0, 256), (256, 512), (512, 704))


def _position():
    return lax.axis_index("x"), lax.axis_index("y"), lax.axis_index("c")


def _dev_index(px, py, pc):
    return 4 * px + 2 * py + pc


def _flip(v, bit):
    return 1 - v if bit else v


OTHER_CHIPS = ((1, 0), (0, 1), (1, 1))
N_GATHER_COPIES = 1 + 2 * len(OTHER_CHIPS)


class _Gather:
    def __init__(self, shard_ref, out_ref, cast_buf, send_sems, recv_sems, local_sem):
        self.shard, self.out, self.buf = shard_ref, out_ref, cast_buf
        self.send_sems, self.recv_sems, self.local_sem = send_sems, recv_sems, local_sem
        x, y, c = _position()
        self.c = c
        self.me, self.sibling = (x, y, c), (x, y, 1 - c)
        self.chips = [(_flip(x, fx), _flip(y, fy)) for fx, fy in OTHER_CHIPS]

    def _copy(self, k, block, to, from_buf=False):
        rows = self.out.at[_dev_index(*block)]
        return pltpu.make_async_remote_copy(src_ref=self.buf if from_buf else rows, dst_ref=rows,
                                            send_sem=self.send_sems.at[k], recv_sem=self.recv_sems.at[k],
                                            device_id=to, device_id_type=MESH)

    def _local(self):
        return pltpu.make_async_copy(self.buf, self.out.at[_dev_index(*self.me)], self.local_sem)

    def start(self):
        self.buf[...] = self.shard[...].astype(self.buf.dtype)
        self._local().start()
        for j, chip in enumerate(self.chips):
            self._copy(1 + j, self.me, (*chip, self.c), from_buf=True).start()
        self._copy(0, self.me, self.sibling, from_buf=True).start()

    def forward(self):
        for j, chip in enumerate(self.chips):
            self._copy(1 + j, (*chip, self.c), self.me).wait_recv()
            self._copy(1 + len(self.chips) + j, (*chip, self.c), self.sibling).start()

    def finish(self):
        self._copy(0, self.sibling, self.me).wait_recv()
        for j, chip in enumerate(self.chips):
            self._copy(1 + len(self.chips) + j, (*chip, 1 - self.c), self.me).wait_recv()
        for k in range(N_GATHER_COPIES):
            self._copy(k, self.me, self.sibling).wait_send()
        self._local().wait()


def _gather_specs(shards):
    whole = [pl.BlockSpec(w.shape, lambda *_, nd=w.ndim: (0,) * nd, pipeline_mode=pl.Buffered(1)) for w in shards]
    outs = [pl.BlockSpec(memory_space=pl.ANY) for _ in shards]
    shapes = [jax.ShapeDtypeStruct((N_DEV,) + w.shape, BF16) for w in shards]
    scratch = []
    for w in shards:
        scratch += [pltpu.VMEM(w.shape, BF16), pltpu.SemaphoreType.DMA((N_GATHER_COPIES,)),
                    pltpu.SemaphoreType.DMA((N_GATHER_COPIES,)), pltpu.SemaphoreType.DMA(())]
    return whole, outs, shapes, scratch


def _run_gathers(gathers, step, n_steps):
    @pl.when(step == 0)
    def _():
        for g in gathers:
            g.start()

    @pl.when(step == 3 * n_steps // 4)
    def _():
        for g in gathers:
            g.forward()

    @pl.when(step == n_steps - 1)
    def _():
        for g in gathers:
            g.finish()


class _ReduceScatter:
    def __init__(self, g_ref, out_ref, stage, load_sems, send_a, recv_a, send_b, recv_b, sa_send, sa_recv, sb_send, sb_recv):
        self.g, self.out, self.stage, self.load_sems = g_ref, out_ref, stage, load_sems
        self.send_a, self.recv_a, self.send_b, self.recv_b = send_a, recv_a, send_b, recv_b
        self.sems = (sa_send, sa_recv, sb_send, sb_recv)
        x, y, c = _position()
        self.c, self.sibling = c, (x, y, 1 - c)
        self.chips = [(x, y)] + [(_flip(x, fx), _flip(y, fy)) for fx, fy in OTHER_CHIPS]

    def _copy_a(self, j):
        return pltpu.make_async_remote_copy(src_ref=self.send_a.at[j], dst_ref=self.recv_a.at[j], send_sem=self.sems[0].at[j],
                                            recv_sem=self.sems[1].at[j], device_id=self.sibling, device_id_type=MESH)

    def _copy_b(self, j):
        return pltpu.make_async_remote_copy(src_ref=self.send_b.at[j], dst_ref=self.recv_b.at[j], send_sem=self.sems[2].at[j],
                                            recv_sem=self.sems[3].at[j], device_id=(*self.chips[1 + j], self.c),
                                            device_id_type=MESH)

    def _load(self, j, core):
        return pltpu.make_async_copy(self.g.at[_dev_index(*self.chips[j], core)], self.stage.at[j % 2], self.load_sems.at[j % 2])

    def start(self):
        self._load(0, 1 - self.c).start()
        for j in range(len(self.chips)):
            self._load(j, 1 - self.c).wait()
            if j + 1 < len(self.chips):
                self._load(j + 1, 1 - self.c).start()
            self.send_a[j] = self.stage[j % 2].astype(BF16)
            self._copy_a(j).start()

    def middle(self):
        self._load(0, self.c).start()
        for j in range(len(self.chips)):
            self._load(j, self.c).wait()
            if j + 1 < len(self.chips):
                self._load(j + 1, self.c).start()
            self._copy_a(j).wait_recv()
            part = self.stage[j % 2] + self.recv_a[j].astype(F32)
            if j == 0:
                self.out[...] = part
            else:
                self.send_b[j - 1] = part.astype(BF16)
                self._copy_b(j - 1).start()

    def finish(self):
        for j in range(len(OTHER_CHIPS)):
            self._copy_b(j).wait_recv()
            self.out[...] += self.recv_b[j].astype(F32)
        for j in range(len(self.chips)):
            self._copy_a(j).wait_send()
        for j in range(len(OTHER_CHIPS)):
            self._copy_b(j).wait_send()


N_SCATTER_SCRATCH = 10


def _scatter_specs(g8s):
    na, nb = 1 + len(OTHER_CHIPS), len(OTHER_CHIPS)
    ins = [pl.BlockSpec(memory_space=pl.ANY) for _ in g8s]
    outs = [pl.BlockSpec(g.shape[1:], lambda *_: (0, 0)) for g in g8s]
    shapes = [jax.ShapeDtypeStruct(g.shape[1:], F32) for g in g8s]
    scratch = []
    for g in g8s:
        blk = g.shape[1:]
        scratch += [pltpu.VMEM((2,) + blk, F32), pltpu.SemaphoreType.DMA((2,)), pltpu.VMEM((na,) + blk, BF16), pltpu.VMEM((na,) + blk, BF16),
                    pltpu.VMEM((nb,) + blk, BF16), pltpu.VMEM((nb,) + blk, BF16),
                    pltpu.SemaphoreType.DMA((na,)), pltpu.SemaphoreType.DMA((na,)),
                    pltpu.SemaphoreType.DMA((nb,)), pltpu.SemaphoreType.DMA((nb,))]
    return ins, outs, shapes, scratch


def _run_scatters(scatters, step, n_steps):
    @pl.when(step == 0)
    def _():
        for r in scatters:
            r.start()

    @pl.when(step == min(max(1, n_steps // 4), n_steps - 1))
    def _():
        for r in scatters:
            r.middle()

    @pl.when(step == n_steps - 1)
    def _():
        for r in scatters:
            r.finish()


def _mix_proj(x, g_mix, w_in_t, b_qkv, b_cin):
    s = x.shape[0]
    tm = _tile(s)

    def body(x_ref, g_ref, w_ref, bq_ref, bc_ref, qkv_ref, cin_ref, h1_ref):
        xv = x_ref[...]
        r = lax.rsqrt(jnp.mean(xv * xv, axis=-1, keepdims=True) + EPS)
        h = (xv * r * g_ref[...]).astype(BF16)
        h1_ref[...] = h
        qkv_ref[...] = _dot_nt(h, w_ref[0:QKV_COLS, :]) + bq_ref[...]
        cin_ref[...] = _dot_nt(h, w_ref[QKV_COLS:, :]) + bc_ref[...]

    return pl.pallas_call(
        body, grid=(s // tm,),
        in_specs=[pl.BlockSpec((tm, D_MODEL), lambda i: (i, 0)), _resident((1, D_MODEL)),
                  _resident((QKV_COLS + CIN_COLS, D_MODEL)), _resident((1, QKV_COLS)), _resident((1, CIN_COLS))],
        out_specs=[pl.BlockSpec((tm, QKV_COLS), lambda i: (i, 0)), pl.BlockSpec((tm, CIN_COLS), lambda i: (i, 0)),
                   pl.BlockSpec((tm, D_MODEL), lambda i: (i, 0))],
        out_shape=[jax.ShapeDtypeStruct((s, QKV_COLS), F32), jax.ShapeDtypeStruct((s, CIN_COLS), F32),
                   jax.ShapeDtypeStruct((s, D_MODEL), BF16)],
        compiler_params=_cparams("parallel"), name="mix_proj")(x, g_mix, w_in_t, b_qkv, b_cin)


def _kv_variants(kv_all, gk2, lo):
    k_all = kv_all[:, :LANES]
    v_all = kv_all[:, LANES:]
    kn_pre, rk = _head_norm(k_all, lo)
    kn = kn_pre * gk2
    kr = pltpu.roll(kn, HEAD_DIM, 1)
    vr = pltpu.roll(v_all, HEAD_DIM, 1)
    zero = jnp.zeros_like(kn)
    k_lo = [jnp.where(lo, kn, zero).astype(BF16), jnp.where(lo, kr, zero).astype(BF16)]
    k_hi = [jnp.where(lo, zero, kr).astype(BF16), jnp.where(lo, zero, kn).astype(BF16)]
    v_lo = [jnp.where(lo, v_all, zero).astype(BF16), jnp.where(lo, vr, zero).astype(BF16)]
    v_hi = [jnp.where(lo, zero, vr).astype(BF16), jnp.where(lo, zero, v_all).astype(BF16)]
    return k_lo, k_hi, v_lo, v_hi, kn_pre, rk


def _att_consts(first_tile, b):
    rows = 2 * ATT_BLOCK
    qi = lax.broadcasted_iota(jnp.int32, (rows, 2 * ATT_BLOCK), 0) % ATT_BLOCK
    kj = lax.broadcasted_iota(jnp.int32, (rows, 2 * ATT_BLOCK), 1)
    rel = qi + ATT_BLOCK - kj
    valid = (rel >= 0) & (rel < ATT_BLOCK)
    if b == 0:
        valid = valid & ((kj >= ATT_BLOCK) | jnp.logical_not(first_tile))
    return rel.astype(F32), valid


def _row_const(va, vb):
    top = lax.broadcasted_iota(jnp.int32, (2 * ATT_BLOCK, 1), 0) < ATT_BLOCK
    return jnp.where(top, va, vb)


def _probs(q2, k_op, rel, valid, slope, sink):
    sc = _dot_nt(q2, k_op) * (1.0 / math.sqrt(HEAD_DIM)) - slope * rel
    sc = jnp.where(valid, sc, NEG_INF)
    m = jnp.maximum(jnp.max(sc, axis=-1, keepdims=True), sink)
    p = jnp.exp(sc - m)
    e_sink = jnp.exp(sink - m)
    inv = 1.0 / (jnp.sum(p, axis=-1, keepdims=True) + e_sink)
    return p * inv, e_sink * inv


def _attn_fwd(qkv, gq2, gk2, sinks, shards):
    s = qkv.shape[0]
    tq = _tile(s)
    nb = tq // ATT_BLOCK
    ng = len(shards)
    g_in, g_out, g_shape, g_scratch = _gather_specs(shards)

    def body(q_ref, kv_ref, kvp_ref, gq_ref, gk_ref, sink_ref, *rest):
        out_ref = rest[ng]
        i = pl.program_id(0)
        _run_gathers([_Gather(rest[a], rest[ng + 1 + a], *rest[2 * ng + 1 + 4 * a:2 * ng + 5 + 4 * a]) for a in range(ng)],
                     i, s // tq)
        lo = _lo_mask((1, LANES))
        kv_all = jnp.concatenate([kvp_ref[...], kv_ref[...]], axis=0)
        k_lo, k_hi, v_lo, v_hi, _, _ = _kv_variants(kv_all, gk_ref[...], lo)
        for b in range(nb):
            rel, valid = _att_consts(i == 0, b)
            rows = slice(b * ATT_BLOCK, (b + 1) * ATT_BLOCK)
            keys = slice(b * ATT_BLOCK, (b + 2) * ATT_BLOCK)
            for kvh in range(N_KV_HEADS):
                pairs = (2 * kvh, 2 * kvh + 1)
                q2 = jnp.concatenate([q_ref[rows, p * LANES:(p + 1) * LANES] for p in pairs], axis=0)
                qn, _ = _head_norm(q2, lo)
                q2 = (qn * gq_ref[...]).astype(BF16)
                out = None
                for odd, (k_op, v_op) in enumerate(((k_lo[kvh][keys], v_lo[kvh][keys]), (k_hi[kvh][keys], v_hi[kvh][keys]))):
                    ha, hb = 2 * pairs[0] + odd, 2 * pairs[1] + odd
                    p, _ = _probs(q2, k_op, rel, valid, _row_const(SLOPES[ha], SLOPES[hb]),
                                  _row_const(sink_ref[ha], sink_ref[hb]))
                    o = _dot(p.astype(BF16), v_op)
                    out = o if out is None else out + o
                for n, p in enumerate(pairs):
                    out_ref[rows, p * LANES:(p + 1) * LANES] = out[n * ATT_BLOCK:(n + 1) * ATT_BLOCK].astype(BF16)

    return pl.pallas_call(
        body, grid=(s // tq,),
        in_specs=[pl.BlockSpec((tq, Q_COLS), lambda i: (i, 0)),
                  pl.BlockSpec((tq, 2 * KV_COLS), lambda i: (i, 2)),
                  pl.BlockSpec((ATT_BLOCK, 2 * KV_COLS), lambda i: (jnp.maximum(i * nb - 1, 0), 2)),
                  _resident((1, LANES)), _resident((1, LANES)),
                  pl.BlockSpec(memory_space=pltpu.SMEM)] + g_in,
        out_specs=[pl.BlockSpec((tq, Q_COLS), lambda i: (i, 0))] + g_out,
        out_shape=[jax.ShapeDtypeStruct((s, Q_COLS), BF16)] + g_shape,
        scratch_shapes=g_scratch,
        compiler_params=_cparams("arbitrary"), name="attn_fwd")(qkv, qkv, qkv, gq2, gk2, sinks, *shards)


def _group_stats(c1, lo):
    mu = _half_sums(c1, lo) * (1.0 / HEAD_DIM)
    d = c1 - mu
    rstd = lax.rsqrt(_half_sums(d * d, lo) * (1.0 / HEAD_DIM) + EPS)
    return d * rstd, rstd


def _rows(ref, first_row, n):
    return ref[pl.ds(first_row, n, stride=1), :].reshape(n // SUBLANES, SUBLANES, LANES)


def _conv_fwd(cin, cw8, cb, gain, bias, shards):
    s = cin.shape[0]
    tm = _tile(s)
    rc = 64
    nchunk = CONV_WIDTH // LANES
    lead = CONV_HALO - (CONV_KERNEL - 1)
    ng = len(shards)
    g_in, g_out, g_shape, g_scratch = _gather_specs(shards)

    def body(cin_ref, cw_ref, cb_ref, gain_ref, bias_ref, *rest):
        c3_ref, c1_ref, ext_ref = rest[ng], rest[ng + 1], rest[2 * ng + 2]
        _run_gathers([_Gather(rest[a], rest[ng + 2 + a], *rest[2 * ng + 3 + 4 * a:2 * ng + 7 + 4 * a]) for a in range(ng)],
                     pl.program_id(0), s // tm)

        @pl.when(pl.program_id(0) == 0)
        def _():
            ext_ref[:, 0:CONV_HALO, :] = jnp.zeros((nchunk, CONV_HALO, LANES), F32)

        lo = _lo_mask((1, LANES))
        for cc in range(nchunk):
            cols = slice(cc * LANES, (cc + 1) * LANES)
            gcols = slice(CONV_WIDTH + cc * LANES, CONV_WIDTH + (cc + 1) * LANES)
            ext_ref[cc, CONV_HALO:CONV_HALO + tm, :] = cin_ref[:, cols] * _sigmoid(cin_ref[:, gcols])
            ext = ext_ref.at[cc]
            for r in range(tm // rc):
                rows = slice(r * rc, (r + 1) * rc)
                acc = jnp.zeros((rc // SUBLANES, SUBLANES, LANES), F32)
                for k in range(CONV_KERNEL):
                    acc = acc + cw_ref[k * SUBLANES:(k + 1) * SUBLANES, cols][None] * _rows(ext, r * rc + lead + k, rc)
                c1 = acc.reshape(rc, LANES) + cb_ref[:, cols]
                c1_ref[cc, rows, :] = c1
                nrm, _ = _group_stats(c1, lo)
                c2 = nrm * gain_ref[:, cols] + bias_ref[:, cols]
                c3_ref[rows, cols] = (c2 * _sigmoid(c2)).astype(BF16)
        ext_ref[:, 0:CONV_HALO, :] = ext_ref[:, tm:tm + CONV_HALO, :]

    return pl.pallas_call(
        body, grid=(s // tm,),
        in_specs=[pl.BlockSpec((tm, CIN_COLS), lambda i: (i, 0)), _resident((CONV_KERNEL * SUBLANES, CONV_WIDTH)),
                  _resident((1, CONV_WIDTH)), _resident((1, CONV_WIDTH)), _resident((1, CONV_WIDTH))] + g_in,
        out_specs=[pl.BlockSpec((tm, CONV_WIDTH), lambda i: (i, 0)), pl.BlockSpec((nchunk, tm, LANES), lambda i: (0, i, 0))] + g_out,
        out_shape=[jax.ShapeDtypeStruct((s, CONV_WIDTH), BF16), jax.ShapeDtypeStruct((nchunk, s, LANES), F32)] + g_shape,
        scratch_shapes=[pltpu.VMEM((nchunk, tm + CONV_HALO, LANES), F32)] + g_scratch,
        compiler_params=_cparams("arbitrary"), name="conv_fwd")(cin, cw8, cb, gain, bias, *shards)


def _out_proj(x, attn, c3, wo_a, wo_c, b_out, g_ffn):
    s = x.shape[0]
    tm = _tile(s)

    def body(x_ref, a_ref, c_ref, wa_ref, wc_ref, b_ref, g_ref, x2_ref, h2_ref):
        x2 = x_ref[...] + _dot(a_ref[...], wa_ref[...]) + _dot(c_ref[...], wc_ref[...]) + b_ref[...]
        x2_ref[...] = x2
        r = lax.rsqrt(jnp.mean(x2 * x2, axis=-1, keepdims=True) + EPS)
        h2_ref[...] = (x2 * r * g_ref[...]).astype(BF16)

    return pl.pallas_call(
        body, grid=(s // tm,),
        in_specs=[pl.BlockSpec((tm, D_MODEL), lambda i: (i, 0)), pl.BlockSpec((tm, Q_COLS), lambda i: (i, 0)),
                  pl.BlockSpec((tm, CONV_WIDTH), lambda i: (i, 0)),
                  pl.BlockSpec((Q_COLS, D_MODEL), lambda i: (0, 0), pipeline_mode=pl.Buffered(1)),
                  pl.BlockSpec((CONV_WIDTH, D_MODEL), lambda i: (1, 0), pipeline_mode=pl.Buffered(1)),
                  _resident((1, D_MODEL)), _resident((1, D_MODEL))],
        out_specs=[pl.BlockSpec((tm, D_MODEL), lambda i: (i, 0)), pl.BlockSpec((tm, D_MODEL), lambda i: (i, 0))],
        out_shape=[jax.ShapeDtypeStruct((s, D_MODEL), F32), jax.ShapeDtypeStruct((s, D_MODEL), BF16)],
        compiler_params=_cparams("parallel"), name="out_proj")(x, attn, c3, wo_a, wo_c, b_out, g_ffn)


FF_LANE_CHUNKS = -(-FF_CHUNK // LANES)
FF_PADDED = FF_LANE_CHUNKS * LANES


def _tap(ref, first_row, n):
    return ref[pl.ds(first_row, n, stride=1), :]


def _ffn_fwd(h2, x2, target, w_up, fw, fb, w_down):
    s = h2.shape[0]
    tm = _tile(s)
    hal = SUBLANES
    rc = min(128, tm)

    def body(h_ref, x2_ref, t_ref, wu_ref, fw_ref, fb_ref, wd_ref, up0_ref, gu_ref, act_ref, dy_ref, loss_ref,
             ext_ref, carry_ref, act_buf, y_ref):
        i, ci = pl.program_id(0), pl.program_id(1)

        @pl.when((i == 0) & (ci == 0))
        def _():
            carry_ref[...] = jnp.zeros(carry_ref.shape, F32)
            ext_ref[...] = jnp.zeros(ext_ref.shape, F32)
            act_buf[...] = jnp.zeros(act_buf.shape, BF16)
            loss_ref[...] = jnp.zeros((1, 1), F32)

        @pl.when(ci == 0)
        def _():
            y_ref[...] = x2_ref[...]

        ws = (fw_ref[ci], fw_ref[ci + N_FF_PAIRS])
        bs = (fb_ref[ci], fb_ref[ci + N_FF_PAIRS])
        half_rows = (slice(0, tm // 2), slice(tm // 2, tm))
        n_grp = len(FF_COLS)

        def up_slices(grp):
            lo_c, hi_c = FF_COLS[grp]
            chunks = range(lo_c // LANES, -(-hi_c // LANES))

            def make(half, n, rows):
                def run():
                    c = ci + half * N_FF_PAIRS
                    u0 = _dot_nt(h_ref[rows, :], wu_ref[c, lo_c:hi_c, :])
                    up0_ref[half, 0, rows, lo_c:hi_c] = u0.astype(BF16)
                    if hi_c == FF_CHUNK:
                        up0_ref[half, 0, rows, FF_CHUNK:] = jnp.zeros((u0.shape[0], FF_PADDED - FF_CHUNK), BF16)
                    for j in chunks:
                        w = min(LANES, hi_c - j * LANES)
                        if n == 0:
                            ext_ref[half, j, 0:hal, 0:w] = carry_ref[c, :, j * LANES:j * LANES + w]
                        ext_ref[half, j, hal + rows.start:hal + rows.stop, 0:w] = u0[:, j * LANES - lo_c:j * LANES - lo_c + w]
                    if n == len(half_rows) - 1:
                        carry_ref[c, :, lo_c:hi_c] = u0[u0.shape[0] - hal:, :]
                return run
            return [make(half, n, rows) for half in range(2) for n, rows in enumerate(half_rows)]

        def down_slices(grp):
            lo_c, hi_c = FF_COLS[grp]

            def make(rows):
                def run():
                    y_ref[rows, :] += _dot(act_buf[rows, lo_c:hi_c], wd_ref[ci, lo_c:hi_c, :])
                return run
            return [make(rows) for rows in half_rows]

        def vector_blocks(grp):
            lo_c, hi_c = FF_COLS[grp]
            blocks = []
            for j in range(lo_c // LANES, -(-hi_c // LANES)):
                lanes = slice(j * LANES, (j + 1) * LANES)

                def gate(r, lanes=lanes, j=j):
                    base = r * rc
                    ups = []
                    for half in range(2):
                        e, w = ext_ref.at[half, j], ws[half]
                        ups.append(w[0:1, lanes] * _tap(e, base + hal - 2, rc) + w[1:2, lanes] * _tap(e, base + hal - 1, rc)
                                   + w[2:3, lanes] * _tap(e, base + hal, rc) + bs[half][:, lanes])
                    g, u = ups
                    gu_ref[0, 0, base:base + rc, lanes] = g.astype(BF16)
                    gu_ref[1, 0, base:base + rc, lanes] = u.astype(BF16)
                    act_buf[base:base + rc, lanes] = (g * _sigmoid(g) * u).astype(BF16)

                blocks += [functools.partial(gate, r) for r in range(tm // rc)]

            def finish():
                act_ref[0, :, lo_c:hi_c] = act_buf[:, lo_c:hi_c]
            blocks.append(finish)
            return blocks

        for run in up_slices(0):
            run()
        for grp in range(n_grp):
            matmuls = (up_slices(grp + 1) if grp + 1 < n_grp else []) + (down_slices(grp - 1) if grp > 0 else [])
            blocks = vector_blocks(grp)
            every = max(1, len(blocks) // (len(matmuls) + 1))
            for n, run in enumerate(blocks):
                run()
                if n % every == every - 1 and matmuls:
                    matmuls.pop(0)()
            for run in matmuls:
                run()
        for run in down_slices(n_grp - 1):
            run()

        @pl.when(ci == N_FF_PAIRS - 1)
        def _():
            e = y_ref[...] - t_ref[...]
            dy_ref[...] = e * (1.0 / D_MODEL)
            loss_ref[...] += (0.5 / D_MODEL) * jnp.sum(e * e).reshape(1, 1)

    tok = lambda i, ci: (i, 0)
    return pl.pallas_call(
        body, grid=(s // tm, N_FF_PAIRS),
        in_specs=[pl.BlockSpec((tm, D_MODEL), tok), pl.BlockSpec((tm, D_MODEL), tok), pl.BlockSpec((tm, D_MODEL), tok),
                  _resident((N_DEV, FF_CHUNK, D_MODEL)), _resident((N_DEV, 3, FF_PADDED)), _resident((N_DEV, 1, FF_PADDED)),
                  _resident((N_FF_PAIRS, FF_CHUNK, D_MODEL))],
        out_specs=[pl.BlockSpec((2, 1, tm, FF_PADDED), lambda i, ci: (0, ci, i, 0)),
                   pl.BlockSpec((2, 1, tm, FF_PADDED), lambda i, ci: (0, ci, i, 0)),
                   pl.BlockSpec((1, tm, FF_CHUNK), lambda i, ci: (ci, i, 0)),
                   pl.BlockSpec((tm, D_MODEL), tok), pl.BlockSpec((1, 1), lambda i, ci: (0, 0))],
        out_shape=[jax.ShapeDtypeStruct((2, N_FF_PAIRS, s, FF_PADDED), BF16), jax.ShapeDtypeStruct((2, N_FF_PAIRS, s, FF_PADDED), BF16),
                   jax.ShapeDtypeStruct((N_FF_PAIRS, s, FF_CHUNK), BF16), jax.ShapeDtypeStruct((s, D_MODEL), F32),
                   jax.ShapeDtypeStruct((1, 1), F32)],
        scratch_shapes=[pltpu.VMEM((2, FF_LANE_CHUNKS, tm + hal, LANES), F32), pltpu.VMEM((N_DEV, hal, FF_CHUNK), F32),
                        pltpu.VMEM((tm, FF_PADDED), BF16), pltpu.VMEM((tm, D_MODEL), F32)],
        compiler_params=_cparams("arbitrary", "arbitrary"), name="ffn_fwd")(h2, x2, target, w_up, fw, fb, w_down)


def _ffn_bwd(dy, up0, gu, w_up, fw, w_down):
    s = dy.shape[0]
    tm = _tile(s)
    nt = s // tm
    nxt = SUBLANES
    rc = min(128, tm)

    def body(dy_ref, up0_ref, gu_ref, wu_ref, fw_ref, wd_ref,
             dup0_ref, dh2_ref, dfw_ref, dfb_ref, dext_ref, carry_ref, dact_buf, dup0_buf):
        i, ci = pl.program_id(0), pl.program_id(1)

        @pl.when((i == 0) & (ci == 0))
        def _():
            for ref in (carry_ref, dfw_ref, dfb_ref, dext_ref, dact_buf):
                ref[...] = jnp.zeros(ref.shape, F32)
            dup0_buf[...] = jnp.zeros(dup0_buf.shape, BF16)

        @pl.when(ci == 0)
        def _():
            dh2_ref[...] = jnp.zeros(dh2_ref.shape, F32)

        ws = (fw_ref[ci], fw_ref[ci + N_FF_PAIRS])
        fold = lambda v: jnp.sum(v.reshape(rc // SUBLANES, SUBLANES, LANES), axis=0)
        half_rows = (slice(0, tm // 2), slice(tm // 2, tm))
        n_grp = len(FF_COLS)

        def dact_slices(grp):
            lo_c, hi_c = FF_COLS[grp]

            def make(rows):
                def run():
                    dact_buf[rows, lo_c:hi_c] = _dot_nt(dy_ref[rows, :].astype(BF16), wd_ref[ci, lo_c:hi_c, :])
                return run
            return [make(rows) for rows in half_rows]

        def dh2_slices(grp):
            lo_c, hi_c = FF_COLS[grp]

            def make(half, rows):
                def run():
                    c = ci + half * N_FF_PAIRS
                    dh2_ref[rows, :] += _dot(dup0_buf[half, rows, lo_c:hi_c], wu_ref[c, lo_c:hi_c, :])
                return run
            return [make(half, rows) for half in range(2) for rows in half_rows]

        def vector_blocks(grp):
            lo_c, hi_c = FF_COLS[grp]
            chunks = range(lo_c // LANES, -(-hi_c // LANES))
            blocks = []

            def stage():
                for half in range(2):
                    c = ci + half * N_FF_PAIRS
                    for j in chunks:
                        dext_ref[half, j, tm:tm + nxt, :] = carry_ref[c, :, j * LANES:(j + 1) * LANES]
            blocks.append(stage)
            for j in chunks:
                lanes = slice(j * LANES, (j + 1) * LANES)
                acc = [jnp.zeros((SUBLANES, LANES), F32)] * 8

                def grads(r, lanes=lanes, j=j, acc=acc):
                    base = r * rc
                    g = gu_ref[0, 0, base:base + rc, lanes].astype(F32)
                    u = gu_ref[1, 0, base:base + rc, lanes].astype(F32)
                    sg = _sigmoid(g)
                    silu = g * sg
                    dact = dact_buf[base:base + rc, lanes]
                    ds = (dact * u * (sg + silu - silu * sg), dact * silu)
                    for half in range(2):
                        dext_ref[half, j, base:base + rc, :] = ds[half]
                        acc[4 * half] = acc[4 * half] + fold(ds[half])

                def conv_back(r, lanes=lanes, j=j, acc=acc):
                    base = r * rc
                    for half in range(2):
                        d, w = dext_ref.at[half, j], ws[half]
                        taps = [_tap(d, base + k, rc) for k in range(3)]
                        dup0 = w[2:3, lanes] * taps[0] + w[1:2, lanes] * taps[1] + w[0:1, lanes] * taps[2]
                        dup0_buf[half, base:base + rc, lanes] = dup0.astype(BF16)
                        u0 = up0_ref[half, 0, base:base + rc, lanes].astype(F32)
                        for k in range(3):
                            acc[4 * half + 1 + k] = acc[4 * half + 1 + k] + fold(taps[2 - k] * u0)

                def sums(lanes=lanes, j=j, acc=acc):
                    for half in range(2):
                        c = ci + half * N_FF_PAIRS
                        carry_ref[c, :, lanes] = dext_ref[half, j, 0:nxt, :]
                        dfb_ref[c, :, lanes] += jnp.sum(acc[4 * half], axis=0, keepdims=True)
                        dfw_ref[c, :, lanes] += jnp.concatenate(
                            [jnp.sum(acc[4 * half + 1 + k], axis=0, keepdims=True) for k in range(3)], axis=0)

                blocks += [functools.partial(grads, r) for r in range(tm // rc)]
                blocks += [functools.partial(conv_back, r) for r in range(tm // rc)] + [sums]

            def finish():
                for half in range(2):
                    dup0_ref[half, 0, :, lo_c:hi_c] = dup0_buf[half, :, lo_c:hi_c]
            blocks.append(finish)
            return blocks

        for run in dact_slices(0):
            run()
        for grp in range(n_grp):
            matmuls = (dact_slices(grp + 1) if grp + 1 < n_grp else []) + (dh2_slices(grp - 1) if grp > 0 else [])
            blocks = vector_blocks(grp)
            every = max(1, len(blocks) // (len(matmuls) + 1))
            for n, run in enumerate(blocks):
                run()
                if n % every == every - 1 and matmuls:
                    matmuls.pop(0)()
            for run in matmuls:
                run()
        for run in dh2_slices(n_grp - 1):
            run()

    tok = lambda i, ci: (nt - 1 - i, 0)
    acc = lambda shape: pl.BlockSpec(shape, lambda i, ci: (0,) * len(shape))
    saved = pl.BlockSpec((2, 1, tm, FF_PADDED), lambda i, ci: (0, ci, nt - 1 - i, 0))
    return pl.pallas_call(
        body, grid=(nt, N_FF_PAIRS),
        in_specs=[pl.BlockSpec((tm, D_MODEL), tok), saved, saved,
                  _resident((N_DEV, FF_CHUNK, D_MODEL)), _resident((N_DEV, 3, FF_PADDED)),
                  _resident((N_FF_PAIRS, FF_CHUNK, D_MODEL))],
        out_specs=[pl.BlockSpec((2, 1, tm, FF_CHUNK), lambda i, ci: (0, ci, nt - 1 - i, 0)),
                   pl.BlockSpec((tm, D_MODEL), tok), acc((N_DEV, 3, FF_PADDED)), acc((N_DEV, 1, FF_PADDED))],
        out_shape=[jax.ShapeDtypeStruct((2, N_FF_PAIRS, s, FF_CHUNK), BF16), jax.ShapeDtypeStruct((s, D_MODEL), F32),
                   jax.ShapeDtypeStruct((N_DEV, 3, FF_PADDED), F32), jax.ShapeDtypeStruct((N_DEV, 1, FF_PADDED), F32)],
        scratch_shapes=[pltpu.VMEM((2, FF_LANE_CHUNKS, tm + nxt, LANES), F32), pltpu.VMEM((N_DEV, nxt, FF_PADDED), F32),
                        pltpu.VMEM((tm, FF_PADDED), F32), pltpu.VMEM((2, tm, FF_PADDED), BF16)],
        compiler_params=_cparams("arbitrary", "arbitrary"), name="ffn_bwd")(dy, up0, gu, w_up, fw, w_down)


def _ffn_norm_bwd(dh2, dy, x2, g_ffn, w_out):
    s = dy.shape[0]
    tm = _tile(s)

    def body(dh_ref, dy_ref, x2_ref, g_ref, wo_ref, dx2_ref, dmix_ref, dg_ref, dbo_ref):
        @pl.when(pl.program_id(0) == 0)
        def _():
            dg_ref[...] = jnp.zeros(dg_ref.shape, F32)
            dbo_ref[...] = jnp.zeros(dbo_ref.shape, F32)

        x2v = x2_ref[...]
        r = lax.rsqrt(jnp.mean(x2v * x2v, axis=-1, keepdims=True) + EPS)
        n2 = x2v * r
        dh2 = dh_ref[...]
        dg_ref[...] += jnp.sum(dh2 * n2, axis=0, keepdims=True)
        dn = dh2 * g_ref[...]
        dx2 = dy_ref[...] + r * (dn - n2 * jnp.mean(dn * n2, axis=-1, keepdims=True))
        dx2_ref[...] = dx2
        dbo_ref[...] += jnp.sum(dx2, axis=0, keepdims=True)
        dmix_ref[...] = _dot_nt(dx2.astype(BF16), wo_ref[...])

    tok = pl.BlockSpec((tm, D_MODEL), lambda i: (i, 0))
    vec = pl.BlockSpec((1, D_MODEL), lambda i: (0, 0))
    return pl.pallas_call(
        body, grid=(s // tm,),
        in_specs=[tok, tok, tok, _resident((1, D_MODEL)), _resident((D_MODEL, D_MODEL))],
        out_specs=[tok, tok, vec, vec],
        out_shape=[jax.ShapeDtypeStruct((s, D_MODEL), F32), jax.ShapeDtypeStruct((s, D_MODEL), F32),
                   jax.ShapeDtypeStruct((1, D_MODEL), F32), jax.ShapeDtypeStruct((1, D_MODEL), F32)],
        compiler_params=_cparams("arbitrary"), name="ffn_norm_bwd")(dh2, dy, x2, g_ffn, w_out)


def _conv_bwd(dmixed, c1, cin, cw8, gain, bias, g8s):
    ns = len(g8s)
    s_in, s_out, s_shape, s_scratch = _scatter_specs(g8s)
    s = cin.shape[0]
    tm = _tile(s)
    nt = s // tm
    rc = 64
    rn = min(256, tm)
    hal = CONV_HALO
    lead = hal - (CONV_KERNEL - 1)
    nchunk = CONV_WIDTH // LANES

    def body(dc3_ref, dc3n_ref, c1_ref, c1n_ref, cin_ref, cinp_ref, cw_ref, gain_ref, bias_ref, *rest):
        dcin_ref, dcw_ref, dcb_ref, dgain_ref, dbias_ref, dbcin_ref = rest[ns:ns + 6]
        c0_ext, dc1_ext, dcw8 = rest[2 * ns + 6:2 * ns + 9]
        i = pl.program_id(0)
        first, last = i == 0, i == nt - 1
        own = rest[2 * ns + 9:]
        _run_scatters([_ReduceScatter(rest[a], rest[ns + 6 + a], *own[N_SCATTER_SCRATCH * a:N_SCATTER_SCRATCH * (a + 1)])
                       for a in range(ns)], i, nt)

        @pl.when(first)
        def _():
            for ref in (dcw8, dcb_ref, dgain_ref, dbias_ref, dbcin_ref):
                ref[...] = jnp.zeros(ref.shape, F32)

        lo = _lo_mask((1, LANES))

        def norm_bwd(dc3, c1v, cols):
            nrm, rstd = _group_stats(c1v, lo)
            c2 = nrm * gain_ref[:, cols] + bias_ref[:, cols]
            sg = _sigmoid(c2)
            dc2 = dc3 * (sg * (1.0 + c2 * (1.0 - sg)))
            dn = dc2 * gain_ref[:, cols]
            inv = 1.0 / HEAD_DIM
            dc1 = rstd * (dn - _half_sums(dn, lo) * inv - nrm * (_half_sums(dn * nrm, lo) * inv))
            return dc1, dc2, nrm

        def row_sum(v):
            return jnp.sum(v, axis=0, keepdims=True)

        for cc in range(nchunk):
            cols = slice(cc * LANES, (cc + 1) * LANES)
            gcols = slice(CONV_WIDTH + cc * LANES, CONV_WIDTH + (cc + 1) * LANES)
            c0e, d1e = c0_ext.at[cc], dc1_ext.at[cc]
            c0e[0:hal, :] = jnp.where(first, 0.0, cinp_ref[:, cols] * _sigmoid(cinp_ref[:, gcols]))
            dc1n, _, _ = norm_bwd(dc3n_ref[:, cols], c1n_ref[cc], cols)
            d1e[tm:tm + hal, :] = jnp.where(last, 0.0, dc1n)

            for r in range(tm // rn):
                rows = slice(r * rn, (r + 1) * rn)
                c0e[hal + r * rn:hal + (r + 1) * rn, :] = cin_ref[rows, cols] * _sigmoid(cin_ref[rows, gcols])
                dc1, dc2, nrm = norm_bwd(dc3_ref[rows, cols], c1_ref[cc, rows, :], cols)
                d1e[rows, :] = dc1
                dgain_ref[:, cols] += row_sum(dc2 * nrm)
                dbias_ref[:, cols] += row_sum(dc2)
                dcb_ref[:, cols] += row_sum(dc1)
            zero = jnp.zeros((1, LANES), F32)

            for k0 in range(0, CONV_KERNEL, SUBLANES):
                taps = range(k0, min(k0 + SUBLANES, CONV_KERNEL))

                def tap_sums(r, acc, taps=taps):
                    d = _rows(d1e, r * rc, rc)
                    return tuple(a + jnp.sum(d * _rows(c0e, r * rc + lead + k, rc), axis=0) for a, k in zip(acc, taps))

                acc = lax.fori_loop(0, tm // rc, tap_sums, tuple(dcw8[k * SUBLANES:(k + 1) * SUBLANES, cols] for k in taps))
                for a, k in zip(acc, taps):
                    dcw8[k * SUBLANES:(k + 1) * SUBLANES, cols] = a

            def input_grad(r, sums):
                rows = pl.ds(pl.multiple_of(r * rc, rc), rc)
                dc0 = jnp.zeros((rc // SUBLANES, SUBLANES, LANES), F32)
                for k in range(CONV_KERNEL):
                    dc0 = dc0 + cw_ref[k * SUBLANES:(k + 1) * SUBLANES, cols][None] * _rows(d1e, r * rc + CONV_KERNEL - 1 - k, rc)
                dc0 = dc0.reshape(rc, LANES)
                sg = _sigmoid(cin_ref[rows, gcols])
                da = dc0 * sg
                dgate = dc0 * cin_ref[rows, cols] * sg * (1.0 - sg)
                dcin_ref[rows, cols] = da.astype(BF16)
                dcin_ref[rows, gcols] = dgate.astype(BF16)
                return sums[0] + row_sum(da), sums[1] + row_sum(dgate)

            sums = lax.fori_loop(0, tm // rc, input_grad, (zero, zero))
            dbcin_ref[:, cols] += sums[0]
            dbcin_ref[:, gcols] += sums[1]

        @pl.when(last)
        def _():
            for k in range(CONV_KERNEL):
                dcw_ref[k:k + 1, :] = jnp.sum(dcw8[k * SUBLANES:(k + 1) * SUBLANES, :], axis=0, keepdims=True)

    nh = tm // hal
    acc = lambda shape: pl.BlockSpec(shape, lambda i: (0,) * len(shape))
    return pl.pallas_call(
        body, grid=(nt,),
        in_specs=[pl.BlockSpec((tm, CONV_WIDTH), lambda i: (i, 1)),
                  pl.BlockSpec((hal, CONV_WIDTH), lambda i: (jnp.minimum((i + 1) * nh, s // hal - 1), 1)),
                  pl.BlockSpec((nchunk, tm, LANES), lambda i: (0, i, 0)),
                  pl.BlockSpec((nchunk, hal, LANES), lambda i: (0, jnp.minimum((i + 1) * nh, s // hal - 1), 0)),
                  pl.BlockSpec((tm, CIN_COLS), lambda i: (i, 0)),
                  pl.BlockSpec((hal, CIN_COLS), lambda i: (jnp.maximum(i * nh - 1, 0), 0)),
                  _resident((CONV_KERNEL * SUBLANES, CONV_WIDTH)), _resident((1, CONV_WIDTH)), _resident((1, CONV_WIDTH))] + s_in,
        out_specs=[pl.BlockSpec((tm, CIN_COLS), lambda i: (i, 0)), acc((CONV_KERNEL, CONV_WIDTH)), acc((1, CONV_WIDTH)),
                   acc((1, CONV_WIDTH)), acc((1, CONV_WIDTH)), acc((1, CIN_COLS))] + s_out,
        out_shape=[jax.ShapeDtypeStruct((s, CIN_COLS), BF16), jax.ShapeDtypeStruct((CONV_KERNEL, CONV_WIDTH), F32),
                   jax.ShapeDtypeStruct((1, CONV_WIDTH), F32), jax.ShapeDtypeStruct((1, CONV_WIDTH), F32),
                   jax.ShapeDtypeStruct((1, CONV_WIDTH), F32), jax.ShapeDtypeStruct((1, CIN_COLS), F32)] + s_shape,
        scratch_shapes=[pltpu.VMEM((nchunk, tm + hal, LANES), F32), pltpu.VMEM((nchunk, tm + hal, LANES), F32),
                        pltpu.VMEM((CONV_KERNEL * SUBLANES, CONV_WIDTH), F32)] + s_scratch,
        compiler_params=_cparams("arbitrary"), name="conv_bwd")(dmixed, dmixed, c1, c1, cin, cin, cw8, gain, bias, *g8s)


def _attn_bwd(qkv, dmixed, gq2, gk2, sinks, g8s):
    ns = len(g8s)
    s_in, s_out, s_shape, s_scratch = _scatter_specs(g8s)
    s = qkv.shape[0]
    tq = _tile(s)
    nb = tq // ATT_BLOCK
    nt = s // tq

    def body(q_ref, kv_ref, kvp_ref, do_ref, gq_ref, gk_ref, sink_ref, *rest):
        dqkv_ref, dgq_ref, dgk_ref, dsink_ref, dbqkv_ref = rest[ns:ns + 5]
        dk_acc, dv_acc, carry_k, carry_v = rest[2 * ns + 5:2 * ns + 9]
        i = pl.program_id(0)
        t = nt - 1 - i
        own = rest[2 * ns + 9:]
        _run_scatters([_ReduceScatter(rest[a], rest[ns + 5 + a], *own[N_SCATTER_SCRATCH * a:N_SCATTER_SCRATCH * (a + 1)])
                       for a in range(ns)], i, nt)

        @pl.when(i == 0)
        def _():
            for ref in (carry_k, carry_v, dgq_ref, dgk_ref, dsink_ref, dbqkv_ref):
                ref[...] = jnp.zeros(ref.shape, F32)

        lo = _lo_mask((1, LANES))
        lane_id = lax.broadcasted_iota(jnp.int32, (1, LANES), 1)
        kv_all = jnp.concatenate([kvp_ref[...], kv_ref[...]], axis=0)
        k_lo, k_hi, v_lo, v_hi, kn_pre, rk = _kv_variants(kv_all, gk_ref[...], lo)
        for acc_ref, carry in ((dk_acc, carry_k), (dv_acc, carry_v)):
            acc_ref[:, 0:tq, :] = jnp.zeros((N_KV_HEADS, tq, LANES), F32)
            acc_ref[:, tq:tq + ATT_BLOCK, :] = carry[...]
        dsink = jnp.zeros((1, LANES), F32)
        dgq = jnp.zeros((1, LANES), F32)
        gq = gq_ref[...]
        for b in range(nb):
            rel, valid = _att_consts(t == 0, b)
            rows = slice(b * ATT_BLOCK, (b + 1) * ATT_BLOCK)
            keys = slice(b * ATT_BLOCK, (b + 2) * ATT_BLOCK)
            for kvh in range(N_KV_HEADS):
                pairs = (2 * kvh, 2 * kvh + 1)
                q_raw = jnp.concatenate([q_ref[rows, p * LANES:(p + 1) * LANES] for p in pairs], axis=0)
                qn_pre, rq = _head_norm(q_raw, lo)
                q2 = (qn_pre * gq).astype(BF16)
                do2 = jnp.concatenate([do_ref[rows, p * LANES:(p + 1) * LANES] for p in pairs], axis=0).astype(BF16)
                dq2 = jnp.zeros((2 * ATT_BLOCK, LANES), F32)
                for odd, (k_op, v_op) in enumerate(((k_lo[kvh][keys], v_lo[kvh][keys]), (k_hi[kvh][keys], v_hi[kvh][keys]))):
                    ha, hb = 2 * pairs[0] + odd, 2 * pairs[1] + odd
                    p, p_sink = _probs(q2, k_op, rel, valid, _row_const(SLOPES[ha], SLOPES[hb]),
                                       _row_const(sink_ref[ha], sink_ref[hb]))
                    dp = _dot_nt(do2, v_op)
                    delta = jnp.sum(p * dp, axis=-1, keepdims=True)
                    ds = (p * (dp - delta) * (1.0 / math.sqrt(HEAD_DIM))).astype(BF16)
                    dsk = p_sink * delta
                    dsink = dsink - jnp.where(lane_id == ha, jnp.sum(dsk[0:ATT_BLOCK]), 0.0) \
                        - jnp.where(lane_id == hb, jnp.sum(dsk[ATT_BLOCK:]), 0.0)
                    dq2 = dq2 + _dot(ds, k_op)
                    half = lo if odd == 0 else jnp.logical_not(lo)
                    dk_acc[kvh, keys, :] += jnp.where(half, _dot_tn(ds, q2), 0.0)
                    dv_acc[kvh, keys, :] += jnp.where(half, _dot_tn(p.astype(BF16), do2), 0.0)
                dgq = dgq + jnp.sum(dq2 * qn_pre, axis=0, keepdims=True)
                dq_raw = _head_norm_bwd(dq2 * gq, qn_pre, rq, lo)
                for n, p_ in enumerate(pairs):
                    blk = dq_raw[n * ATT_BLOCK:(n + 1) * ATT_BLOCK]
                    dqkv_ref[rows, p_ * LANES:(p_ + 1) * LANES] = blk.astype(BF16)
                    dbqkv_ref[:, p_ * LANES:(p_ + 1) * LANES] += jnp.sum(blk, axis=0, keepdims=True)
        carry_k[...] = dk_acc[:, 0:ATT_BLOCK, :]
        carry_v[...] = dv_acc[:, 0:ATT_BLOCK, :]

        def fold(acc_ref):
            both = []
            for kvh in range(N_KV_HEADS):
                a = acc_ref[kvh, ATT_BLOCK:ATT_BLOCK + tq, :]
                both.append(a + pltpu.roll(a, HEAD_DIM, 1))
            return jnp.where(lo, both[0], both[1])

        dkn = fold(dk_acc)
        dv = fold(dv_acc)
        kn_c, rk_c = kn_pre[ATT_BLOCK:], rk[ATT_BLOCK:]
        dgk_ref[...] += jnp.sum(dkn * kn_c, axis=0, keepdims=True)
        dk_raw = _head_norm_bwd(dkn * gk_ref[...], kn_c, rk_c, lo)
        dqkv_ref[:, Q_COLS:Q_COLS + KV_COLS] = dk_raw.astype(BF16)
        dqkv_ref[:, Q_COLS + KV_COLS:] = dv.astype(BF16)
        dbqkv_ref[:, Q_COLS:Q_COLS + KV_COLS] += jnp.sum(dk_raw, axis=0, keepdims=True)
        dbqkv_ref[:, Q_COLS + KV_COLS:] += jnp.sum(dv, axis=0, keepdims=True)
        dgq_ref[...] += dgq
        dsink_ref[...] += dsink

        @pl.when(i == nt - 1)
        def _():
            for ref in (dgq_ref, dgk_ref):
                v = ref[...]
                ref[...] = v + pltpu.roll(v, HEAD_DIM, 1)

    acc = lambda shape: pl.BlockSpec(shape, lambda i: (0,) * len(shape))
    return pl.pallas_call(
        body, grid=(nt,),
        in_specs=[pl.BlockSpec((tq, Q_COLS), lambda i: (nt - 1 - i, 0)),
                  pl.BlockSpec((tq, 2 * KV_COLS), lambda i: (nt - 1 - i, 2)),
                  pl.BlockSpec((ATT_BLOCK, 2 * KV_COLS), lambda i: (jnp.maximum((nt - 1 - i) * nb - 1, 0), 2)),
                  pl.BlockSpec((tq, Q_COLS), lambda i: (nt - 1 - i, 0)),
                  _resident((1, LANES)), _resident((1, LANES)), pl.BlockSpec(memory_space=pltpu.SMEM)] + s_in,
        out_specs=[pl.BlockSpec((tq, QKV_COLS), lambda i: (nt - 1 - i, 0)), acc((1, LANES)), acc((1, LANES)),
                   acc((1, LANES)), acc((1, QKV_COLS))] + s_out,
        out_shape=[jax.ShapeDtypeStruct((s, QKV_COLS), BF16), jax.ShapeDtypeStruct((1, LANES), F32),
                   jax.ShapeDtypeStruct((1, LANES), F32), jax.ShapeDtypeStruct((1, LANES), F32),
                   jax.ShapeDtypeStruct((1, QKV_COLS), F32)] + s_shape,
        scratch_shapes=[pltpu.VMEM((N_KV_HEADS, tq + ATT_BLOCK, LANES), F32), pltpu.VMEM((N_KV_HEADS, tq + ATT_BLOCK, LANES), F32),
                        pltpu.VMEM((N_KV_HEADS, ATT_BLOCK, LANES), F32), pltpu.VMEM((N_KV_HEADS, ATT_BLOCK, LANES), F32)] + s_scratch,
        compiler_params=_cparams("arbitrary"), name="attn_bwd")(qkv, qkv, qkv, dmixed, gq2, gk2, sinks, *g8s)


def _in_bwd(dqkv, dcin, w_in_t, x, dx2, g_mix):
    s = x.shape[0]
    tm = _tile(s)

    def body(dq_ref, dc_ref, w_ref, x_ref, dx2_ref, g_ref, gx_ref, dg_ref):
        @pl.when(pl.program_id(0) == 0)
        def _():
            dg_ref[...] = jnp.zeros(dg_ref.shape, F32)

        dh = _dot(dq_ref[...], w_ref[0:QKV_COLS, :]) + _dot(dc_ref[...], w_ref[QKV_COLS:, :])
        xv = x_ref[...]
        r = lax.rsqrt(jnp.mean(xv * xv, axis=-1, keepdims=True) + EPS)
        n = xv * r
        dg_ref[...] += jnp.sum(dh * n, axis=0, keepdims=True)
        dn = dh * g_ref[...]
        gx_ref[...] = dx2_ref[...] + r * (dn - n * jnp.mean(dn * n, axis=-1, keepdims=True))

    return pl.pallas_call(
        body, grid=(s // tm,),
        in_specs=[pl.BlockSpec((tm, QKV_COLS), lambda i: (i, 0)), pl.BlockSpec((tm, CIN_COLS), lambda i: (i, 0)),
                  _resident((QKV_COLS + CIN_COLS, D_MODEL)),
                  pl.BlockSpec((tm, D_MODEL), lambda i: (i, 0)), pl.BlockSpec((tm, D_MODEL), lambda i: (i, 0)),
                  _resident((1, D_MODEL))],
        out_specs=[pl.BlockSpec((tm, D_MODEL), lambda i: (i, 0)), pl.BlockSpec((1, D_MODEL), lambda i: (0, 0))],
        out_shape=[jax.ShapeDtypeStruct((s, D_MODEL), F32), jax.ShapeDtypeStruct((1, D_MODEL), F32)],
        compiler_params=_cparams("arbitrary"), name="in_bwd")(dqkv, dcin, w_in_t, x, dx2, g_mix)


def _tn_matmul(a, b, name):
    ga, s, m = a.shape
    gb, _, n = b.shape
    g = max(ga, gb)
    tk = min(TN_TOKENS, s)

    def body(a_ref, b_ref, o_ref):
        @pl.when(pl.program_id(1) == 0)
        def _():
            o_ref[...] = jnp.zeros(o_ref.shape, F32)

        o_ref[0] += _dot_tn(a_ref[0].astype(BF16), b_ref[0].astype(BF16))

    return pl.pallas_call(
        body, grid=(g, s // tk),
        in_specs=[pl.BlockSpec((1, tk, m), (lambda gi, k: (gi, k, 0)) if ga > 1 else (lambda gi, k: (0, k, 0))),
                  pl.BlockSpec((1, tk, n), (lambda gi, k: (gi, k, 0)) if gb > 1 else (lambda gi, k: (0, k, 0)))],
        out_specs=pl.BlockSpec((1, m, n), lambda gi, k: (gi, 0, 0)),
        out_shape=jax.ShapeDtypeStruct((g, m, n), F32),
        compiler_params=_cparams("parallel", "arbitrary"), name=name)(a, b)


def _allgather(shards, dtypes):
    n = len(shards)
    n_copies = 1 + 2 * len(OTHER_CHIPS)

    def body(*refs):
        ins, outs = refs[:n], refs[n:2 * n]
        send_sems, recv_sems = refs[2 * n:]
        x, y, c = _position()
        me, sibling = (x, y, c), (x, y, 1 - c)
        chips = [(_flip(x, fx), _flip(y, fy)) for fx, fy in OTHER_CHIPS]
        for a in range(n):
            outs[a][_dev_index(*me)] = ins[a][...].astype(dtypes[a])

        def copy(a, k, block, to):
            rows = outs[a].at[_dev_index(*block)]
            return pltpu.make_async_remote_copy(src_ref=rows, dst_ref=rows, send_sem=send_sems.at[a, k],
                                                recv_sem=recv_sems.at[a, k], device_id=to, device_id_type=MESH)

        started = []
        for a in range(n):
            for j, chip in enumerate(chips):
                started.append(copy(a, 1 + j, me, (*chip, c)))
            started.append(copy(a, 0, me, sibling))
        for cp in started:
            cp.start()
        for a in range(n):
            for j, chip in enumerate(chips):
                copy(a, 1 + j, (*chip, c), me).wait_recv()
                fwd = copy(a, 1 + len(chips) + j, (*chip, c), sibling)
                fwd.start()
                started.append(fwd)
        for a in range(n):
            copy(a, 0, sibling, me).wait_recv()
            for j, chip in enumerate(chips):
                copy(a, 1 + len(chips) + j, (*chip, 1 - c), me).wait_recv()
        for cp in started:
            cp.wait_send()

    vmem = pl.BlockSpec(memory_space=pltpu.VMEM)
    return pl.pallas_call(
        body, in_specs=[vmem] * n, out_specs=[vmem] * n,
        out_shape=[jax.ShapeDtypeStruct((N_DEV,) + w.shape, dt) for w, dt in zip(shards, dtypes)],
        scratch_shapes=[pltpu.SemaphoreType.DMA((n, n_copies)), pltpu.SemaphoreType.DMA((n, n_copies))],
        compiler_params=pltpu.CompilerParams(vmem_limit_bytes=VMEM_LIMIT), name="allgather_weights")(*shards)


def _final_exchange(g8, v):
    rows = v.shape[0]
    _, _, s_shape, s_scratch = _scatter_specs([g8])

    def body(g_ref, v_ref, gout_ref, vout_ref, gath, send_sems, recv_sems, *rs_scratch):
        scatter = _ReduceScatter(g_ref, gout_ref, *rs_scratch)
        x, y, c = _position()
        me = _dev_index(x, y, c)
        peers = [(_flip(x, k >> 2 & 1), _flip(y, k >> 1 & 1), _flip(c, k & 1)) for k in range(1, N_DEV)]

        def copy(k, block):
            return pltpu.make_async_remote_copy(src_ref=gath.at[block], dst_ref=gath.at[block], send_sem=send_sems.at[k],
                                                recv_sem=recv_sems.at[k], device_id=peers[k], device_id_type=MESH)

        scatter.start()
        gath[me] = v_ref[...]
        for k in range(N_DEV - 1):
            copy(k, me).start()
        scatter.middle()
        for k in range(N_DEV - 1):
            copy(k, _dev_index(*peers[k])).wait_recv()
        for k in range(N_DEV - 1):
            copy(k, me).wait_send()
        total = gath[0]
        for d in range(1, N_DEV):
            total = total + gath[d]
        vout_ref[...] = total
        scatter.finish()

    vmem = pl.BlockSpec(memory_space=pltpu.VMEM)
    return pl.pallas_call(
        body, in_specs=[pl.BlockSpec(memory_space=pl.ANY), vmem], out_specs=[vmem, vmem],
        out_shape=s_shape + [jax.ShapeDtypeStruct((rows, LANES), F32)],
        scratch_shapes=[pltpu.VMEM((N_DEV, rows, LANES), F32), pltpu.SemaphoreType.DMA((N_DEV - 1,)),
                        pltpu.SemaphoreType.DMA((N_DEV - 1,))] + s_scratch,
        compiler_params=pltpu.CompilerParams(vmem_limit_bytes=VMEM_LIMIT), name="final_exchange")(g8, v)


def _row_tile(r):
    for n in (8, 4, 2):
        if r % (n * SUBLANES) == 0:
            return r // n
    return r


def _adam_math(wv, gv, mv, vv):
    mn = ADAM_B1 * mv + (1.0 - ADAM_B1) * gv
    vn = ADAM_B2 * vv + (1.0 - ADAM_B2) * (gv * gv)
    m_hat = mn / (1.0 - ADAM_B1 ** ADAM_STEP)
    v_hat = vn / (1.0 - ADAM_B2 ** ADAM_STEP)
    return -ADAM_LR * (m_hat / (jnp.sqrt(v_hat) + ADAM_EPS) + ADAM_WD * wv), mn, vn


def _adamw(w, g, m, v, name):
    r, c_ = w.shape
    tr = _row_tile(r)

    def body(w_ref, g_ref, m_ref, v_ref, d_ref, mo_ref, vo_ref):
        d_ref[...], mo_ref[...], vo_ref[...] = _adam_math(w_ref[...], g_ref[...], m_ref[...], v_ref[...])

    spec = pl.BlockSpec((tr, c_), lambda i: (i, 0))
    return pl.pallas_call(
        body, grid=(r // tr,), in_specs=[spec] * 4, out_specs=[spec] * 3,
        out_shape=[jax.ShapeDtypeStruct((r, c_), F32)] * 3,
        compiler_params=_cparams("parallel"), name=name)(w, g, m, v)


FW_ROWS = 24
CW_ROWS = 32
R_FW = 0
R_FB = R_FW + N_DEV * FW_ROWS
R_CW = R_FB + 48
R_BQKV = R_CW + (CONV_WIDTH // LANES) * CW_ROWS
R_BCIN = R_BQKV + 8
R_GMIX = R_BCIN + 8
R_BOUT = R_GMIX + 8
R_GFFN = R_BOUT + 8
R_CB = R_GFFN + 8
R_CGAIN = R_CB + 8
R_CBIAS = R_CGAIN + 8
R_QKS = R_CBIAS + 8
SMALL_ROWS = R_QKS + 8


def _pack_small(raw):
    def rows(a, n):
        a = a.reshape(-1, LANES)
        return jnp.pad(a, ((0, n - a.shape[0]), (0, 0)))

    fw = jnp.pad(raw["dfw"].reshape(N_DEV, -1, LANES), ((0, 0), (0, FW_ROWS - 3 * FF_LANE_CHUNKS), (0, 0)))
    cw = jnp.pad(raw["dcw"].reshape(CONV_KERNEL, -1, LANES).transpose(1, 0, 2), ((0, 0), (0, CW_ROWS - CONV_KERNEL), (0, 0)))
    qks = jnp.concatenate([raw["dgq"], raw["dgk"], raw["dsink"], jnp.pad(raw["loss"], ((0, 0), (0, LANES - 1)))], axis=0)
    return jnp.concatenate([
        fw.reshape(-1, LANES), rows(raw["dfb"][:, 0, :FF_CHUNK], 48), cw.reshape(-1, LANES), rows(raw["dbqkv"], 8),
        rows(raw["dbcin"], 8), rows(raw["dg_mix"], 8), rows(raw["db_out"], 8), rows(raw["dg_ffn"], 8), rows(raw["dcb"], 8),
        rows(raw["dcgain"], 8), rows(raw["dcbias"], 8), rows(qks, 8)], axis=0)


def _adamw_small(gpack, w, m, v):
    n = len(SMALL)
    ix = {name: i for i, name in enumerate(SMALL)}

    def body(g_ref, *refs):
        w_refs, m_refs, v_refs, outs = refs[:n], refs[n:2 * n], refs[2 * n:3 * n], refs[3 * n:]
        d = _dev_index(*_position())

        def step(name, idx, gv):
            i = ix[name]
            delta, mn, vn = _adam_math(w_refs[i][idx], gv, m_refs[i][idx], v_refs[i][idx])
            for ref, val in zip(outs[4 * i:4 * i + 4], (gv, delta, mn, vn)):
                ref[idx] = val

        def whole(name, row, nrows):
            step(name, (slice(None), slice(None)), g_ref[row:row + nrows, :])

        whole("mix_norm_gain", R_GMIX, 8)
        whole("b_out", R_BOUT, 8)
        whole("ffn_norm_gain", R_GFFN, 8)
        whole("conv_dw_b", R_CB, 4)
        whole("conv_norm_gain", R_CGAIN, 4)
        whole("conv_norm_bias", R_CBIAS, 4)
        whole("ffn_dw_b", R_FB, 2 * D_FF // LANES)
        nq = QKV_COLS // LANES
        step("b_in", (slice(0, nq), slice(None)), g_ref[R_BQKV:R_BQKV + nq, :])
        step("b_in", (slice(nq, nq + CIN_COLS // LANES), slice(None)), g_ref[R_BCIN:R_BCIN + CIN_COLS // LANES, :])
        step("q_norm_gain", (slice(None), slice(None)), g_ref[R_QKS:R_QKS + 1, 0:HEAD_DIM])
        step("k_norm_gain", (slice(None), slice(None)), g_ref[R_QKS + 1:R_QKS + 2, 0:HEAD_DIM])
        step("attn_sinks", (slice(None), slice(None)), g_ref[R_QKS + 2:R_QKS + 3, 0:N_Q_HEADS])
        blk = g_ref[pl.ds(pl.multiple_of(R_CW + CW_ROWS * lax.shift_right_logical(d, 1), SUBLANES), CW_ROWS), :]
        blk = jnp.where((d & 1) == 1, pltpu.roll(blk, HEAD_DIM, 1), blk)
        step("conv_dw_w", (slice(None), slice(None)), blk[0:CONV_KERNEL, 0:CONV_WIDTH // N_DEV])
        blk = g_ref[pl.ds(pl.multiple_of(R_FW + FW_ROWS * d, SUBLANES), FW_ROWS), :]
        for k in range(3):
            for j in range(FF_LANE_CHUNKS):
                wd = min(LANES, FF_CHUNK - j * LANES)
                row = k * FF_LANE_CHUNKS + j
                step("ffn_dw_w", (slice(k, k + 1), slice(j * LANES, j * LANES + wd)), blk[row:row + 1, 0:wd])

    vmem = pl.BlockSpec(memory_space=pltpu.VMEM)
    args = [gpack] + [d[name] for d in (w, m, v) for name in SMALL]
    outs = pl.pallas_call(
        body, in_specs=[vmem] * len(args), out_specs=[vmem] * (4 * n),
        out_shape=[jax.ShapeDtypeStruct(w[name].shape, F32) for name in SMALL for _ in range(4)],
        compiler_params=pltpu.CompilerParams(vmem_limit_bytes=VMEM_LIMIT), name="adamw_small")(*args)
    return {name: outs[4 * i:4 * i + 4] for i, name in enumerate(SMALL)}


def _token_mixing(x, p, attn_shards, conv_shards):
    qkv, cin, h1 = _mix_proj(x, p["g_mix"], p["w_in_t"], p["b_qkv"], p["b_cin"])
    attn, *from_attn = _attn_fwd(qkv, p["gq2"], p["gk2"], p["sinks"], attn_shards)
    c3, c1, *from_conv = _conv_fwd(cin, p["cw8"], p["cb"], p["cgain"], p["cbias"], conv_shards)
    return (qkv, cin, h1, attn, c3, c1), from_attn, from_conv


def _rest_of_step(x, target, p, saved, scatter):
    s = x.shape[0]
    qkv, cin, h1, attn, c3, c1 = saved
    cw8, w_out, w_up, w_down = p["cw8"], p["w_out"], p["w_up"], p["w_down"]
    x2, h2 = _out_proj(x, attn, c3, w_out, w_out, p["b_out"], p["g_ffn"])
    fw, fb = p["fw"], p["fb"]
    up0, gu, act, dy, loss = _ffn_fwd(h2, x2, target, w_up, fw, fb, w_down)
    dup0, dh2, dfw, dfb = _ffn_bwd(dy, up0, gu, w_up, fw, w_down)
    dx2, dmixed, dg_ffn, db_out = _ffn_norm_bwd(dh2, dy, x2, p["g_ffn"], w_out)
    dw_up = _tn_matmul(dup0.reshape(N_DEV, s, FF_CHUNK), h2[None], "dw_up")
    dw_down = _tn_matmul(act, dy[None], "dw_down").reshape(N_DEV, -1, D_MODEL)
    dw_out = jnp.concatenate([_tn_matmul(attn[None], dx2[None], "dw_out_attn")[0],
                              _tn_matmul(c3[None], dx2[None], "dw_out_conv")[0]], axis=0).reshape(N_DEV, -1, D_MODEL)
    dcin, dcw, dcb, dcgain, dcbias, dbcin, *g_up = _conv_bwd(dmixed, c1, cin, cw8, p["cgain"], p["cbias"], [dw_up] if scatter else [])
    dqkv, dgq, dgk, dsink, dbqkv, *g_down_out = _attn_bwd(qkv, dmixed, p["gq2"], p["gk2"], p["sinks"],
                                                          [dw_down, dw_out] if scatter else [])
    dw_in = jnp.concatenate([_tn_matmul(dqkv[None], h1[None], "dw_qkv")[0], _tn_matmul(dcin[None], h1[None], "dw_cin")[0]], axis=0)
    dw_in = dw_in.reshape(N_DEV, -1, D_MODEL)
    grad_x, dg_mix = _in_bwd(dqkv, dcin, p["w_in_t"], x, dx2, p["g_mix"])
    if scatter:
        big = {"w_up": g_up[0], "w_down": g_down_out[0], "w_in": dw_in, "w_out": g_down_out[1]}
    else:
        big = {"w_up": dw_up, "w_down": dw_down, "w_in": dw_in, "w_out": dw_out}
    small = dict(dg_mix=dg_mix, dbqkv=dbqkv, dbcin=dbcin, dgq=dgq, dgk=dgk, dsink=dsink, dcw=dcw, dcb=dcb, dcgain=dcgain,
                 dcbias=dcbias, db_out=db_out, dg_ffn=dg_ffn, dfw=dfw, dfb=dfb, loss=loss)
    return loss, grad_x, big, small


BIG = ("w_in", "w_out", "w_up", "w_down")
SMALL = ("mix_norm_gain", "b_in", "q_norm_gain", "k_norm_gain", "attn_sinks", "conv_dw_w", "conv_dw_b",
         "conv_norm_gain", "conv_norm_bias", "b_out", "ffn_norm_gain", "ffn_dw_w", "ffn_dw_b")
ORDER = ("mix_norm_gain", "w_in", "b_in", "q_norm_gain", "k_norm_gain", "attn_sinks", "conv_dw_w", "conv_dw_b",
         "conv_norm_gain", "conv_norm_bias", "w_out", "b_out", "ffn_norm_gain", "w_up", "ffn_dw_w", "ffn_dw_b", "w_down")


def kernel(x, mix_norm_gain, w_in, b_in, q_norm_gain, k_norm_gain, attn_sinks, conv_dw_w, conv_dw_b, conv_norm_gain, conv_norm_bias, w_out, b_out, ffn_norm_gain, w_up, ffn_dw_w, ffn_dw_b, w_down, loss_target, m_mix_norm_gain, m_w_in, m_b_in, m_q_norm_gain, m_k_norm_gain, m_attn_sinks, m_conv_dw_w, m_conv_dw_b, m_conv_norm_gain, m_conv_norm_bias, m_w_out, m_b_out, m_ffn_norm_gain, m_w_up, m_ffn_dw_w, m_ffn_dw_b, m_w_down, v_mix_norm_gain, v_w_in, v_b_in, v_q_norm_gain, v_k_norm_gain, v_attn_sinks, v_conv_dw_w, v_conv_dw_b, v_conv_norm_gain, v_conv_norm_bias, v_w_out, v_b_out, v_ffn_norm_gain, v_w_up, v_ffn_dw_w, v_ffn_dw_b, v_w_down):
    w = dict(mix_norm_gain=mix_norm_gain, w_in=w_in, b_in=b_in, q_norm_gain=q_norm_gain, k_norm_gain=k_norm_gain,
             attn_sinks=attn_sinks, conv_dw_w=conv_dw_w, conv_dw_b=conv_dw_b, conv_norm_gain=conv_norm_gain,
             conv_norm_bias=conv_norm_bias, w_out=w_out, b_out=b_out, ffn_norm_gain=ffn_norm_gain, w_up=w_up,
             ffn_dw_w=ffn_dw_w, ffn_dw_b=ffn_dw_b, w_down=w_down)
    m = dict(mix_norm_gain=m_mix_norm_gain, w_in=m_w_in, b_in=m_b_in, q_norm_gain=m_q_norm_gain, k_norm_gain=m_k_norm_gain,
             attn_sinks=m_attn_sinks, conv_dw_w=m_conv_dw_w, conv_dw_b=m_conv_dw_b, conv_norm_gain=m_conv_norm_gain,
             conv_norm_bias=m_conv_norm_bias, w_out=m_w_out, b_out=m_b_out, ffn_norm_gain=m_ffn_norm_gain, w_up=m_w_up,
             ffn_dw_w=m_ffn_dw_w, ffn_dw_b=m_ffn_dw_b, w_down=m_w_down)
    v = dict(mix_norm_gain=v_mix_norm_gain, w_in=v_w_in, b_in=v_b_in, q_norm_gain=v_q_norm_gain, k_norm_gain=v_k_norm_gain,
             attn_sinks=v_attn_sinks, conv_dw_w=v_conv_dw_w, conv_dw_b=v_conv_dw_b, conv_norm_gain=v_conv_norm_gain,
             conv_norm_bias=v_conv_norm_bias, w_out=v_w_out, b_out=v_b_out, ffn_norm_gain=v_ffn_norm_gain, w_up=v_w_up,
             ffn_dw_w=v_ffn_dw_w, ffn_dw_b=v_ffn_dw_b, w_down=v_w_down)
    s = x.shape[1]

    wi8, cw8, fw8 = _allgather([w_in.T, conv_dw_w, ffn_dw_w], [BF16, F32, F32])
    lane_pad = ((0, 0), (0, 0), (0, FF_PADDED - FF_CHUNK))
    p = {
        "g_mix": mix_norm_gain.reshape(1, -1), "w_in_t": wi8.reshape(QKV_COLS + CIN_COLS, D_MODEL),
        "b_qkv": b_in[:QKV_COLS].reshape(1, -1), "b_cin": b_in[QKV_COLS:].reshape(1, -1),
        "gq2": jnp.tile(q_norm_gain, 2).reshape(1, -1), "gk2": jnp.tile(k_norm_gain, 2).reshape(1, -1), "sinks": attn_sinks,
        "cw8": jnp.repeat(cw8.transpose(1, 0, 2).reshape(CONV_KERNEL, CONV_WIDTH), SUBLANES, axis=0),
        "cb": conv_dw_b.reshape(1, -1), "cgain": conv_norm_gain.reshape(1, -1), "cbias": conv_norm_bias.reshape(1, -1),
        "b_out": b_out.reshape(1, -1), "g_ffn": ffn_norm_gain.reshape(1, -1),
        "fw": jnp.pad(fw8, lane_pad), "fb": jnp.pad(ffn_dw_b.reshape(N_DEV, 1, FF_CHUNK), lane_pad),
    }

    saved, (wu8,), (wo8, wd8) = _token_mixing(x[0], p, [w_up.T], [w_out, w_down])
    p.update(w_out=wo8.reshape(D_MODEL, D_MODEL), w_up=wu8, w_down=wd8.reshape(N_FF_PAIRS, FF_CHUNK, D_MODEL))
    loss, grad_x, big, small = _rest_of_step(x[0], loss_target[0], p, saved, True)

    g = dict(big)
    g["w_in"], gpack = _final_exchange(big["w_in"], _pack_small(small))

    delta, new_m, new_v = {}, {}, {}
    for n in BIG:
        if n in ("w_in", "w_up"):
            outs = _adamw(w[n].T, g[n], m[n].T, v[n].T, "adamw_" + n)
            g[n], delta[n], new_m[n], new_v[n] = g[n].T, *[o.T for o in outs]
        else:
            delta[n], new_m[n], new_v[n] = _adamw(w[n], g[n], m[n], v[n], "adamw_" + n)

    def view(a):
        return a if a.ndim == 2 else (a.reshape(-1, LANES) if a.size % LANES == 0 else a.reshape(1, -1))

    small_out = _adamw_small(gpack, *[{n: view(d[n]) for n in SMALL} for d in (w, m, v)])
    for n in SMALL:
        g[n], delta[n], new_m[n], new_v[n] = [a.reshape(w[n].shape) for a in small_out[n]]

    total = gpack[R_QKS + 3, 0]
    return (total, grad_x.reshape(1, s, D_MODEL), *[g[n] for n in ORDER], *[delta[n] for n in ORDER],
            *[new_m[n] for n in ORDER], *[new_v[n] for n in ORDER])
```

```python
import functools
import math

import jax
import jax.numpy as jnp
from jax import lax
from jax.experimental import pallas as pl
from jax.experimental.pallas import tpu as pltpu

F32 = jnp.float32
BF16 = jnp.bfloat16

D_MODEL = 1024
HEAD_DIM = 64
N_Q_HEADS = 8
N_KV_HEADS = 2
Q_COLS = 512
KV_COLS = 128
QKV_COLS = Q_COLS + 2 * KV_COLS
CONV_WIDTH = 512
CIN_COLS = 2 * CONV_WIDTH
CONV_KERNEL = 31
CONV_HALO = 32
D_FF = 2816
N_DEV = 8
FF_CHUNK = 2 * D_FF // N_DEV
N_FF_PAIRS = N_DEV // 2
ATT_BLOCK = 128
EPS = 1e-6
NEG_INF = -1e30
SLOPES = [float(2.0 ** (-8.0 * (h + 1.0) / N_Q_HEADS)) for h in range(N_Q_HEADS)]

ADAM_LR = 0.001
ADAM_B1 = 0.9
ADAM_B2 = 0.999
ADAM_EPS = 1e-08
ADAM_WD = 0.01
ADAM_STEP = 10

LANES = 128
SUBLANES = 8
VMEM_LIMIT = 56 * 1024 * 1024
MESH = pl.DeviceIdType.MESH


def _cparams(*sem, **kw):
    return pltpu.CompilerParams(dimension_semantics=sem or None, vmem_limit_bytes=VMEM_LIMIT, **kw)


def _resident(shape):
    nd = len(shape)
    return pl.BlockSpec(shape, lambda *_: (0,) * nd, pipeline_mode=pl.Buffered(1))


def _dot(a, b):
    return jnp.dot(a, b, preferred_element_type=F32)


def _dot_nt(a, b):
    return lax.dot_general(a, b, (((1,), (1,)), ((), ())), preferred_element_type=F32)


def _dot_tn(a, b):
    return lax.dot_general(a, b, (((0,), (0,)), ((), ())), preferred_element_type=F32)


def _sigmoid(x):
    return 1.0 / (1.0 + jnp.exp(-x))


def _lo_mask(shape):
    return lax.broadcasted_iota(jnp.int32, shape, len(shape) - 1) % LANES < HEAD_DIM


def _half_sums(t, lo):
    s_lo = jnp.sum(jnp.where(lo, t, 0.0), axis=-1, keepdims=True)
    s_hi = jnp.sum(jnp.where(lo, 0.0, t), axis=-1, keepdims=True)
    return jnp.where(lo, s_lo, s_hi)


def _head_norm(t, lo):
    r = lax.rsqrt(_half_sums(t * t, lo) * (1.0 / HEAD_DIM) + EPS)
    return t * r, r


def _head_norm_bwd(dn, n, r, lo):
    return r * (dn - n * (_half_sums(dn * n, lo) * (1.0 / HEAD_DIM)))


def _tile(s):
    return min(512, s)


TN_TOKENS = 2048
FF_COLS = ((0, 256), (256, 512), (512, 704))


def _position():
    return lax.axis_index("x"), lax.axis_index("y"), lax.axis_index("c")


def _dev_index(px, py, pc):
    return 4 * px + 2 * py + pc


def _flip(v, bit):
    return 1 - v if bit else v


OTHER_CHIPS = ((1, 0), (0, 1), (1, 1))
N_GATHER_COPIES = 1 + 2 * len(OTHER_CHIPS)


class _Gather:
    def __init__(self, shard_ref, out_ref, cast_buf, send_sems, recv_sems, local_sem):
        self.shard, self.out, self.buf = shard_ref, out_ref, cast_buf
        self.send_sems, self.recv_sems, self.local_sem = send_sems, recv_sems, local_sem
        x, y, c = _position()
        self.c = c
        self.me, self.sibling = (x, y, c), (x, y, 1 - c)
        self.chips = [(_flip(x, fx), _flip(y, fy)) for fx, fy in OTHER_CHIPS]

    def _copy(self, k, block, to, from_buf=False):
        rows = self.out.at[_dev_index(*block)]
        return pltpu.make_async_remote_copy(src_ref=self.buf if from_buf else rows, dst_ref=rows,
                                            send_sem=self.send_sems.at[k], recv_sem=self.recv_sems.at[k],
                                            device_id=to, device_id_type=MESH)

    def _local(self):
        return pltpu.make_async_copy(self.buf, self.out.at[_dev_index(*self.me)], self.local_sem)

    def start(self):
        self.buf[...] = self.shard[...].astype(self.buf.dtype)
        self._local().start()
        for j, chip in enumerate(self.chips):
            self._copy(1 + j, self.me, (*chip, self.c), from_buf=True).start()
        self._copy(0, self.me, self.sibling, from_buf=True).start()

    def forward(self):
        for j, chip in enumerate(self.chips):
            self._copy(1 + j, (*chip, self.c), self.me).wait_recv()
            self._copy(1 + len(self.chips) + j, (*chip, self.c), self.sibling).start()

    def finish(self):
        self._copy(0, self.sibling, self.me).wait_recv()
        for j, chip in enumerate(self.chips):
            self._copy(1 + len(self.chips) + j, (*chip, 1 - self.c), self.me).wait_recv()
        for k in range(N_GATHER_COPIES):
            self._copy(k, self.me, self.sibling).wait_send()
        self._local().wait()


def _gather_specs(shards):
    whole = [pl.BlockSpec(w.shape, lambda *_, nd=w.ndim: (0,) * nd, pipeline_mode=pl.Buffered(1)) for w in shards]
    outs = [pl.BlockSpec(memory_space=pl.ANY) for _ in shards]
    shapes = [jax.ShapeDtypeStruct((N_DEV,) + w.shape, BF16) for w in shards]
    scratch = []
    for w in shards:
        scratch += [pltpu.VMEM(w.shape, BF16), pltpu.SemaphoreType.DMA((N_GATHER_COPIES,)),
                    pltpu.SemaphoreType.DMA((N_GATHER_COPIES,)), pltpu.SemaphoreType.DMA(())]
    return whole, outs, shapes, scratch


def _run_gathers(gathers, step, n_steps):
    @pl.when(step == 0)
    def _():
        for g in gathers:
            g.start()

    @pl.when(step == 3 * n_steps // 4)
    def _():
        for g in gathers:
            g.forward()

    @pl.when(step == n_steps - 1)
    def _():
        for g in gathers:
            g.finish()


class _ReduceScatter:
    def __init__(self, g_ref, out_ref, stage, load_sems, send_a, recv_a, send_b, recv_b, sa_send, sa_recv, sb_send, sb_recv):
        self.g, self.out, self.stage, self.load_sems = g_ref, out_ref, stage, load_sems
        self.send_a, self.recv_a, self.send_b, self.recv_b = send_a, recv_a, send_b, recv_b
        self.sems = (sa_send, sa_recv, sb_send, sb_recv)
        x, y, c = _position()
        self.c, self.sibling = c, (x, y, 1 - c)
        self.chips = [(x, y)] + [(_flip(x, fx), _flip(y, fy)) for fx, fy in OTHER_CHIPS]

    def _copy_a(self, j):
        return pltpu.make_async_remote_copy(src_ref=self.send_a.at[j], dst_ref=self.recv_a.at[j], send_sem=self.sems[0].at[j],
                                            recv_sem=self.sems[1].at[j], device_id=self.sibling, device_id_type=MESH)

    def _copy_b(self, j):
        return pltpu.make_async_remote_copy(src_ref=self.send_b.at[j], dst_ref=self.recv_b.at[j], send_sem=self.sems[2].at[j],
                                            recv_sem=self.sems[3].at[j], device_id=(*self.chips[1 + j], self.c),
                                            device_id_type=MESH)

    def _load(self, j, core):
        return pltpu.make_async_copy(self.g.at[_dev_index(*self.chips[j], core)], self.stage.at[j % 2], self.load_sems.at[j % 2])

    def start(self):
        self._load(0, 1 - self.c).start()
        for j in range(len(self.chips)):
            self._load(j, 1 - self.c).wait()
            if j + 1 < len(self.chips):
                self._load(j + 1, 1 - self.c).start()
            self.send_a[j] = self.stage[j % 2].astype(BF16)
            self._copy_a(j).start()

    def middle(self):
        self._load(0, self.c).start()
        for j in range(len(self.chips)):
            self._load(j, self.c).wait()
            if j + 1 < len(self.chips):
                self._load(j + 1, self.c).start()
            self._copy_a(j).wait_recv()
            part = self.stage[j % 2] + self.recv_a[j].astype(F32)
            if j == 0:
                self.out[...] = part
            else:
                self.send_b[j - 1] = part.astype(BF16)
                self._copy_b(j - 1).start()

    def finish(self):
        for j in range(len(OTHER_CHIPS)):
            self._copy_b(j).wait_recv()
            self.out[...] += self.recv_b[j].astype(F32)
        for j in range(len(self.chips)):
            self._copy_a(j).wait_send()
        for j in range(len(OTHER_CHIPS)):
            self._copy_b(j).wait_send()


N_SCATTER_SCRATCH = 10


def _scatter_specs(g8s):
    na, nb = 1 + len(OTHER_CHIPS), len(OTHER_CHIPS)
    ins = [pl.BlockSpec(memory_space=pl.ANY) for _ in g8s]
    outs = [pl.BlockSpec(g.shape[1:], lambda *_: (0, 0)) for g in g8s]
    shapes = [jax.ShapeDtypeStruct(g.shape[1:], F32) for g in g8s]
    scratch = []
    for g in g8s:
        blk = g.shape[1:]
        scratch += [pltpu.VMEM((2,) + blk, F32), pltpu.SemaphoreType.DMA((2,)), pltpu.VMEM((na,) + blk, BF16), pltpu.VMEM((na,) + blk, BF16),
                    pltpu.VMEM((nb,) + blk, BF16), pltpu.VMEM((nb,) + blk, BF16),
                    pltpu.SemaphoreType.DMA((na,)), pltpu.SemaphoreType.DMA((na,)),
                    pltpu.SemaphoreType.DMA((nb,)), pltpu.SemaphoreType.DMA((nb,))]
    return ins, outs, shapes, scratch


def _run_scatters(scatters, step, n_steps):
    @pl.when(step == 0)
    def _():
        for r in scatters:
            r.start()

    @pl.when(step == min(max(1, n_steps // 4), n_steps - 1))
    def _():
        for r in scatters:
            r.middle()

    @pl.when(step == n_steps - 1)
    def _():
        for r in scatters:
            r.finish()


def _mix_proj(x, g_mix, w_in_t, b_qkv, b_cin):
    s = x.shape[0]
    tm = _tile(s)

    def body(x_ref, g_ref, w_ref, bq_ref, bc_ref, qkv_ref, cin_ref, h1_ref):
        xv = x_ref[...]
        r = lax.rsqrt(jnp.mean(xv * xv, axis=-1, keepdims=True) + EPS)
        h = (xv * r * g_ref[...]).astype(BF16)
        h1_ref[...] = h
        qkv_ref[...] = _dot_nt(h, w_ref[0:QKV_COLS, :]) + bq_ref[...]
        cin_ref[...] = _dot_nt(h, w_ref[QKV_COLS:, :]) + bc_ref[...]

    return pl.pallas_call(
        body, grid=(s // tm,),
        in_specs=[pl.BlockSpec((tm, D_MODEL), lambda i: (i, 0)), _resident((1, D_MODEL)),
                  _resident((QKV_COLS + CIN_COLS, D_MODEL)), _resident((1, QKV_COLS)), _resident((1, CIN_COLS))],
        out_specs=[pl.BlockSpec((tm, QKV_COLS), lambda i: (i, 0)), pl.BlockSpec((tm, CIN_COLS), lambda i: (i, 0)),
                   pl.BlockSpec((tm, D_MODEL), lambda i: (i, 0))],
        out_shape=[jax.ShapeDtypeStruct((s, QKV_COLS), F32), jax.ShapeDtypeStruct((s, CIN_COLS), F32),
                   jax.ShapeDtypeStruct((s, D_MODEL), BF16)],
        compiler_params=_cparams("parallel"), name="mix_proj")(x, g_mix, w_in_t, b_qkv, b_cin)


def _kv_variants(kv_all, gk2, lo):
    k_all = kv_all[:, :LANES]
    v_all = kv_all[:, LANES:]
    kn_pre, rk = _head_norm(k_all, lo)
    kn = kn_pre * gk2
    kr = pltpu.roll(kn, HEAD_DIM, 1)
    vr = pltpu.roll(v_all, HEAD_DIM, 1)
    zero = jnp.zeros_like(kn)
    k_lo = [jnp.where(lo, kn, zero).astype(BF16), jnp.where(lo, kr, zero).astype(BF16)]
    k_hi = [jnp.where(lo, zero, kr).astype(BF16), jnp.where(lo, zero, kn).astype(BF16)]
    v_lo = [jnp.where(lo, v_all, zero).astype(BF16), jnp.where(lo, vr, zero).astype(BF16)]
    v_hi = [jnp.where(lo, zero, vr).astype(BF16), jnp.where(lo, zero, v_all).astype(BF16)]
    return k_lo, k_hi, v_lo, v_hi, kn_pre, rk


def _att_consts(first_tile, b):
    rows = 2 * ATT_BLOCK
    qi = lax.broadcasted_iota(jnp.int32, (rows, 2 * ATT_BLOCK), 0) % ATT_BLOCK
    kj = lax.broadcasted_iota(jnp.int32, (rows, 2 * ATT_BLOCK), 1)
    rel = qi + ATT_BLOCK - kj
    valid = (rel >= 0) & (rel < ATT_BLOCK)
    if b == 0:
        valid = valid & ((kj >= ATT_BLOCK) | jnp.logical_not(first_tile))
    return rel.astype(F32), valid


def _row_const(va, vb):
    top = lax.broadcasted_iota(jnp.int32, (2 * ATT_BLOCK, 1), 0) < ATT_BLOCK
    return jnp.where(top, va, vb)


def _probs(q2, k_op, rel, valid, slope, sink):
    sc = _dot_nt(q2, k_op) * (1.0 / math.sqrt(HEAD_DIM)) - slope * rel
    sc = jnp.where(valid, sc, NEG_INF)
    m = jnp.maximum(jnp.max(sc, axis=-1, keepdims=True), sink)
    p = jnp.exp(sc - m)
    e_sink = jnp.exp(sink - m)
    inv = 1.0 / (jnp.sum(p, axis=-1, keepdims=True) + e_sink)
    return p * inv, e_sink * inv


def _attn_fwd(qkv, gq2, gk2, sinks, shards):
    s = qkv.shape[0]
    tq = _tile(s)
    nb = tq // ATT_BLOCK
    ng = len(shards)
    g_in, g_out, g_shape, g_scratch = _gather_specs(shards)

    def body(q_ref, kv_ref, kvp_ref, gq_ref, gk_ref, sink_ref, *rest):
        out_ref = rest[ng]
        i = pl.program_id(0)
        _run_gathers([_Gather(rest[a], rest[ng + 1 + a], *rest[2 * ng + 1 + 4 * a:2 * ng + 5 + 4 * a]) for a in range(ng)],
                     i, s // tq)
        lo = _lo_mask((1, LANES))
        kv_all = jnp.concatenate([kvp_ref[...], kv_ref[...]], axis=0)
        k_lo, k_hi, v_lo, v_hi, _, _ = _kv_variants(kv_all, gk_ref[...], lo)
        for b in range(nb):
            rel, valid = _att_consts(i == 0, b)
            rows = slice(b * ATT_BLOCK, (b + 1) * ATT_BLOCK)
            keys = slice(b * ATT_BLOCK, (b + 2) * ATT_BLOCK)
            for kvh in range(N_KV_HEADS):
                pairs = (2 * kvh, 2 * kvh + 1)
                q2 = jnp.concatenate([q_ref[rows, p * LANES:(p + 1) * LANES] for p in pairs], axis=0)
                qn, _ = _head_norm(q2, lo)
                q2 = (qn * gq_ref[...]).astype(BF16)
                out = None
                for odd, (k_op, v_op) in enumerate(((k_lo[kvh][keys], v_lo[kvh][keys]), (k_hi[kvh][keys], v_hi[kvh][keys]))):
                    ha, hb = 2 * pairs[0] + odd, 2 * pairs[1] + odd
                    p, _ = _probs(q2, k_op, rel, valid, _row_const(SLOPES[ha], SLOPES[hb]),
                                  _row_const(sink_ref[ha], sink_ref[hb]))
                    o = _dot(p.astype(BF16), v_op)
                    out = o if out is None else out + o
                for n, p in enumerate(pairs):
                    out_ref[rows, p * LANES:(p + 1) * LANES] = out[n * ATT_BLOCK:(n + 1) * ATT_BLOCK].astype(BF16)

    return pl.pallas_call(
        body, grid=(s // tq,),
        in_specs=[pl.BlockSpec((tq, Q_COLS), lambda i: (i, 0)),
                  pl.BlockSpec((tq, 2 * KV_COLS), lambda i: (i, 2)),
                  pl.BlockSpec((ATT_BLOCK, 2 * KV_COLS), lambda i: (jnp.maximum(i * nb - 1, 0), 2)),
                  _resident((1, LANES)), _resident((1, LANES)),
                  pl.BlockSpec(memory_space=pltpu.SMEM)] + g_in,
        out_specs=[pl.BlockSpec((tq, Q_COLS), lambda i: (i, 0))] + g_out,
        out_shape=[jax.ShapeDtypeStruct((s, Q_COLS), BF16)] + g_shape,
        scratch_shapes=g_scratch,
        compiler_params=_cparams("arbitrary"), name="attn_fwd")(qkv, qkv, qkv, gq2, gk2, sinks, *shards)


def _group_stats(c1, lo):
    mu = _half_sums(c1, lo) * (1.0 / HEAD_DIM)
    d = c1 - mu
    rstd = lax.rsqrt(_half_sums(d * d, lo) * (1.0 / HEAD_DIM) + EPS)
    return d * rstd, rstd


def _rows(ref, first_row, n):
    return ref[pl.ds(first_row, n, stride=1), :].reshape(n // SUBLANES, SUBLANES, LANES)


def _conv_fwd(cin, cw8, cb, gain, bias, shards):
    s = cin.shape[0]
    tm = _tile(s)
    rc = 64
    nchunk = CONV_WIDTH // LANES
    lead = CONV_HALO - (CONV_KERNEL - 1)
    ng = len(shards)
    g_in, g_out, g_shape, g_scratch = _gather_specs(shards)

    def body(cin_ref, cw_ref, cb_ref, gain_ref, bias_ref, *rest):
        c3_ref, c1_ref, ext_ref = rest[ng], rest[ng + 1], rest[2 * ng + 2]
        _run_gathers([_Gather(rest[a], rest[ng + 2 + a], *rest[2 * ng + 3 + 4 * a:2 * ng + 7 + 4 * a]) for a in range(ng)],
                     pl.program_id(0), s // tm)

        @pl.when(pl.program_id(0) == 0)
        def _():
            ext_ref[:, 0:CONV_HALO, :] = jnp.zeros((nchunk, CONV_HALO, LANES), F32)

        lo = _lo_mask((1, LANES))
        for cc in range(nchunk):
            cols = slice(cc * LANES, (cc + 1) * LANES)
            gcols = slice(CONV_WIDTH + cc * LANES, CONV_WIDTH + (cc + 1) * LANES)
            ext_ref[cc, CONV_HALO:CONV_HALO + tm, :] = cin_ref[:, cols] * _sigmoid(cin_ref[:, gcols])
            ext = ext_ref.at[cc]
            for r in range(tm // rc):
                rows = slice(r * rc, (r + 1) * rc)
                acc = jnp.zeros((rc // SUBLANES, SUBLANES, LANES), F32)
                for k in range(CONV_KERNEL):
                    acc = acc + cw_ref[k * SUBLANES:(k + 1) * SUBLANES, cols][None] * _rows(ext, r * rc + lead + k, rc)
                c1 = acc.reshape(rc, LANES) + cb_ref[:, cols]
                c1_ref[cc, rows, :] = c1
                nrm, _ = _group_stats(c1, lo)
                c2 = nrm * gain_ref[:, cols] + bias_ref[:, cols]
                c3_ref[rows, cols] = (c2 * _sigmoid(c2)).astype(BF16)
        ext_ref[:, 0:CONV_HALO, :] = ext_ref[:, tm:tm + CONV_HALO, :]

    return pl.pallas_call(
        body, grid=(s // tm,),
        in_specs=[pl.BlockSpec((tm, CIN_COLS), lambda i: (i, 0)), _resident((CONV_KERNEL * SUBLANES, CONV_WIDTH)),
                  _resident((1, CONV_WIDTH)), _resident((1, CONV_WIDTH)), _resident((1, CONV_WIDTH))] + g_in,
        out_specs=[pl.BlockSpec((tm, CONV_WIDTH), lambda i: (i, 0)), pl.BlockSpec((nchunk, tm, LANES), lambda i: (0, i, 0))] + g_out,
        out_shape=[jax.ShapeDtypeStruct((s, CONV_WIDTH), BF16), jax.ShapeDtypeStruct((nchunk, s, LANES), F32)] + g_shape,
        scratch_shapes=[pltpu.VMEM((nchunk, tm + CONV_HALO, LANES), F32)] + g_scratch,
        compiler_params=_cparams("arbitrary"), name="conv_fwd")(cin, cw8, cb, gain, bias, *shards)


def _out_proj(x, attn, c3, wo_a, wo_c, b_out, g_ffn):
    s = x.shape[0]
    tm = _tile(s)

    def body(x_ref, a_ref, c_ref, wa_ref, wc_ref, b_ref, g_ref, x2_ref, h2_ref):
        x2 = x_ref[...] + _dot(a_ref[...], wa_ref[...]) + _dot(c_ref[...], wc_ref[...]) + b_ref[...]
        x2_ref[...] = x2
        r = lax.rsqrt(jnp.mean(x2 * x2, axis=-1, keepdims=True) + EPS)
        h2_ref[...] = (x2 * r * g_ref[...]).astype(BF16)

    return pl.pallas_call(
        body, grid=(s // tm,),
        in_specs=[pl.BlockSpec((tm, D_MODEL), lambda i: (i, 0)), pl.BlockSpec((tm, Q_COLS), lambda i: (i, 0)),
                  pl.BlockSpec((tm, CONV_WIDTH), lambda i: (i, 0)),
                  pl.BlockSpec((Q_COLS, D_MODEL), lambda i: (0, 0), pipeline_mode=pl.Buffered(1)),
                  pl.BlockSpec((CONV_WIDTH, D_MODEL), lambda i: (1, 0), pipeline_mode=pl.Buffered(1)),
                  _resident((1, D_MODEL)), _resident((1, D_MODEL))],
        out_specs=[pl.BlockSpec((tm, D_MODEL), lambda i: (i, 0)), pl.BlockSpec((tm, D_MODEL), lambda i: (i, 0))],
        out_shape=[jax.ShapeDtypeStruct((s, D_MODEL), F32), jax.ShapeDtypeStruct((s, D_MODEL), BF16)],
        compiler_params=_cparams("parallel"), name="out_proj")(x, attn, c3, wo_a, wo_c, b_out, g_ffn)


FF_LANE_CHUNKS = -(-FF_CHUNK // LANES)
FF_PADDED = FF_LANE_CHUNKS * LANES


def _tap(ref, first_row, n):
    return ref[pl.ds(first_row, n, stride=1), :]


def _ffn_fwd(h2, x2, target, w_up, fw, fb, w_down):
    s = h2.shape[0]
    tm = _tile(s)
    hal = SUBLANES
    rc = min(128, tm)

    def body(h_ref, x2_ref, t_ref, wu_ref, fw_ref, fb_ref, wd_ref, up0_ref, gu_ref, act_ref, dy_ref, loss_ref,
             ext_ref, carry_ref, act_buf, y_ref):
        i, ci = pl.program_id(0), pl.program_id(1)

        @pl.when((i == 0) & (ci == 0))
        def _():
            carry_ref[...] = jnp.zeros(carry_ref.shape, F32)
            ext_ref[...] = jnp.zeros(ext_ref.shape, F32)
            act_buf[...] = jnp.zeros(act_buf.shape, BF16)
            loss_ref[...] = jnp.zeros((1, 1), F32)

        @pl.when(ci == 0)
        def _():
            y_ref[...] = x2_ref[...]

        ws = (fw_ref[ci], fw_ref[ci + N_FF_PAIRS])
        bs = (fb_ref[ci], fb_ref[ci + N_FF_PAIRS])
        half_rows = (slice(0, tm // 2), slice(tm // 2, tm))
        n_grp = len(FF_COLS)

        def up_slices(grp):
            lo_c, hi_c = FF_COLS[grp]
            chunks = range(lo_c // LANES, -(-hi_c // LANES))

            def make(half, n, rows):
                def run():
                    c = ci + half * N_FF_PAIRS
                    u0 = _dot_nt(h_ref[rows, :], wu_ref[c, lo_c:hi_c, :])
                    up0_ref[half, 0, rows, lo_c:hi_c] = u0.astype(BF16)
                    if hi_c == FF_CHUNK:
                        up0_ref[half, 0, rows, FF_CHUNK:] = jnp.zeros((u0.shape[0], FF_PADDED - FF_CHUNK), BF16)
                    for j in chunks:
                        w = min(LANES, hi_c - j * LANES)
                        if n == 0:
                            ext_ref[half, j, 0:hal, 0:w] = carry_ref[c, :, j * LANES:j * LANES + w]
                        ext_ref[half, j, hal + rows.start:hal + rows.stop, 0:w] = u0[:, j * LANES - lo_c:j * LANES - lo_c + w]
                    if n == len(half_rows) - 1:
                        carry_ref[c, :, lo_c:hi_c] = u0[u0.shape[0] - hal:, :]
                return run
            return [make(half, n, rows) for half in range(2) for n, rows in enumerate(half_rows)]

        def down_slices(grp):
            lo_c, hi_c = FF_COLS[grp]

            def make(rows):
                def run():
                    y_ref[rows, :] += _dot(act_buf[rows, lo_c:hi_c], wd_ref[ci, lo_c:hi_c, :])
                return run
            return [make(rows) for rows in half_rows]

        def vector_blocks(grp):
            lo_c, hi_c = FF_COLS[grp]
            blocks = []
            for j in range(lo_c // LANES, -(-hi_c // LANES)):
                lanes = slice(j * LANES, (j + 1) * LANES)

                def gate(r, lanes=lanes, j=j):
                    base = r * rc
                    ups = []
                    for half in range(2):
                        e, w = ext_ref.at[half, j], ws[half]
                        ups.append(w[0:1, lanes] * _tap(e, base + hal - 2, rc) + w[1:2, lanes] * _tap(e, base + hal - 1, rc)
                                   + w[2:3, lanes] * _tap(e, base + hal, rc) + bs[half][:, lanes])
                    g, u = ups
                    gu_ref[0, 0, base:base + rc, lanes] = g.astype(BF16)
                    gu_ref[1, 0, base:base + rc, lanes] = u.astype(BF16)
                    act_buf[base:base + rc, lanes] = (g * _sigmoid(g) * u).astype(BF16)

                blocks += [functools.partial(gate, r) for r in range(tm // rc)]

            def finish():
                act_ref[0, :, lo_c:hi_c] = act_buf[:, lo_c:hi_c]
            blocks.append(finish)
            return blocks

        for run in up_slices(0):
            run()
        for grp in range(n_grp):
            matmuls = (up_slices(grp + 1) if grp + 1 < n_grp else []) + (down_slices(grp - 1) if grp > 0 else [])
            blocks = vector_blocks(grp)
            every = max(1, len(blocks) // (len(matmuls) + 1))
            for n, run in enumerate(blocks):
                run()
                if n % every == every - 1 and matmuls:
                    matmuls.pop(0)()
            for run in matmuls:
                run()
        for run in down_slices(n_grp - 1):
            run()

        @pl.when(ci == N_FF_PAIRS - 1)
        def _():
            e = y_ref[...] - t_ref[...]
            dy_ref[...] = e * (1.0 / D_MODEL)
            loss_ref[...] += (0.5 / D_MODEL) * jnp.sum(e * e).reshape(1, 1)

    tok = lambda i, ci: (i, 0)
    return pl.pallas_call(
        body, grid=(s // tm, N_FF_PAIRS),
        in_specs=[pl.BlockSpec((tm, D_MODEL), tok), pl.BlockSpec((tm, D_MODEL), tok), pl.BlockSpec((tm, D_MODEL), tok),
                  _resident((N_DEV, FF_CHUNK, D_MODEL)), _resident((N_DEV, 3, FF_PADDED)), _resident((N_DEV, 1, FF_PADDED)),
                  _resident((N_FF_PAIRS, FF_CHUNK, D_MODEL))],
        out_specs=[pl.BlockSpec((2, 1, tm, FF_PADDED), lambda i, ci: (0, ci, i, 0)),
                   pl.BlockSpec((2, 1, tm, FF_PADDED), lambda i, ci: (0, ci, i, 0)),
                   pl.BlockSpec((1, tm, FF_CHUNK), lambda i, ci: (ci, i, 0)),
                   pl.BlockSpec((tm, D_MODEL), tok), pl.BlockSpec((1, 1), lambda i, ci: (0, 0))],
        out_shape=[jax.ShapeDtypeStruct((2, N_FF_PAIRS, s, FF_PADDED), BF16), jax.ShapeDtypeStruct((2, N_FF_PAIRS, s, FF_PADDED), BF16),
                   jax.ShapeDtypeStruct((N_FF_PAIRS, s, FF_CHUNK), BF16), jax.ShapeDtypeStruct((s, D_MODEL), F32),
                   jax.ShapeDtypeStruct((1, 1), F32)],
        scratch_shapes=[pltpu.VMEM((2, FF_LANE_CHUNKS, tm + hal, LANES), F32), pltpu.VMEM((N_DEV, hal, FF_CHUNK), F32),
                        pltpu.VMEM((tm, FF_PADDED), BF16), pltpu.VMEM((tm, D_MODEL), F32)],
        compiler_params=_cparams("arbitrary", "arbitrary"), name="ffn_fwd")(h2, x2, target, w_up, fw, fb, w_down)


def _ffn_bwd(dy, up0, gu, w_up, fw, w_down):
    s = dy.shape[0]
    tm = _tile(s)
    nt = s // tm
    nxt = SUBLANES
    rc = min(128, tm)

    def body(dy_ref, up0_ref, gu_ref, wu_ref, fw_ref, wd_ref,
             dup0_ref, dh2_ref, dfw_ref, dfb_ref, dext_ref, carry_ref, dact_buf, dup0_buf):
        i, ci = pl.program_id(0), pl.program_id(1)

        @pl.when((i == 0) & (ci == 0))
        def _():
            for ref in (carry_ref, dfw_ref, dfb_ref, dext_ref, dact_buf):
                ref[...] = jnp.zeros(ref.shape, F32)
            dup0_buf[...] = jnp.zeros(dup0_buf.shape, BF16)

        @pl.when(ci == 0)
        def _():
            dh2_ref[...] = jnp.zeros(dh2_ref.shape, F32)

        ws = (fw_ref[ci], fw_ref[ci + N_FF_PAIRS])
        fold = lambda v: jnp.sum(v.reshape(rc // SUBLANES, SUBLANES, LANES), axis=0)
        half_rows = (slice(0, tm // 2), slice(tm // 2, tm))
        n_grp = len(FF_COLS)

        def dact_slices(grp):
            lo_c, hi_c = FF_COLS[grp]

            def make(rows):
                def run():
                    dact_buf[rows, lo_c:hi_c] = _dot_nt(dy_ref[rows, :].astype(BF16), wd_ref[ci, lo_c:hi_c, :])
                return run
            return [make(rows) for rows in half_rows]

        def dh2_slices(grp):
            lo_c, hi_c = FF_COLS[grp]

            def make(half, rows):
                def run():
                    c = ci + half * N_FF_PAIRS
                    dh2_ref[rows, :] += _dot(dup0_buf[half, rows, lo_c:hi_c], wu_ref[c, lo_c:hi_c, :])
                return run
            return [make(half, rows) for half in range(2) for rows in half_rows]

        def vector_blocks(grp):
            lo_c, hi_c = FF_COLS[grp]
            chunks = range(lo_c // LANES, -(-hi_c // LANES))
            blocks = []

            def stage():
                for half in range(2):
                    c = ci + half * N_FF_PAIRS
                    for j in chunks:
                        dext_ref[half, j, tm:tm + nxt, :] = carry_ref[c, :, j * LANES:(j + 1) * LANES]
            blocks.append(stage)
            for j in chunks:
                lanes = slice(j * LANES, (j + 1) * LANES)
                acc = [jnp.zeros((SUBLANES, LANES), F32)] * 8

                def grads(r, lanes=lanes, j=j, acc=acc):
                    base = r * rc
                    g = gu_ref[0, 0, base:base + rc, lanes].astype(F32)
                    u = gu_ref[1, 0, base:base + rc, lanes].astype(F32)
                    sg = _sigmoid(g)
                    silu = g * sg
                    dact = dact_buf[base:base + rc, lanes]
                    ds = (dact * u * (sg + silu - silu * sg), dact * silu)
                    for half in range(2):
                        dext_ref[half, j, base:base + rc, :] = ds[half]
                        acc[4 * half] = acc[4 * half] + fold(ds[half])

                def conv_back(r, lanes=lanes, j=j, acc=acc):
                    base = r * rc
                    for half in range(2):
                        d, w = dext_ref.at[half, j], ws[half]
                        taps = [_tap(d, base + k, rc) for k in range(3)]
                        dup0 = w[2:3, lanes] * taps[0] + w[1:2, lanes] * taps[1] + w[0:1, lanes] * taps[2]
                        dup0_buf[half, base:base + rc, lanes] = dup0.astype(BF16)
                        u0 = up0_ref[half, 0, base:base + rc, lanes].astype(F32)
                        for k in range(3):
                            acc[4 * half + 1 + k] = acc[4 * half + 1 + k] + fold(taps[2 - k] * u0)

                def sums(lanes=lanes, j=j, acc=acc):
                    for half in range(2):
                        c = ci + half * N_FF_PAIRS
                        carry_ref[c, :, lanes] = dext_ref[half, j, 0:nxt, :]
                        dfb_ref[c, :, lanes] += jnp.sum(acc[4 * half], axis=0, keepdims=True)
                        dfw_ref[c, :, lanes] += jnp.concatenate(
                            [jnp.sum(acc[4 * half + 1 + k], axis=0, keepdims=True) for k in range(3)], axis=0)

                blocks += [functools.partial(grads, r) for r in range(tm // rc)]
                blocks += [functools.partial(conv_back, r) for r in range(tm // rc)] + [sums]

            def finish():
                for half in range(2):
                    dup0_ref[half, 0, :, lo_c:hi_c] = dup0_buf[half, :, lo_c:hi_c]
            blocks.append(finish)
            return blocks

        for run in dact_slices(0):
            run()
        for grp in range(n_grp):
            matmuls = (dact_slices(grp + 1) if grp + 1 < n_grp else []) + (dh2_slices(grp - 1) if grp > 0 else [])
            blocks = vector_blocks(grp)
            every = max(1, len(blocks) // (len(matmuls) + 1))
            for n, run in enumerate(blocks):
                run()
                if n % every == every - 1 and matmuls:
                    matmuls.pop(0)()
            for run in matmuls:
                run()
        for run in dh2_slices(n_grp - 1):
            run()

    tok = lambda i, ci: (nt - 1 - i, 0)
    acc = lambda shape: pl.BlockSpec(shape, lambda i, ci: (0,) * len(shape))
    saved = pl.BlockSpec((2, 1, tm, FF_PADDED), lambda i, ci: (0, ci, nt - 1 - i, 0))
    return pl.pallas_call(
        body, grid=(nt, N_FF_PAIRS),
        in_specs=[pl.BlockSpec((tm, D_MODEL), tok), saved, saved,
                  _resident((N_DEV, FF_CHUNK, D_MODEL)), _resident((N_DEV, 3, FF_PADDED)),
                  _resident((N_FF_PAIRS, FF_CHUNK, D_MODEL))],
        out_specs=[pl.BlockSpec((2, 1, tm, FF_CHUNK), lambda i, ci: (0, ci, nt - 1 - i, 0)),
                   pl.BlockSpec((tm, D_MODEL), tok), acc((N_DEV, 3, FF_PADDED)), acc((N_DEV, 1, FF_PADDED))],
        out_shape=[jax.ShapeDtypeStruct((2, N_FF_PAIRS, s, FF_CHUNK), BF16), jax.ShapeDtypeStruct((s, D_MODEL), F32),
                   jax.ShapeDtypeStruct((N_DEV, 3, FF_PADDED), F32), jax.ShapeDtypeStruct((N_DEV, 1, FF_PADDED), F32)],
        scratch_shapes=[pltpu.VMEM((2, FF_LANE_CHUNKS, tm + nxt, LANES), F32), pltpu.VMEM((N_DEV, nxt, FF_PADDED), F32),
                        pltpu.VMEM((tm, FF_PADDED), F32), pltpu.VMEM((2, tm, FF_PADDED), BF16)],
        compiler_params=_cparams("arbitrary", "arbitrary"), name="ffn_bwd")(dy, up0, gu, w_up, fw, w_down)


def _ffn_norm_bwd(dh2, dy, x2, g_ffn, w_out):
    s = dy.shape[0]
    tm = _tile(s)

    def body(dh_ref, dy_ref, x2_ref, g_ref, wo_ref, dx2_ref, dmix_ref, dg_ref, dbo_ref):
        @pl.when(pl.program_id(0) == 0)
        def _():
            dg_ref[...] = jnp.zeros(dg_ref.shape, F32)
            dbo_ref[...] = jnp.zeros(dbo_ref.shape, F32)

        x2v = x2_ref[...]
        r = lax.rsqrt(jnp.mean(x2v * x2v, axis=-1, keepdims=True) + EPS)
        n2 = x2v * r
        dh2 = dh_ref[...]
        dg_ref[...] += jnp.sum(dh2 * n2, axis=0, keepdims=True)
        dn = dh2 * g_ref[...]
        dx2 = dy_ref[...] + r * (dn - n2 * jnp.mean(dn * n2, axis=-1, keepdims=True))
        dx2_ref[...] = dx2
        dbo_ref[...] += jnp.sum(dx2, axis=0, keepdims=True)
        dmix_ref[...] = _dot_nt(dx2.astype(BF16), wo_ref[...])

    tok = pl.BlockSpec((tm, D_MODEL), lambda i: (i, 0))
    vec = pl.BlockSpec((1, D_MODEL), lambda i: (0, 0))
    return pl.pallas_call(
        body, grid=(s // tm,),
        in_specs=[tok, tok, tok, _resident((1, D_MODEL)), _resident((D_MODEL, D_MODEL))],
        out_specs=[tok, tok, vec, vec],
        out_shape=[jax.ShapeDtypeStruct((s, D_MODEL), F32), jax.ShapeDtypeStruct((s, D_MODEL), F32),
                   jax.ShapeDtypeStruct((1, D_MODEL), F32), jax.ShapeDtypeStruct((1, D_MODEL), F32)],
        compiler_params=_cparams("arbitrary"), name="ffn_norm_bwd")(dh2, dy, x2, g_ffn, w_out)


def _conv_bwd(dmixed, c1, cin, cw8, gain, bias, g8s):
    ns = len(g8s)
    s_in, s_out, s_shape, s_scratch = _scatter_specs(g8s)
    s = cin.shape[0]
    tm = _tile(s)
    nt = s // tm
    rc = 64
    rn = min(256, tm)
    hal = CONV_HALO
    nchunk = CONV_WIDTH // LANES

    def body(dc3_ref, dc3n_ref, c1_ref, c1n_ref, cin_ref, cw_ref, gain_ref, bias_ref, *rest):
        dcin_ref, dcw_ref, dcb_ref, dgain_ref, dbias_ref, dbcin_ref = rest[ns:ns + 6]
        dc1_ext, dcw8 = rest[2 * ns + 6:2 * ns + 8]
        i = pl.program_id(0)
        first, last = i == 0, i == nt - 1
        own = rest[2 * ns + 8:]
        _run_scatters([_ReduceScatter(rest[a], rest[ns + 6 + a], *own[N_SCATTER_SCRATCH * a:N_SCATTER_SCRATCH * (a + 1)])
                       for a in range(ns)], i, nt)

        @pl.when(first)
        def _():
            for ref in (dcw8, dcb_ref, dgain_ref, dbias_ref, dbcin_ref):
                ref[...] = jnp.zeros(ref.shape, F32)

        lo = _lo_mask((1, LANES))

        def norm_bwd(dc3, c1v, cols):
            nrm, rstd = _group_stats(c1v, lo)
            c2 = nrm * gain_ref[:, cols] + bias_ref[:, cols]
            sg = _sigmoid(c2)
            dc2 = dc3 * (sg * (1.0 + c2 * (1.0 - sg)))
            dn = dc2 * gain_ref[:, cols]
            inv = 1.0 / HEAD_DIM
            dc1 = rstd * (dn - _half_sums(dn, lo) * inv - nrm * (_half_sums(dn * nrm, lo) * inv))
            return dc1, dc2, nrm

        def row_sum(v):
            return jnp.sum(v, axis=0, keepdims=True)

        for cc in range(nchunk):
            cols = slice(cc * LANES, (cc + 1) * LANES)
            gcols = slice(CONV_WIDTH + cc * LANES, CONV_WIDTH + (cc + 1) * LANES)
            d1e = dc1_ext.at[cc]
            dc1n, _, _ = norm_bwd(dc3n_ref[:, cols], c1n_ref[cc], cols)
            d1e[tm:tm + hal, :] = jnp.where(last, 0.0, dc1n)

            for r in range(tm // rn):
                rows = slice(r * rn, (r + 1) * rn)
                dc1, dc2, nrm = norm_bwd(dc3_ref[rows, cols], c1_ref[cc, rows, :], cols)
                d1e[rows, :] = dc1
                dgain_ref[:, cols] += row_sum(dc2 * nrm)
                dbias_ref[:, cols] += row_sum(dc2)
                dcb_ref[:, cols] += row_sum(dc1)
            zero = jnp.zeros((1, LANES), F32)

            def taps(r, sums):
                rows = pl.ds(pl.multiple_of(r * rc, rc), rc)
                a = cin_ref[rows, cols]
                sg = _sigmoid(cin_ref[rows, gcols])
                c0 = (a * sg).reshape(rc // SUBLANES, SUBLANES, LANES)
                dc0 = jnp.zeros((rc // SUBLANES, SUBLANES, LANES), F32)
                for k in range(CONV_KERNEL):
                    krows = slice(k * SUBLANES, (k + 1) * SUBLANES)
                    shifted = _rows(d1e, r * rc + CONV_KERNEL - 1 - k, rc)
                    dc0 = dc0 + cw_ref[krows, cols][None] * shifted
                    dcw8[krows, cols] += jnp.sum(shifted * c0, axis=0)
                dc0 = dc0.reshape(rc, LANES)
                da = dc0 * sg
                dgate = dc0 * a * sg * (1.0 - sg)
                dcin_ref[rows, cols] = da.astype(BF16)
                dcin_ref[rows, gcols] = dgate.astype(BF16)
                return sums[0] + row_sum(da), sums[1] + row_sum(dgate)

            sums = lax.fori_loop(0, tm // rc, taps, (zero, zero))
            dbcin_ref[:, cols] += sums[0]
            dbcin_ref[:, gcols] += sums[1]

        @pl.when(last)
        def _():
            for k in range(CONV_KERNEL):
                dcw_ref[k:k + 1, :] = jnp.sum(dcw8[k * SUBLANES:(k + 1) * SUBLANES, :], axis=0, keepdims=True)

    nh = tm // hal
    acc = lambda shape: pl.BlockSpec(shape, lambda i: (0,) * len(shape))
    return pl.pallas_call(
        body, grid=(nt,),
        in_specs=[pl.BlockSpec((tm, CONV_WIDTH), lambda i: (i, 1)),
                  pl.BlockSpec((hal, CONV_WIDTH), lambda i: (jnp.minimum((i + 1) * nh, s // hal - 1), 1)),
                  pl.BlockSpec((nchunk, tm, LANES), lambda i: (0, i, 0)),
                  pl.BlockSpec((nchunk, hal, LANES), lambda i: (0, jnp.minimum((i + 1) * nh, s // hal - 1), 0)),
                  pl.BlockSpec((tm, CIN_COLS), lambda i: (i, 0)),
                  _resident((CONV_KERNEL * SUBLANES, CONV_WIDTH)), _resident((1, CONV_WIDTH)), _resident((1, CONV_WIDTH))] + s_in,
        out_specs=[pl.BlockSpec((tm, CIN_COLS), lambda i: (i, 0)), acc((CONV_KERNEL, CONV_WIDTH)), acc((1, CONV_WIDTH)),
                   acc((1, CONV_WIDTH)), acc((1, CONV_WIDTH)), acc((1, CIN_COLS))] + s_out,
        out_shape=[jax.ShapeDtypeStruct((s, CIN_COLS), BF16), jax.ShapeDtypeStruct((CONV_KERNEL, CONV_WIDTH), F32),
                   jax.ShapeDtypeStruct((1, CONV_WIDTH), F32), jax.ShapeDtypeStruct((1, CONV_WIDTH), F32),
                   jax.ShapeDtypeStruct((1, CONV_WIDTH), F32), jax.ShapeDtypeStruct((1, CIN_COLS), F32)] + s_shape,
        scratch_shapes=[pltpu.VMEM((nchunk, tm + hal, LANES), F32),
                        pltpu.VMEM((CONV_KERNEL * SUBLANES, CONV_WIDTH), F32)] + s_scratch,
        compiler_params=_cparams("arbitrary"), name="conv_bwd")(dmixed, dmixed, c1, c1, cin, cw8, gain, bias, *g8s)


def _attn_bwd(qkv, dmixed, gq2, gk2, sinks, g8s):
    ns = len(g8s)
    s_in, s_out, s_shape, s_scratch = _scatter_specs(g8s)
    s = qkv.shape[0]
    tq = _tile(s)
    nb = tq // ATT_BLOCK
    nt = s // tq

    def body(q_ref, kv_ref, kvp_ref, do_ref, gq_ref, gk_ref, sink_ref, *rest):
        dqkv_ref, dgq_ref, dgk_ref, dsink_ref, dbqkv_ref = rest[ns:ns + 5]
        dk_acc, dv_acc, carry_k, carry_v = rest[2 * ns + 5:2 * ns + 9]
        i = pl.program_id(0)
        t = nt - 1 - i
        own = rest[2 * ns + 9:]
        _run_scatters([_ReduceScatter(rest[a], rest[ns + 5 + a], *own[N_SCATTER_SCRATCH * a:N_SCATTER_SCRATCH * (a + 1)])
                       for a in range(ns)], i, nt)

        @pl.when(i == 0)
        def _():
            for ref in (carry_k, carry_v, dgq_ref, dgk_ref, dsink_ref, dbqkv_ref):
                ref[...] = jnp.zeros(ref.shape, F32)

        lo = _lo_mask((1, LANES))
        lane_id = lax.broadcasted_iota(jnp.int32, (1, LANES), 1)
        kv_all = jnp.concatenate([kvp_ref[...], kv_ref[...]], axis=0)
        k_lo, k_hi, v_lo, v_hi, kn_pre, rk = _kv_variants(kv_all, gk_ref[...], lo)
        for acc_ref, carry in ((dk_acc, carry_k), (dv_acc, carry_v)):
            acc_ref[:, 0:tq, :] = jnp.zeros((N_KV_HEADS, tq, LANES), F32)
            acc_ref[:, tq:tq + ATT_BLOCK, :] = carry[...]
        dsink = jnp.zeros((1, LANES), F32)
        dgq = jnp.zeros((1, LANES), F32)
        gq = gq_ref[...]
        for b in range(nb):
            rel, valid = _att_consts(t == 0, b)
            rows = slice(b * ATT_BLOCK, (b + 1) * ATT_BLOCK)
            keys = slice(b * ATT_BLOCK, (b + 2) * ATT_BLOCK)
            for kvh in range(N_KV_HEADS):
                pairs = (2 * kvh, 2 * kvh + 1)
                q_raw = jnp.concatenate([q_ref[rows, p * LANES:(p + 1) * LANES] for p in pairs], axis=0)
                qn_pre, rq = _head_norm(q_raw, lo)
                q2 = (qn_pre * gq).astype(BF16)
                do2 = jnp.concatenate([do_ref[rows, p * LANES:(p + 1) * LANES] for p in pairs], axis=0).astype(BF16)
                dq2 = jnp.zeros((2 * ATT_BLOCK, LANES), F32)
                for odd, (k_op, v_op) in enumerate(((k_lo[kvh][keys], v_lo[kvh][keys]), (k_hi[kvh][keys], v_hi[kvh][keys]))):
                    ha, hb = 2 * pairs[0] + odd, 2 * pairs[1] + odd
                    p, p_sink = _probs(q2, k_op, rel, valid, _row_const(SLOPES[ha], SLOPES[hb]),
                                       _row_const(sink_ref[ha], sink_ref[hb]))
                    dp = _dot_nt(do2, v_op)
                    delta = jnp.sum(p * dp, axis=-1, keepdims=True)
                    ds = (p * (dp - delta) * (1.0 / math.sqrt(HEAD_DIM))).astype(BF16)
                    dsk = p_sink * delta
                    dsink = dsink - jnp.where(lane_id == ha, jnp.sum(dsk[0:ATT_BLOCK]), 0.0) \
                        - jnp.where(lane_id == hb, jnp.sum(dsk[ATT_BLOCK:]), 0.0)
                    dq2 = dq2 + _dot(ds, k_op)
                    half = lo if odd == 0 else jnp.logical_not(lo)
                    dk_acc[kvh, keys, :] += jnp.where(half, _dot_tn(ds, q2), 0.0)
                    dv_acc[kvh, keys, :] += jnp.where(half, _dot_tn(p.astype(BF16), do2), 0.0)
                dgq = dgq + jnp.sum(dq2 * qn_pre, axis=0, keepdims=True)
                dq_raw = _head_norm_bwd(dq2 * gq, qn_pre, rq, lo)
                for n, p_ in enumerate(pairs):
                    blk = dq_raw[n * ATT_BLOCK:(n + 1) * ATT_BLOCK]
                    dqkv_ref[rows, p_ * LANES:(p_ + 1) * LANES] = blk.astype(BF16)
                    dbqkv_ref[:, p_ * LANES:(p_ + 1) * LANES] += jnp.sum(blk, axis=0, keepdims=True)
        carry_k[...] = dk_acc[:, 0:ATT_BLOCK, :]
        carry_v[...] = dv_acc[:, 0:ATT_BLOCK, :]

        def fold(acc_ref):
            both = []
            for kvh in range(N_KV_HEADS):
                a = acc_ref[kvh, ATT_BLOCK:ATT_BLOCK + tq, :]
                both.append(a + pltpu.roll(a, HEAD_DIM, 1))
            return jnp.where(lo, both[0], both[1])

        dkn = fold(dk_acc)
        dv = fold(dv_acc)
        kn_c, rk_c = kn_pre[ATT_BLOCK:], rk[ATT_BLOCK:]
        dgk_ref[...] += jnp.sum(dkn * kn_c, axis=0, keepdims=True)
        dk_raw = _head_norm_bwd(dkn * gk_ref[...], kn_c, rk_c, lo)
        dqkv_ref[:, Q_COLS:Q_COLS + KV_COLS] = dk_raw.astype(BF16)
        dqkv_ref[:, Q_COLS + KV_COLS:] = dv.astype(BF16)
        dbqkv_ref[:, Q_COLS:Q_COLS + KV_COLS] += jnp.sum(dk_raw, axis=0, keepdims=True)
        dbqkv_ref[:, Q_COLS + KV_COLS:] += jnp.sum(dv, axis=0, keepdims=True)
        dgq_ref[...] += dgq
        dsink_ref[...] += dsink

        @pl.when(i == nt - 1)
        def _():
            for ref in (dgq_ref, dgk_ref):
                v = ref[...]
                ref[...] = v + pltpu.roll(v, HEAD_DIM, 1)

    acc = lambda shape: pl.BlockSpec(shape, lambda i: (0,) * len(shape))
    return pl.pallas_call(
        body, grid=(nt,),
        in_specs=[pl.BlockSpec((tq, Q_COLS), lambda i: (nt - 1 - i, 0)),
                  pl.BlockSpec((tq, 2 * KV_COLS), lambda i: (nt - 1 - i, 2)),
                  pl.BlockSpec((ATT_BLOCK, 2 * KV_COLS), lambda i: (jnp.maximum((nt - 1 - i) * nb - 1, 0), 2)),
                  pl.BlockSpec((tq, Q_COLS), lambda i: (nt - 1 - i, 0)),
                  _resident((1, LANES)), _resident((1, LANES)), pl.BlockSpec(memory_space=pltpu.SMEM)] + s_in,
        out_specs=[pl.BlockSpec((tq, QKV_COLS), lambda i: (nt - 1 - i, 0)), acc((1, LANES)), acc((1, LANES)),
                   acc((1, LANES)), acc((1, QKV_COLS))] + s_out,
        out_shape=[jax.ShapeDtypeStruct((s, QKV_COLS), BF16), jax.ShapeDtypeStruct((1, LANES), F32),
                   jax.ShapeDtypeStruct((1, LANES), F32), jax.ShapeDtypeStruct((1, LANES), F32),
                   jax.ShapeDtypeStruct((1, QKV_COLS), F32)] + s_shape,
        scratch_shapes=[pltpu.VMEM((N_KV_HEADS, tq + ATT_BLOCK, LANES), F32), pltpu.VMEM((N_KV_HEADS, tq + ATT_BLOCK, LANES), F32),
                        pltpu.VMEM((N_KV_HEADS, ATT_BLOCK, LANES), F32), pltpu.VMEM((N_KV_HEADS, ATT_BLOCK, LANES), F32)] + s_scratch,
        compiler_params=_cparams("arbitrary"), name="attn_bwd")(qkv, qkv, qkv, dmixed, gq2, gk2, sinks, *g8s)


def _in_bwd(dqkv, dcin, w_in_t, x, dx2, g_mix):
    s = x.shape[0]
    tm = _tile(s)

    def body(dq_ref, dc_ref, w_ref, x_ref, dx2_ref, g_ref, gx_ref, dg_ref):
        @pl.when(pl.program_id(0) == 0)
        def _():
            dg_ref[...] = jnp.zeros(dg_ref.shape, F32)

        dh = _dot(dq_ref[...], w_ref[0:QKV_COLS, :]) + _dot(dc_ref[...], w_ref[QKV_COLS:, :])
        xv = x_ref[...]
        r = lax.rsqrt(jnp.mean(xv * xv, axis=-1, keepdims=True) + EPS)
        n = xv * r
        dg_ref[...] += jnp.sum(dh * n, axis=0, keepdims=True)
        dn = dh * g_ref[...]
        gx_ref[...] = dx2_ref[...] + r * (dn - n * jnp.mean(dn * n, axis=-1, keepdims=True))

    return pl.pallas_call(
        body, grid=(s // tm,),
        in_specs=[pl.BlockSpec((tm, QKV_COLS), lambda i: (i, 0)), pl.BlockSpec((tm, CIN_COLS), lambda i: (i, 0)),
                  _resident((QKV_COLS + CIN_COLS, D_MODEL)),
                  pl.BlockSpec((tm, D_MODEL), lambda i: (i, 0)), pl.BlockSpec((tm, D_MODEL), lambda i: (i, 0)),
                  _resident((1, D_MODEL))],
        out_specs=[pl.BlockSpec((tm, D_MODEL), lambda i: (i, 0)), pl.BlockSpec((1, D_MODEL), lambda i: (0, 0))],
        out_shape=[jax.ShapeDtypeStruct((s, D_MODEL), F32), jax.ShapeDtypeStruct((1, D_MODEL), F32)],
        compiler_params=_cparams("arbitrary"), name="in_bwd")(dqkv, dcin, w_in_t, x, dx2, g_mix)


def _tn_matmul(a, b, name):
    ga, s, m = a.shape
    gb, _, n = b.shape
    g = max(ga, gb)
    tk = min(TN_TOKENS, s)

    def body(a_ref, b_ref, o_ref):
        @pl.when(pl.program_id(1) == 0)
        def _():
            o_ref[...] = jnp.zeros(o_ref.shape, F32)

        o_ref[0] += _dot_tn(a_ref[0].astype(BF16), b_ref[0].astype(BF16))

    return pl.pallas_call(
        body, grid=(g, s // tk),
        in_specs=[pl.BlockSpec((1, tk, m), (lambda gi, k: (gi, k, 0)) if ga > 1 else (lambda gi, k: (0, k, 0))),
                  pl.BlockSpec((1, tk, n), (lambda gi, k: (gi, k, 0)) if gb > 1 else (lambda gi, k: (0, k, 0)))],
        out_specs=pl.BlockSpec((1, m, n), lambda gi, k: (gi, 0, 0)),
        out_shape=jax.ShapeDtypeStruct((g, m, n), F32),
        compiler_params=_cparams("parallel", "arbitrary"), name=name)(a, b)


def _allgather(shards, dtypes):
    n = len(shards)
    n_copies = 1 + 2 * len(OTHER_CHIPS)

    def body(*refs):
        ins, outs = refs[:n], refs[n:2 * n]
        send_sems, recv_sems = refs[2 * n:]
        x, y, c = _position()
        me, sibling = (x, y, c), (x, y, 1 - c)
        chips = [(_flip(x, fx), _flip(y, fy)) for fx, fy in OTHER_CHIPS]
        for a in range(n):
            outs[a][_dev_index(*me)] = ins[a][...].astype(dtypes[a])

        def copy(a, k, block, to):
            rows = outs[a].at[_dev_index(*block)]
            return pltpu.make_async_remote_copy(src_ref=rows, dst_ref=rows, send_sem=send_sems.at[a, k],
                                                recv_sem=recv_sems.at[a, k], device_id=to, device_id_type=MESH)

        started = []
        for a in range(n):
            for j, chip in enumerate(chips):
                started.append(copy(a, 1 + j, me, (*chip, c)))
            started.append(copy(a, 0, me, sibling))
        for cp in started:
            cp.start()
        for a in range(n):
            for j, chip in enumerate(chips):
                copy(a, 1 + j, (*chip, c), me).wait_recv()
                fwd = copy(a, 1 + len(chips) + j, (*chip, c), sibling)
                fwd.start()
                started.append(fwd)
        for a in range(n):
            copy(a, 0, sibling, me).wait_recv()
            for j, chip in enumerate(chips):
                copy(a, 1 + len(chips) + j, (*chip, 1 - c), me).wait_recv()
        for cp in started:
            cp.wait_send()

    vmem = pl.BlockSpec(memory_space=pltpu.VMEM)
    return pl.pallas_call(
        body, in_specs=[vmem] * n, out_specs=[vmem] * n,
        out_shape=[jax.ShapeDtypeStruct((N_DEV,) + w.shape, dt) for w, dt in zip(shards, dtypes)],
        scratch_shapes=[pltpu.SemaphoreType.DMA((n, n_copies)), pltpu.SemaphoreType.DMA((n, n_copies))],
        compiler_params=pltpu.CompilerParams(vmem_limit_bytes=VMEM_LIMIT), name="allgather_weights")(*shards)


def _final_exchange(g8, v):
    rows = v.shape[0]
    _, _, s_shape, s_scratch = _scatter_specs([g8])

    def body(g_ref, v_ref, gout_ref, vout_ref, gath, send_sems, recv_sems, *rs_scratch):
        scatter = _ReduceScatter(g_ref, gout_ref, *rs_scratch)
        x, y, c = _position()
        me = _dev_index(x, y, c)
        peers = [(_flip(x, k >> 2 & 1), _flip(y, k >> 1 & 1), _flip(c, k & 1)) for k in range(1, N_DEV)]

        def copy(k, block):
            return pltpu.make_async_remote_copy(src_ref=gath.at[block], dst_ref=gath.at[block], send_sem=send_sems.at[k],
                                                recv_sem=recv_sems.at[k], device_id=peers[k], device_id_type=MESH)

        scatter.start()
        gath[me] = v_ref[...]
        for k in range(N_DEV - 1):
            copy(k, me).start()
        scatter.middle()
        for k in range(N_DEV - 1):
            copy(k, _dev_index(*peers[k])).wait_recv()
        for k in range(N_DEV - 1):
            copy(k, me).wait_send()
        total = gath[0]
        for d in range(1, N_DEV):
            total = total + gath[d]
        vout_ref[...] = total
        scatter.finish()

    vmem = pl.BlockSpec(memory_space=pltpu.VMEM)
    return pl.pallas_call(
        body, in_specs=[pl.BlockSpec(memory_space=pl.ANY), vmem], out_specs=[vmem, vmem],
        out_shape=s_shape + [jax.ShapeDtypeStruct((rows, LANES), F32)],
        scratch_shapes=[pltpu.VMEM((N_DEV, rows, LANES), F32), pltpu.SemaphoreType.DMA((N_DEV - 1,)),
                        pltpu.SemaphoreType.DMA((N_DEV - 1,))] + s_scratch,
        compiler_params=pltpu.CompilerParams(vmem_limit_bytes=VMEM_LIMIT), name="final_exchange")(g8, v)


def _row_tile(r):
    for n in (8, 4, 2):
        if r % (n * SUBLANES) == 0:
            return r // n
    return r


def _adam_math(wv, gv, mv, vv):
    mn = ADAM_B1 * mv + (1.0 - ADAM_B1) * gv
    vn = ADAM_B2 * vv + (1.0 - ADAM_B2) * (gv * gv)
    m_hat = mn / (1.0 - ADAM_B1 ** ADAM_STEP)
    v_hat = vn / (1.0 - ADAM_B2 ** ADAM_STEP)
    return -ADAM_LR * (m_hat / (jnp.sqrt(v_hat) + ADAM_EPS) + ADAM_WD * wv), mn, vn


def _adamw(w, g, m, v, name):
    r, c_ = w.shape
    tr = _row_tile(r)

    def body(w_ref, g_ref, m_ref, v_ref, d_ref, mo_ref, vo_ref):
        d_ref[...], mo_ref[...], vo_ref[...] = _adam_math(w_ref[...], g_ref[...], m_ref[...], v_ref[...])

    spec = pl.BlockSpec((tr, c_), lambda i: (i, 0))
    return pl.pallas_call(
        body, grid=(r // tr,), in_specs=[spec] * 4, out_specs=[spec] * 3,
        out_shape=[jax.ShapeDtypeStruct((r, c_), F32)] * 3,
        compiler_params=_cparams("parallel"), name=name)(w, g, m, v)


FW_ROWS = 24
CW_ROWS = 32
R_FW = 0
R_FB = R_FW + N_DEV * FW_ROWS
R_CW = R_FB + 48
R_BQKV = R_CW + (CONV_WIDTH // LANES) * CW_ROWS
R_BCIN = R_BQKV + 8
R_GMIX = R_BCIN + 8
R_BOUT = R_GMIX + 8
R_GFFN = R_BOUT + 8
R_CB = R_GFFN + 8
R_CGAIN = R_CB + 8
R_CBIAS = R_CGAIN + 8
R_QKS = R_CBIAS + 8
SMALL_ROWS = R_QKS + 8


def _pack_small(raw):
    def rows(a, n):
        a = a.reshape(-1, LANES)
        return jnp.pad(a, ((0, n - a.shape[0]), (0, 0)))

    fw = jnp.pad(raw["dfw"].reshape(N_DEV, -1, LANES), ((0, 0), (0, FW_ROWS - 3 * FF_LANE_CHUNKS), (0, 0)))
    cw = jnp.pad(raw["dcw"].reshape(CONV_KERNEL, -1, LANES).transpose(1, 0, 2), ((0, 0), (0, CW_ROWS - CONV_KERNEL), (0, 0)))
    qks = jnp.concatenate([raw["dgq"], raw["dgk"], raw["dsink"], jnp.pad(raw["loss"], ((0, 0), (0, LANES - 1)))], axis=0)
    return jnp.concatenate([
        fw.reshape(-1, LANES), rows(raw["dfb"][:, 0, :FF_CHUNK], 48), cw.reshape(-1, LANES), rows(raw["dbqkv"], 8),
        rows(raw["dbcin"], 8), rows(raw["dg_mix"], 8), rows(raw["db_out"], 8), rows(raw["dg_ffn"], 8), rows(raw["dcb"], 8),
        rows(raw["dcgain"], 8), rows(raw["dcbias"], 8), rows(qks, 8)], axis=0)


def _adamw_small(gpack, w, m, v):
    n = len(SMALL)
    ix = {name: i for i, name in enumerate(SMALL)}

    def body(g_ref, *refs):
        w_refs, m_refs, v_refs, outs = refs[:n], refs[n:2 * n], refs[2 * n:3 * n], refs[3 * n:]
        d = _dev_index(*_position())

        def step(name, idx, gv):
            i = ix[name]
            delta, mn, vn = _adam_math(w_refs[i][idx], gv, m_refs[i][idx], v_refs[i][idx])
            for ref, val in zip(outs[4 * i:4 * i + 4], (gv, delta, mn, vn)):
                ref[idx] = val

        def whole(name, row, nrows):
            step(name, (slice(None), slice(None)), g_ref[row:row + nrows, :])

        whole("mix_norm_gain", R_GMIX, 8)
        whole("b_out", R_BOUT, 8)
        whole("ffn_norm_gain", R_GFFN, 8)
        whole("conv_dw_b", R_CB, 4)
        whole("conv_norm_gain", R_CGAIN, 4)
        whole("conv_norm_bias", R_CBIAS, 4)
        whole("ffn_dw_b", R_FB, 2 * D_FF // LANES)
        nq = QKV_COLS // LANES
        step("b_in", (slice(0, nq), slice(None)), g_ref[R_BQKV:R_BQKV + nq, :])
        step("b_in", (slice(nq, nq + CIN_COLS // LANES), slice(None)), g_ref[R_BCIN:R_BCIN + CIN_COLS // LANES, :])
        step("q_norm_gain", (slice(None), slice(None)), g_ref[R_QKS:R_QKS + 1, 0:HEAD_DIM])
        step("k_norm_gain", (slice(None), slice(None)), g_ref[R_QKS + 1:R_QKS + 2, 0:HEAD_DIM])
        step("attn_sinks", (slice(None), slice(None)), g_ref[R_QKS + 2:R_QKS + 3, 0:N_Q_HEADS])
        blk = g_ref[pl.ds(pl.multiple_of(R_CW + CW_ROWS * lax.shift_right_logical(d, 1), SUBLANES), CW_ROWS), :]
        blk = jnp.where((d & 1) == 1, pltpu.roll(blk, HEAD_DIM, 1), blk)
        step("conv_dw_w", (slice(None), slice(None)), blk[0:CONV_KERNEL, 0:CONV_WIDTH // N_DEV])
        blk = g_ref[pl.ds(pl.multiple_of(R_FW + FW_ROWS * d, SUBLANES), FW_ROWS), :]
        for k in range(3):
            for j in range(FF_LANE_CHUNKS):
                wd = min(LANES, FF_CHUNK - j * LANES)
                row = k * FF_LANE_CHUNKS + j
                step("ffn_dw_w", (slice(k, k + 1), slice(j * LANES, j * LANES + wd)), blk[row:row + 1, 0:wd])

    vmem = pl.BlockSpec(memory_space=pltpu.VMEM)
    args = [gpack] + [d[name] for d in (w, m, v) for name in SMALL]
    outs = pl.pallas_call(
        body, in_specs=[vmem] * len(args), out_specs=[vmem] * (4 * n),
        out_shape=[jax.ShapeDtypeStruct(w[name].shape, F32) for name in SMALL for _ in range(4)],
        compiler_params=pltpu.CompilerParams(vmem_limit_bytes=VMEM_LIMIT), name="adamw_small")(*args)
    return {name: outs[4 * i:4 * i + 4] for i, name in enumerate(SMALL)}


def _token_mixing(x, p, attn_shards, conv_shards):
    qkv, cin, h1 = _mix_proj(x, p["g_mix"], p["w_in_t"], p["b_qkv"], p["b_cin"])
    attn, *from_attn = _attn_fwd(qkv, p["gq2"], p["gk2"], p["sinks"], attn_shards)
    c3, c1, *from_conv = _conv_fwd(cin, p["cw8"], p["cb"], p["cgain"], p["cbias"], conv_shards)
    return (qkv, cin, h1, attn, c3, c1), from_attn, from_conv


def _rest_of_step(x, target, p, saved, scatter):
    s = x.shape[0]
    qkv, cin, h1, attn, c3, c1 = saved
    cw8, w_out, w_up, w_down = p["cw8"], p["w_out"], p["w_up"], p["w_down"]
    x2, h2 = _out_proj(x, attn, c3, w_out, w_out, p["b_out"], p["g_ffn"])
    fw, fb = p["fw"], p["fb"]
    up0, gu, act, dy, loss = _ffn_fwd(h2, x2, target, w_up, fw, fb, w_down)
    dup0, dh2, dfw, dfb = _ffn_bwd(dy, up0, gu, w_up, fw, w_down)
    dx2, dmixed, dg_ffn, db_out = _ffn_norm_bwd(dh2, dy, x2, p["g_ffn"], w_out)
    dw_up = _tn_matmul(dup0.reshape(N_DEV, s, FF_CHUNK), h2[None], "dw_up")
    dw_down = _tn_matmul(act, dy[None], "dw_down").reshape(N_DEV, -1, D_MODEL)
    dw_out = jnp.concatenate([_tn_matmul(attn[None], dx2[None], "dw_out_attn")[0],
                              _tn_matmul(c3[None], dx2[None], "dw_out_conv")[0]], axis=0).reshape(N_DEV, -1, D_MODEL)
    dcin, dcw, dcb, dcgain, dcbias, dbcin, *g_up = _conv_bwd(dmixed, c1, cin, cw8, p["cgain"], p["cbias"], [dw_up] if scatter else [])
    dqkv, dgq, dgk, dsink, dbqkv, *g_down_out = _attn_bwd(qkv, dmixed, p["gq2"], p["gk2"], p["sinks"],
                                                          [dw_down, dw_out] if scatter else [])
    dw_in = jnp.concatenate([_tn_matmul(dqkv[None], h1[None], "dw_qkv")[0], _tn_matmul(dcin[None], h1[None], "dw_cin")[0]], axis=0)
    dw_in = dw_in.reshape(N_DEV, -1, D_MODEL)
    grad_x, dg_mix = _in_bwd(dqkv, dcin, p["w_in_t"], x, dx2, p["g_mix"])
    if scatter:
        big = {"w_up": g_up[0], "w_down": g_down_out[0], "w_in": dw_in, "w_out": g_down_out[1]}
    else:
        big = {"w_up": dw_up, "w_down": dw_down, "w_in": dw_in, "w_out": dw_out}
    small = dict(dg_mix=dg_mix, dbqkv=dbqkv, dbcin=dbcin, dgq=dgq, dgk=dgk, dsink=dsink, dcw=dcw, dcb=dcb, dcgain=dcgain,
                 dcbias=dcbias, db_out=db_out, dg_ffn=dg_ffn, dfw=dfw, dfb=dfb, loss=loss)
    return loss, grad_x, big, small


BIG = ("w_in", "w_out", "w_up", "w_down")
SMALL = ("mix_norm_gain", "b_in", "q_norm_gain", "k_norm_gain", "attn_sinks", "conv_dw_w", "conv_dw_b",
         "conv_norm_gain", "conv_norm_bias", "b_out", "ffn_norm_gain", "ffn_dw_w", "ffn_dw_b")
ORDER = ("mix_norm_gain", "w_in", "b_in", "q_norm_gain", "k_norm_gain", "attn_sinks", "conv_dw_w", "conv_dw_b",
         "conv_norm_gain", "conv_norm_bias", "w_out", "b_out", "ffn_norm_gain", "w_up", "ffn_dw_w", "ffn_dw_b", "w_down")


def kernel(x, mix_norm_gain, w_in, b_in, q_norm_gain, k_norm_gain, attn_sinks, conv_dw_w, conv_dw_b, conv_norm_gain, conv_norm_bias, w_out, b_out, ffn_norm_gain, w_up, ffn_dw_w, ffn_dw_b, w_down, loss_target, m_mix_norm_gain, m_w_in, m_b_in, m_q_norm_gain, m_k_norm_gain, m_attn_sinks, m_conv_dw_w, m_conv_dw_b, m_conv_norm_gain, m_conv_norm_bias, m_w_out, m_b_out, m_ffn_norm_gain, m_w_up, m_ffn_dw_w, m_ffn_dw_b, m_w_down, v_mix_norm_gain, v_w_in, v_b_in, v_q_norm_gain, v_k_norm_gain, v_attn_sinks, v_conv_dw_w, v_conv_dw_b, v_conv_norm_gain, v_conv_norm_bias, v_w_out, v_b_out, v_ffn_norm_gain, v_w_up, v_ffn_dw_w, v_ffn_dw_b, v_w_down):
    w = dict(mix_norm_gain=mix_norm_gain, w_in=w_in, b_in=b_in, q_norm_gain=q_norm_gain, k_norm_gain=k_norm_gain,
             attn_sinks=attn_sinks, conv_dw_w=conv_dw_w, conv_dw_b=conv_dw_b, conv_norm_gain=conv_norm_gain,
             conv_norm_bias=conv_norm_bias, w_out=w_out, b_out=b_out, ffn_norm_gain=ffn_norm_gain, w_up=w_up,
             ffn_dw_w=ffn_dw_w, ffn_dw_b=ffn_dw_b, w_down=w_down)
    m = dict(mix_norm_gain=m_mix_norm_gain, w_in=m_w_in, b_in=m_b_in, q_norm_gain=m_q_norm_gain, k_norm_gain=m_k_norm_gain,
             attn_sinks=m_attn_sinks, conv_dw_w=m_conv_dw_w, conv_dw_b=m_conv_dw_b, conv_norm_gain=m_conv_norm_gain,
             conv_norm_bias=m_conv_norm_bias, w_out=m_w_out, b_out=m_b_out, ffn_norm_gain=m_ffn_norm_gain, w_up=m_w_up,
             ffn_dw_w=m_ffn_dw_w, ffn_dw_b=m_ffn_dw_b, w_down=m_w_down)
    v = dict(mix_norm_gain=v_mix_norm_gain, w_in=v_w_in, b_in=v_b_in, q_norm_gain=v_q_norm_gain, k_norm_gain=v_k_norm_gain,
             attn_sinks=v_attn_sinks, conv_dw_w=v_conv_dw_w, conv_dw_b=v_conv_dw_b, conv_norm_gain=v_conv_norm_gain,
             conv_norm_bias=v_conv_norm_bias, w_out=v_w_out, b_out=v_b_out, ffn_norm_gain=v_ffn_norm_gain, w_up=v_w_up,
             ffn_dw_w=v_ffn_dw_w, ffn_dw_b=v_ffn_dw_b, w_down=v_w_down)
    s = x.shape[1]

    wi8, cw8, fw8 = _allgather([w_in.T, conv_dw_w, ffn_dw_w], [BF16, F32, F32])
    lane_pad = ((0, 0), (0, 0), (0, FF_PADDED - FF_CHUNK))
    p = {
        "g_mix": mix_norm_gain.reshape(1, -1), "w_in_t": wi8.reshape(QKV_COLS + CIN_COLS, D_MODEL),
        "b_qkv": b_in[:QKV_COLS].reshape(1, -1), "b_cin": b_in[QKV_COLS:].reshape(1, -1),
        "gq2": jnp.tile(q_norm_gain, 2).reshape(1, -1), "gk2": jnp.tile(k_norm_gain, 2).reshape(1, -1), "sinks": attn_sinks,
        "cw8": jnp.repeat(cw8.transpose(1, 0, 2).reshape(CONV_KERNEL, CONV_WIDTH), SUBLANES, axis=0),
        "cb": conv_dw_b.reshape(1, -1), "cgain": conv_norm_gain.reshape(1, -1), "cbias": conv_norm_bias.reshape(1, -1),
        "b_out": b_out.reshape(1, -1), "g_ffn": ffn_norm_gain.reshape(1, -1),
        "fw": jnp.pad(fw8, lane_pad), "fb": jnp.pad(ffn_dw_b.reshape(N_DEV, 1, FF_CHUNK), lane_pad),
    }

    saved, (wu8,), (wo8, wd8) = _token_mixing(x[0], p, [w_up.T], [w_out, w_down])
    p.update(w_out=wo8.reshape(D_MODEL, D_MODEL), w_up=wu8, w_down=wd8.reshape(N_FF_PAIRS, FF_CHUNK, D_MODEL))
    loss, grad_x, big, small = _rest_of_step(x[0], loss_target[0], p, saved, True)

    g = dict(big)
    g["w_in"], gpack = _final_exchange(big["w_in"], _pack_small(small))

    delta, new_m, new_v = {}, {}, {}
    for n in BIG:
        if n in ("w_in", "w_up"):
            outs = _adamw(w[n].T, g[n], m[n].T, v[n].T, "adamw_" + n)
            g[n], delta[n], new_m[n], new_v[n] = g[n].T, *[o.T for o in outs]
        else:
            delta[n], new_m[n], new_v[n] = _adamw(w[n], g[n], m[n], v[n], "adamw_" + n)

    def view(a):
        return a if a.ndim == 2 else (a.reshape(-1, LANES) if a.size % LANES == 0 else a.reshape(1, -1))

    small_out = _adamw_small(gpack, *[{n: view(d[n]) for n in SMALL} for d in (w, m, v)])
    for n in SMALL:
        g[n], delta[n], new_m[n], new_v[n] = [a.reshape(w[n].shape) for a in small_out[n]]

    total = gpack[R_QKS + 3, 0]
    return (total, grad_x.reshape(1, s, D_MODEL), *[g[n] for n in ORDER], *[delta[n] for n in ORDER],
            *[new_m[n] for n in ORDER], *[new_v[n] for n in ORDER])
```

```python
import functools
import math

import jax
import jax.numpy as jnp
from jax import lax
from jax.experimental import pallas as pl
from jax.experimental.pallas import tpu as pltpu

F32 = jnp.float32
BF16 = jnp.bfloat16

D_MODEL = 1024
HEAD_DIM = 64
N_Q_HEADS = 8
N_KV_HEADS = 2
Q_COLS = 512
KV_COLS = 128
QKV_COLS = Q_COLS + 2 * KV_COLS
CONV_WIDTH = 512
CIN_COLS = 2 * CONV_WIDTH
CONV_KERNEL = 31
CONV_HALO = 32
D_FF = 2816
N_DEV = 8
FF_CHUNK = 2 * D_FF // N_DEV
N_FF_PAIRS = N_DEV // 2
ATT_BLOCK = 128
EPS = 1e-6
NEG_INF = -1e30
SLOPES = [float(2.0 ** (-8.0 * (h + 1.0) / N_Q_HEADS)) for h in range(N_Q_HEADS)]

ADAM_LR = 0.001
ADAM_B1 = 0.9
ADAM_B2 = 0.999
ADAM_EPS = 1e-08
ADAM_WD = 0.01
ADAM_STEP = 10

LANES = 128
SUBLANES = 8
VMEM_LIMIT = 56 * 1024 * 1024
MESH = pl.DeviceIdType.MESH


def _cparams(*sem, **kw):
    return pltpu.CompilerParams(dimension_semantics=sem or None, vmem_limit_bytes=VMEM_LIMIT, **kw)


def _resident(shape):
    nd = len(shape)
    return pl.BlockSpec(shape, lambda *_: (0,) * nd, pipeline_mode=pl.Buffered(1))


def _dot(a, b):
    return jnp.dot(a, b, preferred_element_type=F32)


def _dot_nt(a, b):
    return lax.dot_general(a, b, (((1,), (1,)), ((), ())), preferred_element_type=F32)


def _dot_tn(a, b):
    return lax.dot_general(a, b, (((0,), (0,)), ((), ())), preferred_element_type=F32)


def _sigmoid(x):
    return 1.0 / (1.0 + jnp.exp(-x))


def _lo_mask(shape):
    return lax.broadcasted_iota(jnp.int32, shape, len(shape) - 1) % LANES < HEAD_DIM


def _half_sums(t, lo):
    s_lo = jnp.sum(jnp.where(lo, t, 0.0), axis=-1, keepdims=True)
    s_hi = jnp.sum(jnp.where(lo, 0.0, t), axis=-1, keepdims=True)
    return jnp.where(lo, s_lo, s_hi)


def _head_norm(t, lo):
    r = lax.rsqrt(_half_sums(t * t, lo) * (1.0 / HEAD_DIM) + EPS)
    return t * r, r


def _head_norm_bwd(dn, n, r, lo):
    return r * (dn - n * (_half_sums(dn * n, lo) * (1.0 / HEAD_DIM)))


def _tile(s):
    return min(512, s)


TN_TOKENS = 2048
FF_COLS = ((0, 256), (256, 512), (512, 704))


def _position():
    return lax.axis_index("x"), lax.axis_index("y"), lax.axis_index("c")


def _dev_index(px, py, pc):
    return 4 * px + 2 * py + pc


def _flip(v, bit):
    return 1 - v if bit else v


OTHER_CHIPS = ((1, 0), (0, 1), (1, 1))
N_GATHER_COPIES = 1 + 2 * len(OTHER_CHIPS)


class _Gather:
    def __init__(self, shard_ref, out_ref, cast_buf, send_sems, recv_sems, local_sem):
        self.shard, self.out, self.buf = shard_ref, out_ref, cast_buf
        self.send_sems, self.recv_sems, self.local_sem = send_sems, recv_sems, local_sem
        x, y, c = _position()
        self.c = c
        self.me, self.sibling = (x, y, c), (x, y, 1 - c)
        self.chips = [(_flip(x, fx), _flip(y, fy)) for fx, fy in OTHER_CHIPS]

    def _copy(self, k, block, to, from_buf=False):
        rows = self.out.at[_dev_index(*block)]
        return pltpu.make_async_remote_copy(src_ref=self.buf if from_buf else rows, dst_ref=rows,
                                            send_sem=self.send_sems.at[k], recv_sem=self.recv_sems.at[k],
                                            device_id=to, device_id_type=MESH)

    def _local(self):
        return pltpu.make_async_copy(self.buf, self.out.at[_dev_index(*self.me)], self.local_sem)

    def start(self):
        self.buf[...] = self.shard[...].astype(self.buf.dtype)
        self._local().start()
        for j, chip in enumerate(self.chips):
            self._copy(1 + j, self.me, (*chip, self.c), from_buf=True).start()
        self._copy(0, self.me, self.sibling, from_buf=True).start()

    def forward(self):
        for j, chip in enumerate(self.chips):
            self._copy(1 + j, (*chip, self.c), self.me).wait_recv()
            self._copy(1 + len(self.chips) + j, (*chip, self.c), self.sibling).start()

    def finish(self):
        self._copy(0, self.sibling, self.me).wait_recv()
        for j, chip in enumerate(self.chips):
            self._copy(1 + len(self.chips) + j, (*chip, 1 - self.c), self.me).wait_recv()
        for k in range(N_GATHER_COPIES):
            self._copy(k, self.me, self.sibling).wait_send()
        self._local().wait()


def _gather_specs(shards):
    whole = [pl.BlockSpec(w.shape, lambda *_, nd=w.ndim: (0,) * nd, pipeline_mode=pl.Buffered(1)) for w in shards]
    outs = [pl.BlockSpec(memory_space=pl.ANY) for _ in shards]
    shapes = [jax.ShapeDtypeStruct((N_DEV,) + w.shape, BF16) for w in shards]
    scratch = []
    for w in shards:
        scratch += [pltpu.VMEM(w.shape, BF16), pltpu.SemaphoreType.DMA((N_GATHER_COPIES,)),
                    pltpu.SemaphoreType.DMA((N_GATHER_COPIES,)), pltpu.SemaphoreType.DMA(())]
    return whole, outs, shapes, scratch


def _run_gathers(gathers, step, n_steps):
    @pl.when(step == 0)
    def _():
        for g in gathers:
            g.start()

    @pl.when(step == 3 * n_steps // 4)
    def _():
        for g in gathers:
            g.forward()

    @pl.when(step == n_steps - 1)
    def _():
        for g in gathers:
            g.finish()


class _ReduceScatter:
    def __init__(self, g_ref, out_ref, stage, load_sems, send_a, recv_a, send_b, recv_b, sa_send, sa_recv, sb_send, sb_recv):
        self.g, self.out, self.stage, self.load_sems = g_ref, out_ref, stage, load_sems
        self.send_a, self.recv_a, self.send_b, self.recv_b = send_a, recv_a, send_b, recv_b
        self.sems = (sa_send, sa_recv, sb_send, sb_recv)
        x, y, c = _position()
        self.c, self.sibling = c, (x, y, 1 - c)
        self.chips = [(x, y)] + [(_flip(x, fx), _flip(y, fy)) for fx, fy in OTHER_CHIPS]

    def _copy_a(self, j):
        return pltpu.make_async_remote_copy(src_ref=self.send_a.at[j], dst_ref=self.recv_a.at[j], send_sem=self.sems[0].at[j],
                                            recv_sem=self.sems[1].at[j], device_id=self.sibling, device_id_type=MESH)

    def _copy_b(self, j):
        return pltpu.make_async_remote_copy(src_ref=self.send_b.at[j], dst_ref=self.recv_b.at[j], send_sem=self.sems[2].at[j],
                                            recv_sem=self.sems[3].at[j], device_id=(*self.chips[1 + j], self.c),
                                            device_id_type=MESH)

    def _load(self, j, core):
        return pltpu.make_async_copy(self.g.at[_dev_index(*self.chips[j], core)], self.stage.at[j % 2], self.load_sems.at[j % 2])

    def start(self):
        self._load(0, 1 - self.c).start()
        for j in range(len(self.chips)):
            self._load(j, 1 - self.c).wait()
            if j + 1 < len(self.chips):
                self._load(j + 1, 1 - self.c).start()
            self.send_a[j] = self.stage[j % 2].astype(BF16)
            self._copy_a(j).start()

    def middle(self):
        self._load(0, self.c).start()
        for j in range(len(self.chips)):
            self._load(j, self.c).wait()
            if j + 1 < len(self.chips):
                self._load(j + 1, self.c).start()
            self._copy_a(j).wait_recv()
            part = self.stage[j % 2] + self.recv_a[j].astype(F32)
            if j == 0:
                self.out[...] = part
            else:
                self.send_b[j - 1] = part.astype(BF16)
                self._copy_b(j - 1).start()

    def finish(self):
        for j in range(len(OTHER_CHIPS)):
            self._copy_b(j).wait_recv()
            self.out[...] += self.recv_b[j].astype(F32)
        for j in range(len(self.chips)):
            self._copy_a(j).wait_send()
        for j in range(len(OTHER_CHIPS)):
            self._copy_b(j).wait_send()


N_SCATTER_SCRATCH = 10


def _scatter_specs(g8s):
    na, nb = 1 + len(OTHER_CHIPS), len(OTHER_CHIPS)
    ins = [pl.BlockSpec(memory_space=pl.ANY) for _ in g8s]
    outs = [pl.BlockSpec(g.shape[1:], lambda *_: (0, 0)) for g in g8s]
    shapes = [jax.ShapeDtypeStruct(g.shape[1:], F32) for g in g8s]
    scratch = []
    for g in g8s:
        blk = g.shape[1:]
        scratch += [pltpu.VMEM((2,) + blk, F32), pltpu.SemaphoreType.DMA((2,)), pltpu.VMEM((na,) + blk, BF16), pltpu.VMEM((na,) + blk, BF16),
                    pltpu.VMEM((nb,) + blk, BF16), pltpu.VMEM((nb,) + blk, BF16),
                    pltpu.SemaphoreType.DMA((na,)), pltpu.SemaphoreType.DMA((na,)),
                    pltpu.SemaphoreType.DMA((nb,)), pltpu.SemaphoreType.DMA((nb,))]
    return ins, outs, shapes, scratch


def _run_scatters(scatters, step, n_steps):
    @pl.when(step == 0)
    def _():
        for r in scatters:
            r.start()

    @pl.when(step == min(max(1, n_steps // 4), n_steps - 1))
    def _():
        for r in scatters:
            r.middle()

    @pl.when(step == n_steps - 1)
    def _():
        for r in scatters:
            r.finish()


def _mix_proj(x, g_mix, w_in_t, b_qkv, b_cin):
    s = x.shape[0]
    tm = _tile(s)

    def body(x_ref, g_ref, w_ref, bq_ref, bc_ref, qkv_ref, cin_ref, h1_ref):
        xv = x_ref[...]
        r = lax.rsqrt(jnp.mean(xv * xv, axis=-1, keepdims=True) + EPS)
        h = (xv * r * g_ref[...]).astype(BF16)
        h1_ref[...] = h
        qkv_ref[...] = _dot_nt(h, w_ref[0:QKV_COLS, :]) + bq_ref[...]
        cin_ref[...] = _dot_nt(h, w_ref[QKV_COLS:, :]) + bc_ref[...]

    return pl.pallas_call(
        body, grid=(s // tm,),
        in_specs=[pl.BlockSpec((tm, D_MODEL), lambda i: (i, 0)), _resident((1, D_MODEL)),
                  _resident((QKV_COLS + CIN_COLS, D_MODEL)), _resident((1, QKV_COLS)), _resident((1, CIN_COLS))],
        out_specs=[pl.BlockSpec((tm, QKV_COLS), lambda i: (i, 0)), pl.BlockSpec((tm, CIN_COLS), lambda i: (i, 0)),
                   pl.BlockSpec((tm, D_MODEL), lambda i: (i, 0))],
        out_shape=[jax.ShapeDtypeStruct((s, QKV_COLS), F32), jax.ShapeDtypeStruct((s, CIN_COLS), F32),
                   jax.ShapeDtypeStruct((s, D_MODEL), BF16)],
        compiler_params=_cparams("parallel"), name="mix_proj")(x, g_mix, w_in_t, b_qkv, b_cin)


def _kv_variants(kv_all, gk2, lo):
    k_all = kv_all[:, :LANES]
    v_all = kv_all[:, LANES:]
    kn_pre, rk = _head_norm(k_all, lo)
    kn = kn_pre * gk2
    kr = pltpu.roll(kn, HEAD_DIM, 1)
    vr = pltpu.roll(v_all, HEAD_DIM, 1)
    zero = jnp.zeros_like(kn)
    k_lo = [jnp.where(lo, kn, zero).astype(BF16), jnp.where(lo, kr, zero).astype(BF16)]
    k_hi = [jnp.where(lo, zero, kr).astype(BF16), jnp.where(lo, zero, kn).astype(BF16)]
    v_lo = [jnp.where(lo, v_all, zero).astype(BF16), jnp.where(lo, vr, zero).astype(BF16)]
    v_hi = [jnp.where(lo, zero, vr).astype(BF16), jnp.where(lo, zero, v_all).astype(BF16)]
    return k_lo, k_hi, v_lo, v_hi, kn_pre, rk


def _att_consts(first_tile, b):
    rows = 2 * ATT_BLOCK
    qi = lax.broadcasted_iota(jnp.int32, (rows, 2 * ATT_BLOCK), 0) % ATT_BLOCK
    kj = lax.broadcasted_iota(jnp.int32, (rows, 2 * ATT_BLOCK), 1)
    rel = qi + ATT_BLOCK - kj
    valid = (rel >= 0) & (rel < ATT_BLOCK)
    if b == 0:
        valid = valid & ((kj >= ATT_BLOCK) | jnp.logical_not(first_tile))
    return rel.astype(F32), valid


def _row_const(va, vb):
    top = lax.broadcasted_iota(jnp.int32, (2 * ATT_BLOCK, 1), 0) < ATT_BLOCK
    return jnp.where(top, va, vb)


def _probs(q2, k_op, rel, valid, slope, sink):
    sc = _dot_nt(q2, k_op) * (1.0 / math.sqrt(HEAD_DIM)) - slope * rel
    sc = jnp.where(valid, sc, NEG_INF)
    m = jnp.maximum(jnp.max(sc, axis=-1, keepdims=True), sink)
    p = jnp.exp(sc - m)
    e_sink = jnp.exp(sink - m)
    inv = 1.0 / (jnp.sum(p, axis=-1, keepdims=True) + e_sink)
    return p * inv, e_sink * inv


def _attn_fwd(qkv, gq2, gk2, sinks, shards):
    s = qkv.shape[0]
    tq = _tile(s)
    nb = tq // ATT_BLOCK
    ng = len(shards)
    g_in, g_out, g_shape, g_scratch = _gather_specs(shards)

    def body(q_ref, kv_ref, kvp_ref, gq_ref, gk_ref, sink_ref, *rest):
        out_ref = rest[ng]
        i = pl.program_id(0)
        _run_gathers([_Gather(rest[a], rest[ng + 1 + a], *rest[2 * ng + 1 + 4 * a:2 * ng + 5 + 4 * a]) for a in range(ng)],
                     i, s // tq)
        lo = _lo_mask((1, LANES))
        kv_all = jnp.concatenate([kvp_ref[...], kv_ref[...]], axis=0)
        k_lo, k_hi, v_lo, v_hi, _, _ = _kv_variants(kv_all, gk_ref[...], lo)
        for b in range(nb):
            rel, valid = _att_consts(i == 0, b)
            rows = slice(b * ATT_BLOCK, (b + 1) * ATT_BLOCK)
            keys = slice(b * ATT_BLOCK, (b + 2) * ATT_BLOCK)
            for kvh in range(N_KV_HEADS):
                pairs = (2 * kvh, 2 * kvh + 1)
                q2 = jnp.concatenate([q_ref[rows, p * LANES:(p + 1) * LANES] for p in pairs], axis=0)
                qn, _ = _head_norm(q2, lo)
                q2 = (qn * gq_ref[...]).astype(BF16)
                out = None
                for odd, (k_op, v_op) in enumerate(((k_lo[kvh][keys], v_lo[kvh][keys]), (k_hi[kvh][keys], v_hi[kvh][keys]))):
                    ha, hb = 2 * pairs[0] + odd, 2 * pairs[1] + odd
                    p, _ = _probs(q2, k_op, rel, valid, _row_const(SLOPES[ha], SLOPES[hb]),
                                  _row_const(sink_ref[ha], sink_ref[hb]))
                    o = _dot(p.astype(BF16), v_op)
                    out = o if out is None else out + o
                for n, p in enumerate(pairs):
                    out_ref[rows, p * LANES:(p + 1) * LANES] = out[n * ATT_BLOCK:(n + 1) * ATT_BLOCK].astype(BF16)

    return pl.pallas_call(
        body, grid=(s // tq,),
        in_specs=[pl.BlockSpec((tq, Q_COLS), lambda i: (i, 0)),
                  pl.BlockSpec((tq, 2 * KV_COLS), lambda i: (i, 2)),
                  pl.BlockSpec((ATT_BLOCK, 2 * KV_COLS), lambda i: (jnp.maximum(i * nb - 1, 0), 2)),
                  _resident((1, LANES)), _resident((1, LANES)),
                  pl.BlockSpec(memory_space=pltpu.SMEM)] + g_in,
        out_specs=[pl.BlockSpec((tq, Q_COLS), lambda i: (i, 0))] + g_out,
        out_shape=[jax.ShapeDtypeStruct((s, Q_COLS), BF16)] + g_shape,
        scratch_shapes=g_scratch,
        compiler_params=_cparams("arbitrary"), name="attn_fwd")(qkv, qkv, qkv, gq2, gk2, sinks, *shards)


def _group_stats(c1, lo):
    mu = _half_sums(c1, lo) * (1.0 / HEAD_DIM)
    d = c1 - mu
    rstd = lax.rsqrt(_half_sums(d * d, lo) * (1.0 / HEAD_DIM) + EPS)
    return d * rstd, rstd


def _rows(ref, first_row, n):
    return ref[pl.ds(first_row, n, stride=1), :].reshape(n // SUBLANES, SUBLANES, LANES)


def _conv_fwd(cin, cw8, cb, gain, bias, shards):
    s = cin.shape[0]
    tm = _tile(s)
    rc = 64
    nchunk = CONV_WIDTH // LANES
    lead = CONV_HALO - (CONV_KERNEL - 1)
    ng = len(shards)
    g_in, g_out, g_shape, g_scratch = _gather_specs(shards)

    def body(cin_ref, cw_ref, cb_ref, gain_ref, bias_ref, *rest):
        c3_ref, c1_ref, ext_ref = rest[ng], rest[ng + 1], rest[2 * ng + 2]
        _run_gathers([_Gather(rest[a], rest[ng + 2 + a], *rest[2 * ng + 3 + 4 * a:2 * ng + 7 + 4 * a]) for a in range(ng)],
                     pl.program_id(0), s // tm)

        @pl.when(pl.program_id(0) == 0)
        def _():
            ext_ref[:, 0:CONV_HALO, :] = jnp.zeros((nchunk, CONV_HALO, LANES), F32)

        lo = _lo_mask((1, LANES))
        for cc in range(nchunk):
            cols = slice(cc * LANES, (cc + 1) * LANES)
            gcols = slice(CONV_WIDTH + cc * LANES, CONV_WIDTH + (cc + 1) * LANES)
            ext_ref[cc, CONV_HALO:CONV_HALO + tm, :] = cin_ref[:, cols] * _sigmoid(cin_ref[:, gcols])
            ext = ext_ref.at[cc]
            for r in range(tm // rc):
                rows = slice(r * rc, (r + 1) * rc)
                acc = jnp.zeros((rc // SUBLANES, SUBLANES, LANES), F32)
                for k in range(CONV_KERNEL):
                    acc = acc + cw_ref[k * SUBLANES:(k + 1) * SUBLANES, cols][None] * _rows(ext, r * rc + lead + k, rc)
                c1 = acc.reshape(rc, LANES) + cb_ref[:, cols]
                c1_ref[cc, rows, :] = c1
                nrm, _ = _group_stats(c1, lo)
                c2 = nrm * gain_ref[:, cols] + bias_ref[:, cols]
                c3_ref[rows, cols] = (c2 * _sigmoid(c2)).astype(BF16)
        ext_ref[:, 0:CONV_HALO, :] = ext_ref[:, tm:tm + CONV_HALO, :]

    return pl.pallas_call(
        body, grid=(s // tm,),
        in_specs=[pl.BlockSpec((tm, CIN_COLS), lambda i: (i, 0)), _resident((CONV_KERNEL * SUBLANES, CONV_WIDTH)),
                  _resident((1, CONV_WIDTH)), _resident((1, CONV_WIDTH)), _resident((1, CONV_WIDTH))] + g_in,
        out_specs=[pl.BlockSpec((tm, CONV_WIDTH), lambda i: (i, 0)), pl.BlockSpec((nchunk, tm, LANES), lambda i: (0, i, 0))] + g_out,
        out_shape=[jax.ShapeDtypeStruct((s, CONV_WIDTH), BF16), jax.ShapeDtypeStruct((nchunk, s, LANES), F32)] + g_shape,
        scratch_shapes=[pltpu.VMEM((nchunk, tm + CONV_HALO, LANES), F32)] + g_scratch,
        compiler_params=_cparams("arbitrary"), name="conv_fwd")(cin, cw8, cb, gain, bias, *shards)


def _out_proj(x, attn, c3, wo_a, wo_c, b_out, g_ffn):
    s = x.shape[0]
    tm = _tile(s)

    def body(x_ref, a_ref, c_ref, wa_ref, wc_ref, b_ref, g_ref, x2_ref, h2_ref):
        x2 = x_ref[...] + _dot(a_ref[...], wa_ref[...]) + _dot(c_ref[...], wc_ref[...]) + b_ref[...]
        x2_ref[...] = x2
        r = lax.rsqrt(jnp.mean(x2 * x2, axis=-1, keepdims=True) + EPS)
        h2_ref[...] = (x2 * r * g_ref[...]).astype(BF16)

    return pl.pallas_call(
        body, grid=(s // tm,),
        in_specs=[pl.BlockSpec((tm, D_MODEL), lambda i: (i, 0)), pl.BlockSpec((tm, Q_COLS), lambda i: (i, 0)),
                  pl.BlockSpec((tm, CONV_WIDTH), lambda i: (i, 0)),
                  pl.BlockSpec((Q_COLS, D_MODEL), lambda i: (0, 0), pipeline_mode=pl.Buffered(1)),
                  pl.BlockSpec((CONV_WIDTH, D_MODEL), lambda i: (1, 0), pipeline_mode=pl.Buffered(1)),
                  _resident((1, D_MODEL)), _resident((1, D_MODEL))],
        out_specs=[pl.BlockSpec((tm, D_MODEL), lambda i: (i, 0)), pl.BlockSpec((tm, D_MODEL), lambda i: (i, 0))],
        out_shape=[jax.ShapeDtypeStruct((s, D_MODEL), F32), jax.ShapeDtypeStruct((s, D_MODEL), BF16)],
        compiler_params=_cparams("parallel"), name="out_proj")(x, attn, c3, wo_a, wo_c, b_out, g_ffn)


FF_LANE_CHUNKS = -(-FF_CHUNK // LANES)
FF_PADDED = FF_LANE_CHUNKS * LANES


def _tap(ref, first_row, n):
    return ref[pl.ds(first_row, n, stride=1), :]


def _ffn_fwd(h2, x2, target, w_up, fw, fb, w_down):
    s = h2.shape[0]
    tm = _tile(s)
    hal = SUBLANES
    rc = min(128, tm)

    def body(h_ref, x2_ref, t_ref, wu_ref, fw_ref, fb_ref, wd_ref, up0_ref, gu_ref, act_ref, dy_ref, dyb_ref, loss_ref,
             ext_ref, carry_ref, act_buf, y_ref):
        i, ci = pl.program_id(0), pl.program_id(1)

        @pl.when((i == 0) & (ci == 0))
        def _():
            carry_ref[...] = jnp.zeros(carry_ref.shape, F32)
            ext_ref[...] = jnp.zeros(ext_ref.shape, F32)
            act_buf[...] = jnp.zeros(act_buf.shape, BF16)
            loss_ref[...] = jnp.zeros((1, 1), F32)

        @pl.when(ci == 0)
        def _():
            y_ref[...] = x2_ref[...]

        ws = (fw_ref[ci], fw_ref[ci + N_FF_PAIRS])
        bs = (fb_ref[ci], fb_ref[ci + N_FF_PAIRS])
        half_rows = (slice(0, tm // 2), slice(tm // 2, tm))
        n_grp = len(FF_COLS)

        def up_slices(grp):
            lo_c, hi_c = FF_COLS[grp]
            chunks = range(lo_c // LANES, -(-hi_c // LANES))

            def make(half, n, rows):
                def run():
                    c = ci + half * N_FF_PAIRS
                    u0 = _dot_nt(h_ref[rows, :], wu_ref[c, lo_c:hi_c, :])
                    up0_ref[half, 0, rows, lo_c:hi_c] = u0.astype(BF16)
                    if hi_c == FF_CHUNK:
                        up0_ref[half, 0, rows, FF_CHUNK:] = jnp.zeros((u0.shape[0], FF_PADDED - FF_CHUNK), BF16)
                    for j in chunks:
                        w = min(LANES, hi_c - j * LANES)
                        if n == 0:
                            ext_ref[half, j, 0:hal, 0:w] = carry_ref[c, :, j * LANES:j * LANES + w]
                        ext_ref[half, j, hal + rows.start:hal + rows.stop, 0:w] = u0[:, j * LANES - lo_c:j * LANES - lo_c + w]
                    if n == len(half_rows) - 1:
                        carry_ref[c, :, lo_c:hi_c] = u0[u0.shape[0] - hal:, :]
                return run
            return [make(half, n, rows) for half in range(2) for n, rows in enumerate(half_rows)]

        def down_slices(grp):
            lo_c, hi_c = FF_COLS[grp]

            def make(rows):
                def run():
                    y_ref[rows, :] += _dot(act_buf[rows, lo_c:hi_c], wd_ref[ci, lo_c:hi_c, :])
                return run
            return [make(rows) for rows in half_rows]

        def vector_blocks(grp):
            lo_c, hi_c = FF_COLS[grp]
            blocks = []
            for j in range(lo_c // LANES, -(-hi_c // LANES)):
                lanes = slice(j * LANES, (j + 1) * LANES)

                def gate(r, lanes=lanes, j=j):
                    base = r * rc
                    ups = []
                    for half in range(2):
                        e, w = ext_ref.at[half, j], ws[half]
                        ups.append(w[0:1, lanes] * _tap(e, base + hal - 2, rc) + w[1:2, lanes] * _tap(e, base + hal - 1, rc)
                                   + w[2:3, lanes] * _tap(e, base + hal, rc) + bs[half][:, lanes])
                    g, u = ups
                    gu_ref[0, 0, base:base + rc, lanes] = g.astype(BF16)
                    gu_ref[1, 0, base:base + rc, lanes] = u.astype(BF16)
                    act_buf[base:base + rc, lanes] = (g * _sigmoid(g) * u).astype(BF16)

                blocks += [functools.partial(gate, r) for r in range(tm // rc)]

            def finish():
                act_ref[0, :, lo_c:hi_c] = act_buf[:, lo_c:hi_c]
            blocks.append(finish)
            return blocks

        for run in up_slices(0):
            run()
        for grp in range(n_grp):
            matmuls = (up_slices(grp + 1) if grp + 1 < n_grp else []) + (down_slices(grp - 1) if grp > 0 else [])
            blocks = vector_blocks(grp)
            every = max(1, len(blocks) // (len(matmuls) + 1))
            for n, run in enumerate(blocks):
                run()
                if n % every == every - 1 and matmuls:
                    matmuls.pop(0)()
            for run in matmuls:
                run()
        for run in down_slices(n_grp - 1):
            run()

        @pl.when(ci == N_FF_PAIRS - 1)
        def _():
            e = y_ref[...] - t_ref[...]
            dy_ref[...] = e * (1.0 / D_MODEL)
            dyb_ref[...] = (e * (1.0 / D_MODEL)).astype(BF16)
            loss_ref[...] += (0.5 / D_MODEL) * jnp.sum(e * e).reshape(1, 1)

    tok = lambda i, ci: (i, 0)
    return pl.pallas_call(
        body, grid=(s // tm, N_FF_PAIRS),
        in_specs=[pl.BlockSpec((tm, D_MODEL), tok), pl.BlockSpec((tm, D_MODEL), tok), pl.BlockSpec((tm, D_MODEL), tok),
                  _resident((N_DEV, FF_CHUNK, D_MODEL)), _resident((N_DEV, 3, FF_PADDED)), _resident((N_DEV, 1, FF_PADDED)),
                  _resident((N_FF_PAIRS, FF_CHUNK, D_MODEL))],
        out_specs=[pl.BlockSpec((2, 1, tm, FF_PADDED), lambda i, ci: (0, ci, i, 0)),
                   pl.BlockSpec((2, 1, tm, FF_PADDED), lambda i, ci: (0, ci, i, 0)),
                   pl.BlockSpec((1, tm, FF_CHUNK), lambda i, ci: (ci, i, 0)),
                   pl.BlockSpec((tm, D_MODEL), tok), pl.BlockSpec((tm, D_MODEL), tok), pl.BlockSpec((1, 1), lambda i, ci: (0, 0))],
        out_shape=[jax.ShapeDtypeStruct((2, N_FF_PAIRS, s, FF_PADDED), BF16), jax.ShapeDtypeStruct((2, N_FF_PAIRS, s, FF_PADDED), BF16),
                   jax.ShapeDtypeStruct((N_FF_PAIRS, s, FF_CHUNK), BF16), jax.ShapeDtypeStruct((s, D_MODEL), F32),
                   jax.ShapeDtypeStruct((s, D_MODEL), BF16), jax.ShapeDtypeStruct((1, 1), F32)],
        scratch_shapes=[pltpu.VMEM((2, FF_LANE_CHUNKS, tm + hal, LANES), F32), pltpu.VMEM((N_DEV, hal, FF_CHUNK), F32),
                        pltpu.VMEM((tm, FF_PADDED), BF16), pltpu.VMEM((tm, D_MODEL), F32)],
        compiler_params=_cparams("arbitrary", "arbitrary"), name="ffn_fwd")(h2, x2, target, w_up, fw, fb, w_down)


def _ffn_bwd(dyb, up0, gu, w_up, fw, w_down):
    s = dyb.shape[0]
    tm = _tile(s)
    nt = s // tm
    nxt = SUBLANES
    rc = min(128, tm)

    def body(dy_ref, up0_ref, gu_ref, wu_ref, fw_ref, wd_ref,
             dup0_ref, dh2_ref, dfw_ref, dfb_ref, dext_ref, carry_ref, dact_buf, dup0_buf):
        i, ci = pl.program_id(0), pl.program_id(1)

        @pl.when((i == 0) & (ci == 0))
        def _():
            for ref in (carry_ref, dfw_ref, dfb_ref, dext_ref, dact_buf):
                ref[...] = jnp.zeros(ref.shape, F32)
            dup0_buf[...] = jnp.zeros(dup0_buf.shape, BF16)

        @pl.when(ci == 0)
        def _():
            dh2_ref[...] = jnp.zeros(dh2_ref.shape, F32)

        ws = (fw_ref[ci], fw_ref[ci + N_FF_PAIRS])
        fold = lambda v: jnp.sum(v.reshape(rc // SUBLANES, SUBLANES, LANES), axis=0)
        half_rows = (slice(0, tm // 2), slice(tm // 2, tm))
        n_grp = len(FF_COLS)

        def dact_slices(grp):
            lo_c, hi_c = FF_COLS[grp]

            def make(rows):
                def run():
                    dact_buf[rows, lo_c:hi_c] = _dot_nt(dy_ref[rows, :], wd_ref[ci, lo_c:hi_c, :])
                return run
            return [make(rows) for rows in half_rows]

        def dh2_slices(grp):
            lo_c, hi_c = FF_COLS[grp]

            def make(half, rows):
                def run():
                    c = ci + half * N_FF_PAIRS
                    dh2_ref[rows, :] += _dot(dup0_buf[half, rows, lo_c:hi_c], wu_ref[c, lo_c:hi_c, :])
                return run
            return [make(half, rows) for half in range(2) for rows in half_rows]

        def vector_blocks(grp):
            lo_c, hi_c = FF_COLS[grp]
            chunks = range(lo_c // LANES, -(-hi_c // LANES))
            blocks = []

            def stage():
                for half in range(2):
                    c = ci + half * N_FF_PAIRS
                    for j in chunks:
                        dext_ref[half, j, tm:tm + nxt, :] = carry_ref[c, :, j * LANES:(j + 1) * LANES]
            blocks.append(stage)
            for j in chunks:
                lanes = slice(j * LANES, (j + 1) * LANES)
                acc = [jnp.zeros((SUBLANES, LANES), F32)] * 8

                def grads(r, lanes=lanes, j=j, acc=acc):
                    base = r * rc
                    g = gu_ref[0, 0, base:base + rc, lanes].astype(F32)
                    u = gu_ref[1, 0, base:base + rc, lanes].astype(F32)
                    sg = _sigmoid(g)
                    silu = g * sg
                    dact = dact_buf[base:base + rc, lanes]
                    ds = (dact * u * (sg + silu - silu * sg), dact * silu)
                    for half in range(2):
                        dext_ref[half, j, base:base + rc, :] = ds[half]
                        acc[4 * half] = acc[4 * half] + fold(ds[half])

                def conv_back(r, lanes=lanes, j=j, acc=acc):
                    base = r * rc
                    for half in range(2):
                        d, w = dext_ref.at[half, j], ws[half]
                        taps = [_tap(d, base + k, rc) for k in range(3)]
                        dup0 = w[2:3, lanes] * taps[0] + w[1:2, lanes] * taps[1] + w[0:1, lanes] * taps[2]
                        dup0_buf[half, base:base + rc, lanes] = dup0.astype(BF16)
                        u0 = up0_ref[half, 0, base:base + rc, lanes].astype(F32)
                        for k in range(3):
                            acc[4 * half + 1 + k] = acc[4 * half + 1 + k] + fold(taps[2 - k] * u0)

                def sums(lanes=lanes, j=j, acc=acc):
                    for half in range(2):
                        c = ci + half * N_FF_PAIRS
                        carry_ref[c, :, lanes] = dext_ref[half, j, 0:nxt, :]
                        dfb_ref[c, :, lanes] += jnp.sum(acc[4 * half], axis=0, keepdims=True)
                        dfw_ref[c, :, lanes] += jnp.concatenate(
                            [jnp.sum(acc[4 * half + 1 + k], axis=0, keepdims=True) for k in range(3)], axis=0)

                blocks += [functools.partial(grads, r) for r in range(tm // rc)]
                blocks += [functools.partial(conv_back, r) for r in range(tm // rc)] + [sums]

            def finish():
                for half in range(2):
                    dup0_ref[half, 0, :, lo_c:hi_c] = dup0_buf[half, :, lo_c:hi_c]
            blocks.append(finish)
            return blocks

        for run in dact_slices(0):
            run()
        for grp in range(n_grp):
            matmuls = (dact_slices(grp + 1) if grp + 1 < n_grp else []) + (dh2_slices(grp - 1) if grp > 0 else [])
            blocks = vector_blocks(grp)
            every = max(1, len(blocks) // (len(matmuls) + 1))
            for n, run in enumerate(blocks):
                run()
                if n % every == every - 1 and matmuls:
                    matmuls.pop(0)()
            for run in matmuls:
                run()
        for run in dh2_slices(n_grp - 1):
            run()

    tok = lambda i, ci: (nt - 1 - i, 0)
    acc = lambda shape: pl.BlockSpec(shape, lambda i, ci: (0,) * len(shape))
    saved = pl.BlockSpec((2, 1, tm, FF_PADDED), lambda i, ci: (0, ci, nt - 1 - i, 0))
    return pl.pallas_call(
        body, grid=(nt, N_FF_PAIRS),
        in_specs=[pl.BlockSpec((tm, D_MODEL), tok), saved, saved,
                  _resident((N_DEV, FF_CHUNK, D_MODEL)), _resident((N_DEV, 3, FF_PADDED)),
                  _resident((N_FF_PAIRS, FF_CHUNK, D_MODEL))],
        out_specs=[pl.BlockSpec((2, 1, tm, FF_CHUNK), lambda i, ci: (0, ci, nt - 1 - i, 0)),
                   pl.BlockSpec((tm, D_MODEL), tok), acc((N_DEV, 3, FF_PADDED)), acc((N_DEV, 1, FF_PADDED))],
        out_shape=[jax.ShapeDtypeStruct((2, N_FF_PAIRS, s, FF_CHUNK), BF16), jax.ShapeDtypeStruct((s, D_MODEL), F32),
                   jax.ShapeDtypeStruct((N_DEV, 3, FF_PADDED), F32), jax.ShapeDtypeStruct((N_DEV, 1, FF_PADDED), F32)],
        scratch_shapes=[pltpu.VMEM((2, FF_LANE_CHUNKS, tm + nxt, LANES), F32), pltpu.VMEM((N_DEV, nxt, FF_PADDED), F32),
                        pltpu.VMEM((tm, FF_PADDED), F32), pltpu.VMEM((2, tm, FF_PADDED), BF16)],
        compiler_params=_cparams("arbitrary", "arbitrary"), name="ffn_bwd")(dyb, up0, gu, w_up, fw, w_down)


def _ffn_norm_bwd(dh2, dy, x2, g_ffn, w_out):
    s = dy.shape[0]
    tm = _tile(s)

    def body(dh_ref, dy_ref, x2_ref, g_ref, wo_ref, dx2_ref, dmix_ref, dg_ref, dbo_ref):
        @pl.when(pl.program_id(0) == 0)
        def _():
            dg_ref[...] = jnp.zeros(dg_ref.shape, F32)
            dbo_ref[...] = jnp.zeros(dbo_ref.shape, F32)

        x2v = x2_ref[...]
        r = lax.rsqrt(jnp.mean(x2v * x2v, axis=-1, keepdims=True) + EPS)
        n2 = x2v * r
        dh2 = dh_ref[...]
        dg_ref[...] += jnp.sum(dh2 * n2, axis=0, keepdims=True)
        dn = dh2 * g_ref[...]
        dx2 = dy_ref[...] + r * (dn - n2 * jnp.mean(dn * n2, axis=-1, keepdims=True))
        dx2_ref[...] = dx2
        dbo_ref[...] += jnp.sum(dx2, axis=0, keepdims=True)
        dmix_ref[...] = _dot_nt(dx2.astype(BF16), wo_ref[...])

    tok = pl.BlockSpec((tm, D_MODEL), lambda i: (i, 0))
    vec = pl.BlockSpec((1, D_MODEL), lambda i: (0, 0))
    return pl.pallas_call(
        body, grid=(s // tm,),
        in_specs=[tok, tok, tok, _resident((1, D_MODEL)), _resident((D_MODEL, D_MODEL))],
        out_specs=[tok, tok, vec, vec],
        out_shape=[jax.ShapeDtypeStruct((s, D_MODEL), F32), jax.ShapeDtypeStruct((s, D_MODEL), F32),
                   jax.ShapeDtypeStruct((1, D_MODEL), F32), jax.ShapeDtypeStruct((1, D_MODEL), F32)],
        compiler_params=_cparams("arbitrary"), name="ffn_norm_bwd")(dh2, dy, x2, g_ffn, w_out)


def _conv_bwd(dmixed, c1, cin, cw8, gain, bias, g8s):
    ns = len(g8s)
    s_in, s_out, s_shape, s_scratch = _scatter_specs(g8s)
    s = cin.shape[0]
    tm = _tile(s)
    nt = s // tm
    rc = 64
    rn = min(256, tm)
    hal = CONV_HALO
    nchunk = CONV_WIDTH // LANES

    def body(dc3_ref, dc3n_ref, c1_ref, c1n_ref, cin_ref, cw_ref, gain_ref, bias_ref, *rest):
        dcin_ref, dcw_ref, dcb_ref, dgain_ref, dbias_ref, dbcin_ref = rest[ns:ns + 6]
        dc1_ext, dcw8 = rest[2 * ns + 6:2 * ns + 8]
        i = pl.program_id(0)
        first, last = i == 0, i == nt - 1
        own = rest[2 * ns + 8:]
        _run_scatters([_ReduceScatter(rest[a], rest[ns + 6 + a], *own[N_SCATTER_SCRATCH * a:N_SCATTER_SCRATCH * (a + 1)])
                       for a in range(ns)], i, nt)

        @pl.when(first)
        def _():
            for ref in (dcw8, dcb_ref, dgain_ref, dbias_ref, dbcin_ref):
                ref[...] = jnp.zeros(ref.shape, F32)

        lo = _lo_mask((1, LANES))

        def norm_bwd(dc3, c1v, cols):
            nrm, rstd = _group_stats(c1v, lo)
            c2 = nrm * gain_ref[:, cols] + bias_ref[:, cols]
            sg = _sigmoid(c2)
            dc2 = dc3 * (sg * (1.0 + c2 * (1.0 - sg)))
            dn = dc2 * gain_ref[:, cols]
            inv = 1.0 / HEAD_DIM
            dc1 = rstd * (dn - _half_sums(dn, lo) * inv - nrm * (_half_sums(dn * nrm, lo) * inv))
            return dc1, dc2, nrm

        def row_sum(v):
            return jnp.sum(v, axis=0, keepdims=True)

        for cc in range(nchunk):
            cols = slice(cc * LANES, (cc + 1) * LANES)
            gcols = slice(CONV_WIDTH + cc * LANES, CONV_WIDTH + (cc + 1) * LANES)
            d1e = dc1_ext.at[cc]
            dc1n, _, _ = norm_bwd(dc3n_ref[:, cols], c1n_ref[cc], cols)
            d1e[tm:tm + hal, :] = jnp.where(last, 0.0, dc1n)

            for r in range(tm // rn):
                rows = slice(r * rn, (r + 1) * rn)
                dc1, dc2, nrm = norm_bwd(dc3_ref[rows, cols], c1_ref[cc, rows, :], cols)
                d1e[rows, :] = dc1
                dgain_ref[:, cols] += row_sum(dc2 * nrm)
                dbias_ref[:, cols] += row_sum(dc2)
                dcb_ref[:, cols] += row_sum(dc1)
            zero = jnp.zeros((1, LANES), F32)

            def taps(r, sums):
                rows = pl.ds(pl.multiple_of(r * rc, rc), rc)
                a = cin_ref[rows, cols]
                sg = _sigmoid(cin_ref[rows, gcols])
                c0 = (a * sg).reshape(rc // SUBLANES, SUBLANES, LANES)
                dc0 = jnp.zeros((rc // SUBLANES, SUBLANES, LANES), F32)
                for k in range(CONV_KERNEL):
                    krows = slice(k * SUBLANES, (k + 1) * SUBLANES)
                    shifted = _rows(d1e, r * rc + CONV_KERNEL - 1 - k, rc)
                    dc0 = dc0 + cw_ref[krows, cols][None] * shifted
                    dcw8[krows, cols] += jnp.sum(shifted * c0, axis=0)
                dc0 = dc0.reshape(rc, LANES)
                da = dc0 * sg
                dgate = dc0 * a * sg * (1.0 - sg)
                dcin_ref[rows, cols] = da.astype(BF16)
                dcin_ref[rows, gcols] = dgate.astype(BF16)
                return sums[0] + row_sum(da), sums[1] + row_sum(dgate)

            sums = lax.fori_loop(0, tm // rc, taps, (zero, zero))
            dbcin_ref[:, cols] += sums[0]
            dbcin_ref[:, gcols] += sums[1]

        @pl.when(last)
        def _():
            for k in range(CONV_KERNEL):
                dcw_ref[k:k + 1, :] = jnp.sum(dcw8[k * SUBLANES:(k + 1) * SUBLANES, :], axis=0, keepdims=True)

    nh = tm // hal
    acc = lambda shape: pl.BlockSpec(shape, lambda i: (0,) * len(shape))
    return pl.pallas_call(
        body, grid=(nt,),
        in_specs=[pl.BlockSpec((tm, CONV_WIDTH), lambda i: (i, 1)),
                  pl.BlockSpec((hal, CONV_WIDTH), lambda i: (jnp.minimum((i + 1) * nh, s // hal - 1), 1)),
                  pl.BlockSpec((nchunk, tm, LANES), lambda i: (0, i, 0)),
                  pl.BlockSpec((nchunk, hal, LANES), lambda i: (0, jnp.minimum((i + 1) * nh, s // hal - 1), 0)),
                  pl.BlockSpec((tm, CIN_COLS), lambda i: (i, 0)),
                  _resident((CONV_KERNEL * SUBLANES, CONV_WIDTH)), _resident((1, CONV_WIDTH)), _resident((1, CONV_WIDTH))] + s_in,
        out_specs=[pl.BlockSpec((tm, CIN_COLS), lambda i: (i, 0)), acc((CONV_KERNEL, CONV_WIDTH)), acc((1, CONV_WIDTH)),
                   acc((1, CONV_WIDTH)), acc((1, CONV_WIDTH)), acc((1, CIN_COLS))] + s_out,
        out_shape=[jax.ShapeDtypeStruct((s, CIN_COLS), BF16), jax.ShapeDtypeStruct((CONV_KERNEL, CONV_WIDTH), F32),
                   jax.ShapeDtypeStruct((1, CONV_WIDTH), F32), jax.ShapeDtypeStruct((1, CONV_WIDTH), F32),
                   jax.ShapeDtypeStruct((1, CONV_WIDTH), F32), jax.ShapeDtypeStruct((1, CIN_COLS), F32)] + s_shape,
        scratch_shapes=[pltpu.VMEM((nchunk, tm + hal, LANES), F32),
                        pltpu.VMEM((CONV_KERNEL * SUBLANES, CONV_WIDTH), F32)] + s_scratch,
        compiler_params=_cparams("arbitrary"), name="conv_bwd")(dmixed, dmixed, c1, c1, cin, cw8, gain, bias, *g8s)


def _attn_bwd(qkv, dmixed, gq2, gk2, sinks, g8s):
    ns = len(g8s)
    s_in, s_out, s_shape, s_scratch = _scatter_specs(g8s)
    s = qkv.shape[0]
    tq = _tile(s)
    nb = tq // ATT_BLOCK
    nt = s // tq

    def body(q_ref, kv_ref, kvp_ref, do_ref, gq_ref, gk_ref, sink_ref, *rest):
        dqkv_ref, dgq_ref, dgk_ref, dsink_ref, dbqkv_ref = rest[ns:ns + 5]
        dk_acc, dv_acc, carry_k, carry_v = rest[2 * ns + 5:2 * ns + 9]
        i = pl.program_id(0)
        t = nt - 1 - i
        own = rest[2 * ns + 9:]
        _run_scatters([_ReduceScatter(rest[a], rest[ns + 5 + a], *own[N_SCATTER_SCRATCH * a:N_SCATTER_SCRATCH * (a + 1)])
                       for a in range(ns)], i, nt)

        @pl.when(i == 0)
        def _():
            for ref in (carry_k, carry_v, dgq_ref, dgk_ref, dsink_ref, dbqkv_ref):
                ref[...] = jnp.zeros(ref.shape, F32)

        lo = _lo_mask((1, LANES))
        lane_id = lax.broadcasted_iota(jnp.int32, (1, LANES), 1)
        kv_all = jnp.concatenate([kvp_ref[...], kv_ref[...]], axis=0)
        k_lo, k_hi, v_lo, v_hi, kn_pre, rk = _kv_variants(kv_all, gk_ref[...], lo)
        for acc_ref, carry in ((dk_acc, carry_k), (dv_acc, carry_v)):
            acc_ref[:, 0:tq, :] = jnp.zeros((N_KV_HEADS, tq, LANES), F32)
            acc_ref[:, tq:tq + ATT_BLOCK, :] = carry[...]
        dsink = jnp.zeros((1, LANES), F32)
        dgq = jnp.zeros((1, LANES), F32)
        gq = gq_ref[...]
        for b in range(nb):
            rel, valid = _att_consts(t == 0, b)
            rows = slice(b * ATT_BLOCK, (b + 1) * ATT_BLOCK)
            keys = slice(b * ATT_BLOCK, (b + 2) * ATT_BLOCK)
            for kvh in range(N_KV_HEADS):
                pairs = (2 * kvh, 2 * kvh + 1)
                q_raw = jnp.concatenate([q_ref[rows, p * LANES:(p + 1) * LANES] for p in pairs], axis=0)
                qn_pre, rq = _head_norm(q_raw, lo)
                q2 = (qn_pre * gq).astype(BF16)
                do2 = jnp.concatenate([do_ref[rows, p * LANES:(p + 1) * LANES] for p in pairs], axis=0).astype(BF16)
                dq2 = jnp.zeros((2 * ATT_BLOCK, LANES), F32)
                for odd, (k_op, v_op) in enumerate(((k_lo[kvh][keys], v_lo[kvh][keys]), (k_hi[kvh][keys], v_hi[kvh][keys]))):
                    ha, hb = 2 * pairs[0] + odd, 2 * pairs[1] + odd
                    p, p_sink = _probs(q2, k_op, rel, valid, _row_const(SLOPES[ha], SLOPES[hb]),
                                       _row_const(sink_ref[ha], sink_ref[hb]))
                    dp = _dot_nt(do2, v_op)
                    delta = jnp.sum(p * dp, axis=-1, keepdims=True)
                    ds = (p * (dp - delta) * (1.0 / math.sqrt(HEAD_DIM))).astype(BF16)
                    dsk = p_sink * delta
                    dsink = dsink - jnp.where(lane_id == ha, jnp.sum(dsk[0:ATT_BLOCK]), 0.0) \
                        - jnp.where(lane_id == hb, jnp.sum(dsk[ATT_BLOCK:]), 0.0)
                    dq2 = dq2 + _dot(ds, k_op)
                    half = lo if odd == 0 else jnp.logical_not(lo)
                    dk_acc[kvh, keys, :] += jnp.where(half, _dot_tn(ds, q2), 0.0)
                    dv_acc[kvh, keys, :] += jnp.where(half, _dot_tn(p.astype(BF16), do2), 0.0)
                dgq = dgq + jnp.sum(dq2 * qn_pre, axis=0, keepdims=True)
                dq_raw = _head_norm_bwd(dq2 * gq, qn_pre, rq, lo)
                for n, p_ in enumerate(pairs):
                    blk = dq_raw[n * ATT_BLOCK:(n + 1) * ATT_BLOCK]
                    dqkv_ref[rows, p_ * LANES:(p_ + 1) * LANES] = blk.astype(BF16)
                    dbqkv_ref[:, p_ * LANES:(p_ + 1) * LANES] += jnp.sum(blk, axis=0, keepdims=True)
        carry_k[...] = dk_acc[:, 0:ATT_BLOCK, :]
        carry_v[...] = dv_acc[:, 0:ATT_BLOCK, :]

        def fold(acc_ref):
            both = []
            for kvh in range(N_KV_HEADS):
                a = acc_ref[kvh, ATT_BLOCK:ATT_BLOCK + tq, :]
                both.append(a + pltpu.roll(a, HEAD_DIM, 1))
            return jnp.where(lo, both[0], both[1])

        dkn = fold(dk_acc)
        dv = fold(dv_acc)
        kn_c, rk_c = kn_pre[ATT_BLOCK:], rk[ATT_BLOCK:]
        dgk_ref[...] += jnp.sum(dkn * kn_c, axis=0, keepdims=True)
        dk_raw = _head_norm_bwd(dkn * gk_ref[...], kn_c, rk_c, lo)
        dqkv_ref[:, Q_COLS:Q_COLS + KV_COLS] = dk_raw.astype(BF16)
        dqkv_ref[:, Q_COLS + KV_COLS:] = dv.astype(BF16)
        dbqkv_ref[:, Q_COLS:Q_COLS + KV_COLS] += jnp.sum(dk_raw, axis=0, keepdims=True)
        dbqkv_ref[:, Q_COLS + KV_COLS:] += jnp.sum(dv, axis=0, keepdims=True)
        dgq_ref[...] += dgq
        dsink_ref[...] += dsink

        @pl.when(i == nt - 1)
        def _():
            for ref in (dgq_ref, dgk_ref):
                v = ref[...]
                ref[...] = v + pltpu.roll(v, HEAD_DIM, 1)

    acc = lambda shape: pl.BlockSpec(shape, lambda i: (0,) * len(shape))
    return pl.pallas_call(
        body, grid=(nt,),
        in_specs=[pl.BlockSpec((tq, Q_COLS), lambda i: (nt - 1 - i, 0)),
                  pl.BlockSpec((tq, 2 * KV_COLS), lambda i: (nt - 1 - i, 2)),
                  pl.BlockSpec((ATT_BLOCK, 2 * KV_COLS), lambda i: (jnp.maximum((nt - 1 - i) * nb - 1, 0), 2)),
                  pl.BlockSpec((tq, Q_COLS), lambda i: (nt - 1 - i, 0)),
                  _resident((1, LANES)), _resident((1, LANES)), pl.BlockSpec(memory_space=pltpu.SMEM)] + s_in,
        out_specs=[pl.BlockSpec((tq, QKV_COLS), lambda i: (nt - 1 - i, 0)), acc((1, LANES)), acc((1, LANES)),
                   acc((1, LANES)), acc((1, QKV_COLS))] + s_out,
        out_shape=[jax.ShapeDtypeStruct((s, QKV_COLS), BF16), jax.ShapeDtypeStruct((1, LANES), F32),
                   jax.ShapeDtypeStruct((1, LANES), F32), jax.ShapeDtypeStruct((1, LANES), F32),
                   jax.ShapeDtypeStruct((1, QKV_COLS), F32)] + s_shape,
        scratch_shapes=[pltpu.VMEM((N_KV_HEADS, tq + ATT_BLOCK, LANES), F32), pltpu.VMEM((N_KV_HEADS, tq + ATT_BLOCK, LANES), F32),
                        pltpu.VMEM((N_KV_HEADS, ATT_BLOCK, LANES), F32), pltpu.VMEM((N_KV_HEADS, ATT_BLOCK, LANES), F32)] + s_scratch,
        compiler_params=_cparams("arbitrary"), name="attn_bwd")(qkv, qkv, qkv, dmixed, gq2, gk2, sinks, *g8s)


def _in_bwd(dqkv, dcin, w_in_t, x, dx2, g_mix):
    s = x.shape[0]
    tm = _tile(s)

    def body(dq_ref, dc_ref, w_ref, x_ref, dx2_ref, g_ref, gx_ref, dg_ref):
        @pl.when(pl.program_id(0) == 0)
        def _():
            dg_ref[...] = jnp.zeros(dg_ref.shape, F32)

        dh = _dot(dq_ref[...], w_ref[0:QKV_COLS, :]) + _dot(dc_ref[...], w_ref[QKV_COLS:, :])
        xv = x_ref[...]
        r = lax.rsqrt(jnp.mean(xv * xv, axis=-1, keepdims=True) + EPS)
        n = xv * r
        dg_ref[...] += jnp.sum(dh * n, axis=0, keepdims=True)
        dn = dh * g_ref[...]
        gx_ref[...] = dx2_ref[...] + r * (dn - n * jnp.mean(dn * n, axis=-1, keepdims=True))

    return pl.pallas_call(
        body, grid=(s // tm,),
        in_specs=[pl.BlockSpec((tm, QKV_COLS), lambda i: (i, 0)), pl.BlockSpec((tm, CIN_COLS), lambda i: (i, 0)),
                  _resident((QKV_COLS + CIN_COLS, D_MODEL)),
                  pl.BlockSpec((tm, D_MODEL), lambda i: (i, 0)), pl.BlockSpec((tm, D_MODEL), lambda i: (i, 0)),
                  _resident((1, D_MODEL))],
        out_specs=[pl.BlockSpec((tm, D_MODEL), lambda i: (i, 0)), pl.BlockSpec((1, D_MODEL), lambda i: (0, 0))],
        out_shape=[jax.ShapeDtypeStruct((s, D_MODEL), F32), jax.ShapeDtypeStruct((1, D_MODEL), F32)],
        compiler_params=_cparams("arbitrary"), name="in_bwd")(dqkv, dcin, w_in_t, x, dx2, g_mix)


def _tn_matmul(a, b, name):
    ga, s, m = a.shape
    gb, _, n = b.shape
    g = max(ga, gb)
    tk = min(TN_TOKENS, s)

    def body(a_ref, b_ref, o_ref):
        @pl.when(pl.program_id(1) == 0)
        def _():
            o_ref[...] = jnp.zeros(o_ref.shape, F32)

        o_ref[0] += _dot_tn(a_ref[0].astype(BF16), b_ref[0].astype(BF16))

    return pl.pallas_call(
        body, grid=(g, s // tk),
        in_specs=[pl.BlockSpec((1, tk, m), (lambda gi, k: (gi, k, 0)) if ga > 1 else (lambda gi, k: (0, k, 0))),
                  pl.BlockSpec((1, tk, n), (lambda gi, k: (gi, k, 0)) if gb > 1 else (lambda gi, k: (0, k, 0)))],
        out_specs=pl.BlockSpec((1, m, n), lambda gi, k: (gi, 0, 0)),
        out_shape=jax.ShapeDtypeStruct((g, m, n), F32),
        compiler_params=_cparams("parallel", "arbitrary"), name=name)(a, b)


def _tn_matmul_pair(a0, a1, b, name):
    s, m0 = a0.shape
    m1, n = a1.shape[1], b.shape[1]
    tk = min(TN_TOKENS, s)

    def body(a0_ref, a1_ref, b_ref, o_ref):
        @pl.when(pl.program_id(0) == 0)
        def _():
            o_ref[...] = jnp.zeros(o_ref.shape, F32)

        bv = b_ref[...].astype(BF16)
        o_ref[0:m0, :] += _dot_tn(a0_ref[...].astype(BF16), bv)
        o_ref[m0:, :] += _dot_tn(a1_ref[...].astype(BF16), bv)

    return pl.pallas_call(
        body, grid=(s // tk,),
        in_specs=[pl.BlockSpec((tk, m0), lambda k: (k, 0)), pl.BlockSpec((tk, m1), lambda k: (k, 0)),
                  pl.BlockSpec((tk, n), lambda k: (k, 0))],
        out_specs=pl.BlockSpec((m0 + m1, n), lambda k: (0, 0)),
        out_shape=jax.ShapeDtypeStruct((m0 + m1, n), F32),
        compiler_params=_cparams("arbitrary"), name=name)(a0, a1, b)


def _allgather(shards, dtypes):
    n = len(shards)
    n_copies = 1 + 2 * len(OTHER_CHIPS)

    def body(*refs):
        ins, outs = refs[:n], refs[n:2 * n]
        send_sems, recv_sems = refs[2 * n:]
        x, y, c = _position()
        me, sibling = (x, y, c), (x, y, 1 - c)
        chips = [(_flip(x, fx), _flip(y, fy)) for fx, fy in OTHER_CHIPS]
        for a in range(n):
            outs[a][_dev_index(*me)] = ins[a][...].astype(dtypes[a])

        def copy(a, k, block, to):
            rows = outs[a].at[_dev_index(*block)]
            return pltpu.make_async_remote_copy(src_ref=rows, dst_ref=rows, send_sem=send_sems.at[a, k],
                                                recv_sem=recv_sems.at[a, k], device_id=to, device_id_type=MESH)

        started = []
        for a in range(n):
            for j, chip in enumerate(chips):
                started.append(copy(a, 1 + j, me, (*chip, c)))
            started.append(copy(a, 0, me, sibling))
        for cp in started:
            cp.start()
        for a in range(n):
            for j, chip in enumerate(chips):
                copy(a, 1 + j, (*chip, c), me).wait_recv()
                fwd = copy(a, 1 + len(chips) + j, (*chip, c), sibling)
                fwd.start()
                started.append(fwd)
        for a in range(n):
            copy(a, 0, sibling, me).wait_recv()
            for j, chip in enumerate(chips):
                copy(a, 1 + len(chips) + j, (*chip, 1 - c), me).wait_recv()
        for cp in started:
            cp.wait_send()

    vmem = pl.BlockSpec(memory_space=pltpu.VMEM)
    return pl.pallas_call(
        body, in_specs=[vmem] * n, out_specs=[vmem] * n,
        out_shape=[jax.ShapeDtypeStruct((N_DEV,) + w.shape, dt) for w, dt in zip(shards, dtypes)],
        scratch_shapes=[pltpu.SemaphoreType.DMA((n, n_copies)), pltpu.SemaphoreType.DMA((n, n_copies))],
        compiler_params=pltpu.CompilerParams(vmem_limit_bytes=VMEM_LIMIT), name="allgather_weights")(*shards)


def _final_exchange(g8, v):
    rows = v.shape[0]
    _, _, s_shape, s_scratch = _scatter_specs([g8])

    def body(g_ref, v_ref, gout_ref, vout_ref, gath, send_sems, recv_sems, *rs_scratch):
        scatter = _ReduceScatter(g_ref, gout_ref, *rs_scratch)
        x, y, c = _position()
        me = _dev_index(x, y, c)
        peers = [(_flip(x, k >> 2 & 1), _flip(y, k >> 1 & 1), _flip(c, k & 1)) for k in range(1, N_DEV)]

        def copy(k, block):
            return pltpu.make_async_remote_copy(src_ref=gath.at[block], dst_ref=gath.at[block], send_sem=send_sems.at[k],
                                                recv_sem=recv_sems.at[k], device_id=peers[k], device_id_type=MESH)

        scatter.start()
        gath[me] = v_ref[...]
        for k in range(N_DEV - 1):
            copy(k, me).start()
        scatter.middle()
        for k in range(N_DEV - 1):
            copy(k, _dev_index(*peers[k])).wait_recv()
        for k in range(N_DEV - 1):
            copy(k, me).wait_send()
        total = gath[0]
        for d in range(1, N_DEV):
            total = total + gath[d]
        vout_ref[...] = total
        scatter.finish()

    vmem = pl.BlockSpec(memory_space=pltpu.VMEM)
    return pl.pallas_call(
        body, in_specs=[pl.BlockSpec(memory_space=pl.ANY), vmem], out_specs=[vmem, vmem],
        out_shape=s_shape + [jax.ShapeDtypeStruct((rows, LANES), F32)],
        scratch_shapes=[pltpu.VMEM((N_DEV, rows, LANES), F32), pltpu.SemaphoreType.DMA((N_DEV - 1,)),
                        pltpu.SemaphoreType.DMA((N_DEV - 1,))] + s_scratch,
        compiler_params=pltpu.CompilerParams(vmem_limit_bytes=VMEM_LIMIT), name="final_exchange")(g8, v)


def _row_tile(r):
    for n in (8, 4, 2):
        if r % (n * SUBLANES) == 0:
            return r // n
    return r


def _adam_math(wv, gv, mv, vv):
    mn = ADAM_B1 * mv + (1.0 - ADAM_B1) * gv
    vn = ADAM_B2 * vv + (1.0 - ADAM_B2) * (gv * gv)
    m_hat = mn / (1.0 - ADAM_B1 ** ADAM_STEP)
    v_hat = vn / (1.0 - ADAM_B2 ** ADAM_STEP)
    return -ADAM_LR * (m_hat / (jnp.sqrt(v_hat) + ADAM_EPS) + ADAM_WD * wv), mn, vn


def _adamw(w, g, m, v, name):
    r, c_ = w.shape
    tr = _row_tile(r)

    def body(w_ref, g_ref, m_ref, v_ref, d_ref, mo_ref, vo_ref):
        d_ref[...], mo_ref[...], vo_ref[...] = _adam_math(w_ref[...], g_ref[...], m_ref[...], v_ref[...])

    spec = pl.BlockSpec((tr, c_), lambda i: (i, 0))
    return pl.pallas_call(
        body, grid=(r // tr,), in_specs=[spec] * 4, out_specs=[spec] * 3,
        out_shape=[jax.ShapeDtypeStruct((r, c_), F32)] * 3,
        compiler_params=_cparams("parallel"), name=name)(w, g, m, v)


FW_ROWS = 24
CW_ROWS = 32
R_FW = 0
R_FB = R_FW + N_DEV * FW_ROWS
R_CW = R_FB + 48
R_BQKV = R_CW + (CONV_WIDTH // LANES) * CW_ROWS
R_BCIN = R_BQKV + 8
R_GMIX = R_BCIN + 8
R_BOUT = R_GMIX + 8
R_GFFN = R_BOUT + 8
R_CB = R_GFFN + 8
R_CGAIN = R_CB + 8
R_CBIAS = R_CGAIN + 8
R_QKS = R_CBIAS + 8
SMALL_ROWS = R_QKS + 8


def _pack_small(raw):
    def rows(a, n):
        a = a.reshape(-1, LANES)
        return jnp.pad(a, ((0, n - a.shape[0]), (0, 0)))

    fw = jnp.pad(raw["dfw"].reshape(N_DEV, -1, LANES), ((0, 0), (0, FW_ROWS - 3 * FF_LANE_CHUNKS), (0, 0)))
    cw = jnp.pad(raw["dcw"].reshape(CONV_KERNEL, -1, LANES).transpose(1, 0, 2), ((0, 0), (0, CW_ROWS - CONV_KERNEL), (0, 0)))
    qks = jnp.concatenate([raw["dgq"], raw["dgk"], raw["dsink"], jnp.pad(raw["loss"], ((0, 0), (0, LANES - 1)))], axis=0)
    return jnp.concatenate([
        fw.reshape(-1, LANES), rows(raw["dfb"][:, 0, :FF_CHUNK], 48), cw.reshape(-1, LANES), rows(raw["dbqkv"], 8),
        rows(raw["dbcin"], 8), rows(raw["dg_mix"], 8), rows(raw["db_out"], 8), rows(raw["dg_ffn"], 8), rows(raw["dcb"], 8),
        rows(raw["dcgain"], 8), rows(raw["dcbias"], 8), rows(qks, 8)], axis=0)


def _adamw_small(gpack, w, m, v):
    n = len(SMALL)
    ix = {name: i for i, name in enumerate(SMALL)}

    def body(g_ref, *refs):
        w_refs, m_refs, v_refs, outs = refs[:n], refs[n:2 * n], refs[2 * n:3 * n], refs[3 * n:]
        d = _dev_index(*_position())

        def step(name, idx, gv):
            i = ix[name]
            delta, mn, vn = _adam_math(w_refs[i][idx], gv, m_refs[i][idx], v_refs[i][idx])
            for ref, val in zip(outs[4 * i:4 * i + 4], (gv, delta, mn, vn)):
                ref[idx] = val

        def whole(name, row, nrows):
            step(name, (slice(None), slice(None)), g_ref[row:row + nrows, :])

        whole("mix_norm_gain", R_GMIX, 8)
        whole("b_out", R_BOUT, 8)
        whole("ffn_norm_gain", R_GFFN, 8)
        whole("conv_dw_b", R_CB, 4)
        whole("conv_norm_gain", R_CGAIN, 4)
        whole("conv_norm_bias", R_CBIAS, 4)
        whole("ffn_dw_b", R_FB, 2 * D_FF // LANES)
        nq = QKV_COLS // LANES
        step("b_in", (slice(0, nq), slice(None)), g_ref[R_BQKV:R_BQKV + nq, :])
        step("b_in", (slice(nq, nq + CIN_COLS // LANES), slice(None)), g_ref[R_BCIN:R_BCIN + CIN_COLS // LANES, :])
        step("q_norm_gain", (slice(None), slice(None)), g_ref[R_QKS:R_QKS + 1, 0:HEAD_DIM])
        step("k_norm_gain", (slice(None), slice(None)), g_ref[R_QKS + 1:R_QKS + 2, 0:HEAD_DIM])
        step("attn_sinks", (slice(None), slice(None)), g_ref[R_QKS + 2:R_QKS + 3, 0:N_Q_HEADS])
        blk = g_ref[pl.ds(pl.multiple_of(R_CW + CW_ROWS * lax.shift_right_logical(d, 1), SUBLANES), CW_ROWS), :]
        blk = jnp.where((d & 1) == 1, pltpu.roll(blk, HEAD_DIM, 1), blk)
        step("conv_dw_w", (slice(None), slice(None)), blk[0:CONV_KERNEL, 0:CONV_WIDTH // N_DEV])
        blk = g_ref[pl.ds(pl.multiple_of(R_FW + FW_ROWS * d, SUBLANES), FW_ROWS), :]
        for k in range(3):
            for j in range(FF_LANE_CHUNKS):
                wd = min(LANES, FF_CHUNK - j * LANES)
                row = k * FF_LANE_CHUNKS + j
                step("ffn_dw_w", (slice(k, k + 1), slice(j * LANES, j * LANES + wd)), blk[row:row + 1, 0:wd])

    vmem = pl.BlockSpec(memory_space=pltpu.VMEM)
    args = [gpack] + [d[name] for d in (w, m, v) for name in SMALL]
    outs = pl.pallas_call(
        body, in_specs=[vmem] * len(args), out_specs=[vmem] * (4 * n),
        out_shape=[jax.ShapeDtypeStruct(w[name].shape, F32) for name in SMALL for _ in range(4)],
        compiler_params=pltpu.CompilerParams(vmem_limit_bytes=VMEM_LIMIT), name="adamw_small")(*args)
    return {name: outs[4 * i:4 * i + 4] for i, name in enumerate(SMALL)}


def _token_mixing(x, p, attn_shards, conv_shards):
    qkv, cin, h1 = _mix_proj(x, p["g_mix"], p["w_in_t"], p["b_qkv"], p["b_cin"])
    attn, *from_attn = _attn_fwd(qkv, p["gq2"], p["gk2"], p["sinks"], attn_shards)
    c3, c1, *from_conv = _conv_fwd(cin, p["cw8"], p["cb"], p["cgain"], p["cbias"], conv_shards)
    return (qkv, cin, h1, attn, c3, c1), from_attn, from_conv


def _rest_of_step(x, target, p, saved, scatter):
    s = x.shape[0]
    qkv, cin, h1, attn, c3, c1 = saved
    cw8, w_out, w_up, w_down = p["cw8"], p["w_out"], p["w_up"], p["w_down"]
    x2, h2 = _out_proj(x, attn, c3, w_out, w_out, p["b_out"], p["g_ffn"])
    fw, fb = p["fw"], p["fb"]
    up0, gu, act, dy, dyb, loss = _ffn_fwd(h2, x2, target, w_up, fw, fb, w_down)
    dup0, dh2, dfw, dfb = _ffn_bwd(dyb, up0, gu, w_up, fw, w_down)
    dx2, dmixed, dg_ffn, db_out = _ffn_norm_bwd(dh2, dy, x2, p["g_ffn"], w_out)
    dw_up = _tn_matmul(dup0.reshape(N_DEV, s, FF_CHUNK), h2[None], "dw_up")
    dw_down = _tn_matmul(act, dyb[None], "dw_down").reshape(N_DEV, -1, D_MODEL)
    dw_out = _tn_matmul_pair(attn, c3, dx2, "dw_out").reshape(N_DEV, -1, D_MODEL)
    dcin, dcw, dcb, dcgain, dcbias, dbcin, *g_up = _conv_bwd(dmixed, c1, cin, cw8, p["cgain"], p["cbias"], [dw_up] if scatter else [])
    dqkv, dgq, dgk, dsink, dbqkv, *g_down_out = _attn_bwd(qkv, dmixed, p["gq2"], p["gk2"], p["sinks"],
                                                          [dw_down, dw_out] if scatter else [])
    dw_in = _tn_matmul_pair(dqkv, dcin, h1, "dw_in").reshape(N_DEV, -1, D_MODEL)
    grad_x, dg_mix = _in_bwd(dqkv, dcin, p["w_in_t"], x, dx2, p["g_mix"])
    if scatter:
        big = {"w_up": g_up[0], "w_down": g_down_out[0], "w_in": dw_in, "w_out": g_down_out[1]}
    else:
        big = {"w_up": dw_up, "w_down": dw_down, "w_in": dw_in, "w_out": dw_out}
    small = dict(dg_mix=dg_mix, dbqkv=dbqkv, dbcin=dbcin, dgq=dgq, dgk=dgk, dsink=dsink, dcw=dcw, dcb=dcb, dcgain=dcgain,
                 dcbias=dcbias, db_out=db_out, dg_ffn=dg_ffn, dfw=dfw, dfb=dfb, loss=loss)
    return loss, grad_x, big, small


BIG = ("w_in", "w_out", "w_up", "w_down")
SMALL = ("mix_norm_gain", "b_in", "q_norm_gain", "k_norm_gain", "attn_sinks", "conv_dw_w", "conv_dw_b",
         "conv_norm_gain", "conv_norm_bias", "b_out", "ffn_norm_gain", "ffn_dw_w", "ffn_dw_b")
ORDER = ("mix_norm_gain", "w_in", "b_in", "q_norm_gain", "k_norm_gain", "attn_sinks", "conv_dw_w", "conv_dw_b",
         "conv_norm_gain", "conv_norm_bias", "w_out", "b_out", "ffn_norm_gain", "w_up", "ffn_dw_w", "ffn_dw_b", "w_down")


def kernel(x, mix_norm_gain, w_in, b_in, q_norm_gain, k_norm_gain, attn_sinks, conv_dw_w, conv_dw_b, conv_norm_gain, conv_norm_bias, w_out, b_out, ffn_norm_gain, w_up, ffn_dw_w, ffn_dw_b, w_down, loss_target, m_mix_norm_gain, m_w_in, m_b_in, m_q_norm_gain, m_k_norm_gain, m_attn_sinks, m_conv_dw_w, m_conv_dw_b, m_conv_norm_gain, m_conv_norm_bias, m_w_out, m_b_out, m_ffn_norm_gain, m_w_up, m_ffn_dw_w, m_ffn_dw_b, m_w_down, v_mix_norm_gain, v_w_in, v_b_in, v_q_norm_gain, v_k_norm_gain, v_attn_sinks, v_conv_dw_w, v_conv_dw_b, v_conv_norm_gain, v_conv_norm_bias, v_w_out, v_b_out, v_ffn_norm_gain, v_w_up, v_ffn_dw_w, v_ffn_dw_b, v_w_down):
    w = dict(mix_norm_gain=mix_norm_gain, w_in=w_in, b_in=b_in, q_norm_gain=q_norm_gain, k_norm_gain=k_norm_gain,
             attn_sinks=attn_sinks, conv_dw_w=conv_dw_w, conv_dw_b=conv_dw_b, conv_norm_gain=conv_norm_gain,
             conv_norm_bias=conv_norm_bias, w_out=w_out, b_out=b_out, ffn_norm_gain=ffn_norm_gain, w_up=w_up,
             ffn_dw_w=ffn_dw_w, ffn_dw_b=ffn_dw_b, w_down=w_down)
    m = dict(mix_norm_gain=m_mix_norm_gain, w_in=m_w_in, b_in=m_b_in, q_norm_gain=m_q_norm_gain, k_norm_gain=m_k_norm_gain,
             attn_sinks=m_attn_sinks, conv_dw_w=m_conv_dw_w, conv_dw_b=m_conv_dw_b, conv_norm_gain=m_conv_norm_gain,
             conv_norm_bias=m_conv_norm_bias, w_out=m_w_out, b_out=m_b_out, ffn_norm_gain=m_ffn_norm_gain, w_up=m_w_up,
             ffn_dw_w=m_ffn_dw_w, ffn_dw_b=m_ffn_dw_b, w_down=m_w_down)
    v = dict(mix_norm_gain=v_mix_norm_gain, w_in=v_w_in, b_in=v_b_in, q_norm_gain=v_q_norm_gain, k_norm_gain=v_k_norm_gain,
             attn_sinks=v_attn_sinks, conv_dw_w=v_conv_dw_w, conv_dw_b=v_conv_dw_b, conv_norm_gain=v_conv_norm_gain,
             conv_norm_bias=v_conv_norm_bias, w_out=v_w_out, b_out=v_b_out, ffn_norm_gain=v_ffn_norm_gain, w_up=v_w_up,
             ffn_dw_w=v_ffn_dw_w, ffn_dw_b=v_ffn_dw_b, w_down=v_w_down)
    s = x.shape[1]

    wi8, cw8, fw8 = _allgather([w_in.T, conv_dw_w, ffn_dw_w], [BF16, F32, F32])
    lane_pad = ((0, 0), (0, 0), (0, FF_PADDED - FF_CHUNK))
    p = {
        "g_mix": mix_norm_gain.reshape(1, -1), "w_in_t": wi8.reshape(QKV_COLS + CIN_COLS, D_MODEL),
        "b_qkv": b_in[:QKV_COLS].reshape(1, -1), "b_cin": b_in[QKV_COLS:].reshape(1, -1),
        "gq2": jnp.tile(q_norm_gain, 2).reshape(1, -1), "gk2": jnp.tile(k_norm_gain, 2).reshape(1, -1), "sinks": attn_sinks,
        "cw8": jnp.repeat(cw8.transpose(1, 0, 2).reshape(CONV_KERNEL, CONV_WIDTH), SUBLANES, axis=0),
        "cb": conv_dw_b.reshape(1, -1), "cgain": conv_norm_gain.reshape(1, -1), "cbias": conv_norm_bias.reshape(1, -1),
        "b_out": b_out.reshape(1, -1), "g_ffn": ffn_norm_gain.reshape(1, -1),
        "fw": jnp.pad(fw8, lane_pad), "fb": jnp.pad(ffn_dw_b.reshape(N_DEV, 1, FF_CHUNK), lane_pad),
    }

    saved, (wu8,), (wo8, wd8) = _token_mixing(x[0], p, [w_up.T], [w_out, w_down])
    p.update(w_out=wo8.reshape(D_MODEL, D_MODEL), w_up=wu8, w_down=wd8.reshape(N_FF_PAIRS, FF_CHUNK, D_MODEL))
    loss, grad_x, big, small = _rest_of_step(x[0], loss_target[0], p, saved, True)

    g = dict(big)
    g["w_in"], gpack = _final_exchange(big["w_in"], _pack_small(small))

    delta, new_m, new_v = {}, {}, {}
    for n in BIG:
        if n in ("w_in", "w_up"):
            outs = _adamw(w[n].T, g[n], m[n].T, v[n].T, "adamw_" + n)
            g[n], delta[n], new_m[n], new_v[n] = g[n].T, *[o.T for o in outs]
        else:
            delta[n], new_m[n], new_v[n] = _adamw(w[n], g[n], m[n], v[n], "adamw_" + n)

    def view(a):
        return a if a.ndim == 2 else (a.reshape(-1, LANES) if a.size % LANES == 0 else a.reshape(1, -1))

    small_out = _adamw_small(gpack, *[{n: view(d[n]) for n in SMALL} for d in (w, m, v)])
    for n in SMALL:
        g[n], delta[n], new_m[n], new_v[n] = [a.reshape(w[n].shape) for a in small_out[n]]

    total = gpack[R_QKS + 3, 0]
    return (total, grad_x.reshape(1, s, D_MODEL), *[g[n] for n in ORDER], *[delta[n] for n in ORDER],
            *[new_m[n] for n in ORDER], *[new_v[n] for n in ORDER])
```

```python
import functools
import math

import jax
import jax.numpy as jnp
from jax import lax
from jax.experimental import pallas as pl
from jax.experimental.pallas import tpu as pltpu

F32 = jnp.float32
BF16 = jnp.bfloat16

D_MODEL = 1024
HEAD_DIM = 64
N_Q_HEADS = 8
N_KV_HEADS = 2
Q_COLS = 512
KV_COLS = 128
QKV_COLS = Q_COLS + 2 * KV_COLS
CONV_WIDTH = 512
CIN_COLS = 2 * CONV_WIDTH
CONV_KERNEL = 31
CONV_HALO = 32
D_FF = 2816
N_DEV = 8
FF_CHUNK = 2 * D_FF // N_DEV
N_FF_PAIRS = N_DEV // 2
ATT_BLOCK = 128
EPS = 1e-6
NEG_INF = -1e30
SLOPES = [float(2.0 ** (-8.0 * (h + 1.0) / N_Q_HEADS)) for h in range(N_Q_HEADS)]

ADAM_LR = 0.001
ADAM_B1 = 0.9
ADAM_B2 = 0.999
ADAM_EPS = 1e-08
ADAM_WD = 0.01
ADAM_STEP = 10

LANES = 128
SUBLANES = 8
VMEM_LIMIT = 56 * 1024 * 1024
MESH = pl.DeviceIdType.MESH


def _cparams(*sem, **kw):
    return pltpu.CompilerParams(dimension_semantics=sem or None, vmem_limit_bytes=VMEM_LIMIT, **kw)


def _resident(shape):
    nd = len(shape)
    return pl.BlockSpec(shape, lambda *_: (0,) * nd, pipeline_mode=pl.Buffered(1))


def _dot(a, b):
    return jnp.dot(a, b, preferred_element_type=F32)


def _dot_nt(a, b):
    return lax.dot_general(a, b, (((1,), (1,)), ((), ())), preferred_element_type=F32)


def _dot_tn(a, b):
    return lax.dot_general(a, b, (((0,), (0,)), ((), ())), preferred_element_type=F32)


def _sigmoid(x):
    return 1.0 / (1.0 + jnp.exp(-x))


def _lo_mask(shape):
    return lax.broadcasted_iota(jnp.int32, shape, len(shape) - 1) % LANES < HEAD_DIM


def _half_sums(t, lo):
    s_lo = jnp.sum(jnp.where(lo, t, 0.0), axis=-1, keepdims=True)
    s_hi = jnp.sum(jnp.where(lo, 0.0, t), axis=-1, keepdims=True)
    return jnp.where(lo, s_lo, s_hi)


def _head_norm(t, lo):
    r = lax.rsqrt(_half_sums(t * t, lo) * (1.0 / HEAD_DIM) + EPS)
    return t * r, r


def _head_norm_bwd(dn, n, r, lo):
    return r * (dn - n * (_half_sums(dn * n, lo) * (1.0 / HEAD_DIM)))


def _tile(s):
    return min(512, s)


TN_TOKENS = 2048
FF_COLS = ((0, 256), (256, 512), (512, 704))


def _position():
    return lax.axis_index("x"), lax.axis_index("y"), lax.axis_index("c")


def _dev_index(px, py, pc):
    return 4 * px + 2 * py + pc


def _flip(v, bit):
    return 1 - v if bit else v


OTHER_CHIPS = ((1, 0), (0, 1), (1, 1))
N_GATHER_COPIES = 1 + 2 * len(OTHER_CHIPS)


class _Gather:
    def __init__(self, shard_ref, out_ref, cast_buf, send_sems, recv_sems, local_sem):
        self.shard, self.out, self.buf = shard_ref, out_ref, cast_buf
        self.send_sems, self.recv_sems, self.local_sem = send_sems, recv_sems, local_sem
        x, y, c = _position()
        self.c = c
        self.me, self.sibling = (x, y, c), (x, y, 1 - c)
        self.chips = [(_flip(x, fx), _flip(y, fy)) for fx, fy in OTHER_CHIPS]

    def _copy(self, k, block, to, from_buf=False):
        rows = self.out.at[_dev_index(*block)]
        return pltpu.make_async_remote_copy(src_ref=self.buf if from_buf else rows, dst_ref=rows,
                                            send_sem=self.send_sems.at[k], recv_sem=self.recv_sems.at[k],
                                            device_id=to, device_id_type=MESH)

    def _local(self):
        return pltpu.make_async_copy(self.buf, self.out.at[_dev_index(*self.me)], self.local_sem)

    def start(self):
        self.buf[...] = self.shard[...].astype(self.buf.dtype)
        self._local().start()
        for j, chip in enumerate(self.chips):
            self._copy(1 + j, self.me, (*chip, self.c), from_buf=True).start()
        self._copy(0, self.me, self.sibling, from_buf=True).start()

    def forward(self):
        for j, chip in enumerate(self.chips):
            self._copy(1 + j, (*chip, self.c), self.me).wait_recv()
            self._copy(1 + len(self.chips) + j, (*chip, self.c), self.sibling).start()

    def finish(self):
        self._copy(0, self.sibling, self.me).wait_recv()
        for j, chip in enumerate(self.chips):
            self._copy(1 + len(self.chips) + j, (*chip, 1 - self.c), self.me).wait_recv()
        for k in range(N_GATHER_COPIES):
            self._copy(k, self.me, self.sibling).wait_send()
        self._local().wait()


def _gather_specs(shards):
    whole = [pl.BlockSpec(w.shape, lambda *_, nd=w.ndim: (0,) * nd, pipeline_mode=pl.Buffered(1)) for w in shards]
    outs = [pl.BlockSpec(memory_space=pl.ANY) for _ in shards]
    shapes = [jax.ShapeDtypeStruct((N_DEV,) + w.shape, BF16) for w in shards]
    scratch = []
    for w in shards:
        scratch += [pltpu.VMEM(w.shape, BF16), pltpu.SemaphoreType.DMA((N_GATHER_COPIES,)),
                    pltpu.SemaphoreType.DMA((N_GATHER_COPIES,)), pltpu.SemaphoreType.DMA(())]
    return whole, outs, shapes, scratch


def _run_gathers(gathers, step, n_steps):
    @pl.when(step == 0)
    def _():
        for g in gathers:
            g.start()

    @pl.when(step == 3 * n_steps // 4)
    def _():
        for g in gathers:
            g.forward()

    @pl.when(step == n_steps - 1)
    def _():
        for g in gathers:
            g.finish()


class _ReduceScatter:
    def __init__(self, g_ref, out_ref, stage, load_sems, send_a, recv_a, send_b, recv_b, sa_send, sa_recv, sb_send, sb_recv):
        self.g, self.out, self.stage, self.load_sems = g_ref, out_ref, stage, load_sems
        self.send_a, self.recv_a, self.send_b, self.recv_b = send_a, recv_a, send_b, recv_b
        self.sems = (sa_send, sa_recv, sb_send, sb_recv)
        x, y, c = _position()
        self.c, self.sibling = c, (x, y, 1 - c)
        self.chips = [(x, y)] + [(_flip(x, fx), _flip(y, fy)) for fx, fy in OTHER_CHIPS]

    def _copy_a(self, j):
        return pltpu.make_async_remote_copy(src_ref=self.send_a.at[j], dst_ref=self.recv_a.at[j], send_sem=self.sems[0].at[j],
                                            recv_sem=self.sems[1].at[j], device_id=self.sibling, device_id_type=MESH)

    def _copy_b(self, j):
        return pltpu.make_async_remote_copy(src_ref=self.send_b.at[j], dst_ref=self.recv_b.at[j], send_sem=self.sems[2].at[j],
                                            recv_sem=self.sems[3].at[j], device_id=(*self.chips[1 + j], self.c),
                                            device_id_type=MESH)

    def _load(self, j, core):
        return pltpu.make_async_copy(self.g.at[_dev_index(*self.chips[j], core)], self.stage.at[j % 2], self.load_sems.at[j % 2])

    def start(self):
        self._load(0, 1 - self.c).start()
        for j in range(len(self.chips)):
            self._load(j, 1 - self.c).wait()
            if j + 1 < len(self.chips):
                self._load(j + 1, 1 - self.c).start()
            self.send_a[j] = self.stage[j % 2].astype(BF16)
            self._copy_a(j).start()

    def middle(self):
        self._load(0, self.c).start()
        for j in range(len(self.chips)):
            self._load(j, self.c).wait()
            if j + 1 < len(self.chips):
                self._load(j + 1, self.c).start()
            self._copy_a(j).wait_recv()
            part = self.stage[j % 2] + self.recv_a[j].astype(F32)
            if j == 0:
                self.out[...] = part
            else:
                self.send_b[j - 1] = part.astype(BF16)
                self._copy_b(j - 1).start()

    def finish(self):
        for j in range(len(OTHER_CHIPS)):
            self._copy_b(j).wait_recv()
            self.out[...] += self.recv_b[j].astype(F32)
        for j in range(len(self.chips)):
            self._copy_a(j).wait_send()
        for j in range(len(OTHER_CHIPS)):
            self._copy_b(j).wait_send()


N_SCATTER_SCRATCH = 10


def _scatter_specs(g8s):
    na, nb = 1 + len(OTHER_CHIPS), len(OTHER_CHIPS)
    ins = [pl.BlockSpec(memory_space=pl.ANY) for _ in g8s]
    outs = [pl.BlockSpec(g.shape[1:], lambda *_: (0, 0)) for g in g8s]
    shapes = [jax.ShapeDtypeStruct(g.shape[1:], F32) for g in g8s]
    scratch = []
    for g in g8s:
        blk = g.shape[1:]
        scratch += [pltpu.VMEM((2,) + blk, F32), pltpu.SemaphoreType.DMA((2,)), pltpu.VMEM((na,) + blk, BF16), pltpu.VMEM((na,) + blk, BF16),
                    pltpu.VMEM((nb,) + blk, BF16), pltpu.VMEM((nb,) + blk, BF16),
                    pltpu.SemaphoreType.DMA((na,)), pltpu.SemaphoreType.DMA((na,)),
                    pltpu.SemaphoreType.DMA((nb,)), pltpu.SemaphoreType.DMA((nb,))]
    return ins, outs, shapes, scratch


def _run_scatters(scatters, step, n_steps):
    @pl.when(step == 0)
    def _():
        for r in scatters:
            r.start()

    @pl.when(step == min(max(1, n_steps // 4), n_steps - 1))
    def _():
        for r in scatters:
            r.middle()

    @pl.when(step == n_steps - 1)
    def _():
        for r in scatters:
            r.finish()


def _mix_proj(x, g_mix, w_in_t, b_qkv, b_cin):
    s = x.shape[0]
    tm = _tile(s)

    def body(x_ref, g_ref, w_ref, bq_ref, bc_ref, qkv_ref, cin_ref, h1_ref):
        xv = x_ref[...]
        r = lax.rsqrt(jnp.mean(xv * xv, axis=-1, keepdims=True) + EPS)
        h = (xv * r * g_ref[...]).astype(BF16)
        h1_ref[...] = h
        qkv_ref[...] = _dot_nt(h, w_ref[0:QKV_COLS, :]) + bq_ref[...]
        cin_ref[...] = _dot_nt(h, w_ref[QKV_COLS:, :]) + bc_ref[...]

    return pl.pallas_call(
        body, grid=(s // tm,),
        in_specs=[pl.BlockSpec((tm, D_MODEL), lambda i: (i, 0)), _resident((1, D_MODEL)),
                  _resident((QKV_COLS + CIN_COLS, D_MODEL)), _resident((1, QKV_COLS)), _resident((1, CIN_COLS))],
        out_specs=[pl.BlockSpec((tm, QKV_COLS), lambda i: (i, 0)), pl.BlockSpec((tm, CIN_COLS), lambda i: (i, 0)),
                   pl.BlockSpec((tm, D_MODEL), lambda i: (i, 0))],
        out_shape=[jax.ShapeDtypeStruct((s, QKV_COLS), F32), jax.ShapeDtypeStruct((s, CIN_COLS), F32),
                   jax.ShapeDtypeStruct((s, D_MODEL), BF16)],
        compiler_params=_cparams("parallel"), name="mix_proj")(x, g_mix, w_in_t, b_qkv, b_cin)


def _kv_variants(kv_all, gk2, lo):
    k_all = kv_all[:, :LANES]
    v_all = kv_all[:, LANES:]
    kn_pre, rk = _head_norm(k_all, lo)
    kn = kn_pre * gk2
    kr = pltpu.roll(kn, HEAD_DIM, 1)
    vr = pltpu.roll(v_all, HEAD_DIM, 1)
    zero = jnp.zeros_like(kn)
    k_lo = [jnp.where(lo, kn, zero).astype(BF16), jnp.where(lo, kr, zero).astype(BF16)]
    k_hi = [jnp.where(lo, zero, kr).astype(BF16), jnp.where(lo, zero, kn).astype(BF16)]
    v_lo = [jnp.where(lo, v_all, zero).astype(BF16), jnp.where(lo, vr, zero).astype(BF16)]
    v_hi = [jnp.where(lo, zero, vr).astype(BF16), jnp.where(lo, zero, v_all).astype(BF16)]
    return k_lo, k_hi, v_lo, v_hi, kn_pre, rk


def _att_consts(first_tile, b):
    rows = 2 * ATT_BLOCK
    qi = lax.broadcasted_iota(jnp.int32, (rows, 2 * ATT_BLOCK), 0) % ATT_BLOCK
    kj = lax.broadcasted_iota(jnp.int32, (rows, 2 * ATT_BLOCK), 1)
    rel = qi + ATT_BLOCK - kj
    valid = (rel >= 0) & (rel < ATT_BLOCK)
    if b == 0:
        valid = valid & ((kj >= ATT_BLOCK) | jnp.logical_not(first_tile))
    return rel.astype(F32), valid


def _row_const(va, vb):
    top = lax.broadcasted_iota(jnp.int32, (2 * ATT_BLOCK, 1), 0) < ATT_BLOCK
    return jnp.where(top, va, vb)


def _probs(q2, k_op, rel, valid, slope, sink):
    sc = _dot_nt(q2, k_op) * (1.0 / math.sqrt(HEAD_DIM)) - slope * rel
    sc = jnp.where(valid, sc, NEG_INF)
    m = jnp.maximum(jnp.max(sc, axis=-1, keepdims=True), sink)
    p = jnp.exp(sc - m)
    e_sink = jnp.exp(sink - m)
    inv = 1.0 / (jnp.sum(p, axis=-1, keepdims=True) + e_sink)
    return p * inv, e_sink * inv


def _attn_fwd(qkv, gq2, gk2, sinks, shards):
    s = qkv.shape[0]
    tq = _tile(s)
    nb = tq // ATT_BLOCK
    ng = len(shards)
    g_in, g_out, g_shape, g_scratch = _gather_specs(shards)

    def body(q_ref, kv_ref, kvp_ref, gq_ref, gk_ref, sink_ref, *rest):
        out_ref = rest[ng]
        i = pl.program_id(0)
        _run_gathers([_Gather(rest[a], rest[ng + 1 + a], *rest[2 * ng + 1 + 4 * a:2 * ng + 5 + 4 * a]) for a in range(ng)],
                     i, s // tq)
        lo = _lo_mask((1, LANES))
        kv_all = jnp.concatenate([kvp_ref[...], kv_ref[...]], axis=0)
        k_lo, k_hi, v_lo, v_hi, _, _ = _kv_variants(kv_all, gk_ref[...], lo)
        for b in range(nb):
            rel, valid = _att_consts(i == 0, b)
            rows = slice(b * ATT_BLOCK, (b + 1) * ATT_BLOCK)
            keys = slice(b * ATT_BLOCK, (b + 2) * ATT_BLOCK)
            for kvh in range(N_KV_HEADS):
                pairs = (2 * kvh, 2 * kvh + 1)
                q2 = jnp.concatenate([q_ref[rows, p * LANES:(p + 1) * LANES] for p in pairs], axis=0)
                qn, _ = _head_norm(q2, lo)
                q2 = (qn * gq_ref[...]).astype(BF16)
                out = None
                for odd, (k_op, v_op) in enumerate(((k_lo[kvh][keys], v_lo[kvh][keys]), (k_hi[kvh][keys], v_hi[kvh][keys]))):
                    ha, hb = 2 * pairs[0] + odd, 2 * pairs[1] + odd
                    p, _ = _probs(q2, k_op, rel, valid, _row_const(SLOPES[ha], SLOPES[hb]),
                                  _row_const(sink_ref[ha], sink_ref[hb]))
                    o = _dot(p.astype(BF16), v_op)
                    out = o if out is None else out + o
                for n, p in enumerate(pairs):
                    out_ref[rows, p * LANES:(p + 1) * LANES] = out[n * ATT_BLOCK:(n + 1) * ATT_BLOCK].astype(BF16)

    return pl.pallas_call(
        body, grid=(s // tq,),
        in_specs=[pl.BlockSpec((tq, Q_COLS), lambda i: (i, 0)),
                  pl.BlockSpec((tq, 2 * KV_COLS), lambda i: (i, 2)),
                  pl.BlockSpec((ATT_BLOCK, 2 * KV_COLS), lambda i: (jnp.maximum(i * nb - 1, 0), 2)),
                  _resident((1, LANES)), _resident((1, LANES)),
                  pl.BlockSpec(memory_space=pltpu.SMEM)] + g_in,
        out_specs=[pl.BlockSpec((tq, Q_COLS), lambda i: (i, 0))] + g_out,
        out_shape=[jax.ShapeDtypeStruct((s, Q_COLS), BF16)] + g_shape,
        scratch_shapes=g_scratch,
        compiler_params=_cparams("arbitrary"), name="attn_fwd")(qkv, qkv, qkv, gq2, gk2, sinks, *shards)


def _group_stats(c1, lo):
    mu = _half_sums(c1, lo) * (1.0 / HEAD_DIM)
    d = c1 - mu
    rstd = lax.rsqrt(_half_sums(d * d, lo) * (1.0 / HEAD_DIM) + EPS)
    return d * rstd, rstd


def _rows(ref, first_row, n):
    return ref[pl.ds(first_row, n, stride=1), :].reshape(n // SUBLANES, SUBLANES, LANES)


def _conv_fwd(cin, cw8, cb, gain, bias, shards):
    s = cin.shape[0]
    tm = _tile(s)
    rc = 64
    nchunk = CONV_WIDTH // LANES
    lead = CONV_HALO - (CONV_KERNEL - 1)
    ng = len(shards)
    g_in, g_out, g_shape, g_scratch = _gather_specs(shards)

    def body(cin_ref, cw_ref, cb_ref, gain_ref, bias_ref, *rest):
        c3_ref, c1_ref, ext_ref = rest[ng], rest[ng + 1], rest[2 * ng + 2]
        _run_gathers([_Gather(rest[a], rest[ng + 2 + a], *rest[2 * ng + 3 + 4 * a:2 * ng + 7 + 4 * a]) for a in range(ng)],
                     pl.program_id(0), s // tm)

        @pl.when(pl.program_id(0) == 0)
        def _():
            ext_ref[:, 0:CONV_HALO, :] = jnp.zeros((nchunk, CONV_HALO, LANES), F32)

        lo = _lo_mask((1, LANES))
        for cc in range(nchunk):
            cols = slice(cc * LANES, (cc + 1) * LANES)
            gcols = slice(CONV_WIDTH + cc * LANES, CONV_WIDTH + (cc + 1) * LANES)
            ext_ref[cc, CONV_HALO:CONV_HALO + tm, :] = cin_ref[:, cols] * _sigmoid(cin_ref[:, gcols])
            ext = ext_ref.at[cc]
            for r in range(tm // rc):
                rows = slice(r * rc, (r + 1) * rc)
                acc = jnp.zeros((rc // SUBLANES, SUBLANES, LANES), F32)
                for k in range(CONV_KERNEL):
                    acc = acc + cw_ref[k * SUBLANES:(k + 1) * SUBLANES, cols][None] * _rows(ext, r * rc + lead + k, rc)
                c1 = acc.reshape(rc, LANES) + cb_ref[:, cols]
                c1_ref[cc, rows, :] = c1
                nrm, _ = _group_stats(c1, lo)
                c2 = nrm * gain_ref[:, cols] + bias_ref[:, cols]
                c3_ref[rows, cols] = (c2 * _sigmoid(c2)).astype(BF16)
        ext_ref[:, 0:CONV_HALO, :] = ext_ref[:, tm:tm + CONV_HALO, :]

    return pl.pallas_call(
        body, grid=(s // tm,),
        in_specs=[pl.BlockSpec((tm, CIN_COLS), lambda i: (i, 0)), _resident((CONV_KERNEL * SUBLANES, CONV_WIDTH)),
                  _resident((1, CONV_WIDTH)), _resident((1, CONV_WIDTH)), _resident((1, CONV_WIDTH))] + g_in,
        out_specs=[pl.BlockSpec((tm, CONV_WIDTH), lambda i: (i, 0)), pl.BlockSpec((nchunk, tm, LANES), lambda i: (0, i, 0))] + g_out,
        out_shape=[jax.ShapeDtypeStruct((s, CONV_WIDTH), BF16), jax.ShapeDtypeStruct((nchunk, s, LANES), F32)] + g_shape,
        scratch_shapes=[pltpu.VMEM((nchunk, tm + CONV_HALO, LANES), F32)] + g_scratch,
        compiler_params=_cparams("arbitrary"), name="conv_fwd")(cin, cw8, cb, gain, bias, *shards)


def _out_proj(x, attn, c3, wo_a, wo_c, b_out, g_ffn):
    s = x.shape[0]
    tm = _tile(s)

    def body(x_ref, a_ref, c_ref, wa_ref, wc_ref, b_ref, g_ref, x2_ref, h2_ref):
        x2 = x_ref[...] + _dot(a_ref[...], wa_ref[...]) + _dot(c_ref[...], wc_ref[...]) + b_ref[...]
        x2_ref[...] = x2
        r = lax.rsqrt(jnp.mean(x2 * x2, axis=-1, keepdims=True) + EPS)
        h2_ref[...] = (x2 * r * g_ref[...]).astype(BF16)

    return pl.pallas_call(
        body, grid=(s // tm,),
        in_specs=[pl.BlockSpec((tm, D_MODEL), lambda i: (i, 0)), pl.BlockSpec((tm, Q_COLS), lambda i: (i, 0)),
                  pl.BlockSpec((tm, CONV_WIDTH), lambda i: (i, 0)),
                  pl.BlockSpec((Q_COLS, D_MODEL), lambda i: (0, 0), pipeline_mode=pl.Buffered(1)),
                  pl.BlockSpec((CONV_WIDTH, D_MODEL), lambda i: (1, 0), pipeline_mode=pl.Buffered(1)),
                  _resident((1, D_MODEL)), _resident((1, D_MODEL))],
        out_specs=[pl.BlockSpec((tm, D_MODEL), lambda i: (i, 0)), pl.BlockSpec((tm, D_MODEL), lambda i: (i, 0))],
        out_shape=[jax.ShapeDtypeStruct((s, D_MODEL), F32), jax.ShapeDtypeStruct((s, D_MODEL), BF16)],
        compiler_params=_cparams("parallel"), name="out_proj")(x, attn, c3, wo_a, wo_c, b_out, g_ffn)


FF_LANE_CHUNKS = -(-FF_CHUNK // LANES)
FF_PADDED = FF_LANE_CHUNKS * LANES


def _tap(ref, first_row, n):
    return ref[pl.ds(first_row, n, stride=1), :]


def _ffn_fwd(h2, x2, target, w_up, fw, fb, w_down):
    s = h2.shape[0]
    tm = _tile(s)
    hal = SUBLANES
    rc = min(128, tm)

    def body(h_ref, x2_ref, t_ref, wu_ref, fw_ref, fb_ref, wd_ref, up0_ref, gu_ref, act_ref, dy_ref, dyb_ref, loss_ref,
             ext_ref, carry_ref, act_buf, y_ref):
        i, ci = pl.program_id(0), pl.program_id(1)

        @pl.when((i == 0) & (ci == 0))
        def _():
            carry_ref[...] = jnp.zeros(carry_ref.shape, F32)
            ext_ref[...] = jnp.zeros(ext_ref.shape, F32)
            act_buf[...] = jnp.zeros(act_buf.shape, BF16)
            loss_ref[...] = jnp.zeros((1, 1), F32)

        @pl.when(ci == 0)
        def _():
            y_ref[...] = x2_ref[...]

        ws = (fw_ref[ci], fw_ref[ci + N_FF_PAIRS])
        bs = (fb_ref[ci], fb_ref[ci + N_FF_PAIRS])
        half_rows = (slice(0, tm // 2), slice(tm // 2, tm))
        n_grp = len(FF_COLS)

        def up_slices(grp):
            lo_c, hi_c = FF_COLS[grp]
            chunks = range(lo_c // LANES, -(-hi_c // LANES))

            def make(half, n, rows):
                def run():
                    c = ci + half * N_FF_PAIRS
                    u0 = _dot_nt(h_ref[rows, :], wu_ref[c, lo_c:hi_c, :])
                    up0_ref[half, 0, rows, lo_c:hi_c] = u0.astype(BF16)
                    if hi_c == FF_CHUNK:
                        up0_ref[half, 0, rows, FF_CHUNK:] = jnp.zeros((u0.shape[0], FF_PADDED - FF_CHUNK), BF16)
                    for j in chunks:
                        w = min(LANES, hi_c - j * LANES)
                        if n == 0:
                            ext_ref[half, j, 0:hal, 0:w] = carry_ref[c, :, j * LANES:j * LANES + w]
                        ext_ref[half, j, hal + rows.start:hal + rows.stop, 0:w] = u0[:, j * LANES - lo_c:j * LANES - lo_c + w]
                    if n == len(half_rows) - 1:
                        carry_ref[c, :, lo_c:hi_c] = u0[u0.shape[0] - hal:, :]
                return run
            return [make(half, n, rows) for half in range(2) for n, rows in enumerate(half_rows)]

        def down_slices(grp):
            lo_c, hi_c = FF_COLS[grp]

            def make(rows):
                def run():
                    y_ref[rows, :] += _dot(act_buf[rows, lo_c:hi_c], wd_ref[ci, lo_c:hi_c, :])
                return run
            return [make(rows) for rows in half_rows]

        def vector_blocks(grp):
            lo_c, hi_c = FF_COLS[grp]
            blocks = []
            for j in range(lo_c // LANES, -(-hi_c // LANES)):
                lanes = slice(j * LANES, (j + 1) * LANES)

                def gate(r, lanes=lanes, j=j):
                    base = r * rc
                    ups = []
                    for half in range(2):
                        e, w = ext_ref.at[half, j], ws[half]
                        ups.append(w[0:1, lanes] * _tap(e, base + hal - 2, rc) + w[1:2, lanes] * _tap(e, base + hal - 1, rc)
                                   + w[2:3, lanes] * _tap(e, base + hal, rc) + bs[half][:, lanes])
                    g, u = ups
                    gu_ref[0, 0, base:base + rc, lanes] = g.astype(BF16)
                    gu_ref[1, 0, base:base + rc, lanes] = u.astype(BF16)
                    act_buf[base:base + rc, lanes] = (g * _sigmoid(g) * u).astype(BF16)

                blocks += [functools.partial(gate, r) for r in range(tm // rc)]

            def finish():
                act_ref[0, :, lo_c:hi_c] = act_buf[:, lo_c:hi_c]
            blocks.append(finish)
            return blocks

        for run in up_slices(0):
            run()
        for grp in range(n_grp):
            matmuls = (up_slices(grp + 1) if grp + 1 < n_grp else []) + (down_slices(grp - 1) if grp > 0 else [])
            blocks = vector_blocks(grp)
            every = max(1, len(blocks) // (len(matmuls) + 1))
            for n, run in enumerate(blocks):
                run()
                if n % every == every - 1 and matmuls:
                    matmuls.pop(0)()
            for run in matmuls:
                run()
        for run in down_slices(n_grp - 1):
            run()

        @pl.when(ci == N_FF_PAIRS - 1)
        def _():
            e = y_ref[...] - t_ref[...]
            dy_ref[...] = e * (1.0 / D_MODEL)
            dyb_ref[...] = (e * (1.0 / D_MODEL)).astype(BF16)
            loss_ref[...] += (0.5 / D_MODEL) * jnp.sum(e * e).reshape(1, 1)

    tok = lambda i, ci: (i, 0)
    return pl.pallas_call(
        body, grid=(s // tm, N_FF_PAIRS),
        in_specs=[pl.BlockSpec((tm, D_MODEL), tok), pl.BlockSpec((tm, D_MODEL), tok), pl.BlockSpec((tm, D_MODEL), tok),
                  _resident((N_DEV, FF_CHUNK, D_MODEL)), _resident((N_DEV, 3, FF_PADDED)), _resident((N_DEV, 1, FF_PADDED)),
                  _resident((N_FF_PAIRS, FF_CHUNK, D_MODEL))],
        out_specs=[pl.BlockSpec((2, 1, tm, FF_PADDED), lambda i, ci: (0, ci, i, 0)),
                   pl.BlockSpec((2, 1, tm, FF_PADDED), lambda i, ci: (0, ci, i, 0)),
                   pl.BlockSpec((1, tm, FF_CHUNK), lambda i, ci: (ci, i, 0)),
                   pl.BlockSpec((tm, D_MODEL), tok), pl.BlockSpec((tm, D_MODEL), tok), pl.BlockSpec((1, 1), lambda i, ci: (0, 0))],
        out_shape=[jax.ShapeDtypeStruct((2, N_FF_PAIRS, s, FF_PADDED), BF16), jax.ShapeDtypeStruct((2, N_FF_PAIRS, s, FF_PADDED), BF16),
                   jax.ShapeDtypeStruct((N_FF_PAIRS, s, FF_CHUNK), BF16), jax.ShapeDtypeStruct((s, D_MODEL), F32),
                   jax.ShapeDtypeStruct((s, D_MODEL), BF16), jax.ShapeDtypeStruct((1, 1), F32)],
        scratch_shapes=[pltpu.VMEM((2, FF_LANE_CHUNKS, tm + hal, LANES), F32), pltpu.VMEM((N_DEV, hal, FF_CHUNK), F32),
                        pltpu.VMEM((tm, FF_PADDED), BF16), pltpu.VMEM((tm, D_MODEL), F32)],
        compiler_params=_cparams("arbitrary", "arbitrary"), name="ffn_fwd")(h2, x2, target, w_up, fw, fb, w_down)


def _ffn_bwd(dyb, up0, gu, w_up, fw, w_down):
    s = dyb.shape[0]
    tm = _tile(s)
    nt = s // tm
    nxt = SUBLANES
    rc = min(128, tm)

    def body(dy_ref, up0_ref, gu_ref, wu_ref, fw_ref, wd_ref,
             dup0_ref, dh2_ref, dfw_ref, dfb_ref, dext_ref, carry_ref, dact_buf, dup0_buf, dh2_acc):
        i, ci = pl.program_id(0), pl.program_id(1)

        @pl.when((i == 0) & (ci == 0))
        def _():
            for ref in (carry_ref, dfw_ref, dfb_ref, dext_ref, dact_buf):
                ref[...] = jnp.zeros(ref.shape, F32)
            dup0_buf[...] = jnp.zeros(dup0_buf.shape, BF16)

        @pl.when(ci == 0)
        def _():
            dh2_acc[...] = jnp.zeros(dh2_acc.shape, F32)

        ws = (fw_ref[ci], fw_ref[ci + N_FF_PAIRS])
        fold = lambda v: jnp.sum(v.reshape(rc // SUBLANES, SUBLANES, LANES), axis=0)
        half_rows = (slice(0, tm // 2), slice(tm // 2, tm))
        n_grp = len(FF_COLS)

        def dact_slices(grp):
            lo_c, hi_c = FF_COLS[grp]

            def make(rows):
                def run():
                    dact_buf[rows, lo_c:hi_c] = _dot_nt(dy_ref[rows, :], wd_ref[ci, lo_c:hi_c, :])
                return run
            return [make(rows) for rows in half_rows]

        def dh2_slices(grp):
            lo_c, hi_c = FF_COLS[grp]

            def make(half, rows):
                def run():
                    c = ci + half * N_FF_PAIRS
                    dh2_acc[rows, :] += _dot(dup0_buf[half, rows, lo_c:hi_c], wu_ref[c, lo_c:hi_c, :])
                return run
            return [make(half, rows) for half in range(2) for rows in half_rows]

        def vector_blocks(grp):
            lo_c, hi_c = FF_COLS[grp]
            chunks = range(lo_c // LANES, -(-hi_c // LANES))
            blocks = []

            def stage():
                for half in range(2):
                    c = ci + half * N_FF_PAIRS
                    for j in chunks:
                        dext_ref[half, j, tm:tm + nxt, :] = carry_ref[c, :, j * LANES:(j + 1) * LANES]
            blocks.append(stage)
            for j in chunks:
                lanes = slice(j * LANES, (j + 1) * LANES)
                acc = [jnp.zeros((SUBLANES, LANES), F32)] * 8

                def grads(r, lanes=lanes, j=j, acc=acc):
                    base = r * rc
                    g = gu_ref[0, 0, base:base + rc, lanes].astype(F32)
                    u = gu_ref[1, 0, base:base + rc, lanes].astype(F32)
                    sg = _sigmoid(g)
                    silu = g * sg
                    dact = dact_buf[base:base + rc, lanes]
                    ds = (dact * u * (sg + silu - silu * sg), dact * silu)
                    for half in range(2):
                        dext_ref[half, j, base:base + rc, :] = ds[half]
                        acc[4 * half] = acc[4 * half] + fold(ds[half])

                def conv_back(r, lanes=lanes, j=j, acc=acc):
                    base = r * rc
                    for half in range(2):
                        d, w = dext_ref.at[half, j], ws[half]
                        taps = [_tap(d, base + k, rc) for k in range(3)]
                        dup0 = w[2:3, lanes] * taps[0] + w[1:2, lanes] * taps[1] + w[0:1, lanes] * taps[2]
                        dup0_buf[half, base:base + rc, lanes] = dup0.astype(BF16)
                        u0 = up0_ref[half, 0, base:base + rc, lanes].astype(F32)
                        for k in range(3):
                            acc[4 * half + 1 + k] = acc[4 * half + 1 + k] + fold(taps[2 - k] * u0)

                def sums(lanes=lanes, j=j, acc=acc):
                    for half in range(2):
                        c = ci + half * N_FF_PAIRS
                        carry_ref[c, :, lanes] = dext_ref[half, j, 0:nxt, :]
                        dfb_ref[c, :, lanes] += jnp.sum(acc[4 * half], axis=0, keepdims=True)
                        dfw_ref[c, :, lanes] += jnp.concatenate(
                            [jnp.sum(acc[4 * half + 1 + k], axis=0, keepdims=True) for k in range(3)], axis=0)

                blocks += [functools.partial(grads, r) for r in range(tm // rc)]
                blocks += [functools.partial(conv_back, r) for r in range(tm // rc)] + [sums]

            def finish():
                for half in range(2):
                    dup0_ref[half, 0, :, lo_c:hi_c] = dup0_buf[half, :, lo_c:hi_c]
            blocks.append(finish)
            return blocks

        for run in dact_slices(0):
            run()
        for grp in range(n_grp):
            matmuls = (dact_slices(grp + 1) if grp + 1 < n_grp else []) + (dh2_slices(grp - 1) if grp > 0 else [])
            blocks = vector_blocks(grp)
            every = max(1, len(blocks) // (len(matmuls) + 1))
            for n, run in enumerate(blocks):
                run()
                if n % every == every - 1 and matmuls:
                    matmuls.pop(0)()
            for run in matmuls:
                run()
        for run in dh2_slices(n_grp - 1):
            run()

        @pl.when(ci == N_FF_PAIRS - 1)
        def _():
            dh2_ref[...] = dh2_acc[...].astype(BF16)

    tok = lambda i, ci: (nt - 1 - i, 0)
    acc = lambda shape: pl.BlockSpec(shape, lambda i, ci: (0,) * len(shape))
    saved = pl.BlockSpec((2, 1, tm, FF_PADDED), lambda i, ci: (0, ci, nt - 1 - i, 0))
    return pl.pallas_call(
        body, grid=(nt, N_FF_PAIRS),
        in_specs=[pl.BlockSpec((tm, D_MODEL), tok), saved, saved,
                  _resident((N_DEV, FF_CHUNK, D_MODEL)), _resident((N_DEV, 3, FF_PADDED)),
                  _resident((N_FF_PAIRS, FF_CHUNK, D_MODEL))],
        out_specs=[pl.BlockSpec((2, 1, tm, FF_CHUNK), lambda i, ci: (0, ci, nt - 1 - i, 0)),
                   pl.BlockSpec((tm, D_MODEL), tok), acc((N_DEV, 3, FF_PADDED)), acc((N_DEV, 1, FF_PADDED))],
        out_shape=[jax.ShapeDtypeStruct((2, N_FF_PAIRS, s, FF_CHUNK), BF16), jax.ShapeDtypeStruct((s, D_MODEL), BF16),
                   jax.ShapeDtypeStruct((N_DEV, 3, FF_PADDED), F32), jax.ShapeDtypeStruct((N_DEV, 1, FF_PADDED), F32)],
        scratch_shapes=[pltpu.VMEM((2, FF_LANE_CHUNKS, tm + nxt, LANES), F32), pltpu.VMEM((N_DEV, nxt, FF_PADDED), F32),
                        pltpu.VMEM((tm, FF_PADDED), F32), pltpu.VMEM((2, tm, FF_PADDED), BF16), pltpu.VMEM((tm, D_MODEL), F32)],
        compiler_params=_cparams("arbitrary", "arbitrary"), name="ffn_bwd")(dyb, up0, gu, w_up, fw, w_down)


def _ffn_norm_bwd(dh2, dy, x2, g_ffn, w_out):
    s = dy.shape[0]
    tm = _tile(s)

    def body(dh_ref, dy_ref, x2_ref, g_ref, wo_ref, dx2_ref, dmix_ref, dg_ref, dbo_ref):
        @pl.when(pl.program_id(0) == 0)
        def _():
            dg_ref[...] = jnp.zeros(dg_ref.shape, F32)
            dbo_ref[...] = jnp.zeros(dbo_ref.shape, F32)

        x2v = x2_ref[...]
        r = lax.rsqrt(jnp.mean(x2v * x2v, axis=-1, keepdims=True) + EPS)
        n2 = x2v * r
        dh2 = dh_ref[...].astype(F32)
        dg_ref[...] += jnp.sum(dh2 * n2, axis=0, keepdims=True)
        dn = dh2 * g_ref[...]
        dx2 = dy_ref[...] + r * (dn - n2 * jnp.mean(dn * n2, axis=-1, keepdims=True))
        dx2_ref[...] = dx2
        dbo_ref[...] += jnp.sum(dx2, axis=0, keepdims=True)
        dmix_ref[...] = _dot_nt(dx2.astype(BF16), wo_ref[...]).astype(BF16)

    tok = pl.BlockSpec((tm, D_MODEL), lambda i: (i, 0))
    vec = pl.BlockSpec((1, D_MODEL), lambda i: (0, 0))
    return pl.pallas_call(
        body, grid=(s // tm,),
        in_specs=[tok, tok, tok, _resident((1, D_MODEL)), _resident((D_MODEL, D_MODEL))],
        out_specs=[tok, tok, vec, vec],
        out_shape=[jax.ShapeDtypeStruct((s, D_MODEL), F32), jax.ShapeDtypeStruct((s, D_MODEL), BF16),
                   jax.ShapeDtypeStruct((1, D_MODEL), F32), jax.ShapeDtypeStruct((1, D_MODEL), F32)],
        compiler_params=_cparams("arbitrary"), name="ffn_norm_bwd")(dh2, dy, x2, g_ffn, w_out)


def _conv_bwd(dmixed, c1, cin, cw8, gain, bias, g8s):
    ns = len(g8s)
    s_in, s_out, s_shape, s_scratch = _scatter_specs(g8s)
    s = cin.shape[0]
    tm = _tile(s)
    nt = s // tm
    rc = 64
    rn = min(256, tm)
    hal = CONV_HALO
    nchunk = CONV_WIDTH // LANES

    def body(dc3_ref, dc3n_ref, c1_ref, c1n_ref, cin_ref, cw_ref, gain_ref, bias_ref, *rest):
        dcin_ref, dcw_ref, dcb_ref, dgain_ref, dbias_ref, dbcin_ref = rest[ns:ns + 6]
        dc1_ext, dcw8 = rest[2 * ns + 6:2 * ns + 8]
        i = pl.program_id(0)
        first, last = i == 0, i == nt - 1
        own = rest[2 * ns + 8:]
        _run_scatters([_ReduceScatter(rest[a], rest[ns + 6 + a], *own[N_SCATTER_SCRATCH * a:N_SCATTER_SCRATCH * (a + 1)])
                       for a in range(ns)], i, nt)

        @pl.when(first)
        def _():
            for ref in (dcw8, dcb_ref, dgain_ref, dbias_ref, dbcin_ref):
                ref[...] = jnp.zeros(ref.shape, F32)

        lo = _lo_mask((1, LANES))

        def norm_bwd(dc3, c1v, cols):
            nrm, rstd = _group_stats(c1v, lo)
            c2 = nrm * gain_ref[:, cols] + bias_ref[:, cols]
            sg = _sigmoid(c2)
            dc2 = dc3 * (sg * (1.0 + c2 * (1.0 - sg)))
            dn = dc2 * gain_ref[:, cols]
            inv = 1.0 / HEAD_DIM
            dc1 = rstd * (dn - _half_sums(dn, lo) * inv - nrm * (_half_sums(dn * nrm, lo) * inv))
            return dc1, dc2, nrm

        def row_sum(v):
            return jnp.sum(v, axis=0, keepdims=True)

        for cc in range(nchunk):
            cols = slice(cc * LANES, (cc + 1) * LANES)
            gcols = slice(CONV_WIDTH + cc * LANES, CONV_WIDTH + (cc + 1) * LANES)
            d1e = dc1_ext.at[cc]
            dc1n, _, _ = norm_bwd(dc3n_ref[:, cols].astype(F32), c1n_ref[cc], cols)
            d1e[tm:tm + hal, :] = jnp.where(last, 0.0, dc1n)

            for r in range(tm // rn):
                rows = slice(r * rn, (r + 1) * rn)
                dc1, dc2, nrm = norm_bwd(dc3_ref[rows, cols].astype(F32), c1_ref[cc, rows, :], cols)
                d1e[rows, :] = dc1
                dgain_ref[:, cols] += row_sum(dc2 * nrm)
                dbias_ref[:, cols] += row_sum(dc2)
                dcb_ref[:, cols] += row_sum(dc1)
            zero = jnp.zeros((1, LANES), F32)

            def taps(r, sums):
                rows = pl.ds(pl.multiple_of(r * rc, rc), rc)
                a = cin_ref[rows, cols]
                sg = _sigmoid(cin_ref[rows, gcols])
                c0 = (a * sg).reshape(rc // SUBLANES, SUBLANES, LANES)
                dc0 = jnp.zeros((rc // SUBLANES, SUBLANES, LANES), F32)
                for k in range(CONV_KERNEL):
                    krows = slice(k * SUBLANES, (k + 1) * SUBLANES)
                    shifted = _rows(d1e, r * rc + CONV_KERNEL - 1 - k, rc)
                    dc0 = dc0 + cw_ref[krows, cols][None] * shifted
                    dcw8[krows, cols] += jnp.sum(shifted * c0, axis=0)
                dc0 = dc0.reshape(rc, LANES)
                da = dc0 * sg
                dgate = dc0 * a * sg * (1.0 - sg)
                dcin_ref[rows, cols] = da.astype(BF16)
                dcin_ref[rows, gcols] = dgate.astype(BF16)
                return sums[0] + row_sum(da), sums[1] + row_sum(dgate)

            sums = lax.fori_loop(0, tm // rc, taps, (zero, zero))
            dbcin_ref[:, cols] += sums[0]
            dbcin_ref[:, gcols] += sums[1]

        @pl.when(last)
        def _():
            for k in range(CONV_KERNEL):
                dcw_ref[k:k + 1, :] = jnp.sum(dcw8[k * SUBLANES:(k + 1) * SUBLANES, :], axis=0, keepdims=True)

    nh = tm // hal
    acc = lambda shape: pl.BlockSpec(shape, lambda i: (0,) * len(shape))
    return pl.pallas_call(
        body, grid=(nt,),
        in_specs=[pl.BlockSpec((tm, CONV_WIDTH), lambda i: (i, 1)),
                  pl.BlockSpec((hal, CONV_WIDTH), lambda i: (jnp.minimum((i + 1) * nh, s // hal - 1), 1)),
                  pl.BlockSpec((nchunk, tm, LANES), lambda i: (0, i, 0)),
                  pl.BlockSpec((nchunk, hal, LANES), lambda i: (0, jnp.minimum((i + 1) * nh, s // hal - 1), 0)),
                  pl.BlockSpec((tm, CIN_COLS), lambda i: (i, 0)),
                  _resident((CONV_KERNEL * SUBLANES, CONV_WIDTH)), _resident((1, CONV_WIDTH)), _resident((1, CONV_WIDTH))] + s_in,
        out_specs=[pl.BlockSpec((tm, CIN_COLS), lambda i: (i, 0)), acc((CONV_KERNEL, CONV_WIDTH)), acc((1, CONV_WIDTH)),
                   acc((1, CONV_WIDTH)), acc((1, CONV_WIDTH)), acc((1, CIN_COLS))] + s_out,
        out_shape=[jax.ShapeDtypeStruct((s, CIN_COLS), BF16), jax.ShapeDtypeStruct((CONV_KERNEL, CONV_WIDTH), F32),
                   jax.ShapeDtypeStruct((1, CONV_WIDTH), F32), jax.ShapeDtypeStruct((1, CONV_WIDTH), F32),
                   jax.ShapeDtypeStruct((1, CONV_WIDTH), F32), jax.ShapeDtypeStruct((1, CIN_COLS), F32)] + s_shape,
        scratch_shapes=[pltpu.VMEM((nchunk, tm + hal, LANES), F32),
                        pltpu.VMEM((CONV_KERNEL * SUBLANES, CONV_WIDTH), F32)] + s_scratch,
        compiler_params=_cparams("arbitrary"), name="conv_bwd")(dmixed, dmixed, c1, c1, cin, cw8, gain, bias, *g8s)


def _attn_bwd(qkv, dmixed, gq2, gk2, sinks, g8s):
    ns = len(g8s)
    s_in, s_out, s_shape, s_scratch = _scatter_specs(g8s)
    s = qkv.shape[0]
    tq = _tile(s)
    nb = tq // ATT_BLOCK
    nt = s // tq

    def body(q_ref, kv_ref, kvp_ref, do_ref, gq_ref, gk_ref, sink_ref, *rest):
        dqkv_ref, dgq_ref, dgk_ref, dsink_ref, dbqkv_ref = rest[ns:ns + 5]
        dk_acc, dv_acc, carry_k, carry_v = rest[2 * ns + 5:2 * ns + 9]
        i = pl.program_id(0)
        t = nt - 1 - i
        own = rest[2 * ns + 9:]
        _run_scatters([_ReduceScatter(rest[a], rest[ns + 5 + a], *own[N_SCATTER_SCRATCH * a:N_SCATTER_SCRATCH * (a + 1)])
                       for a in range(ns)], i, nt)

        @pl.when(i == 0)
        def _():
            for ref in (carry_k, carry_v, dgq_ref, dgk_ref, dsink_ref, dbqkv_ref):
                ref[...] = jnp.zeros(ref.shape, F32)

        lo = _lo_mask((1, LANES))
        lane_id = lax.broadcasted_iota(jnp.int32, (1, LANES), 1)
        kv_all = jnp.concatenate([kvp_ref[...], kv_ref[...]], axis=0)
        k_lo, k_hi, v_lo, v_hi, kn_pre, rk = _kv_variants(kv_all, gk_ref[...], lo)
        for acc_ref, carry in ((dk_acc, carry_k), (dv_acc, carry_v)):
            acc_ref[:, 0:tq, :] = jnp.zeros((N_KV_HEADS, tq, LANES), F32)
            acc_ref[:, tq:tq + ATT_BLOCK, :] = carry[...]
        dsink = jnp.zeros((1, LANES), F32)
        dgq = jnp.zeros((1, LANES), F32)
        gq = gq_ref[...]
        for b in range(nb):
            rel, valid = _att_consts(t == 0, b)
            rows = slice(b * ATT_BLOCK, (b + 1) * ATT_BLOCK)
            keys = slice(b * ATT_BLOCK, (b + 2) * ATT_BLOCK)
            for kvh in range(N_KV_HEADS):
                pairs = (2 * kvh, 2 * kvh + 1)
                q_raw = jnp.concatenate([q_ref[rows, p * LANES:(p + 1) * LANES] for p in pairs], axis=0)
                qn_pre, rq = _head_norm(q_raw, lo)
                q2 = (qn_pre * gq).astype(BF16)
                do2 = jnp.concatenate([do_ref[rows, p * LANES:(p + 1) * LANES] for p in pairs], axis=0).astype(BF16)
                dq2 = jnp.zeros((2 * ATT_BLOCK, LANES), F32)
                for odd, (k_op, v_op) in enumerate(((k_lo[kvh][keys], v_lo[kvh][keys]), (k_hi[kvh][keys], v_hi[kvh][keys]))):
                    ha, hb = 2 * pairs[0] + odd, 2 * pairs[1] + odd
                    p, p_sink = _probs(q2, k_op, rel, valid, _row_const(SLOPES[ha], SLOPES[hb]),
                                       _row_const(sink_ref[ha], sink_ref[hb]))
                    dp = _dot_nt(do2, v_op)
                    delta = jnp.sum(p * dp, axis=-1, keepdims=True)
                    ds = (p * (dp - delta) * (1.0 / math.sqrt(HEAD_DIM))).astype(BF16)
                    dsk = p_sink * delta
                    dsink = dsink - jnp.where(lane_id == ha, jnp.sum(dsk[0:ATT_BLOCK]), 0.0) \
                        - jnp.where(lane_id == hb, jnp.sum(dsk[ATT_BLOCK:]), 0.0)
                    dq2 = dq2 + _dot(ds, k_op)
                    half = lo if odd == 0 else jnp.logical_not(lo)
                    dk_acc[kvh, keys, :] += jnp.where(half, _dot_tn(ds, q2), 0.0)
                    dv_acc[kvh, keys, :] += jnp.where(half, _dot_tn(p.astype(BF16), do2), 0.0)
                dgq = dgq + jnp.sum(dq2 * qn_pre, axis=0, keepdims=True)
                dq_raw = _head_norm_bwd(dq2 * gq, qn_pre, rq, lo)
                for n, p_ in enumerate(pairs):
                    blk = dq_raw[n * ATT_BLOCK:(n + 1) * ATT_BLOCK]
                    dqkv_ref[rows, p_ * LANES:(p_ + 1) * LANES] = blk.astype(BF16)
                    dbqkv_ref[:, p_ * LANES:(p_ + 1) * LANES] += jnp.sum(blk, axis=0, keepdims=True)
        carry_k[...] = dk_acc[:, 0:ATT_BLOCK, :]
        carry_v[...] = dv_acc[:, 0:ATT_BLOCK, :]

        def fold(acc_ref):
            both = []
            for kvh in range(N_KV_HEADS):
                a = acc_ref[kvh, ATT_BLOCK:ATT_BLOCK + tq, :]
                both.append(a + pltpu.roll(a, HEAD_DIM, 1))
            return jnp.where(lo, both[0], both[1])

        dkn = fold(dk_acc)
        dv = fold(dv_acc)
        kn_c, rk_c = kn_pre[ATT_BLOCK:], rk[ATT_BLOCK:]
        dgk_ref[...] += jnp.sum(dkn * kn_c, axis=0, keepdims=True)
        dk_raw = _head_norm_bwd(dkn * gk_ref[...], kn_c, rk_c, lo)
        dqkv_ref[:, Q_COLS:Q_COLS + KV_COLS] = dk_raw.astype(BF16)
        dqkv_ref[:, Q_COLS + KV_COLS:] = dv.astype(BF16)
        dbqkv_ref[:, Q_COLS:Q_COLS + KV_COLS] += jnp.sum(dk_raw, axis=0, keepdims=True)
        dbqkv_ref[:, Q_COLS + KV_COLS:] += jnp.sum(dv, axis=0, keepdims=True)
        dgq_ref[...] += dgq
        dsink_ref[...] += dsink

        @pl.when(i == nt - 1)
        def _():
            for ref in (dgq_ref, dgk_ref):
                v = ref[...]
                ref[...] = v + pltpu.roll(v, HEAD_DIM, 1)

    acc = lambda shape: pl.BlockSpec(shape, lambda i: (0,) * len(shape))
    return pl.pallas_call(
        body, grid=(nt,),
        in_specs=[pl.BlockSpec((tq, Q_COLS), lambda i: (nt - 1 - i, 0)),
                  pl.BlockSpec((tq, 2 * KV_COLS), lambda i: (nt - 1 - i, 2)),
                  pl.BlockSpec((ATT_BLOCK, 2 * KV_COLS), lambda i: (jnp.maximum((nt - 1 - i) * nb - 1, 0), 2)),
                  pl.BlockSpec((tq, Q_COLS), lambda i: (nt - 1 - i, 0)),
                  _resident((1, LANES)), _resident((1, LANES)), pl.BlockSpec(memory_space=pltpu.SMEM)] + s_in,
        out_specs=[pl.BlockSpec((tq, QKV_COLS), lambda i: (nt - 1 - i, 0)), acc((1, LANES)), acc((1, LANES)),
                   acc((1, LANES)), acc((1, QKV_COLS))] + s_out,
        out_shape=[jax.ShapeDtypeStruct((s, QKV_COLS), BF16), jax.ShapeDtypeStruct((1, LANES), F32),
                   jax.ShapeDtypeStruct((1, LANES), F32), jax.ShapeDtypeStruct((1, LANES), F32),
                   jax.ShapeDtypeStruct((1, QKV_COLS), F32)] + s_shape,
        scratch_shapes=[pltpu.VMEM((N_KV_HEADS, tq + ATT_BLOCK, LANES), F32), pltpu.VMEM((N_KV_HEADS, tq + ATT_BLOCK, LANES), F32),
                        pltpu.VMEM((N_KV_HEADS, ATT_BLOCK, LANES), F32), pltpu.VMEM((N_KV_HEADS, ATT_BLOCK, LANES), F32)] + s_scratch,
        compiler_params=_cparams("arbitrary"), name="attn_bwd")(qkv, qkv, qkv, dmixed, gq2, gk2, sinks, *g8s)


def _in_bwd(dqkv, dcin, w_in_t, x, dx2, g_mix):
    s = x.shape[0]
    tm = _tile(s)

    def body(dq_ref, dc_ref, w_ref, x_ref, dx2_ref, g_ref, gx_ref, dg_ref):
        @pl.when(pl.program_id(0) == 0)
        def _():
            dg_ref[...] = jnp.zeros(dg_ref.shape, F32)

        dh = _dot(dq_ref[...], w_ref[0:QKV_COLS, :]) + _dot(dc_ref[...], w_ref[QKV_COLS:, :])
        xv = x_ref[...]
        r = lax.rsqrt(jnp.mean(xv * xv, axis=-1, keepdims=True) + EPS)
        n = xv * r
        dg_ref[...] += jnp.sum(dh * n, axis=0, keepdims=True)
        dn = dh * g_ref[...]
        gx_ref[...] = dx2_ref[...] + r * (dn - n * jnp.mean(dn * n, axis=-1, keepdims=True))

    return pl.pallas_call(
        body, grid=(s // tm,),
        in_specs=[pl.BlockSpec((tm, QKV_COLS), lambda i: (i, 0)), pl.BlockSpec((tm, CIN_COLS), lambda i: (i, 0)),
                  _resident((QKV_COLS + CIN_COLS, D_MODEL)),
                  pl.BlockSpec((tm, D_MODEL), lambda i: (i, 0)), pl.BlockSpec((tm, D_MODEL), lambda i: (i, 0)),
                  _resident((1, D_MODEL))],
        out_specs=[pl.BlockSpec((tm, D_MODEL), lambda i: (i, 0)), pl.BlockSpec((1, D_MODEL), lambda i: (0, 0))],
        out_shape=[jax.ShapeDtypeStruct((s, D_MODEL), F32), jax.ShapeDtypeStruct((1, D_MODEL), F32)],
        compiler_params=_cparams("arbitrary"), name="in_bwd")(dqkv, dcin, w_in_t, x, dx2, g_mix)


def _tn_matmul(a, b, name):
    ga, s, m = a.shape
    gb, _, n = b.shape
    g = max(ga, gb)
    tk = min(TN_TOKENS, s)

    def body(a_ref, b_ref, o_ref):
        @pl.when(pl.program_id(1) == 0)
        def _():
            o_ref[...] = jnp.zeros(o_ref.shape, F32)

        o_ref[0] += _dot_tn(a_ref[0].astype(BF16), b_ref[0].astype(BF16))

    return pl.pallas_call(
        body, grid=(g, s // tk),
        in_specs=[pl.BlockSpec((1, tk, m), (lambda gi, k: (gi, k, 0)) if ga > 1 else (lambda gi, k: (0, k, 0))),
                  pl.BlockSpec((1, tk, n), (lambda gi, k: (gi, k, 0)) if gb > 1 else (lambda gi, k: (0, k, 0)))],
        out_specs=pl.BlockSpec((1, m, n), lambda gi, k: (gi, 0, 0)),
        out_shape=jax.ShapeDtypeStruct((g, m, n), F32),
        compiler_params=_cparams("parallel", "arbitrary"), name=name)(a, b)


def _tn_matmul_pair(a0, a1, b, name):
    s, m0 = a0.shape
    m1, n = a1.shape[1], b.shape[1]
    tk = min(TN_TOKENS, s)

    def body(a0_ref, a1_ref, b_ref, o_ref):
        @pl.when(pl.program_id(0) == 0)
        def _():
            o_ref[...] = jnp.zeros(o_ref.shape, F32)

        bv = b_ref[...].astype(BF16)
        o_ref[0:m0, :] += _dot_tn(a0_ref[...].astype(BF16), bv)
        o_ref[m0:, :] += _dot_tn(a1_ref[...].astype(BF16), bv)

    return pl.pallas_call(
        body, grid=(s // tk,),
        in_specs=[pl.BlockSpec((tk, m0), lambda k: (k, 0)), pl.BlockSpec((tk, m1), lambda k: (k, 0)),
                  pl.BlockSpec((tk, n), lambda k: (k, 0))],
        out_specs=pl.BlockSpec((m0 + m1, n), lambda k: (0, 0)),
        out_shape=jax.ShapeDtypeStruct((m0 + m1, n), F32),
        compiler_params=_cparams("arbitrary"), name=name)(a0, a1, b)


def _allgather(shards, dtypes):
    n = len(shards)
    n_copies = 1 + 2 * len(OTHER_CHIPS)

    def body(*refs):
        ins, outs = refs[:n], refs[n:2 * n]
        send_sems, recv_sems = refs[2 * n:]
        x, y, c = _position()
        me, sibling = (x, y, c), (x, y, 1 - c)
        chips = [(_flip(x, fx), _flip(y, fy)) for fx, fy in OTHER_CHIPS]
        for a in range(n):
            outs[a][_dev_index(*me)] = ins[a][...].astype(dtypes[a])

        def copy(a, k, block, to):
            rows = outs[a].at[_dev_index(*block)]
            return pltpu.make_async_remote_copy(src_ref=rows, dst_ref=rows, send_sem=send_sems.at[a, k],
                                                recv_sem=recv_sems.at[a, k], device_id=to, device_id_type=MESH)

        started = []
        for a in range(n):
            for j, chip in enumerate(chips):
                started.append(copy(a, 1 + j, me, (*chip, c)))
            started.append(copy(a, 0, me, sibling))
        for cp in started:
            cp.start()
        for a in range(n):
            for j, chip in enumerate(chips):
                copy(a, 1 + j, (*chip, c), me).wait_recv()
                fwd = copy(a, 1 + len(chips) + j, (*chip, c), sibling)
                fwd.start()
                started.append(fwd)
        for a in range(n):
            copy(a, 0, sibling, me).wait_recv()
            for j, chip in enumerate(chips):
                copy(a, 1 + len(chips) + j, (*chip, 1 - c), me).wait_recv()
        for cp in started:
            cp.wait_send()

    vmem = pl.BlockSpec(memory_space=pltpu.VMEM)
    return pl.pallas_call(
        body, in_specs=[vmem] * n, out_specs=[vmem] * n,
        out_shape=[jax.ShapeDtypeStruct((N_DEV,) + w.shape, dt) for w, dt in zip(shards, dtypes)],
        scratch_shapes=[pltpu.SemaphoreType.DMA((n, n_copies)), pltpu.SemaphoreType.DMA((n, n_copies))],
        compiler_params=pltpu.CompilerParams(vmem_limit_bytes=VMEM_LIMIT), name="allgather_weights")(*shards)


def _final_exchange(g8, v):
    rows = v.shape[0]
    _, _, s_shape, s_scratch = _scatter_specs([g8])

    def body(g_ref, v_ref, gout_ref, vout_ref, gath, send_sems, recv_sems, *rs_scratch):
        scatter = _ReduceScatter(g_ref, gout_ref, *rs_scratch)
        x, y, c = _position()
        me = _dev_index(x, y, c)
        peers = [(_flip(x, k >> 2 & 1), _flip(y, k >> 1 & 1), _flip(c, k & 1)) for k in range(1, N_DEV)]

        def copy(k, block):
            return pltpu.make_async_remote_copy(src_ref=gath.at[block], dst_ref=gath.at[block], send_sem=send_sems.at[k],
                                                recv_sem=recv_sems.at[k], device_id=peers[k], device_id_type=MESH)

        scatter.start()
        gath[me] = v_ref[...]
        for k in range(N_DEV - 1):
            copy(k, me).start()
        scatter.middle()
        for k in range(N_DEV - 1):
            copy(k, _dev_index(*peers[k])).wait_recv()
        for k in range(N_DEV - 1):
            copy(k, me).wait_send()
        total = gath[0]
        for d in range(1, N_DEV):
            total = total + gath[d]
        vout_ref[...] = total
        scatter.finish()

    vmem = pl.BlockSpec(memory_space=pltpu.VMEM)
    return pl.pallas_call(
        body, in_specs=[pl.BlockSpec(memory_space=pl.ANY), vmem], out_specs=[vmem, vmem],
        out_shape=s_shape + [jax.ShapeDtypeStruct((rows, LANES), F32)],
        scratch_shapes=[pltpu.VMEM((N_DEV, rows, LANES), F32), pltpu.SemaphoreType.DMA((N_DEV - 1,)),
                        pltpu.SemaphoreType.DMA((N_DEV - 1,))] + s_scratch,
        compiler_params=pltpu.CompilerParams(vmem_limit_bytes=VMEM_LIMIT), name="final_exchange")(g8, v)


def _row_tile(r):
    for n in (8, 4, 2):
        if r % (n * SUBLANES) == 0:
            return r // n
    return r


def _adam_math(wv, gv, mv, vv):
    mn = ADAM_B1 * mv + (1.0 - ADAM_B1) * gv
    vn = ADAM_B2 * vv + (1.0 - ADAM_B2) * (gv * gv)
    m_hat = mn / (1.0 - ADAM_B1 ** ADAM_STEP)
    v_hat = vn / (1.0 - ADAM_B2 ** ADAM_STEP)
    return -ADAM_LR * (m_hat / (jnp.sqrt(v_hat) + ADAM_EPS) + ADAM_WD * wv), mn, vn


def _adamw(w, g, m, v, name):
    r, c_ = w.shape
    tr = _row_tile(r)

    def body(w_ref, g_ref, m_ref, v_ref, d_ref, mo_ref, vo_ref):
        d_ref[...], mo_ref[...], vo_ref[...] = _adam_math(w_ref[...], g_ref[...], m_ref[...], v_ref[...])

    spec = pl.BlockSpec((tr, c_), lambda i: (i, 0))
    return pl.pallas_call(
        body, grid=(r // tr,), in_specs=[spec] * 4, out_specs=[spec] * 3,
        out_shape=[jax.ShapeDtypeStruct((r, c_), F32)] * 3,
        compiler_params=_cparams("parallel"), name=name)(w, g, m, v)


FW_ROWS = 24
CW_ROWS = 32
R_FW = 0
R_FB = R_FW + N_DEV * FW_ROWS
R_CW = R_FB + 48
R_BQKV = R_CW + (CONV_WIDTH // LANES) * CW_ROWS
R_BCIN = R_BQKV + 8
R_GMIX = R_BCIN + 8
R_BOUT = R_GMIX + 8
R_GFFN = R_BOUT + 8
R_CB = R_GFFN + 8
R_CGAIN = R_CB + 8
R_CBIAS = R_CGAIN + 8
R_QKS = R_CBIAS + 8
SMALL_ROWS = R_QKS + 8


def _pack_small(raw):
    def rows(a, n):
        a = a.reshape(-1, LANES)
        return jnp.pad(a, ((0, n - a.shape[0]), (0, 0)))

    fw = jnp.pad(raw["dfw"].reshape(N_DEV, -1, LANES), ((0, 0), (0, FW_ROWS - 3 * FF_LANE_CHUNKS), (0, 0)))
    cw = jnp.pad(raw["dcw"].reshape(CONV_KERNEL, -1, LANES).transpose(1, 0, 2), ((0, 0), (0, CW_ROWS - CONV_KERNEL), (0, 0)))
    qks = jnp.concatenate([raw["dgq"], raw["dgk"], raw["dsink"], jnp.pad(raw["loss"], ((0, 0), (0, LANES - 1)))], axis=0)
    return jnp.concatenate([
        fw.reshape(-1, LANES), rows(raw["dfb"][:, 0, :FF_CHUNK], 48), cw.reshape(-1, LANES), rows(raw["dbqkv"], 8),
        rows(raw["dbcin"], 8), rows(raw["dg_mix"], 8), rows(raw["db_out"], 8), rows(raw["dg_ffn"], 8), rows(raw["dcb"], 8),
        rows(raw["dcgain"], 8), rows(raw["dcbias"], 8), rows(qks, 8)], axis=0)


def _adamw_small(gpack, w, m, v):
    n = len(SMALL)
    ix = {name: i for i, name in enumerate(SMALL)}

    def body(g_ref, *refs):
        w_refs, m_refs, v_refs, outs = refs[:n], refs[n:2 * n], refs[2 * n:3 * n], refs[3 * n:]
        d = _dev_index(*_position())

        def step(name, idx, gv):
            i = ix[name]
            delta, mn, vn = _adam_math(w_refs[i][idx], gv, m_refs[i][idx], v_refs[i][idx])
            for ref, val in zip(outs[4 * i:4 * i + 4], (gv, delta, mn, vn)):
                ref[idx] = val

        def whole(name, row, nrows):
            step(name, (slice(None), slice(None)), g_ref[row:row + nrows, :])

        whole("mix_norm_gain", R_GMIX, 8)
        whole("b_out", R_BOUT, 8)
        whole("ffn_norm_gain", R_GFFN, 8)
        whole("conv_dw_b", R_CB, 4)
        whole("conv_norm_gain", R_CGAIN, 4)
        whole("conv_norm_bias", R_CBIAS, 4)
        whole("ffn_dw_b", R_FB, 2 * D_FF // LANES)
        nq = QKV_COLS // LANES
        step("b_in", (slice(0, nq), slice(None)), g_ref[R_BQKV:R_BQKV + nq, :])
        step("b_in", (slice(nq, nq + CIN_COLS // LANES), slice(None)), g_ref[R_BCIN:R_BCIN + CIN_COLS // LANES, :])
        step("q_norm_gain", (slice(None), slice(None)), g_ref[R_QKS:R_QKS + 1, 0:HEAD_DIM])
        step("k_norm_gain", (slice(None), slice(None)), g_ref[R_QKS + 1:R_QKS + 2, 0:HEAD_DIM])
        step("attn_sinks", (slice(None), slice(None)), g_ref[R_QKS + 2:R_QKS + 3, 0:N_Q_HEADS])
        blk = g_ref[pl.ds(pl.multiple_of(R_CW + CW_ROWS * lax.shift_right_logical(d, 1), SUBLANES), CW_ROWS), :]
        blk = jnp.where((d & 1) == 1, pltpu.roll(blk, HEAD_DIM, 1), blk)
        step("conv_dw_w", (slice(None), slice(None)), blk[0:CONV_KERNEL, 0:CONV_WIDTH // N_DEV])
        blk = g_ref[pl.ds(pl.multiple_of(R_FW + FW_ROWS * d, SUBLANES), FW_ROWS), :]
        for k in range(3):
            for j in range(FF_LANE_CHUNKS):
                wd = min(LANES, FF_CHUNK - j * LANES)
                row = k * FF_LANE_CHUNKS + j
                step("ffn_dw_w", (slice(k, k + 1), slice(j * LANES, j * LANES + wd)), blk[row:row + 1, 0:wd])

    vmem = pl.BlockSpec(memory_space=pltpu.VMEM)
    args = [gpack] + [d[name] for d in (w, m, v) for name in SMALL]
    outs = pl.pallas_call(
        body, in_specs=[vmem] * len(args), out_specs=[vmem] * (4 * n),
        out_shape=[jax.ShapeDtypeStruct(w[name].shape, F32) for name in SMALL for _ in range(4)],
        compiler_params=pltpu.CompilerParams(vmem_limit_bytes=VMEM_LIMIT), name="adamw_small")(*args)
    return {name: outs[4 * i:4 * i + 4] for i, name in enumerate(SMALL)}


def _token_mixing(x, p, attn_shards, conv_shards):
    qkv, cin, h1 = _mix_proj(x, p["g_mix"], p["w_in_t"], p["b_qkv"], p["b_cin"])
    attn, *from_attn = _attn_fwd(qkv, p["gq2"], p["gk2"], p["sinks"], attn_shards)
    c3, c1, *from_conv = _conv_fwd(cin, p["cw8"], p["cb"], p["cgain"], p["cbias"], conv_shards)
    return (qkv, cin, h1, attn, c3, c1), from_attn, from_conv


def _rest_of_step(x, target, p, saved, scatter):
    s = x.shape[0]
    qkv, cin, h1, attn, c3, c1 = saved
    cw8, w_out, w_up, w_down = p["cw8"], p["w_out"], p["w_up"], p["w_down"]
    x2, h2 = _out_proj(x, attn, c3, w_out, w_out, p["b_out"], p["g_ffn"])
    fw, fb = p["fw"], p["fb"]
    up0, gu, act, dy, dyb, loss = _ffn_fwd(h2, x2, target, w_up, fw, fb, w_down)
    dup0, dh2, dfw, dfb = _ffn_bwd(dyb, up0, gu, w_up, fw, w_down)
    dx2, dmixed, dg_ffn, db_out = _ffn_norm_bwd(dh2, dy, x2, p["g_ffn"], w_out)
    dw_up = _tn_matmul(dup0.reshape(N_DEV, s, FF_CHUNK), h2[None], "dw_up")
    dw_down = _tn_matmul(act, dyb[None], "dw_down").reshape(N_DEV, -1, D_MODEL)
    dw_out = _tn_matmul_pair(attn, c3, dx2, "dw_out").reshape(N_DEV, -1, D_MODEL)
    dcin, dcw, dcb, dcgain, dcbias, dbcin, *g_up = _conv_bwd(dmixed, c1, cin, cw8, p["cgain"], p["cbias"], [dw_up] if scatter else [])
    dqkv, dgq, dgk, dsink, dbqkv, *g_down_out = _attn_bwd(qkv, dmixed, p["gq2"], p["gk2"], p["sinks"],
                                                          [dw_down, dw_out] if scatter else [])
    dw_in = _tn_matmul_pair(dqkv, dcin, h1, "dw_in").reshape(N_DEV, -1, D_MODEL)
    grad_x, dg_mix = _in_bwd(dqkv, dcin, p["w_in_t"], x, dx2, p["g_mix"])
    if scatter:
        big = {"w_up": g_up[0], "w_down": g_down_out[0], "w_in": dw_in, "w_out": g_down_out[1]}
    else:
        big = {"w_up": dw_up, "w_down": dw_down, "w_in": dw_in, "w_out": dw_out}
    small = dict(dg_mix=dg_mix, dbqkv=dbqkv, dbcin=dbcin, dgq=dgq, dgk=dgk, dsink=dsink, dcw=dcw, dcb=dcb, dcgain=dcgain,
                 dcbias=dcbias, db_out=db_out, dg_ffn=dg_ffn, dfw=dfw, dfb=dfb, loss=loss)
    return loss, grad_x, big, small


BIG = ("w_in", "w_out", "w_up", "w_down")
SMALL = ("mix_norm_gain", "b_in", "q_norm_gain", "k_norm_gain", "attn_sinks", "conv_dw_w", "conv_dw_b",
         "conv_norm_gain", "conv_norm_bias", "b_out", "ffn_norm_gain", "ffn_dw_w", "ffn_dw_b")
ORDER = ("mix_norm_gain", "w_in", "b_in", "q_norm_gain", "k_norm_gain", "attn_sinks", "conv_dw_w", "conv_dw_b",
         "conv_norm_gain", "conv_norm_bias", "w_out", "b_out", "ffn_norm_gain", "w_up", "ffn_dw_w", "ffn_dw_b", "w_down")


def kernel(x, mix_norm_gain, w_in, b_in, q_norm_gain, k_norm_gain, attn_sinks, conv_dw_w, conv_dw_b, conv_norm_gain, conv_norm_bias, w_out, b_out, ffn_norm_gain, w_up, ffn_dw_w, ffn_dw_b, w_down, loss_target, m_mix_norm_gain, m_w_in, m_b_in, m_q_norm_gain, m_k_norm_gain, m_attn_sinks, m_conv_dw_w, m_conv_dw_b, m_conv_norm_gain, m_conv_norm_bias, m_w_out, m_b_out, m_ffn_norm_gain, m_w_up, m_ffn_dw_w, m_ffn_dw_b, m_w_down, v_mix_norm_gain, v_w_in, v_b_in, v_q_norm_gain, v_k_norm_gain, v_attn_sinks, v_conv_dw_w, v_conv_dw_b, v_conv_norm_gain, v_conv_norm_bias, v_w_out, v_b_out, v_ffn_norm_gain, v_w_up, v_ffn_dw_w, v_ffn_dw_b, v_w_down):
    w = dict(mix_norm_gain=mix_norm_gain, w_in=w_in, b_in=b_in, q_norm_gain=q_norm_gain, k_norm_gain=k_norm_gain,
             attn_sinks=attn_sinks, conv_dw_w=conv_dw_w, conv_dw_b=conv_dw_b, conv_norm_gain=conv_norm_gain,
             conv_norm_bias=conv_norm_bias, w_out=w_out, b_out=b_out, ffn_norm_gain=ffn_norm_gain, w_up=w_up,
             ffn_dw_w=ffn_dw_w, ffn_dw_b=ffn_dw_b, w_down=w_down)
    m = dict(mix_norm_gain=m_mix_norm_gain, w_in=m_w_in, b_in=m_b_in, q_norm_gain=m_q_norm_gain, k_norm_gain=m_k_norm_gain,
             attn_sinks=m_attn_sinks, conv_dw_w=m_conv_dw_w, conv_dw_b=m_conv_dw_b, conv_norm_gain=m_conv_norm_gain,
             conv_norm_bias=m_conv_norm_bias, w_out=m_w_out, b_out=m_b_out, ffn_norm_gain=m_ffn_norm_gain, w_up=m_w_up,
             ffn_dw_w=m_ffn_dw_w, ffn_dw_b=m_ffn_dw_b, w_down=m_w_down)
    v = dict(mix_norm_gain=v_mix_norm_gain, w_in=v_w_in, b_in=v_b_in, q_norm_gain=v_q_norm_gain, k_norm_gain=v_k_norm_gain,
             attn_sinks=v_attn_sinks, conv_dw_w=v_conv_dw_w, conv_dw_b=v_conv_dw_b, conv_norm_gain=v_conv_norm_gain,
             conv_norm_bias=v_conv_norm_bias, w_out=v_w_out, b_out=v_b_out, ffn_norm_gain=v_ffn_norm_gain, w_up=v_w_up,
             ffn_dw_w=v_ffn_dw_w, ffn_dw_b=v_ffn_dw_b, w_down=v_w_down)
    s = x.shape[1]

    wi8, cw8, fw8 = _allgather([w_in.T, conv_dw_w, ffn_dw_w], [BF16, F32, F32])
    lane_pad = ((0, 0), (0, 0), (0, FF_PADDED - FF_CHUNK))
    p = {
        "g_mix": mix_norm_gain.reshape(1, -1), "w_in_t": wi8.reshape(QKV_COLS + CIN_COLS, D_MODEL),
        "b_qkv": b_in[:QKV_COLS].reshape(1, -1), "b_cin": b_in[QKV_COLS:].reshape(1, -1),
        "gq2": jnp.tile(q_norm_gain, 2).reshape(1, -1), "gk2": jnp.tile(k_norm_gain, 2).reshape(1, -1), "sinks": attn_sinks,
        "cw8": jnp.repeat(cw8.transpose(1, 0, 2).reshape(CONV_KERNEL, CONV_WIDTH), SUBLANES, axis=0),
        "cb": conv_dw_b.reshape(1, -1), "cgain": conv_norm_gain.reshape(1, -1), "cbias": conv_norm_bias.reshape(1, -1),
        "b_out": b_out.reshape(1, -1), "g_ffn": ffn_norm_gain.reshape(1, -1),
        "fw": jnp.pad(fw8, lane_pad), "fb": jnp.pad(ffn_dw_b.reshape(N_DEV, 1, FF_CHUNK), lane_pad),
    }

    saved, (wu8,), (wo8, wd8) = _token_mixing(x[0], p, [w_up.T], [w_out, w_down])
    p.update(w_out=wo8.reshape(D_MODEL, D_MODEL), w_up=wu8, w_down=wd8.reshape(N_FF_PAIRS, FF_CHUNK, D_MODEL))
    loss, grad_x, big, small = _rest_of_step(x[0], loss_target[0], p, saved, True)

    g = dict(big)
    g["w_in"], gpack = _final_exchange(big["w_in"], _pack_small(small))

    delta, new_m, new_v = {}, {}, {}
    for n in BIG:
        if n in ("w_in", "w_up"):
            outs = _adamw(w[n].T, g[n], m[n].T, v[n].T, "adamw_" + n)
            g[n], delta[n], new_m[n], new_v[n] = g[n].T, *[o.T for o in outs]
        else:
            delta[n], new_m[n], new_v[n] = _adamw(w[n], g[n], m[n], v[n], "adamw_" + n)

    def view(a):
        return a if a.ndim == 2 else (a.reshape(-1, LANES) if a.size % LANES == 0 else a.reshape(1, -1))

    small_out = _adamw_small(gpack, *[{n: view(d[n]) for n in SMALL} for d in (w, m, v)])
    for n in SMALL:
        g[n], delta[n], new_m[n], new_v[n] = [a.reshape(w[n].shape) for a in small_out[n]]

    total = gpack[R_QKS + 3, 0]
    return (total, grad_x.reshape(1, s, D_MODEL), *[g[n] for n in ORDER], *[delta[n] for n in ORDER],
            *[new_m[n] for n in ORDER], *[new_v[n] for n in ORDER])
```

```python
import functools
import math

import jax
import jax.numpy as jnp
from jax import lax
from jax.experimental import pallas as pl
from jax.experimental.pallas import tpu as pltpu

F32 = jnp.float32
BF16 = jnp.bfloat16

D_MODEL = 1024
HEAD_DIM = 64
N_Q_HEADS = 8
N_KV_HEADS = 2
Q_COLS = 512
KV_COLS = 128
QKV_COLS = Q_COLS + 2 * KV_COLS
CONV_WIDTH = 512
CIN_COLS = 2 * CONV_WIDTH
CONV_KERNEL = 31
CONV_HALO = 32
D_FF = 2816
N_DEV = 8
FF_CHUNK = 2 * D_FF // N_DEV
N_FF_PAIRS = N_DEV // 2
ATT_BLOCK = 128
EPS = 1e-6
NEG_INF = -1e30
SLOPES = [float(2.0 ** (-8.0 * (h + 1.0) / N_Q_HEADS)) for h in range(N_Q_HEADS)]

ADAM_LR = 0.001
ADAM_B1 = 0.9
ADAM_B2 = 0.999
ADAM_EPS = 1e-08
ADAM_WD = 0.01
ADAM_STEP = 10

LANES = 128
SUBLANES = 8
VMEM_LIMIT = 56 * 1024 * 1024
MESH = pl.DeviceIdType.MESH


def _cparams(*sem, **kw):
    return pltpu.CompilerParams(dimension_semantics=sem or None, vmem_limit_bytes=VMEM_LIMIT, **kw)


def _resident(shape):
    nd = len(shape)
    return pl.BlockSpec(shape, lambda *_: (0,) * nd, pipeline_mode=pl.Buffered(1))


def _dot(a, b):
    return jnp.dot(a, b, preferred_element_type=F32)


def _dot_nt(a, b):
    return lax.dot_general(a, b, (((1,), (1,)), ((), ())), preferred_element_type=F32)


def _dot_tn(a, b):
    return lax.dot_general(a, b, (((0,), (0,)), ((), ())), preferred_element_type=F32)


def _sigmoid(x):
    return 1.0 / (1.0 + jnp.exp(-x))


def _lo_mask(shape):
    return lax.broadcasted_iota(jnp.int32, shape, len(shape) - 1) % LANES < HEAD_DIM


def _half_sums(t, lo):
    s_lo = jnp.sum(jnp.where(lo, t, 0.0), axis=-1, keepdims=True)
    s_hi = jnp.sum(jnp.where(lo, 0.0, t), axis=-1, keepdims=True)
    return jnp.where(lo, s_lo, s_hi)


def _head_norm(t, lo):
    r = lax.rsqrt(_half_sums(t * t, lo) * (1.0 / HEAD_DIM) + EPS)
    return t * r, r


def _head_norm_bwd(dn, n, r, lo):
    return r * (dn - n * (_half_sums(dn * n, lo) * (1.0 / HEAD_DIM)))


def _tile(s):
    return min(512, s)


def _wide_tile(s):
    return min(1024, s)


TN_TOKENS = 2048
FF_COLS = ((0, 256), (256, 512), (512, 704))


def _position():
    return lax.axis_index("x"), lax.axis_index("y"), lax.axis_index("c")


def _dev_index(px, py, pc):
    return 4 * px + 2 * py + pc


def _flip(v, bit):
    return 1 - v if bit else v


OTHER_CHIPS = ((1, 0), (0, 1), (1, 1))
N_GATHER_COPIES = 1 + 2 * len(OTHER_CHIPS)


class _Gather:
    def __init__(self, shard_ref, out_ref, cast_buf, send_sems, recv_sems, local_sem):
        self.shard, self.out, self.buf = shard_ref, out_ref, cast_buf
        self.send_sems, self.recv_sems, self.local_sem = send_sems, recv_sems, local_sem
        x, y, c = _position()
        self.c = c
        self.me, self.sibling = (x, y, c), (x, y, 1 - c)
        self.chips = [(_flip(x, fx), _flip(y, fy)) for fx, fy in OTHER_CHIPS]

    def _copy(self, k, block, to, from_buf=False):
        rows = self.out.at[_dev_index(*block)]
        return pltpu.make_async_remote_copy(src_ref=self.buf if from_buf else rows, dst_ref=rows,
                                            send_sem=self.send_sems.at[k], recv_sem=self.recv_sems.at[k],
                                            device_id=to, device_id_type=MESH)

    def _local(self):
        return pltpu.make_async_copy(self.buf, self.out.at[_dev_index(*self.me)], self.local_sem)

    def start(self):
        self.buf[...] = self.shard[...].astype(self.buf.dtype)
        self._local().start()
        for j, chip in enumerate(self.chips):
            self._copy(1 + j, self.me, (*chip, self.c), from_buf=True).start()
        self._copy(0, self.me, self.sibling, from_buf=True).start()

    def forward(self):
        for j, chip in enumerate(self.chips):
            self._copy(1 + j, (*chip, self.c), self.me).wait_recv()
            self._copy(1 + len(self.chips) + j, (*chip, self.c), self.sibling).start()

    def finish(self):
        self._copy(0, self.sibling, self.me).wait_recv()
        for j, chip in enumerate(self.chips):
            self._copy(1 + len(self.chips) + j, (*chip, 1 - self.c), self.me).wait_recv()
        for k in range(N_GATHER_COPIES):
            self._copy(k, self.me, self.sibling).wait_send()
        self._local().wait()


def _gather_specs(shards):
    whole = [pl.BlockSpec(w.shape, lambda *_, nd=w.ndim: (0,) * nd, pipeline_mode=pl.Buffered(1)) for w in shards]
    outs = [pl.BlockSpec(memory_space=pl.ANY) for _ in shards]
    shapes = [jax.ShapeDtypeStruct((N_DEV,) + w.shape, BF16) for w in shards]
    scratch = []
    for w in shards:
        scratch += [pltpu.VMEM(w.shape, BF16), pltpu.SemaphoreType.DMA((N_GATHER_COPIES,)),
                    pltpu.SemaphoreType.DMA((N_GATHER_COPIES,)), pltpu.SemaphoreType.DMA(())]
    return whole, outs, shapes, scratch


def _run_gathers(gathers, step, n_steps):
    @pl.when(step == 0)
    def _():
        for g in gathers:
            g.start()

    @pl.when(step == 3 * n_steps // 4)
    def _():
        for g in gathers:
            g.forward()

    @pl.when(step == n_steps - 1)
    def _():
        for g in gathers:
            g.finish()


class _ReduceScatter:
    def __init__(self, g_ref, out_ref, stage, load_sems, send_a, recv_a, send_b, recv_b, sa_send, sa_recv, sb_send, sb_recv):
        self.g, self.out, self.stage, self.load_sems = g_ref, out_ref, stage, load_sems
        self.send_a, self.recv_a, self.send_b, self.recv_b = send_a, recv_a, send_b, recv_b
        self.sems = (sa_send, sa_recv, sb_send, sb_recv)
        x, y, c = _position()
        self.c, self.sibling = c, (x, y, 1 - c)
        self.chips = [(x, y)] + [(_flip(x, fx), _flip(y, fy)) for fx, fy in OTHER_CHIPS]

    def _copy_a(self, j):
        return pltpu.make_async_remote_copy(src_ref=self.send_a.at[j], dst_ref=self.recv_a.at[j], send_sem=self.sems[0].at[j],
                                            recv_sem=self.sems[1].at[j], device_id=self.sibling, device_id_type=MESH)

    def _copy_b(self, j):
        return pltpu.make_async_remote_copy(src_ref=self.send_b.at[j], dst_ref=self.recv_b.at[j], send_sem=self.sems[2].at[j],
                                            recv_sem=self.sems[3].at[j], device_id=(*self.chips[1 + j], self.c),
                                            device_id_type=MESH)

    def _load(self, j, core):
        return pltpu.make_async_copy(self.g.at[_dev_index(*self.chips[j], core)], self.stage.at[j % 2], self.load_sems.at[j % 2])

    def start(self):
        self._load(0, 1 - self.c).start()
        for j in range(len(self.chips)):
            self._load(j, 1 - self.c).wait()
            if j + 1 < len(self.chips):
                self._load(j + 1, 1 - self.c).start()
            self.send_a[j] = self.stage[j % 2].astype(BF16)
            self._copy_a(j).start()

    def middle(self):
        self._load(0, self.c).start()
        for j in range(len(self.chips)):
            self._load(j, self.c).wait()
            if j + 1 < len(self.chips):
                self._load(j + 1, self.c).start()
            self._copy_a(j).wait_recv()
            part = self.stage[j % 2] + self.recv_a[j].astype(F32)
            if j == 0:
                self.out[...] = part
            else:
                self.send_b[j - 1] = part.astype(BF16)
                self._copy_b(j - 1).start()

    def finish(self):
        for j in range(len(OTHER_CHIPS)):
            self._copy_b(j).wait_recv()
            self.out[...] += self.recv_b[j].astype(F32)
        for j in range(len(self.chips)):
            self._copy_a(j).wait_send()
        for j in range(len(OTHER_CHIPS)):
            self._copy_b(j).wait_send()


N_SCATTER_SCRATCH = 10


def _scatter_specs(g8s):
    na, nb = 1 + len(OTHER_CHIPS), len(OTHER_CHIPS)
    ins = [pl.BlockSpec(memory_space=pl.ANY) for _ in g8s]
    outs = [pl.BlockSpec(g.shape[1:], lambda *_: (0, 0)) for g in g8s]
    shapes = [jax.ShapeDtypeStruct(g.shape[1:], F32) for g in g8s]
    scratch = []
    for g in g8s:
        blk = g.shape[1:]
        scratch += [pltpu.VMEM((2,) + blk, F32), pltpu.SemaphoreType.DMA((2,)), pltpu.VMEM((na,) + blk, BF16), pltpu.VMEM((na,) + blk, BF16),
                    pltpu.VMEM((nb,) + blk, BF16), pltpu.VMEM((nb,) + blk, BF16),
                    pltpu.SemaphoreType.DMA((na,)), pltpu.SemaphoreType.DMA((na,)),
                    pltpu.SemaphoreType.DMA((nb,)), pltpu.SemaphoreType.DMA((nb,))]
    return ins, outs, shapes, scratch


def _run_scatters(scatters, step, n_steps):
    @pl.when(step == 0)
    def _():
        for r in scatters:
            r.start()

    @pl.when(step == min(max(1, n_steps // 4), n_steps - 1))
    def _():
        for r in scatters:
            r.middle()

    @pl.when(step == n_steps - 1)
    def _():
        for r in scatters:
            r.finish()


def _mix_proj(x, g_mix, w_in_t, b_qkv, b_cin):
    s = x.shape[0]
    tm = _wide_tile(s)

    def body(x_ref, g_ref, w_ref, bq_ref, bc_ref, qkv_ref, cin_ref, h1_ref):
        xv = x_ref[...]
        r = lax.rsqrt(jnp.mean(xv * xv, axis=-1, keepdims=True) + EPS)
        h = (xv * r * g_ref[...]).astype(BF16)
        h1_ref[...] = h
        qkv_ref[...] = _dot_nt(h, w_ref[0:QKV_COLS, :]) + bq_ref[...]
        cin_ref[...] = _dot_nt(h, w_ref[QKV_COLS:, :]) + bc_ref[...]

    return pl.pallas_call(
        body, grid=(s // tm,),
        in_specs=[pl.BlockSpec((tm, D_MODEL), lambda i: (i, 0)), _resident((1, D_MODEL)),
                  _resident((QKV_COLS + CIN_COLS, D_MODEL)), _resident((1, QKV_COLS)), _resident((1, CIN_COLS))],
        out_specs=[pl.BlockSpec((tm, QKV_COLS), lambda i: (i, 0)), pl.BlockSpec((tm, CIN_COLS), lambda i: (i, 0)),
                   pl.BlockSpec((tm, D_MODEL), lambda i: (i, 0))],
        out_shape=[jax.ShapeDtypeStruct((s, QKV_COLS), F32), jax.ShapeDtypeStruct((s, CIN_COLS), F32),
                   jax.ShapeDtypeStruct((s, D_MODEL), BF16)],
        compiler_params=_cparams("parallel"), name="mix_proj")(x, g_mix, w_in_t, b_qkv, b_cin)


def _kv_variants(kv_all, gk2, lo):
    k_all = kv_all[:, :LANES]
    v_all = kv_all[:, LANES:]
    kn_pre, rk = _head_norm(k_all, lo)
    kn = kn_pre * gk2
    kr = pltpu.roll(kn, HEAD_DIM, 1)
    vr = pltpu.roll(v_all, HEAD_DIM, 1)
    zero = jnp.zeros_like(kn)
    k_lo = [jnp.where(lo, kn, zero).astype(BF16), jnp.where(lo, kr, zero).astype(BF16)]
    k_hi = [jnp.where(lo, zero, kr).astype(BF16), jnp.where(lo, zero, kn).astype(BF16)]
    v_lo = [jnp.where(lo, v_all, zero).astype(BF16), jnp.where(lo, vr, zero).astype(BF16)]
    v_hi = [jnp.where(lo, zero, vr).astype(BF16), jnp.where(lo, zero, v_all).astype(BF16)]
    return k_lo, k_hi, v_lo, v_hi, kn_pre, rk


def _att_consts(first_tile, b):
    rows = 2 * ATT_BLOCK
    qi = lax.broadcasted_iota(jnp.int32, (rows, 2 * ATT_BLOCK), 0) % ATT_BLOCK
    kj = lax.broadcasted_iota(jnp.int32, (rows, 2 * ATT_BLOCK), 1)
    rel = qi + ATT_BLOCK - kj
    valid = (rel >= 0) & (rel < ATT_BLOCK)
    if b == 0:
        valid = valid & ((kj >= ATT_BLOCK) | jnp.logical_not(first_tile))
    return rel.astype(F32), valid


def _row_const(va, vb):
    top = lax.broadcasted_iota(jnp.int32, (2 * ATT_BLOCK, 1), 0) < ATT_BLOCK
    return jnp.where(top, va, vb)


def _probs(q2, k_op, rel, valid, slope, sink):
    sc = _dot_nt(q2, k_op) * (1.0 / math.sqrt(HEAD_DIM)) - slope * rel
    sc = jnp.where(valid, sc, NEG_INF)
    m = jnp.maximum(jnp.max(sc, axis=-1, keepdims=True), sink)
    p = jnp.exp(sc - m)
    e_sink = jnp.exp(sink - m)
    inv = 1.0 / (jnp.sum(p, axis=-1, keepdims=True) + e_sink)
    return p * inv, e_sink * inv


def _attn_fwd(qkv, gq2, gk2, sinks, shards):
    s = qkv.shape[0]
    tq = _tile(s)
    nb = tq // ATT_BLOCK
    ng = len(shards)
    g_in, g_out, g_shape, g_scratch = _gather_specs(shards)

    def body(q_ref, kv_ref, kvp_ref, gq_ref, gk_ref, sink_ref, *rest):
        out_ref = rest[ng]
        i = pl.program_id(0)
        _run_gathers([_Gather(rest[a], rest[ng + 1 + a], *rest[2 * ng + 1 + 4 * a:2 * ng + 5 + 4 * a]) for a in range(ng)],
                     i, s // tq)
        lo = _lo_mask((1, LANES))
        kv_all = jnp.concatenate([kvp_ref[...], kv_ref[...]], axis=0)
        k_lo, k_hi, v_lo, v_hi, _, _ = _kv_variants(kv_all, gk_ref[...], lo)
        for b in range(nb):
            rel, valid = _att_consts(i == 0, b)
            rows = slice(b * ATT_BLOCK, (b + 1) * ATT_BLOCK)
            keys = slice(b * ATT_BLOCK, (b + 2) * ATT_BLOCK)
            for kvh in range(N_KV_HEADS):
                pairs = (2 * kvh, 2 * kvh + 1)
                q2 = jnp.concatenate([q_ref[rows, p * LANES:(p + 1) * LANES] for p in pairs], axis=0)
                qn, _ = _head_norm(q2, lo)
                q2 = (qn * gq_ref[...]).astype(BF16)
                out = None
                for odd, (k_op, v_op) in enumerate(((k_lo[kvh][keys], v_lo[kvh][keys]), (k_hi[kvh][keys], v_hi[kvh][keys]))):
                    ha, hb = 2 * pairs[0] + odd, 2 * pairs[1] + odd
                    p, _ = _probs(q2, k_op, rel, valid, _row_const(SLOPES[ha], SLOPES[hb]),
                                  _row_const(sink_ref[ha], sink_ref[hb]))
                    o = _dot(p.astype(BF16), v_op)
                    out = o if out is None else out + o
                for n, p in enumerate(pairs):
                    out_ref[rows, p * LANES:(p + 1) * LANES] = out[n * ATT_BLOCK:(n + 1) * ATT_BLOCK].astype(BF16)

    return pl.pallas_call(
        body, grid=(s // tq,),
        in_specs=[pl.BlockSpec((tq, Q_COLS), lambda i: (i, 0)),
                  pl.BlockSpec((tq, 2 * KV_COLS), lambda i: (i, 2)),
                  pl.BlockSpec((ATT_BLOCK, 2 * KV_COLS), lambda i: (jnp.maximum(i * nb - 1, 0), 2)),
                  _resident((1, LANES)), _resident((1, LANES)),
                  pl.BlockSpec(memory_space=pltpu.SMEM)] + g_in,
        out_specs=[pl.BlockSpec((tq, Q_COLS), lambda i: (i, 0))] + g_out,
        out_shape=[jax.ShapeDtypeStruct((s, Q_COLS), BF16)] + g_shape,
        scratch_shapes=g_scratch,
        compiler_params=_cparams("arbitrary"), name="attn_fwd")(qkv, qkv, qkv, gq2, gk2, sinks, *shards)


def _group_stats(c1, lo):
    mu = _half_sums(c1, lo) * (1.0 / HEAD_DIM)
    d = c1 - mu
    rstd = lax.rsqrt(_half_sums(d * d, lo) * (1.0 / HEAD_DIM) + EPS)
    return d * rstd, rstd


def _rows(ref, first_row, n):
    return ref[pl.ds(first_row, n, stride=1), :].reshape(n // SUBLANES, SUBLANES, LANES)


def _conv_fwd(cin, cw8, cb, gain, bias, shards):
    s = cin.shape[0]
    tm = _tile(s)
    rc = 64
    nchunk = CONV_WIDTH // LANES
    lead = CONV_HALO - (CONV_KERNEL - 1)
    ng = len(shards)
    g_in, g_out, g_shape, g_scratch = _gather_specs(shards)

    def body(cin_ref, cw_ref, cb_ref, gain_ref, bias_ref, *rest):
        c3_ref, c1_ref, ext_ref = rest[ng], rest[ng + 1], rest[2 * ng + 2]
        _run_gathers([_Gather(rest[a], rest[ng + 2 + a], *rest[2 * ng + 3 + 4 * a:2 * ng + 7 + 4 * a]) for a in range(ng)],
                     pl.program_id(0), s // tm)

        @pl.when(pl.program_id(0) == 0)
        def _():
            ext_ref[:, 0:CONV_HALO, :] = jnp.zeros((nchunk, CONV_HALO, LANES), F32)

        lo = _lo_mask((1, LANES))
        for cc in range(nchunk):
            cols = slice(cc * LANES, (cc + 1) * LANES)
            gcols = slice(CONV_WIDTH + cc * LANES, CONV_WIDTH + (cc + 1) * LANES)
            ext_ref[cc, CONV_HALO:CONV_HALO + tm, :] = cin_ref[:, cols] * _sigmoid(cin_ref[:, gcols])
            ext = ext_ref.at[cc]
            for r in range(tm // rc):
                rows = slice(r * rc, (r + 1) * rc)
                acc = jnp.zeros((rc // SUBLANES, SUBLANES, LANES), F32)
                for k in range(CONV_KERNEL):
                    acc = acc + cw_ref[k * SUBLANES:(k + 1) * SUBLANES, cols][None] * _rows(ext, r * rc + lead + k, rc)
                c1 = acc.reshape(rc, LANES) + cb_ref[:, cols]
                c1_ref[cc, rows, :] = c1
                nrm, _ = _group_stats(c1, lo)
                c2 = nrm * gain_ref[:, cols] + bias_ref[:, cols]
                c3_ref[rows, cols] = (c2 * _sigmoid(c2)).astype(BF16)
        ext_ref[:, 0:CONV_HALO, :] = ext_ref[:, tm:tm + CONV_HALO, :]

    return pl.pallas_call(
        body, grid=(s // tm,),
        in_specs=[pl.BlockSpec((tm, CIN_COLS), lambda i: (i, 0)), _resident((CONV_KERNEL * SUBLANES, CONV_WIDTH)),
                  _resident((1, CONV_WIDTH)), _resident((1, CONV_WIDTH)), _resident((1, CONV_WIDTH))] + g_in,
        out_specs=[pl.BlockSpec((tm, CONV_WIDTH), lambda i: (i, 0)), pl.BlockSpec((nchunk, tm, LANES), lambda i: (0, i, 0))] + g_out,
        out_shape=[jax.ShapeDtypeStruct((s, CONV_WIDTH), BF16), jax.ShapeDtypeStruct((nchunk, s, LANES), F32)] + g_shape,
        scratch_shapes=[pltpu.VMEM((nchunk, tm + CONV_HALO, LANES), F32)] + g_scratch,
        compiler_params=_cparams("arbitrary"), name="conv_fwd")(cin, cw8, cb, gain, bias, *shards)


def _out_proj(x, attn, c3, wo_a, wo_c, b_out, g_ffn):
    s = x.shape[0]
    tm = _wide_tile(s)

    def body(x_ref, a_ref, c_ref, wa_ref, wc_ref, b_ref, g_ref, x2_ref, h2_ref):
        x2 = x_ref[...] + _dot(a_ref[...], wa_ref[...]) + _dot(c_ref[...], wc_ref[...]) + b_ref[...]
        x2_ref[...] = x2
        r = lax.rsqrt(jnp.mean(x2 * x2, axis=-1, keepdims=True) + EPS)
        h2_ref[...] = (x2 * r * g_ref[...]).astype(BF16)

    return pl.pallas_call(
        body, grid=(s // tm,),
        in_specs=[pl.BlockSpec((tm, D_MODEL), lambda i: (i, 0)), pl.BlockSpec((tm, Q_COLS), lambda i: (i, 0)),
                  pl.BlockSpec((tm, CONV_WIDTH), lambda i: (i, 0)),
                  pl.BlockSpec((Q_COLS, D_MODEL), lambda i: (0, 0), pipeline_mode=pl.Buffered(1)),
                  pl.BlockSpec((CONV_WIDTH, D_MODEL), lambda i: (1, 0), pipeline_mode=pl.Buffered(1)),
                  _resident((1, D_MODEL)), _resident((1, D_MODEL))],
        out_specs=[pl.BlockSpec((tm, D_MODEL), lambda i: (i, 0)), pl.BlockSpec((tm, D_MODEL), lambda i: (i, 0))],
        out_shape=[jax.ShapeDtypeStruct((s, D_MODEL), F32), jax.ShapeDtypeStruct((s, D_MODEL), BF16)],
        compiler_params=_cparams("parallel"), name="out_proj")(x, attn, c3, wo_a, wo_c, b_out, g_ffn)


FF_LANE_CHUNKS = -(-FF_CHUNK // LANES)
FF_PADDED = FF_LANE_CHUNKS * LANES


def _tap(ref, first_row, n):
    return ref[pl.ds(first_row, n, stride=1), :]


def _ffn_fwd(h2, x2, target, w_up, fw, fb, w_down):
    s = h2.shape[0]
    tm = _tile(s)
    hal = SUBLANES
    rc = min(128, tm)

    def body(h_ref, x2_ref, t_ref, wu_ref, fw_ref, fb_ref, wd_ref, up0_ref, gu_ref, act_ref, dy_ref, dyb_ref, loss_ref,
             ext_ref, carry_ref, act_buf, y_ref):
        i, ci = pl.program_id(0), pl.program_id(1)

        @pl.when((i == 0) & (ci == 0))
        def _():
            carry_ref[...] = jnp.zeros(carry_ref.shape, F32)
            ext_ref[...] = jnp.zeros(ext_ref.shape, F32)
            act_buf[...] = jnp.zeros(act_buf.shape, BF16)
            loss_ref[...] = jnp.zeros((1, 1), F32)

        @pl.when(ci == 0)
        def _():
            y_ref[...] = x2_ref[...]

        ws = (fw_ref[ci], fw_ref[ci + N_FF_PAIRS])
        bs = (fb_ref[ci], fb_ref[ci + N_FF_PAIRS])
        half_rows = (slice(0, tm // 2), slice(tm // 2, tm))
        n_grp = len(FF_COLS)

        def up_slices(grp):
            lo_c, hi_c = FF_COLS[grp]
            chunks = range(lo_c // LANES, -(-hi_c // LANES))

            def make(half, n, rows):
                def run():
                    c = ci + half * N_FF_PAIRS
                    u0 = _dot_nt(h_ref[rows, :], wu_ref[c, lo_c:hi_c, :])
                    up0_ref[half, 0, rows, lo_c:hi_c] = u0.astype(BF16)
                    if hi_c == FF_CHUNK:
                        up0_ref[half, 0, rows, FF_CHUNK:] = jnp.zeros((u0.shape[0], FF_PADDED - FF_CHUNK), BF16)
                    for j in chunks:
                        w = min(LANES, hi_c - j * LANES)
                        if n == 0:
                            ext_ref[half, j, 0:hal, 0:w] = carry_ref[c, :, j * LANES:j * LANES + w]
                        ext_ref[half, j, hal + rows.start:hal + rows.stop, 0:w] = u0[:, j * LANES - lo_c:j * LANES - lo_c + w]
                    if n == len(half_rows) - 1:
                        carry_ref[c, :, lo_c:hi_c] = u0[u0.shape[0] - hal:, :]
                return run
            return [make(half, n, rows) for half in range(2) for n, rows in enumerate(half_rows)]

        def down_slices(grp):
            lo_c, hi_c = FF_COLS[grp]

            def make(rows):
                def run():
                    y_ref[rows, :] += _dot(act_buf[rows, lo_c:hi_c], wd_ref[ci, lo_c:hi_c, :])
                return run
            return [make(rows) for rows in half_rows]

        def vector_blocks(grp):
            lo_c, hi_c = FF_COLS[grp]
            blocks = []
            for j in range(lo_c // LANES, -(-hi_c // LANES)):
                lanes = slice(j * LANES, (j + 1) * LANES)

                def gate(r, lanes=lanes, j=j):
                    base = r * rc
                    ups = []
                    for half in range(2):
                        e, w = ext_ref.at[half, j], ws[half]
                        ups.append(w[0:1, lanes] * _tap(e, base + hal - 2, rc) + w[1:2, lanes] * _tap(e, base + hal - 1, rc)
                                   + w[2:3, lanes] * _tap(e, base + hal, rc) + bs[half][:, lanes])
                    g, u = ups
                    gu_ref[0, 0, base:base + rc, lanes] = g.astype(BF16)
                    gu_ref[1, 0, base:base + rc, lanes] = u.astype(BF16)
                    act_buf[base:base + rc, lanes] = (g * _sigmoid(g) * u).astype(BF16)

                blocks += [functools.partial(gate, r) for r in range(tm // rc)]

            def finish():
                act_ref[0, :, lo_c:hi_c] = act_buf[:, lo_c:hi_c]
            blocks.append(finish)
            return blocks

        for run in up_slices(0):
            run()
        for grp in range(n_grp):
            matmuls = (up_slices(grp + 1) if grp + 1 < n_grp else []) + (down_slices(grp - 1) if grp > 0 else [])
            blocks = vector_blocks(grp)
            every = max(1, len(blocks) // (len(matmuls) + 1))
            for n, run in enumerate(blocks):
                run()
                if n % every == every - 1 and matmuls:
                    matmuls.pop(0)()
            for run in matmuls:
                run()
        for run in down_slices(n_grp - 1):
            run()

        @pl.when(ci == N_FF_PAIRS - 1)
        def _():
            e = y_ref[...] - t_ref[...]
            dy_ref[...] = e * (1.0 / D_MODEL)
            dyb_ref[...] = (e * (1.0 / D_MODEL)).astype(BF16)
            loss_ref[...] += (0.5 / D_MODEL) * jnp.sum(e * e).reshape(1, 1)

    tok = lambda i, ci: (i, 0)
    return pl.pallas_call(
        body, grid=(s // tm, N_FF_PAIRS),
        in_specs=[pl.BlockSpec((tm, D_MODEL), tok), pl.BlockSpec((tm, D_MODEL), tok), pl.BlockSpec((tm, D_MODEL), tok),
                  _resident((N_DEV, FF_CHUNK, D_MODEL)), _resident((N_DEV, 3, FF_PADDED)), _resident((N_DEV, 1, FF_PADDED)),
                  _resident((N_FF_PAIRS, FF_CHUNK, D_MODEL))],
        out_specs=[pl.BlockSpec((2, 1, tm, FF_PADDED), lambda i, ci: (0, ci, i, 0)),
                   pl.BlockSpec((2, 1, tm, FF_PADDED), lambda i, ci: (0, ci, i, 0)),
                   pl.BlockSpec((1, tm, FF_CHUNK), lambda i, ci: (ci, i, 0)),
                   pl.BlockSpec((tm, D_MODEL), tok), pl.BlockSpec((tm, D_MODEL), tok), pl.BlockSpec((1, 1), lambda i, ci: (0, 0))],
        out_shape=[jax.ShapeDtypeStruct((2, N_FF_PAIRS, s, FF_PADDED), BF16), jax.ShapeDtypeStruct((2, N_FF_PAIRS, s, FF_PADDED), BF16),
                   jax.ShapeDtypeStruct((N_FF_PAIRS, s, FF_CHUNK), BF16), jax.ShapeDtypeStruct((s, D_MODEL), F32),
                   jax.ShapeDtypeStruct((s, D_MODEL), BF16), jax.ShapeDtypeStruct((1, 1), F32)],
        scratch_shapes=[pltpu.VMEM((2, FF_LANE_CHUNKS, tm + hal, LANES), F32), pltpu.VMEM((N_DEV, hal, FF_CHUNK), F32),
                        pltpu.VMEM((tm, FF_PADDED), BF16), pltpu.VMEM((tm, D_MODEL), F32)],
        compiler_params=_cparams("arbitrary", "arbitrary"), name="ffn_fwd")(h2, x2, target, w_up, fw, fb, w_down)


def _ffn_bwd(dyb, up0, gu, w_up, fw, w_down):
    s = dyb.shape[0]
    tm = _tile(s)
    nt = s // tm
    nxt = SUBLANES
    rc = min(128, tm)

    def body(dy_ref, up0_ref, gu_ref, wu_ref, fw_ref, wd_ref,
             dup0_ref, dh2_ref, dfw_ref, dfb_ref, dext_ref, carry_ref, dact_buf, dup0_buf, dh2_acc):
        i, ci = pl.program_id(0), pl.program_id(1)

        @pl.when((i == 0) & (ci == 0))
        def _():
            for ref in (carry_ref, dfw_ref, dfb_ref, dext_ref, dact_buf):
                ref[...] = jnp.zeros(ref.shape, F32)
            dup0_buf[...] = jnp.zeros(dup0_buf.shape, BF16)

        @pl.when(ci == 0)
        def _():
            dh2_acc[...] = jnp.zeros(dh2_acc.shape, F32)

        ws = (fw_ref[ci], fw_ref[ci + N_FF_PAIRS])
        fold = lambda v: jnp.sum(v.reshape(rc // SUBLANES, SUBLANES, LANES), axis=0)
        half_rows = (slice(0, tm // 2), slice(tm // 2, tm))
        n_grp = len(FF_COLS)

        def dact_slices(grp):
            lo_c, hi_c = FF_COLS[grp]

            def make(rows):
                def run():
                    dact_buf[rows, lo_c:hi_c] = _dot_nt(dy_ref[rows, :], wd_ref[ci, lo_c:hi_c, :])
                return run
            return [make(rows) for rows in half_rows]

        def dh2_slices(grp):
            lo_c, hi_c = FF_COLS[grp]

            def make(half, rows):
                def run():
                    c = ci + half * N_FF_PAIRS
                    dh2_acc[rows, :] += _dot(dup0_buf[half, rows, lo_c:hi_c], wu_ref[c, lo_c:hi_c, :])
                return run
            return [make(half, rows) for half in range(2) for rows in half_rows]

        def vector_blocks(grp):
            lo_c, hi_c = FF_COLS[grp]
            chunks = range(lo_c // LANES, -(-hi_c // LANES))
            blocks = []

            def stage():
                for half in range(2):
                    c = ci + half * N_FF_PAIRS
                    for j in chunks:
                        dext_ref[half, j, tm:tm + nxt, :] = carry_ref[c, :, j * LANES:(j + 1) * LANES]
            blocks.append(stage)
            for j in chunks:
                lanes = slice(j * LANES, (j + 1) * LANES)
                acc = [jnp.zeros((SUBLANES, LANES), F32)] * 8

                def grads(r, lanes=lanes, j=j, acc=acc):
                    base = r * rc
                    g = gu_ref[0, 0, base:base + rc, lanes].astype(F32)
                    u = gu_ref[1, 0, base:base + rc, lanes].astype(F32)
                    sg = _sigmoid(g)
                    silu = g * sg
                    dact = dact_buf[base:base + rc, lanes]
                    ds = (dact * u * (sg + silu - silu * sg), dact * silu)
                    for half in range(2):
                        dext_ref[half, j, base:base + rc, :] = ds[half]
                        acc[4 * half] = acc[4 * half] + fold(ds[half])

                def conv_back(r, lanes=lanes, j=j, acc=acc):
                    base = r * rc
                    for half in range(2):
                        d, w = dext_ref.at[half, j], ws[half]
                        taps = [_tap(d, base + k, rc) for k in range(3)]
                        dup0 = w[2:3, lanes] * taps[0] + w[1:2, lanes] * taps[1] + w[0:1, lanes] * taps[2]
                        dup0_buf[half, base:base + rc, lanes] = dup0.astype(BF16)
                        u0 = up0_ref[half, 0, base:base + rc, lanes].astype(F32)
                        for k in range(3):
                            acc[4 * half + 1 + k] = acc[4 * half + 1 + k] + fold(taps[2 - k] * u0)

                def sums(lanes=lanes, j=j, acc=acc):
                    for half in range(2):
                        c = ci + half * N_FF_PAIRS
                        carry_ref[c, :, lanes] = dext_ref[half, j, 0:nxt, :]
                        dfb_ref[c, :, lanes] += jnp.sum(acc[4 * half], axis=0, keepdims=True)
                        dfw_ref[c, :, lanes] += jnp.concatenate(
                            [jnp.sum(acc[4 * half + 1 + k], axis=0, keepdims=True) for k in range(3)], axis=0)

                blocks += [functools.partial(grads, r) for r in range(tm // rc)]
                blocks += [functools.partial(conv_back, r) for r in range(tm // rc)] + [sums]

            def finish():
                for half in range(2):
                    dup0_ref[half, 0, :, lo_c:hi_c] = dup0_buf[half, :, lo_c:hi_c]
            blocks.append(finish)
            return blocks

        for run in dact_slices(0):
            run()
        for grp in range(n_grp):
            matmuls = (dact_slices(grp + 1) if grp + 1 < n_grp else []) + (dh2_slices(grp - 1) if grp > 0 else [])
            blocks = vector_blocks(grp)
            every = max(1, len(blocks) // (len(matmuls) + 1))
            for n, run in enumerate(blocks):
                run()
                if n % every == every - 1 and matmuls:
                    matmuls.pop(0)()
            for run in matmuls:
                run()
        for run in dh2_slices(n_grp - 1):
            run()

        @pl.when(ci == N_FF_PAIRS - 1)
        def _():
            dh2_ref[...] = dh2_acc[...].astype(BF16)

    tok = lambda i, ci: (nt - 1 - i, 0)
    acc = lambda shape: pl.BlockSpec(shape, lambda i, ci: (0,) * len(shape))
    saved = pl.BlockSpec((2, 1, tm, FF_PADDED), lambda i, ci: (0, ci, nt - 1 - i, 0))
    return pl.pallas_call(
        body, grid=(nt, N_FF_PAIRS),
        in_specs=[pl.BlockSpec((tm, D_MODEL), tok), saved, saved,
                  _resident((N_DEV, FF_CHUNK, D_MODEL)), _resident((N_DEV, 3, FF_PADDED)),
                  _resident((N_FF_PAIRS, FF_CHUNK, D_MODEL))],
        out_specs=[pl.BlockSpec((2, 1, tm, FF_CHUNK), lambda i, ci: (0, ci, nt - 1 - i, 0)),
                   pl.BlockSpec((tm, D_MODEL), tok), acc((N_DEV, 3, FF_PADDED)), acc((N_DEV, 1, FF_PADDED))],
        out_shape=[jax.ShapeDtypeStruct((2, N_FF_PAIRS, s, FF_CHUNK), BF16), jax.ShapeDtypeStruct((s, D_MODEL), BF16),
                   jax.ShapeDtypeStruct((N_DEV, 3, FF_PADDED), F32), jax.ShapeDtypeStruct((N_DEV, 1, FF_PADDED), F32)],
        scratch_shapes=[pltpu.VMEM((2, FF_LANE_CHUNKS, tm + nxt, LANES), F32), pltpu.VMEM((N_DEV, nxt, FF_PADDED), F32),
                        pltpu.VMEM((tm, FF_PADDED), F32), pltpu.VMEM((2, tm, FF_PADDED), BF16), pltpu.VMEM((tm, D_MODEL), F32)],
        compiler_params=_cparams("arbitrary", "arbitrary"), name="ffn_bwd")(dyb, up0, gu, w_up, fw, w_down)


def _ffn_norm_bwd(dh2, dy, x2, g_ffn, w_out):
    s = dy.shape[0]
    tm = _wide_tile(s)

    def body(dh_ref, dy_ref, x2_ref, g_ref, wo_ref, dx2_ref, dmix_ref, dg_ref, dbo_ref):
        @pl.when(pl.program_id(0) == 0)
        def _():
            dg_ref[...] = jnp.zeros(dg_ref.shape, F32)
            dbo_ref[...] = jnp.zeros(dbo_ref.shape, F32)

        x2v = x2_ref[...]
        r = lax.rsqrt(jnp.mean(x2v * x2v, axis=-1, keepdims=True) + EPS)
        n2 = x2v * r
        dh2 = dh_ref[...].astype(F32)
        dg_ref[...] += jnp.sum(dh2 * n2, axis=0, keepdims=True)
        dn = dh2 * g_ref[...]
        dx2 = dy_ref[...] + r * (dn - n2 * jnp.mean(dn * n2, axis=-1, keepdims=True))
        dx2_ref[...] = dx2
        dbo_ref[...] += jnp.sum(dx2, axis=0, keepdims=True)
        dmix_ref[...] = _dot_nt(dx2.astype(BF16), wo_ref[...]).astype(BF16)

    tok = pl.BlockSpec((tm, D_MODEL), lambda i: (i, 0))
    vec = pl.BlockSpec((1, D_MODEL), lambda i: (0, 0))
    return pl.pallas_call(
        body, grid=(s // tm,),
        in_specs=[tok, tok, tok, _resident((1, D_MODEL)), _resident((D_MODEL, D_MODEL))],
        out_specs=[tok, tok, vec, vec],
        out_shape=[jax.ShapeDtypeStruct((s, D_MODEL), F32), jax.ShapeDtypeStruct((s, D_MODEL), BF16),
                   jax.ShapeDtypeStruct((1, D_MODEL), F32), jax.ShapeDtypeStruct((1, D_MODEL), F32)],
        compiler_params=_cparams("arbitrary"), name="ffn_norm_bwd")(dh2, dy, x2, g_ffn, w_out)


def _conv_bwd(dmixed, c1, cin, cw8, gain, bias, g8s):
    ns = len(g8s)
    s_in, s_out, s_shape, s_scratch = _scatter_specs(g8s)
    s = cin.shape[0]
    tm = _tile(s)
    nt = s // tm
    rc = 64
    rn = min(256, tm)
    hal = CONV_HALO
    nchunk = CONV_WIDTH // LANES

    def body(dc3_ref, dc3n_ref, c1_ref, c1n_ref, cin_ref, cw_ref, gain_ref, bias_ref, *rest):
        dcin_ref, dcw_ref, dcb_ref, dgain_ref, dbias_ref, dbcin_ref = rest[ns:ns + 6]
        dc1_ext, dcw8 = rest[2 * ns + 6:2 * ns + 8]
        i = pl.program_id(0)
        first, last = i == 0, i == nt - 1
        own = rest[2 * ns + 8:]
        _run_scatters([_ReduceScatter(rest[a], rest[ns + 6 + a], *own[N_SCATTER_SCRATCH * a:N_SCATTER_SCRATCH * (a + 1)])
                       for a in range(ns)], i, nt)

        @pl.when(first)
        def _():
            for ref in (dcw8, dcb_ref, dgain_ref, dbias_ref, dbcin_ref):
                ref[...] = jnp.zeros(ref.shape, F32)

        lo = _lo_mask((1, LANES))

        def norm_bwd(dc3, c1v, cols):
            nrm, rstd = _group_stats(c1v, lo)
            c2 = nrm * gain_ref[:, cols] + bias_ref[:, cols]
            sg = _sigmoid(c2)
            dc2 = dc3 * (sg * (1.0 + c2 * (1.0 - sg)))
            dn = dc2 * gain_ref[:, cols]
            inv = 1.0 / HEAD_DIM
            dc1 = rstd * (dn - _half_sums(dn, lo) * inv - nrm * (_half_sums(dn * nrm, lo) * inv))
            return dc1, dc2, nrm

        def row_sum(v):
            return jnp.sum(v, axis=0, keepdims=True)

        for cc in range(nchunk):
            cols = slice(cc * LANES, (cc + 1) * LANES)
            gcols = slice(CONV_WIDTH + cc * LANES, CONV_WIDTH + (cc + 1) * LANES)
            d1e = dc1_ext.at[cc]
            dc1n, _, _ = norm_bwd(dc3n_ref[:, cols].astype(F32), c1n_ref[cc], cols)
            d1e[tm:tm + hal, :] = jnp.where(last, 0.0, dc1n)

            for r in range(tm // rn):
                rows = slice(r * rn, (r + 1) * rn)
                dc1, dc2, nrm = norm_bwd(dc3_ref[rows, cols].astype(F32), c1_ref[cc, rows, :], cols)
                d1e[rows, :] = dc1
                dgain_ref[:, cols] += row_sum(dc2 * nrm)
                dbias_ref[:, cols] += row_sum(dc2)
                dcb_ref[:, cols] += row_sum(dc1)
            zero = jnp.zeros((1, LANES), F32)

            def taps(r, sums):
                rows = pl.ds(pl.multiple_of(r * rc, rc), rc)
                a = cin_ref[rows, cols]
                sg = _sigmoid(cin_ref[rows, gcols])
                c0 = (a * sg).reshape(rc // SUBLANES, SUBLANES, LANES)
                dc0 = jnp.zeros((rc // SUBLANES, SUBLANES, LANES), F32)
                for k in range(CONV_KERNEL):
                    krows = slice(k * SUBLANES, (k + 1) * SUBLANES)
                    shifted = _rows(d1e, r * rc + CONV_KERNEL - 1 - k, rc)
                    dc0 = dc0 + cw_ref[krows, cols][None] * shifted
                    dcw8[krows, cols] += jnp.sum(shifted * c0, axis=0)
                dc0 = dc0.reshape(rc, LANES)
                da = dc0 * sg
                dgate = dc0 * a * sg * (1.0 - sg)
                dcin_ref[rows, cols] = da.astype(BF16)
                dcin_ref[rows, gcols] = dgate.astype(BF16)
                return sums[0] + row_sum(da), sums[1] + row_sum(dgate)

            sums = lax.fori_loop(0, tm // rc, taps, (zero, zero))
            dbcin_ref[:, cols] += sums[0]
            dbcin_ref[:, gcols] += sums[1]

        @pl.when(last)
        def _():
            for k in range(CONV_KERNEL):
                dcw_ref[k:k + 1, :] = jnp.sum(dcw8[k * SUBLANES:(k + 1) * SUBLANES, :], axis=0, keepdims=True)

    nh = tm // hal
    acc = lambda shape: pl.BlockSpec(shape, lambda i: (0,) * len(shape))
    return pl.pallas_call(
        body, grid=(nt,),
        in_specs=[pl.BlockSpec((tm, CONV_WIDTH), lambda i: (i, 1)),
                  pl.BlockSpec((hal, CONV_WIDTH), lambda i: (jnp.minimum((i + 1) * nh, s // hal - 1), 1)),
                  pl.BlockSpec((nchunk, tm, LANES), lambda i: (0, i, 0)),
                  pl.BlockSpec((nchunk, hal, LANES), lambda i: (0, jnp.minimum((i + 1) * nh, s // hal - 1), 0)),
                  pl.BlockSpec((tm, CIN_COLS), lambda i: (i, 0)),
                  _resident((CONV_KERNEL * SUBLANES, CONV_WIDTH)), _resident((1, CONV_WIDTH)), _resident((1, CONV_WIDTH))] + s_in,
        out_specs=[pl.BlockSpec((tm, CIN_COLS), lambda i: (i, 0)), acc((CONV_KERNEL, CONV_WIDTH)), acc((1, CONV_WIDTH)),
                   acc((1, CONV_WIDTH)), acc((1, CONV_WIDTH)), acc((1, CIN_COLS))] + s_out,
        out_shape=[jax.ShapeDtypeStruct((s, CIN_COLS), BF16), jax.ShapeDtypeStruct((CONV_KERNEL, CONV_WIDTH), F32),
                   jax.ShapeDtypeStruct((1, CONV_WIDTH), F32), jax.ShapeDtypeStruct((1, CONV_WIDTH), F32),
                   jax.ShapeDtypeStruct((1, CONV_WIDTH), F32), jax.ShapeDtypeStruct((1, CIN_COLS), F32)] + s_shape,
        scratch_shapes=[pltpu.VMEM((nchunk, tm + hal, LANES), F32),
                        pltpu.VMEM((CONV_KERNEL * SUBLANES, CONV_WIDTH), F32)] + s_scratch,
        compiler_params=_cparams("arbitrary"), name="conv_bwd")(dmixed, dmixed, c1, c1, cin, cw8, gain, bias, *g8s)


def _attn_bwd(qkv, dmixed, gq2, gk2, sinks, g8s):
    ns = len(g8s)
    s_in, s_out, s_shape, s_scratch = _scatter_specs(g8s)
    s = qkv.shape[0]
    tq = _tile(s)
    nb = tq // ATT_BLOCK
    nt = s // tq

    def body(q_ref, kv_ref, kvp_ref, do_ref, gq_ref, gk_ref, sink_ref, *rest):
        dqkv_ref, dgq_ref, dgk_ref, dsink_ref, dbqkv_ref = rest[ns:ns + 5]
        dk_acc, dv_acc, carry_k, carry_v = rest[2 * ns + 5:2 * ns + 9]
        i = pl.program_id(0)
        t = nt - 1 - i
        own = rest[2 * ns + 9:]
        _run_scatters([_ReduceScatter(rest[a], rest[ns + 5 + a], *own[N_SCATTER_SCRATCH * a:N_SCATTER_SCRATCH * (a + 1)])
                       for a in range(ns)], i, nt)

        @pl.when(i == 0)
        def _():
            for ref in (carry_k, carry_v, dgq_ref, dgk_ref, dsink_ref, dbqkv_ref):
                ref[...] = jnp.zeros(ref.shape, F32)

        lo = _lo_mask((1, LANES))
        lane_id = lax.broadcasted_iota(jnp.int32, (1, LANES), 1)
        kv_all = jnp.concatenate([kvp_ref[...], kv_ref[...]], axis=0)
        k_lo, k_hi, v_lo, v_hi, kn_pre, rk = _kv_variants(kv_all, gk_ref[...], lo)
        for acc_ref, carry in ((dk_acc, carry_k), (dv_acc, carry_v)):
            acc_ref[:, 0:tq, :] = jnp.zeros((N_KV_HEADS, tq, LANES), F32)
            acc_ref[:, tq:tq + ATT_BLOCK, :] = carry[...]
        dsink = jnp.zeros((1, LANES), F32)
        dgq = jnp.zeros((1, LANES), F32)
        gq = gq_ref[...]
        for b in range(nb):
            rel, valid = _att_consts(t == 0, b)
            rows = slice(b * ATT_BLOCK, (b + 1) * ATT_BLOCK)
            keys = slice(b * ATT_BLOCK, (b + 2) * ATT_BLOCK)
            for kvh in range(N_KV_HEADS):
                pairs = (2 * kvh, 2 * kvh + 1)
                q_raw = jnp.concatenate([q_ref[rows, p * LANES:(p + 1) * LANES] for p in pairs], axis=0)
                qn_pre, rq = _head_norm(q_raw, lo)
                q2 = (qn_pre * gq).astype(BF16)
                do2 = jnp.concatenate([do_ref[rows, p * LANES:(p + 1) * LANES] for p in pairs], axis=0).astype(BF16)
                dq2 = jnp.zeros((2 * ATT_BLOCK, LANES), F32)
                for odd, (k_op, v_op) in enumerate(((k_lo[kvh][keys], v_lo[kvh][keys]), (k_hi[kvh][keys], v_hi[kvh][keys]))):
                    ha, hb = 2 * pairs[0] + odd, 2 * pairs[1] + odd
                    p, p_sink = _probs(q2, k_op, rel, valid, _row_const(SLOPES[ha], SLOPES[hb]),
                                       _row_const(sink_ref[ha], sink_ref[hb]))
                    dp = _dot_nt(do2, v_op)
                    delta = jnp.sum(p * dp, axis=-1, keepdims=True)
                    ds = (p * (dp - delta) * (1.0 / math.sqrt(HEAD_DIM))).astype(BF16)
                    dsk = p_sink * delta
                    dsink = dsink - jnp.where(lane_id == ha, jnp.sum(dsk[0:ATT_BLOCK]), 0.0) \
                        - jnp.where(lane_id == hb, jnp.sum(dsk[ATT_BLOCK:]), 0.0)
                    dq2 = dq2 + _dot(ds, k_op)
                    half = lo if odd == 0 else jnp.logical_not(lo)
                    dk_acc[kvh, keys, :] += jnp.where(half, _dot_tn(ds, q2), 0.0)
                    dv_acc[kvh, keys, :] += jnp.where(half, _dot_tn(p.astype(BF16), do2), 0.0)
                dgq = dgq + jnp.sum(dq2 * qn_pre, axis=0, keepdims=True)
                dq_raw = _head_norm_bwd(dq2 * gq, qn_pre, rq, lo)
                for n, p_ in enumerate(pairs):
                    blk = dq_raw[n * ATT_BLOCK:(n + 1) * ATT_BLOCK]
                    dqkv_ref[rows, p_ * LANES:(p_ + 1) * LANES] = blk.astype(BF16)
                    dbqkv_ref[:, p_ * LANES:(p_ + 1) * LANES] += jnp.sum(blk, axis=0, keepdims=True)
        carry_k[...] = dk_acc[:, 0:ATT_BLOCK, :]
        carry_v[...] = dv_acc[:, 0:ATT_BLOCK, :]

        def fold(acc_ref):
            both = []
            for kvh in range(N_KV_HEADS):
                a = acc_ref[kvh, ATT_BLOCK:ATT_BLOCK + tq, :]
                both.append(a + pltpu.roll(a, HEAD_DIM, 1))
            return jnp.where(lo, both[0], both[1])

        dkn = fold(dk_acc)
        dv = fold(dv_acc)
        kn_c, rk_c = kn_pre[ATT_BLOCK:], rk[ATT_BLOCK:]
        dgk_ref[...] += jnp.sum(dkn * kn_c, axis=0, keepdims=True)
        dk_raw = _head_norm_bwd(dkn * gk_ref[...], kn_c, rk_c, lo)
        dqkv_ref[:, Q_COLS:Q_COLS + KV_COLS] = dk_raw.astype(BF16)
        dqkv_ref[:, Q_COLS + KV_COLS:] = dv.astype(BF16)
        dbqkv_ref[:, Q_COLS:Q_COLS + KV_COLS] += jnp.sum(dk_raw, axis=0, keepdims=True)
        dbqkv_ref[:, Q_COLS + KV_COLS:] += jnp.sum(dv, axis=0, keepdims=True)
        dgq_ref[...] += dgq
        dsink_ref[...] += dsink

        @pl.when(i == nt - 1)
        def _():
            for ref in (dgq_ref, dgk_ref):
                v = ref[...]
                ref[...] = v + pltpu.roll(v, HEAD_DIM, 1)

    acc = lambda shape: pl.BlockSpec(shape, lambda i: (0,) * len(shape))
    return pl.pallas_call(
        body, grid=(nt,),
        in_specs=[pl.BlockSpec((tq, Q_COLS), lambda i: (nt - 1 - i, 0)),
                  pl.BlockSpec((tq, 2 * KV_COLS), lambda i: (nt - 1 - i, 2)),
                  pl.BlockSpec((ATT_BLOCK, 2 * KV_COLS), lambda i: (jnp.maximum((nt - 1 - i) * nb - 1, 0), 2)),
                  pl.BlockSpec((tq, Q_COLS), lambda i: (nt - 1 - i, 0)),
                  _resident((1, LANES)), _resident((1, LANES)), pl.BlockSpec(memory_space=pltpu.SMEM)] + s_in,
        out_specs=[pl.BlockSpec((tq, QKV_COLS), lambda i: (nt - 1 - i, 0)), acc((1, LANES)), acc((1, LANES)),
                   acc((1, LANES)), acc((1, QKV_COLS))] + s_out,
        out_shape=[jax.ShapeDtypeStruct((s, QKV_COLS), BF16), jax.ShapeDtypeStruct((1, LANES), F32),
                   jax.ShapeDtypeStruct((1, LANES), F32), jax.ShapeDtypeStruct((1, LANES), F32),
                   jax.ShapeDtypeStruct((1, QKV_COLS), F32)] + s_shape,
        scratch_shapes=[pltpu.VMEM((N_KV_HEADS, tq + ATT_BLOCK, LANES), F32), pltpu.VMEM((N_KV_HEADS, tq + ATT_BLOCK, LANES), F32),
                        pltpu.VMEM((N_KV_HEADS, ATT_BLOCK, LANES), F32), pltpu.VMEM((N_KV_HEADS, ATT_BLOCK, LANES), F32)] + s_scratch,
        compiler_params=_cparams("arbitrary"), name="attn_bwd")(qkv, qkv, qkv, dmixed, gq2, gk2, sinks, *g8s)


def _in_bwd(dqkv, dcin, w_in_t, x, dx2, g_mix):
    s = x.shape[0]
    tm = _wide_tile(s)

    def body(dq_ref, dc_ref, w_ref, x_ref, dx2_ref, g_ref, gx_ref, dg_ref):
        @pl.when(pl.program_id(0) == 0)
        def _():
            dg_ref[...] = jnp.zeros(dg_ref.shape, F32)

        dh = _dot(dq_ref[...], w_ref[0:QKV_COLS, :]) + _dot(dc_ref[...], w_ref[QKV_COLS:, :])
        xv = x_ref[...]
        r = lax.rsqrt(jnp.mean(xv * xv, axis=-1, keepdims=True) + EPS)
        n = xv * r
        dg_ref[...] += jnp.sum(dh * n, axis=0, keepdims=True)
        dn = dh * g_ref[...]
        gx_ref[...] = dx2_ref[...] + r * (dn - n * jnp.mean(dn * n, axis=-1, keepdims=True))

    return pl.pallas_call(
        body, grid=(s // tm,),
        in_specs=[pl.BlockSpec((tm, QKV_COLS), lambda i: (i, 0)), pl.BlockSpec((tm, CIN_COLS), lambda i: (i, 0)),
                  _resident((QKV_COLS + CIN_COLS, D_MODEL)),
                  pl.BlockSpec((tm, D_MODEL), lambda i: (i, 0)), pl.BlockSpec((tm, D_MODEL), lambda i: (i, 0)),
                  _resident((1, D_MODEL))],
        out_specs=[pl.BlockSpec((tm, D_MODEL), lambda i: (i, 0)), pl.BlockSpec((1, D_MODEL), lambda i: (0, 0))],
        out_shape=[jax.ShapeDtypeStruct((s, D_MODEL), F32), jax.ShapeDtypeStruct((1, D_MODEL), F32)],
        compiler_params=_cparams("arbitrary"), name="in_bwd")(dqkv, dcin, w_in_t, x, dx2, g_mix)


def _tn_matmul(a, b, name, tokens):
    ga, s, m = a.shape
    gb, _, n = b.shape
    g = max(ga, gb)
    tk = min(tokens, s)

    def body(a_ref, b_ref, o_ref):
        @pl.when(pl.program_id(1) == 0)
        def _():
            o_ref[...] = jnp.zeros(o_ref.shape, F32)

        o_ref[0] += _dot_tn(a_ref[0].astype(BF16), b_ref[0].astype(BF16))

    return pl.pallas_call(
        body, grid=(g, s // tk),
        in_specs=[pl.BlockSpec((1, tk, m), (lambda gi, k: (gi, k, 0)) if ga > 1 else (lambda gi, k: (0, k, 0))),
                  pl.BlockSpec((1, tk, n), (lambda gi, k: (gi, k, 0)) if gb > 1 else (lambda gi, k: (0, k, 0)))],
        out_specs=pl.BlockSpec((1, m, n), lambda gi, k: (gi, 0, 0)),
        out_shape=jax.ShapeDtypeStruct((g, m, n), F32),
        compiler_params=_cparams("parallel", "arbitrary"), name=name)(a, b)


def _tn_matmul_pair(a0, a1, b, name):
    s, m0 = a0.shape
    m1, n = a1.shape[1], b.shape[1]
    tk = min(TN_TOKENS, s)

    def body(a0_ref, a1_ref, b_ref, o_ref):
        @pl.when(pl.program_id(0) == 0)
        def _():
            o_ref[...] = jnp.zeros(o_ref.shape, F32)

        bv = b_ref[...].astype(BF16)
        o_ref[0:m0, :] += _dot_tn(a0_ref[...].astype(BF16), bv)
        o_ref[m0:, :] += _dot_tn(a1_ref[...].astype(BF16), bv)

    return pl.pallas_call(
        body, grid=(s // tk,),
        in_specs=[pl.BlockSpec((tk, m0), lambda k: (k, 0)), pl.BlockSpec((tk, m1), lambda k: (k, 0)),
                  pl.BlockSpec((tk, n), lambda k: (k, 0))],
        out_specs=pl.BlockSpec((m0 + m1, n), lambda k: (0, 0)),
        out_shape=jax.ShapeDtypeStruct((m0 + m1, n), F32),
        compiler_params=_cparams("arbitrary"), name=name)(a0, a1, b)


def _allgather(shards, dtypes):
    n = len(shards)
    n_copies = 1 + 2 * len(OTHER_CHIPS)

    def body(*refs):
        ins, outs = refs[:n], refs[n:2 * n]
        send_sems, recv_sems = refs[2 * n:]
        x, y, c = _position()
        me, sibling = (x, y, c), (x, y, 1 - c)
        chips = [(_flip(x, fx), _flip(y, fy)) for fx, fy in OTHER_CHIPS]
        for a in range(n):
            outs[a][_dev_index(*me)] = ins[a][...].astype(dtypes[a])

        def copy(a, k, block, to):
            rows = outs[a].at[_dev_index(*block)]
            return pltpu.make_async_remote_copy(src_ref=rows, dst_ref=rows, send_sem=send_sems.at[a, k],
                                                recv_sem=recv_sems.at[a, k], device_id=to, device_id_type=MESH)

        started = []
        for a in range(n):
            for j, chip in enumerate(chips):
                started.append(copy(a, 1 + j, me, (*chip, c)))
            started.append(copy(a, 0, me, sibling))
        for cp in started:
            cp.start()
        for a in range(n):
            for j, chip in enumerate(chips):
                copy(a, 1 + j, (*chip, c), me).wait_recv()
                fwd = copy(a, 1 + len(chips) + j, (*chip, c), sibling)
                fwd.start()
                started.append(fwd)
        for a in range(n):
            copy(a, 0, sibling, me).wait_recv()
            for j, chip in enumerate(chips):
                copy(a, 1 + len(chips) + j, (*chip, 1 - c), me).wait_recv()
        for cp in started:
            cp.wait_send()

    vmem = pl.BlockSpec(memory_space=pltpu.VMEM)
    return pl.pallas_call(
        body, in_specs=[vmem] * n, out_specs=[vmem] * n,
        out_shape=[jax.ShapeDtypeStruct((N_DEV,) + w.shape, dt) for w, dt in zip(shards, dtypes)],
        scratch_shapes=[pltpu.SemaphoreType.DMA((n, n_copies)), pltpu.SemaphoreType.DMA((n, n_copies))],
        compiler_params=pltpu.CompilerParams(vmem_limit_bytes=VMEM_LIMIT), name="allgather_weights")(*shards)


def _final_exchange(g8, v):
    rows = v.shape[0]
    _, _, s_shape, s_scratch = _scatter_specs([g8])

    def body(g_ref, v_ref, gout_ref, vout_ref, gath, send_sems, recv_sems, *rs_scratch):
        scatter = _ReduceScatter(g_ref, gout_ref, *rs_scratch)
        x, y, c = _position()
        me = _dev_index(x, y, c)
        peers = [(_flip(x, k >> 2 & 1), _flip(y, k >> 1 & 1), _flip(c, k & 1)) for k in range(1, N_DEV)]

        def copy(k, block):
            return pltpu.make_async_remote_copy(src_ref=gath.at[block], dst_ref=gath.at[block], send_sem=send_sems.at[k],
                                                recv_sem=recv_sems.at[k], device_id=peers[k], device_id_type=MESH)

        scatter.start()
        gath[me] = v_ref[...]
        for k in range(N_DEV - 1):
            copy(k, me).start()
        scatter.middle()
        for k in range(N_DEV - 1):
            copy(k, _dev_index(*peers[k])).wait_recv()
        for k in range(N_DEV - 1):
            copy(k, me).wait_send()
        total = gath[0]
        for d in range(1, N_DEV):
            total = total + gath[d]
        vout_ref[...] = total
        scatter.finish()

    vmem = pl.BlockSpec(memory_space=pltpu.VMEM)
    return pl.pallas_call(
        body, in_specs=[pl.BlockSpec(memory_space=pl.ANY), vmem], out_specs=[vmem, vmem],
        out_shape=s_shape + [jax.ShapeDtypeStruct((rows, LANES), F32)],
        scratch_shapes=[pltpu.VMEM((N_DEV, rows, LANES), F32), pltpu.SemaphoreType.DMA((N_DEV - 1,)),
                        pltpu.SemaphoreType.DMA((N_DEV - 1,))] + s_scratch,
        compiler_params=pltpu.CompilerParams(vmem_limit_bytes=VMEM_LIMIT), name="final_exchange")(g8, v)


def _row_tile(r):
    for n in (8, 4, 2):
        if r % (n * SUBLANES) == 0:
            return r // n
    return r


def _adam_math(wv, gv, mv, vv):
    mn = ADAM_B1 * mv + (1.0 - ADAM_B1) * gv
    vn = ADAM_B2 * vv + (1.0 - ADAM_B2) * (gv * gv)
    m_hat = mn / (1.0 - ADAM_B1 ** ADAM_STEP)
    v_hat = vn / (1.0 - ADAM_B2 ** ADAM_STEP)
    return -ADAM_LR * (m_hat / (jnp.sqrt(v_hat) + ADAM_EPS) + ADAM_WD * wv), mn, vn


def _adamw(w, g, m, v, name):
    r, c_ = w.shape
    tr = _row_tile(r)

    def body(w_ref, g_ref, m_ref, v_ref, d_ref, mo_ref, vo_ref):
        d_ref[...], mo_ref[...], vo_ref[...] = _adam_math(w_ref[...], g_ref[...], m_ref[...], v_ref[...])

    spec = pl.BlockSpec((tr, c_), lambda i: (i, 0))
    return pl.pallas_call(
        body, grid=(r // tr,), in_specs=[spec] * 4, out_specs=[spec] * 3,
        out_shape=[jax.ShapeDtypeStruct((r, c_), F32)] * 3,
        compiler_params=_cparams("parallel"), name=name)(w, g, m, v)


FW_ROWS = 24
CW_ROWS = 32
R_FW = 0
R_FB = R_FW + N_DEV * FW_ROWS
R_CW = R_FB + 48
R_BQKV = R_CW + (CONV_WIDTH // LANES) * CW_ROWS
R_BCIN = R_BQKV + 8
R_GMIX = R_BCIN + 8
R_BOUT = R_GMIX + 8
R_GFFN = R_BOUT + 8
R_CB = R_GFFN + 8
R_CGAIN = R_CB + 8
R_CBIAS = R_CGAIN + 8
R_QKS = R_CBIAS + 8
SMALL_ROWS = R_QKS + 8


def _pack_small(raw):
    def rows(a, n):
        a = a.reshape(-1, LANES)
        return jnp.pad(a, ((0, n - a.shape[0]), (0, 0)))

    fw = jnp.pad(raw["dfw"].reshape(N_DEV, -1, LANES), ((0, 0), (0, FW_ROWS - 3 * FF_LANE_CHUNKS), (0, 0)))
    cw = jnp.pad(raw["dcw"].reshape(CONV_KERNEL, -1, LANES).transpose(1, 0, 2), ((0, 0), (0, CW_ROWS - CONV_KERNEL), (0, 0)))
    qks = jnp.concatenate([raw["dgq"], raw["dgk"], raw["dsink"], jnp.pad(raw["loss"], ((0, 0), (0, LANES - 1)))], axis=0)
    return jnp.concatenate([
        fw.reshape(-1, LANES), rows(raw["dfb"][:, 0, :FF_CHUNK], 48), cw.reshape(-1, LANES), rows(raw["dbqkv"], 8),
        rows(raw["dbcin"], 8), rows(raw["dg_mix"], 8), rows(raw["db_out"], 8), rows(raw["dg_ffn"], 8), rows(raw["dcb"], 8),
        rows(raw["dcgain"], 8), rows(raw["dcbias"], 8), rows(qks, 8)], axis=0)


def _adamw_small(gpack, w, m, v):
    n = len(SMALL)
    ix = {name: i for i, name in enumerate(SMALL)}

    def body(g_ref, *refs):
        w_refs, m_refs, v_refs, outs = refs[:n], refs[n:2 * n], refs[2 * n:3 * n], refs[3 * n:]
        d = _dev_index(*_position())

        def step(name, idx, gv):
            i = ix[name]
            delta, mn, vn = _adam_math(w_refs[i][idx], gv, m_refs[i][idx], v_refs[i][idx])
            for ref, val in zip(outs[4 * i:4 * i + 4], (gv, delta, mn, vn)):
                ref[idx] = val

        def whole(name, row, nrows):
            step(name, (slice(None), slice(None)), g_ref[row:row + nrows, :])

        whole("mix_norm_gain", R_GMIX, 8)
        whole("b_out", R_BOUT, 8)
        whole("ffn_norm_gain", R_GFFN, 8)
        whole("conv_dw_b", R_CB, 4)
        whole("conv_norm_gain", R_CGAIN, 4)
        whole("conv_norm_bias", R_CBIAS, 4)
        whole("ffn_dw_b", R_FB, 2 * D_FF // LANES)
        nq = QKV_COLS // LANES
        step("b_in", (slice(0, nq), slice(None)), g_ref[R_BQKV:R_BQKV + nq, :])
        step("b_in", (slice(nq, nq + CIN_COLS // LANES), slice(None)), g_ref[R_BCIN:R_BCIN + CIN_COLS // LANES, :])
        step("q_norm_gain", (slice(None), slice(None)), g_ref[R_QKS:R_QKS + 1, 0:HEAD_DIM])
        step("k_norm_gain", (slice(None), slice(None)), g_ref[R_QKS + 1:R_QKS + 2, 0:HEAD_DIM])
        step("attn_sinks", (slice(None), slice(None)), g_ref[R_QKS + 2:R_QKS + 3, 0:N_Q_HEADS])
        blk = g_ref[pl.ds(pl.multiple_of(R_CW + CW_ROWS * lax.shift_right_logical(d, 1), SUBLANES), CW_ROWS), :]
        blk = jnp.where((d & 1) == 1, pltpu.roll(blk, HEAD_DIM, 1), blk)
        step("conv_dw_w", (slice(None), slice(None)), blk[0:CONV_KERNEL, 0:CONV_WIDTH // N_DEV])
        blk = g_ref[pl.ds(pl.multiple_of(R_FW + FW_ROWS * d, SUBLANES), FW_ROWS), :]
        for k in range(3):
            for j in range(FF_LANE_CHUNKS):
                wd = min(LANES, FF_CHUNK - j * LANES)
                row = k * FF_LANE_CHUNKS + j
                step("ffn_dw_w", (slice(k, k + 1), slice(j * LANES, j * LANES + wd)), blk[row:row + 1, 0:wd])

    vmem = pl.BlockSpec(memory_space=pltpu.VMEM)
    args = [gpack] + [d[name] for d in (w, m, v) for name in SMALL]
    outs = pl.pallas_call(
        body, in_specs=[vmem] * len(args), out_specs=[vmem] * (4 * n),
        out_shape=[jax.ShapeDtypeStruct(w[name].shape, F32) for name in SMALL for _ in range(4)],
        compiler_params=pltpu.CompilerParams(vmem_limit_bytes=VMEM_LIMIT), name="adamw_small")(*args)
    return {name: outs[4 * i:4 * i + 4] for i, name in enumerate(SMALL)}


def _token_mixing(x, p, attn_shards, conv_shards):
    qkv, cin, h1 = _mix_proj(x, p["g_mix"], p["w_in_t"], p["b_qkv"], p["b_cin"])
    attn, *from_attn = _attn_fwd(qkv, p["gq2"], p["gk2"], p["sinks"], attn_shards)
    c3, c1, *from_conv = _conv_fwd(cin, p["cw8"], p["cb"], p["cgain"], p["cbias"], conv_shards)
    return (qkv, cin, h1, attn, c3, c1), from_attn, from_conv


def _rest_of_step(x, target, p, saved, scatter):
    s = x.shape[0]
    qkv, cin, h1, attn, c3, c1 = saved
    cw8, w_out, w_up, w_down = p["cw8"], p["w_out"], p["w_up"], p["w_down"]
    x2, h2 = _out_proj(x, attn, c3, w_out, w_out, p["b_out"], p["g_ffn"])
    fw, fb = p["fw"], p["fb"]
    up0, gu, act, dy, dyb, loss = _ffn_fwd(h2, x2, target, w_up, fw, fb, w_down)
    dup0, dh2, dfw, dfb = _ffn_bwd(dyb, up0, gu, w_up, fw, w_down)
    dx2, dmixed, dg_ffn, db_out = _ffn_norm_bwd(dh2, dy, x2, p["g_ffn"], w_out)
    dw_up = _tn_matmul(dup0.reshape(N_DEV, s, FF_CHUNK), h2[None], "dw_up", 2 * TN_TOKENS)
    dw_down = _tn_matmul(act, dyb[None], "dw_down", 2 * TN_TOKENS).reshape(N_DEV, -1, D_MODEL)
    dw_out = _tn_matmul_pair(attn, c3, dx2, "dw_out").reshape(N_DEV, -1, D_MODEL)
    dcin, dcw, dcb, dcgain, dcbias, dbcin, *g_up = _conv_bwd(dmixed, c1, cin, cw8, p["cgain"], p["cbias"], [dw_up] if scatter else [])
    dqkv, dgq, dgk, dsink, dbqkv, *g_down_out = _attn_bwd(qkv, dmixed, p["gq2"], p["gk2"], p["sinks"],
                                                          [dw_down, dw_out] if scatter else [])
    dw_in = _tn_matmul_pair(dqkv, dcin, h1, "dw_in").reshape(N_DEV, -1, D_MODEL)
    grad_x, dg_mix = _in_bwd(dqkv, dcin, p["w_in_t"], x, dx2, p["g_mix"])
    if scatter:
        big = {"w_up": g_up[0], "w_down": g_down_out[0], "w_in": dw_in, "w_out": g_down_out[1]}
    else:
        big = {"w_up": dw_up, "w_down": dw_down, "w_in": dw_in, "w_out": dw_out}
    small = dict(dg_mix=dg_mix, dbqkv=dbqkv, dbcin=dbcin, dgq=dgq, dgk=dgk, dsink=dsink, dcw=dcw, dcb=dcb, dcgain=dcgain,
                 dcbias=dcbias, db_out=db_out, dg_ffn=dg_ffn, dfw=dfw, dfb=dfb, loss=loss)
    return loss, grad_x, big, small


BIG = ("w_in", "w_out", "w_up", "w_down")
SMALL = ("mix_norm_gain", "b_in", "q_norm_gain", "k_norm_gain", "attn_sinks", "conv_dw_w", "conv_dw_b",
         "conv_norm_gain", "conv_norm_bias", "b_out", "ffn_norm_gain", "ffn_dw_w", "ffn_dw_b")
ORDER = ("mix_norm_gain", "w_in", "b_in", "q_norm_gain", "k_norm_gain", "attn_sinks", "conv_dw_w", "conv_dw_b",
         "conv_norm_gain", "conv_norm_bias", "w_out", "b_out", "ffn_norm_gain", "w_up", "ffn_dw_w", "ffn_dw_b", "w_down")


def kernel(x, mix_norm_gain, w_in, b_in, q_norm_gain, k_norm_gain, attn_sinks, conv_dw_w, conv_dw_b, conv_norm_gain, conv_norm_bias, w_out, b_out, ffn_norm_gain, w_up, ffn_dw_w, ffn_dw_b, w_down, loss_target, m_mix_norm_gain, m_w_in, m_b_in, m_q_norm_gain, m_k_norm_gain, m_attn_sinks, m_conv_dw_w, m_conv_dw_b, m_conv_norm_gain, m_conv_norm_bias, m_w_out, m_b_out, m_ffn_norm_gain, m_w_up, m_ffn_dw_w, m_ffn_dw_b, m_w_down, v_mix_norm_gain, v_w_in, v_b_in, v_q_norm_gain, v_k_norm_gain, v_attn_sinks, v_conv_dw_w, v_conv_dw_b, v_conv_norm_gain, v_conv_norm_bias, v_w_out, v_b_out, v_ffn_norm_gain, v_w_up, v_ffn_dw_w, v_ffn_dw_b, v_w_down):
    w = dict(mix_norm_gain=mix_norm_gain, w_in=w_in, b_in=b_in, q_norm_gain=q_norm_gain, k_norm_gain=k_norm_gain,
             attn_sinks=attn_sinks, conv_dw_w=conv_dw_w, conv_dw_b=conv_dw_b, conv_norm_gain=conv_norm_gain,
             conv_norm_bias=conv_norm_bias, w_out=w_out, b_out=b_out, ffn_norm_gain=ffn_norm_gain, w_up=w_up,
             ffn_dw_w=ffn_dw_w, ffn_dw_b=ffn_dw_b, w_down=w_down)
    m = dict(mix_norm_gain=m_mix_norm_gain, w_in=m_w_in, b_in=m_b_in, q_norm_gain=m_q_norm_gain, k_norm_gain=m_k_norm_gain,
             attn_sinks=m_attn_sinks, conv_dw_w=m_conv_dw_w, conv_dw_b=m_conv_dw_b, conv_norm_gain=m_conv_norm_gain,
             conv_norm_bias=m_conv_norm_bias, w_out=m_w_out, b_out=m_b_out, ffn_norm_gain=m_ffn_norm_gain, w_up=m_w_up,
             ffn_dw_w=m_ffn_dw_w, ffn_dw_b=m_ffn_dw_b, w_down=m_w_down)
    v = dict(mix_norm_gain=v_mix_norm_gain, w_in=v_w_in, b_in=v_b_in, q_norm_gain=v_q_norm_gain, k_norm_gain=v_k_norm_gain,
             attn_sinks=v_attn_sinks, conv_dw_w=v_conv_dw_w, conv_dw_b=v_conv_dw_b, conv_norm_gain=v_conv_norm_gain,
             conv_norm_bias=v_conv_norm_bias, w_out=v_w_out, b_out=v_b_out, ffn_norm_gain=v_ffn_norm_gain, w_up=v_w_up,
             ffn_dw_w=v_ffn_dw_w, ffn_dw_b=v_ffn_dw_b, w_down=v_w_down)
    s = x.shape[1]

    wi8, cw8, fw8 = _allgather([w_in.T, conv_dw_w, ffn_dw_w], [BF16, F32, F32])
    lane_pad = ((0, 0), (0, 0), (0, FF_PADDED - FF_CHUNK))
    p = {
        "g_mix": mix_norm_gain.reshape(1, -1), "w_in_t": wi8.reshape(QKV_COLS + CIN_COLS, D_MODEL),
        "b_qkv": b_in[:QKV_COLS].reshape(1, -1), "b_cin": b_in[QKV_COLS:].reshape(1, -1),
        "gq2": jnp.tile(q_norm_gain, 2).reshape(1, -1), "gk2": jnp.tile(k_norm_gain, 2).reshape(1, -1), "sinks": attn_sinks,
        "cw8": jnp.repeat(cw8.transpose(1, 0, 2).reshape(CONV_KERNEL, CONV_WIDTH), SUBLANES, axis=0),
        "cb": conv_dw_b.reshape(1, -1), "cgain": conv_norm_gain.reshape(1, -1), "cbias": conv_norm_bias.reshape(1, -1),
        "b_out": b_out.reshape(1, -1), "g_ffn": ffn_norm_gain.reshape(1, -1),
        "fw": jnp.pad(fw8, lane_pad), "fb": jnp.pad(ffn_dw_b.reshape(N_DEV, 1, FF_CHUNK), lane_pad),
    }

    saved, (wu8,), (wo8, wd8) = _token_mixing(x[0], p, [w_up.T], [w_out, w_down])
    p.update(w_out=wo8.reshape(D_MODEL, D_MODEL), w_up=wu8, w_down=wd8.reshape(N_FF_PAIRS, FF_CHUNK, D_MODEL))
    loss, grad_x, big, small = _rest_of_step(x[0], loss_target[0], p, saved, True)

    g = dict(big)
    g["w_in"], gpack = _final_exchange(big["w_in"], _pack_small(small))

    delta, new_m, new_v = {}, {}, {}
    for n in BIG:
        if n in ("w_in", "w_up"):
            outs = _adamw(w[n].T, g[n], m[n].T, v[n].T, "adamw_" + n)
            g[n], delta[n], new_m[n], new_v[n] = g[n].T, *[o.T for o in outs]
        else:
            delta[n], new_m[n], new_v[n] = _adamw(w[n], g[n], m[n], v[n], "adamw_" + n)

    def view(a):
        return a if a.ndim == 2 else (a.reshape(-1, LANES) if a.size % LANES == 0 else a.reshape(1, -1))

    small_out = _adamw_small(gpack, *[{n: view(d[n]) for n in SMALL} for d in (w, m, v)])
    for n in SMALL:
        g[n], delta[n], new_m[n], new_v[n] = [a.reshape(w[n].shape) for a in small_out[n]]

    total = gpack[R_QKS + 3, 0]
    return (total, grad_x.reshape(1, s, D_MODEL), *[g[n] for n in ORDER], *[delta[n] for n in ORDER],
            *[new_m[n] for n in ORDER], *[new_v[n] for n in ORDER])
```

```python
import functools
import math

import jax
import jax.numpy as jnp
from jax import lax
from jax.experimental import pallas as pl
from jax.experimental.pallas import tpu as pltpu

F32 = jnp.float32
BF16 = jnp.bfloat16

D_MODEL = 1024
HEAD_DIM = 64
N_Q_HEADS = 8
N_KV_HEADS = 2
Q_COLS = 512
KV_COLS = 128
QKV_COLS = Q_COLS + 2 * KV_COLS
CONV_WIDTH = 512
CIN_COLS = 2 * CONV_WIDTH
CONV_KERNEL = 31
CONV_HALO = 32
D_FF = 2816
N_DEV = 8
FF_CHUNK = 2 * D_FF // N_DEV
N_FF_PAIRS = N_DEV // 2
ATT_BLOCK = 128
EPS = 1e-6
NEG_INF = -1e30
SLOPES = [float(2.0 ** (-8.0 * (h + 1.0) / N_Q_HEADS)) for h in range(N_Q_HEADS)]

ADAM_LR = 0.001
ADAM_B1 = 0.9
ADAM_B2 = 0.999
ADAM_EPS = 1e-08
ADAM_WD = 0.01
ADAM_STEP = 10

LANES = 128
SUBLANES = 8
VMEM_LIMIT = 56 * 1024 * 1024
MESH = pl.DeviceIdType.MESH


def _cparams(*sem, **kw):
    return pltpu.CompilerParams(dimension_semantics=sem or None, vmem_limit_bytes=VMEM_LIMIT, **kw)


def _resident(shape):
    nd = len(shape)
    return pl.BlockSpec(shape, lambda *_: (0,) * nd, pipeline_mode=pl.Buffered(1))


def _dot(a, b):
    return jnp.dot(a, b, preferred_element_type=F32)


def _dot_nt(a, b):
    return lax.dot_general(a, b, (((1,), (1,)), ((), ())), preferred_element_type=F32)


def _dot_tn(a, b):
    return lax.dot_general(a, b, (((0,), (0,)), ((), ())), preferred_element_type=F32)


def _sigmoid(x):
    return 1.0 / (1.0 + jnp.exp(-x))


def _lo_mask(shape):
    return lax.broadcasted_iota(jnp.int32, shape, len(shape) - 1) % LANES < HEAD_DIM


def _half_sums(t, lo):
    s_lo = jnp.sum(jnp.where(lo, t, 0.0), axis=-1, keepdims=True)
    s_hi = jnp.sum(jnp.where(lo, 0.0, t), axis=-1, keepdims=True)
    return jnp.where(lo, s_lo, s_hi)


def _head_norm(t, lo):
    r = lax.rsqrt(_half_sums(t * t, lo) * (1.0 / HEAD_DIM) + EPS)
    return t * r, r


def _head_norm_bwd(dn, n, r, lo):
    return r * (dn - n * (_half_sums(dn * n, lo) * (1.0 / HEAD_DIM)))


def _tile(s):
    return min(512, s)


def _wide_tile(s):
    return min(1024, s)


TN_TOKENS = 2048
FF_COLS = ((0, 256), (256, 512), (512, 704))


def _position():
    return lax.axis_index("x"), lax.axis_index("y"), lax.axis_index("c")


def _dev_index(px, py, pc):
    return 4 * px + 2 * py + pc


def _flip(v, bit):
    return 1 - v if bit else v


OTHER_CHIPS = ((1, 0), (0, 1), (1, 1))
N_GATHER_COPIES = 1 + 2 * len(OTHER_CHIPS)


class _Gather:
    def __init__(self, shard_ref, out_ref, cast_buf, send_sems, recv_sems, local_sem):
        self.shard, self.out, self.buf = shard_ref, out_ref, cast_buf
        self.send_sems, self.recv_sems, self.local_sem = send_sems, recv_sems, local_sem
        x, y, c = _position()
        self.c = c
        self.me, self.sibling = (x, y, c), (x, y, 1 - c)
        self.chips = [(_flip(x, fx), _flip(y, fy)) for fx, fy in OTHER_CHIPS]

    def _copy(self, k, block, to, from_buf=False):
        rows = self.out.at[_dev_index(*block)]
        return pltpu.make_async_remote_copy(src_ref=self.buf if from_buf else rows, dst_ref=rows,
                                            send_sem=self.send_sems.at[k], recv_sem=self.recv_sems.at[k],
                                            device_id=to, device_id_type=MESH)

    def _local(self):
        return pltpu.make_async_copy(self.buf, self.out.at[_dev_index(*self.me)], self.local_sem)

    def start(self):
        self.buf[...] = self.shard[...].astype(self.buf.dtype)
        self._local().start()
        for j, chip in enumerate(self.chips):
            self._copy(1 + j, self.me, (*chip, self.c), from_buf=True).start()
        self._copy(0, self.me, self.sibling, from_buf=True).start()

    def forward(self):
        for j, chip in enumerate(self.chips):
            self._copy(1 + j, (*chip, self.c), self.me).wait_recv()
            self._copy(1 + len(self.chips) + j, (*chip, self.c), self.sibling).start()

    def finish(self):
        self._copy(0, self.sibling, self.me).wait_recv()
        for j, chip in enumerate(self.chips):
            self._copy(1 + len(self.chips) + j, (*chip, 1 - self.c), self.me).wait_recv()
        for k in range(N_GATHER_COPIES):
            self._copy(k, self.me, self.sibling).wait_send()
        self._local().wait()


def _gather_specs(shards):
    whole = [pl.BlockSpec(w.shape, lambda *_, nd=w.ndim: (0,) * nd, pipeline_mode=pl.Buffered(1)) for w in shards]
    outs = [pl.BlockSpec(memory_space=pl.ANY) for _ in shards]
    shapes = [jax.ShapeDtypeStruct((N_DEV,) + w.shape, BF16) for w in shards]
    scratch = []
    for w in shards:
        scratch += [pltpu.VMEM(w.shape, BF16), pltpu.SemaphoreType.DMA((N_GATHER_COPIES,)),
                    pltpu.SemaphoreType.DMA((N_GATHER_COPIES,)), pltpu.SemaphoreType.DMA(())]
    return whole, outs, shapes, scratch


def _run_gathers(gathers, step, n_steps):
    @pl.when(step == 0)
    def _():
        for g in gathers:
            g.start()

    @pl.when(step == 3 * n_steps // 4)
    def _():
        for g in gathers:
            g.forward()

    @pl.when(step == n_steps - 1)
    def _():
        for g in gathers:
            g.finish()


class _ReduceScatter:
    def __init__(self, g_ref, out_ref, stage, load_sems, send_a, recv_a, send_b, recv_b, sa_send, sa_recv, sb_send, sb_recv):
        self.g, self.out, self.stage, self.load_sems = g_ref, out_ref, stage, load_sems
        self.send_a, self.recv_a, self.send_b, self.recv_b = send_a, recv_a, send_b, recv_b
        self.sems = (sa_send, sa_recv, sb_send, sb_recv)
        x, y, c = _position()
        self.c, self.sibling = c, (x, y, 1 - c)
        self.chips = [(x, y)] + [(_flip(x, fx), _flip(y, fy)) for fx, fy in OTHER_CHIPS]

    def _copy_a(self, j):
        return pltpu.make_async_remote_copy(src_ref=self.send_a.at[j], dst_ref=self.recv_a.at[j], send_sem=self.sems[0].at[j],
                                            recv_sem=self.sems[1].at[j], device_id=self.sibling, device_id_type=MESH)

    def _copy_b(self, j):
        return pltpu.make_async_remote_copy(src_ref=self.send_b.at[j], dst_ref=self.recv_b.at[j], send_sem=self.sems[2].at[j],
                                            recv_sem=self.sems[3].at[j], device_id=(*self.chips[1 + j], self.c),
                                            device_id_type=MESH)

    def _load(self, j, core):
        return pltpu.make_async_copy(self.g.at[_dev_index(*self.chips[j], core)], self.stage.at[j % 2], self.load_sems.at[j % 2])

    def send_block(self, j):
        if j == 0:
            self._load(0, 1 - self.c).start()
        self._load(j, 1 - self.c).wait()
        if j + 1 < len(self.chips):
            self._load(j + 1, 1 - self.c).start()
        else:
            self._load(0, self.c).start()
        self.send_a[j] = self.stage[j % 2].astype(BF16)
        self._copy_a(j).start()

    def sum_block(self, j):
        self._load(j, self.c).wait()
        if j + 1 < len(self.chips):
            self._load(j + 1, self.c).start()
        self._copy_a(j).wait_recv()
        part = self.stage[j % 2] + self.recv_a[j].astype(F32)
        if j == 0:
            self.out[...] = part
        else:
            self.send_b[j - 1] = part.astype(BF16)
            self._copy_b(j - 1).start()

    def start(self):
        for j in range(len(self.chips)):
            self.send_block(j)

    def middle(self):
        for j in range(len(self.chips)):
            self.sum_block(j)

    def finish(self):
        for j in range(len(OTHER_CHIPS)):
            self._copy_b(j).wait_recv()
            self.out[...] += self.recv_b[j].astype(F32)
        for j in range(len(self.chips)):
            self._copy_a(j).wait_send()
        for j in range(len(OTHER_CHIPS)):
            self._copy_b(j).wait_send()


N_SCATTER_SCRATCH = 10


def _scatter_specs(g8s):
    na, nb = 1 + len(OTHER_CHIPS), len(OTHER_CHIPS)
    ins = [pl.BlockSpec(memory_space=pl.ANY) for _ in g8s]
    outs = [pl.BlockSpec(g.shape[1:], lambda *_: (0, 0)) for g in g8s]
    shapes = [jax.ShapeDtypeStruct(g.shape[1:], F32) for g in g8s]
    scratch = []
    for g in g8s:
        blk = g.shape[1:]
        scratch += [pltpu.VMEM((2,) + blk, F32), pltpu.SemaphoreType.DMA((2,)), pltpu.VMEM((na,) + blk, BF16), pltpu.VMEM((na,) + blk, BF16),
                    pltpu.VMEM((nb,) + blk, BF16), pltpu.VMEM((nb,) + blk, BF16),
                    pltpu.SemaphoreType.DMA((na,)), pltpu.SemaphoreType.DMA((na,)),
                    pltpu.SemaphoreType.DMA((nb,)), pltpu.SemaphoreType.DMA((nb,))]
    return ins, outs, shapes, scratch


def _run_scatters(scatters, step, n_steps):
    n_blocks = 1 + len(OTHER_CHIPS)
    for j in range(n_blocks):
        @pl.when(step == min(j, n_steps - 1))
        def _(j=j):
            for r in scatters:
                r.send_block(j)

    for j in range(n_blocks):
        @pl.when(step == min(n_blocks + j, n_steps - 1))
        def _(j=j):
            for r in scatters:
                r.sum_block(j)

    @pl.when(step == n_steps - 1)
    def _():
        for r in scatters:
            r.finish()


def _mix_proj(x, g_mix, w_in_t, b_qkv, b_cin):
    s = x.shape[0]
    tm = _wide_tile(s)

    def body(x_ref, g_ref, w_ref, bq_ref, bc_ref, qkv_ref, cin_ref, h1_ref):
        xv = x_ref[...]
        r = lax.rsqrt(jnp.mean(xv * xv, axis=-1, keepdims=True) + EPS)
        h = (xv * r * g_ref[...]).astype(BF16)
        h1_ref[...] = h
        qkv_ref[...] = _dot_nt(h, w_ref[0:QKV_COLS, :]) + bq_ref[...]
        cin_ref[...] = _dot_nt(h, w_ref[QKV_COLS:, :]) + bc_ref[...]

    return pl.pallas_call(
        body, grid=(s // tm,),
        in_specs=[pl.BlockSpec((tm, D_MODEL), lambda i: (i, 0)), _resident((1, D_MODEL)),
                  _resident((QKV_COLS + CIN_COLS, D_MODEL)), _resident((1, QKV_COLS)), _resident((1, CIN_COLS))],
        out_specs=[pl.BlockSpec((tm, QKV_COLS), lambda i: (i, 0)), pl.BlockSpec((tm, CIN_COLS), lambda i: (i, 0)),
                   pl.BlockSpec((tm, D_MODEL), lambda i: (i, 0))],
        out_shape=[jax.ShapeDtypeStruct((s, QKV_COLS), F32), jax.ShapeDtypeStruct((s, CIN_COLS), F32),
                   jax.ShapeDtypeStruct((s, D_MODEL), BF16)],
        compiler_params=_cparams("parallel"), name="mix_proj")(x, g_mix, w_in_t, b_qkv, b_cin)


def _kv_variants(kv_all, gk2, lo):
    k_all = kv_all[:, :LANES]
    v_all = kv_all[:, LANES:]
    kn_pre, rk = _head_norm(k_all, lo)
    kn = kn_pre * gk2
    kr = pltpu.roll(kn, HEAD_DIM, 1)
    vr = pltpu.roll(v_all, HEAD_DIM, 1)
    zero = jnp.zeros_like(kn)
    k_lo = [jnp.where(lo, kn, zero).astype(BF16), jnp.where(lo, kr, zero).astype(BF16)]
    k_hi = [jnp.where(lo, zero, kr).astype(BF16), jnp.where(lo, zero, kn).astype(BF16)]
    v_lo = [jnp.where(lo, v_all, zero).astype(BF16), jnp.where(lo, vr, zero).astype(BF16)]
    v_hi = [jnp.where(lo, zero, vr).astype(BF16), jnp.where(lo, zero, v_all).astype(BF16)]
    return k_lo, k_hi, v_lo, v_hi, kn_pre, rk


def _att_consts(first_tile, b):
    rows = 2 * ATT_BLOCK
    qi = lax.broadcasted_iota(jnp.int32, (rows, 2 * ATT_BLOCK), 0) % ATT_BLOCK
    kj = lax.broadcasted_iota(jnp.int32, (rows, 2 * ATT_BLOCK), 1)
    rel = qi + ATT_BLOCK - kj
    valid = (rel >= 0) & (rel < ATT_BLOCK)
    if b == 0:
        valid = valid & ((kj >= ATT_BLOCK) | jnp.logical_not(first_tile))
    return rel.astype(F32), valid


def _row_const(va, vb):
    top = lax.broadcasted_iota(jnp.int32, (2 * ATT_BLOCK, 1), 0) < ATT_BLOCK
    return jnp.where(top, va, vb)


def _probs(q2, k_op, rel, valid, slope, sink):
    sc = _dot_nt(q2, k_op) * (1.0 / math.sqrt(HEAD_DIM)) - slope * rel
    sc = jnp.where(valid, sc, NEG_INF)
    m = jnp.maximum(jnp.max(sc, axis=-1, keepdims=True), sink)
    p = jnp.exp(sc - m)
    e_sink = jnp.exp(sink - m)
    inv = 1.0 / (jnp.sum(p, axis=-1, keepdims=True) + e_sink)
    return p * inv, e_sink * inv


def _attn_fwd(qkv, gq2, gk2, sinks, shards):
    s = qkv.shape[0]
    tq = _tile(s)
    nb = tq // ATT_BLOCK
    ng = len(shards)
    g_in, g_out, g_shape, g_scratch = _gather_specs(shards)

    def body(q_ref, kv_ref, kvp_ref, gq_ref, gk_ref, sink_ref, *rest):
        out_ref = rest[ng]
        i = pl.program_id(0)
        _run_gathers([_Gather(rest[a], rest[ng + 1 + a], *rest[2 * ng + 1 + 4 * a:2 * ng + 5 + 4 * a]) for a in range(ng)],
                     i, s // tq)
        lo = _lo_mask((1, LANES))
        kv_all = jnp.concatenate([kvp_ref[...], kv_ref[...]], axis=0)
        k_lo, k_hi, v_lo, v_hi, _, _ = _kv_variants(kv_all, gk_ref[...], lo)
        for b in range(nb):
            rel, valid = _att_consts(i == 0, b)
            rows = slice(b * ATT_BLOCK, (b + 1) * ATT_BLOCK)
            keys = slice(b * ATT_BLOCK, (b + 2) * ATT_BLOCK)
            for kvh in range(N_KV_HEADS):
                pairs = (2 * kvh, 2 * kvh + 1)
                q2 = jnp.concatenate([q_ref[rows, p * LANES:(p + 1) * LANES] for p in pairs], axis=0)
                qn, _ = _head_norm(q2, lo)
                q2 = (qn * gq_ref[...]).astype(BF16)
                out = None
                for odd, (k_op, v_op) in enumerate(((k_lo[kvh][keys], v_lo[kvh][keys]), (k_hi[kvh][keys], v_hi[kvh][keys]))):
                    ha, hb = 2 * pairs[0] + odd, 2 * pairs[1] + odd
                    p, _ = _probs(q2, k_op, rel, valid, _row_const(SLOPES[ha], SLOPES[hb]),
                                  _row_const(sink_ref[ha], sink_ref[hb]))
                    o = _dot(p.astype(BF16), v_op)
                    out = o if out is None else out + o
                for n, p in enumerate(pairs):
                    out_ref[rows, p * LANES:(p + 1) * LANES] = out[n * ATT_BLOCK:(n + 1) * ATT_BLOCK].astype(BF16)

    return pl.pallas_call(
        body, grid=(s // tq,),
        in_specs=[pl.BlockSpec((tq, Q_COLS), lambda i: (i, 0)),
                  pl.BlockSpec((tq, 2 * KV_COLS), lambda i: (i, 2)),
                  pl.BlockSpec((ATT_BLOCK, 2 * KV_COLS), lambda i: (jnp.maximum(i * nb - 1, 0), 2)),
                  _resident((1, LANES)), _resident((1, LANES)),
                  pl.BlockSpec(memory_space=pltpu.SMEM)] + g_in,
        out_specs=[pl.BlockSpec((tq, Q_COLS), lambda i: (i, 0))] + g_out,
        out_shape=[jax.ShapeDtypeStruct((s, Q_COLS), BF16)] + g_shape,
        scratch_shapes=g_scratch,
        compiler_params=_cparams("arbitrary"), name="attn_fwd")(qkv, qkv, qkv, gq2, gk2, sinks, *shards)


def _group_stats(c1, lo):
    mu = _half_sums(c1, lo) * (1.0 / HEAD_DIM)
    d = c1 - mu
    rstd = lax.rsqrt(_half_sums(d * d, lo) * (1.0 / HEAD_DIM) + EPS)
    return d * rstd, rstd


def _rows(ref, first_row, n):
    return ref[pl.ds(first_row, n, stride=1), :].reshape(n // SUBLANES, SUBLANES, LANES)


def _conv_fwd(cin, cw8, cb, gain, bias, shards):
    s = cin.shape[0]
    tm = _tile(s)
    rc = 64
    nchunk = CONV_WIDTH // LANES
    lead = CONV_HALO - (CONV_KERNEL - 1)
    ng = len(shards)
    g_in, g_out, g_shape, g_scratch = _gather_specs(shards)

    def body(cin_ref, cw_ref, cb_ref, gain_ref, bias_ref, *rest):
        c3_ref, c1_ref, ext_ref = rest[ng], rest[ng + 1], rest[2 * ng + 2]
        _run_gathers([_Gather(rest[a], rest[ng + 2 + a], *rest[2 * ng + 3 + 4 * a:2 * ng + 7 + 4 * a]) for a in range(ng)],
                     pl.program_id(0), s // tm)

        @pl.when(pl.program_id(0) == 0)
        def _():
            ext_ref[:, 0:CONV_HALO, :] = jnp.zeros((nchunk, CONV_HALO, LANES), F32)

        lo = _lo_mask((1, LANES))
        for cc in range(nchunk):
            cols = slice(cc * LANES, (cc + 1) * LANES)
            gcols = slice(CONV_WIDTH + cc * LANES, CONV_WIDTH + (cc + 1) * LANES)
            ext_ref[cc, CONV_HALO:CONV_HALO + tm, :] = cin_ref[:, cols] * _sigmoid(cin_ref[:, gcols])
            ext = ext_ref.at[cc]
            for r in range(tm // rc):
                rows = slice(r * rc, (r + 1) * rc)
                acc = jnp.zeros((rc // SUBLANES, SUBLANES, LANES), F32)
                for k in range(CONV_KERNEL):
                    acc = acc + cw_ref[k * SUBLANES:(k + 1) * SUBLANES, cols][None] * _rows(ext, r * rc + lead + k, rc)
                c1 = acc.reshape(rc, LANES) + cb_ref[:, cols]
                c1_ref[cc, rows, :] = c1
                nrm, _ = _group_stats(c1, lo)
                c2 = nrm * gain_ref[:, cols] + bias_ref[:, cols]
                c3_ref[rows, cols] = (c2 * _sigmoid(c2)).astype(BF16)
        ext_ref[:, 0:CONV_HALO, :] = ext_ref[:, tm:tm + CONV_HALO, :]

    return pl.pallas_call(
        body, grid=(s // tm,),
        in_specs=[pl.BlockSpec((tm, CIN_COLS), lambda i: (i, 0)), _resident((CONV_KERNEL * SUBLANES, CONV_WIDTH)),
                  _resident((1, CONV_WIDTH)), _resident((1, CONV_WIDTH)), _resident((1, CONV_WIDTH))] + g_in,
        out_specs=[pl.BlockSpec((tm, CONV_WIDTH), lambda i: (i, 0)), pl.BlockSpec((nchunk, tm, LANES), lambda i: (0, i, 0))] + g_out,
        out_shape=[jax.ShapeDtypeStruct((s, CONV_WIDTH), BF16), jax.ShapeDtypeStruct((nchunk, s, LANES), F32)] + g_shape,
        scratch_shapes=[pltpu.VMEM((nchunk, tm + CONV_HALO, LANES), F32)] + g_scratch,
        compiler_params=_cparams("arbitrary"), name="conv_fwd")(cin, cw8, cb, gain, bias, *shards)


def _out_proj(x, attn, c3, wo_a, wo_c, b_out, g_ffn):
    s = x.shape[0]
    tm = _wide_tile(s)

    def body(x_ref, a_ref, c_ref, wa_ref, wc_ref, b_ref, g_ref, x2_ref, h2_ref):
        x2 = x_ref[...] + _dot(a_ref[...], wa_ref[...]) + _dot(c_ref[...], wc_ref[...]) + b_ref[...]
        x2_ref[...] = x2
        r = lax.rsqrt(jnp.mean(x2 * x2, axis=-1, keepdims=True) + EPS)
        h2_ref[...] = (x2 * r * g_ref[...]).astype(BF16)

    return pl.pallas_call(
        body, grid=(s // tm,),
        in_specs=[pl.BlockSpec((tm, D_MODEL), lambda i: (i, 0)), pl.BlockSpec((tm, Q_COLS), lambda i: (i, 0)),
                  pl.BlockSpec((tm, CONV_WIDTH), lambda i: (i, 0)),
                  pl.BlockSpec((Q_COLS, D_MODEL), lambda i: (0, 0), pipeline_mode=pl.Buffered(1)),
                  pl.BlockSpec((CONV_WIDTH, D_MODEL), lambda i: (1, 0), pipeline_mode=pl.Buffered(1)),
                  _resident((1, D_MODEL)), _resident((1, D_MODEL))],
        out_specs=[pl.BlockSpec((tm, D_MODEL), lambda i: (i, 0)), pl.BlockSpec((tm, D_MODEL), lambda i: (i, 0))],
        out_shape=[jax.ShapeDtypeStruct((s, D_MODEL), F32), jax.ShapeDtypeStruct((s, D_MODEL), BF16)],
        compiler_params=_cparams("parallel"), name="out_proj")(x, attn, c3, wo_a, wo_c, b_out, g_ffn)


FF_LANE_CHUNKS = -(-FF_CHUNK // LANES)
FF_PADDED = FF_LANE_CHUNKS * LANES


def _tap(ref, first_row, n):
    return ref[pl.ds(first_row, n, stride=1), :]


def _ffn_fwd(h2, x2, target, w_up, fw, fb, w_down):
    s = h2.shape[0]
    tm = _tile(s)
    hal = SUBLANES
    rc = min(128, tm)

    def body(h_ref, x2_ref, t_ref, wu_ref, fw_ref, fb_ref, wd_ref, up0_ref, gu_ref, act_ref, dy_ref, dyb_ref, loss_ref,
             ext_ref, carry_ref, act_buf, y_ref):
        i, ci = pl.program_id(0), pl.program_id(1)

        @pl.when((i == 0) & (ci == 0))
        def _():
            carry_ref[...] = jnp.zeros(carry_ref.shape, F32)
            ext_ref[...] = jnp.zeros(ext_ref.shape, F32)
            act_buf[...] = jnp.zeros(act_buf.shape, BF16)
            loss_ref[...] = jnp.zeros((1, 1), F32)

        @pl.when(ci == 0)
        def _():
            y_ref[...] = x2_ref[...]

        ws = (fw_ref[ci], fw_ref[ci + N_FF_PAIRS])
        bs = (fb_ref[ci], fb_ref[ci + N_FF_PAIRS])
        half_rows = (slice(0, tm // 2), slice(tm // 2, tm))
        n_grp = len(FF_COLS)

        def up_slices(grp):
            lo_c, hi_c = FF_COLS[grp]
            chunks = range(lo_c // LANES, -(-hi_c // LANES))

            def make(half, n, rows):
                def run():
                    c = ci + half * N_FF_PAIRS
                    u0 = _dot_nt(h_ref[rows, :], wu_ref[c, lo_c:hi_c, :])
                    up0_ref[half, 0, rows, lo_c:hi_c] = u0.astype(BF16)
                    if hi_c == FF_CHUNK:
                        up0_ref[half, 0, rows, FF_CHUNK:] = jnp.zeros((u0.shape[0], FF_PADDED - FF_CHUNK), BF16)
                    for j in chunks:
                        w = min(LANES, hi_c - j * LANES)
                        if n == 0:
                            ext_ref[half, j, 0:hal, 0:w] = carry_ref[c, :, j * LANES:j * LANES + w]
                        ext_ref[half, j, hal + rows.start:hal + rows.stop, 0:w] = u0[:, j * LANES - lo_c:j * LANES - lo_c + w]
                    if n == len(half_rows) - 1:
                        carry_ref[c, :, lo_c:hi_c] = u0[u0.shape[0] - hal:, :]
                return run
            return [make(half, n, rows) for half in range(2) for n, rows in enumerate(half_rows)]

        def down_slices(grp):
            lo_c, hi_c = FF_COLS[grp]

            def make(rows):
                def run():
                    y_ref[rows, :] += _dot(act_buf[rows, lo_c:hi_c], wd_ref[ci, lo_c:hi_c, :])
                return run
            return [make(rows) for rows in half_rows]

        def vector_blocks(grp):
            lo_c, hi_c = FF_COLS[grp]
            blocks = []
            for j in range(lo_c // LANES, -(-hi_c // LANES)):
                lanes = slice(j * LANES, (j + 1) * LANES)

                def gate(r, lanes=lanes, j=j):
                    base = r * rc
                    ups = []
                    for half in range(2):
                        e, w = ext_ref.at[half, j], ws[half]
                        ups.append(w[0:1, lanes] * _tap(e, base + hal - 2, rc) + w[1:2, lanes] * _tap(e, base + hal - 1, rc)
                                   + w[2:3, lanes] * _tap(e, base + hal, rc) + bs[half][:, lanes])
                    g, u = ups
                    gu_ref[0, 0, base:base + rc, lanes] = g.astype(BF16)
                    gu_ref[1, 0, base:base + rc, lanes] = u.astype(BF16)
                    act_buf[base:base + rc, lanes] = (g * _sigmoid(g) * u).astype(BF16)

                blocks += [functools.partial(gate, r) for r in range(tm // rc)]

            def finish():
                act_ref[0, :, lo_c:hi_c] = act_buf[:, lo_c:hi_c]
            blocks.append(finish)
            return blocks

        for run in up_slices(0):
            run()
        for grp in range(n_grp):
            matmuls = (up_slices(grp + 1) if grp + 1 < n_grp else []) + (down_slices(grp - 1) if grp > 0 else [])
            blocks = vector_blocks(grp)
            every = max(1, len(blocks) // (len(matmuls) + 1))
            for n, run in enumerate(blocks):
                run()
                if n % every == every - 1 and matmuls:
                    matmuls.pop(0)()
            for run in matmuls:
                run()
        for run in down_slices(n_grp - 1):
            run()

        @pl.when(ci == N_FF_PAIRS - 1)
        def _():
            e = y_ref[...] - t_ref[...]
            dy_ref[...] = e * (1.0 / D_MODEL)
            dyb_ref[...] = (e * (1.0 / D_MODEL)).astype(BF16)
            loss_ref[...] += (0.5 / D_MODEL) * jnp.sum(e * e).reshape(1, 1)

    tok = lambda i, ci: (i, 0)
    return pl.pallas_call(
        body, grid=(s // tm, N_FF_PAIRS),
        in_specs=[pl.BlockSpec((tm, D_MODEL), tok), pl.BlockSpec((tm, D_MODEL), tok), pl.BlockSpec((tm, D_MODEL), tok),
                  _resident((N_DEV, FF_CHUNK, D_MODEL)), _resident((N_DEV, 3, FF_PADDED)), _resident((N_DEV, 1, FF_PADDED)),
                  _resident((N_FF_PAIRS, FF_CHUNK, D_MODEL))],
        out_specs=[pl.BlockSpec((2, 1, tm, FF_PADDED), lambda i, ci: (0, ci, i, 0)),
                   pl.BlockSpec((2, 1, tm, FF_PADDED), lambda i, ci: (0, ci, i, 0)),
                   pl.BlockSpec((1, tm, FF_CHUNK), lambda i, ci: (ci, i, 0)),
                   pl.BlockSpec((tm, D_MODEL), tok), pl.BlockSpec((tm, D_MODEL), tok), pl.BlockSpec((1, 1), lambda i, ci: (0, 0))],
        out_shape=[jax.ShapeDtypeStruct((2, N_FF_PAIRS, s, FF_PADDED), BF16), jax.ShapeDtypeStruct((2, N_FF_PAIRS, s, FF_PADDED), BF16),
                   jax.ShapeDtypeStruct((N_FF_PAIRS, s, FF_CHUNK), BF16), jax.ShapeDtypeStruct((s, D_MODEL), F32),
                   jax.ShapeDtypeStruct((s, D_MODEL), BF16), jax.ShapeDtypeStruct((1, 1), F32)],
        scratch_shapes=[pltpu.VMEM((2, FF_LANE_CHUNKS, tm + hal, LANES), F32), pltpu.VMEM((N_DEV, hal, FF_CHUNK), F32),
                        pltpu.VMEM((tm, FF_PADDED), BF16), pltpu.VMEM((tm, D_MODEL), F32)],
        compiler_params=_cparams("arbitrary", "arbitrary"), name="ffn_fwd")(h2, x2, target, w_up, fw, fb, w_down)


def _ffn_bwd(dyb, up0, gu, w_up, fw, w_down):
    s = dyb.shape[0]
    tm = _tile(s)
    nt = s // tm
    nxt = SUBLANES
    rc = min(128, tm)

    def body(dy_ref, up0_ref, gu_ref, wu_ref, fw_ref, wd_ref,
             dup0_ref, dh2_ref, dfw_ref, dfb_ref, dext_ref, carry_ref, dact_buf, dup0_buf, dh2_acc):
        i, ci = pl.program_id(0), pl.program_id(1)

        @pl.when((i == 0) & (ci == 0))
        def _():
            for ref in (carry_ref, dfw_ref, dfb_ref, dext_ref, dact_buf):
                ref[...] = jnp.zeros(ref.shape, F32)
            dup0_buf[...] = jnp.zeros(dup0_buf.shape, BF16)

        @pl.when(ci == 0)
        def _():
            dh2_acc[...] = jnp.zeros(dh2_acc.shape, F32)

        ws = (fw_ref[ci], fw_ref[ci + N_FF_PAIRS])
        fold = lambda v: jnp.sum(v.reshape(rc // SUBLANES, SUBLANES, LANES), axis=0)
        half_rows = (slice(0, tm // 2), slice(tm // 2, tm))
        n_grp = len(FF_COLS)

        def dact_slices(grp):
            lo_c, hi_c = FF_COLS[grp]

            def make(rows):
                def run():
                    dact_buf[rows, lo_c:hi_c] = _dot_nt(dy_ref[rows, :], wd_ref[ci, lo_c:hi_c, :])
                return run
            return [make(rows) for rows in half_rows]

        def dh2_slices(grp):
            lo_c, hi_c = FF_COLS[grp]

            def make(half, rows):
                def run():
                    c = ci + half * N_FF_PAIRS
                    dh2_acc[rows, :] += _dot(dup0_buf[half, rows, lo_c:hi_c], wu_ref[c, lo_c:hi_c, :])
                return run
            return [make(half, rows) for half in range(2) for rows in half_rows]

        def vector_blocks(grp):
            lo_c, hi_c = FF_COLS[grp]
            chunks = range(lo_c // LANES, -(-hi_c // LANES))
            blocks = []

            def stage():
                for half in range(2):
                    c = ci + half * N_FF_PAIRS
                    for j in chunks:
                        dext_ref[half, j, tm:tm + nxt, :] = carry_ref[c, :, j * LANES:(j + 1) * LANES]
            blocks.append(stage)
            for j in chunks:
                lanes = slice(j * LANES, (j + 1) * LANES)
                acc = [jnp.zeros((SUBLANES, LANES), F32)] * 8

                def grads(r, lanes=lanes, j=j, acc=acc):
                    base = r * rc
                    g = gu_ref[0, 0, base:base + rc, lanes].astype(F32)
                    u = gu_ref[1, 0, base:base + rc, lanes].astype(F32)
                    sg = _sigmoid(g)
                    silu = g * sg
                    dact = dact_buf[base:base + rc, lanes]
                    ds = (dact * u * (sg + silu - silu * sg), dact * silu)
                    for half in range(2):
                        dext_ref[half, j, base:base + rc, :] = ds[half]
                        acc[4 * half] = acc[4 * half] + fold(ds[half])

                def conv_back(r, lanes=lanes, j=j, acc=acc):
                    base = r * rc
                    for half in range(2):
                        d, w = dext_ref.at[half, j], ws[half]
                        taps = [_tap(d, base + k, rc) for k in range(3)]
                        dup0 = w[2:3, lanes] * taps[0] + w[1:2, lanes] * taps[1] + w[0:1, lanes] * taps[2]
                        dup0_buf[half, base:base + rc, lanes] = dup0.astype(BF16)
                        u0 = up0_ref[half, 0, base:base + rc, lanes].astype(F32)
                        for k in range(3):
                            acc[4 * half + 1 + k] = acc[4 * half + 1 + k] + fold(taps[2 - k] * u0)

                def sums(lanes=lanes, j=j, acc=acc):
                    for half in range(2):
                        c = ci + half * N_FF_PAIRS
                        carry_ref[c, :, lanes] = dext_ref[half, j, 0:nxt, :]
                        dfb_ref[c, :, lanes] += jnp.sum(acc[4 * half], axis=0, keepdims=True)
                        dfw_ref[c, :, lanes] += jnp.concatenate(
                            [jnp.sum(acc[4 * half + 1 + k], axis=0, keepdims=True) for k in range(3)], axis=0)

                blocks += [functools.partial(grads, r) for r in range(tm // rc)]
                blocks += [functools.partial(conv_back, r) for r in range(tm // rc)] + [sums]

            def finish():
                for half in range(2):
                    dup0_ref[half, 0, :, lo_c:hi_c] = dup0_buf[half, :, lo_c:hi_c]
            blocks.append(finish)
            return blocks

        for run in dact_slices(0):
            run()
        for grp in range(n_grp):
            matmuls = (dact_slices(grp + 1) if grp + 1 < n_grp else []) + (dh2_slices(grp - 1) if grp > 0 else [])
            blocks = vector_blocks(grp)
            every = max(1, len(blocks) // (len(matmuls) + 1))
            for n, run in enumerate(blocks):
                run()
                if n % every == every - 1 and matmuls:
                    matmuls.pop(0)()
            for run in matmuls:
                run()
        for run in dh2_slices(n_grp - 1):
            run()

        @pl.when(ci == N_FF_PAIRS - 1)
        def _():
            dh2_ref[...] = dh2_acc[...].astype(BF16)

    tok = lambda i, ci: (nt - 1 - i, 0)
    acc = lambda shape: pl.BlockSpec(shape, lambda i, ci: (0,) * len(shape))
    saved = pl.BlockSpec((2, 1, tm, FF_PADDED), lambda i, ci: (0, ci, nt - 1 - i, 0))
    return pl.pallas_call(
        body, grid=(nt, N_FF_PAIRS),
        in_specs=[pl.BlockSpec((tm, D_MODEL), tok), saved, saved,
                  _resident((N_DEV, FF_CHUNK, D_MODEL)), _resident((N_DEV, 3, FF_PADDED)),
                  _resident((N_FF_PAIRS, FF_CHUNK, D_MODEL))],
        out_specs=[pl.BlockSpec((2, 1, tm, FF_CHUNK), lambda i, ci: (0, ci, nt - 1 - i, 0)),
                   pl.BlockSpec((tm, D_MODEL), tok), acc((N_DEV, 3, FF_PADDED)), acc((N_DEV, 1, FF_PADDED))],
        out_shape=[jax.ShapeDtypeStruct((2, N_FF_PAIRS, s, FF_CHUNK), BF16), jax.ShapeDtypeStruct((s, D_MODEL), BF16),
                   jax.ShapeDtypeStruct((N_DEV, 3, FF_PADDED), F32), jax.ShapeDtypeStruct((N_DEV, 1, FF_PADDED), F32)],
        scratch_shapes=[pltpu.VMEM((2, FF_LANE_CHUNKS, tm + nxt, LANES), F32), pltpu.VMEM((N_DEV, nxt, FF_PADDED), F32),
                        pltpu.VMEM((tm, FF_PADDED), F32), pltpu.VMEM((2, tm, FF_PADDED), BF16), pltpu.VMEM((tm, D_MODEL), F32)],
        compiler_params=_cparams("arbitrary", "arbitrary"), name="ffn_bwd")(dyb, up0, gu, w_up, fw, w_down)


def _ffn_norm_bwd(dh2, dy, x2, g_ffn, w_out):
    s = dy.shape[0]
    tm = _wide_tile(s)

    def body(dh_ref, dy_ref, x2_ref, g_ref, wo_ref, dx2_ref, dmix_ref, dg_ref, dbo_ref):
        @pl.when(pl.program_id(0) == 0)
        def _():
            dg_ref[...] = jnp.zeros(dg_ref.shape, F32)
            dbo_ref[...] = jnp.zeros(dbo_ref.shape, F32)

        x2v = x2_ref[...]
        r = lax.rsqrt(jnp.mean(x2v * x2v, axis=-1, keepdims=True) + EPS)
        n2 = x2v * r
        dh2 = dh_ref[...].astype(F32)
        dg_ref[...] += jnp.sum(dh2 * n2, axis=0, keepdims=True)
        dn = dh2 * g_ref[...]
        dx2 = dy_ref[...] + r * (dn - n2 * jnp.mean(dn * n2, axis=-1, keepdims=True))
        dx2_ref[...] = dx2
        dbo_ref[...] += jnp.sum(dx2, axis=0, keepdims=True)
        dmix_ref[...] = _dot_nt(dx2.astype(BF16), wo_ref[...]).astype(BF16)

    tok = pl.BlockSpec((tm, D_MODEL), lambda i: (i, 0))
    vec = pl.BlockSpec((1, D_MODEL), lambda i: (0, 0))
    return pl.pallas_call(
        body, grid=(s // tm,),
        in_specs=[tok, tok, tok, _resident((1, D_MODEL)), _resident((D_MODEL, D_MODEL))],
        out_specs=[tok, tok, vec, vec],
        out_shape=[jax.ShapeDtypeStruct((s, D_MODEL), F32), jax.ShapeDtypeStruct((s, D_MODEL), BF16),
                   jax.ShapeDtypeStruct((1, D_MODEL), F32), jax.ShapeDtypeStruct((1, D_MODEL), F32)],
        compiler_params=_cparams("arbitrary"), name="ffn_norm_bwd")(dh2, dy, x2, g_ffn, w_out)


def _conv_bwd(dmixed, c1, cin, cw8, gain, bias, g8s):
    ns = len(g8s)
    s_in, s_out, s_shape, s_scratch = _scatter_specs(g8s)
    s = cin.shape[0]
    tm = _tile(s)
    nt = s // tm
    rc = 64
    rn = min(256, tm)
    hal = CONV_HALO
    nchunk = CONV_WIDTH // LANES

    def body(dc3_ref, dc3n_ref, c1_ref, c1n_ref, cin_ref, cw_ref, gain_ref, bias_ref, *rest):
        dcin_ref, dcw_ref, dcb_ref, dgain_ref, dbias_ref, dbcin_ref = rest[ns:ns + 6]
        dc1_ext, dcw8 = rest[2 * ns + 6:2 * ns + 8]
        i = pl.program_id(0)
        first, last = i == 0, i == nt - 1
        own = rest[2 * ns + 8:]
        _run_scatters([_ReduceScatter(rest[a], rest[ns + 6 + a], *own[N_SCATTER_SCRATCH * a:N_SCATTER_SCRATCH * (a + 1)])
                       for a in range(ns)], i, nt)

        @pl.when(first)
        def _():
            for ref in (dcw8, dcb_ref, dgain_ref, dbias_ref, dbcin_ref):
                ref[...] = jnp.zeros(ref.shape, F32)

        lo = _lo_mask((1, LANES))

        def norm_bwd(dc3, c1v, cols):
            nrm, rstd = _group_stats(c1v, lo)
            c2 = nrm * gain_ref[:, cols] + bias_ref[:, cols]
            sg = _sigmoid(c2)
            dc2 = dc3 * (sg * (1.0 + c2 * (1.0 - sg)))
            dn = dc2 * gain_ref[:, cols]
            inv = 1.0 / HEAD_DIM
            dc1 = rstd * (dn - _half_sums(dn, lo) * inv - nrm * (_half_sums(dn * nrm, lo) * inv))
            return dc1, dc2, nrm

        def row_sum(v):
            return jnp.sum(v, axis=0, keepdims=True)

        for cc in range(nchunk):
            cols = slice(cc * LANES, (cc + 1) * LANES)
            gcols = slice(CONV_WIDTH + cc * LANES, CONV_WIDTH + (cc + 1) * LANES)
            d1e = dc1_ext.at[cc]
            dc1n, _, _ = norm_bwd(dc3n_ref[:, cols].astype(F32), c1n_ref[cc], cols)
            d1e[tm:tm + hal, :] = jnp.where(last, 0.0, dc1n)

            for r in range(tm // rn):
                rows = slice(r * rn, (r + 1) * rn)
                dc1, dc2, nrm = norm_bwd(dc3_ref[rows, cols].astype(F32), c1_ref[cc, rows, :], cols)
                d1e[rows, :] = dc1
                dgain_ref[:, cols] += row_sum(dc2 * nrm)
                dbias_ref[:, cols] += row_sum(dc2)
                dcb_ref[:, cols] += row_sum(dc1)
            zero = jnp.zeros((1, LANES), F32)

            def taps(r, sums):
                rows = pl.ds(pl.multiple_of(r * rc, rc), rc)
                a = cin_ref[rows, cols]
                sg = _sigmoid(cin_ref[rows, gcols])
                c0 = (a * sg).reshape(rc // SUBLANES, SUBLANES, LANES)
                dc0 = jnp.zeros((rc // SUBLANES, SUBLANES, LANES), F32)
                for k in range(CONV_KERNEL):
                    krows = slice(k * SUBLANES, (k + 1) * SUBLANES)
                    shifted = _rows(d1e, r * rc + CONV_KERNEL - 1 - k, rc)
                    dc0 = dc0 + cw_ref[krows, cols][None] * shifted
                    dcw8[krows, cols] += jnp.sum(shifted * c0, axis=0)
                dc0 = dc0.reshape(rc, LANES)
                da = dc0 * sg
                dgate = dc0 * a * sg * (1.0 - sg)
                dcin_ref[rows, cols] = da.astype(BF16)
                dcin_ref[rows, gcols] = dgate.astype(BF16)
                return sums[0] + row_sum(da), sums[1] + row_sum(dgate)

            sums = lax.fori_loop(0, tm // rc, taps, (zero, zero))
            dbcin_ref[:, cols] += sums[0]
            dbcin_ref[:, gcols] += sums[1]

        @pl.when(last)
        def _():
            for k in range(CONV_KERNEL):
                dcw_ref[k:k + 1, :] = jnp.sum(dcw8[k * SUBLANES:(k + 1) * SUBLANES, :], axis=0, keepdims=True)

    nh = tm // hal
    acc = lambda shape: pl.BlockSpec(shape, lambda i: (0,) * len(shape))
    return pl.pallas_call(
        body, grid=(nt,),
        in_specs=[pl.BlockSpec((tm, CONV_WIDTH), lambda i: (i, 1)),
                  pl.BlockSpec((hal, CONV_WIDTH), lambda i: (jnp.minimum((i + 1) * nh, s // hal - 1), 1)),
                  pl.BlockSpec((nchunk, tm, LANES), lambda i: (0, i, 0)),
                  pl.BlockSpec((nchunk, hal, LANES), lambda i: (0, jnp.minimum((i + 1) * nh, s // hal - 1), 0)),
                  pl.BlockSpec((tm, CIN_COLS), lambda i: (i, 0)),
                  _resident((CONV_KERNEL * SUBLANES, CONV_WIDTH)), _resident((1, CONV_WIDTH)), _resident((1, CONV_WIDTH))] + s_in,
        out_specs=[pl.BlockSpec((tm, CIN_COLS), lambda i: (i, 0)), acc((CONV_KERNEL, CONV_WIDTH)), acc((1, CONV_WIDTH)),
                   acc((1, CONV_WIDTH)), acc((1, CONV_WIDTH)), acc((1, CIN_COLS))] + s_out,
        out_shape=[jax.ShapeDtypeStruct((s, CIN_COLS), BF16), jax.ShapeDtypeStruct((CONV_KERNEL, CONV_WIDTH), F32),
                   jax.ShapeDtypeStruct((1, CONV_WIDTH), F32), jax.ShapeDtypeStruct((1, CONV_WIDTH), F32),
                   jax.ShapeDtypeStruct((1, CONV_WIDTH), F32), jax.ShapeDtypeStruct((1, CIN_COLS), F32)] + s_shape,
        scratch_shapes=[pltpu.VMEM((nchunk, tm + hal, LANES), F32),
                        pltpu.VMEM((CONV_KERNEL * SUBLANES, CONV_WIDTH), F32)] + s_scratch,
        compiler_params=_cparams("arbitrary"), name="conv_bwd")(dmixed, dmixed, c1, c1, cin, cw8, gain, bias, *g8s)


def _attn_bwd(qkv, dmixed, gq2, gk2, sinks, g8s):
    ns = len(g8s)
    s_in, s_out, s_shape, s_scratch = _scatter_specs(g8s)
    s = qkv.shape[0]
    tq = _tile(s)
    nb = tq // ATT_BLOCK
    nt = s // tq

    def body(q_ref, kv_ref, kvp_ref, do_ref, gq_ref, gk_ref, sink_ref, *rest):
        dqkv_ref, dgq_ref, dgk_ref, dsink_ref, dbqkv_ref = rest[ns:ns + 5]
        dk_acc, dv_acc, carry_k, carry_v = rest[2 * ns + 5:2 * ns + 9]
        i = pl.program_id(0)
        t = nt - 1 - i
        own = rest[2 * ns + 9:]
        _run_scatters([_ReduceScatter(rest[a], rest[ns + 5 + a], *own[N_SCATTER_SCRATCH * a:N_SCATTER_SCRATCH * (a + 1)])
                       for a in range(ns)], i, nt)

        @pl.when(i == 0)
        def _():
            for ref in (carry_k, carry_v, dgq_ref, dgk_ref, dsink_ref, dbqkv_ref):
                ref[...] = jnp.zeros(ref.shape, F32)

        lo = _lo_mask((1, LANES))
        lane_id = lax.broadcasted_iota(jnp.int32, (1, LANES), 1)
        kv_all = jnp.concatenate([kvp_ref[...], kv_ref[...]], axis=0)
        k_lo, k_hi, v_lo, v_hi, kn_pre, rk = _kv_variants(kv_all, gk_ref[...], lo)
        for acc_ref, carry in ((dk_acc, carry_k), (dv_acc, carry_v)):
            acc_ref[:, 0:tq, :] = jnp.zeros((N_KV_HEADS, tq, LANES), F32)
            acc_ref[:, tq:tq + ATT_BLOCK, :] = carry[...]
        dsink = jnp.zeros((1, LANES), F32)
        dgq = jnp.zeros((1, LANES), F32)
        gq = gq_ref[...]
        for b in range(nb):
            rel, valid = _att_consts(t == 0, b)
            rows = slice(b * ATT_BLOCK, (b + 1) * ATT_BLOCK)
            keys = slice(b * ATT_BLOCK, (b + 2) * ATT_BLOCK)
            for kvh in range(N_KV_HEADS):
                pairs = (2 * kvh, 2 * kvh + 1)
                q_raw = jnp.concatenate([q_ref[rows, p * LANES:(p + 1) * LANES] for p in pairs], axis=0)
                qn_pre, rq = _head_norm(q_raw, lo)
                q2 = (qn_pre * gq).astype(BF16)
                do2 = jnp.concatenate([do_ref[rows, p * LANES:(p + 1) * LANES] for p in pairs], axis=0).astype(BF16)
                dq2 = jnp.zeros((2 * ATT_BLOCK, LANES), F32)
                for odd, (k_op, v_op) in enumerate(((k_lo[kvh][keys], v_lo[kvh][keys]), (k_hi[kvh][keys], v_hi[kvh][keys]))):
                    ha, hb = 2 * pairs[0] + odd, 2 * pairs[1] + odd
                    p, p_sink = _probs(q2, k_op, rel, valid, _row_const(SLOPES[ha], SLOPES[hb]),
                                       _row_const(sink_ref[ha], sink_ref[hb]))
                    dp = _dot_nt(do2, v_op)
                    delta = jnp.sum(p * dp, axis=-1, keepdims=True)
                    ds = (p * (dp - delta) * (1.0 / math.sqrt(HEAD_DIM))).astype(BF16)
                    dsk = p_sink * delta
                    dsink = dsink - jnp.where(lane_id == ha, jnp.sum(dsk[0:ATT_BLOCK]), 0.0) \
                        - jnp.where(lane_id == hb, jnp.sum(dsk[ATT_BLOCK:]), 0.0)
                    dq2 = dq2 + _dot(ds, k_op)
                    half = lo if odd == 0 else jnp.logical_not(lo)
                    dk_acc[kvh, keys, :] += jnp.where(half, _dot_tn(ds, q2), 0.0)
                    dv_acc[kvh, keys, :] += jnp.where(half, _dot_tn(p.astype(BF16), do2), 0.0)
                dgq = dgq + jnp.sum(dq2 * qn_pre, axis=0, keepdims=True)
                dq_raw = _head_norm_bwd(dq2 * gq, qn_pre, rq, lo)
                for n, p_ in enumerate(pairs):
                    blk = dq_raw[n * ATT_BLOCK:(n + 1) * ATT_BLOCK]
                    dqkv_ref[rows, p_ * LANES:(p_ + 1) * LANES] = blk.astype(BF16)
                    dbqkv_ref[:, p_ * LANES:(p_ + 1) * LANES] += jnp.sum(blk, axis=0, keepdims=True)
        carry_k[...] = dk_acc[:, 0:ATT_BLOCK, :]
        carry_v[...] = dv_acc[:, 0:ATT_BLOCK, :]

        def fold(acc_ref):
            both = []
            for kvh in range(N_KV_HEADS):
                a = acc_ref[kvh, ATT_BLOCK:ATT_BLOCK + tq, :]
                both.append(a + pltpu.roll(a, HEAD_DIM, 1))
            return jnp.where(lo, both[0], both[1])

        dkn = fold(dk_acc)
        dv = fold(dv_acc)
        kn_c, rk_c = kn_pre[ATT_BLOCK:], rk[ATT_BLOCK:]
        dgk_ref[...] += jnp.sum(dkn * kn_c, axis=0, keepdims=True)
        dk_raw = _head_norm_bwd(dkn * gk_ref[...], kn_c, rk_c, lo)
        dqkv_ref[:, Q_COLS:Q_COLS + KV_COLS] = dk_raw.astype(BF16)
        dqkv_ref[:, Q_COLS + KV_COLS:] = dv.astype(BF16)
        dbqkv_ref[:, Q_COLS:Q_COLS + KV_COLS] += jnp.sum(dk_raw, axis=0, keepdims=True)
        dbqkv_ref[:, Q_COLS + KV_COLS:] += jnp.sum(dv, axis=0, keepdims=True)
        dgq_ref[...] += dgq
        dsink_ref[...] += dsink

        @pl.when(i == nt - 1)
        def _():
            for ref in (dgq_ref, dgk_ref):
                v = ref[...]
                ref[...] = v + pltpu.roll(v, HEAD_DIM, 1)

    acc = lambda shape: pl.BlockSpec(shape, lambda i: (0,) * len(shape))
    return pl.pallas_call(
        body, grid=(nt,),
        in_specs=[pl.BlockSpec((tq, Q_COLS), lambda i: (nt - 1 - i, 0)),
                  pl.BlockSpec((tq, 2 * KV_COLS), lambda i: (nt - 1 - i, 2)),
                  pl.BlockSpec((ATT_BLOCK, 2 * KV_COLS), lambda i: (jnp.maximum((nt - 1 - i) * nb - 1, 0), 2)),
                  pl.BlockSpec((tq, Q_COLS), lambda i: (nt - 1 - i, 0)),
                  _resident((1, LANES)), _resident((1, LANES)), pl.BlockSpec(memory_space=pltpu.SMEM)] + s_in,
        out_specs=[pl.BlockSpec((tq, QKV_COLS), lambda i: (nt - 1 - i, 0)), acc((1, LANES)), acc((1, LANES)),
                   acc((1, LANES)), acc((1, QKV_COLS))] + s_out,
        out_shape=[jax.ShapeDtypeStruct((s, QKV_COLS), BF16), jax.ShapeDtypeStruct((1, LANES), F32),
                   jax.ShapeDtypeStruct((1, LANES), F32), jax.ShapeDtypeStruct((1, LANES), F32),
                   jax.ShapeDtypeStruct((1, QKV_COLS), F32)] + s_shape,
        scratch_shapes=[pltpu.VMEM((N_KV_HEADS, tq + ATT_BLOCK, LANES), F32), pltpu.VMEM((N_KV_HEADS, tq + ATT_BLOCK, LANES), F32),
                        pltpu.VMEM((N_KV_HEADS, ATT_BLOCK, LANES), F32), pltpu.VMEM((N_KV_HEADS, ATT_BLOCK, LANES), F32)] + s_scratch,
        compiler_params=_cparams("arbitrary"), name="attn_bwd")(qkv, qkv, qkv, dmixed, gq2, gk2, sinks, *g8s)


def _in_bwd(dqkv, dcin, w_in_t, x, dx2, g_mix):
    s = x.shape[0]
    tm = _wide_tile(s)

    def body(dq_ref, dc_ref, w_ref, x_ref, dx2_ref, g_ref, gx_ref, dg_ref):
        @pl.when(pl.program_id(0) == 0)
        def _():
            dg_ref[...] = jnp.zeros(dg_ref.shape, F32)

        dh = _dot(dq_ref[...], w_ref[0:QKV_COLS, :]) + _dot(dc_ref[...], w_ref[QKV_COLS:, :])
        xv = x_ref[...]
        r = lax.rsqrt(jnp.mean(xv * xv, axis=-1, keepdims=True) + EPS)
        n = xv * r
        dg_ref[...] += jnp.sum(dh * n, axis=0, keepdims=True)
        dn = dh * g_ref[...]
        gx_ref[...] = dx2_ref[...] + r * (dn - n * jnp.mean(dn * n, axis=-1, keepdims=True))

    return pl.pallas_call(
        body, grid=(s // tm,),
        in_specs=[pl.BlockSpec((tm, QKV_COLS), lambda i: (i, 0)), pl.BlockSpec((tm, CIN_COLS), lambda i: (i, 0)),
                  _resident((QKV_COLS + CIN_COLS, D_MODEL)),
                  pl.BlockSpec((tm, D_MODEL), lambda i: (i, 0)), pl.BlockSpec((tm, D_MODEL), lambda i: (i, 0)),
                  _resident((1, D_MODEL))],
        out_specs=[pl.BlockSpec((tm, D_MODEL), lambda i: (i, 0)), pl.BlockSpec((1, D_MODEL), lambda i: (0, 0))],
        out_shape=[jax.ShapeDtypeStruct((s, D_MODEL), F32), jax.ShapeDtypeStruct((1, D_MODEL), F32)],
        compiler_params=_cparams("arbitrary"), name="in_bwd")(dqkv, dcin, w_in_t, x, dx2, g_mix)


def _tn_matmul(a, b, name, tokens):
    ga, s, m = a.shape
    gb, _, n = b.shape
    g = max(ga, gb)
    tk = min(tokens, s)

    def body(a_ref, b_ref, o_ref):
        @pl.when(pl.program_id(1) == 0)
        def _():
            o_ref[...] = jnp.zeros(o_ref.shape, F32)

        o_ref[0] += _dot_tn(a_ref[0].astype(BF16), b_ref[0].astype(BF16))

    return pl.pallas_call(
        body, grid=(g, s // tk),
        in_specs=[pl.BlockSpec((1, tk, m), (lambda gi, k: (gi, k, 0)) if ga > 1 else (lambda gi, k: (0, k, 0))),
                  pl.BlockSpec((1, tk, n), (lambda gi, k: (gi, k, 0)) if gb > 1 else (lambda gi, k: (0, k, 0)))],
        out_specs=pl.BlockSpec((1, m, n), lambda gi, k: (gi, 0, 0)),
        out_shape=jax.ShapeDtypeStruct((g, m, n), F32),
        compiler_params=_cparams("parallel", "arbitrary"), name=name)(a, b)


def _tn_matmul_pair(a0, a1, b, name):
    s, m0 = a0.shape
    m1, n = a1.shape[1], b.shape[1]
    tk = min(TN_TOKENS, s)

    def body(a0_ref, a1_ref, b_ref, o_ref):
        @pl.when(pl.program_id(0) == 0)
        def _():
            o_ref[...] = jnp.zeros(o_ref.shape, F32)

        bv = b_ref[...].astype(BF16)
        o_ref[0:m0, :] += _dot_tn(a0_ref[...].astype(BF16), bv)
        o_ref[m0:, :] += _dot_tn(a1_ref[...].astype(BF16), bv)

    return pl.pallas_call(
        body, grid=(s // tk,),
        in_specs=[pl.BlockSpec((tk, m0), lambda k: (k, 0)), pl.BlockSpec((tk, m1), lambda k: (k, 0)),
                  pl.BlockSpec((tk, n), lambda k: (k, 0))],
        out_specs=pl.BlockSpec((m0 + m1, n), lambda k: (0, 0)),
        out_shape=jax.ShapeDtypeStruct((m0 + m1, n), F32),
        compiler_params=_cparams("arbitrary"), name=name)(a0, a1, b)


def _allgather(shards, dtypes):
    n = len(shards)
    n_copies = 1 + 2 * len(OTHER_CHIPS)

    def body(*refs):
        ins, outs = refs[:n], refs[n:2 * n]
        send_sems, recv_sems = refs[2 * n:]
        x, y, c = _position()
        me, sibling = (x, y, c), (x, y, 1 - c)
        chips = [(_flip(x, fx), _flip(y, fy)) for fx, fy in OTHER_CHIPS]
        for a in range(n):
            outs[a][_dev_index(*me)] = ins[a][...].astype(dtypes[a])

        def copy(a, k, block, to):
            rows = outs[a].at[_dev_index(*block)]
            return pltpu.make_async_remote_copy(src_ref=rows, dst_ref=rows, send_sem=send_sems.at[a, k],
                                                recv_sem=recv_sems.at[a, k], device_id=to, device_id_type=MESH)

        started = []
        for a in range(n):
            for j, chip in enumerate(chips):
                started.append(copy(a, 1 + j, me, (*chip, c)))
            started.append(copy(a, 0, me, sibling))
        for cp in started:
            cp.start()
        for a in range(n):
            for j, chip in enumerate(chips):
                copy(a, 1 + j, (*chip, c), me).wait_recv()
                fwd = copy(a, 1 + len(chips) + j, (*chip, c), sibling)
                fwd.start()
                started.append(fwd)
        for a in range(n):
            copy(a, 0, sibling, me).wait_recv()
            for j, chip in enumerate(chips):
                copy(a, 1 + len(chips) + j, (*chip, 1 - c), me).wait_recv()
        for cp in started:
            cp.wait_send()

    vmem = pl.BlockSpec(memory_space=pltpu.VMEM)
    return pl.pallas_call(
        body, in_specs=[vmem] * n, out_specs=[vmem] * n,
        out_shape=[jax.ShapeDtypeStruct((N_DEV,) + w.shape, dt) for w, dt in zip(shards, dtypes)],
        scratch_shapes=[pltpu.SemaphoreType.DMA((n, n_copies)), pltpu.SemaphoreType.DMA((n, n_copies))],
        compiler_params=pltpu.CompilerParams(vmem_limit_bytes=VMEM_LIMIT), name="allgather_weights")(*shards)


def _final_exchange(g8, v):
    rows = v.shape[0]
    _, _, s_shape, s_scratch = _scatter_specs([g8])

    def body(g_ref, v_ref, gout_ref, vout_ref, gath, send_sems, recv_sems, *rs_scratch):
        scatter = _ReduceScatter(g_ref, gout_ref, *rs_scratch)
        x, y, c = _position()
        me = _dev_index(x, y, c)
        peers = [(_flip(x, k >> 2 & 1), _flip(y, k >> 1 & 1), _flip(c, k & 1)) for k in range(1, N_DEV)]

        def copy(k, block):
            return pltpu.make_async_remote_copy(src_ref=gath.at[block], dst_ref=gath.at[block], send_sem=send_sems.at[k],
                                                recv_sem=recv_sems.at[k], device_id=peers[k], device_id_type=MESH)

        scatter.start()
        gath[me] = v_ref[...]
        for k in range(N_DEV - 1):
            copy(k, me).start()
        scatter.middle()
        for k in range(N_DEV - 1):
            copy(k, _dev_index(*peers[k])).wait_recv()
        for k in range(N_DEV - 1):
            copy(k, me).wait_send()
        total = gath[0]
        for d in range(1, N_DEV):
            total = total + gath[d]
        vout_ref[...] = total
        scatter.finish()

    vmem = pl.BlockSpec(memory_space=pltpu.VMEM)
    return pl.pallas_call(
        body, in_specs=[pl.BlockSpec(memory_space=pl.ANY), vmem], out_specs=[vmem, vmem],
        out_shape=s_shape + [jax.ShapeDtypeStruct((rows, LANES), F32)],
        scratch_shapes=[pltpu.VMEM((N_DEV, rows, LANES), F32), pltpu.SemaphoreType.DMA((N_DEV - 1,)),
                        pltpu.SemaphoreType.DMA((N_DEV - 1,))] + s_scratch,
        compiler_params=pltpu.CompilerParams(vmem_limit_bytes=VMEM_LIMIT), name="final_exchange")(g8, v)


def _row_tile(r):
    for n in (8, 4, 2):
        if r % (n * SUBLANES) == 0:
            return r // n
    return r


def _adam_math(wv, gv, mv, vv):
    mn = ADAM_B1 * mv + (1.0 - ADAM_B1) * gv
    vn = ADAM_B2 * vv + (1.0 - ADAM_B2) * (gv * gv)
    m_hat = mn / (1.0 - ADAM_B1 ** ADAM_STEP)
    v_hat = vn / (1.0 - ADAM_B2 ** ADAM_STEP)
    return -ADAM_LR * (m_hat / (jnp.sqrt(v_hat) + ADAM_EPS) + ADAM_WD * wv), mn, vn


def _adamw(w, g, m, v, name):
    r, c_ = w.shape
    tr = _row_tile(r)

    def body(w_ref, g_ref, m_ref, v_ref, d_ref, mo_ref, vo_ref):
        d_ref[...], mo_ref[...], vo_ref[...] = _adam_math(w_ref[...], g_ref[...], m_ref[...], v_ref[...])

    spec = pl.BlockSpec((tr, c_), lambda i: (i, 0))
    return pl.pallas_call(
        body, grid=(r // tr,), in_specs=[spec] * 4, out_specs=[spec] * 3,
        out_shape=[jax.ShapeDtypeStruct((r, c_), F32)] * 3,
        compiler_params=_cparams("parallel"), name=name)(w, g, m, v)


FW_ROWS = 24
CW_ROWS = 32
R_FW = 0
R_FB = R_FW + N_DEV * FW_ROWS
R_CW = R_FB + 48
R_BQKV = R_CW + (CONV_WIDTH // LANES) * CW_ROWS
R_BCIN = R_BQKV + 8
R_GMIX = R_BCIN + 8
R_BOUT = R_GMIX + 8
R_GFFN = R_BOUT + 8
R_CB = R_GFFN + 8
R_CGAIN = R_CB + 8
R_CBIAS = R_CGAIN + 8
R_QKS = R_CBIAS + 8
SMALL_ROWS = R_QKS + 8


def _pack_small(raw):
    def rows(a, n):
        a = a.reshape(-1, LANES)
        return jnp.pad(a, ((0, n - a.shape[0]), (0, 0)))

    fw = jnp.pad(raw["dfw"].reshape(N_DEV, -1, LANES), ((0, 0), (0, FW_ROWS - 3 * FF_LANE_CHUNKS), (0, 0)))
    cw = jnp.pad(raw["dcw"].reshape(CONV_KERNEL, -1, LANES).transpose(1, 0, 2), ((0, 0), (0, CW_ROWS - CONV_KERNEL), (0, 0)))
    qks = jnp.concatenate([raw["dgq"], raw["dgk"], raw["dsink"], jnp.pad(raw["loss"], ((0, 0), (0, LANES - 1)))], axis=0)
    return jnp.concatenate([
        fw.reshape(-1, LANES), rows(raw["dfb"][:, 0, :FF_CHUNK], 48), cw.reshape(-1, LANES), rows(raw["dbqkv"], 8),
        rows(raw["dbcin"], 8), rows(raw["dg_mix"], 8), rows(raw["db_out"], 8), rows(raw["dg_ffn"], 8), rows(raw["dcb"], 8),
        rows(raw["dcgain"], 8), rows(raw["dcbias"], 8), rows(qks, 8)], axis=0)


def _adamw_small(gpack, w, m, v):
    n = len(SMALL)
    ix = {name: i for i, name in enumerate(SMALL)}

    def body(g_ref, *refs):
        w_refs, m_refs, v_refs, outs = refs[:n], refs[n:2 * n], refs[2 * n:3 * n], refs[3 * n:]
        d = _dev_index(*_position())

        def step(name, idx, gv):
            i = ix[name]
            delta, mn, vn = _adam_math(w_refs[i][idx], gv, m_refs[i][idx], v_refs[i][idx])
            for ref, val in zip(outs[4 * i:4 * i + 4], (gv, delta, mn, vn)):
                ref[idx] = val

        def whole(name, row, nrows):
            step(name, (slice(None), slice(None)), g_ref[row:row + nrows, :])

        whole("mix_norm_gain", R_GMIX, 8)
        whole("b_out", R_BOUT, 8)
        whole("ffn_norm_gain", R_GFFN, 8)
        whole("conv_dw_b", R_CB, 4)
        whole("conv_norm_gain", R_CGAIN, 4)
        whole("conv_norm_bias", R_CBIAS, 4)
        whole("ffn_dw_b", R_FB, 2 * D_FF // LANES)
        nq = QKV_COLS // LANES
        step("b_in", (slice(0, nq), slice(None)), g_ref[R_BQKV:R_BQKV + nq, :])
        step("b_in", (slice(nq, nq + CIN_COLS // LANES), slice(None)), g_ref[R_BCIN:R_BCIN + CIN_COLS // LANES, :])
        step("q_norm_gain", (slice(None), slice(None)), g_ref[R_QKS:R_QKS + 1, 0:HEAD_DIM])
        step("k_norm_gain", (slice(None), slice(None)), g_ref[R_QKS + 1:R_QKS + 2, 0:HEAD_DIM])
        step("attn_sinks", (slice(None), slice(None)), g_ref[R_QKS + 2:R_QKS + 3, 0:N_Q_HEADS])
        blk = g_ref[pl.ds(pl.multiple_of(R_CW + CW_ROWS * lax.shift_right_logical(d, 1), SUBLANES), CW_ROWS), :]
        blk = jnp.where((d & 1) == 1, pltpu.roll(blk, HEAD_DIM, 1), blk)
        step("conv_dw_w", (slice(None), slice(None)), blk[0:CONV_KERNEL, 0:CONV_WIDTH // N_DEV])
        blk = g_ref[pl.ds(pl.multiple_of(R_FW + FW_ROWS * d, SUBLANES), FW_ROWS), :]
        for k in range(3):
            for j in range(FF_LANE_CHUNKS):
                wd = min(LANES, FF_CHUNK - j * LANES)
                row = k * FF_LANE_CHUNKS + j
                step("ffn_dw_w", (slice(k, k + 1), slice(j * LANES, j * LANES + wd)), blk[row:row + 1, 0:wd])

    vmem = pl.BlockSpec(memory_space=pltpu.VMEM)
    args = [gpack] + [d[name] for d in (w, m, v) for name in SMALL]
    outs = pl.pallas_call(
        body, in_specs=[vmem] * len(args), out_specs=[vmem] * (4 * n),
        out_shape=[jax.ShapeDtypeStruct(w[name].shape, F32) for name in SMALL for _ in range(4)],
        compiler_params=pltpu.CompilerParams(vmem_limit_bytes=VMEM_LIMIT), name="adamw_small")(*args)
    return {name: outs[4 * i:4 * i + 4] for i, name in enumerate(SMALL)}


def _token_mixing(x, p, attn_shards, conv_shards):
    qkv, cin, h1 = _mix_proj(x, p["g_mix"], p["w_in_t"], p["b_qkv"], p["b_cin"])
    attn, *from_attn = _attn_fwd(qkv, p["gq2"], p["gk2"], p["sinks"], attn_shards)
    c3, c1, *from_conv = _conv_fwd(cin, p["cw8"], p["cb"], p["cgain"], p["cbias"], conv_shards)
    return (qkv, cin, h1, attn, c3, c1), from_attn, from_conv


def _rest_of_step(x, target, p, saved, scatter):
    s = x.shape[0]
    qkv, cin, h1, attn, c3, c1 = saved
    cw8, w_out, w_up, w_down = p["cw8"], p["w_out"], p["w_up"], p["w_down"]
    x2, h2 = _out_proj(x, attn, c3, w_out, w_out, p["b_out"], p["g_ffn"])
    fw, fb = p["fw"], p["fb"]
    up0, gu, act, dy, dyb, loss = _ffn_fwd(h2, x2, target, w_up, fw, fb, w_down)
    dup0, dh2, dfw, dfb = _ffn_bwd(dyb, up0, gu, w_up, fw, w_down)
    dx2, dmixed, dg_ffn, db_out = _ffn_norm_bwd(dh2, dy, x2, p["g_ffn"], w_out)
    dw_up = _tn_matmul(dup0.reshape(N_DEV, s, FF_CHUNK), h2[None], "dw_up", 2 * TN_TOKENS)
    dw_down = _tn_matmul(act, dyb[None], "dw_down", 2 * TN_TOKENS).reshape(N_DEV, -1, D_MODEL)
    dw_out = _tn_matmul_pair(attn, c3, dx2, "dw_out").reshape(N_DEV, -1, D_MODEL)
    dcin, dcw, dcb, dcgain, dcbias, dbcin, *g_up = _conv_bwd(dmixed, c1, cin, cw8, p["cgain"], p["cbias"], [dw_up] if scatter else [])
    dqkv, dgq, dgk, dsink, dbqkv, *g_down_out = _attn_bwd(qkv, dmixed, p["gq2"], p["gk2"], p["sinks"],
                                                          [dw_down, dw_out] if scatter else [])
    dw_in = _tn_matmul_pair(dqkv, dcin, h1, "dw_in").reshape(N_DEV, -1, D_MODEL)
    grad_x, dg_mix = _in_bwd(dqkv, dcin, p["w_in_t"], x, dx2, p["g_mix"])
    if scatter:
        big = {"w_up": g_up[0], "w_down": g_down_out[0], "w_in": dw_in, "w_out": g_down_out[1]}
    else:
        big = {"w_up": dw_up, "w_down": dw_down, "w_in": dw_in, "w_out": dw_out}
    small = dict(dg_mix=dg_mix, dbqkv=dbqkv, dbcin=dbcin, dgq=dgq, dgk=dgk, dsink=dsink, dcw=dcw, dcb=dcb, dcgain=dcgain,
                 dcbias=dcbias, db_out=db_out, dg_ffn=dg_ffn, dfw=dfw, dfb=dfb, loss=loss)
    return loss, grad_x, big, small


BIG = ("w_in", "w_out", "w_up", "w_down")
SMALL = ("mix_norm_gain", "b_in", "q_norm_gain", "k_norm_gain", "attn_sinks", "conv_dw_w", "conv_dw_b",
         "conv_norm_gain", "conv_norm_bias", "b_out", "ffn_norm_gain", "ffn_dw_w", "ffn_dw_b")
ORDER = ("mix_norm_gain", "w_in", "b_in", "q_norm_gain", "k_norm_gain", "attn_sinks", "conv_dw_w", "conv_dw_b",
         "conv_norm_gain", "conv_norm_bias", "w_out", "b_out", "ffn_norm_gain", "w_up", "ffn_dw_w", "ffn_dw_b", "w_down")


def kernel(x, mix_norm_gain, w_in, b_in, q_norm_gain, k_norm_gain, attn_sinks, conv_dw_w, conv_dw_b, conv_norm_gain, conv_norm_bias, w_out, b_out, ffn_norm_gain, w_up, ffn_dw_w, ffn_dw_b, w_down, loss_target, m_mix_norm_gain, m_w_in, m_b_in, m_q_norm_gain, m_k_norm_gain, m_attn_sinks, m_conv_dw_w, m_conv_dw_b, m_conv_norm_gain, m_conv_norm_bias, m_w_out, m_b_out, m_ffn_norm_gain, m_w_up, m_ffn_dw_w, m_ffn_dw_b, m_w_down, v_mix_norm_gain, v_w_in, v_b_in, v_q_norm_gain, v_k_norm_gain, v_attn_sinks, v_conv_dw_w, v_conv_dw_b, v_conv_norm_gain, v_conv_norm_bias, v_w_out, v_b_out, v_ffn_norm_gain, v_w_up, v_ffn_dw_w, v_ffn_dw_b, v_w_down):
    w = dict(mix_norm_gain=mix_norm_gain, w_in=w_in, b_in=b_in, q_norm_gain=q_norm_gain, k_norm_gain=k_norm_gain,
             attn_sinks=attn_sinks, conv_dw_w=conv_dw_w, conv_dw_b=conv_dw_b, conv_norm_gain=conv_norm_gain,
             conv_norm_bias=conv_norm_bias, w_out=w_out, b_out=b_out, ffn_norm_gain=ffn_norm_gain, w_up=w_up,
             ffn_dw_w=ffn_dw_w, ffn_dw_b=ffn_dw_b, w_down=w_down)
    m = dict(mix_norm_gain=m_mix_norm_gain, w_in=m_w_in, b_in=m_b_in, q_norm_gain=m_q_norm_gain, k_norm_gain=m_k_norm_gain,
             attn_sinks=m_attn_sinks, conv_dw_w=m_conv_dw_w, conv_dw_b=m_conv_dw_b, conv_norm_gain=m_conv_norm_gain,
             conv_norm_bias=m_conv_norm_bias, w_out=m_w_out, b_out=m_b_out, ffn_norm_gain=m_ffn_norm_gain, w_up=m_w_up,
             ffn_dw_w=m_ffn_dw_w, ffn_dw_b=m_ffn_dw_b, w_down=m_w_down)
    v = dict(mix_norm_gain=v_mix_norm_gain, w_in=v_w_in, b_in=v_b_in, q_norm_gain=v_q_norm_gain, k_norm_gain=v_k_norm_gain,
             attn_sinks=v_attn_sinks, conv_dw_w=v_conv_dw_w, conv_dw_b=v_conv_dw_b, conv_norm_gain=v_conv_norm_gain,
             conv_norm_bias=v_conv_norm_bias, w_out=v_w_out, b_out=v_b_out, ffn_norm_gain=v_ffn_norm_gain, w_up=v_w_up,
             ffn_dw_w=v_ffn_dw_w, ffn_dw_b=v_ffn_dw_b, w_down=v_w_down)
    s = x.shape[1]

    wi8, cw8, fw8 = _allgather([w_in.T, conv_dw_w, ffn_dw_w], [BF16, F32, F32])
    lane_pad = ((0, 0), (0, 0), (0, FF_PADDED - FF_CHUNK))
    p = {
        "g_mix": mix_norm_gain.reshape(1, -1), "w_in_t": wi8.reshape(QKV_COLS + CIN_COLS, D_MODEL),
        "b_qkv": b_in[:QKV_COLS].reshape(1, -1), "b_cin": b_in[QKV_COLS:].reshape(1, -1),
        "gq2": jnp.tile(q_norm_gain, 2).reshape(1, -1), "gk2": jnp.tile(k_norm_gain, 2).reshape(1, -1), "sinks": attn_sinks,
        "cw8": jnp.repeat(cw8.transpose(1, 0, 2).reshape(CONV_KERNEL, CONV_WIDTH), SUBLANES, axis=0),
        "cb": conv_dw_b.reshape(1, -1), "cgain": conv_norm_gain.reshape(1, -1), "cbias": conv_norm_bias.reshape(1, -1),
        "b_out": b_out.reshape(1, -1), "g_ffn": ffn_norm_gain.reshape(1, -1),
        "fw": jnp.pad(fw8, lane_pad), "fb": jnp.pad(ffn_dw_b.reshape(N_DEV, 1, FF_CHUNK), lane_pad),
    }

    saved, (wu8,), (wo8, wd8) = _token_mixing(x[0], p, [w_up.T], [w_out, w_down])
    p.update(w_out=wo8.reshape(D_MODEL, D_MODEL), w_up=wu8, w_down=wd8.reshape(N_FF_PAIRS, FF_CHUNK, D_MODEL))
    loss, grad_x, big, small = _rest_of_step(x[0], loss_target[0], p, saved, True)

    g = dict(big)
    g["w_in"], gpack = _final_exchange(big["w_in"], _pack_small(small))

    delta, new_m, new_v = {}, {}, {}
    for n in BIG:
        if n in ("w_in", "w_up"):
            outs = _adamw(w[n].T, g[n], m[n].T, v[n].T, "adamw_" + n)
            g[n], delta[n], new_m[n], new_v[n] = g[n].T, *[o.T for o in outs]
        else:
            delta[n], new_m[n], new_v[n] = _adamw(w[n], g[n], m[n], v[n], "adamw_" + n)

    def view(a):
        return a if a.ndim == 2 else (a.reshape(-1, LANES) if a.size % LANES == 0 else a.reshape(1, -1))

    small_out = _adamw_small(gpack, *[{n: view(d[n]) for n in SMALL} for d in (w, m, v)])
    for n in SMALL:
        g[n], delta[n], new_m[n], new_v[n] = [a.reshape(w[n].shape) for a in small_out[n]]

    total = gpack[R_QKS + 3, 0]
    return (total, grad_x.reshape(1, s, D_MODEL), *[g[n] for n in ORDER], *[delta[n] for n in ORDER],
            *[new_m[n] for n in ORDER], *[new_v[n] for n in ORDER])
```

```python
import functools
import math

import jax
import jax.numpy as jnp
from jax import lax
from jax.experimental import pallas as pl
from jax.experimental.pallas import tpu as pltpu

F32 = jnp.float32
BF16 = jnp.bfloat16

D_MODEL = 1024
HEAD_DIM = 64
N_Q_HEADS = 8
N_KV_HEADS = 2
Q_COLS = 512
KV_COLS = 128
QKV_COLS = Q_COLS + 2 * KV_COLS
CONV_WIDTH = 512
CIN_COLS = 2 * CONV_WIDTH
CONV_KERNEL = 31
CONV_HALO = 32
D_FF = 2816
N_DEV = 8
FF_CHUNK = 2 * D_FF // N_DEV
N_FF_PAIRS = N_DEV // 2
ATT_BLOCK = 128
EPS = 1e-6
NEG_INF = -1e30
SLOPES = [float(2.0 ** (-8.0 * (h + 1.0) / N_Q_HEADS)) for h in range(N_Q_HEADS)]

ADAM_LR = 0.001
ADAM_B1 = 0.9
ADAM_B2 = 0.999
ADAM_EPS = 1e-08
ADAM_WD = 0.01
ADAM_STEP = 10

LANES = 128
SUBLANES = 8
VMEM_LIMIT = 56 * 1024 * 1024
MESH = pl.DeviceIdType.MESH


def _cparams(*sem, **kw):
    return pltpu.CompilerParams(dimension_semantics=sem or None, vmem_limit_bytes=VMEM_LIMIT, **kw)


def _resident(shape):
    nd = len(shape)
    return pl.BlockSpec(shape, lambda *_: (0,) * nd, pipeline_mode=pl.Buffered(1))


def _dot(a, b):
    return jnp.dot(a, b, preferred_element_type=F32)


def _dot_nt(a, b):
    return lax.dot_general(a, b, (((1,), (1,)), ((), ())), preferred_element_type=F32)


def _dot_tn(a, b):
    return lax.dot_general(a, b, (((0,), (0,)), ((), ())), preferred_element_type=F32)


def _sigmoid(x):
    return 1.0 / (1.0 + jnp.exp(-x))


def _lo_mask(shape):
    return lax.broadcasted_iota(jnp.int32, shape, len(shape) - 1) % LANES < HEAD_DIM


def _half_sums(t, lo):
    s_lo = jnp.sum(jnp.where(lo, t, 0.0), axis=-1, keepdims=True)
    s_hi = jnp.sum(jnp.where(lo, 0.0, t), axis=-1, keepdims=True)
    return jnp.where(lo, s_lo, s_hi)


def _head_norm(t, lo):
    r = lax.rsqrt(_half_sums(t * t, lo) * (1.0 / HEAD_DIM) + EPS)
    return t * r, r


def _head_norm_bwd(dn, n, r, lo):
    return r * (dn - n * (_half_sums(dn * n, lo) * (1.0 / HEAD_DIM)))


def _tile(s):
    return min(512, s)


def _wide_tile(s):
    return min(1024, s)


TN_TOKENS = 2048
FF_COLS = ((0, 256), (256, 512), (512, 704))


def _position():
    return lax.axis_index("x"), lax.axis_index("y"), lax.axis_index("c")


def _dev_index(px, py, pc):
    return 4 * px + 2 * py + pc


def _flip(v, bit):
    return 1 - v if bit else v


OTHER_CHIPS = ((1, 0), (0, 1), (1, 1))
N_GATHER_COPIES = 1 + 2 * len(OTHER_CHIPS)


class _Gather:
    def __init__(self, shard_ref, out_ref, cast_buf, send_sems, recv_sems, local_sem):
        self.shard, self.out, self.buf = shard_ref, out_ref, cast_buf
        self.send_sems, self.recv_sems, self.local_sem = send_sems, recv_sems, local_sem
        x, y, c = _position()
        self.c = c
        self.me, self.sibling = (x, y, c), (x, y, 1 - c)
        self.chips = [(_flip(x, fx), _flip(y, fy)) for fx, fy in OTHER_CHIPS]

    def _copy(self, k, block, to, from_buf=False):
        rows = self.out.at[_dev_index(*block)]
        return pltpu.make_async_remote_copy(src_ref=self.buf if from_buf else rows, dst_ref=rows,
                                            send_sem=self.send_sems.at[k], recv_sem=self.recv_sems.at[k],
                                            device_id=to, device_id_type=MESH)

    def _local(self):
        return pltpu.make_async_copy(self.buf, self.out.at[_dev_index(*self.me)], self.local_sem)

    def start(self):
        self.buf[...] = self.shard[...].astype(self.buf.dtype)
        self._local().start()
        for j, chip in enumerate(self.chips):
            self._copy(1 + j, self.me, (*chip, self.c), from_buf=True).start()
        self._copy(0, self.me, self.sibling, from_buf=True).start()

    def forward(self):
        for j, chip in enumerate(self.chips):
            self._copy(1 + j, (*chip, self.c), self.me).wait_recv()
            self._copy(1 + len(self.chips) + j, (*chip, self.c), self.sibling).start()

    def finish(self):
        self._copy(0, self.sibling, self.me).wait_recv()
        for j, chip in enumerate(self.chips):
            self._copy(1 + len(self.chips) + j, (*chip, 1 - self.c), self.me).wait_recv()
        for k in range(N_GATHER_COPIES):
            self._copy(k, self.me, self.sibling).wait_send()
        self._local().wait()


def _gather_specs(shards):
    whole = [pl.BlockSpec(w.shape, lambda *_, nd=w.ndim: (0,) * nd, pipeline_mode=pl.Buffered(1)) for w in shards]
    outs = [pl.BlockSpec(memory_space=pl.ANY) for _ in shards]
    shapes = [jax.ShapeDtypeStruct((N_DEV,) + w.shape, BF16) for w in shards]
    scratch = []
    for w in shards:
        scratch += [pltpu.VMEM(w.shape, BF16), pltpu.SemaphoreType.DMA((N_GATHER_COPIES,)),
                    pltpu.SemaphoreType.DMA((N_GATHER_COPIES,)), pltpu.SemaphoreType.DMA(())]
    return whole, outs, shapes, scratch


def _run_gathers(gathers, step, n_steps):
    @pl.when(step == 0)
    def _():
        for g in gathers:
            g.start()

    @pl.when(step == 3 * n_steps // 4)
    def _():
        for g in gathers:
            g.forward()

    @pl.when(step == n_steps - 1)
    def _():
        for g in gathers:
            g.finish()


class _ReduceScatter:
    def __init__(self, g_ref, out_ref, stage, load_sems, send_a, recv_a, send_b, recv_b, sa_send, sa_recv, sb_send, sb_recv):
        self.g, self.out, self.stage, self.load_sems = g_ref, out_ref, stage, load_sems
        self.send_a, self.recv_a, self.send_b, self.recv_b = send_a, recv_a, send_b, recv_b
        self.sems = (sa_send, sa_recv, sb_send, sb_recv)
        x, y, c = _position()
        self.c, self.sibling = c, (x, y, 1 - c)
        self.chips = [(x, y)] + [(_flip(x, fx), _flip(y, fy)) for fx, fy in OTHER_CHIPS]

    def _copy_a(self, j):
        return pltpu.make_async_remote_copy(src_ref=self.send_a.at[j], dst_ref=self.recv_a.at[j], send_sem=self.sems[0].at[j],
                                            recv_sem=self.sems[1].at[j], device_id=self.sibling, device_id_type=MESH)

    def _copy_b(self, j):
        return pltpu.make_async_remote_copy(src_ref=self.send_b.at[j], dst_ref=self.recv_b.at[j], send_sem=self.sems[2].at[j],
                                            recv_sem=self.sems[3].at[j], device_id=(*self.chips[1 + j], self.c),
                                            device_id_type=MESH)

    def _load(self, j, core):
        return pltpu.make_async_copy(self.g.at[_dev_index(*self.chips[j], core)], self.stage.at[j % 2], self.load_sems.at[j % 2])

    def send_block(self, j):
        if j == 0:
            self._load(0, 1 - self.c).start()
        self._load(j, 1 - self.c).wait()
        if j + 1 < len(self.chips):
            self._load(j + 1, 1 - self.c).start()
        else:
            self._load(0, self.c).start()
        self.send_a[j] = self.stage[j % 2].astype(BF16)
        self._copy_a(j).start()

    def sum_block(self, j):
        self._load(j, self.c).wait()
        if j + 1 < len(self.chips):
            self._load(j + 1, self.c).start()
        self._copy_a(j).wait_recv()
        part = self.stage[j % 2] + self.recv_a[j].astype(F32)
        if j == 0:
            self.out[...] = part
        else:
            self.send_b[j - 1] = part.astype(BF16)
            self._copy_b(j - 1).start()

    def start(self):
        for j in range(len(self.chips)):
            self.send_block(j)

    def middle(self):
        for j in range(len(self.chips)):
            self.sum_block(j)

    def finish(self):
        for j in range(len(OTHER_CHIPS)):
            self._copy_b(j).wait_recv()
            self.out[...] += self.recv_b[j].astype(F32)
        for j in range(len(self.chips)):
            self._copy_a(j).wait_send()
        for j in range(len(OTHER_CHIPS)):
            self._copy_b(j).wait_send()


N_SCATTER_SCRATCH = 10


def _scatter_specs(g8s):
    na, nb = 1 + len(OTHER_CHIPS), len(OTHER_CHIPS)
    ins = [pl.BlockSpec(memory_space=pl.ANY) for _ in g8s]
    outs = [pl.BlockSpec(g.shape[1:], lambda *_: (0, 0)) for g in g8s]
    shapes = [jax.ShapeDtypeStruct(g.shape[1:], F32) for g in g8s]
    scratch = []
    for g in g8s:
        blk = g.shape[1:]
        scratch += [pltpu.VMEM((2,) + blk, F32), pltpu.SemaphoreType.DMA((2,)), pltpu.VMEM((na,) + blk, BF16), pltpu.VMEM((na,) + blk, BF16),
                    pltpu.VMEM((nb,) + blk, BF16), pltpu.VMEM((nb,) + blk, BF16),
                    pltpu.SemaphoreType.DMA((na,)), pltpu.SemaphoreType.DMA((na,)),
                    pltpu.SemaphoreType.DMA((nb,)), pltpu.SemaphoreType.DMA((nb,))]
    return ins, outs, shapes, scratch


def _run_scatters(scatters, step, n_steps):
    n_blocks = 1 + len(OTHER_CHIPS)
    for j in range(n_blocks):
        @pl.when(step == min(j, n_steps - 1))
        def _(j=j):
            for r in scatters:
                r.send_block(j)

    for j in range(n_blocks):
        @pl.when(step == min(n_blocks + j, n_steps - 1))
        def _(j=j):
            for r in scatters:
                r.sum_block(j)

    @pl.when(step == n_steps - 1)
    def _():
        for r in scatters:
            r.finish()


def _mix_proj(x, g_mix, w_in_t, b_qkv, b_cin):
    s = x.shape[0]
    tm = _wide_tile(s)

    def body(x_ref, g_ref, w_ref, bq_ref, bc_ref, qkv_ref, cin_ref, h1_ref):
        xv = x_ref[...]
        r = lax.rsqrt(jnp.mean(xv * xv, axis=-1, keepdims=True) + EPS)
        h = (xv * r * g_ref[...]).astype(BF16)
        h1_ref[...] = h
        qkv_ref[...] = _dot_nt(h, w_ref[0:QKV_COLS, :]) + bq_ref[...]
        cin_ref[...] = _dot_nt(h, w_ref[QKV_COLS:, :]) + bc_ref[...]

    return pl.pallas_call(
        body, grid=(s // tm,),
        in_specs=[pl.BlockSpec((tm, D_MODEL), lambda i: (i, 0)), _resident((1, D_MODEL)),
                  _resident((QKV_COLS + CIN_COLS, D_MODEL)), _resident((1, QKV_COLS)), _resident((1, CIN_COLS))],
        out_specs=[pl.BlockSpec((tm, QKV_COLS), lambda i: (i, 0)), pl.BlockSpec((tm, CIN_COLS), lambda i: (i, 0)),
                   pl.BlockSpec((tm, D_MODEL), lambda i: (i, 0))],
        out_shape=[jax.ShapeDtypeStruct((s, QKV_COLS), F32), jax.ShapeDtypeStruct((s, CIN_COLS), F32),
                   jax.ShapeDtypeStruct((s, D_MODEL), BF16)],
        compiler_params=_cparams("parallel"), name="mix_proj")(x, g_mix, w_in_t, b_qkv, b_cin)


def _kv_variants(kv_all, gk2, lo):
    k_all = kv_all[:, :LANES]
    v_all = kv_all[:, LANES:]
    kn_pre, rk = _head_norm(k_all, lo)
    kn = kn_pre * gk2
    kr = pltpu.roll(kn, HEAD_DIM, 1)
    vr = pltpu.roll(v_all, HEAD_DIM, 1)
    zero = jnp.zeros_like(kn)
    k_lo = [jnp.where(lo, kn, zero).astype(BF16), jnp.where(lo, kr, zero).astype(BF16)]
    k_hi = [jnp.where(lo, zero, kr).astype(BF16), jnp.where(lo, zero, kn).astype(BF16)]
    v_lo = [jnp.where(lo, v_all, zero).astype(BF16), jnp.where(lo, vr, zero).astype(BF16)]
    v_hi = [jnp.where(lo, zero, vr).astype(BF16), jnp.where(lo, zero, v_all).astype(BF16)]
    return k_lo, k_hi, v_lo, v_hi, kn_pre, rk


def _att_consts(first_tile, b):
    rows = 2 * ATT_BLOCK
    qi = lax.broadcasted_iota(jnp.int32, (rows, 2 * ATT_BLOCK), 0) % ATT_BLOCK
    kj = lax.broadcasted_iota(jnp.int32, (rows, 2 * ATT_BLOCK), 1)
    rel = qi + ATT_BLOCK - kj
    valid = (rel >= 0) & (rel < ATT_BLOCK)
    if b == 0:
        valid = valid & ((kj >= ATT_BLOCK) | jnp.logical_not(first_tile))
    return rel.astype(F32), valid


def _row_const(va, vb):
    top = lax.broadcasted_iota(jnp.int32, (2 * ATT_BLOCK, 1), 0) < ATT_BLOCK
    return jnp.where(top, va, vb)


def _probs(q2, k_op, rel, valid, slope, sink):
    sc = _dot_nt(q2, k_op) * (1.0 / math.sqrt(HEAD_DIM)) - slope * rel
    sc = jnp.where(valid, sc, NEG_INF)
    m = jnp.maximum(jnp.max(sc, axis=-1, keepdims=True), sink)
    p = jnp.exp(sc - m)
    e_sink = jnp.exp(sink - m)
    inv = 1.0 / (jnp.sum(p, axis=-1, keepdims=True) + e_sink)
    return p * inv, e_sink * inv


def _attn_fwd(qkv, gq2, gk2, sinks, shards):
    s = qkv.shape[0]
    tq = _tile(s)
    nb = tq // ATT_BLOCK
    ng = len(shards)
    g_in, g_out, g_shape, g_scratch = _gather_specs(shards)

    def body(q_ref, kv_ref, kvp_ref, gq_ref, gk_ref, sink_ref, *rest):
        out_ref = rest[ng]
        i = pl.program_id(0)
        _run_gathers([_Gather(rest[a], rest[ng + 1 + a], *rest[2 * ng + 1 + 4 * a:2 * ng + 5 + 4 * a]) for a in range(ng)],
                     i, s // tq)
        lo = _lo_mask((1, LANES))
        kv_all = jnp.concatenate([kvp_ref[...], kv_ref[...]], axis=0)
        k_lo, k_hi, v_lo, v_hi, _, _ = _kv_variants(kv_all, gk_ref[...], lo)
        for b in range(nb):
            rel, valid = _att_consts(i == 0, b)
            rows = slice(b * ATT_BLOCK, (b + 1) * ATT_BLOCK)
            keys = slice(b * ATT_BLOCK, (b + 2) * ATT_BLOCK)
            for kvh in range(N_KV_HEADS):
                pairs = (2 * kvh, 2 * kvh + 1)
                q2 = jnp.concatenate([q_ref[rows, p * LANES:(p + 1) * LANES] for p in pairs], axis=0)
                qn, _ = _head_norm(q2, lo)
                q2 = (qn * gq_ref[...]).astype(BF16)
                out = None
                for odd, (k_op, v_op) in enumerate(((k_lo[kvh][keys], v_lo[kvh][keys]), (k_hi[kvh][keys], v_hi[kvh][keys]))):
                    ha, hb = 2 * pairs[0] + odd, 2 * pairs[1] + odd
                    p, _ = _probs(q2, k_op, rel, valid, _row_const(SLOPES[ha], SLOPES[hb]),
                                  _row_const(sink_ref[ha], sink_ref[hb]))
                    o = _dot(p.astype(BF16), v_op)
                    out = o if out is None else out + o
                for n, p in enumerate(pairs):
                    out_ref[rows, p * LANES:(p + 1) * LANES] = out[n * ATT_BLOCK:(n + 1) * ATT_BLOCK].astype(BF16)

    return pl.pallas_call(
        body, grid=(s // tq,),
        in_specs=[pl.BlockSpec((tq, Q_COLS), lambda i: (i, 0)),
                  pl.BlockSpec((tq, 2 * KV_COLS), lambda i: (i, 2)),
                  pl.BlockSpec((ATT_BLOCK, 2 * KV_COLS), lambda i: (jnp.maximum(i * nb - 1, 0), 2)),
                  _resident((1, LANES)), _resident((1, LANES)),
                  pl.BlockSpec(memory_space=pltpu.SMEM)] + g_in,
        out_specs=[pl.BlockSpec((tq, Q_COLS), lambda i: (i, 0))] + g_out,
        out_shape=[jax.ShapeDtypeStruct((s, Q_COLS), BF16)] + g_shape,
        scratch_shapes=g_scratch,
        compiler_params=_cparams("arbitrary"), name="attn_fwd")(qkv, qkv, qkv, gq2, gk2, sinks, *shards)


def _group_stats(c1, lo):
    mu = _half_sums(c1, lo) * (1.0 / HEAD_DIM)
    d = c1 - mu
    rstd = lax.rsqrt(_half_sums(d * d, lo) * (1.0 / HEAD_DIM) + EPS)
    return d * rstd, rstd


def _rows(ref, first_row, n):
    return ref[pl.ds(first_row, n, stride=1), :].reshape(n // SUBLANES, SUBLANES, LANES)


def _conv_fwd(cin, cw8, cb, gain, bias, shards):
    s = cin.shape[0]
    tm = _tile(s)
    rc = 64
    nchunk = CONV_WIDTH // LANES
    lead = CONV_HALO - (CONV_KERNEL - 1)
    ng = len(shards)
    g_in, g_out, g_shape, g_scratch = _gather_specs(shards)

    def body(cin_ref, cw_ref, cb_ref, gain_ref, bias_ref, *rest):
        c3_ref, c1_ref, ext_ref = rest[ng], rest[ng + 1], rest[2 * ng + 2]
        _run_gathers([_Gather(rest[a], rest[ng + 2 + a], *rest[2 * ng + 3 + 4 * a:2 * ng + 7 + 4 * a]) for a in range(ng)],
                     pl.program_id(0), s // tm)

        @pl.when(pl.program_id(0) == 0)
        def _():
            ext_ref[:, 0:CONV_HALO, :] = jnp.zeros((nchunk, CONV_HALO, LANES), F32)

        lo = _lo_mask((1, LANES))
        for cc in range(nchunk):
            cols = slice(cc * LANES, (cc + 1) * LANES)
            gcols = slice(CONV_WIDTH + cc * LANES, CONV_WIDTH + (cc + 1) * LANES)
            ext_ref[cc, CONV_HALO:CONV_HALO + tm, :] = cin_ref[:, cols] * _sigmoid(cin_ref[:, gcols])
            ext = ext_ref.at[cc]
            for r in range(tm // rc):
                rows = slice(r * rc, (r + 1) * rc)
                acc = jnp.zeros((rc // SUBLANES, SUBLANES, LANES), F32)
                for k in range(CONV_KERNEL):
                    acc = acc + cw_ref[k * SUBLANES:(k + 1) * SUBLANES, cols][None] * _rows(ext, r * rc + lead + k, rc)
                c1 = acc.reshape(rc, LANES) + cb_ref[:, cols]
                c1_ref[cc, rows, :] = c1
                nrm, _ = _group_stats(c1, lo)
                c2 = nrm * gain_ref[:, cols] + bias_ref[:, cols]
                c3_ref[rows, cols] = (c2 * _sigmoid(c2)).astype(BF16)
        ext_ref[:, 0:CONV_HALO, :] = ext_ref[:, tm:tm + CONV_HALO, :]

    return pl.pallas_call(
        body, grid=(s // tm,),
        in_specs=[pl.BlockSpec((tm, CIN_COLS), lambda i: (i, 0)), _resident((CONV_KERNEL * SUBLANES, CONV_WIDTH)),
                  _resident((1, CONV_WIDTH)), _resident((1, CONV_WIDTH)), _resident((1, CONV_WIDTH))] + g_in,
        out_specs=[pl.BlockSpec((tm, CONV_WIDTH), lambda i: (i, 0)), pl.BlockSpec((nchunk, tm, LANES), lambda i: (0, i, 0))] + g_out,
        out_shape=[jax.ShapeDtypeStruct((s, CONV_WIDTH), BF16), jax.ShapeDtypeStruct((nchunk, s, LANES), F32)] + g_shape,
        scratch_shapes=[pltpu.VMEM((nchunk, tm + CONV_HALO, LANES), F32)] + g_scratch,
        compiler_params=_cparams("arbitrary"), name="conv_fwd")(cin, cw8, cb, gain, bias, *shards)


def _out_proj(x, attn, c3, wo_a, wo_c, b_out, g_ffn):
    s = x.shape[0]
    tm = _wide_tile(s)

    def body(x_ref, a_ref, c_ref, wa_ref, wc_ref, b_ref, g_ref, x2_ref, h2_ref):
        x2 = x_ref[...] + _dot(a_ref[...], wa_ref[...]) + _dot(c_ref[...], wc_ref[...]) + b_ref[...]
        x2_ref[...] = x2
        r = lax.rsqrt(jnp.mean(x2 * x2, axis=-1, keepdims=True) + EPS)
        h2_ref[...] = (x2 * r * g_ref[...]).astype(BF16)

    return pl.pallas_call(
        body, grid=(s // tm,),
        in_specs=[pl.BlockSpec((tm, D_MODEL), lambda i: (i, 0)), pl.BlockSpec((tm, Q_COLS), lambda i: (i, 0)),
                  pl.BlockSpec((tm, CONV_WIDTH), lambda i: (i, 0)),
                  pl.BlockSpec((Q_COLS, D_MODEL), lambda i: (0, 0), pipeline_mode=pl.Buffered(1)),
                  pl.BlockSpec((CONV_WIDTH, D_MODEL), lambda i: (1, 0), pipeline_mode=pl.Buffered(1)),
                  _resident((1, D_MODEL)), _resident((1, D_MODEL))],
        out_specs=[pl.BlockSpec((tm, D_MODEL), lambda i: (i, 0)), pl.BlockSpec((tm, D_MODEL), lambda i: (i, 0))],
        out_shape=[jax.ShapeDtypeStruct((s, D_MODEL), F32), jax.ShapeDtypeStruct((s, D_MODEL), BF16)],
        compiler_params=_cparams("parallel"), name="out_proj")(x, attn, c3, wo_a, wo_c, b_out, g_ffn)


FF_LANE_CHUNKS = -(-FF_CHUNK // LANES)
FF_PADDED = FF_LANE_CHUNKS * LANES


def _tap(ref, first_row, n):
    return ref[pl.ds(first_row, n, stride=1), :]


def _ffn_fwd(h2, x2, target, w_up, fw, fb, w_down):
    s = h2.shape[0]
    tm = _tile(s)
    hal = SUBLANES
    rc = min(128, tm)

    def body(h_ref, x2_ref, t_ref, wu_ref, fw_ref, fb_ref, wd_ref, up0_ref, gu_ref, act_ref, dy_ref, dyb_ref, loss_ref,
             ext_ref, carry_ref, act_buf, y_ref):
        i, ci = pl.program_id(0), pl.program_id(1)

        @pl.when((i == 0) & (ci == 0))
        def _():
            carry_ref[...] = jnp.zeros(carry_ref.shape, F32)
            ext_ref[...] = jnp.zeros(ext_ref.shape, F32)
            act_buf[...] = jnp.zeros(act_buf.shape, BF16)
            loss_ref[...] = jnp.zeros((1, 1), F32)

        @pl.when(ci == 0)
        def _():
            y_ref[...] = x2_ref[...]

        ws = (fw_ref[ci], fw_ref[ci + N_FF_PAIRS])
        bs = (fb_ref[ci], fb_ref[ci + N_FF_PAIRS])
        half_rows = (slice(0, tm // 2), slice(tm // 2, tm))
        n_grp = len(FF_COLS)

        def up_slices(grp):
            lo_c, hi_c = FF_COLS[grp]
            chunks = range(lo_c // LANES, -(-hi_c // LANES))

            def make(half, n, rows):
                def run():
                    c = ci + half * N_FF_PAIRS
                    u0 = _dot_nt(h_ref[rows, :], wu_ref[c, lo_c:hi_c, :])
                    up0_ref[half, 0, rows, lo_c:hi_c] = u0.astype(BF16)
                    if hi_c == FF_CHUNK:
                        up0_ref[half, 0, rows, FF_CHUNK:] = jnp.zeros((u0.shape[0], FF_PADDED - FF_CHUNK), BF16)
                    for j in chunks:
                        w = min(LANES, hi_c - j * LANES)
                        if n == 0:
                            ext_ref[half, j, 0:hal, 0:w] = carry_ref[c, :, j * LANES:j * LANES + w]
                        ext_ref[half, j, hal + rows.start:hal + rows.stop, 0:w] = u0[:, j * LANES - lo_c:j * LANES - lo_c + w]
                    if n == len(half_rows) - 1:
                        carry_ref[c, :, lo_c:hi_c] = u0[u0.shape[0] - hal:, :]
                return run
            return [make(half, n, rows) for half in range(2) for n, rows in enumerate(half_rows)]

        def down_slices(grp):
            lo_c, hi_c = FF_COLS[grp]

            def make(rows):
                def run():
                    y_ref[rows, :] += _dot(act_buf[rows, lo_c:hi_c], wd_ref[ci, lo_c:hi_c, :])
                return run
            return [make(rows) for rows in half_rows]

        def vector_blocks(grp):
            lo_c, hi_c = FF_COLS[grp]
            blocks = []
            for j in range(lo_c // LANES, -(-hi_c // LANES)):
                lanes = slice(j * LANES, (j + 1) * LANES)

                def gate(r, lanes=lanes, j=j):
                    base = r * rc
                    ups = []
                    for half in range(2):
                        e, w = ext_ref.at[half, j], ws[half]
                        ups.append(w[0:1, lanes] * _tap(e, base + hal - 2, rc) + w[1:2, lanes] * _tap(e, base + hal - 1, rc)
                                   + w[2:3, lanes] * _tap(e, base + hal, rc) + bs[half][:, lanes])
                    g, u = ups
                    gu_ref[0, 0, base:base + rc, lanes] = g.astype(BF16)
                    gu_ref[1, 0, base:base + rc, lanes] = u.astype(BF16)
                    act_buf[base:base + rc, lanes] = (g * _sigmoid(g) * u).astype(BF16)

                blocks += [functools.partial(gate, r) for r in range(tm // rc)]

            def finish():
                act_ref[0, :, lo_c:hi_c] = act_buf[:, lo_c:hi_c]
            blocks.append(finish)
            return blocks

        for run in up_slices(0):
            run()
        for grp in range(n_grp):
            matmuls = (up_slices(grp + 1) if grp + 1 < n_grp else []) + (down_slices(grp - 1) if grp > 0 else [])
            blocks = vector_blocks(grp)
            every = max(1, len(blocks) // (len(matmuls) + 1))
            for n, run in enumerate(blocks):
                run()
                if n % every == every - 1 and matmuls:
                    matmuls.pop(0)()
            for run in matmuls:
                run()
        for run in down_slices(n_grp - 1):
            run()

        @pl.when(ci == N_FF_PAIRS - 1)
        def _():
            e = y_ref[...] - t_ref[...]
            dy_ref[...] = e * (1.0 / D_MODEL)
            dyb_ref[...] = (e * (1.0 / D_MODEL)).astype(BF16)
            loss_ref[...] += (0.5 / D_MODEL) * jnp.sum(e * e).reshape(1, 1)

    tok = lambda i, ci: (i, 0)
    return pl.pallas_call(
        body, grid=(s // tm, N_FF_PAIRS),
        in_specs=[pl.BlockSpec((tm, D_MODEL), tok), pl.BlockSpec((tm, D_MODEL), tok), pl.BlockSpec((tm, D_MODEL), tok),
                  _resident((N_DEV, FF_CHUNK, D_MODEL)), _resident((N_DEV, 3, FF_PADDED)), _resident((N_DEV, 1, FF_PADDED)),
                  _resident((N_FF_PAIRS, FF_CHUNK, D_MODEL))],
        out_specs=[pl.BlockSpec((2, 1, tm, FF_PADDED), lambda i, ci: (0, ci, i, 0)),
                   pl.BlockSpec((2, 1, tm, FF_PADDED), lambda i, ci: (0, ci, i, 0)),
                   pl.BlockSpec((1, tm, FF_CHUNK), lambda i, ci: (ci, i, 0)),
                   pl.BlockSpec((tm, D_MODEL), tok), pl.BlockSpec((tm, D_MODEL), tok), pl.BlockSpec((1, 1), lambda i, ci: (0, 0))],
        out_shape=[jax.ShapeDtypeStruct((2, N_FF_PAIRS, s, FF_PADDED), BF16), jax.ShapeDtypeStruct((2, N_FF_PAIRS, s, FF_PADDED), BF16),
                   jax.ShapeDtypeStruct((N_FF_PAIRS, s, FF_CHUNK), BF16), jax.ShapeDtypeStruct((s, D_MODEL), F32),
                   jax.ShapeDtypeStruct((s, D_MODEL), BF16), jax.ShapeDtypeStruct((1, 1), F32)],
        scratch_shapes=[pltpu.VMEM((2, FF_LANE_CHUNKS, tm + hal, LANES), F32), pltpu.VMEM((N_DEV, hal, FF_CHUNK), F32),
                        pltpu.VMEM((tm, FF_PADDED), BF16), pltpu.VMEM((tm, D_MODEL), F32)],
        compiler_params=_cparams("arbitrary", "arbitrary"), name="ffn_fwd")(h2, x2, target, w_up, fw, fb, w_down)


def _ffn_bwd(dyb, up0, gu, w_up, fw, w_down):
    s = dyb.shape[0]
    tm = _tile(s)
    nt = s // tm
    nxt = SUBLANES
    rc = min(128, tm)

    def body(dy_ref, up0_ref, gu_ref, wu_ref, fw_ref, wd_ref,
             dup0_ref, dh2_ref, dfw_ref, dfb_ref, dext_ref, carry_ref, dact_buf, dup0_buf, dh2_acc):
        i, ci = pl.program_id(0), pl.program_id(1)

        @pl.when((i == 0) & (ci == 0))
        def _():
            for ref in (carry_ref, dfw_ref, dfb_ref, dext_ref, dact_buf):
                ref[...] = jnp.zeros(ref.shape, F32)
            dup0_buf[...] = jnp.zeros(dup0_buf.shape, BF16)

        @pl.when(ci == 0)
        def _():
            dh2_acc[...] = jnp.zeros(dh2_acc.shape, F32)

        ws = (fw_ref[ci], fw_ref[ci + N_FF_PAIRS])
        fold = lambda v: jnp.sum(v.reshape(rc // SUBLANES, SUBLANES, LANES), axis=0)
        half_rows = (slice(0, tm // 2), slice(tm // 2, tm))
        n_grp = len(FF_COLS)

        def dact_slices(grp):
            lo_c, hi_c = FF_COLS[grp]

            def make(rows):
                def run():
                    dact_buf[rows, lo_c:hi_c] = _dot_nt(dy_ref[rows, :], wd_ref[ci, lo_c:hi_c, :])
                return run
            return [make(rows) for rows in half_rows]

        def dh2_slices(grp):
            lo_c, hi_c = FF_COLS[grp]

            def make(half, rows):
                def run():
                    c = ci + half * N_FF_PAIRS
                    dh2_acc[rows, :] += _dot(dup0_buf[half, rows, lo_c:hi_c], wu_ref[c, lo_c:hi_c, :])
                return run
            return [make(half, rows) for half in range(2) for rows in half_rows]

        def vector_blocks(grp):
            lo_c, hi_c = FF_COLS[grp]
            chunks = range(lo_c // LANES, -(-hi_c // LANES))
            blocks = []

            def stage():
                for half in range(2):
                    c = ci + half * N_FF_PAIRS
                    for j in chunks:
                        dext_ref[half, j, tm:tm + nxt, :] = carry_ref[c, :, j * LANES:(j + 1) * LANES]
            blocks.append(stage)
            for j in chunks:
                lanes = slice(j * LANES, (j + 1) * LANES)
                acc = [jnp.zeros((SUBLANES, LANES), F32)] * 8

                def grads(r, lanes=lanes, j=j, acc=acc):
                    base = r * rc
                    g = gu_ref[0, 0, base:base + rc, lanes].astype(F32)
                    u = gu_ref[1, 0, base:base + rc, lanes].astype(F32)
                    sg = _sigmoid(g)
                    silu = g * sg
                    dact = dact_buf[base:base + rc, lanes]
                    ds = (dact * u * (sg + silu - silu * sg), dact * silu)
                    for half in range(2):
                        dext_ref[half, j, base:base + rc, :] = ds[half]
                        acc[4 * half] = acc[4 * half] + fold(ds[half])

                def conv_back(r, lanes=lanes, j=j, acc=acc):
                    base = r * rc
                    for half in range(2):
                        d, w = dext_ref.at[half, j], ws[half]
                        taps = [_tap(d, base + k, rc) for k in range(3)]
                        dup0 = w[2:3, lanes] * taps[0] + w[1:2, lanes] * taps[1] + w[0:1, lanes] * taps[2]
                        dup0_buf[half, base:base + rc, lanes] = dup0.astype(BF16)
                        u0 = up0_ref[half, 0, base:base + rc, lanes].astype(F32)
                        for k in range(3):
                            acc[4 * half + 1 + k] = acc[4 * half + 1 + k] + fold(taps[2 - k] * u0)

                def sums(lanes=lanes, j=j, acc=acc):
                    for half in range(2):
                        c = ci + half * N_FF_PAIRS
                        carry_ref[c, :, lanes] = dext_ref[half, j, 0:nxt, :]
                        dfb_ref[c, :, lanes] += jnp.sum(acc[4 * half], axis=0, keepdims=True)
                        dfw_ref[c, :, lanes] += jnp.concatenate(
                            [jnp.sum(acc[4 * half + 1 + k], axis=0, keepdims=True) for k in range(3)], axis=0)

                blocks += [functools.partial(grads, r) for r in range(tm // rc)]
                blocks += [functools.partial(conv_back, r) for r in range(tm // rc)] + [sums]

            def finish():
                for half in range(2):
                    dup0_ref[half, 0, :, lo_c:hi_c] = dup0_buf[half, :, lo_c:hi_c]
            blocks.append(finish)
            return blocks

        for run in dact_slices(0):
            run()
        for grp in range(n_grp):
            matmuls = (dact_slices(grp + 1) if grp + 1 < n_grp else []) + (dh2_slices(grp - 1) if grp > 0 else [])
            blocks = vector_blocks(grp)
            every = max(1, len(blocks) // (len(matmuls) + 1))
            for n, run in enumerate(blocks):
                run()
                if n % every == every - 1 and matmuls:
                    matmuls.pop(0)()
            for run in matmuls:
                run()
        for run in dh2_slices(n_grp - 1):
            run()

        @pl.when(ci == N_FF_PAIRS - 1)
        def _():
            dh2_ref[...] = dh2_acc[...].astype(BF16)

    tok = lambda i, ci: (nt - 1 - i, 0)
    acc = lambda shape: pl.BlockSpec(shape, lambda i, ci: (0,) * len(shape))
    saved = pl.BlockSpec((2, 1, tm, FF_PADDED), lambda i, ci: (0, ci, nt - 1 - i, 0))
    return pl.pallas_call(
        body, grid=(nt, N_FF_PAIRS),
        in_specs=[pl.BlockSpec((tm, D_MODEL), tok), saved, saved,
                  _resident((N_DEV, FF_CHUNK, D_MODEL)), _resident((N_DEV, 3, FF_PADDED)),
                  _resident((N_FF_PAIRS, FF_CHUNK, D_MODEL))],
        out_specs=[pl.BlockSpec((2, 1, tm, FF_CHUNK), lambda i, ci: (0, ci, nt - 1 - i, 0)),
                   pl.BlockSpec((tm, D_MODEL), tok), acc((N_DEV, 3, FF_PADDED)), acc((N_DEV, 1, FF_PADDED))],
        out_shape=[jax.ShapeDtypeStruct((2, N_FF_PAIRS, s, FF_CHUNK), BF16), jax.ShapeDtypeStruct((s, D_MODEL), BF16),
                   jax.ShapeDtypeStruct((N_DEV, 3, FF_PADDED), F32), jax.ShapeDtypeStruct((N_DEV, 1, FF_PADDED), F32)],
        scratch_shapes=[pltpu.VMEM((2, FF_LANE_CHUNKS, tm + nxt, LANES), F32), pltpu.VMEM((N_DEV, nxt, FF_PADDED), F32),
                        pltpu.VMEM((tm, FF_PADDED), F32), pltpu.VMEM((2, tm, FF_PADDED), BF16), pltpu.VMEM((tm, D_MODEL), F32)],
        compiler_params=_cparams("arbitrary", "arbitrary"), name="ffn_bwd")(dyb, up0, gu, w_up, fw, w_down)


def _ffn_norm_bwd(dh2, dy, x2, g_ffn, w_out):
    s = dy.shape[0]
    tm = _wide_tile(s)

    def body(dh_ref, dy_ref, x2_ref, g_ref, wo_ref, dx2_ref, dmix_ref, dg_ref, dbo_ref):
        @pl.when(pl.program_id(0) == 0)
        def _():
            dg_ref[...] = jnp.zeros(dg_ref.shape, F32)
            dbo_ref[...] = jnp.zeros(dbo_ref.shape, F32)

        x2v = x2_ref[...]
        r = lax.rsqrt(jnp.mean(x2v * x2v, axis=-1, keepdims=True) + EPS)
        n2 = x2v * r
        dh2 = dh_ref[...].astype(F32)
        dg_ref[...] += jnp.sum(dh2 * n2, axis=0, keepdims=True)
        dn = dh2 * g_ref[...]
        dx2 = dy_ref[...] + r * (dn - n2 * jnp.mean(dn * n2, axis=-1, keepdims=True))
        dx2_ref[...] = dx2
        dbo_ref[...] += jnp.sum(dx2, axis=0, keepdims=True)
        dmix_ref[...] = _dot_nt(dx2.astype(BF16), wo_ref[...]).astype(BF16)

    tok = pl.BlockSpec((tm, D_MODEL), lambda i: (i, 0))
    vec = pl.BlockSpec((1, D_MODEL), lambda i: (0, 0))
    return pl.pallas_call(
        body, grid=(s // tm,),
        in_specs=[tok, tok, tok, _resident((1, D_MODEL)), _resident((D_MODEL, D_MODEL))],
        out_specs=[tok, tok, vec, vec],
        out_shape=[jax.ShapeDtypeStruct((s, D_MODEL), F32), jax.ShapeDtypeStruct((s, D_MODEL), BF16),
                   jax.ShapeDtypeStruct((1, D_MODEL), F32), jax.ShapeDtypeStruct((1, D_MODEL), F32)],
        compiler_params=_cparams("arbitrary"), name="ffn_norm_bwd")(dh2, dy, x2, g_ffn, w_out)


def _conv_bwd(dmixed, c1, cin, cw8, gain, bias, g8s):
    ns = len(g8s)
    s_in, s_out, s_shape, s_scratch = _scatter_specs(g8s)
    s = cin.shape[0]
    tm = _tile(s)
    nt = s // tm
    rc = 64
    rn = min(256, tm)
    hal = CONV_HALO
    nchunk = CONV_WIDTH // LANES

    def body(dc3_ref, dc3n_ref, c1_ref, c1n_ref, cin_ref, cw_ref, gain_ref, bias_ref, *rest):
        dcin_ref, dcw_ref, dcb_ref, dgain_ref, dbias_ref, dbcin_ref = rest[ns:ns + 6]
        dc1_ext, dcw8 = rest[2 * ns + 6:2 * ns + 8]
        i = pl.program_id(0)
        first, last = i == 0, i == nt - 1
        own = rest[2 * ns + 8:]
        _run_scatters([_ReduceScatter(rest[a], rest[ns + 6 + a], *own[N_SCATTER_SCRATCH * a:N_SCATTER_SCRATCH * (a + 1)])
                       for a in range(ns)], i, nt)

        @pl.when(first)
        def _():
            for ref in (dcw8, dcb_ref, dgain_ref, dbias_ref, dbcin_ref):
                ref[...] = jnp.zeros(ref.shape, F32)

        lo = _lo_mask((1, LANES))

        def norm_bwd(dc3, c1v, cols):
            nrm, rstd = _group_stats(c1v, lo)
            c2 = nrm * gain_ref[:, cols] + bias_ref[:, cols]
            sg = _sigmoid(c2)
            dc2 = dc3 * (sg * (1.0 + c2 * (1.0 - sg)))
            dn = dc2 * gain_ref[:, cols]
            inv = 1.0 / HEAD_DIM
            dc1 = rstd * (dn - _half_sums(dn, lo) * inv - nrm * (_half_sums(dn * nrm, lo) * inv))
            return dc1, dc2, nrm

        def row_sum(v):
            return jnp.sum(v, axis=0, keepdims=True)

        for cc in range(nchunk):
            cols = slice(cc * LANES, (cc + 1) * LANES)
            gcols = slice(CONV_WIDTH + cc * LANES, CONV_WIDTH + (cc + 1) * LANES)
            d1e = dc1_ext.at[cc]
            dc1n, _, _ = norm_bwd(dc3n_ref[:, cols].astype(F32), c1n_ref[cc], cols)
            d1e[tm:tm + hal, :] = jnp.where(last, 0.0, dc1n)

            for r in range(tm // rn):
                rows = slice(r * rn, (r + 1) * rn)
                dc1, dc2, nrm = norm_bwd(dc3_ref[rows, cols].astype(F32), c1_ref[cc, rows, :], cols)
                d1e[rows, :] = dc1
                dgain_ref[:, cols] += row_sum(dc2 * nrm)
                dbias_ref[:, cols] += row_sum(dc2)
                dcb_ref[:, cols] += row_sum(dc1)
            zero = jnp.zeros((1, LANES), F32)

            def taps(r, sums):
                rows = pl.ds(pl.multiple_of(r * rc, rc), rc)
                a = cin_ref[rows, cols]
                sg = _sigmoid(cin_ref[rows, gcols])
                c0 = (a * sg).reshape(rc // SUBLANES, SUBLANES, LANES)
                dc0 = jnp.zeros((rc // SUBLANES, SUBLANES, LANES), F32)
                for k in range(CONV_KERNEL):
                    krows = slice(k * SUBLANES, (k + 1) * SUBLANES)
                    shifted = _rows(d1e, r * rc + CONV_KERNEL - 1 - k, rc)
                    dc0 = dc0 + cw_ref[krows, cols][None] * shifted
                    dcw8[krows, cols] += jnp.sum(shifted * c0, axis=0)
                dc0 = dc0.reshape(rc, LANES)
                da = dc0 * sg
                dgate = dc0 * a * sg * (1.0 - sg)
                dcin_ref[rows, cols] = da.astype(BF16)
                dcin_ref[rows, gcols] = dgate.astype(BF16)
                return sums[0] + row_sum(da), sums[1] + row_sum(dgate)

            sums = lax.fori_loop(0, tm // rc, taps, (zero, zero))
            dbcin_ref[:, cols] += sums[0]
            dbcin_ref[:, gcols] += sums[1]

        @pl.when(last)
        def _():
            for k in range(CONV_KERNEL):
                dcw_ref[k:k + 1, :] = jnp.sum(dcw8[k * SUBLANES:(k + 1) * SUBLANES, :], axis=0, keepdims=True)

    nh = tm // hal
    acc = lambda shape: pl.BlockSpec(shape, lambda i: (0,) * len(shape))
    return pl.pallas_call(
        body, grid=(nt,),
        in_specs=[pl.BlockSpec((tm, CONV_WIDTH), lambda i: (i, 1)),
                  pl.BlockSpec((hal, CONV_WIDTH), lambda i: (jnp.minimum((i + 1) * nh, s // hal - 1), 1)),
                  pl.BlockSpec((nchunk, tm, LANES), lambda i: (0, i, 0)),
                  pl.BlockSpec((nchunk, hal, LANES), lambda i: (0, jnp.minimum((i + 1) * nh, s // hal - 1), 0)),
                  pl.BlockSpec((tm, CIN_COLS), lambda i: (i, 0)),
                  _resident((CONV_KERNEL * SUBLANES, CONV_WIDTH)), _resident((1, CONV_WIDTH)), _resident((1, CONV_WIDTH))] + s_in,
        out_specs=[pl.BlockSpec((tm, CIN_COLS), lambda i: (i, 0)), acc((CONV_KERNEL, CONV_WIDTH)), acc((1, CONV_WIDTH)),
                   acc((1, CONV_WIDTH)), acc((1, CONV_WIDTH)), acc((1, CIN_COLS))] + s_out,
        out_shape=[jax.ShapeDtypeStruct((s, CIN_COLS), BF16), jax.ShapeDtypeStruct((CONV_KERNEL, CONV_WIDTH), F32),
                   jax.ShapeDtypeStruct((1, CONV_WIDTH), F32), jax.ShapeDtypeStruct((1, CONV_WIDTH), F32),
                   jax.ShapeDtypeStruct((1, CONV_WIDTH), F32), jax.ShapeDtypeStruct((1, CIN_COLS), F32)] + s_shape,
        scratch_shapes=[pltpu.VMEM((nchunk, tm + hal, LANES), F32),
                        pltpu.VMEM((CONV_KERNEL * SUBLANES, CONV_WIDTH), F32)] + s_scratch,
        compiler_params=_cparams("arbitrary"), name="conv_bwd")(dmixed, dmixed, c1, c1, cin, cw8, gain, bias, *g8s)


def _attn_bwd(qkv, dmixed, gq2, gk2, sinks, g8s):
    ns = len(g8s)
    s_in, s_out, s_shape, s_scratch = _scatter_specs(g8s)
    s = qkv.shape[0]
    tq = _tile(s)
    nb = tq // ATT_BLOCK
    nt = s // tq

    def body(q_ref, kv_ref, kvp_ref, do_ref, gq_ref, gk_ref, sink_ref, *rest):
        dqkv_ref, dgq_ref, dgk_ref, dsink_ref, dbqkv_ref = rest[ns:ns + 5]
        dk_acc, dv_acc, carry_k, carry_v = rest[2 * ns + 5:2 * ns + 9]
        i = pl.program_id(0)
        t = nt - 1 - i
        own = rest[2 * ns + 9:]
        _run_scatters([_ReduceScatter(rest[a], rest[ns + 5 + a], *own[N_SCATTER_SCRATCH * a:N_SCATTER_SCRATCH * (a + 1)])
                       for a in range(ns)], i, nt)

        @pl.when(i == 0)
        def _():
            for ref in (carry_k, carry_v, dgq_ref, dgk_ref, dsink_ref, dbqkv_ref):
                ref[...] = jnp.zeros(ref.shape, F32)

        lo = _lo_mask((1, LANES))
        lane_id = lax.broadcasted_iota(jnp.int32, (1, LANES), 1)
        kv_all = jnp.concatenate([kvp_ref[...], kv_ref[...]], axis=0)
        k_lo, k_hi, v_lo, v_hi, kn_pre, rk = _kv_variants(kv_all, gk_ref[...], lo)
        for acc_ref, carry in ((dk_acc, carry_k), (dv_acc, carry_v)):
            acc_ref[:, 0:tq, :] = jnp.zeros((N_KV_HEADS, tq, LANES), F32)
            acc_ref[:, tq:tq + ATT_BLOCK, :] = carry[...]
        dsink = jnp.zeros((1, LANES), F32)
        dgq = jnp.zeros((1, LANES), F32)
        gq = gq_ref[...]
        for b in range(nb):
            rel, valid = _att_consts(t == 0, b)
            rows = slice(b * ATT_BLOCK, (b + 1) * ATT_BLOCK)
            keys = slice(b * ATT_BLOCK, (b + 2) * ATT_BLOCK)
            for kvh in range(N_KV_HEADS):
                pairs = (2 * kvh, 2 * kvh + 1)
                q_raw = jnp.concatenate([q_ref[rows, p * LANES:(p + 1) * LANES] for p in pairs], axis=0)
                qn_pre, rq = _head_norm(q_raw, lo)
                q2 = (qn_pre * gq).astype(BF16)
                do2 = jnp.concatenate([do_ref[rows, p * LANES:(p + 1) * LANES] for p in pairs], axis=0).astype(BF16)
                dq2 = jnp.zeros((2 * ATT_BLOCK, LANES), F32)
                for odd, (k_op, v_op) in enumerate(((k_lo[kvh][keys], v_lo[kvh][keys]), (k_hi[kvh][keys], v_hi[kvh][keys]))):
                    ha, hb = 2 * pairs[0] + odd, 2 * pairs[1] + odd
                    p, p_sink = _probs(q2, k_op, rel, valid, _row_const(SLOPES[ha], SLOPES[hb]),
                                       _row_const(sink_ref[ha], sink_ref[hb]))
                    dp = _dot_nt(do2, v_op)
                    delta = jnp.sum(p * dp, axis=-1, keepdims=True)
                    ds = (p * (dp - delta) * (1.0 / math.sqrt(HEAD_DIM))).astype(BF16)
                    dsk = p_sink * delta
                    dsink = dsink - jnp.where(lane_id == ha, jnp.sum(dsk[0:ATT_BLOCK]), 0.0) \
                        - jnp.where(lane_id == hb, jnp.sum(dsk[ATT_BLOCK:]), 0.0)
                    dq2 = dq2 + _dot(ds, k_op)
                    half = lo if odd == 0 else jnp.logical_not(lo)
                    dk_acc[kvh, keys, :] += jnp.where(half, _dot_tn(ds, q2), 0.0)
                    dv_acc[kvh, keys, :] += jnp.where(half, _dot_tn(p.astype(BF16), do2), 0.0)
                dgq = dgq + jnp.sum(dq2 * qn_pre, axis=0, keepdims=True)
                dq_raw = _head_norm_bwd(dq2 * gq, qn_pre, rq, lo)
                for n, p_ in enumerate(pairs):
                    blk = dq_raw[n * ATT_BLOCK:(n + 1) * ATT_BLOCK]
                    dqkv_ref[rows, p_ * LANES:(p_ + 1) * LANES] = blk.astype(BF16)
                    dbqkv_ref[:, p_ * LANES:(p_ + 1) * LANES] += jnp.sum(blk, axis=0, keepdims=True)
        carry_k[...] = dk_acc[:, 0:ATT_BLOCK, :]
        carry_v[...] = dv_acc[:, 0:ATT_BLOCK, :]

        def fold(acc_ref):
            both = []
            for kvh in range(N_KV_HEADS):
                a = acc_ref[kvh, ATT_BLOCK:ATT_BLOCK + tq, :]
                both.append(a + pltpu.roll(a, HEAD_DIM, 1))
            return jnp.where(lo, both[0], both[1])

        dkn = fold(dk_acc)
        dv = fold(dv_acc)
        kn_c, rk_c = kn_pre[ATT_BLOCK:], rk[ATT_BLOCK:]
        dgk_ref[...] += jnp.sum(dkn * kn_c, axis=0, keepdims=True)
        dk_raw = _head_norm_bwd(dkn * gk_ref[...], kn_c, rk_c, lo)
        dqkv_ref[:, Q_COLS:Q_COLS + KV_COLS] = dk_raw.astype(BF16)
        dqkv_ref[:, Q_COLS + KV_COLS:] = dv.astype(BF16)
        dbqkv_ref[:, Q_COLS:Q_COLS + KV_COLS] += jnp.sum(dk_raw, axis=0, keepdims=True)
        dbqkv_ref[:, Q_COLS + KV_COLS:] += jnp.sum(dv, axis=0, keepdims=True)
        dgq_ref[...] += dgq
        dsink_ref[...] += dsink

        @pl.when(i == nt - 1)
        def _():
            for ref in (dgq_ref, dgk_ref):
                v = ref[...]
                ref[...] = v + pltpu.roll(v, HEAD_DIM, 1)

    acc = lambda shape: pl.BlockSpec(shape, lambda i: (0,) * len(shape))
    return pl.pallas_call(
        body, grid=(nt,),
        in_specs=[pl.BlockSpec((tq, Q_COLS), lambda i: (nt - 1 - i, 0)),
                  pl.BlockSpec((tq, 2 * KV_COLS), lambda i: (nt - 1 - i, 2)),
                  pl.BlockSpec((ATT_BLOCK, 2 * KV_COLS), lambda i: (jnp.maximum((nt - 1 - i) * nb - 1, 0), 2)),
                  pl.BlockSpec((tq, Q_COLS), lambda i: (nt - 1 - i, 0)),
                  _resident((1, LANES)), _resident((1, LANES)), pl.BlockSpec(memory_space=pltpu.SMEM)] + s_in,
        out_specs=[pl.BlockSpec((tq, QKV_COLS), lambda i: (nt - 1 - i, 0)), acc((1, LANES)), acc((1, LANES)),
                   acc((1, LANES)), acc((1, QKV_COLS))] + s_out,
        out_shape=[jax.ShapeDtypeStruct((s, QKV_COLS), BF16), jax.ShapeDtypeStruct((1, LANES), F32),
                   jax.ShapeDtypeStruct((1, LANES), F32), jax.ShapeDtypeStruct((1, LANES), F32),
                   jax.ShapeDtypeStruct((1, QKV_COLS), F32)] + s_shape,
        scratch_shapes=[pltpu.VMEM((N_KV_HEADS, tq + ATT_BLOCK, LANES), F32), pltpu.VMEM((N_KV_HEADS, tq + ATT_BLOCK, LANES), F32),
                        pltpu.VMEM((N_KV_HEADS, ATT_BLOCK, LANES), F32), pltpu.VMEM((N_KV_HEADS, ATT_BLOCK, LANES), F32)] + s_scratch,
        compiler_params=_cparams("arbitrary"), name="attn_bwd")(qkv, qkv, qkv, dmixed, gq2, gk2, sinks, *g8s)


def _in_bwd(dqkv, dcin, w_in_t, x, dx2, g_mix):
    s = x.shape[0]
    tm = _wide_tile(s)

    def body(dq_ref, dc_ref, w_ref, x_ref, dx2_ref, g_ref, gx_ref, dg_ref):
        @pl.when(pl.program_id(0) == 0)
        def _():
            dg_ref[...] = jnp.zeros(dg_ref.shape, F32)

        dh = _dot(dq_ref[...], w_ref[0:QKV_COLS, :]) + _dot(dc_ref[...], w_ref[QKV_COLS:, :])
        xv = x_ref[...]
        r = lax.rsqrt(jnp.mean(xv * xv, axis=-1, keepdims=True) + EPS)
        n = xv * r
        dg_ref[...] += jnp.sum(dh * n, axis=0, keepdims=True)
        dn = dh * g_ref[...]
        gx_ref[...] = dx2_ref[...] + r * (dn - n * jnp.mean(dn * n, axis=-1, keepdims=True))

    return pl.pallas_call(
        body, grid=(s // tm,),
        in_specs=[pl.BlockSpec((tm, QKV_COLS), lambda i: (i, 0)), pl.BlockSpec((tm, CIN_COLS), lambda i: (i, 0)),
                  _resident((QKV_COLS + CIN_COLS, D_MODEL)),
                  pl.BlockSpec((tm, D_MODEL), lambda i: (i, 0)), pl.BlockSpec((tm, D_MODEL), lambda i: (i, 0)),
                  _resident((1, D_MODEL))],
        out_specs=[pl.BlockSpec((tm, D_MODEL), lambda i: (i, 0)), pl.BlockSpec((1, D_MODEL), lambda i: (0, 0))],
        out_shape=[jax.ShapeDtypeStruct((s, D_MODEL), F32), jax.ShapeDtypeStruct((1, D_MODEL), F32)],
        compiler_params=_cparams("arbitrary"), name="in_bwd")(dqkv, dcin, w_in_t, x, dx2, g_mix)


def _tn_matmul(a, b, name, tokens):
    ga, s, m = a.shape
    gb, _, n = b.shape
    g = max(ga, gb)
    tk = min(tokens, s)

    def body(a_ref, b_ref, o_ref):
        @pl.when(pl.program_id(1) == 0)
        def _():
            o_ref[...] = jnp.zeros(o_ref.shape, F32)

        o_ref[0] += _dot_tn(a_ref[0].astype(BF16), b_ref[0].astype(BF16))

    return pl.pallas_call(
        body, grid=(g, s // tk),
        in_specs=[pl.BlockSpec((1, tk, m), (lambda gi, k: (gi, k, 0)) if ga > 1 else (lambda gi, k: (0, k, 0))),
                  pl.BlockSpec((1, tk, n), (lambda gi, k: (gi, k, 0)) if gb > 1 else (lambda gi, k: (0, k, 0)))],
        out_specs=pl.BlockSpec((1, m, n), lambda gi, k: (gi, 0, 0)),
        out_shape=jax.ShapeDtypeStruct((g, m, n), F32),
        compiler_params=_cparams("parallel", "arbitrary"), name=name)(a, b)


def _tn_matmul_pair(a0, a1, b, name):
    s, m0 = a0.shape
    m1, n = a1.shape[1], b.shape[1]
    tk = min(TN_TOKENS, s)

    def body(a0_ref, a1_ref, b_ref, o_ref):
        @pl.when(pl.program_id(0) == 0)
        def _():
            o_ref[...] = jnp.zeros(o_ref.shape, F32)

        bv = b_ref[...].astype(BF16)
        o_ref[0:m0, :] += _dot_tn(a0_ref[...].astype(BF16), bv)
        o_ref[m0:, :] += _dot_tn(a1_ref[...].astype(BF16), bv)

    return pl.pallas_call(
        body, grid=(s // tk,),
        in_specs=[pl.BlockSpec((tk, m0), lambda k: (k, 0)), pl.BlockSpec((tk, m1), lambda k: (k, 0)),
                  pl.BlockSpec((tk, n), lambda k: (k, 0))],
        out_specs=pl.BlockSpec((m0 + m1, n), lambda k: (0, 0)),
        out_shape=jax.ShapeDtypeStruct((m0 + m1, n), F32),
        compiler_params=_cparams("arbitrary"), name=name)(a0, a1, b)


def _allgather(shards, dtypes):
    n = len(shards)
    n_copies = 1 + 2 * len(OTHER_CHIPS)

    def body(*refs):
        ins, outs = refs[:n], refs[n:2 * n]
        send_sems, recv_sems = refs[2 * n:]
        x, y, c = _position()
        me, sibling = (x, y, c), (x, y, 1 - c)
        chips = [(_flip(x, fx), _flip(y, fy)) for fx, fy in OTHER_CHIPS]
        for a in range(n):
            outs[a][_dev_index(*me)] = ins[a][...].astype(dtypes[a])

        def copy(a, k, block, to):
            rows = outs[a].at[_dev_index(*block)]
            return pltpu.make_async_remote_copy(src_ref=rows, dst_ref=rows, send_sem=send_sems.at[a, k],
                                                recv_sem=recv_sems.at[a, k], device_id=to, device_id_type=MESH)

        started = []
        for a in range(n):
            for j, chip in enumerate(chips):
                started.append(copy(a, 1 + j, me, (*chip, c)))
            started.append(copy(a, 0, me, sibling))
        for cp in started:
            cp.start()
        for a in range(n):
            for j, chip in enumerate(chips):
                copy(a, 1 + j, (*chip, c), me).wait_recv()
                fwd = copy(a, 1 + len(chips) + j, (*chip, c), sibling)
                fwd.start()
                started.append(fwd)
        for a in range(n):
            copy(a, 0, sibling, me).wait_recv()
            for j, chip in enumerate(chips):
                copy(a, 1 + len(chips) + j, (*chip, 1 - c), me).wait_recv()
        for cp in started:
            cp.wait_send()

    vmem = pl.BlockSpec(memory_space=pltpu.VMEM)
    return pl.pallas_call(
        body, in_specs=[vmem] * n, out_specs=[vmem] * n,
        out_shape=[jax.ShapeDtypeStruct((N_DEV,) + w.shape, dt) for w, dt in zip(shards, dtypes)],
        scratch_shapes=[pltpu.SemaphoreType.DMA((n, n_copies)), pltpu.SemaphoreType.DMA((n, n_copies))],
        compiler_params=pltpu.CompilerParams(vmem_limit_bytes=VMEM_LIMIT), name="allgather_weights")(*shards)


def _final_exchange(g8, v):
    rows = v.shape[0]
    n_chips = 1 + len(OTHER_CHIPS)
    _, _, s_shape, s_scratch = _scatter_specs([g8])

    def body(g_ref, v_ref, gout_ref, vout_ref, from_sibling, chip_sums, send_sems, recv_sems, *rs_scratch):
        scatter = _ReduceScatter(g_ref, gout_ref, *rs_scratch)
        x, y, c = _position()
        my_chip = 2 * x + y
        chips = [(_flip(x, fx), _flip(y, fy)) for fx, fy in OTHER_CHIPS]

        def swap():
            return pltpu.make_async_remote_copy(src_ref=v_ref, dst_ref=from_sibling, send_sem=send_sems.at[0],
                                                recv_sem=recv_sems.at[0], device_id=(x, y, 1 - c), device_id_type=MESH)

        def push(j):
            return pltpu.make_async_remote_copy(src_ref=chip_sums.at[my_chip], dst_ref=chip_sums.at[my_chip],
                                                send_sem=send_sems.at[1 + j], recv_sem=recv_sems.at[1 + j],
                                                device_id=(*chips[j], c), device_id_type=MESH)

        swap().start()
        scatter.start()
        swap().wait_recv()
        chip_sums[my_chip] = v_ref[...] + from_sibling[...]
        for j in range(len(chips)):
            push(j).start()
        scatter.middle()
        for j in range(len(chips)):
            push(j).wait_recv()
        swap().wait_send()
        for j in range(len(chips)):
            push(j).wait_send()
        total = chip_sums[0]
        for q in range(1, n_chips):
            total = total + chip_sums[q]
        vout_ref[...] = total
        scatter.finish()

    vmem = pl.BlockSpec(memory_space=pltpu.VMEM)
    return pl.pallas_call(
        body, in_specs=[pl.BlockSpec(memory_space=pl.ANY), vmem], out_specs=[vmem, vmem],
        out_shape=s_shape + [jax.ShapeDtypeStruct((rows, LANES), F32)],
        scratch_shapes=[pltpu.VMEM((rows, LANES), F32), pltpu.VMEM((n_chips, rows, LANES), F32),
                        pltpu.SemaphoreType.DMA((n_chips,)), pltpu.SemaphoreType.DMA((n_chips,))] + s_scratch,
        compiler_params=pltpu.CompilerParams(vmem_limit_bytes=VMEM_LIMIT), name="final_exchange")(g8, v)


def _row_tile(r):
    for n in (8, 4, 2):
        if r % (n * SUBLANES) == 0:
            return r // n
    return r


def _adam_math(wv, gv, mv, vv):
    mn = ADAM_B1 * mv + (1.0 - ADAM_B1) * gv
    vn = ADAM_B2 * vv + (1.0 - ADAM_B2) * (gv * gv)
    m_hat = mn / (1.0 - ADAM_B1 ** ADAM_STEP)
    v_hat = vn / (1.0 - ADAM_B2 ** ADAM_STEP)
    return -ADAM_LR * (m_hat / (jnp.sqrt(v_hat) + ADAM_EPS) + ADAM_WD * wv), mn, vn


def _adamw(w, g, m, v, name):
    r, c_ = w.shape
    tr = _row_tile(r)

    def body(w_ref, g_ref, m_ref, v_ref, d_ref, mo_ref, vo_ref):
        d_ref[...], mo_ref[...], vo_ref[...] = _adam_math(w_ref[...], g_ref[...], m_ref[...], v_ref[...])

    spec = pl.BlockSpec((tr, c_), lambda i: (i, 0))
    return pl.pallas_call(
        body, grid=(r // tr,), in_specs=[spec] * 4, out_specs=[spec] * 3,
        out_shape=[jax.ShapeDtypeStruct((r, c_), F32)] * 3,
        compiler_params=_cparams("parallel"), name=name)(w, g, m, v)


FW_ROWS = 24
CW_ROWS = 32
R_FW = 0
R_FB = R_FW + N_DEV * FW_ROWS
R_CW = R_FB + 48
R_BQKV = R_CW + (CONV_WIDTH // LANES) * CW_ROWS
R_BCIN = R_BQKV + 8
R_GMIX = R_BCIN + 8
R_BOUT = R_GMIX + 8
R_GFFN = R_BOUT + 8
R_CB = R_GFFN + 8
R_CGAIN = R_CB + 8
R_CBIAS = R_CGAIN + 8
R_QKS = R_CBIAS + 8
SMALL_ROWS = R_QKS + 8


def _pack_small(raw):
    def rows(a, n):
        a = a.reshape(-1, LANES)
        return jnp.pad(a, ((0, n - a.shape[0]), (0, 0)))

    fw = jnp.pad(raw["dfw"].reshape(N_DEV, -1, LANES), ((0, 0), (0, FW_ROWS - 3 * FF_LANE_CHUNKS), (0, 0)))
    cw = jnp.pad(raw["dcw"].reshape(CONV_KERNEL, -1, LANES).transpose(1, 0, 2), ((0, 0), (0, CW_ROWS - CONV_KERNEL), (0, 0)))
    qks = jnp.concatenate([raw["dgq"], raw["dgk"], raw["dsink"], jnp.pad(raw["loss"], ((0, 0), (0, LANES - 1)))], axis=0)
    return jnp.concatenate([
        fw.reshape(-1, LANES), rows(raw["dfb"][:, 0, :FF_CHUNK], 48), cw.reshape(-1, LANES), rows(raw["dbqkv"], 8),
        rows(raw["dbcin"], 8), rows(raw["dg_mix"], 8), rows(raw["db_out"], 8), rows(raw["dg_ffn"], 8), rows(raw["dcb"], 8),
        rows(raw["dcgain"], 8), rows(raw["dcbias"], 8), rows(qks, 8)], axis=0)


def _adamw_small(gpack, w, m, v):
    n = len(SMALL)
    ix = {name: i for i, name in enumerate(SMALL)}

    def body(g_ref, *refs):
        w_refs, m_refs, v_refs, outs = refs[:n], refs[n:2 * n], refs[2 * n:3 * n], refs[3 * n:]
        d = _dev_index(*_position())

        def step(name, idx, gv):
            i = ix[name]
            delta, mn, vn = _adam_math(w_refs[i][idx], gv, m_refs[i][idx], v_refs[i][idx])
            for ref, val in zip(outs[4 * i:4 * i + 4], (gv, delta, mn, vn)):
                ref[idx] = val

        def whole(name, row, nrows):
            step(name, (slice(None), slice(None)), g_ref[row:row + nrows, :])

        whole("mix_norm_gain", R_GMIX, 8)
        whole("b_out", R_BOUT, 8)
        whole("ffn_norm_gain", R_GFFN, 8)
        whole("conv_dw_b", R_CB, 4)
        whole("conv_norm_gain", R_CGAIN, 4)
        whole("conv_norm_bias", R_CBIAS, 4)
        whole("ffn_dw_b", R_FB, 2 * D_FF // LANES)
        nq = QKV_COLS // LANES
        step("b_in", (slice(0, nq), slice(None)), g_ref[R_BQKV:R_BQKV + nq, :])
        step("b_in", (slice(nq, nq + CIN_COLS // LANES), slice(None)), g_ref[R_BCIN:R_BCIN + CIN_COLS // LANES, :])
        step("q_norm_gain", (slice(None), slice(None)), g_ref[R_QKS:R_QKS + 1, 0:HEAD_DIM])
        step("k_norm_gain", (slice(None), slice(None)), g_ref[R_QKS + 1:R_QKS + 2, 0:HEAD_DIM])
        step("attn_sinks", (slice(None), slice(None)), g_ref[R_QKS + 2:R_QKS + 3, 0:N_Q_HEADS])
        blk = g_ref[pl.ds(pl.multiple_of(R_CW + CW_ROWS * lax.shift_right_logical(d, 1), SUBLANES), CW_ROWS), :]
        blk = jnp.where((d & 1) == 1, pltpu.roll(blk, HEAD_DIM, 1), blk)
        step("conv_dw_w", (slice(None), slice(None)), blk[0:CONV_KERNEL, 0:CONV_WIDTH // N_DEV])
        blk = g_ref[pl.ds(pl.multiple_of(R_FW + FW_ROWS * d, SUBLANES), FW_ROWS), :]
        for k in range(3):
            for j in range(FF_LANE_CHUNKS):
                wd = min(LANES, FF_CHUNK - j * LANES)
                row = k * FF_LANE_CHUNKS + j
                step("ffn_dw_w", (slice(k, k + 1), slice(j * LANES, j * LANES + wd)), blk[row:row + 1, 0:wd])

    vmem = pl.BlockSpec(memory_space=pltpu.VMEM)
    args = [gpack] + [d[name] for d in (w, m, v) for name in SMALL]
    outs = pl.pallas_call(
        body, in_specs=[vmem] * len(args), out_specs=[vmem] * (4 * n),
        out_shape=[jax.ShapeDtypeStruct(w[name].shape, F32) for name in SMALL for _ in range(4)],
        compiler_params=pltpu.CompilerParams(vmem_limit_bytes=VMEM_LIMIT), name="adamw_small")(*args)
    return {name: outs[4 * i:4 * i + 4] for i, name in enumerate(SMALL)}


def _token_mixing(x, p, attn_shards, conv_shards):
    qkv, cin, h1 = _mix_proj(x, p["g_mix"], p["w_in_t"], p["b_qkv"], p["b_cin"])
    attn, *from_attn = _attn_fwd(qkv, p["gq2"], p["gk2"], p["sinks"], attn_shards)
    c3, c1, *from_conv = _conv_fwd(cin, p["cw8"], p["cb"], p["cgain"], p["cbias"], conv_shards)
    return (qkv, cin, h1, attn, c3, c1), from_attn, from_conv


def _rest_of_step(x, target, p, saved, scatter):
    s = x.shape[0]
    qkv, cin, h1, attn, c3, c1 = saved
    cw8, w_out, w_up, w_down = p["cw8"], p["w_out"], p["w_up"], p["w_down"]
    x2, h2 = _out_proj(x, attn, c3, w_out, w_out, p["b_out"], p["g_ffn"])
    fw, fb = p["fw"], p["fb"]
    up0, gu, act, dy, dyb, loss = _ffn_fwd(h2, x2, target, w_up, fw, fb, w_down)
    dup0, dh2, dfw, dfb = _ffn_bwd(dyb, up0, gu, w_up, fw, w_down)
    dx2, dmixed, dg_ffn, db_out = _ffn_norm_bwd(dh2, dy, x2, p["g_ffn"], w_out)
    dw_up = _tn_matmul(dup0.reshape(N_DEV, s, FF_CHUNK), h2[None], "dw_up", 2 * TN_TOKENS)
    dw_down = _tn_matmul(act, dyb[None], "dw_down", 2 * TN_TOKENS).reshape(N_DEV, -1, D_MODEL)
    dw_out = _tn_matmul_pair(attn, c3, dx2, "dw_out").reshape(N_DEV, -1, D_MODEL)
    dcin, dcw, dcb, dcgain, dcbias, dbcin, *g_up = _conv_bwd(dmixed, c1, cin, cw8, p["cgain"], p["cbias"], [dw_up] if scatter else [])
    dqkv, dgq, dgk, dsink, dbqkv, *g_down_out = _attn_bwd(qkv, dmixed, p["gq2"], p["gk2"], p["sinks"],
                                                          [dw_down, dw_out] if scatter else [])
    dw_in = _tn_matmul_pair(dqkv, dcin, h1, "dw_in").reshape(N_DEV, -1, D_MODEL)
    grad_x, dg_mix = _in_bwd(dqkv, dcin, p["w_in_t"], x, dx2, p["g_mix"])
    if scatter:
        big = {"w_up": g_up[0], "w_down": g_down_out[0], "w_in": dw_in, "w_out": g_down_out[1]}
    else:
        big = {"w_up": dw_up, "w_down": dw_down, "w_in": dw_in, "w_out": dw_out}
    small = dict(dg_mix=dg_mix, dbqkv=dbqkv, dbcin=dbcin, dgq=dgq, dgk=dgk, dsink=dsink, dcw=dcw, dcb=dcb, dcgain=dcgain,
                 dcbias=dcbias, db_out=db_out, dg_ffn=dg_ffn, dfw=dfw, dfb=dfb, loss=loss)
    return loss, grad_x, big, small


BIG = ("w_in", "w_out", "w_up", "w_down")
SMALL = ("mix_norm_gain", "b_in", "q_norm_gain", "k_norm_gain", "attn_sinks", "conv_dw_w", "conv_dw_b",
         "conv_norm_gain", "conv_norm_bias", "b_out", "ffn_norm_gain", "ffn_dw_w", "ffn_dw_b")
ORDER = ("mix_norm_gain", "w_in", "b_in", "q_norm_gain", "k_norm_gain", "attn_sinks", "conv_dw_w", "conv_dw_b",
         "conv_norm_gain", "conv_norm_bias", "w_out", "b_out", "ffn_norm_gain", "w_up", "ffn_dw_w", "ffn_dw_b", "w_down")


def kernel(x, mix_norm_gain, w_in, b_in, q_norm_gain, k_norm_gain, attn_sinks, conv_dw_w, conv_dw_b, conv_norm_gain, conv_norm_bias, w_out, b_out, ffn_norm_gain, w_up, ffn_dw_w, ffn_dw_b, w_down, loss_target, m_mix_norm_gain, m_w_in, m_b_in, m_q_norm_gain, m_k_norm_gain, m_attn_sinks, m_conv_dw_w, m_conv_dw_b, m_conv_norm_gain, m_conv_norm_bias, m_w_out, m_b_out, m_ffn_norm_gain, m_w_up, m_ffn_dw_w, m_ffn_dw_b, m_w_down, v_mix_norm_gain, v_w_in, v_b_in, v_q_norm_gain, v_k_norm_gain, v_attn_sinks, v_conv_dw_w, v_conv_dw_b, v_conv_norm_gain, v_conv_norm_bias, v_w_out, v_b_out, v_ffn_norm_gain, v_w_up, v_ffn_dw_w, v_ffn_dw_b, v_w_down):
    w = dict(mix_norm_gain=mix_norm_gain, w_in=w_in, b_in=b_in, q_norm_gain=q_norm_gain, k_norm_gain=k_norm_gain,
             attn_sinks=attn_sinks, conv_dw_w=conv_dw_w, conv_dw_b=conv_dw_b, conv_norm_gain=conv_norm_gain,
             conv_norm_bias=conv_norm_bias, w_out=w_out, b_out=b_out, ffn_norm_gain=ffn_norm_gain, w_up=w_up,
             ffn_dw_w=ffn_dw_w, ffn_dw_b=ffn_dw_b, w_down=w_down)
    m = dict(mix_norm_gain=m_mix_norm_gain, w_in=m_w_in, b_in=m_b_in, q_norm_gain=m_q_norm_gain, k_norm_gain=m_k_norm_gain,
             attn_sinks=m_attn_sinks, conv_dw_w=m_conv_dw_w, conv_dw_b=m_conv_dw_b, conv_norm_gain=m_conv_norm_gain,
             conv_norm_bias=m_conv_norm_bias, w_out=m_w_out, b_out=m_b_out, ffn_norm_gain=m_ffn_norm_gain, w_up=m_w_up,
             ffn_dw_w=m_ffn_dw_w, ffn_dw_b=m_ffn_dw_b, w_down=m_w_down)
    v = dict(mix_norm_gain=v_mix_norm_gain, w_in=v_w_in, b_in=v_b_in, q_norm_gain=v_q_norm_gain, k_norm_gain=v_k_norm_gain,
             attn_sinks=v_attn_sinks, conv_dw_w=v_conv_dw_w, conv_dw_b=v_conv_dw_b, conv_norm_gain=v_conv_norm_gain,
             conv_norm_bias=v_conv_norm_bias, w_out=v_w_out, b_out=v_b_out, ffn_norm_gain=v_ffn_norm_gain, w_up=v_w_up,
             ffn_dw_w=v_ffn_dw_w, ffn_dw_b=v_ffn_dw_b, w_down=v_w_down)
    s = x.shape[1]

    wi8, cw8, fw8 = _allgather([w_in.T, conv_dw_w, ffn_dw_w], [BF16, F32, F32])
    lane_pad = ((0, 0), (0, 0), (0, FF_PADDED - FF_CHUNK))
    p = {
        "g_mix": mix_norm_gain.reshape(1, -1), "w_in_t": wi8.reshape(QKV_COLS + CIN_COLS, D_MODEL),
        "b_qkv": b_in[:QKV_COLS].reshape(1, -1), "b_cin": b_in[QKV_COLS:].reshape(1, -1),
        "gq2": jnp.tile(q_norm_gain, 2).reshape(1, -1), "gk2": jnp.tile(k_norm_gain, 2).reshape(1, -1), "sinks": attn_sinks,
        "cw8": jnp.repeat(cw8.transpose(1, 0, 2).reshape(CONV_KERNEL, CONV_WIDTH), SUBLANES, axis=0),
        "cb": conv_dw_b.reshape(1, -1), "cgain": conv_norm_gain.reshape(1, -1), "cbias": conv_norm_bias.reshape(1, -1),
        "b_out": b_out.reshape(1, -1), "g_ffn": ffn_norm_gain.reshape(1, -1),
        "fw": jnp.pad(fw8, lane_pad), "fb": jnp.pad(ffn_dw_b.reshape(N_DEV, 1, FF_CHUNK), lane_pad),
    }

    saved, (wu8,), (wo8, wd8) = _token_mixing(x[0], p, [w_up.T], [w_out, w_down])
    p.update(w_out=wo8.reshape(D_MODEL, D_MODEL), w_up=wu8, w_down=wd8.reshape(N_FF_PAIRS, FF_CHUNK, D_MODEL))
    loss, grad_x, big, small = _rest_of_step(x[0], loss_target[0], p, saved, True)

    g = dict(big)
    g["w_in"], gpack = _final_exchange(big["w_in"], _pack_small(small))

    delta, new_m, new_v = {}, {}, {}
    for n in BIG:
        if n in ("w_in", "w_up"):
            outs = _adamw(w[n].T, g[n], m[n].T, v[n].T, "adamw_" + n)
            g[n], delta[n], new_m[n], new_v[n] = g[n].T, *[o.T for o in outs]
        else:
            delta[n], new_m[n], new_v[n] = _adamw(w[n], g[n], m[n], v[n], "adamw_" + n)

    def view(a):
        return a if a.ndim == 2 else (a.reshape(-1, LANES) if a.size % LANES == 0 else a.reshape(1, -1))

    small_out = _adamw_small(gpack, *[{n: view(d[n]) for n in SMALL} for d in (w, m, v)])
    for n in SMALL:
        g[n], delta[n], new_m[n], new_v[n] = [a.reshape(w[n].shape) for a in small_out[n]]

    total = gpack[R_QKS + 3, 0]
    return (total, grad_x.reshape(1, s, D_MODEL), *[g[n] for n in ORDER], *[delta[n] for n in ORDER],
            *[new_m[n] for n in ORDER], *[new_v[n] for n in ORDER])
```

```python
import functools
import math

import jax
import jax.numpy as jnp
from jax import lax
from jax.experimental import pallas as pl
from jax.experimental.pallas import tpu as pltpu

F32 = jnp.float32
BF16 = jnp.bfloat16

D_MODEL = 1024
HEAD_DIM = 64
N_Q_HEADS = 8
N_KV_HEADS = 2
Q_COLS = 512
KV_COLS = 128
QKV_COLS = Q_COLS + 2 * KV_COLS
CONV_WIDTH = 512
CIN_COLS = 2 * CONV_WIDTH
CONV_KERNEL = 31
CONV_HALO = 32
D_FF = 2816
N_DEV = 8
FF_CHUNK = 2 * D_FF // N_DEV
N_FF_PAIRS = N_DEV // 2
ATT_BLOCK = 128
EPS = 1e-6
NEG_INF = -1e30
SLOPES = [float(2.0 ** (-8.0 * (h + 1.0) / N_Q_HEADS)) for h in range(N_Q_HEADS)]

ADAM_LR = 0.001
ADAM_B1 = 0.9
ADAM_B2 = 0.999
ADAM_EPS = 1e-08
ADAM_WD = 0.01
ADAM_STEP = 10

LANES = 128
SUBLANES = 8
VMEM_LIMIT = 56 * 1024 * 1024
MESH = pl.DeviceIdType.MESH


def _cparams(*sem, **kw):
    return pltpu.CompilerParams(dimension_semantics=sem or None, vmem_limit_bytes=VMEM_LIMIT, **kw)


def _resident(shape):
    nd = len(shape)
    return pl.BlockSpec(shape, lambda *_: (0,) * nd, pipeline_mode=pl.Buffered(1))


def _dot(a, b):
    return jnp.dot(a, b, preferred_element_type=F32)


def _dot_nt(a, b):
    return lax.dot_general(a, b, (((1,), (1,)), ((), ())), preferred_element_type=F32)


def _dot_tn(a, b):
    return lax.dot_general(a, b, (((0,), (0,)), ((), ())), preferred_element_type=F32)


def _sigmoid(x):
    return 1.0 / (1.0 + jnp.exp(-x))


def _lo_mask(shape):
    return lax.broadcasted_iota(jnp.int32, shape, len(shape) - 1) % LANES < HEAD_DIM


def _half_sums(t, lo):
    s_lo = jnp.sum(jnp.where(lo, t, 0.0), axis=-1, keepdims=True)
    s_hi = jnp.sum(jnp.where(lo, 0.0, t), axis=-1, keepdims=True)
    return jnp.where(lo, s_lo, s_hi)


def _head_norm(t, lo):
    r = lax.rsqrt(_half_sums(t * t, lo) * (1.0 / HEAD_DIM) + EPS)
    return t * r, r


def _head_norm_bwd(dn, n, r, lo):
    return r * (dn - n * (_half_sums(dn * n, lo) * (1.0 / HEAD_DIM)))


def _tile(s):
    return min(512, s)


def _wide_tile(s):
    return min(1024, s)


TN_TOKENS = 2048
FF_COLS = ((0, 256), (256, 512), (512, 704))


def _position():
    return lax.axis_index("x"), lax.axis_index("y"), lax.axis_index("c")


def _dev_index(px, py, pc):
    return 4 * px + 2 * py + pc


def _flip(v, bit):
    return 1 - v if bit else v


OTHER_CHIPS = ((1, 0), (0, 1), (1, 1))
N_GATHER_COPIES = 1 + 2 * len(OTHER_CHIPS)


class _Gather:
    def __init__(self, shard_ref, out_ref, cast_buf, send_sems, recv_sems, local_sem):
        self.shard, self.out, self.buf = shard_ref, out_ref, cast_buf
        self.send_sems, self.recv_sems, self.local_sem = send_sems, recv_sems, local_sem
        x, y, c = _position()
        self.c = c
        self.me, self.sibling = (x, y, c), (x, y, 1 - c)
        self.chips = [(_flip(x, fx), _flip(y, fy)) for fx, fy in OTHER_CHIPS]

    def _copy(self, k, block, to, from_buf=False):
        rows = self.out.at[_dev_index(*block)]
        return pltpu.make_async_remote_copy(src_ref=self.buf if from_buf else rows, dst_ref=rows,
                                            send_sem=self.send_sems.at[k], recv_sem=self.recv_sems.at[k],
                                            device_id=to, device_id_type=MESH)

    def _local(self):
        return pltpu.make_async_copy(self.buf, self.out.at[_dev_index(*self.me)], self.local_sem)

    def start(self):
        self.buf[...] = self.shard[...].astype(self.buf.dtype)
        self._local().start()
        for j, chip in enumerate(self.chips):
            self._copy(1 + j, self.me, (*chip, self.c), from_buf=True).start()
        self._copy(0, self.me, self.sibling, from_buf=True).start()

    def forward(self):
        for j, chip in enumerate(self.chips):
            self._copy(1 + j, (*chip, self.c), self.me).wait_recv()
            self._copy(1 + len(self.chips) + j, (*chip, self.c), self.sibling).start()

    def finish(self):
        self._copy(0, self.sibling, self.me).wait_recv()
        for j, chip in enumerate(self.chips):
            self._copy(1 + len(self.chips) + j, (*chip, 1 - self.c), self.me).wait_recv()
        for k in range(N_GATHER_COPIES):
            self._copy(k, self.me, self.sibling).wait_send()
        self._local().wait()


def _gather_specs(shards):
    whole = [pl.BlockSpec(w.shape, lambda *_, nd=w.ndim: (0,) * nd, pipeline_mode=pl.Buffered(1)) for w in shards]
    outs = [pl.BlockSpec(memory_space=pl.ANY) for _ in shards]
    shapes = [jax.ShapeDtypeStruct((N_DEV,) + w.shape, BF16) for w in shards]
    scratch = []
    for w in shards:
        scratch += [pltpu.VMEM(w.shape, BF16), pltpu.SemaphoreType.DMA((N_GATHER_COPIES,)),
                    pltpu.SemaphoreType.DMA((N_GATHER_COPIES,)), pltpu.SemaphoreType.DMA(())]
    return whole, outs, shapes, scratch


def _run_gathers(gathers, step, n_steps):
    @pl.when(step == 0)
    def _():
        for g in gathers:
            g.start()

    @pl.when(step == 3 * n_steps // 4)
    def _():
        for g in gathers:
            g.forward()

    @pl.when(step == n_steps - 1)
    def _():
        for g in gathers:
            g.finish()


class _ReduceScatter:
    def __init__(self, g_ref, out_ref, stage, load_sems, send_a, recv_a, send_b, recv_b, sa_send, sa_recv, sb_send, sb_recv):
        self.g, self.out, self.stage, self.load_sems = g_ref, out_ref, stage, load_sems
        self.send_a, self.recv_a, self.send_b, self.recv_b = send_a, recv_a, send_b, recv_b
        self.sems = (sa_send, sa_recv, sb_send, sb_recv)
        x, y, c = _position()
        self.c, self.sibling = c, (x, y, 1 - c)
        self.chips = [(x, y)] + [(_flip(x, fx), _flip(y, fy)) for fx, fy in OTHER_CHIPS]

    def _copy_a(self, j):
        return pltpu.make_async_remote_copy(src_ref=self.send_a.at[j], dst_ref=self.recv_a.at[j], send_sem=self.sems[0].at[j],
                                            recv_sem=self.sems[1].at[j], device_id=self.sibling, device_id_type=MESH)

    def _copy_b(self, j):
        return pltpu.make_async_remote_copy(src_ref=self.send_b.at[j], dst_ref=self.recv_b.at[j], send_sem=self.sems[2].at[j],
                                            recv_sem=self.sems[3].at[j], device_id=(*self.chips[1 + j], self.c),
                                            device_id_type=MESH)

    def _load(self, j, core):
        return pltpu.make_async_copy(self.g.at[_dev_index(*self.chips[j], core)], self.stage.at[j % 2], self.load_sems.at[j % 2])

    def send_block(self, j):
        if j == 0:
            self._load(0, 1 - self.c).start()
        self._load(j, 1 - self.c).wait()
        if j + 1 < len(self.chips):
            self._load(j + 1, 1 - self.c).start()
        else:
            self._load(0, self.c).start()
        self.send_a[j] = self.stage[j % 2].astype(BF16)
        self._copy_a(j).start()

    def sum_block(self, j):
        self._load(j, self.c).wait()
        if j + 1 < len(self.chips):
            self._load(j + 1, self.c).start()
        self._copy_a(j).wait_recv()
        part = self.stage[j % 2] + self.recv_a[j].astype(F32)
        if j == 0:
            self.out[...] = part
        else:
            self.send_b[j - 1] = part.astype(BF16)
            self._copy_b(j - 1).start()

    def start(self):
        for j in range(len(self.chips)):
            self.send_block(j)

    def middle(self):
        for j in range(len(self.chips)):
            self.sum_block(j)

    def finish(self):
        for j in range(len(OTHER_CHIPS)):
            self._copy_b(j).wait_recv()
            self.out[...] += self.recv_b[j].astype(F32)
        for j in range(len(self.chips)):
            self._copy_a(j).wait_send()
        for j in range(len(OTHER_CHIPS)):
            self._copy_b(j).wait_send()


N_SCATTER_SCRATCH = 10


def _scatter_specs(g8s):
    na, nb = 1 + len(OTHER_CHIPS), len(OTHER_CHIPS)
    ins = [pl.BlockSpec(memory_space=pl.ANY) for _ in g8s]
    outs = [pl.BlockSpec(g.shape[1:], lambda *_: (0, 0)) for g in g8s]
    shapes = [jax.ShapeDtypeStruct(g.shape[1:], F32) for g in g8s]
    scratch = []
    for g in g8s:
        blk = g.shape[1:]
        scratch += [pltpu.VMEM((2,) + blk, F32), pltpu.SemaphoreType.DMA((2,)), pltpu.VMEM((na,) + blk, BF16), pltpu.VMEM((na,) + blk, BF16),
                    pltpu.VMEM((nb,) + blk, BF16), pltpu.VMEM((nb,) + blk, BF16),
                    pltpu.SemaphoreType.DMA((na,)), pltpu.SemaphoreType.DMA((na,)),
                    pltpu.SemaphoreType.DMA((nb,)), pltpu.SemaphoreType.DMA((nb,))]
    return ins, outs, shapes, scratch


def _run_scatters(scatters, step, n_steps):
    n_blocks = 1 + len(OTHER_CHIPS)
    for j in range(n_blocks):
        @pl.when(step == min(j, n_steps - 1))
        def _(j=j):
            for r in scatters:
                r.send_block(j)

    for j in range(n_blocks):
        @pl.when(step == min(n_blocks + j, n_steps - 1))
        def _(j=j):
            for r in scatters:
                r.sum_block(j)

    @pl.when(step == n_steps - 1)
    def _():
        for r in scatters:
            r.finish()


def _mix_proj(x, g_mix, w_in_t, b_qkv, b_cin):
    s = x.shape[0]
    tm = _wide_tile(s)

    def body(x_ref, g_ref, w_ref, bq_ref, bc_ref, qkv_ref, cin_ref, h1_ref):
        xv = x_ref[...]
        r = lax.rsqrt(jnp.mean(xv * xv, axis=-1, keepdims=True) + EPS)
        h = (xv * r * g_ref[...]).astype(BF16)
        h1_ref[...] = h
        qkv_ref[...] = _dot_nt(h, w_ref[0:QKV_COLS, :]) + bq_ref[...]
        cin_ref[...] = _dot_nt(h, w_ref[QKV_COLS:, :]) + bc_ref[...]

    return pl.pallas_call(
        body, grid=(s // tm,),
        in_specs=[pl.BlockSpec((tm, D_MODEL), lambda i: (i, 0)), _resident((1, D_MODEL)),
                  _resident((QKV_COLS + CIN_COLS, D_MODEL)), _resident((1, QKV_COLS)), _resident((1, CIN_COLS))],
        out_specs=[pl.BlockSpec((tm, QKV_COLS), lambda i: (i, 0)), pl.BlockSpec((tm, CIN_COLS), lambda i: (i, 0)),
                   pl.BlockSpec((tm, D_MODEL), lambda i: (i, 0))],
        out_shape=[jax.ShapeDtypeStruct((s, QKV_COLS), F32), jax.ShapeDtypeStruct((s, CIN_COLS), F32),
                   jax.ShapeDtypeStruct((s, D_MODEL), BF16)],
        compiler_params=_cparams("parallel"), name="mix_proj")(x, g_mix, w_in_t, b_qkv, b_cin)


def _kv_variants(kv_all, gk2, lo):
    k_all = kv_all[:, :LANES]
    v_all = kv_all[:, LANES:]
    kn_pre, rk = _head_norm(k_all, lo)
    kn = kn_pre * gk2
    kr = pltpu.roll(kn, HEAD_DIM, 1)
    vr = pltpu.roll(v_all, HEAD_DIM, 1)
    zero = jnp.zeros_like(kn)
    k_lo = [jnp.where(lo, kn, zero).astype(BF16), jnp.where(lo, kr, zero).astype(BF16)]
    k_hi = [jnp.where(lo, zero, kr).astype(BF16), jnp.where(lo, zero, kn).astype(BF16)]
    v_lo = [jnp.where(lo, v_all, zero).astype(BF16), jnp.where(lo, vr, zero).astype(BF16)]
    v_hi = [jnp.where(lo, zero, vr).astype(BF16), jnp.where(lo, zero, v_all).astype(BF16)]
    return k_lo, k_hi, v_lo, v_hi, kn_pre, rk


def _att_consts(first_tile, b):
    rows = 2 * ATT_BLOCK
    qi = lax.broadcasted_iota(jnp.int32, (rows, 2 * ATT_BLOCK), 0) % ATT_BLOCK
    kj = lax.broadcasted_iota(jnp.int32, (rows, 2 * ATT_BLOCK), 1)
    rel = qi + ATT_BLOCK - kj
    valid = (rel >= 0) & (rel < ATT_BLOCK)
    if b == 0:
        valid = valid & ((kj >= ATT_BLOCK) | jnp.logical_not(first_tile))
    return rel.astype(F32), valid


def _row_const(va, vb):
    top = lax.broadcasted_iota(jnp.int32, (2 * ATT_BLOCK, 1), 0) < ATT_BLOCK
    return jnp.where(top, va, vb)


def _probs(q2, k_op, rel, valid, slope, sink):
    sc = _dot_nt(q2, k_op) * (1.0 / math.sqrt(HEAD_DIM)) - slope * rel
    sc = jnp.where(valid, sc, NEG_INF)
    m = jnp.maximum(jnp.max(sc, axis=-1, keepdims=True), sink)
    p = jnp.exp(sc - m)
    e_sink = jnp.exp(sink - m)
    inv = 1.0 / (jnp.sum(p, axis=-1, keepdims=True) + e_sink)
    return p * inv, e_sink * inv


def _attn_fwd(qkv, gq2, gk2, sinks, shards):
    s = qkv.shape[0]
    tq = _tile(s)
    nb = tq // ATT_BLOCK
    ng = len(shards)
    g_in, g_out, g_shape, g_scratch = _gather_specs(shards)

    def body(q_ref, kv_ref, kvp_ref, gq_ref, gk_ref, sink_ref, *rest):
        out_ref = rest[ng]
        i = pl.program_id(0)
        _run_gathers([_Gather(rest[a], rest[ng + 1 + a], *rest[2 * ng + 1 + 4 * a:2 * ng + 5 + 4 * a]) for a in range(ng)],
                     i, s // tq)
        lo = _lo_mask((1, LANES))
        kv_all = jnp.concatenate([kvp_ref[...], kv_ref[...]], axis=0)
        k_lo, k_hi, v_lo, v_hi, _, _ = _kv_variants(kv_all, gk_ref[...], lo)
        for b in range(nb):
            rel, valid = _att_consts(i == 0, b)
            rows = slice(b * ATT_BLOCK, (b + 1) * ATT_BLOCK)
            keys = slice(b * ATT_BLOCK, (b + 2) * ATT_BLOCK)
            for kvh in range(N_KV_HEADS):
                pairs = (2 * kvh, 2 * kvh + 1)
                q2 = jnp.concatenate([q_ref[rows, p * LANES:(p + 1) * LANES] for p in pairs], axis=0)
                qn, _ = _head_norm(q2, lo)
                q2 = (qn * gq_ref[...]).astype(BF16)
                out = None
                for odd, (k_op, v_op) in enumerate(((k_lo[kvh][keys], v_lo[kvh][keys]), (k_hi[kvh][keys], v_hi[kvh][keys]))):
                    ha, hb = 2 * pairs[0] + odd, 2 * pairs[1] + odd
                    p, _ = _probs(q2, k_op, rel, valid, _row_const(SLOPES[ha], SLOPES[hb]),
                                  _row_const(sink_ref[ha], sink_ref[hb]))
                    o = _dot(p.astype(BF16), v_op)
                    out = o if out is None else out + o
                for n, p in enumerate(pairs):
                    out_ref[rows, p * LANES:(p + 1) * LANES] = out[n * ATT_BLOCK:(n + 1) * ATT_BLOCK].astype(BF16)

    return pl.pallas_call(
        body, grid=(s // tq,),
        in_specs=[pl.BlockSpec((tq, Q_COLS), lambda i: (i, 0)),
                  pl.BlockSpec((tq, 2 * KV_COLS), lambda i: (i, 2)),
                  pl.BlockSpec((ATT_BLOCK, 2 * KV_COLS), lambda i: (jnp.maximum(i * nb - 1, 0), 2)),
                  _resident((1, LANES)), _resident((1, LANES)),
                  pl.BlockSpec(memory_space=pltpu.SMEM)] + g_in,
        out_specs=[pl.BlockSpec((tq, Q_COLS), lambda i: (i, 0))] + g_out,
        out_shape=[jax.ShapeDtypeStruct((s, Q_COLS), BF16)] + g_shape,
        scratch_shapes=g_scratch,
        compiler_params=_cparams("arbitrary"), name="attn_fwd")(qkv, qkv, qkv, gq2, gk2, sinks, *shards)


def _group_stats(c1, lo):
    mu = _half_sums(c1, lo) * (1.0 / HEAD_DIM)
    d = c1 - mu
    rstd = lax.rsqrt(_half_sums(d * d, lo) * (1.0 / HEAD_DIM) + EPS)
    return d * rstd, rstd


def _rows(ref, first_row, n):
    return ref[pl.ds(first_row, n, stride=1), :].reshape(n // SUBLANES, SUBLANES, LANES)


def _conv_fwd(cin, cw8, cb, gain, bias, shards):
    s = cin.shape[0]
    tm = _tile(s)
    rc = 64
    nchunk = CONV_WIDTH // LANES
    lead = CONV_HALO - (CONV_KERNEL - 1)
    ng = len(shards)
    g_in, g_out, g_shape, g_scratch = _gather_specs(shards)

    def body(cin_ref, cw_ref, cb_ref, gain_ref, bias_ref, *rest):
        c3_ref, c1_ref, ext_ref = rest[ng], rest[ng + 1], rest[2 * ng + 2]
        _run_gathers([_Gather(rest[a], rest[ng + 2 + a], *rest[2 * ng + 3 + 4 * a:2 * ng + 7 + 4 * a]) for a in range(ng)],
                     pl.program_id(0), s // tm)

        @pl.when(pl.program_id(0) == 0)
        def _():
            ext_ref[:, 0:CONV_HALO, :] = jnp.zeros((nchunk, CONV_HALO, LANES), F32)

        lo = _lo_mask((1, LANES))
        for cc in range(nchunk):
            cols = slice(cc * LANES, (cc + 1) * LANES)
            gcols = slice(CONV_WIDTH + cc * LANES, CONV_WIDTH + (cc + 1) * LANES)
            ext_ref[cc, CONV_HALO:CONV_HALO + tm, :] = cin_ref[:, cols] * _sigmoid(cin_ref[:, gcols])
            ext = ext_ref.at[cc]
            for r in range(tm // rc):
                rows = slice(r * rc, (r + 1) * rc)
                acc = jnp.zeros((rc // SUBLANES, SUBLANES, LANES), F32)
                for k in range(CONV_KERNEL):
                    acc = acc + cw_ref[k * SUBLANES:(k + 1) * SUBLANES, cols][None] * _rows(ext, r * rc + lead + k, rc)
                c1 = acc.reshape(rc, LANES) + cb_ref[:, cols]
                c1_ref[cc, rows, :] = c1
                nrm, _ = _group_stats(c1, lo)
                c2 = nrm * gain_ref[:, cols] + bias_ref[:, cols]
                c3_ref[rows, cols] = (c2 * _sigmoid(c2)).astype(BF16)
        ext_ref[:, 0:CONV_HALO, :] = ext_ref[:, tm:tm + CONV_HALO, :]

    return pl.pallas_call(
        body, grid=(s // tm,),
        in_specs=[pl.BlockSpec((tm, CIN_COLS), lambda i: (i, 0)), _resident((CONV_KERNEL * SUBLANES, CONV_WIDTH)),
                  _resident((1, CONV_WIDTH)), _resident((1, CONV_WIDTH)), _resident((1, CONV_WIDTH))] + g_in,
        out_specs=[pl.BlockSpec((tm, CONV_WIDTH), lambda i: (i, 0)), pl.BlockSpec((nchunk, tm, LANES), lambda i: (0, i, 0))] + g_out,
        out_shape=[jax.ShapeDtypeStruct((s, CONV_WIDTH), BF16), jax.ShapeDtypeStruct((nchunk, s, LANES), F32)] + g_shape,
        scratch_shapes=[pltpu.VMEM((nchunk, tm + CONV_HALO, LANES), F32)] + g_scratch,
        compiler_params=_cparams("arbitrary"), name="conv_fwd")(cin, cw8, cb, gain, bias, *shards)


def _out_proj(x, attn, c3, wo_a, wo_c, b_out, g_ffn):
    s = x.shape[0]
    tm = _wide_tile(s)

    def body(x_ref, a_ref, c_ref, wa_ref, wc_ref, b_ref, g_ref, x2_ref, h2_ref):
        x2 = x_ref[...] + _dot(a_ref[...], wa_ref[...]) + _dot(c_ref[...], wc_ref[...]) + b_ref[...]
        x2_ref[...] = x2
        r = lax.rsqrt(jnp.mean(x2 * x2, axis=-1, keepdims=True) + EPS)
        h2_ref[...] = (x2 * r * g_ref[...]).astype(BF16)

    return pl.pallas_call(
        body, grid=(s // tm,),
        in_specs=[pl.BlockSpec((tm, D_MODEL), lambda i: (i, 0)), pl.BlockSpec((tm, Q_COLS), lambda i: (i, 0)),
                  pl.BlockSpec((tm, CONV_WIDTH), lambda i: (i, 0)),
                  pl.BlockSpec((Q_COLS, D_MODEL), lambda i: (0, 0), pipeline_mode=pl.Buffered(1)),
                  pl.BlockSpec((CONV_WIDTH, D_MODEL), lambda i: (1, 0), pipeline_mode=pl.Buffered(1)),
                  _resident((1, D_MODEL)), _resident((1, D_MODEL))],
        out_specs=[pl.BlockSpec((tm, D_MODEL), lambda i: (i, 0)), pl.BlockSpec((tm, D_MODEL), lambda i: (i, 0))],
        out_shape=[jax.ShapeDtypeStruct((s, D_MODEL), F32), jax.ShapeDtypeStruct((s, D_MODEL), BF16)],
        compiler_params=_cparams("parallel"), name="out_proj")(x, attn, c3, wo_a, wo_c, b_out, g_ffn)


FF_LANE_CHUNKS = -(-FF_CHUNK // LANES)
FF_PADDED = FF_LANE_CHUNKS * LANES


def _tap(ref, first_row, n):
    return ref[pl.ds(first_row, n, stride=1), :]


def _ffn_fwd(h2, x2, target, w_up, fw, fb, w_down):
    s = h2.shape[0]
    tm = _tile(s)
    hal = SUBLANES
    rc = min(128, tm)

    def body(h_ref, x2_ref, t_ref, wu_ref, fw_ref, fb_ref, wd_ref, up0_ref, gu_ref, act_ref, dy_ref, dyb_ref, loss_ref,
             ext_ref, carry_ref, act_buf, y_ref):
        i, ci = pl.program_id(0), pl.program_id(1)

        @pl.when((i == 0) & (ci == 0))
        def _():
            carry_ref[...] = jnp.zeros(carry_ref.shape, F32)
            ext_ref[...] = jnp.zeros(ext_ref.shape, F32)
            act_buf[...] = jnp.zeros(act_buf.shape, BF16)
            loss_ref[...] = jnp.zeros((1, 1), F32)

        @pl.when(ci == 0)
        def _():
            y_ref[...] = x2_ref[...]

        ws = (fw_ref[ci], fw_ref[ci + N_FF_PAIRS])
        bs = (fb_ref[ci], fb_ref[ci + N_FF_PAIRS])
        half_rows = (slice(0, tm // 2), slice(tm // 2, tm))
        n_grp = len(FF_COLS)

        def up_slices(grp):
            lo_c, hi_c = FF_COLS[grp]
            chunks = range(lo_c // LANES, -(-hi_c // LANES))

            def make(half, n, rows):
                def run():
                    c = ci + half * N_FF_PAIRS
                    u0 = _dot_nt(h_ref[rows, :], wu_ref[c, lo_c:hi_c, :])
                    up0_ref[half, 0, rows, lo_c:hi_c] = u0.astype(BF16)
                    if hi_c == FF_CHUNK:
                        up0_ref[half, 0, rows, FF_CHUNK:] = jnp.zeros((u0.shape[0], FF_PADDED - FF_CHUNK), BF16)
                    for j in chunks:
                        w = min(LANES, hi_c - j * LANES)
                        if n == 0:
                            ext_ref[half, j, 0:hal, 0:w] = carry_ref[c, :, j * LANES:j * LANES + w]
                        ext_ref[half, j, hal + rows.start:hal + rows.stop, 0:w] = u0[:, j * LANES - lo_c:j * LANES - lo_c + w]
                    if n == len(half_rows) - 1:
                        carry_ref[c, :, lo_c:hi_c] = u0[u0.shape[0] - hal:, :]
                return run
            return [make(half, n, rows) for half in range(2) for n, rows in enumerate(half_rows)]

        def down_slices(grp):
            lo_c, hi_c = FF_COLS[grp]

            def make(rows):
                def run():
                    y_ref[rows, :] += _dot(act_buf[rows, lo_c:hi_c], wd_ref[ci, lo_c:hi_c, :])
                return run
            return [make(rows) for rows in half_rows]

        def vector_blocks(grp):
            lo_c, hi_c = FF_COLS[grp]
            blocks = []
            for j in range(lo_c // LANES, -(-hi_c // LANES)):
                lanes = slice(j * LANES, (j + 1) * LANES)

                def gate(r, lanes=lanes, j=j):
                    base = r * rc
                    ups = []
                    for half in range(2):
                        e, w = ext_ref.at[half, j], ws[half]
                        ups.append(w[0:1, lanes] * _tap(e, base + hal - 2, rc) + w[1:2, lanes] * _tap(e, base + hal - 1, rc)
                                   + w[2:3, lanes] * _tap(e, base + hal, rc) + bs[half][:, lanes])
                    g, u = ups
                    gu_ref[0, 0, base:base + rc, lanes] = g.astype(BF16)
                    gu_ref[1, 0, base:base + rc, lanes] = u.astype(BF16)
                    act_buf[base:base + rc, lanes] = (g * _sigmoid(g) * u).astype(BF16)

                blocks += [functools.partial(gate, r) for r in range(tm // rc)]

            def finish():
                act_ref[0, :, lo_c:hi_c] = act_buf[:, lo_c:hi_c]
            blocks.append(finish)
            return blocks

        for run in up_slices(0):
            run()
        for grp in range(n_grp):
            matmuls = (up_slices(grp + 1) if grp + 1 < n_grp else []) + (down_slices(grp - 1) if grp > 0 else [])
            blocks = vector_blocks(grp)
            every = max(1, len(blocks) // (len(matmuls) + 1))
            for n, run in enumerate(blocks):
                run()
                if n % every == every - 1 and matmuls:
                    matmuls.pop(0)()
            for run in matmuls:
                run()
        for run in down_slices(n_grp - 1):
            run()

        @pl.when(ci == N_FF_PAIRS - 1)
        def _():
            e = y_ref[...] - t_ref[...]
            dy_ref[...] = e * (1.0 / D_MODEL)
            dyb_ref[...] = (e * (1.0 / D_MODEL)).astype(BF16)
            loss_ref[...] += (0.5 / D_MODEL) * jnp.sum(e * e).reshape(1, 1)

    tok = lambda i, ci: (i, 0)
    return pl.pallas_call(
        body, grid=(s // tm, N_FF_PAIRS),
        in_specs=[pl.BlockSpec((tm, D_MODEL), tok), pl.BlockSpec((tm, D_MODEL), tok), pl.BlockSpec((tm, D_MODEL), tok),
                  _resident((N_DEV, FF_CHUNK, D_MODEL)), _resident((N_DEV, 3, FF_PADDED)), _resident((N_DEV, 1, FF_PADDED)),
                  _resident((N_FF_PAIRS, FF_CHUNK, D_MODEL))],
        out_specs=[pl.BlockSpec((2, 1, tm, FF_PADDED), lambda i, ci: (0, ci, i, 0)),
                   pl.BlockSpec((2, 1, tm, FF_PADDED), lambda i, ci: (0, ci, i, 0)),
                   pl.BlockSpec((1, tm, FF_CHUNK), lambda i, ci: (ci, i, 0)),
                   pl.BlockSpec((tm, D_MODEL), tok), pl.BlockSpec((tm, D_MODEL), tok), pl.BlockSpec((1, 1), lambda i, ci: (0, 0))],
        out_shape=[jax.ShapeDtypeStruct((2, N_FF_PAIRS, s, FF_PADDED), BF16), jax.ShapeDtypeStruct((2, N_FF_PAIRS, s, FF_PADDED), BF16),
                   jax.ShapeDtypeStruct((N_FF_PAIRS, s, FF_CHUNK), BF16), jax.ShapeDtypeStruct((s, D_MODEL), F32),
                   jax.ShapeDtypeStruct((s, D_MODEL), BF16), jax.ShapeDtypeStruct((1, 1), F32)],
        scratch_shapes=[pltpu.VMEM((2, FF_LANE_CHUNKS, tm + hal, LANES), F32), pltpu.VMEM((N_DEV, hal, FF_CHUNK), F32),
                        pltpu.VMEM((tm, FF_PADDED), BF16), pltpu.VMEM((tm, D_MODEL), F32)],
        compiler_params=_cparams("arbitrary", "arbitrary"), name="ffn_fwd")(h2, x2, target, w_up, fw, fb, w_down)


def _ffn_bwd(dyb, up0, gu, w_up, fw, w_down):
    s = dyb.shape[0]
    tm = _tile(s)
    nt = s // tm
    nxt = SUBLANES
    rc = min(128, tm)

    def body(dy_ref, up0_ref, gu_ref, wu_ref, fw_ref, wd_ref,
             dup0_ref, dh2_ref, dfw_ref, dfb_ref, dext_ref, carry_ref, dact_buf, dup0_buf, dh2_acc):
        i, ci = pl.program_id(0), pl.program_id(1)

        @pl.when((i == 0) & (ci == 0))
        def _():
            for ref in (carry_ref, dfw_ref, dfb_ref, dext_ref, dact_buf):
                ref[...] = jnp.zeros(ref.shape, F32)
            dup0_buf[...] = jnp.zeros(dup0_buf.shape, BF16)

        @pl.when(ci == 0)
        def _():
            dh2_acc[...] = jnp.zeros(dh2_acc.shape, F32)

        ws = (fw_ref[ci], fw_ref[ci + N_FF_PAIRS])
        fold = lambda v: jnp.sum(v.reshape(rc // SUBLANES, SUBLANES, LANES), axis=0)
        half_rows = (slice(0, tm // 2), slice(tm // 2, tm))
        n_grp = len(FF_COLS)

        def dact_slices(grp):
            lo_c, hi_c = FF_COLS[grp]

            def make(rows):
                def run():
                    dact_buf[rows, lo_c:hi_c] = _dot_nt(dy_ref[rows, :], wd_ref[ci, lo_c:hi_c, :])
                return run
            return [make(rows) for rows in half_rows]

        def dh2_slices(grp):
            lo_c, hi_c = FF_COLS[grp]

            def make(half, rows):
                def run():
                    c = ci + half * N_FF_PAIRS
                    dh2_acc[rows, :] += _dot(dup0_buf[half, rows, lo_c:hi_c], wu_ref[c, lo_c:hi_c, :])
                return run
            return [make(half, rows) for half in range(2) for rows in half_rows]

        def vector_blocks(grp):
            lo_c, hi_c = FF_COLS[grp]
            chunks = range(lo_c // LANES, -(-hi_c // LANES))
            blocks = []

            def stage():
                for half in range(2):
                    c = ci + half * N_FF_PAIRS
                    for j in chunks:
                        dext_ref[half, j, tm:tm + nxt, :] = carry_ref[c, :, j * LANES:(j + 1) * LANES]
            blocks.append(stage)
            for j in chunks:
                lanes = slice(j * LANES, (j + 1) * LANES)
                acc = [jnp.zeros((SUBLANES, LANES), F32)] * 8

                def grads(r, lanes=lanes, j=j, acc=acc):
                    base = r * rc
                    g = gu_ref[0, 0, base:base + rc, lanes].astype(F32)
                    u = gu_ref[1, 0, base:base + rc, lanes].astype(F32)
                    sg = _sigmoid(g)
                    silu = g * sg
                    dact = dact_buf[base:base + rc, lanes]
                    ds = (dact * u * (sg + silu - silu * sg), dact * silu)
                    for half in range(2):
                        dext_ref[half, j, base:base + rc, :] = ds[half]
                        acc[4 * half] = acc[4 * half] + fold(ds[half])

                def conv_back(r, lanes=lanes, j=j, acc=acc):
                    base = r * rc
                    for half in range(2):
                        d, w = dext_ref.at[half, j], ws[half]
                        taps = [_tap(d, base + k, rc) for k in range(3)]
                        dup0 = w[2:3, lanes] * taps[0] + w[1:2, lanes] * taps[1] + w[0:1, lanes] * taps[2]
                        dup0_buf[half, base:base + rc, lanes] = dup0.astype(BF16)
                        u0 = up0_ref[half, 0, base:base + rc, lanes].astype(F32)
                        for k in range(3):
                            acc[4 * half + 1 + k] = acc[4 * half + 1 + k] + fold(taps[2 - k] * u0)

                def sums(lanes=lanes, j=j, acc=acc):
                    for half in range(2):
                        c = ci + half * N_FF_PAIRS
                        carry_ref[c, :, lanes] = dext_ref[half, j, 0:nxt, :]
                        dfb_ref[c, :, lanes] += jnp.sum(acc[4 * half], axis=0, keepdims=True)
                        dfw_ref[c, :, lanes] += jnp.concatenate(
                            [jnp.sum(acc[4 * half + 1 + k], axis=0, keepdims=True) for k in range(3)], axis=0)

                blocks += [functools.partial(grads, r) for r in range(tm // rc)]
                blocks += [functools.partial(conv_back, r) for r in range(tm // rc)] + [sums]

            def finish():
                for half in range(2):
                    dup0_ref[half, 0, :, lo_c:hi_c] = dup0_buf[half, :, lo_c:hi_c]
            blocks.append(finish)
            return blocks

        for run in dact_slices(0):
            run()
        for grp in range(n_grp):
            matmuls = (dact_slices(grp + 1) if grp + 1 < n_grp else []) + (dh2_slices(grp - 1) if grp > 0 else [])
            blocks = vector_blocks(grp)
            every = max(1, len(blocks) // (len(matmuls) + 1))
            for n, run in enumerate(blocks):
                run()
                if n % every == every - 1 and matmuls:
                    matmuls.pop(0)()
            for run in matmuls:
                run()
        for run in dh2_slices(n_grp - 1):
            run()

        @pl.when(ci == N_FF_PAIRS - 1)
        def _():
            dh2_ref[...] = dh2_acc[...].astype(BF16)

    tok = lambda i, ci: (nt - 1 - i, 0)
    acc = lambda shape: pl.BlockSpec(shape, lambda i, ci: (0,) * len(shape))
    saved = pl.BlockSpec((2, 1, tm, FF_PADDED), lambda i, ci: (0, ci, nt - 1 - i, 0))
    return pl.pallas_call(
        body, grid=(nt, N_FF_PAIRS),
        in_specs=[pl.BlockSpec((tm, D_MODEL), tok), saved, saved,
                  _resident((N_DEV, FF_CHUNK, D_MODEL)), _resident((N_DEV, 3, FF_PADDED)),
                  _resident((N_FF_PAIRS, FF_CHUNK, D_MODEL))],
        out_specs=[pl.BlockSpec((2, 1, tm, FF_CHUNK), lambda i, ci: (0, ci, nt - 1 - i, 0)),
                   pl.BlockSpec((tm, D_MODEL), tok), acc((N_DEV, 3, FF_PADDED)), acc((N_DEV, 1, FF_PADDED))],
        out_shape=[jax.ShapeDtypeStruct((2, N_FF_PAIRS, s, FF_CHUNK), BF16), jax.ShapeDtypeStruct((s, D_MODEL), BF16),
                   jax.ShapeDtypeStruct((N_DEV, 3, FF_PADDED), F32), jax.ShapeDtypeStruct((N_DEV, 1, FF_PADDED), F32)],
        scratch_shapes=[pltpu.VMEM((2, FF_LANE_CHUNKS, tm + nxt, LANES), F32), pltpu.VMEM((N_DEV, nxt, FF_PADDED), F32),
                        pltpu.VMEM((tm, FF_PADDED), F32), pltpu.VMEM((2, tm, FF_PADDED), BF16), pltpu.VMEM((tm, D_MODEL), F32)],
        compiler_params=_cparams("arbitrary", "arbitrary"), name="ffn_bwd")(dyb, up0, gu, w_up, fw, w_down)


def _ffn_norm_bwd(dh2, dy, x2, g_ffn, w_out):
    s = dy.shape[0]
    tm = _wide_tile(s)

    def body(dh_ref, dy_ref, x2_ref, g_ref, wo_ref, dx2_ref, dmix_ref, dg_ref, dbo_ref):
        @pl.when(pl.program_id(0) == 0)
        def _():
            dg_ref[...] = jnp.zeros(dg_ref.shape, F32)
            dbo_ref[...] = jnp.zeros(dbo_ref.shape, F32)

        x2v = x2_ref[...]
        r = lax.rsqrt(jnp.mean(x2v * x2v, axis=-1, keepdims=True) + EPS)
        n2 = x2v * r
        dh2 = dh_ref[...].astype(F32)
        dg_ref[...] += jnp.sum(dh2 * n2, axis=0, keepdims=True)
        dn = dh2 * g_ref[...]
        dx2 = dy_ref[...] + r * (dn - n2 * jnp.mean(dn * n2, axis=-1, keepdims=True))
        dx2_ref[...] = dx2
        dbo_ref[...] += jnp.sum(dx2, axis=0, keepdims=True)
        dmix_ref[...] = _dot_nt(dx2.astype(BF16), wo_ref[...]).astype(BF16)

    tok = pl.BlockSpec((tm, D_MODEL), lambda i: (i, 0))
    vec = pl.BlockSpec((1, D_MODEL), lambda i: (0, 0))
    return pl.pallas_call(
        body, grid=(s // tm,),
        in_specs=[tok, tok, tok, _resident((1, D_MODEL)), _resident((D_MODEL, D_MODEL))],
        out_specs=[tok, tok, vec, vec],
        out_shape=[jax.ShapeDtypeStruct((s, D_MODEL), F32), jax.ShapeDtypeStruct((s, D_MODEL), BF16),
                   jax.ShapeDtypeStruct((1, D_MODEL), F32), jax.ShapeDtypeStruct((1, D_MODEL), F32)],
        compiler_params=_cparams("arbitrary"), name="ffn_norm_bwd")(dh2, dy, x2, g_ffn, w_out)


def _conv_bwd(dmixed, c1, cin, cw8, gain, bias, g8s):
    ns = len(g8s)
    s_in, s_out, s_shape, s_scratch = _scatter_specs(g8s)
    s = cin.shape[0]
    tm = _tile(s)
    nt = s // tm
    rc = 64
    rn = min(256, tm)
    hal = CONV_HALO
    nchunk = CONV_WIDTH // LANES

    def body(dc3_ref, dc3n_ref, c1_ref, c1n_ref, cin_ref, cw_ref, gain_ref, bias_ref, *rest):
        dcin_ref, dcw_ref, dcb_ref, dgain_ref, dbias_ref, dbcin_ref = rest[ns:ns + 6]
        dc1_ext, dcw8 = rest[2 * ns + 6:2 * ns + 8]
        i = pl.program_id(0)
        first, last = i == 0, i == nt - 1
        own = rest[2 * ns + 8:]
        _run_scatters([_ReduceScatter(rest[a], rest[ns + 6 + a], *own[N_SCATTER_SCRATCH * a:N_SCATTER_SCRATCH * (a + 1)])
                       for a in range(ns)], i, nt)

        @pl.when(first)
        def _():
            for ref in (dcw8, dcb_ref, dgain_ref, dbias_ref, dbcin_ref):
                ref[...] = jnp.zeros(ref.shape, F32)

        lo = _lo_mask((1, LANES))

        def norm_bwd(dc3, c1v, cols):
            nrm, rstd = _group_stats(c1v, lo)
            c2 = nrm * gain_ref[:, cols] + bias_ref[:, cols]
            sg = _sigmoid(c2)
            dc2 = dc3 * (sg * (1.0 + c2 * (1.0 - sg)))
            dn = dc2 * gain_ref[:, cols]
            inv = 1.0 / HEAD_DIM
            dc1 = rstd * (dn - _half_sums(dn, lo) * inv - nrm * (_half_sums(dn * nrm, lo) * inv))
            return dc1, dc2, nrm

        def row_sum(v):
            return jnp.sum(v, axis=0, keepdims=True)

        for cc in range(nchunk):
            cols = slice(cc * LANES, (cc + 1) * LANES)
            gcols = slice(CONV_WIDTH + cc * LANES, CONV_WIDTH + (cc + 1) * LANES)
            d1e = dc1_ext.at[cc]
            dc1n, _, _ = norm_bwd(dc3n_ref[:, cols].astype(F32), c1n_ref[cc], cols)
            d1e[tm:tm + hal, :] = jnp.where(last, 0.0, dc1n)

            for r in range(tm // rn):
                rows = slice(r * rn, (r + 1) * rn)
                dc1, dc2, nrm = norm_bwd(dc3_ref[rows, cols].astype(F32), c1_ref[cc, rows, :], cols)
                d1e[rows, :] = dc1
                dgain_ref[:, cols] += row_sum(dc2 * nrm)
                dbias_ref[:, cols] += row_sum(dc2)
                dcb_ref[:, cols] += row_sum(dc1)
            zero = jnp.zeros((1, LANES), F32)

            def taps(r, sums):
                rows = pl.ds(pl.multiple_of(r * rc, rc), rc)
                a = cin_ref[rows, cols]
                sg = _sigmoid(cin_ref[rows, gcols])
                c0 = (a * sg).reshape(rc // SUBLANES, SUBLANES, LANES)
                dc0 = jnp.zeros((rc // SUBLANES, SUBLANES, LANES), F32)
                for k in range(CONV_KERNEL):
                    krows = slice(k * SUBLANES, (k + 1) * SUBLANES)
                    shifted = _rows(d1e, r * rc + CONV_KERNEL - 1 - k, rc)
                    dc0 = dc0 + cw_ref[krows, cols][None] * shifted
                    dcw8[krows, cols] += jnp.sum(shifted * c0, axis=0)
                dc0 = dc0.reshape(rc, LANES)
                da = dc0 * sg
                dgate = dc0 * a * sg * (1.0 - sg)
                dcin_ref[rows, cols] = da.astype(BF16)
                dcin_ref[rows, gcols] = dgate.astype(BF16)
                return sums[0] + row_sum(da), sums[1] + row_sum(dgate)

            sums = lax.fori_loop(0, tm // rc, taps, (zero, zero))
            dbcin_ref[:, cols] += sums[0]
            dbcin_ref[:, gcols] += sums[1]

        @pl.when(last)
        def _():
            for k in range(CONV_KERNEL):
                dcw_ref[k:k + 1, :] = jnp.sum(dcw8[k * SUBLANES:(k + 1) * SUBLANES, :], axis=0, keepdims=True)

    nh = tm // hal
    acc = lambda shape: pl.BlockSpec(shape, lambda i: (0,) * len(shape))
    return pl.pallas_call(
        body, grid=(nt,),
        in_specs=[pl.BlockSpec((tm, CONV_WIDTH), lambda i: (i, 1)),
                  pl.BlockSpec((hal, CONV_WIDTH), lambda i: (jnp.minimum((i + 1) * nh, s // hal - 1), 1)),
                  pl.BlockSpec((nchunk, tm, LANES), lambda i: (0, i, 0)),
                  pl.BlockSpec((nchunk, hal, LANES), lambda i: (0, jnp.minimum((i + 1) * nh, s // hal - 1), 0)),
                  pl.BlockSpec((tm, CIN_COLS), lambda i: (i, 0)),
                  _resident((CONV_KERNEL * SUBLANES, CONV_WIDTH)), _resident((1, CONV_WIDTH)), _resident((1, CONV_WIDTH))] + s_in,
        out_specs=[pl.BlockSpec((tm, CIN_COLS), lambda i: (i, 0)), acc((CONV_KERNEL, CONV_WIDTH)), acc((1, CONV_WIDTH)),
                   acc((1, CONV_WIDTH)), acc((1, CONV_WIDTH)), acc((1, CIN_COLS))] + s_out,
        out_shape=[jax.ShapeDtypeStruct((s, CIN_COLS), BF16), jax.ShapeDtypeStruct((CONV_KERNEL, CONV_WIDTH), F32),
                   jax.ShapeDtypeStruct((1, CONV_WIDTH), F32), jax.ShapeDtypeStruct((1, CONV_WIDTH), F32),
                   jax.ShapeDtypeStruct((1, CONV_WIDTH), F32), jax.ShapeDtypeStruct((1, CIN_COLS), F32)] + s_shape,
        scratch_shapes=[pltpu.VMEM((nchunk, tm + hal, LANES), F32),
                        pltpu.VMEM((CONV_KERNEL * SUBLANES, CONV_WIDTH), F32)] + s_scratch,
        compiler_params=_cparams("arbitrary"), name="conv_bwd")(dmixed, dmixed, c1, c1, cin, cw8, gain, bias, *g8s)


def _attn_bwd(qkv, dmixed, gq2, gk2, sinks, g8s):
    ns = len(g8s)
    s_in, s_out, s_shape, s_scratch = _scatter_specs(g8s)
    s = qkv.shape[0]
    tq = _tile(s)
    nb = tq // ATT_BLOCK
    nt = s // tq

    def body(q_ref, kv_ref, kvp_ref, do_ref, gq_ref, gk_ref, sink_ref, *rest):
        dqkv_ref, dgq_ref, dgk_ref, dsink_ref, dbqkv_ref = rest[ns:ns + 5]
        dk_acc, dv_acc, carry_k, carry_v = rest[2 * ns + 5:2 * ns + 9]
        i = pl.program_id(0)
        t = nt - 1 - i
        own = rest[2 * ns + 9:]
        _run_scatters([_ReduceScatter(rest[a], rest[ns + 5 + a], *own[N_SCATTER_SCRATCH * a:N_SCATTER_SCRATCH * (a + 1)])
                       for a in range(ns)], i, nt)

        @pl.when(i == 0)
        def _():
            for ref in (carry_k, carry_v, dgq_ref, dgk_ref, dsink_ref, dbqkv_ref):
                ref[...] = jnp.zeros(ref.shape, F32)

        lo = _lo_mask((1, LANES))
        lane_id = lax.broadcasted_iota(jnp.int32, (1, LANES), 1)
        kv_all = jnp.concatenate([kvp_ref[...], kv_ref[...]], axis=0)
        k_lo, k_hi, v_lo, v_hi, kn_pre, rk = _kv_variants(kv_all, gk_ref[...], lo)
        for acc_ref, carry in ((dk_acc, carry_k), (dv_acc, carry_v)):
            acc_ref[:, 0:tq, :] = jnp.zeros((N_KV_HEADS, tq, LANES), F32)
            acc_ref[:, tq:tq + ATT_BLOCK, :] = carry[...]
        dsink = jnp.zeros((1, LANES), F32)
        dgq = jnp.zeros((1, LANES), F32)
        gq = gq_ref[...]
        for b in range(nb):
            rel, valid = _att_consts(t == 0, b)
            rows = slice(b * ATT_BLOCK, (b + 1) * ATT_BLOCK)
            keys = slice(b * ATT_BLOCK, (b + 2) * ATT_BLOCK)
            for kvh in range(N_KV_HEADS):
                pairs = (2 * kvh, 2 * kvh + 1)
                q_raw = jnp.concatenate([q_ref[rows, p * LANES:(p + 1) * LANES] for p in pairs], axis=0)
                qn_pre, rq = _head_norm(q_raw, lo)
                q2 = (qn_pre * gq).astype(BF16)
                do2 = jnp.concatenate([do_ref[rows, p * LANES:(p + 1) * LANES] for p in pairs], axis=0).astype(BF16)
                dq2 = jnp.zeros((2 * ATT_BLOCK, LANES), F32)
                for odd, (k_op, v_op) in enumerate(((k_lo[kvh][keys], v_lo[kvh][keys]), (k_hi[kvh][keys], v_hi[kvh][keys]))):
                    ha, hb = 2 * pairs[0] + odd, 2 * pairs[1] + odd
                    p, p_sink = _probs(q2, k_op, rel, valid, _row_const(SLOPES[ha], SLOPES[hb]),
                                       _row_const(sink_ref[ha], sink_ref[hb]))
                    dp = _dot_nt(do2, v_op)
                    delta = jnp.sum(p * dp, axis=-1, keepdims=True)
                    ds = (p * (dp - delta) * (1.0 / math.sqrt(HEAD_DIM))).astype(BF16)
                    dsk = p_sink * delta
                    dsink = dsink - jnp.where(lane_id == ha, jnp.sum(dsk[0:ATT_BLOCK]), 0.0) \
                        - jnp.where(lane_id == hb, jnp.sum(dsk[ATT_BLOCK:]), 0.0)
                    dq2 = dq2 + _dot(ds, k_op)
                    half = lo if odd == 0 else jnp.logical_not(lo)
                    dk_acc[kvh, keys, :] += jnp.where(half, _dot_tn(ds, q2), 0.0)
                    dv_acc[kvh, keys, :] += jnp.where(half, _dot_tn(p.astype(BF16), do2), 0.0)
                dgq = dgq + jnp.sum(dq2 * qn_pre, axis=0, keepdims=True)
                dq_raw = _head_norm_bwd(dq2 * gq, qn_pre, rq, lo)
                for n, p_ in enumerate(pairs):
                    blk = dq_raw[n * ATT_BLOCK:(n + 1) * ATT_BLOCK]
                    dqkv_ref[rows, p_ * LANES:(p_ + 1) * LANES] = blk.astype(BF16)
                    dbqkv_ref[:, p_ * LANES:(p_ + 1) * LANES] += jnp.sum(blk, axis=0, keepdims=True)
        carry_k[...] = dk_acc[:, 0:ATT_BLOCK, :]
        carry_v[...] = dv_acc[:, 0:ATT_BLOCK, :]

        def fold(acc_ref):
            both = []
            for kvh in range(N_KV_HEADS):
                a = acc_ref[kvh, ATT_BLOCK:ATT_BLOCK + tq, :]
                both.append(a + pltpu.roll(a, HEAD_DIM, 1))
            return jnp.where(lo, both[0], both[1])

        dkn = fold(dk_acc)
        dv = fold(dv_acc)
        kn_c, rk_c = kn_pre[ATT_BLOCK:], rk[ATT_BLOCK:]
        dgk_ref[...] += jnp.sum(dkn * kn_c, axis=0, keepdims=True)
        dk_raw = _head_norm_bwd(dkn * gk_ref[...], kn_c, rk_c, lo)
        dqkv_ref[:, Q_COLS:Q_COLS + KV_COLS] = dk_raw.astype(BF16)
        dqkv_ref[:, Q_COLS + KV_COLS:] = dv.astype(BF16)
        dbqkv_ref[:, Q_COLS:Q_COLS + KV_COLS] += jnp.sum(dk_raw, axis=0, keepdims=True)
        dbqkv_ref[:, Q_COLS + KV_COLS:] += jnp.sum(dv, axis=0, keepdims=True)
        dgq_ref[...] += dgq
        dsink_ref[...] += dsink

        @pl.when(i == nt - 1)
        def _():
            for ref in (dgq_ref, dgk_ref):
                v = ref[...]
                ref[...] = v + pltpu.roll(v, HEAD_DIM, 1)

    acc = lambda shape: pl.BlockSpec(shape, lambda i: (0,) * len(shape))
    return pl.pallas_call(
        body, grid=(nt,),
        in_specs=[pl.BlockSpec((tq, Q_COLS), lambda i: (nt - 1 - i, 0)),
                  pl.BlockSpec((tq, 2 * KV_COLS), lambda i: (nt - 1 - i, 2)),
                  pl.BlockSpec((ATT_BLOCK, 2 * KV_COLS), lambda i: (jnp.maximum((nt - 1 - i) * nb - 1, 0), 2)),
                  pl.BlockSpec((tq, Q_COLS), lambda i: (nt - 1 - i, 0)),
                  _resident((1, LANES)), _resident((1, LANES)), pl.BlockSpec(memory_space=pltpu.SMEM)] + s_in,
        out_specs=[pl.BlockSpec((tq, QKV_COLS), lambda i: (nt - 1 - i, 0)), acc((1, LANES)), acc((1, LANES)),
                   acc((1, LANES)), acc((1, QKV_COLS))] + s_out,
        out_shape=[jax.ShapeDtypeStruct((s, QKV_COLS), BF16), jax.ShapeDtypeStruct((1, LANES), F32),
                   jax.ShapeDtypeStruct((1, LANES), F32), jax.ShapeDtypeStruct((1, LANES), F32),
                   jax.ShapeDtypeStruct((1, QKV_COLS), F32)] + s_shape,
        scratch_shapes=[pltpu.VMEM((N_KV_HEADS, tq + ATT_BLOCK, LANES), F32), pltpu.VMEM((N_KV_HEADS, tq + ATT_BLOCK, LANES), F32),
                        pltpu.VMEM((N_KV_HEADS, ATT_BLOCK, LANES), F32), pltpu.VMEM((N_KV_HEADS, ATT_BLOCK, LANES), F32)] + s_scratch,
        compiler_params=_cparams("arbitrary"), name="attn_bwd")(qkv, qkv, qkv, dmixed, gq2, gk2, sinks, *g8s)


def _in_bwd(dqkv, dcin, w_in_t, x, dx2, g_mix):
    s = x.shape[0]
    tm = _wide_tile(s)

    def body(dq_ref, dc_ref, w_ref, x_ref, dx2_ref, g_ref, gx_ref, dg_ref):
        @pl.when(pl.program_id(0) == 0)
        def _():
            dg_ref[...] = jnp.zeros(dg_ref.shape, F32)

        dh = _dot(dq_ref[...], w_ref[0:QKV_COLS, :]) + _dot(dc_ref[...], w_ref[QKV_COLS:, :])
        xv = x_ref[...]
        r = lax.rsqrt(jnp.mean(xv * xv, axis=-1, keepdims=True) + EPS)
        n = xv * r
        dg_ref[...] += jnp.sum(dh * n, axis=0, keepdims=True)
        dn = dh * g_ref[...]
        gx_ref[...] = dx2_ref[...] + r * (dn - n * jnp.mean(dn * n, axis=-1, keepdims=True))

    return pl.pallas_call(
        body, grid=(s // tm,),
        in_specs=[pl.BlockSpec((tm, QKV_COLS), lambda i: (i, 0)), pl.BlockSpec((tm, CIN_COLS), lambda i: (i, 0)),
                  _resident((QKV_COLS + CIN_COLS, D_MODEL)),
                  pl.BlockSpec((tm, D_MODEL), lambda i: (i, 0)), pl.BlockSpec((tm, D_MODEL), lambda i: (i, 0)),
                  _resident((1, D_MODEL))],
        out_specs=[pl.BlockSpec((tm, D_MODEL), lambda i: (i, 0)), pl.BlockSpec((1, D_MODEL), lambda i: (0, 0))],
        out_shape=[jax.ShapeDtypeStruct((s, D_MODEL), F32), jax.ShapeDtypeStruct((1, D_MODEL), F32)],
        compiler_params=_cparams("arbitrary"), name="in_bwd")(dqkv, dcin, w_in_t, x, dx2, g_mix)


def _tn_matmul(a, b, name, tokens):
    ga, s, m = a.shape
    gb, _, n = b.shape
    g = max(ga, gb)
    tk = min(tokens, s)

    def body(a_ref, b_ref, o_ref):
        @pl.when(pl.program_id(1) == 0)
        def _():
            o_ref[...] = jnp.zeros(o_ref.shape, F32)

        o_ref[0] += _dot_tn(a_ref[0].astype(BF16), b_ref[0].astype(BF16))

    return pl.pallas_call(
        body, grid=(g, s // tk),
        in_specs=[pl.BlockSpec((1, tk, m), (lambda gi, k: (gi, k, 0)) if ga > 1 else (lambda gi, k: (0, k, 0))),
                  pl.BlockSpec((1, tk, n), (lambda gi, k: (gi, k, 0)) if gb > 1 else (lambda gi, k: (0, k, 0)))],
        out_specs=pl.BlockSpec((1, m, n), lambda gi, k: (gi, 0, 0)),
        out_shape=jax.ShapeDtypeStruct((g, m, n), F32),
        compiler_params=_cparams("parallel", "arbitrary"), name=name)(a, b)


def _tn_matmul_pair(a0, a1, b, name):
    s, m0 = a0.shape
    m1, n = a1.shape[1], b.shape[1]
    tk = min(TN_TOKENS, s)

    def body(a0_ref, a1_ref, b_ref, o_ref):
        @pl.when(pl.program_id(0) == 0)
        def _():
            o_ref[...] = jnp.zeros(o_ref.shape, F32)

        bv = b_ref[...].astype(BF16)
        o_ref[0:m0, :] += _dot_tn(a0_ref[...].astype(BF16), bv)
        o_ref[m0:, :] += _dot_tn(a1_ref[...].astype(BF16), bv)

    return pl.pallas_call(
        body, grid=(s // tk,),
        in_specs=[pl.BlockSpec((tk, m0), lambda k: (k, 0)), pl.BlockSpec((tk, m1), lambda k: (k, 0)),
                  pl.BlockSpec((tk, n), lambda k: (k, 0))],
        out_specs=pl.BlockSpec((m0 + m1, n), lambda k: (0, 0)),
        out_shape=jax.ShapeDtypeStruct((m0 + m1, n), F32),
        compiler_params=_cparams("arbitrary"), name=name)(a0, a1, b)


def _allgather(shards, dtypes):
    n = len(shards)
    n_copies = 1 + 2 * len(OTHER_CHIPS)

    def body(*refs):
        ins, outs = refs[:n], refs[n:2 * n]
        send_sems, recv_sems = refs[2 * n:]
        x, y, c = _position()
        me, sibling = (x, y, c), (x, y, 1 - c)
        chips = [(_flip(x, fx), _flip(y, fy)) for fx, fy in OTHER_CHIPS]
        for a in range(n):
            outs[a][_dev_index(*me)] = ins[a][...].astype(dtypes[a])

        def copy(a, k, block, to):
            rows = outs[a].at[_dev_index(*block)]
            return pltpu.make_async_remote_copy(src_ref=rows, dst_ref=rows, send_sem=send_sems.at[a, k],
                                                recv_sem=recv_sems.at[a, k], device_id=to, device_id_type=MESH)

        started = []
        for a in range(n):
            for j, chip in enumerate(chips):
                started.append(copy(a, 1 + j, me, (*chip, c)))
            started.append(copy(a, 0, me, sibling))
        for cp in started:
            cp.start()
        for a in range(n):
            for j, chip in enumerate(chips):
                copy(a, 1 + j, (*chip, c), me).wait_recv()
                fwd = copy(a, 1 + len(chips) + j, (*chip, c), sibling)
                fwd.start()
                started.append(fwd)
        for a in range(n):
            copy(a, 0, sibling, me).wait_recv()
            for j, chip in enumerate(chips):
                copy(a, 1 + len(chips) + j, (*chip, 1 - c), me).wait_recv()
        for cp in started:
            cp.wait_send()

    vmem = pl.BlockSpec(memory_space=pltpu.VMEM)
    return pl.pallas_call(
        body, in_specs=[vmem] * n, out_specs=[vmem] * n,
        out_shape=[jax.ShapeDtypeStruct((N_DEV,) + w.shape, dt) for w, dt in zip(shards, dtypes)],
        scratch_shapes=[pltpu.SemaphoreType.DMA((n, n_copies)), pltpu.SemaphoreType.DMA((n, n_copies))],
        compiler_params=pltpu.CompilerParams(vmem_limit_bytes=VMEM_LIMIT), name="allgather_weights")(*shards)


def _final_exchange(g8, v):
    rows = v.shape[0]
    n_chips = 1 + len(OTHER_CHIPS)
    _, _, s_shape, s_scratch = _scatter_specs([g8])

    def body(g_ref, v_ref, gout_ref, vout_ref, from_sibling, chip_sums, send_sems, recv_sems, *rs_scratch):
        scatter = _ReduceScatter(g_ref, gout_ref, *rs_scratch)
        x, y, c = _position()
        my_chip = 2 * x + y
        chips = [(_flip(x, fx), _flip(y, fy)) for fx, fy in OTHER_CHIPS]

        def swap():
            return pltpu.make_async_remote_copy(src_ref=v_ref, dst_ref=from_sibling, send_sem=send_sems.at[0],
                                                recv_sem=recv_sems.at[0], device_id=(x, y, 1 - c), device_id_type=MESH)

        def push(j):
            return pltpu.make_async_remote_copy(src_ref=chip_sums.at[my_chip], dst_ref=chip_sums.at[my_chip],
                                                send_sem=send_sems.at[1 + j], recv_sem=recv_sems.at[1 + j],
                                                device_id=(*chips[j], c), device_id_type=MESH)

        swap().start()
        scatter.start()
        swap().wait_recv()
        chip_sums[my_chip] = v_ref[...] + from_sibling[...]
        for j in range(len(chips)):
            push(j).start()
        scatter.middle()
        for j in range(len(chips)):
            push(j).wait_recv()
        swap().wait_send()
        for j in range(len(chips)):
            push(j).wait_send()
        total = chip_sums[0]
        for q in range(1, n_chips):
            total = total + chip_sums[q]
        vout_ref[...] = total
        scatter.finish()

    vmem = pl.BlockSpec(memory_space=pltpu.VMEM)
    return pl.pallas_call(
        body, in_specs=[pl.BlockSpec(memory_space=pl.ANY), vmem], out_specs=[vmem, vmem],
        out_shape=s_shape + [jax.ShapeDtypeStruct((rows, LANES), F32)],
        scratch_shapes=[pltpu.VMEM((rows, LANES), F32), pltpu.VMEM((n_chips, rows, LANES), F32),
                        pltpu.SemaphoreType.DMA((n_chips,)), pltpu.SemaphoreType.DMA((n_chips,))] + s_scratch,
        compiler_params=pltpu.CompilerParams(vmem_limit_bytes=VMEM_LIMIT), name="final_exchange")(g8, v)


def _adam_math(wv, gv, mv, vv):
    mn = ADAM_B1 * mv + (1.0 - ADAM_B1) * gv
    vn = ADAM_B2 * vv + (1.0 - ADAM_B2) * (gv * gv)
    m_hat = mn / (1.0 - ADAM_B1 ** ADAM_STEP)
    v_hat = vn / (1.0 - ADAM_B2 ** ADAM_STEP)
    return -ADAM_LR * (m_hat / (jnp.sqrt(v_hat) + ADAM_EPS) + ADAM_WD * wv), mn, vn


ADAM_ROW_BLOCKS = 4


def _adamw(ws, gs, ms, vs):
    n = len(ws)
    assert all(w.shape[0] % (ADAM_ROW_BLOCKS * SUBLANES) == 0 for w in ws)

    def body(*refs):
        ins, outs = refs[:4 * n], refs[4 * n:]
        for i in range(n):
            w_ref, g_ref, m_ref, v_ref = ins[4 * i:4 * i + 4]
            d_ref, mo_ref, vo_ref = outs[3 * i:3 * i + 3]
            d_ref[...], mo_ref[...], vo_ref[...] = _adam_math(w_ref[...], g_ref[...], m_ref[...], v_ref[...])

    specs = [pl.BlockSpec((w.shape[0] // ADAM_ROW_BLOCKS, w.shape[1]), lambda i: (i, 0)) for w in ws]
    outs = pl.pallas_call(
        body, grid=(ADAM_ROW_BLOCKS,), in_specs=[s for s in specs for _ in range(4)],
        out_specs=[s for s in specs for _ in range(3)],
        out_shape=[jax.ShapeDtypeStruct(w.shape, F32) for w in ws for _ in range(3)],
        compiler_params=_cparams("arbitrary"), name="adamw_big")(*[a for four in zip(ws, gs, ms, vs) for a in four])
    return [tuple(outs[3 * i:3 * i + 3]) for i in range(n)]


FW_ROWS = 24
CW_ROWS = 32
R_FW = 0
R_FB = R_FW + N_DEV * FW_ROWS
R_CW = R_FB + 48
R_BQKV = R_CW + (CONV_WIDTH // LANES) * CW_ROWS
R_BCIN = R_BQKV + 8
R_GMIX = R_BCIN + 8
R_BOUT = R_GMIX + 8
R_GFFN = R_BOUT + 8
R_CB = R_GFFN + 8
R_CGAIN = R_CB + 8
R_CBIAS = R_CGAIN + 8
R_QKS = R_CBIAS + 8
SMALL_ROWS = R_QKS + 8


def _pack_small(raw):
    def rows(a, n):
        a = a.reshape(-1, LANES)
        return jnp.pad(a, ((0, n - a.shape[0]), (0, 0)))

    fw = jnp.pad(raw["dfw"].reshape(N_DEV, -1, LANES), ((0, 0), (0, FW_ROWS - 3 * FF_LANE_CHUNKS), (0, 0)))
    cw = jnp.pad(raw["dcw"].reshape(CONV_KERNEL, -1, LANES).transpose(1, 0, 2), ((0, 0), (0, CW_ROWS - CONV_KERNEL), (0, 0)))
    qks = jnp.concatenate([raw["dgq"], raw["dgk"], raw["dsink"], jnp.pad(raw["loss"], ((0, 0), (0, LANES - 1)))], axis=0)
    return jnp.concatenate([
        fw.reshape(-1, LANES), rows(raw["dfb"][:, 0, :FF_CHUNK], 48), cw.reshape(-1, LANES), rows(raw["dbqkv"], 8),
        rows(raw["dbcin"], 8), rows(raw["dg_mix"], 8), rows(raw["db_out"], 8), rows(raw["dg_ffn"], 8), rows(raw["dcb"], 8),
        rows(raw["dcgain"], 8), rows(raw["dcbias"], 8), rows(qks, 8)], axis=0)


def _adamw_small(gpack, w, m, v):
    n = len(SMALL)
    ix = {name: i for i, name in enumerate(SMALL)}

    def body(g_ref, *refs):
        w_refs, m_refs, v_refs, outs = refs[:n], refs[n:2 * n], refs[2 * n:3 * n], refs[3 * n:]
        d = _dev_index(*_position())

        def step(name, idx, gv):
            i = ix[name]
            delta, mn, vn = _adam_math(w_refs[i][idx], gv, m_refs[i][idx], v_refs[i][idx])
            for ref, val in zip(outs[4 * i:4 * i + 4], (gv, delta, mn, vn)):
                ref[idx] = val

        def whole(name, row, nrows):
            step(name, (slice(None), slice(None)), g_ref[row:row + nrows, :])

        whole("mix_norm_gain", R_GMIX, 8)
        whole("b_out", R_BOUT, 8)
        whole("ffn_norm_gain", R_GFFN, 8)
        whole("conv_dw_b", R_CB, 4)
        whole("conv_norm_gain", R_CGAIN, 4)
        whole("conv_norm_bias", R_CBIAS, 4)
        whole("ffn_dw_b", R_FB, 2 * D_FF // LANES)
        nq = QKV_COLS // LANES
        step("b_in", (slice(0, nq), slice(None)), g_ref[R_BQKV:R_BQKV + nq, :])
        step("b_in", (slice(nq, nq + CIN_COLS // LANES), slice(None)), g_ref[R_BCIN:R_BCIN + CIN_COLS // LANES, :])
        step("q_norm_gain", (slice(None), slice(None)), g_ref[R_QKS:R_QKS + 1, 0:HEAD_DIM])
        step("k_norm_gain", (slice(None), slice(None)), g_ref[R_QKS + 1:R_QKS + 2, 0:HEAD_DIM])
        step("attn_sinks", (slice(None), slice(None)), g_ref[R_QKS + 2:R_QKS + 3, 0:N_Q_HEADS])
        blk = g_ref[pl.ds(pl.multiple_of(R_CW + CW_ROWS * lax.shift_right_logical(d, 1), SUBLANES), CW_ROWS), :]
        blk = jnp.where((d & 1) == 1, pltpu.roll(blk, HEAD_DIM, 1), blk)
        step("conv_dw_w", (slice(None), slice(None)), blk[0:CONV_KERNEL, 0:CONV_WIDTH // N_DEV])
        blk = g_ref[pl.ds(pl.multiple_of(R_FW + FW_ROWS * d, SUBLANES), FW_ROWS), :]
        for k in range(3):
            for j in range(FF_LANE_CHUNKS):
                wd = min(LANES, FF_CHUNK - j * LANES)
                row = k * FF_LANE_CHUNKS + j
                step("ffn_dw_w", (slice(k, k + 1), slice(j * LANES, j * LANES + wd)), blk[row:row + 1, 0:wd])

    vmem = pl.BlockSpec(memory_space=pltpu.VMEM)
    args = [gpack] + [d[name] for d in (w, m, v) for name in SMALL]
    outs = pl.pallas_call(
        body, in_specs=[vmem] * len(args), out_specs=[vmem] * (4 * n),
        out_shape=[jax.ShapeDtypeStruct(w[name].shape, F32) for name in SMALL for _ in range(4)],
        compiler_params=pltpu.CompilerParams(vmem_limit_bytes=VMEM_LIMIT), name="adamw_small")(*args)
    return {name: outs[4 * i:4 * i + 4] for i, name in enumerate(SMALL)}


def _token_mixing(x, p, attn_shards, conv_shards):
    qkv, cin, h1 = _mix_proj(x, p["g_mix"], p["w_in_t"], p["b_qkv"], p["b_cin"])
    attn, *from_attn = _attn_fwd(qkv, p["gq2"], p["gk2"], p["sinks"], attn_shards)
    c3, c1, *from_conv = _conv_fwd(cin, p["cw8"], p["cb"], p["cgain"], p["cbias"], conv_shards)
    return (qkv, cin, h1, attn, c3, c1), from_attn, from_conv


def _rest_of_step(x, target, p, saved, scatter):
    s = x.shape[0]
    qkv, cin, h1, attn, c3, c1 = saved
    cw8, w_out, w_up, w_down = p["cw8"], p["w_out"], p["w_up"], p["w_down"]
    x2, h2 = _out_proj(x, attn, c3, w_out, w_out, p["b_out"], p["g_ffn"])
    fw, fb = p["fw"], p["fb"]
    up0, gu, act, dy, dyb, loss = _ffn_fwd(h2, x2, target, w_up, fw, fb, w_down)
    dup0, dh2, dfw, dfb = _ffn_bwd(dyb, up0, gu, w_up, fw, w_down)
    dx2, dmixed, dg_ffn, db_out = _ffn_norm_bwd(dh2, dy, x2, p["g_ffn"], w_out)
    dw_up = _tn_matmul(dup0.reshape(N_DEV, s, FF_CHUNK), h2[None], "dw_up", 2 * TN_TOKENS)
    dw_down = _tn_matmul(act, dyb[None], "dw_down", 2 * TN_TOKENS).reshape(N_DEV, -1, D_MODEL)
    dw_out = _tn_matmul_pair(attn, c3, dx2, "dw_out").reshape(N_DEV, -1, D_MODEL)
    dcin, dcw, dcb, dcgain, dcbias, dbcin, *g_up = _conv_bwd(dmixed, c1, cin, cw8, p["cgain"], p["cbias"], [dw_up] if scatter else [])
    dqkv, dgq, dgk, dsink, dbqkv, *g_down_out = _attn_bwd(qkv, dmixed, p["gq2"], p["gk2"], p["sinks"],
                                                          [dw_down, dw_out] if scatter else [])
    dw_in = _tn_matmul_pair(dqkv, dcin, h1, "dw_in").reshape(N_DEV, -1, D_MODEL)
    grad_x, dg_mix = _in_bwd(dqkv, dcin, p["w_in_t"], x, dx2, p["g_mix"])
    if scatter:
        big = {"w_up": g_up[0], "w_down": g_down_out[0], "w_in": dw_in, "w_out": g_down_out[1]}
    else:
        big = {"w_up": dw_up, "w_down": dw_down, "w_in": dw_in, "w_out": dw_out}
    small = dict(dg_mix=dg_mix, dbqkv=dbqkv, dbcin=dbcin, dgq=dgq, dgk=dgk, dsink=dsink, dcw=dcw, dcb=dcb, dcgain=dcgain,
                 dcbias=dcbias, db_out=db_out, dg_ffn=dg_ffn, dfw=dfw, dfb=dfb, loss=loss)
    return loss, grad_x, big, small


BIG = ("w_in", "w_out", "w_up", "w_down")
SMALL = ("mix_norm_gain", "b_in", "q_norm_gain", "k_norm_gain", "attn_sinks", "conv_dw_w", "conv_dw_b",
         "conv_norm_gain", "conv_norm_bias", "b_out", "ffn_norm_gain", "ffn_dw_w", "ffn_dw_b")
ORDER = ("mix_norm_gain", "w_in", "b_in", "q_norm_gain", "k_norm_gain", "attn_sinks", "conv_dw_w", "conv_dw_b",
         "conv_norm_gain", "conv_norm_bias", "w_out", "b_out", "ffn_norm_gain", "w_up", "ffn_dw_w", "ffn_dw_b", "w_down")


def kernel(x, mix_norm_gain, w_in, b_in, q_norm_gain, k_norm_gain, attn_sinks, conv_dw_w, conv_dw_b, conv_norm_gain, conv_norm_bias, w_out, b_out, ffn_norm_gain, w_up, ffn_dw_w, ffn_dw_b, w_down, loss_target, m_mix_norm_gain, m_w_in, m_b_in, m_q_norm_gain, m_k_norm_gain, m_attn_sinks, m_conv_dw_w, m_conv_dw_b, m_conv_norm_gain, m_conv_norm_bias, m_w_out, m_b_out, m_ffn_norm_gain, m_w_up, m_ffn_dw_w, m_ffn_dw_b, m_w_down, v_mix_norm_gain, v_w_in, v_b_in, v_q_norm_gain, v_k_norm_gain, v_attn_sinks, v_conv_dw_w, v_conv_dw_b, v_conv_norm_gain, v_conv_norm_bias, v_w_out, v_b_out, v_ffn_norm_gain, v_w_up, v_ffn_dw_w, v_ffn_dw_b, v_w_down):
    w = dict(mix_norm_gain=mix_norm_gain, w_in=w_in, b_in=b_in, q_norm_gain=q_norm_gain, k_norm_gain=k_norm_gain,
             attn_sinks=attn_sinks, conv_dw_w=conv_dw_w, conv_dw_b=conv_dw_b, conv_norm_gain=conv_norm_gain,
             conv_norm_bias=conv_norm_bias, w_out=w_out, b_out=b_out, ffn_norm_gain=ffn_norm_gain, w_up=w_up,
             ffn_dw_w=ffn_dw_w, ffn_dw_b=ffn_dw_b, w_down=w_down)
    m = dict(mix_norm_gain=m_mix_norm_gain, w_in=m_w_in, b_in=m_b_in, q_norm_gain=m_q_norm_gain, k_norm_gain=m_k_norm_gain,
             attn_sinks=m_attn_sinks, conv_dw_w=m_conv_dw_w, conv_dw_b=m_conv_dw_b, conv_norm_gain=m_conv_norm_gain,
             conv_norm_bias=m_conv_norm_bias, w_out=m_w_out, b_out=m_b_out, ffn_norm_gain=m_ffn_norm_gain, w_up=m_w_up,
             ffn_dw_w=m_ffn_dw_w, ffn_dw_b=m_ffn_dw_b, w_down=m_w_down)
    v = dict(mix_norm_gain=v_mix_norm_gain, w_in=v_w_in, b_in=v_b_in, q_norm_gain=v_q_norm_gain, k_norm_gain=v_k_norm_gain,
             attn_sinks=v_attn_sinks, conv_dw_w=v_conv_dw_w, conv_dw_b=v_conv_dw_b, conv_norm_gain=v_conv_norm_gain,
             conv_norm_bias=v_conv_norm_bias, w_out=v_w_out, b_out=v_b_out, ffn_norm_gain=v_ffn_norm_gain, w_up=v_w_up,
             ffn_dw_w=v_ffn_dw_w, ffn_dw_b=v_ffn_dw_b, w_down=v_w_down)
    s = x.shape[1]

    wi8, cw8, fw8 = _allgather([w_in.T, conv_dw_w, ffn_dw_w], [BF16, F32, F32])
    lane_pad = ((0, 0), (0, 0), (0, FF_PADDED - FF_CHUNK))
    p = {
        "g_mix": mix_norm_gain.reshape(1, -1), "w_in_t": wi8.reshape(QKV_COLS + CIN_COLS, D_MODEL),
        "b_qkv": b_in[:QKV_COLS].reshape(1, -1), "b_cin": b_in[QKV_COLS:].reshape(1, -1),
        "gq2": jnp.tile(q_norm_gain, 2).reshape(1, -1), "gk2": jnp.tile(k_norm_gain, 2).reshape(1, -1), "sinks": attn_sinks,
        "cw8": jnp.repeat(cw8.transpose(1, 0, 2).reshape(CONV_KERNEL, CONV_WIDTH), SUBLANES, axis=0),
        "cb": conv_dw_b.reshape(1, -1), "cgain": conv_norm_gain.reshape(1, -1), "cbias": conv_norm_bias.reshape(1, -1),
        "b_out": b_out.reshape(1, -1), "g_ffn": ffn_norm_gain.reshape(1, -1),
        "fw": jnp.pad(fw8, lane_pad), "fb": jnp.pad(ffn_dw_b.reshape(N_DEV, 1, FF_CHUNK), lane_pad),
    }

    saved, (wu8,), (wo8, wd8) = _token_mixing(x[0], p, [w_up.T], [w_out, w_down])
    p.update(w_out=wo8.reshape(D_MODEL, D_MODEL), w_up=wu8, w_down=wd8.reshape(N_FF_PAIRS, FF_CHUNK, D_MODEL))
    loss, grad_x, big, small = _rest_of_step(x[0], loss_target[0], p, saved, True)

    g = dict(big)
    g["w_in"], gpack = _final_exchange(big["w_in"], _pack_small(small))

    delta, new_m, new_v = {}, {}, {}
    transposed = ("w_in", "w_up")
    big_out = _adamw(*[[d[n].T if n in transposed and d is not g else d[n] for n in BIG] for d in (w, g, m, v)])
    for n, outs in zip(BIG, big_out):
        if n in transposed:
            g[n], delta[n], new_m[n], new_v[n] = g[n].T, *[o.T for o in outs]
        else:
            delta[n], new_m[n], new_v[n] = outs

    def view(a):
        return a if a.ndim == 2 else (a.reshape(-1, LANES) if a.size % LANES == 0 else a.reshape(1, -1))

    small_out = _adamw_small(gpack, *[{n: view(d[n]) for n in SMALL} for d in (w, m, v)])
    for n in SMALL:
        g[n], delta[n], new_m[n], new_v[n] = [a.reshape(w[n].shape) for a in small_out[n]]

    total = gpack[R_QKS + 3, 0]
    return (total, grad_x.reshape(1, s, D_MODEL), *[g[n] for n in ORDER], *[delta[n] for n in ORDER],
            *[new_m[n] for n in ORDER], *[new_v[n] for n in ORDER])
```

```python
import functools
import math

import jax
import jax.numpy as jnp
from jax import lax
from jax.experimental import pallas as pl
from jax.experimental.pallas import tpu as pltpu

F32 = jnp.float32
BF16 = jnp.bfloat16

D_MODEL = 1024
HEAD_DIM = 64
N_Q_HEADS = 8
N_KV_HEADS = 2
Q_COLS = 512
KV_COLS = 128
QKV_COLS = Q_COLS + 2 * KV_COLS
CONV_WIDTH = 512
CIN_COLS = 2 * CONV_WIDTH
CONV_KERNEL = 31
CONV_HALO = 32
D_FF = 2816
N_DEV = 8
FF_CHUNK = 2 * D_FF // N_DEV
N_FF_PAIRS = N_DEV // 2
ATT_BLOCK = 128
EPS = 1e-6
NEG_INF = -1e30
SLOPES = [float(2.0 ** (-8.0 * (h + 1.0) / N_Q_HEADS)) for h in range(N_Q_HEADS)]

ADAM_LR = 0.001
ADAM_B1 = 0.9
ADAM_B2 = 0.999
ADAM_EPS = 1e-08
ADAM_WD = 0.01
ADAM_STEP = 10

LANES = 128
SUBLANES = 8
VMEM_LIMIT = 56 * 1024 * 1024
MESH = pl.DeviceIdType.MESH


def _cparams(*sem, **kw):
    return pltpu.CompilerParams(dimension_semantics=sem or None, vmem_limit_bytes=VMEM_LIMIT, **kw)


def _resident(shape):
    nd = len(shape)
    return pl.BlockSpec(shape, lambda *_: (0,) * nd, pipeline_mode=pl.Buffered(1))


def _dot(a, b):
    return jnp.dot(a, b, preferred_element_type=F32)


def _dot_nt(a, b):
    return lax.dot_general(a, b, (((1,), (1,)), ((), ())), preferred_element_type=F32)


def _dot_tn(a, b):
    return lax.dot_general(a, b, (((0,), (0,)), ((), ())), preferred_element_type=F32)


def _sigmoid(x):
    return 1.0 / (1.0 + jnp.exp(-x))


def _lo_mask(shape):
    return lax.broadcasted_iota(jnp.int32, shape, len(shape) - 1) % LANES < HEAD_DIM


def _half_sums(t, lo):
    s_lo = jnp.sum(jnp.where(lo, t, 0.0), axis=-1, keepdims=True)
    s_hi = jnp.sum(jnp.where(lo, 0.0, t), axis=-1, keepdims=True)
    return jnp.where(lo, s_lo, s_hi)


def _head_norm(t, lo):
    r = lax.rsqrt(_half_sums(t * t, lo) * (1.0 / HEAD_DIM) + EPS)
    return t * r, r


def _head_norm_bwd(dn, n, r, lo):
    return r * (dn - n * (_half_sums(dn * n, lo) * (1.0 / HEAD_DIM)))


def _tile(s):
    return min(512, s)


def _wide_tile(s):
    return min(1024, s)


TN_TOKENS = 2048
FF_COLS = ((0, 256), (256, 512), (512, 704))


def _position():
    return lax.axis_index("x"), lax.axis_index("y"), lax.axis_index("c")


def _dev_index(px, py, pc):
    return 4 * px + 2 * py + pc


def _flip(v, bit):
    return 1 - v if bit else v


OTHER_CHIPS = ((1, 0), (0, 1), (1, 1))
N_GATHER_COPIES = 1 + 2 * len(OTHER_CHIPS)


class _Gather:
    def __init__(self, shard_ref, out_ref, cast_buf, send_sems, recv_sems, local_sem):
        self.shard, self.out, self.buf = shard_ref, out_ref, cast_buf
        self.send_sems, self.recv_sems, self.local_sem = send_sems, recv_sems, local_sem
        x, y, c = _position()
        self.c = c
        self.me, self.sibling = (x, y, c), (x, y, 1 - c)
        self.chips = [(_flip(x, fx), _flip(y, fy)) for fx, fy in OTHER_CHIPS]

    def _copy(self, k, block, to, from_buf=False):
        rows = self.out.at[_dev_index(*block)]
        return pltpu.make_async_remote_copy(src_ref=self.buf if from_buf else rows, dst_ref=rows,
                                            send_sem=self.send_sems.at[k], recv_sem=self.recv_sems.at[k],
                                            device_id=to, device_id_type=MESH)

    def _local(self):
        return pltpu.make_async_copy(self.buf, self.out.at[_dev_index(*self.me)], self.local_sem)

    def start(self):
        self.buf[...] = self.shard[...].astype(self.buf.dtype)
        self._local().start()
        for j, chip in enumerate(self.chips):
            self._copy(1 + j, self.me, (*chip, self.c), from_buf=True).start()
        self._copy(0, self.me, self.sibling, from_buf=True).start()

    def forward(self):
        for j, chip in enumerate(self.chips):
            self._copy(1 + j, (*chip, self.c), self.me).wait_recv()
            self._copy(1 + len(self.chips) + j, (*chip, self.c), self.sibling).start()

    def finish(self):
        self._copy(0, self.sibling, self.me).wait_recv()
        for j, chip in enumerate(self.chips):
            self._copy(1 + len(self.chips) + j, (*chip, 1 - self.c), self.me).wait_recv()
        for k in range(N_GATHER_COPIES):
            self._copy(k, self.me, self.sibling).wait_send()
        self._local().wait()


def _gather_specs(shards):
    whole = [pl.BlockSpec(w.shape, lambda *_, nd=w.ndim: (0,) * nd, pipeline_mode=pl.Buffered(1)) for w in shards]
    outs = [pl.BlockSpec(memory_space=pl.ANY) for _ in shards]
    shapes = [jax.ShapeDtypeStruct((N_DEV,) + w.shape, BF16) for w in shards]
    scratch = []
    for w in shards:
        scratch += [pltpu.VMEM(w.shape, BF16), pltpu.SemaphoreType.DMA((N_GATHER_COPIES,)),
                    pltpu.SemaphoreType.DMA((N_GATHER_COPIES,)), pltpu.SemaphoreType.DMA(())]
    return whole, outs, shapes, scratch


def _run_gathers(gathers, step, n_steps):
    @pl.when(step == 0)
    def _():
        for g in gathers:
            g.start()

    @pl.when(step == 3 * n_steps // 4)
    def _():
        for g in gathers:
            g.forward()

    @pl.when(step == n_steps - 1)
    def _():
        for g in gathers:
            g.finish()


class _ReduceScatter:
    def __init__(self, g_ref, out_ref, stage, load_sems, send_a, recv_a, send_b, recv_b, sa_send, sa_recv, sb_send, sb_recv):
        self.g, self.out, self.stage, self.load_sems = g_ref, out_ref, stage, load_sems
        self.send_a, self.recv_a, self.send_b, self.recv_b = send_a, recv_a, send_b, recv_b
        self.sems = (sa_send, sa_recv, sb_send, sb_recv)
        x, y, c = _position()
        self.c, self.sibling = c, (x, y, 1 - c)
        self.chips = [(x, y)] + [(_flip(x, fx), _flip(y, fy)) for fx, fy in OTHER_CHIPS]

    def _copy_a(self, j):
        return pltpu.make_async_remote_copy(src_ref=self.send_a.at[j], dst_ref=self.recv_a.at[j], send_sem=self.sems[0].at[j],
                                            recv_sem=self.sems[1].at[j], device_id=self.sibling, device_id_type=MESH)

    def _copy_b(self, j):
        return pltpu.make_async_remote_copy(src_ref=self.send_b.at[j], dst_ref=self.recv_b.at[j], send_sem=self.sems[2].at[j],
                                            recv_sem=self.sems[3].at[j], device_id=(*self.chips[1 + j], self.c),
                                            device_id_type=MESH)

    def _load(self, j, core):
        return pltpu.make_async_copy(self.g.at[_dev_index(*self.chips[j], core)], self.stage.at[j % 2], self.load_sems.at[j % 2])

    def send_block(self, j):
        if j == 0:
            self._load(0, 1 - self.c).start()
        self._load(j, 1 - self.c).wait()
        if j + 1 < len(self.chips):
            self._load(j + 1, 1 - self.c).start()
        else:
            self._load(0, self.c).start()
        self.send_a[j] = self.stage[j % 2].astype(BF16)
        self._copy_a(j).start()

    def sum_block(self, j):
        self._load(j, self.c).wait()
        if j + 1 < len(self.chips):
            self._load(j + 1, self.c).start()
        self._copy_a(j).wait_recv()
        part = self.stage[j % 2] + self.recv_a[j].astype(F32)
        if j == 0:
            self.out[...] = part
        else:
            self.send_b[j - 1] = part.astype(BF16)
            self._copy_b(j - 1).start()

    def start(self):
        for j in range(len(self.chips)):
            self.send_block(j)

    def middle(self):
        for j in range(len(self.chips)):
            self.sum_block(j)

    def finish(self):
        for j in range(len(OTHER_CHIPS)):
            self._copy_b(j).wait_recv()
            self.out[...] += self.recv_b[j].astype(F32)
        for j in range(len(self.chips)):
            self._copy_a(j).wait_send()
        for j in range(len(OTHER_CHIPS)):
            self._copy_b(j).wait_send()


N_SCATTER_SCRATCH = 10


def _scatter_specs(g8s):
    na, nb = 1 + len(OTHER_CHIPS), len(OTHER_CHIPS)
    ins = [pl.BlockSpec(memory_space=pl.ANY) for _ in g8s]
    outs = [pl.BlockSpec(g.shape[1:], lambda *_: (0, 0)) for g in g8s]
    shapes = [jax.ShapeDtypeStruct(g.shape[1:], F32) for g in g8s]
    scratch = []
    for g in g8s:
        blk = g.shape[1:]
        scratch += [pltpu.VMEM((2,) + blk, F32), pltpu.SemaphoreType.DMA((2,)), pltpu.VMEM((na,) + blk, BF16), pltpu.VMEM((na,) + blk, BF16),
                    pltpu.VMEM((nb,) + blk, BF16), pltpu.VMEM((nb,) + blk, BF16),
                    pltpu.SemaphoreType.DMA((na,)), pltpu.SemaphoreType.DMA((na,)),
                    pltpu.SemaphoreType.DMA((nb,)), pltpu.SemaphoreType.DMA((nb,))]
    return ins, outs, shapes, scratch


def _run_scatters(scatters, step, n_steps):
    n_blocks = 1 + len(OTHER_CHIPS)
    for j in range(n_blocks):
        @pl.when(step == min(j, n_steps - 1))
        def _(j=j):
            for r in scatters:
                r.send_block(j)

    for j in range(n_blocks):
        @pl.when(step == min(n_blocks + j, n_steps - 1))
        def _(j=j):
            for r in scatters:
                r.sum_block(j)

    @pl.when(step == n_steps - 1)
    def _():
        for r in scatters:
            r.finish()


def _mix_proj(x, g_mix, w_in_t, b_qkv, b_cin):
    s = x.shape[0]
    tm = _wide_tile(s)

    def body(x_ref, g_ref, w_ref, bq_ref, bc_ref, qkv_ref, cin_ref, h1_ref):
        xv = x_ref[...]
        r = lax.rsqrt(jnp.mean(xv * xv, axis=-1, keepdims=True) + EPS)
        h = (xv * r * g_ref[...]).astype(BF16)
        h1_ref[...] = h
        qkv_ref[...] = _dot_nt(h, w_ref[0:QKV_COLS, :]) + bq_ref[...]
        cin_ref[...] = _dot_nt(h, w_ref[QKV_COLS:, :]) + bc_ref[...]

    return pl.pallas_call(
        body, grid=(s // tm,),
        in_specs=[pl.BlockSpec((tm, D_MODEL), lambda i: (i, 0)), _resident((1, D_MODEL)),
                  _resident((QKV_COLS + CIN_COLS, D_MODEL)), _resident((1, QKV_COLS)), _resident((1, CIN_COLS))],
        out_specs=[pl.BlockSpec((tm, QKV_COLS), lambda i: (i, 0)), pl.BlockSpec((tm, CIN_COLS), lambda i: (i, 0)),
                   pl.BlockSpec((tm, D_MODEL), lambda i: (i, 0))],
        out_shape=[jax.ShapeDtypeStruct((s, QKV_COLS), F32), jax.ShapeDtypeStruct((s, CIN_COLS), F32),
                   jax.ShapeDtypeStruct((s, D_MODEL), BF16)],
        compiler_params=_cparams("parallel"), name="mix_proj")(x, g_mix, w_in_t, b_qkv, b_cin)


def _kv_variants(kv_all, gk2, lo):
    k_all = kv_all[:, :LANES]
    v_all = kv_all[:, LANES:]
    kn_pre, rk = _head_norm(k_all, lo)
    kn = kn_pre * gk2
    kr = pltpu.roll(kn, HEAD_DIM, 1)
    vr = pltpu.roll(v_all, HEAD_DIM, 1)
    zero = jnp.zeros_like(kn)
    k_lo = [jnp.where(lo, kn, zero).astype(BF16), jnp.where(lo, kr, zero).astype(BF16)]
    k_hi = [jnp.where(lo, zero, kr).astype(BF16), jnp.where(lo, zero, kn).astype(BF16)]
    v_lo = [jnp.where(lo, v_all, zero).astype(BF16), jnp.where(lo, vr, zero).astype(BF16)]
    v_hi = [jnp.where(lo, zero, vr).astype(BF16), jnp.where(lo, zero, v_all).astype(BF16)]
    return k_lo, k_hi, v_lo, v_hi, kn_pre, rk


def _att_consts(first_tile, b):
    rows = 2 * ATT_BLOCK
    qi = lax.broadcasted_iota(jnp.int32, (rows, 2 * ATT_BLOCK), 0) % ATT_BLOCK
    kj = lax.broadcasted_iota(jnp.int32, (rows, 2 * ATT_BLOCK), 1)
    rel = qi + ATT_BLOCK - kj
    valid = (rel >= 0) & (rel < ATT_BLOCK)
    if b == 0:
        valid = valid & ((kj >= ATT_BLOCK) | jnp.logical_not(first_tile))
    return rel.astype(F32), valid


def _row_const(va, vb):
    top = lax.broadcasted_iota(jnp.int32, (2 * ATT_BLOCK, 1), 0) < ATT_BLOCK
    return jnp.where(top, va, vb)


def _probs(q2, k_op, rel, valid, slope, sink):
    sc = _dot_nt(q2, k_op) * (1.0 / math.sqrt(HEAD_DIM)) - slope * rel
    sc = jnp.where(valid, sc, NEG_INF)
    m = jnp.maximum(jnp.max(sc, axis=-1, keepdims=True), sink)
    p = jnp.exp(sc - m)
    e_sink = jnp.exp(sink - m)
    inv = 1.0 / (jnp.sum(p, axis=-1, keepdims=True) + e_sink)
    return p * inv, e_sink * inv


def _attn_fwd(qkv, gq2, gk2, sinks, shards):
    s = qkv.shape[0]
    tq = _tile(s)
    nb = tq // ATT_BLOCK
    ng = len(shards)
    g_in, g_out, g_shape, g_scratch = _gather_specs(shards)

    def body(q_ref, kv_ref, kvp_ref, gq_ref, gk_ref, sink_ref, *rest):
        out_ref = rest[ng]
        i = pl.program_id(0)
        _run_gathers([_Gather(rest[a], rest[ng + 1 + a], *rest[2 * ng + 1 + 4 * a:2 * ng + 5 + 4 * a]) for a in range(ng)],
                     i, s // tq)
        lo = _lo_mask((1, LANES))
        kv_all = jnp.concatenate([kvp_ref[...], kv_ref[...]], axis=0)
        k_lo, k_hi, v_lo, v_hi, _, _ = _kv_variants(kv_all, gk_ref[...], lo)
        for b in range(nb):
            rel, valid = _att_consts(i == 0, b)
            rows = slice(b * ATT_BLOCK, (b + 1) * ATT_BLOCK)
            keys = slice(b * ATT_BLOCK, (b + 2) * ATT_BLOCK)
            for kvh in range(N_KV_HEADS):
                pairs = (2 * kvh, 2 * kvh + 1)
                q2 = jnp.concatenate([q_ref[rows, p * LANES:(p + 1) * LANES] for p in pairs], axis=0)
                qn, _ = _head_norm(q2, lo)
                q2 = (qn * gq_ref[...]).astype(BF16)
                out = None
                for odd, (k_op, v_op) in enumerate(((k_lo[kvh][keys], v_lo[kvh][keys]), (k_hi[kvh][keys], v_hi[kvh][keys]))):
                    ha, hb = 2 * pairs[0] + odd, 2 * pairs[1] + odd
                    p, _ = _probs(q2, k_op, rel, valid, _row_const(SLOPES[ha], SLOPES[hb]),
                                  _row_const(sink_ref[ha], sink_ref[hb]))
                    o = _dot(p.astype(BF16), v_op)
                    out = o if out is None else out + o
                for n, p in enumerate(pairs):
                    out_ref[rows, p * LANES:(p + 1) * LANES] = out[n * ATT_BLOCK:(n + 1) * ATT_BLOCK].astype(BF16)

    return pl.pallas_call(
        body, grid=(s // tq,),
        in_specs=[pl.BlockSpec((tq, Q_COLS), lambda i: (i, 0)),
                  pl.BlockSpec((tq, 2 * KV_COLS), lambda i: (i, 2)),
                  pl.BlockSpec((ATT_BLOCK, 2 * KV_COLS), lambda i: (jnp.maximum(i * nb - 1, 0), 2)),
                  _resident((1, LANES)), _resident((1, LANES)),
                  pl.BlockSpec(memory_space=pltpu.SMEM)] + g_in,
        out_specs=[pl.BlockSpec((tq, Q_COLS), lambda i: (i, 0))] + g_out,
        out_shape=[jax.ShapeDtypeStruct((s, Q_COLS), BF16)] + g_shape,
        scratch_shapes=g_scratch,
        compiler_params=_cparams("arbitrary"), name="attn_fwd")(qkv, qkv, qkv, gq2, gk2, sinks, *shards)


def _group_stats(c1, lo):
    mu = _half_sums(c1, lo) * (1.0 / HEAD_DIM)
    d = c1 - mu
    rstd = lax.rsqrt(_half_sums(d * d, lo) * (1.0 / HEAD_DIM) + EPS)
    return d * rstd, rstd


def _rows(ref, first_row, n):
    return ref[pl.ds(first_row, n, stride=1), :].reshape(n // SUBLANES, SUBLANES, LANES)


def _conv_fwd(cin, cw8, cb, gain, bias, shards):
    s = cin.shape[0]
    tm = _tile(s)
    rc = 64
    nchunk = CONV_WIDTH // LANES
    lead = CONV_HALO - (CONV_KERNEL - 1)
    ng = len(shards)
    g_in, g_out, g_shape, g_scratch = _gather_specs(shards)

    def body(cin_ref, cw_ref, cb_ref, gain_ref, bias_ref, *rest):
        c3_ref, c1_ref, ext_ref = rest[ng], rest[ng + 1], rest[2 * ng + 2]
        _run_gathers([_Gather(rest[a], rest[ng + 2 + a], *rest[2 * ng + 3 + 4 * a:2 * ng + 7 + 4 * a]) for a in range(ng)],
                     pl.program_id(0), s // tm)

        @pl.when(pl.program_id(0) == 0)
        def _():
            ext_ref[:, 0:CONV_HALO, :] = jnp.zeros((nchunk, CONV_HALO, LANES), F32)

        lo = _lo_mask((1, LANES))
        for cc in range(nchunk):
            cols = slice(cc * LANES, (cc + 1) * LANES)
            gcols = slice(CONV_WIDTH + cc * LANES, CONV_WIDTH + (cc + 1) * LANES)
            ext_ref[cc, CONV_HALO:CONV_HALO + tm, :] = cin_ref[:, cols] * _sigmoid(cin_ref[:, gcols])
            ext = ext_ref.at[cc]
            for r in range(tm // rc):
                rows = slice(r * rc, (r + 1) * rc)
                acc = jnp.zeros((rc // SUBLANES, SUBLANES, LANES), F32)
                for k in range(CONV_KERNEL):
                    acc = acc + cw_ref[k * SUBLANES:(k + 1) * SUBLANES, cols][None] * _rows(ext, r * rc + lead + k, rc)
                c1 = acc.reshape(rc, LANES) + cb_ref[:, cols]
                c1_ref[cc, rows, :] = c1
                nrm, _ = _group_stats(c1, lo)
                c2 = nrm * gain_ref[:, cols] + bias_ref[:, cols]
                c3_ref[rows, cols] = (c2 * _sigmoid(c2)).astype(BF16)
        ext_ref[:, 0:CONV_HALO, :] = ext_ref[:, tm:tm + CONV_HALO, :]

    return pl.pallas_call(
        body, grid=(s // tm,),
        in_specs=[pl.BlockSpec((tm, CIN_COLS), lambda i: (i, 0)), _resident((CONV_KERNEL * SUBLANES, CONV_WIDTH)),
                  _resident((1, CONV_WIDTH)), _resident((1, CONV_WIDTH)), _resident((1, CONV_WIDTH))] + g_in,
        out_specs=[pl.BlockSpec((tm, CONV_WIDTH), lambda i: (i, 0)), pl.BlockSpec((nchunk, tm, LANES), lambda i: (0, i, 0))] + g_out,
        out_shape=[jax.ShapeDtypeStruct((s, CONV_WIDTH), BF16), jax.ShapeDtypeStruct((nchunk, s, LANES), F32)] + g_shape,
        scratch_shapes=[pltpu.VMEM((nchunk, tm + CONV_HALO, LANES), F32)] + g_scratch,
        compiler_params=_cparams("arbitrary"), name="conv_fwd")(cin, cw8, cb, gain, bias, *shards)


def _out_proj(x, attn, c3, wo_a, wo_c, b_out, g_ffn):
    s = x.shape[0]
    tm = _wide_tile(s)

    def body(x_ref, a_ref, c_ref, wa_ref, wc_ref, b_ref, g_ref, x2_ref, h2_ref):
        x2 = x_ref[...] + _dot(a_ref[...], wa_ref[...]) + _dot(c_ref[...], wc_ref[...]) + b_ref[...]
        x2_ref[...] = x2
        r = lax.rsqrt(jnp.mean(x2 * x2, axis=-1, keepdims=True) + EPS)
        h2_ref[...] = (x2 * r * g_ref[...]).astype(BF16)

    return pl.pallas_call(
        body, grid=(s // tm,),
        in_specs=[pl.BlockSpec((tm, D_MODEL), lambda i: (i, 0)), pl.BlockSpec((tm, Q_COLS), lambda i: (i, 0)),
                  pl.BlockSpec((tm, CONV_WIDTH), lambda i: (i, 0)),
                  pl.BlockSpec((Q_COLS, D_MODEL), lambda i: (0, 0), pipeline_mode=pl.Buffered(1)),
                  pl.BlockSpec((CONV_WIDTH, D_MODEL), lambda i: (1, 0), pipeline_mode=pl.Buffered(1)),
                  _resident((1, D_MODEL)), _resident((1, D_MODEL))],
        out_specs=[pl.BlockSpec((tm, D_MODEL), lambda i: (i, 0)), pl.BlockSpec((tm, D_MODEL), lambda i: (i, 0))],
        out_shape=[jax.ShapeDtypeStruct((s, D_MODEL), F32), jax.ShapeDtypeStruct((s, D_MODEL), BF16)],
        compiler_params=_cparams("parallel"), name="out_proj")(x, attn, c3, wo_a, wo_c, b_out, g_ffn)


FF_LANE_CHUNKS = -(-FF_CHUNK // LANES)
FF_PADDED = FF_LANE_CHUNKS * LANES


def _tap(ref, first_row, n):
    return ref[pl.ds(first_row, n, stride=1), :]


def _ffn_fwd(h2, x2, target, w_up, fw, fb, w_down):
    s = h2.shape[0]
    tm = _tile(s)
    hal = SUBLANES
    rc = min(128, tm)

    def body(h_ref, x2_ref, t_ref, wu_ref, fw_ref, fb_ref, wd_ref, up0_ref, gu_ref, act_ref, dy_ref, dyb_ref, loss_ref,
             ext_ref, carry_ref, act_buf, y_ref):
        i, ci = pl.program_id(0), pl.program_id(1)

        @pl.when((i == 0) & (ci == 0))
        def _():
            carry_ref[...] = jnp.zeros(carry_ref.shape, F32)
            ext_ref[...] = jnp.zeros(ext_ref.shape, F32)
            act_buf[...] = jnp.zeros(act_buf.shape, BF16)
            loss_ref[...] = jnp.zeros((1, 1), F32)

        @pl.when(ci == 0)
        def _():
            y_ref[...] = x2_ref[...]

        ws = (fw_ref[ci], fw_ref[ci + N_FF_PAIRS])
        bs = (fb_ref[ci], fb_ref[ci + N_FF_PAIRS])
        half_rows = (slice(0, tm // 2), slice(tm // 2, tm))
        n_grp = len(FF_COLS)

        def up_slices(grp):
            lo_c, hi_c = FF_COLS[grp]
            chunks = range(lo_c // LANES, -(-hi_c // LANES))

            def make(half, n, rows):
                def run():
                    c = ci + half * N_FF_PAIRS
                    u0 = _dot_nt(h_ref[rows, :], wu_ref[c, lo_c:hi_c, :])
                    up0_ref[half, 0, rows, lo_c:hi_c] = u0.astype(BF16)
                    if hi_c == FF_CHUNK:
                        up0_ref[half, 0, rows, FF_CHUNK:] = jnp.zeros((u0.shape[0], FF_PADDED - FF_CHUNK), BF16)
                    for j in chunks:
                        w = min(LANES, hi_c - j * LANES)
                        if n == 0:
                            ext_ref[half, j, 0:hal, 0:w] = carry_ref[c, :, j * LANES:j * LANES + w]
                        ext_ref[half, j, hal + rows.start:hal + rows.stop, 0:w] = u0[:, j * LANES - lo_c:j * LANES - lo_c + w]
                    if n == len(half_rows) - 1:
                        carry_ref[c, :, lo_c:hi_c] = u0[u0.shape[0] - hal:, :]
                return run
            return [make(half, n, rows) for half in range(2) for n, rows in enumerate(half_rows)]

        def down_slices(grp):
            lo_c, hi_c = FF_COLS[grp]

            def make(rows):
                def run():
                    y_ref[rows, :] += _dot(act_buf[rows, lo_c:hi_c], wd_ref[ci, lo_c:hi_c, :])
                return run
            return [make(rows) for rows in half_rows]

        def vector_blocks(grp):
            lo_c, hi_c = FF_COLS[grp]
            blocks = []
            for j in range(lo_c // LANES, -(-hi_c // LANES)):
                lanes = slice(j * LANES, (j + 1) * LANES)

                def gate(r, lanes=lanes, j=j):
                    base = r * rc
                    ups = []
                    for half in range(2):
                        e, w = ext_ref.at[half, j], ws[half]
                        ups.append(w[0:1, lanes] * _tap(e, base + hal - 2, rc) + w[1:2, lanes] * _tap(e, base + hal - 1, rc)
                                   + w[2:3, lanes] * _tap(e, base + hal, rc) + bs[half][:, lanes])
                    g, u = ups
                    gu_ref[0, 0, base:base + rc, lanes] = g.astype(BF16)
                    gu_ref[1, 0, base:base + rc, lanes] = u.astype(BF16)
                    act_buf[base:base + rc, lanes] = (g * _sigmoid(g) * u).astype(BF16)

                blocks += [functools.partial(gate, r) for r in range(tm // rc)]

            def finish():
                act_ref[0, :, lo_c:hi_c] = act_buf[:, lo_c:hi_c]
            blocks.append(finish)
            return blocks

        for run in up_slices(0):
            run()
        for grp in range(n_grp):
            matmuls = (up_slices(grp + 1) if grp + 1 < n_grp else []) + (down_slices(grp - 1) if grp > 0 else [])
            blocks = vector_blocks(grp)
            every = max(1, len(blocks) // (len(matmuls) + 1))
            for n, run in enumerate(blocks):
                run()
                if n % every == every - 1 and matmuls:
                    matmuls.pop(0)()
            for run in matmuls:
                run()
        for run in down_slices(n_grp - 1):
            run()

        @pl.when(ci == N_FF_PAIRS - 1)
        def _():
            e = y_ref[...] - t_ref[...]
            dy_ref[...] = e * (1.0 / D_MODEL)
            dyb_ref[...] = (e * (1.0 / D_MODEL)).astype(BF16)
            loss_ref[...] += (0.5 / D_MODEL) * jnp.sum(e * e).reshape(1, 1)

    tok = lambda i, ci: (i, 0)
    return pl.pallas_call(
        body, grid=(s // tm, N_FF_PAIRS),
        in_specs=[pl.BlockSpec((tm, D_MODEL), tok), pl.BlockSpec((tm, D_MODEL), tok), pl.BlockSpec((tm, D_MODEL), tok),
                  _resident((N_DEV, FF_CHUNK, D_MODEL)), _resident((N_DEV, 3, FF_PADDED)), _resident((N_DEV, 1, FF_PADDED)),
                  _resident((N_FF_PAIRS, FF_CHUNK, D_MODEL))],
        out_specs=[pl.BlockSpec((2, 1, tm, FF_PADDED), lambda i, ci: (0, ci, i, 0)),
                   pl.BlockSpec((2, 1, tm, FF_PADDED), lambda i, ci: (0, ci, i, 0)),
                   pl.BlockSpec((1, tm, FF_CHUNK), lambda i, ci: (ci, i, 0)),
                   pl.BlockSpec((tm, D_MODEL), tok), pl.BlockSpec((tm, D_MODEL), tok), pl.BlockSpec((1, 1), lambda i, ci: (0, 0))],
        out_shape=[jax.ShapeDtypeStruct((2, N_FF_PAIRS, s, FF_PADDED), BF16), jax.ShapeDtypeStruct((2, N_FF_PAIRS, s, FF_PADDED), BF16),
                   jax.ShapeDtypeStruct((N_FF_PAIRS, s, FF_CHUNK), BF16), jax.ShapeDtypeStruct((s, D_MODEL), F32),
                   jax.ShapeDtypeStruct((s, D_MODEL), BF16), jax.ShapeDtypeStruct((1, 1), F32)],
        scratch_shapes=[pltpu.VMEM((2, FF_LANE_CHUNKS, tm + hal, LANES), F32), pltpu.VMEM((N_DEV, hal, FF_CHUNK), F32),
                        pltpu.VMEM((tm, FF_PADDED), BF16), pltpu.VMEM((tm, D_MODEL), F32)],
        compiler_params=_cparams("arbitrary", "arbitrary"), name="ffn_fwd")(h2, x2, target, w_up, fw, fb, w_down)


def _ffn_bwd(dyb, up0, gu, w_up, fw, w_down):
    s = dyb.shape[0]
    tm = _tile(s)
    nt = s // tm
    nxt = SUBLANES
    rc = min(128, tm)

    def body(dy_ref, up0_ref, gu_ref, wu_ref, fw_ref, wd_ref,
             dup0_ref, dh2_ref, dfw_ref, dfb_ref, dext_ref, carry_ref, dact_buf, dup0_buf, dh2_acc):
        i, ci = pl.program_id(0), pl.program_id(1)

        @pl.when((i == 0) & (ci == 0))
        def _():
            for ref in (carry_ref, dfw_ref, dfb_ref, dext_ref, dact_buf):
                ref[...] = jnp.zeros(ref.shape, F32)
            dup0_buf[...] = jnp.zeros(dup0_buf.shape, BF16)

        @pl.when(ci == 0)
        def _():
            dh2_acc[...] = jnp.zeros(dh2_acc.shape, F32)

        ws = (fw_ref[ci], fw_ref[ci + N_FF_PAIRS])
        fold = lambda v: jnp.sum(v.reshape(rc // SUBLANES, SUBLANES, LANES), axis=0)
        half_rows = (slice(0, tm // 2), slice(tm // 2, tm))
        n_grp = len(FF_COLS)

        def dact_slices(grp):
            lo_c, hi_c = FF_COLS[grp]

            def make(rows):
                def run():
                    dact_buf[rows, lo_c:hi_c] = _dot_nt(dy_ref[rows, :], wd_ref[ci, lo_c:hi_c, :])
                return run
            return [make(rows) for rows in half_rows]

        def dh2_slices(grp):
            lo_c, hi_c = FF_COLS[grp]

            def make(half, rows):
                def run():
                    c = ci + half * N_FF_PAIRS
                    dh2_acc[rows, :] += _dot(dup0_buf[half, rows, lo_c:hi_c], wu_ref[c, lo_c:hi_c, :])
                return run
            return [make(half, rows) for half in range(2) for rows in half_rows]

        def vector_blocks(grp):
            lo_c, hi_c = FF_COLS[grp]
            chunks = range(lo_c // LANES, -(-hi_c // LANES))
            blocks = []

            def stage():
                for half in range(2):
                    c = ci + half * N_FF_PAIRS
                    for j in chunks:
                        dext_ref[half, j, tm:tm + nxt, :] = carry_ref[c, :, j * LANES:(j + 1) * LANES]
            blocks.append(stage)
            for j in chunks:
                lanes = slice(j * LANES, (j + 1) * LANES)
                acc = [jnp.zeros((SUBLANES, LANES), F32)] * 8

                def grads(r, lanes=lanes, j=j, acc=acc):
                    base = r * rc
                    g = gu_ref[0, 0, base:base + rc, lanes].astype(F32)
                    u = gu_ref[1, 0, base:base + rc, lanes].astype(F32)
                    sg = _sigmoid(g)
                    silu = g * sg
                    dact = dact_buf[base:base + rc, lanes]
                    ds = (dact * u * (sg + silu - silu * sg), dact * silu)
                    for half in range(2):
                        dext_ref[half, j, base:base + rc, :] = ds[half]
                        acc[4 * half] = acc[4 * half] + fold(ds[half])

                def conv_back(r, lanes=lanes, j=j, acc=acc):
                    base = r * rc
                    for half in range(2):
                        d, w = dext_ref.at[half, j], ws[half]
                        taps = [_tap(d, base + k, rc) for k in range(3)]
                        dup0 = w[2:3, lanes] * taps[0] + w[1:2, lanes] * taps[1] + w[0:1, lanes] * taps[2]
                        dup0_buf[half, base:base + rc, lanes] = dup0.astype(BF16)
                        u0 = up0_ref[half, 0, base:base + rc, lanes].astype(F32)
                        for k in range(3):
                            acc[4 * half + 1 + k] = acc[4 * half + 1 + k] + fold(taps[2 - k] * u0)

                def sums(lanes=lanes, j=j, acc=acc):
                    for half in range(2):
                        c = ci + half * N_FF_PAIRS
                        carry_ref[c, :, lanes] = dext_ref[half, j, 0:nxt, :]
                        dfb_ref[c, :, lanes] += jnp.sum(acc[4 * half], axis=0, keepdims=True)
                        dfw_ref[c, :, lanes] += jnp.concatenate(
                            [jnp.sum(acc[4 * half + 1 + k], axis=0, keepdims=True) for k in range(3)], axis=0)

                blocks += [functools.partial(grads, r) for r in range(tm // rc)]
                blocks += [functools.partial(conv_back, r) for r in range(tm // rc)] + [sums]

            def finish():
                for half in range(2):
                    dup0_ref[half, 0, :, lo_c:hi_c] = dup0_buf[half, :, lo_c:hi_c]
            blocks.append(finish)
            return blocks

        for run in dact_slices(0):
            run()
        for grp in range(n_grp):
            matmuls = (dact_slices(grp + 1) if grp + 1 < n_grp else []) + (dh2_slices(grp - 1) if grp > 0 else [])
            blocks = vector_blocks(grp)
            every = max(1, len(blocks) // (len(matmuls) + 1))
            for n, run in enumerate(blocks):
                run()
                if n % every == every - 1 and matmuls:
                    matmuls.pop(0)()
            for run in matmuls:
                run()
        for run in dh2_slices(n_grp - 1):
            run()

        @pl.when(ci == N_FF_PAIRS - 1)
        def _():
            dh2_ref[...] = dh2_acc[...].astype(BF16)

    tok = lambda i, ci: (nt - 1 - i, 0)
    acc = lambda shape: pl.BlockSpec(shape, lambda i, ci: (0,) * len(shape))
    saved = pl.BlockSpec((2, 1, tm, FF_PADDED), lambda i, ci: (0, ci, nt - 1 - i, 0))
    return pl.pallas_call(
        body, grid=(nt, N_FF_PAIRS),
        in_specs=[pl.BlockSpec((tm, D_MODEL), tok), saved, saved,
                  _resident((N_DEV, FF_CHUNK, D_MODEL)), _resident((N_DEV, 3, FF_PADDED)),
                  _resident((N_FF_PAIRS, FF_CHUNK, D_MODEL))],
        out_specs=[pl.BlockSpec((2, 1, tm, FF_CHUNK), lambda i, ci: (0, ci, nt - 1 - i, 0)),
                   pl.BlockSpec((tm, D_MODEL), tok), acc((N_DEV, 3, FF_PADDED)), acc((N_DEV, 1, FF_PADDED))],
        out_shape=[jax.ShapeDtypeStruct((2, N_FF_PAIRS, s, FF_CHUNK), BF16), jax.ShapeDtypeStruct((s, D_MODEL), BF16),
                   jax.ShapeDtypeStruct((N_DEV, 3, FF_PADDED), F32), jax.ShapeDtypeStruct((N_DEV, 1, FF_PADDED), F32)],
        scratch_shapes=[pltpu.VMEM((2, FF_LANE_CHUNKS, tm + nxt, LANES), F32), pltpu.VMEM((N_DEV, nxt, FF_PADDED), F32),
                        pltpu.VMEM((tm, FF_PADDED), F32), pltpu.VMEM((2, tm, FF_PADDED), BF16), pltpu.VMEM((tm, D_MODEL), F32)],
        compiler_params=_cparams("arbitrary", "arbitrary"), name="ffn_bwd")(dyb, up0, gu, w_up, fw, w_down)


def _ffn_norm_bwd(dh2, dy, x2, g_ffn, w_out, attn, c3):
    s = dy.shape[0]
    tm = _wide_tile(s)

    def body(dh_ref, dy_ref, x2_ref, g_ref, wo_ref, attn_ref, c3_ref, dx2_ref, dmix_ref, dg_ref, dbo_ref, dwo_ref):
        @pl.when(pl.program_id(0) == 0)
        def _():
            dg_ref[...] = jnp.zeros(dg_ref.shape, F32)
            dbo_ref[...] = jnp.zeros(dbo_ref.shape, F32)
            dwo_ref[...] = jnp.zeros(dwo_ref.shape, F32)

        x2v = x2_ref[...]
        r = lax.rsqrt(jnp.mean(x2v * x2v, axis=-1, keepdims=True) + EPS)
        n2 = x2v * r
        dh2 = dh_ref[...].astype(F32)
        dg_ref[...] += jnp.sum(dh2 * n2, axis=0, keepdims=True)
        dn = dh2 * g_ref[...]
        dx2 = dy_ref[...] + r * (dn - n2 * jnp.mean(dn * n2, axis=-1, keepdims=True))
        dx2_ref[...] = dx2
        dbo_ref[...] += jnp.sum(dx2, axis=0, keepdims=True)
        dx2b = dx2.astype(BF16)
        dmix_ref[...] = _dot_nt(dx2b, wo_ref[...]).astype(BF16)
        dwo_ref[0:Q_COLS, :] += _dot_tn(attn_ref[...], dx2b)
        dwo_ref[Q_COLS:, :] += _dot_tn(c3_ref[...], dx2b)

    tok = pl.BlockSpec((tm, D_MODEL), lambda i: (i, 0))
    vec = pl.BlockSpec((1, D_MODEL), lambda i: (0, 0))
    return pl.pallas_call(
        body, grid=(s // tm,),
        in_specs=[tok, tok, tok, _resident((1, D_MODEL)), _resident((D_MODEL, D_MODEL)),
                  pl.BlockSpec((tm, Q_COLS), lambda i: (i, 0)), pl.BlockSpec((tm, CONV_WIDTH), lambda i: (i, 0))],
        out_specs=[tok, tok, vec, vec, pl.BlockSpec((D_MODEL, D_MODEL), lambda i: (0, 0))],
        out_shape=[jax.ShapeDtypeStruct((s, D_MODEL), F32), jax.ShapeDtypeStruct((s, D_MODEL), BF16),
                   jax.ShapeDtypeStruct((1, D_MODEL), F32), jax.ShapeDtypeStruct((1, D_MODEL), F32),
                   jax.ShapeDtypeStruct((D_MODEL, D_MODEL), F32)],
        compiler_params=_cparams("arbitrary"), name="ffn_norm_bwd")(dh2, dy, x2, g_ffn, w_out, attn, c3)


def _conv_bwd(dmixed, c1, cin, cw8, gain, bias, g8s):
    ns = len(g8s)
    s_in, s_out, s_shape, s_scratch = _scatter_specs(g8s)
    s = cin.shape[0]
    tm = _tile(s)
    nt = s // tm
    rc = 64
    rn = min(256, tm)
    hal = CONV_HALO
    nchunk = CONV_WIDTH // LANES

    def body(dc3_ref, dc3n_ref, c1_ref, c1n_ref, cin_ref, cw_ref, gain_ref, bias_ref, *rest):
        dcin_ref, dcw_ref, dcb_ref, dgain_ref, dbias_ref, dbcin_ref = rest[ns:ns + 6]
        dc1_ext, dcw8 = rest[2 * ns + 6:2 * ns + 8]
        i = pl.program_id(0)
        first, last = i == 0, i == nt - 1
        own = rest[2 * ns + 8:]
        _run_scatters([_ReduceScatter(rest[a], rest[ns + 6 + a], *own[N_SCATTER_SCRATCH * a:N_SCATTER_SCRATCH * (a + 1)])
                       for a in range(ns)], i, nt)

        @pl.when(first)
        def _():
            for ref in (dcw8, dcb_ref, dgain_ref, dbias_ref, dbcin_ref):
                ref[...] = jnp.zeros(ref.shape, F32)

        lo = _lo_mask((1, LANES))

        def norm_bwd(dc3, c1v, cols):
            nrm, rstd = _group_stats(c1v, lo)
            c2 = nrm * gain_ref[:, cols] + bias_ref[:, cols]
            sg = _sigmoid(c2)
            dc2 = dc3 * (sg * (1.0 + c2 * (1.0 - sg)))
            dn = dc2 * gain_ref[:, cols]
            inv = 1.0 / HEAD_DIM
            dc1 = rstd * (dn - _half_sums(dn, lo) * inv - nrm * (_half_sums(dn * nrm, lo) * inv))
            return dc1, dc2, nrm

        def row_sum(v):
            return jnp.sum(v, axis=0, keepdims=True)

        for cc in range(nchunk):
            cols = slice(cc * LANES, (cc + 1) * LANES)
            gcols = slice(CONV_WIDTH + cc * LANES, CONV_WIDTH + (cc + 1) * LANES)
            d1e = dc1_ext.at[cc]
            dc1n, _, _ = norm_bwd(dc3n_ref[:, cols].astype(F32), c1n_ref[cc], cols)
            d1e[tm:tm + hal, :] = jnp.where(last, 0.0, dc1n)

            for r in range(tm // rn):
                rows = slice(r * rn, (r + 1) * rn)
                dc1, dc2, nrm = norm_bwd(dc3_ref[rows, cols].astype(F32), c1_ref[cc, rows, :], cols)
                d1e[rows, :] = dc1
                dgain_ref[:, cols] += row_sum(dc2 * nrm)
                dbias_ref[:, cols] += row_sum(dc2)
                dcb_ref[:, cols] += row_sum(dc1)
            zero = jnp.zeros((1, LANES), F32)

            def taps(r, sums):
                rows = pl.ds(pl.multiple_of(r * rc, rc), rc)
                a = cin_ref[rows, cols]
                sg = _sigmoid(cin_ref[rows, gcols])
                c0 = (a * sg).reshape(rc // SUBLANES, SUBLANES, LANES)
                dc0 = jnp.zeros((rc // SUBLANES, SUBLANES, LANES), F32)
                for k in range(CONV_KERNEL):
                    krows = slice(k * SUBLANES, (k + 1) * SUBLANES)
                    shifted = _rows(d1e, r * rc + CONV_KERNEL - 1 - k, rc)
                    dc0 = dc0 + cw_ref[krows, cols][None] * shifted
                    dcw8[krows, cols] += jnp.sum(shifted * c0, axis=0)
                dc0 = dc0.reshape(rc, LANES)
                da = dc0 * sg
                dgate = dc0 * a * sg * (1.0 - sg)
                dcin_ref[rows, cols] = da.astype(BF16)
                dcin_ref[rows, gcols] = dgate.astype(BF16)
                return sums[0] + row_sum(da), sums[1] + row_sum(dgate)

            sums = lax.fori_loop(0, tm // rc, taps, (zero, zero))
            dbcin_ref[:, cols] += sums[0]
            dbcin_ref[:, gcols] += sums[1]

        @pl.when(last)
        def _():
            for k in range(CONV_KERNEL):
                dcw_ref[k:k + 1, :] = jnp.sum(dcw8[k * SUBLANES:(k + 1) * SUBLANES, :], axis=0, keepdims=True)

    nh = tm // hal
    acc = lambda shape: pl.BlockSpec(shape, lambda i: (0,) * len(shape))
    return pl.pallas_call(
        body, grid=(nt,),
        in_specs=[pl.BlockSpec((tm, CONV_WIDTH), lambda i: (i, 1)),
                  pl.BlockSpec((hal, CONV_WIDTH), lambda i: (jnp.minimum((i + 1) * nh, s // hal - 1), 1)),
                  pl.BlockSpec((nchunk, tm, LANES), lambda i: (0, i, 0)),
                  pl.BlockSpec((nchunk, hal, LANES), lambda i: (0, jnp.minimum((i + 1) * nh, s // hal - 1), 0)),
                  pl.BlockSpec((tm, CIN_COLS), lambda i: (i, 0)),
                  _resident((CONV_KERNEL * SUBLANES, CONV_WIDTH)), _resident((1, CONV_WIDTH)), _resident((1, CONV_WIDTH))] + s_in,
        out_specs=[pl.BlockSpec((tm, CIN_COLS), lambda i: (i, 0)), acc((CONV_KERNEL, CONV_WIDTH)), acc((1, CONV_WIDTH)),
                   acc((1, CONV_WIDTH)), acc((1, CONV_WIDTH)), acc((1, CIN_COLS))] + s_out,
        out_shape=[jax.ShapeDtypeStruct((s, CIN_COLS), BF16), jax.ShapeDtypeStruct((CONV_KERNEL, CONV_WIDTH), F32),
                   jax.ShapeDtypeStruct((1, CONV_WIDTH), F32), jax.ShapeDtypeStruct((1, CONV_WIDTH), F32),
                   jax.ShapeDtypeStruct((1, CONV_WIDTH), F32), jax.ShapeDtypeStruct((1, CIN_COLS), F32)] + s_shape,
        scratch_shapes=[pltpu.VMEM((nchunk, tm + hal, LANES), F32),
                        pltpu.VMEM((CONV_KERNEL * SUBLANES, CONV_WIDTH), F32)] + s_scratch,
        compiler_params=_cparams("arbitrary"), name="conv_bwd")(dmixed, dmixed, c1, c1, cin, cw8, gain, bias, *g8s)


def _attn_bwd(qkv, dmixed, gq2, gk2, sinks, g8s):
    ns = len(g8s)
    s_in, s_out, s_shape, s_scratch = _scatter_specs(g8s)
    s = qkv.shape[0]
    tq = _tile(s)
    nb = tq // ATT_BLOCK
    nt = s // tq

    def body(q_ref, kv_ref, kvp_ref, do_ref, gq_ref, gk_ref, sink_ref, *rest):
        dqkv_ref, dgq_ref, dgk_ref, dsink_ref, dbqkv_ref = rest[ns:ns + 5]
        dk_acc, dv_acc, carry_k, carry_v = rest[2 * ns + 5:2 * ns + 9]
        i = pl.program_id(0)
        t = nt - 1 - i
        own = rest[2 * ns + 9:]
        _run_scatters([_ReduceScatter(rest[a], rest[ns + 5 + a], *own[N_SCATTER_SCRATCH * a:N_SCATTER_SCRATCH * (a + 1)])
                       for a in range(ns)], i, nt)

        @pl.when(i == 0)
        def _():
            for ref in (carry_k, carry_v, dgq_ref, dgk_ref, dsink_ref, dbqkv_ref):
                ref[...] = jnp.zeros(ref.shape, F32)

        lo = _lo_mask((1, LANES))
        lane_id = lax.broadcasted_iota(jnp.int32, (1, LANES), 1)
        kv_all = jnp.concatenate([kvp_ref[...], kv_ref[...]], axis=0)
        k_lo, k_hi, v_lo, v_hi, kn_pre, rk = _kv_variants(kv_all, gk_ref[...], lo)
        for acc_ref, carry in ((dk_acc, carry_k), (dv_acc, carry_v)):
            acc_ref[:, 0:tq, :] = jnp.zeros((N_KV_HEADS, tq, LANES), F32)
            acc_ref[:, tq:tq + ATT_BLOCK, :] = carry[...]
        dsink = jnp.zeros((1, LANES), F32)
        dgq = jnp.zeros((1, LANES), F32)
        gq = gq_ref[...]
        for b in range(nb):
            rel, valid = _att_consts(t == 0, b)
            rows = slice(b * ATT_BLOCK, (b + 1) * ATT_BLOCK)
            keys = slice(b * ATT_BLOCK, (b + 2) * ATT_BLOCK)
            for kvh in range(N_KV_HEADS):
                pairs = (2 * kvh, 2 * kvh + 1)
                q_raw = jnp.concatenate([q_ref[rows, p * LANES:(p + 1) * LANES] for p in pairs], axis=0)
                qn_pre, rq = _head_norm(q_raw, lo)
                q2 = (qn_pre * gq).astype(BF16)
                do2 = jnp.concatenate([do_ref[rows, p * LANES:(p + 1) * LANES] for p in pairs], axis=0).astype(BF16)
                dq2 = jnp.zeros((2 * ATT_BLOCK, LANES), F32)
                for odd, (k_op, v_op) in enumerate(((k_lo[kvh][keys], v_lo[kvh][keys]), (k_hi[kvh][keys], v_hi[kvh][keys]))):
                    ha, hb = 2 * pairs[0] + odd, 2 * pairs[1] + odd
                    p, p_sink = _probs(q2, k_op, rel, valid, _row_const(SLOPES[ha], SLOPES[hb]),
                                       _row_const(sink_ref[ha], sink_ref[hb]))
                    dp = _dot_nt(do2, v_op)
                    delta = jnp.sum(p * dp, axis=-1, keepdims=True)
                    ds = (p * (dp - delta) * (1.0 / math.sqrt(HEAD_DIM))).astype(BF16)
                    dsk = p_sink * delta
                    dsink = dsink - jnp.where(lane_id == ha, jnp.sum(dsk[0:ATT_BLOCK]), 0.0) \
                        - jnp.where(lane_id == hb, jnp.sum(dsk[ATT_BLOCK:]), 0.0)
                    dq2 = dq2 + _dot(ds, k_op)
                    half = lo if odd == 0 else jnp.logical_not(lo)
                    dk_acc[kvh, keys, :] += jnp.where(half, _dot_tn(ds, q2), 0.0)
                    dv_acc[kvh, keys, :] += jnp.where(half, _dot_tn(p.astype(BF16), do2), 0.0)
                dgq = dgq + jnp.sum(dq2 * qn_pre, axis=0, keepdims=True)
                dq_raw = _head_norm_bwd(dq2 * gq, qn_pre, rq, lo)
                for n, p_ in enumerate(pairs):
                    blk = dq_raw[n * ATT_BLOCK:(n + 1) * ATT_BLOCK]
                    dqkv_ref[rows, p_ * LANES:(p_ + 1) * LANES] = blk.astype(BF16)
                    dbqkv_ref[:, p_ * LANES:(p_ + 1) * LANES] += jnp.sum(blk, axis=0, keepdims=True)
        carry_k[...] = dk_acc[:, 0:ATT_BLOCK, :]
        carry_v[...] = dv_acc[:, 0:ATT_BLOCK, :]

        def fold(acc_ref):
            both = []
            for kvh in range(N_KV_HEADS):
                a = acc_ref[kvh, ATT_BLOCK:ATT_BLOCK + tq, :]
                both.append(a + pltpu.roll(a, HEAD_DIM, 1))
            return jnp.where(lo, both[0], both[1])

        dkn = fold(dk_acc)
        dv = fold(dv_acc)
        kn_c, rk_c = kn_pre[ATT_BLOCK:], rk[ATT_BLOCK:]
        dgk_ref[...] += jnp.sum(dkn * kn_c, axis=0, keepdims=True)
        dk_raw = _head_norm_bwd(dkn * gk_ref[...], kn_c, rk_c, lo)
        dqkv_ref[:, Q_COLS:Q_COLS + KV_COLS] = dk_raw.astype(BF16)
        dqkv_ref[:, Q_COLS + KV_COLS:] = dv.astype(BF16)
        dbqkv_ref[:, Q_COLS:Q_COLS + KV_COLS] += jnp.sum(dk_raw, axis=0, keepdims=True)
        dbqkv_ref[:, Q_COLS + KV_COLS:] += jnp.sum(dv, axis=0, keepdims=True)
        dgq_ref[...] += dgq
        dsink_ref[...] += dsink

        @pl.when(i == nt - 1)
        def _():
            for ref in (dgq_ref, dgk_ref):
                v = ref[...]
                ref[...] = v + pltpu.roll(v, HEAD_DIM, 1)

    acc = lambda shape: pl.BlockSpec(shape, lambda i: (0,) * len(shape))
    return pl.pallas_call(
        body, grid=(nt,),
        in_specs=[pl.BlockSpec((tq, Q_COLS), lambda i: (nt - 1 - i, 0)),
                  pl.BlockSpec((tq, 2 * KV_COLS), lambda i: (nt - 1 - i, 2)),
                  pl.BlockSpec((ATT_BLOCK, 2 * KV_COLS), lambda i: (jnp.maximum((nt - 1 - i) * nb - 1, 0), 2)),
                  pl.BlockSpec((tq, Q_COLS), lambda i: (nt - 1 - i, 0)),
                  _resident((1, LANES)), _resident((1, LANES)), pl.BlockSpec(memory_space=pltpu.SMEM)] + s_in,
        out_specs=[pl.BlockSpec((tq, QKV_COLS), lambda i: (nt - 1 - i, 0)), acc((1, LANES)), acc((1, LANES)),
                   acc((1, LANES)), acc((1, QKV_COLS))] + s_out,
        out_shape=[jax.ShapeDtypeStruct((s, QKV_COLS), BF16), jax.ShapeDtypeStruct((1, LANES), F32),
                   jax.ShapeDtypeStruct((1, LANES), F32), jax.ShapeDtypeStruct((1, LANES), F32),
                   jax.ShapeDtypeStruct((1, QKV_COLS), F32)] + s_shape,
        scratch_shapes=[pltpu.VMEM((N_KV_HEADS, tq + ATT_BLOCK, LANES), F32), pltpu.VMEM((N_KV_HEADS, tq + ATT_BLOCK, LANES), F32),
                        pltpu.VMEM((N_KV_HEADS, ATT_BLOCK, LANES), F32), pltpu.VMEM((N_KV_HEADS, ATT_BLOCK, LANES), F32)] + s_scratch,
        compiler_params=_cparams("arbitrary"), name="attn_bwd")(qkv, qkv, qkv, dmixed, gq2, gk2, sinks, *g8s)


def _in_bwd(dqkv, dcin, w_in_t, x, dx2, g_mix):
    s = x.shape[0]
    tm = _wide_tile(s)

    def body(dq_ref, dc_ref, w_ref, x_ref, dx2_ref, g_ref, gx_ref, dg_ref):
        @pl.when(pl.program_id(0) == 0)
        def _():
            dg_ref[...] = jnp.zeros(dg_ref.shape, F32)

        dh = _dot(dq_ref[...], w_ref[0:QKV_COLS, :]) + _dot(dc_ref[...], w_ref[QKV_COLS:, :])
        xv = x_ref[...]
        r = lax.rsqrt(jnp.mean(xv * xv, axis=-1, keepdims=True) + EPS)
        n = xv * r
        dg_ref[...] += jnp.sum(dh * n, axis=0, keepdims=True)
        dn = dh * g_ref[...]
        gx_ref[...] = dx2_ref[...] + r * (dn - n * jnp.mean(dn * n, axis=-1, keepdims=True))

    return pl.pallas_call(
        body, grid=(s // tm,),
        in_specs=[pl.BlockSpec((tm, QKV_COLS), lambda i: (i, 0)), pl.BlockSpec((tm, CIN_COLS), lambda i: (i, 0)),
                  _resident((QKV_COLS + CIN_COLS, D_MODEL)),
                  pl.BlockSpec((tm, D_MODEL), lambda i: (i, 0)), pl.BlockSpec((tm, D_MODEL), lambda i: (i, 0)),
                  _resident((1, D_MODEL))],
        out_specs=[pl.BlockSpec((tm, D_MODEL), lambda i: (i, 0)), pl.BlockSpec((1, D_MODEL), lambda i: (0, 0))],
        out_shape=[jax.ShapeDtypeStruct((s, D_MODEL), F32), jax.ShapeDtypeStruct((1, D_MODEL), F32)],
        compiler_params=_cparams("arbitrary"), name="in_bwd")(dqkv, dcin, w_in_t, x, dx2, g_mix)


def _tn_matmul(a, b, name, tokens):
    ga, s, m = a.shape
    gb, _, n = b.shape
    g = max(ga, gb)
    tk = min(tokens, s)

    def body(a_ref, b_ref, o_ref):
        @pl.when(pl.program_id(1) == 0)
        def _():
            o_ref[...] = jnp.zeros(o_ref.shape, F32)

        o_ref[0] += _dot_tn(a_ref[0].astype(BF16), b_ref[0].astype(BF16))

    return pl.pallas_call(
        body, grid=(g, s // tk),
        in_specs=[pl.BlockSpec((1, tk, m), (lambda gi, k: (gi, k, 0)) if ga > 1 else (lambda gi, k: (0, k, 0))),
                  pl.BlockSpec((1, tk, n), (lambda gi, k: (gi, k, 0)) if gb > 1 else (lambda gi, k: (0, k, 0)))],
        out_specs=pl.BlockSpec((1, m, n), lambda gi, k: (gi, 0, 0)),
        out_shape=jax.ShapeDtypeStruct((g, m, n), F32),
        compiler_params=_cparams("parallel", "arbitrary"), name=name)(a, b)


def _tn_matmul_pair(a0, a1, b, name):
    s, m0 = a0.shape
    m1, n = a1.shape[1], b.shape[1]
    tk = min(TN_TOKENS, s)

    def body(a0_ref, a1_ref, b_ref, o_ref):
        @pl.when(pl.program_id(0) == 0)
        def _():
            o_ref[...] = jnp.zeros(o_ref.shape, F32)

        bv = b_ref[...].astype(BF16)
        o_ref[0:m0, :] += _dot_tn(a0_ref[...].astype(BF16), bv)
        o_ref[m0:, :] += _dot_tn(a1_ref[...].astype(BF16), bv)

    return pl.pallas_call(
        body, grid=(s // tk,),
        in_specs=[pl.BlockSpec((tk, m0), lambda k: (k, 0)), pl.BlockSpec((tk, m1), lambda k: (k, 0)),
                  pl.BlockSpec((tk, n), lambda k: (k, 0))],
        out_specs=pl.BlockSpec((m0 + m1, n), lambda k: (0, 0)),
        out_shape=jax.ShapeDtypeStruct((m0 + m1, n), F32),
        compiler_params=_cparams("arbitrary"), name=name)(a0, a1, b)


def _allgather(shards, dtypes):
    n = len(shards)
    n_copies = 1 + 2 * len(OTHER_CHIPS)

    def body(*refs):
        ins, outs = refs[:n], refs[n:2 * n]
        send_sems, recv_sems = refs[2 * n:]
        x, y, c = _position()
        me, sibling = (x, y, c), (x, y, 1 - c)
        chips = [(_flip(x, fx), _flip(y, fy)) for fx, fy in OTHER_CHIPS]
        for a in range(n):
            outs[a][_dev_index(*me)] = ins[a][...].astype(dtypes[a])

        def copy(a, k, block, to):
            rows = outs[a].at[_dev_index(*block)]
            return pltpu.make_async_remote_copy(src_ref=rows, dst_ref=rows, send_sem=send_sems.at[a, k],
                                                recv_sem=recv_sems.at[a, k], device_id=to, device_id_type=MESH)

        started = []
        for a in range(n):
            for j, chip in enumerate(chips):
                started.append(copy(a, 1 + j, me, (*chip, c)))
            started.append(copy(a, 0, me, sibling))
        for cp in started:
            cp.start()
        for a in range(n):
            for j, chip in enumerate(chips):
                copy(a, 1 + j, (*chip, c), me).wait_recv()
                fwd = copy(a, 1 + len(chips) + j, (*chip, c), sibling)
                fwd.start()
                started.append(fwd)
        for a in range(n):
            copy(a, 0, sibling, me).wait_recv()
            for j, chip in enumerate(chips):
                copy(a, 1 + len(chips) + j, (*chip, 1 - c), me).wait_recv()
        for cp in started:
            cp.wait_send()

    vmem = pl.BlockSpec(memory_space=pltpu.VMEM)
    return pl.pallas_call(
        body, in_specs=[vmem] * n, out_specs=[vmem] * n,
        out_shape=[jax.ShapeDtypeStruct((N_DEV,) + w.shape, dt) for w, dt in zip(shards, dtypes)],
        scratch_shapes=[pltpu.SemaphoreType.DMA((n, n_copies)), pltpu.SemaphoreType.DMA((n, n_copies))],
        compiler_params=pltpu.CompilerParams(vmem_limit_bytes=VMEM_LIMIT), name="allgather_weights")(*shards)


def _final_exchange(g8, v):
    rows = v.shape[0]
    n_chips = 1 + len(OTHER_CHIPS)
    _, _, s_shape, s_scratch = _scatter_specs([g8])

    def body(g_ref, v_ref, gout_ref, vout_ref, from_sibling, chip_sums, send_sems, recv_sems, *rs_scratch):
        scatter = _ReduceScatter(g_ref, gout_ref, *rs_scratch)
        x, y, c = _position()
        my_chip = 2 * x + y
        chips = [(_flip(x, fx), _flip(y, fy)) for fx, fy in OTHER_CHIPS]

        def swap():
            return pltpu.make_async_remote_copy(src_ref=v_ref, dst_ref=from_sibling, send_sem=send_sems.at[0],
                                                recv_sem=recv_sems.at[0], device_id=(x, y, 1 - c), device_id_type=MESH)

        def push(j):
            return pltpu.make_async_remote_copy(src_ref=chip_sums.at[my_chip], dst_ref=chip_sums.at[my_chip],
                                                send_sem=send_sems.at[1 + j], recv_sem=recv_sems.at[1 + j],
                                                device_id=(*chips[j], c), device_id_type=MESH)

        swap().start()
        scatter.start()
        swap().wait_recv()
        chip_sums[my_chip] = v_ref[...] + from_sibling[...]
        for j in range(len(chips)):
            push(j).start()
        scatter.middle()
        for j in range(len(chips)):
            push(j).wait_recv()
        swap().wait_send()
        for j in range(len(chips)):
            push(j).wait_send()
        total = chip_sums[0]
        for q in range(1, n_chips):
            total = total + chip_sums[q]
        vout_ref[...] = total
        scatter.finish()

    vmem = pl.BlockSpec(memory_space=pltpu.VMEM)
    return pl.pallas_call(
        body, in_specs=[pl.BlockSpec(memory_space=pl.ANY), vmem], out_specs=[vmem, vmem],
        out_shape=s_shape + [jax.ShapeDtypeStruct((rows, LANES), F32)],
        scratch_shapes=[pltpu.VMEM((rows, LANES), F32), pltpu.VMEM((n_chips, rows, LANES), F32),
                        pltpu.SemaphoreType.DMA((n_chips,)), pltpu.SemaphoreType.DMA((n_chips,))] + s_scratch,
        compiler_params=pltpu.CompilerParams(vmem_limit_bytes=VMEM_LIMIT), name="final_exchange")(g8, v)


def _adam_math(wv, gv, mv, vv):
    mn = ADAM_B1 * mv + (1.0 - ADAM_B1) * gv
    vn = ADAM_B2 * vv + (1.0 - ADAM_B2) * (gv * gv)
    m_hat = mn / (1.0 - ADAM_B1 ** ADAM_STEP)
    v_hat = vn / (1.0 - ADAM_B2 ** ADAM_STEP)
    return -ADAM_LR * (m_hat / (jnp.sqrt(v_hat) + ADAM_EPS) + ADAM_WD * wv), mn, vn


ADAM_ROW_BLOCKS = 4


def _adamw(ws, gs, ms, vs):
    n = len(ws)
    assert all(w.shape[0] % (ADAM_ROW_BLOCKS * SUBLANES) == 0 for w in ws)

    def body(*refs):
        ins, outs = refs[:4 * n], refs[4 * n:]
        for i in range(n):
            w_ref, g_ref, m_ref, v_ref = ins[4 * i:4 * i + 4]
            d_ref, mo_ref, vo_ref = outs[3 * i:3 * i + 3]
            d_ref[...], mo_ref[...], vo_ref[...] = _adam_math(w_ref[...], g_ref[...], m_ref[...], v_ref[...])

    specs = [pl.BlockSpec((w.shape[0] // ADAM_ROW_BLOCKS, w.shape[1]), lambda i: (i, 0)) for w in ws]
    outs = pl.pallas_call(
        body, grid=(ADAM_ROW_BLOCKS,), in_specs=[s for s in specs for _ in range(4)],
        out_specs=[s for s in specs for _ in range(3)],
        out_shape=[jax.ShapeDtypeStruct(w.shape, F32) for w in ws for _ in range(3)],
        compiler_params=_cparams("arbitrary"), name="adamw_big")(*[a for four in zip(ws, gs, ms, vs) for a in four])
    return [tuple(outs[3 * i:3 * i + 3]) for i in range(n)]


FW_ROWS = 24
CW_ROWS = 32
R_FW = 0
R_FB = R_FW + N_DEV * FW_ROWS
R_CW = R_FB + 48
R_BQKV = R_CW + (CONV_WIDTH // LANES) * CW_ROWS
R_BCIN = R_BQKV + 8
R_GMIX = R_BCIN + 8
R_BOUT = R_GMIX + 8
R_GFFN = R_BOUT + 8
R_CB = R_GFFN + 8
R_CGAIN = R_CB + 8
R_CBIAS = R_CGAIN + 8
R_QKS = R_CBIAS + 8
SMALL_ROWS = R_QKS + 8


def _pack_small(raw):
    def rows(a, n):
        a = a.reshape(-1, LANES)
        return jnp.pad(a, ((0, n - a.shape[0]), (0, 0)))

    fw = jnp.pad(raw["dfw"].reshape(N_DEV, -1, LANES), ((0, 0), (0, FW_ROWS - 3 * FF_LANE_CHUNKS), (0, 0)))
    cw = jnp.pad(raw["dcw"].reshape(CONV_KERNEL, -1, LANES).transpose(1, 0, 2), ((0, 0), (0, CW_ROWS - CONV_KERNEL), (0, 0)))
    qks = jnp.concatenate([raw["dgq"], raw["dgk"], raw["dsink"], jnp.pad(raw["loss"], ((0, 0), (0, LANES - 1)))], axis=0)
    return jnp.concatenate([
        fw.reshape(-1, LANES), rows(raw["dfb"][:, 0, :FF_CHUNK], 48), cw.reshape(-1, LANES), rows(raw["dbqkv"], 8),
        rows(raw["dbcin"], 8), rows(raw["dg_mix"], 8), rows(raw["db_out"], 8), rows(raw["dg_ffn"], 8), rows(raw["dcb"], 8),
        rows(raw["dcgain"], 8), rows(raw["dcbias"], 8), rows(qks, 8)], axis=0)


def _adamw_small(gpack, w, m, v):
    n = len(SMALL)
    ix = {name: i for i, name in enumerate(SMALL)}

    def body(g_ref, *refs):
        w_refs, m_refs, v_refs, outs = refs[:n], refs[n:2 * n], refs[2 * n:3 * n], refs[3 * n:]
        d = _dev_index(*_position())

        def step(name, idx, gv):
            i = ix[name]
            delta, mn, vn = _adam_math(w_refs[i][idx], gv, m_refs[i][idx], v_refs[i][idx])
            for ref, val in zip(outs[4 * i:4 * i + 4], (gv, delta, mn, vn)):
                ref[idx] = val

        def whole(name, row, nrows):
            step(name, (slice(None), slice(None)), g_ref[row:row + nrows, :])

        whole("mix_norm_gain", R_GMIX, 8)
        whole("b_out", R_BOUT, 8)
        whole("ffn_norm_gain", R_GFFN, 8)
        whole("conv_dw_b", R_CB, 4)
        whole("conv_norm_gain", R_CGAIN, 4)
        whole("conv_norm_bias", R_CBIAS, 4)
        whole("ffn_dw_b", R_FB, 2 * D_FF // LANES)
        nq = QKV_COLS // LANES
        step("b_in", (slice(0, nq), slice(None)), g_ref[R_BQKV:R_BQKV + nq, :])
        step("b_in", (slice(nq, nq + CIN_COLS // LANES), slice(None)), g_ref[R_BCIN:R_BCIN + CIN_COLS // LANES, :])
        step("q_norm_gain", (slice(None), slice(None)), g_ref[R_QKS:R_QKS + 1, 0:HEAD_DIM])
        step("k_norm_gain", (slice(None), slice(None)), g_ref[R_QKS + 1:R_QKS + 2, 0:HEAD_DIM])
        step("attn_sinks", (slice(None), slice(None)), g_ref[R_QKS + 2:R_QKS + 3, 0:N_Q_HEADS])
        blk = g_ref[pl.ds(pl.multiple_of(R_CW + CW_ROWS * lax.shift_right_logical(d, 1), SUBLANES), CW_ROWS), :]
        blk = jnp.where((d & 1) == 1, pltpu.roll(blk, HEAD_DIM, 1), blk)
        step("conv_dw_w", (slice(None), slice(None)), blk[0:CONV_KERNEL, 0:CONV_WIDTH // N_DEV])
        blk = g_ref[pl.ds(pl.multiple_of(R_FW + FW_ROWS * d, SUBLANES), FW_ROWS), :]
        for k in range(3):
            for j in range(FF_LANE_CHUNKS):
                wd = min(LANES, FF_CHUNK - j * LANES)
                row = k * FF_LANE_CHUNKS + j
                step("ffn_dw_w", (slice(k, k + 1), slice(j * LANES, j * LANES + wd)), blk[row:row + 1, 0:wd])

    vmem = pl.BlockSpec(memory_space=pltpu.VMEM)
    args = [gpack] + [d[name] for d in (w, m, v) for name in SMALL]
    outs = pl.pallas_call(
        body, in_specs=[vmem] * len(args), out_specs=[vmem] * (4 * n),
        out_shape=[jax.ShapeDtypeStruct(w[name].shape, F32) for name in SMALL for _ in range(4)],
        compiler_params=pltpu.CompilerParams(vmem_limit_bytes=VMEM_LIMIT), name="adamw_small")(*args)
    return {name: outs[4 * i:4 * i + 4] for i, name in enumerate(SMALL)}


def _token_mixing(x, p, attn_shards, conv_shards):
    qkv, cin, h1 = _mix_proj(x, p["g_mix"], p["w_in_t"], p["b_qkv"], p["b_cin"])
    attn, *from_attn = _attn_fwd(qkv, p["gq2"], p["gk2"], p["sinks"], attn_shards)
    c3, c1, *from_conv = _conv_fwd(cin, p["cw8"], p["cb"], p["cgain"], p["cbias"], conv_shards)
    return (qkv, cin, h1, attn, c3, c1), from_attn, from_conv


def _rest_of_step(x, target, p, saved, scatter):
    s = x.shape[0]
    qkv, cin, h1, attn, c3, c1 = saved
    cw8, w_out, w_up, w_down = p["cw8"], p["w_out"], p["w_up"], p["w_down"]
    x2, h2 = _out_proj(x, attn, c3, w_out, w_out, p["b_out"], p["g_ffn"])
    fw, fb = p["fw"], p["fb"]
    up0, gu, act, dy, dyb, loss = _ffn_fwd(h2, x2, target, w_up, fw, fb, w_down)
    dup0, dh2, dfw, dfb = _ffn_bwd(dyb, up0, gu, w_up, fw, w_down)
    dx2, dmixed, dg_ffn, db_out, dw_out = _ffn_norm_bwd(dh2, dy, x2, p["g_ffn"], w_out, attn, c3)
    dw_out = dw_out.reshape(N_DEV, -1, D_MODEL)
    dw_up = _tn_matmul(dup0.reshape(N_DEV, s, FF_CHUNK), h2[None], "dw_up", 2 * TN_TOKENS)
    dw_down = _tn_matmul(act, dyb[None], "dw_down", 2 * TN_TOKENS).reshape(N_DEV, -1, D_MODEL)
    dcin, dcw, dcb, dcgain, dcbias, dbcin, *g_up = _conv_bwd(dmixed, c1, cin, cw8, p["cgain"], p["cbias"], [dw_up] if scatter else [])
    dqkv, dgq, dgk, dsink, dbqkv, *g_down_out = _attn_bwd(qkv, dmixed, p["gq2"], p["gk2"], p["sinks"],
                                                          [dw_down, dw_out] if scatter else [])
    dw_in = _tn_matmul_pair(dqkv, dcin, h1, "dw_in").reshape(N_DEV, -1, D_MODEL)
    grad_x, dg_mix = _in_bwd(dqkv, dcin, p["w_in_t"], x, dx2, p["g_mix"])
    if scatter:
        big = {"w_up": g_up[0], "w_down": g_down_out[0], "w_in": dw_in, "w_out": g_down_out[1]}
    else:
        big = {"w_up": dw_up, "w_down": dw_down, "w_in": dw_in, "w_out": dw_out}
    small = dict(dg_mix=dg_mix, dbqkv=dbqkv, dbcin=dbcin, dgq=dgq, dgk=dgk, dsink=dsink, dcw=dcw, dcb=dcb, dcgain=dcgain,
                 dcbias=dcbias, db_out=db_out, dg_ffn=dg_ffn, dfw=dfw, dfb=dfb, loss=loss)
    return loss, grad_x, big, small


BIG = ("w_in", "w_out", "w_up", "w_down")
SMALL = ("mix_norm_gain", "b_in", "q_norm_gain", "k_norm_gain", "attn_sinks", "conv_dw_w", "conv_dw_b",
         "conv_norm_gain", "conv_norm_bias", "b_out", "ffn_norm_gain", "ffn_dw_w", "ffn_dw_b")
ORDER = ("mix_norm_gain", "w_in", "b_in", "q_norm_gain", "k_norm_gain", "attn_sinks", "conv_dw_w", "conv_dw_b",
         "conv_norm_gain", "conv_norm_bias", "w_out", "b_out", "ffn_norm_gain", "w_up", "ffn_dw_w", "ffn_dw_b", "w_down")


def kernel(x, mix_norm_gain, w_in, b_in, q_norm_gain, k_norm_gain, attn_sinks, conv_dw_w, conv_dw_b, conv_norm_gain, conv_norm_bias, w_out, b_out, ffn_norm_gain, w_up, ffn_dw_w, ffn_dw_b, w_down, loss_target, m_mix_norm_gain, m_w_in, m_b_in, m_q_norm_gain, m_k_norm_gain, m_attn_sinks, m_conv_dw_w, m_conv_dw_b, m_conv_norm_gain, m_conv_norm_bias, m_w_out, m_b_out, m_ffn_norm_gain, m_w_up, m_ffn_dw_w, m_ffn_dw_b, m_w_down, v_mix_norm_gain, v_w_in, v_b_in, v_q_norm_gain, v_k_norm_gain, v_attn_sinks, v_conv_dw_w, v_conv_dw_b, v_conv_norm_gain, v_conv_norm_bias, v_w_out, v_b_out, v_ffn_norm_gain, v_w_up, v_ffn_dw_w, v_ffn_dw_b, v_w_down):
    w = dict(mix_norm_gain=mix_norm_gain, w_in=w_in, b_in=b_in, q_norm_gain=q_norm_gain, k_norm_gain=k_norm_gain,
             attn_sinks=attn_sinks, conv_dw_w=conv_dw_w, conv_dw_b=conv_dw_b, conv_norm_gain=conv_norm_gain,
             conv_norm_bias=conv_norm_bias, w_out=w_out, b_out=b_out, ffn_norm_gain=ffn_norm_gain, w_up=w_up,
             ffn_dw_w=ffn_dw_w, ffn_dw_b=ffn_dw_b, w_down=w_down)
    m = dict(mix_norm_gain=m_mix_norm_gain, w_in=m_w_in, b_in=m_b_in, q_norm_gain=m_q_norm_gain, k_norm_gain=m_k_norm_gain,
             attn_sinks=m_attn_sinks, conv_dw_w=m_conv_dw_w, conv_dw_b=m_conv_dw_b, conv_norm_gain=m_conv_norm_gain,
             conv_norm_bias=m_conv_norm_bias, w_out=m_w_out, b_out=m_b_out, ffn_norm_gain=m_ffn_norm_gain, w_up=m_w_up,
             ffn_dw_w=m_ffn_dw_w, ffn_dw_b=m_ffn_dw_b, w_down=m_w_down)
    v = dict(mix_norm_gain=v_mix_norm_gain, w_in=v_w_in, b_in=v_b_in, q_norm_gain=v_q_norm_gain, k_norm_gain=v_k_norm_gain,
             attn_sinks=v_attn_sinks, conv_dw_w=v_conv_dw_w, conv_dw_b=v_conv_dw_b, conv_norm_gain=v_conv_norm_gain,
             conv_norm_bias=v_conv_norm_bias, w_out=v_w_out, b_out=v_b_out, ffn_norm_gain=v_ffn_norm_gain, w_up=v_w_up,
             ffn_dw_w=v_ffn_dw_w, ffn_dw_b=v_ffn_dw_b, w_down=v_w_down)
    s = x.shape[1]

    wi8, cw8, fw8 = _allgather([w_in.T, conv_dw_w, ffn_dw_w], [BF16, F32, F32])
    lane_pad = ((0, 0), (0, 0), (0, FF_PADDED - FF_CHUNK))
    p = {
        "g_mix": mix_norm_gain.reshape(1, -1), "w_in_t": wi8.reshape(QKV_COLS + CIN_COLS, D_MODEL),
        "b_qkv": b_in[:QKV_COLS].reshape(1, -1), "b_cin": b_in[QKV_COLS:].reshape(1, -1),
        "gq2": jnp.tile(q_norm_gain, 2).reshape(1, -1), "gk2": jnp.tile(k_norm_gain, 2).reshape(1, -1), "sinks": attn_sinks,
        "cw8": jnp.repeat(cw8.transpose(1, 0, 2).reshape(CONV_KERNEL, CONV_WIDTH), SUBLANES, axis=0),
        "cb": conv_dw_b.reshape(1, -1), "cgain": conv_norm_gain.reshape(1, -1), "cbias": conv_norm_bias.reshape(1, -1),
        "b_out": b_out.reshape(1, -1), "g_ffn": ffn_norm_gain.reshape(1, -1),
        "fw": jnp.pad(fw8, lane_pad), "fb": jnp.pad(ffn_dw_b.reshape(N_DEV, 1, FF_CHUNK), lane_pad),
    }

    saved, (wu8,), (wo8, wd8) = _token_mixing(x[0], p, [w_up.T], [w_out, w_down])
    p.update(w_out=wo8.reshape(D_MODEL, D_MODEL), w_up=wu8, w_down=wd8.reshape(N_FF_PAIRS, FF_CHUNK, D_MODEL))
    loss, grad_x, big, small = _rest_of_step(x[0], loss_target[0], p, saved, True)

    g = dict(big)
    g["w_in"], gpack = _final_exchange(big["w_in"], _pack_small(small))

    delta, new_m, new_v = {}, {}, {}
    transposed = ("w_in", "w_up")
    big_out = _adamw(*[[d[n].T if n in transposed and d is not g else d[n] for n in BIG] for d in (w, g, m, v)])
    for n, outs in zip(BIG, big_out):
        if n in transposed:
            g[n], delta[n], new_m[n], new_v[n] = g[n].T, *[o.T for o in outs]
        else:
            delta[n], new_m[n], new_v[n] = outs

    def view(a):
        return a if a.ndim == 2 else (a.reshape(-1, LANES) if a.size % LANES == 0 else a.reshape(1, -1))

    small_out = _adamw_small(gpack, *[{n: view(d[n]) for n in SMALL} for d in (w, m, v)])
    for n in SMALL:
        g[n], delta[n], new_m[n], new_v[n] = [a.reshape(w[n].shape) for a in small_out[n]]

    total = gpack[R_QKS + 3, 0]
    return (total, grad_x.reshape(1, s, D_MODEL), *[g[n] for n in ORDER], *[delta[n] for n in ORDER],
            *[new_m[n] for n in ORDER], *[new_v[n] for n in ORDER])
```

```python
import functools
import math

import jax
import jax.numpy as jnp
from jax import lax
from jax.experimental import pallas as pl
from jax.experimental.pallas import tpu as pltpu

F32 = jnp.float32
BF16 = jnp.bfloat16

D_MODEL = 1024
HEAD_DIM = 64
N_Q_HEADS = 8
N_KV_HEADS = 2
Q_COLS = 512
KV_COLS = 128
QKV_COLS = Q_COLS + 2 * KV_COLS
CONV_WIDTH = 512
CIN_COLS = 2 * CONV_WIDTH
CONV_KERNEL = 31
CONV_HALO = 32
D_FF = 2816
N_DEV = 8
FF_CHUNK = 2 * D_FF // N_DEV
N_FF_PAIRS = N_DEV // 2
ATT_BLOCK = 128
EPS = 1e-6
NEG_INF = -1e30
SLOPES = [float(2.0 ** (-8.0 * (h + 1.0) / N_Q_HEADS)) for h in range(N_Q_HEADS)]

ADAM_LR = 0.001
ADAM_B1 = 0.9
ADAM_B2 = 0.999
ADAM_EPS = 1e-08
ADAM_WD = 0.01
ADAM_STEP = 10

LANES = 128
SUBLANES = 8
VMEM_LIMIT = 56 * 1024 * 1024
MESH = pl.DeviceIdType.MESH


def _cparams(*sem, **kw):
    return pltpu.CompilerParams(dimension_semantics=sem or None, vmem_limit_bytes=VMEM_LIMIT, **kw)


def _resident(shape):
    nd = len(shape)
    return pl.BlockSpec(shape, lambda *_: (0,) * nd, pipeline_mode=pl.Buffered(1))


def _dot(a, b):
    return jnp.dot(a, b, preferred_element_type=F32)


def _dot_nt(a, b):
    return lax.dot_general(a, b, (((1,), (1,)), ((), ())), preferred_element_type=F32)


def _dot_tn(a, b):
    return lax.dot_general(a, b, (((0,), (0,)), ((), ())), preferred_element_type=F32)


def _sigmoid(x):
    return 1.0 / (1.0 + jnp.exp(-x))


def _lo_mask(shape):
    return lax.broadcasted_iota(jnp.int32, shape, len(shape) - 1) % LANES < HEAD_DIM


def _half_sums(t, lo):
    s_lo = jnp.sum(jnp.where(lo, t, 0.0), axis=-1, keepdims=True)
    s_hi = jnp.sum(jnp.where(lo, 0.0, t), axis=-1, keepdims=True)
    return jnp.where(lo, s_lo, s_hi)


def _head_norm(t, lo):
    r = lax.rsqrt(_half_sums(t * t, lo) * (1.0 / HEAD_DIM) + EPS)
    return t * r, r


def _head_norm_bwd(dn, n, r, lo):
    return r * (dn - n * (_half_sums(dn * n, lo) * (1.0 / HEAD_DIM)))


def _tile(s):
    return min(512, s)


def _wide_tile(s):
    return min(1024, s)


TN_TOKENS = 2048
FF_COLS = ((0, 256), (256, 512), (512, 704))


def _position():
    return lax.axis_index("x"), lax.axis_index("y"), lax.axis_index("c")


def _dev_index(px, py, pc):
    return 4 * px + 2 * py + pc


def _flip(v, bit):
    return 1 - v if bit else v


OTHER_CHIPS = ((1, 0), (0, 1), (1, 1))
N_GATHER_COPIES = 1 + 2 * len(OTHER_CHIPS)


class _Gather:
    def __init__(self, shard_ref, out_ref, cast_buf, send_sems, recv_sems, local_sem):
        self.shard, self.out, self.buf = shard_ref, out_ref, cast_buf
        self.send_sems, self.recv_sems, self.local_sem = send_sems, recv_sems, local_sem
        x, y, c = _position()
        self.c = c
        self.me, self.sibling = (x, y, c), (x, y, 1 - c)
        self.chips = [(_flip(x, fx), _flip(y, fy)) for fx, fy in OTHER_CHIPS]

    def _copy(self, k, block, to, from_buf=False):
        rows = self.out.at[_dev_index(*block)]
        return pltpu.make_async_remote_copy(src_ref=self.buf if from_buf else rows, dst_ref=rows,
                                            send_sem=self.send_sems.at[k], recv_sem=self.recv_sems.at[k],
                                            device_id=to, device_id_type=MESH)

    def _local(self):
        return pltpu.make_async_copy(self.buf, self.out.at[_dev_index(*self.me)], self.local_sem)

    def start(self):
        self.buf[...] = self.shard[...].astype(self.buf.dtype)
        self._local().start()
        for j, chip in enumerate(self.chips):
            self._copy(1 + j, self.me, (*chip, self.c), from_buf=True).start()
        self._copy(0, self.me, self.sibling, from_buf=True).start()

    def forward(self):
        for j, chip in enumerate(self.chips):
            self._copy(1 + j, (*chip, self.c), self.me).wait_recv()
            self._copy(1 + len(self.chips) + j, (*chip, self.c), self.sibling).start()

    def finish(self):
        self._copy(0, self.sibling, self.me).wait_recv()
        for j, chip in enumerate(self.chips):
            self._copy(1 + len(self.chips) + j, (*chip, 1 - self.c), self.me).wait_recv()
        for k in range(N_GATHER_COPIES):
            self._copy(k, self.me, self.sibling).wait_send()
        self._local().wait()


def _gather_specs(shards):
    whole = [pl.BlockSpec(w.shape, lambda *_, nd=w.ndim: (0,) * nd, pipeline_mode=pl.Buffered(1)) for w in shards]
    outs = [pl.BlockSpec(memory_space=pl.ANY) for _ in shards]
    shapes = [jax.ShapeDtypeStruct((N_DEV,) + w.shape, BF16) for w in shards]
    scratch = []
    for w in shards:
        scratch += [pltpu.VMEM(w.shape, BF16), pltpu.SemaphoreType.DMA((N_GATHER_COPIES,)),
                    pltpu.SemaphoreType.DMA((N_GATHER_COPIES,)), pltpu.SemaphoreType.DMA(())]
    return whole, outs, shapes, scratch


def _run_gathers(gathers, step, n_steps):
    @pl.when(step == 0)
    def _():
        for g in gathers:
            g.start()

    @pl.when(step == 3 * n_steps // 4)
    def _():
        for g in gathers:
            g.forward()

    @pl.when(step == n_steps - 1)
    def _():
        for g in gathers:
            g.finish()


class _ReduceScatter:
    def __init__(self, g_ref, out_ref, stage, load_sems, send_a, recv_a, send_b, recv_b, sa_send, sa_recv, sb_send, sb_recv):
        self.g, self.out, self.stage, self.load_sems = g_ref, out_ref, stage, load_sems
        self.send_a, self.recv_a, self.send_b, self.recv_b = send_a, recv_a, send_b, recv_b
        self.sems = (sa_send, sa_recv, sb_send, sb_recv)
        x, y, c = _position()
        self.c, self.sibling = c, (x, y, 1 - c)
        self.chips = [(x, y)] + [(_flip(x, fx), _flip(y, fy)) for fx, fy in OTHER_CHIPS]

    def _copy_a(self, j):
        return pltpu.make_async_remote_copy(src_ref=self.send_a.at[j], dst_ref=self.recv_a.at[j], send_sem=self.sems[0].at[j],
                                            recv_sem=self.sems[1].at[j], device_id=self.sibling, device_id_type=MESH)

    def _copy_b(self, j):
        return pltpu.make_async_remote_copy(src_ref=self.send_b.at[j], dst_ref=self.recv_b.at[j], send_sem=self.sems[2].at[j],
                                            recv_sem=self.sems[3].at[j], device_id=(*self.chips[1 + j], self.c),
                                            device_id_type=MESH)

    def _load(self, j, core):
        return pltpu.make_async_copy(self.g.at[_dev_index(*self.chips[j], core)], self.stage.at[j % 2], self.load_sems.at[j % 2])

    def send_block(self, j):
        if j == 0:
            self._load(0, 1 - self.c).start()
        self._load(j, 1 - self.c).wait()
        if j + 1 < len(self.chips):
            self._load(j + 1, 1 - self.c).start()
        else:
            self._load(0, self.c).start()
        self.send_a[j] = self.stage[j % 2].astype(BF16)
        self._copy_a(j).start()

    def sum_block(self, j):
        self._load(j, self.c).wait()
        if j + 1 < len(self.chips):
            self._load(j + 1, self.c).start()
        self._copy_a(j).wait_recv()
        part = self.stage[j % 2] + self.recv_a[j].astype(F32)
        if j == 0:
            self.out[...] = part
        else:
            self.send_b[j - 1] = part.astype(BF16)
            self._copy_b(j - 1).start()

    def start(self):
        for j in range(len(self.chips)):
            self.send_block(j)

    def middle(self):
        for j in range(len(self.chips)):
            self.sum_block(j)

    def finish(self):
        for j in range(len(OTHER_CHIPS)):
            self._copy_b(j).wait_recv()
            self.out[...] += self.recv_b[j].astype(F32)
        for j in range(len(self.chips)):
            self._copy_a(j).wait_send()
        for j in range(len(OTHER_CHIPS)):
            self._copy_b(j).wait_send()


N_SCATTER_SCRATCH = 10


def _scatter_specs(g8s):
    na, nb = 1 + len(OTHER_CHIPS), len(OTHER_CHIPS)
    ins = [pl.BlockSpec(memory_space=pl.ANY) for _ in g8s]
    outs = [pl.BlockSpec(g.shape[1:], lambda *_: (0, 0)) for g in g8s]
    shapes = [jax.ShapeDtypeStruct(g.shape[1:], F32) for g in g8s]
    scratch = []
    for g in g8s:
        blk = g.shape[1:]
        scratch += [pltpu.VMEM((2,) + blk, F32), pltpu.SemaphoreType.DMA((2,)), pltpu.VMEM((na,) + blk, BF16), pltpu.VMEM((na,) + blk, BF16),
                    pltpu.VMEM((nb,) + blk, BF16), pltpu.VMEM((nb,) + blk, BF16),
                    pltpu.SemaphoreType.DMA((na,)), pltpu.SemaphoreType.DMA((na,)),
                    pltpu.SemaphoreType.DMA((nb,)), pltpu.SemaphoreType.DMA((nb,))]
    return ins, outs, shapes, scratch


def _run_scatters(scatters, step, n_steps):
    n_blocks = 1 + len(OTHER_CHIPS)
    for j in range(n_blocks):
        @pl.when(step == min(j, n_steps - 1))
        def _(j=j):
            for r in scatters:
                r.send_block(j)

    for j in range(n_blocks):
        @pl.when(step == min(n_blocks + j, n_steps - 1))
        def _(j=j):
            for r in scatters:
                r.sum_block(j)

    @pl.when(step == n_steps - 1)
    def _():
        for r in scatters:
            r.finish()


def _mix_proj(x, g_mix, w_in_t, b_qkv, b_cin):
    s = x.shape[0]
    tm = _wide_tile(s)

    def body(x_ref, g_ref, w_ref, bq_ref, bc_ref, qkv_ref, cin_ref, h1_ref):
        xv = x_ref[...]
        r = lax.rsqrt(jnp.mean(xv * xv, axis=-1, keepdims=True) + EPS)
        h = (xv * r * g_ref[...]).astype(BF16)
        h1_ref[...] = h
        qkv_ref[...] = _dot_nt(h, w_ref[0:QKV_COLS, :]) + bq_ref[...]
        cin_ref[...] = _dot_nt(h, w_ref[QKV_COLS:, :]) + bc_ref[...]

    return pl.pallas_call(
        body, grid=(s // tm,),
        in_specs=[pl.BlockSpec((tm, D_MODEL), lambda i: (i, 0)), _resident((1, D_MODEL)),
                  _resident((QKV_COLS + CIN_COLS, D_MODEL)), _resident((1, QKV_COLS)), _resident((1, CIN_COLS))],
        out_specs=[pl.BlockSpec((tm, QKV_COLS), lambda i: (i, 0)), pl.BlockSpec((tm, CIN_COLS), lambda i: (i, 0)),
                   pl.BlockSpec((tm, D_MODEL), lambda i: (i, 0))],
        out_shape=[jax.ShapeDtypeStruct((s, QKV_COLS), F32), jax.ShapeDtypeStruct((s, CIN_COLS), F32),
                   jax.ShapeDtypeStruct((s, D_MODEL), BF16)],
        compiler_params=_cparams("parallel"), name="mix_proj")(x, g_mix, w_in_t, b_qkv, b_cin)


def _kv_variants(kv_all, gk2, lo):
    k_all = kv_all[:, :LANES]
    v_all = kv_all[:, LANES:]
    kn_pre, rk = _head_norm(k_all, lo)
    kn = kn_pre * gk2
    kr = pltpu.roll(kn, HEAD_DIM, 1)
    vr = pltpu.roll(v_all, HEAD_DIM, 1)
    zero = jnp.zeros_like(kn)
    k_lo = [jnp.where(lo, kn, zero).astype(BF16), jnp.where(lo, kr, zero).astype(BF16)]
    k_hi = [jnp.where(lo, zero, kr).astype(BF16), jnp.where(lo, zero, kn).astype(BF16)]
    v_lo = [jnp.where(lo, v_all, zero).astype(BF16), jnp.where(lo, vr, zero).astype(BF16)]
    v_hi = [jnp.where(lo, zero, vr).astype(BF16), jnp.where(lo, zero, v_all).astype(BF16)]
    return k_lo, k_hi, v_lo, v_hi, kn_pre, rk


def _att_consts(first_tile, b):
    rows = 2 * ATT_BLOCK
    qi = lax.broadcasted_iota(jnp.int32, (rows, 2 * ATT_BLOCK), 0) % ATT_BLOCK
    kj = lax.broadcasted_iota(jnp.int32, (rows, 2 * ATT_BLOCK), 1)
    rel = qi + ATT_BLOCK - kj
    valid = (rel >= 0) & (rel < ATT_BLOCK)
    if b == 0:
        valid = valid & ((kj >= ATT_BLOCK) | jnp.logical_not(first_tile))
    return rel.astype(F32), valid


def _row_const(va, vb):
    top = lax.broadcasted_iota(jnp.int32, (2 * ATT_BLOCK, 1), 0) < ATT_BLOCK
    return jnp.where(top, va, vb)


def _probs(q2, k_op, rel, valid, slope, sink):
    sc = _dot_nt(q2, k_op) * (1.0 / math.sqrt(HEAD_DIM)) - slope * rel
    sc = jnp.where(valid, sc, NEG_INF)
    m = jnp.maximum(jnp.max(sc, axis=-1, keepdims=True), sink)
    p = jnp.exp(sc - m)
    e_sink = jnp.exp(sink - m)
    inv = 1.0 / (jnp.sum(p, axis=-1, keepdims=True) + e_sink)
    return p * inv, e_sink * inv


def _attn_fwd(qkv, gq2, gk2, sinks, shards):
    s = qkv.shape[0]
    tq = _tile(s)
    nb = tq // ATT_BLOCK
    ng = len(shards)
    g_in, g_out, g_shape, g_scratch = _gather_specs(shards)

    def body(q_ref, kv_ref, kvp_ref, gq_ref, gk_ref, sink_ref, *rest):
        out_ref = rest[ng]
        i = pl.program_id(0)
        _run_gathers([_Gather(rest[a], rest[ng + 1 + a], *rest[2 * ng + 1 + 4 * a:2 * ng + 5 + 4 * a]) for a in range(ng)],
                     i, s // tq)
        lo = _lo_mask((1, LANES))
        kv_all = jnp.concatenate([kvp_ref[...], kv_ref[...]], axis=0)
        k_lo, k_hi, v_lo, v_hi, _, _ = _kv_variants(kv_all, gk_ref[...], lo)
        for b in range(nb):
            rel, valid = _att_consts(i == 0, b)
            rows = slice(b * ATT_BLOCK, (b + 1) * ATT_BLOCK)
            keys = slice(b * ATT_BLOCK, (b + 2) * ATT_BLOCK)
            for kvh in range(N_KV_HEADS):
                pairs = (2 * kvh, 2 * kvh + 1)
                q2 = jnp.concatenate([q_ref[rows, p * LANES:(p + 1) * LANES] for p in pairs], axis=0)
                qn, _ = _head_norm(q2, lo)
                q2 = (qn * gq_ref[...]).astype(BF16)
                out = None
                for odd, (k_op, v_op) in enumerate(((k_lo[kvh][keys], v_lo[kvh][keys]), (k_hi[kvh][keys], v_hi[kvh][keys]))):
                    ha, hb = 2 * pairs[0] + odd, 2 * pairs[1] + odd
                    p, _ = _probs(q2, k_op, rel, valid, _row_const(SLOPES[ha], SLOPES[hb]),
                                  _row_const(sink_ref[ha], sink_ref[hb]))
                    o = _dot(p.astype(BF16), v_op)
                    out = o if out is None else out + o
                for n, p in enumerate(pairs):
                    out_ref[rows, p * LANES:(p + 1) * LANES] = out[n * ATT_BLOCK:(n + 1) * ATT_BLOCK].astype(BF16)

    return pl.pallas_call(
        body, grid=(s // tq,),
        in_specs=[pl.BlockSpec((tq, Q_COLS), lambda i: (i, 0)),
                  pl.BlockSpec((tq, 2 * KV_COLS), lambda i: (i, 2)),
                  pl.BlockSpec((ATT_BLOCK, 2 * KV_COLS), lambda i: (jnp.maximum(i * nb - 1, 0), 2)),
                  _resident((1, LANES)), _resident((1, LANES)),
                  pl.BlockSpec(memory_space=pltpu.SMEM)] + g_in,
        out_specs=[pl.BlockSpec((tq, Q_COLS), lambda i: (i, 0))] + g_out,
        out_shape=[jax.ShapeDtypeStruct((s, Q_COLS), BF16)] + g_shape,
        scratch_shapes=g_scratch,
        compiler_params=_cparams("arbitrary"), name="attn_fwd")(qkv, qkv, qkv, gq2, gk2, sinks, *shards)


def _group_stats(c1, lo):
    mu = _half_sums(c1, lo) * (1.0 / HEAD_DIM)
    d = c1 - mu
    rstd = lax.rsqrt(_half_sums(d * d, lo) * (1.0 / HEAD_DIM) + EPS)
    return d * rstd, rstd


def _rows(ref, first_row, n):
    return ref[pl.ds(first_row, n, stride=1), :].reshape(n // SUBLANES, SUBLANES, LANES)


def _conv_fwd(cin, cw8, cb, gain, bias, shards):
    s = cin.shape[0]
    tm = _tile(s)
    rc = 64
    nchunk = CONV_WIDTH // LANES
    lead = CONV_HALO - (CONV_KERNEL - 1)
    ng = len(shards)
    g_in, g_out, g_shape, g_scratch = _gather_specs(shards)

    def body(cin_ref, cw_ref, cb_ref, gain_ref, bias_ref, *rest):
        c3_ref, c1_ref, ext_ref = rest[ng], rest[ng + 1], rest[2 * ng + 2]
        _run_gathers([_Gather(rest[a], rest[ng + 2 + a], *rest[2 * ng + 3 + 4 * a:2 * ng + 7 + 4 * a]) for a in range(ng)],
                     pl.program_id(0), s // tm)

        @pl.when(pl.program_id(0) == 0)
        def _():
            ext_ref[:, 0:CONV_HALO, :] = jnp.zeros((nchunk, CONV_HALO, LANES), F32)

        lo = _lo_mask((1, LANES))
        for cc in range(nchunk):
            cols = slice(cc * LANES, (cc + 1) * LANES)
            gcols = slice(CONV_WIDTH + cc * LANES, CONV_WIDTH + (cc + 1) * LANES)
            ext_ref[cc, CONV_HALO:CONV_HALO + tm, :] = cin_ref[:, cols] * _sigmoid(cin_ref[:, gcols])
            ext = ext_ref.at[cc]
            for r in range(tm // rc):
                rows = slice(r * rc, (r + 1) * rc)
                acc = jnp.zeros((rc // SUBLANES, SUBLANES, LANES), F32)
                for k in range(CONV_KERNEL):
                    acc = acc + cw_ref[k * SUBLANES:(k + 1) * SUBLANES, cols][None] * _rows(ext, r * rc + lead + k, rc)
                c1 = acc.reshape(rc, LANES) + cb_ref[:, cols]
                c1_ref[cc, rows, :] = c1
                nrm, _ = _group_stats(c1, lo)
                c2 = nrm * gain_ref[:, cols] + bias_ref[:, cols]
                c3_ref[rows, cols] = (c2 * _sigmoid(c2)).astype(BF16)
        ext_ref[:, 0:CONV_HALO, :] = ext_ref[:, tm:tm + CONV_HALO, :]

    return pl.pallas_call(
        body, grid=(s // tm,),
        in_specs=[pl.BlockSpec((tm, CIN_COLS), lambda i: (i, 0)), _resident((CONV_KERNEL * SUBLANES, CONV_WIDTH)),
                  _resident((1, CONV_WIDTH)), _resident((1, CONV_WIDTH)), _resident((1, CONV_WIDTH))] + g_in,
        out_specs=[pl.BlockSpec((tm, CONV_WIDTH), lambda i: (i, 0)), pl.BlockSpec((nchunk, tm, LANES), lambda i: (0, i, 0))] + g_out,
        out_shape=[jax.ShapeDtypeStruct((s, CONV_WIDTH), BF16), jax.ShapeDtypeStruct((nchunk, s, LANES), F32)] + g_shape,
        scratch_shapes=[pltpu.VMEM((nchunk, tm + CONV_HALO, LANES), F32)] + g_scratch,
        compiler_params=_cparams("arbitrary"), name="conv_fwd")(cin, cw8, cb, gain, bias, *shards)


def _out_proj(x, attn, c3, wo_a, wo_c, b_out, g_ffn):
    s = x.shape[0]
    tm = _wide_tile(s)

    def body(x_ref, a_ref, c_ref, wa_ref, wc_ref, b_ref, g_ref, x2_ref, h2_ref):
        x2 = x_ref[...] + _dot(a_ref[...], wa_ref[...]) + _dot(c_ref[...], wc_ref[...]) + b_ref[...]
        x2_ref[...] = x2
        r = lax.rsqrt(jnp.mean(x2 * x2, axis=-1, keepdims=True) + EPS)
        h2_ref[...] = (x2 * r * g_ref[...]).astype(BF16)

    return pl.pallas_call(
        body, grid=(s // tm,),
        in_specs=[pl.BlockSpec((tm, D_MODEL), lambda i: (i, 0)), pl.BlockSpec((tm, Q_COLS), lambda i: (i, 0)),
                  pl.BlockSpec((tm, CONV_WIDTH), lambda i: (i, 0)),
                  pl.BlockSpec((Q_COLS, D_MODEL), lambda i: (0, 0), pipeline_mode=pl.Buffered(1)),
                  pl.BlockSpec((CONV_WIDTH, D_MODEL), lambda i: (1, 0), pipeline_mode=pl.Buffered(1)),
                  _resident((1, D_MODEL)), _resident((1, D_MODEL))],
        out_specs=[pl.BlockSpec((tm, D_MODEL), lambda i: (i, 0)), pl.BlockSpec((tm, D_MODEL), lambda i: (i, 0))],
        out_shape=[jax.ShapeDtypeStruct((s, D_MODEL), F32), jax.ShapeDtypeStruct((s, D_MODEL), BF16)],
        compiler_params=_cparams("parallel"), name="out_proj")(x, attn, c3, wo_a, wo_c, b_out, g_ffn)


FF_LANE_CHUNKS = -(-FF_CHUNK // LANES)
FF_PADDED = FF_LANE_CHUNKS * LANES


def _tap(ref, first_row, n):
    return ref[pl.ds(first_row, n, stride=1), :]


def _ffn_fwd(h2, x2, target, w_up, fw, fb, w_down):
    s = h2.shape[0]
    tm = _tile(s)
    hal = SUBLANES
    rc = min(128, tm)

    def body(h_ref, x2_ref, t_ref, wu_ref, fw_ref, fb_ref, wd_ref, up0_ref, gu_ref, act_ref, dy_ref, dyb_ref, loss_ref,
             ext_ref, carry_ref, act_buf, y_ref):
        i, ci = pl.program_id(0), pl.program_id(1)

        @pl.when((i == 0) & (ci == 0))
        def _():
            carry_ref[...] = jnp.zeros(carry_ref.shape, F32)
            ext_ref[...] = jnp.zeros(ext_ref.shape, F32)
            act_buf[...] = jnp.zeros(act_buf.shape, BF16)
            loss_ref[...] = jnp.zeros((1, 1), F32)

        @pl.when(ci == 0)
        def _():
            y_ref[...] = x2_ref[...]

        ws = (fw_ref[ci], fw_ref[ci + N_FF_PAIRS])
        bs = (fb_ref[ci], fb_ref[ci + N_FF_PAIRS])
        half_rows = (slice(0, tm // 2), slice(tm // 2, tm))
        n_grp = len(FF_COLS)

        def up_slices(grp):
            lo_c, hi_c = FF_COLS[grp]
            chunks = range(lo_c // LANES, -(-hi_c // LANES))

            def make(half, n, rows):
                def run():
                    c = ci + half * N_FF_PAIRS
                    u0 = _dot_nt(h_ref[rows, :], wu_ref[c, lo_c:hi_c, :])
                    up0_ref[half, 0, rows, lo_c:hi_c] = u0.astype(BF16)
                    if hi_c == FF_CHUNK:
                        up0_ref[half, 0, rows, FF_CHUNK:] = jnp.zeros((u0.shape[0], FF_PADDED - FF_CHUNK), BF16)
                    for j in chunks:
                        w = min(LANES, hi_c - j * LANES)
                        if n == 0:
                            ext_ref[half, j, 0:hal, 0:w] = carry_ref[c, :, j * LANES:j * LANES + w]
                        ext_ref[half, j, hal + rows.start:hal + rows.stop, 0:w] = u0[:, j * LANES - lo_c:j * LANES - lo_c + w]
                    if n == len(half_rows) - 1:
                        carry_ref[c, :, lo_c:hi_c] = u0[u0.shape[0] - hal:, :]
                return run
            return [make(half, n, rows) for half in range(2) for n, rows in enumerate(half_rows)]

        def down_slices(grp):
            lo_c, hi_c = FF_COLS[grp]

            def make(rows):
                def run():
                    y_ref[rows, :] += _dot(act_buf[rows, lo_c:hi_c], wd_ref[ci, lo_c:hi_c, :])
                return run
            return [make(rows) for rows in half_rows]

        def vector_blocks(grp):
            lo_c, hi_c = FF_COLS[grp]
            blocks = []
            for j in range(lo_c // LANES, -(-hi_c // LANES)):
                lanes = slice(j * LANES, (j + 1) * LANES)

                def gate(r, lanes=lanes, j=j):
                    base = r * rc
                    ups = []
                    for half in range(2):
                        e, w = ext_ref.at[half, j], ws[half]
                        ups.append(w[0:1, lanes] * _tap(e, base + hal - 2, rc) + w[1:2, lanes] * _tap(e, base + hal - 1, rc)
                                   + w[2:3, lanes] * _tap(e, base + hal, rc) + bs[half][:, lanes])
                    g, u = ups
                    gu_ref[0, 0, base:base + rc, lanes] = g.astype(BF16)
                    gu_ref[1, 0, base:base + rc, lanes] = u.astype(BF16)
                    act_buf[base:base + rc, lanes] = (g * _sigmoid(g) * u).astype(BF16)

                blocks += [functools.partial(gate, r) for r in range(tm // rc)]

            def finish():
                act_ref[0, :, lo_c:hi_c] = act_buf[:, lo_c:hi_c]
            blocks.append(finish)
            return blocks

        for run in up_slices(0):
            run()
        for grp in range(n_grp):
            matmuls = (up_slices(grp + 1) if grp + 1 < n_grp else []) + (down_slices(grp - 1) if grp > 0 else [])
            blocks = vector_blocks(grp)
            every = max(1, len(blocks) // (len(matmuls) + 1))
            for n, run in enumerate(blocks):
                run()
                if n % every == every - 1 and matmuls:
                    matmuls.pop(0)()
            for run in matmuls:
                run()
        for run in down_slices(n_grp - 1):
            run()

        @pl.when(ci == N_FF_PAIRS - 1)
        def _():
            e = y_ref[...] - t_ref[...]
            dy_ref[...] = e * (1.0 / D_MODEL)
            dyb_ref[...] = (e * (1.0 / D_MODEL)).astype(BF16)
            loss_ref[...] += (0.5 / D_MODEL) * jnp.sum(e * e).reshape(1, 1)

    tok = lambda i, ci: (i, 0)
    return pl.pallas_call(
        body, grid=(s // tm, N_FF_PAIRS),
        in_specs=[pl.BlockSpec((tm, D_MODEL), tok), pl.BlockSpec((tm, D_MODEL), tok), pl.BlockSpec((tm, D_MODEL), tok),
                  _resident((N_DEV, FF_CHUNK, D_MODEL)), _resident((N_DEV, 3, FF_PADDED)), _resident((N_DEV, 1, FF_PADDED)),
                  _resident((N_FF_PAIRS, FF_CHUNK, D_MODEL))],
        out_specs=[pl.BlockSpec((2, 1, tm, FF_PADDED), lambda i, ci: (0, ci, i, 0)),
                   pl.BlockSpec((2, 1, tm, FF_PADDED), lambda i, ci: (0, ci, i, 0)),
                   pl.BlockSpec((1, tm, FF_CHUNK), lambda i, ci: (ci, i, 0)),
                   pl.BlockSpec((tm, D_MODEL), tok), pl.BlockSpec((tm, D_MODEL), tok), pl.BlockSpec((1, 1), lambda i, ci: (0, 0))],
        out_shape=[jax.ShapeDtypeStruct((2, N_FF_PAIRS, s, FF_PADDED), BF16), jax.ShapeDtypeStruct((2, N_FF_PAIRS, s, FF_PADDED), BF16),
                   jax.ShapeDtypeStruct((N_FF_PAIRS, s, FF_CHUNK), BF16), jax.ShapeDtypeStruct((s, D_MODEL), F32),
                   jax.ShapeDtypeStruct((s, D_MODEL), BF16), jax.ShapeDtypeStruct((1, 1), F32)],
        scratch_shapes=[pltpu.VMEM((2, FF_LANE_CHUNKS, tm + hal, LANES), F32), pltpu.VMEM((N_DEV, hal, FF_CHUNK), F32),
                        pltpu.VMEM((tm, FF_PADDED), BF16), pltpu.VMEM((tm, D_MODEL), F32)],
        compiler_params=_cparams("arbitrary", "arbitrary"), name="ffn_fwd")(h2, x2, target, w_up, fw, fb, w_down)


def _ffn_bwd(dyb, up0, gu, w_up, fw, w_down):
    s = dyb.shape[0]
    tm = _tile(s)
    nt = s // tm
    nxt = SUBLANES
    rc = min(128, tm)

    def body(dy_ref, up0_ref, gu_ref, wu_ref, fw_ref, wd_ref,
             dup0_ref, dh2_ref, dfw_ref, dfb_ref, dext_ref, carry_ref, dact_buf, dup0_buf, dh2_acc):
        i, ci = pl.program_id(0), pl.program_id(1)

        @pl.when((i == 0) & (ci == 0))
        def _():
            for ref in (carry_ref, dfw_ref, dfb_ref, dext_ref, dact_buf):
                ref[...] = jnp.zeros(ref.shape, F32)
            dup0_buf[...] = jnp.zeros(dup0_buf.shape, BF16)

        @pl.when(ci == 0)
        def _():
            dh2_acc[...] = jnp.zeros(dh2_acc.shape, F32)

        ws = (fw_ref[ci], fw_ref[ci + N_FF_PAIRS])
        fold = lambda v: jnp.sum(v.reshape(rc // SUBLANES, SUBLANES, LANES), axis=0)
        half_rows = (slice(0, tm // 2), slice(tm // 2, tm))
        n_grp = len(FF_COLS)

        def dact_slices(grp):
            lo_c, hi_c = FF_COLS[grp]

            def make(rows):
                def run():
                    dact_buf[rows, lo_c:hi_c] = _dot_nt(dy_ref[rows, :], wd_ref[ci, lo_c:hi_c, :])
                return run
            return [make(rows) for rows in half_rows]

        def dh2_slices(grp):
            lo_c, hi_c = FF_COLS[grp]

            def make(half, rows):
                def run():
                    c = ci + half * N_FF_PAIRS
                    dh2_acc[rows, :] += _dot(dup0_buf[half, rows, lo_c:hi_c], wu_ref[c, lo_c:hi_c, :])
                return run
            return [make(half, rows) for half in range(2) for rows in half_rows]

        def vector_blocks(grp):
            lo_c, hi_c = FF_COLS[grp]
            chunks = range(lo_c // LANES, -(-hi_c // LANES))
            blocks = []

            def stage():
                for half in range(2):
                    c = ci + half * N_FF_PAIRS
                    for j in chunks:
                        dext_ref[half, j, tm:tm + nxt, :] = carry_ref[c, :, j * LANES:(j + 1) * LANES]
            blocks.append(stage)
            for j in chunks:
                lanes = slice(j * LANES, (j + 1) * LANES)
                acc = [jnp.zeros((SUBLANES, LANES), F32)] * 8

                def grads(r, lanes=lanes, j=j, acc=acc):
                    base = r * rc
                    g = gu_ref[0, 0, base:base + rc, lanes].astype(F32)
                    u = gu_ref[1, 0, base:base + rc, lanes].astype(F32)
                    sg = _sigmoid(g)
                    silu = g * sg
                    dact = dact_buf[base:base + rc, lanes]
                    ds = (dact * u * (sg + silu - silu * sg), dact * silu)
                    for half in range(2):
                        dext_ref[half, j, base:base + rc, :] = ds[half]
                        acc[4 * half] = acc[4 * half] + fold(ds[half])

                def conv_back(r, lanes=lanes, j=j, acc=acc):
                    base = r * rc
                    for half in range(2):
                        d, w = dext_ref.at[half, j], ws[half]
                        taps = [_tap(d, base + k, rc) for k in range(3)]
                        dup0 = w[2:3, lanes] * taps[0] + w[1:2, lanes] * taps[1] + w[0:1, lanes] * taps[2]
                        dup0_buf[half, base:base + rc, lanes] = dup0.astype(BF16)
                        u0 = up0_ref[half, 0, base:base + rc, lanes].astype(F32)
                        for k in range(3):
                            acc[4 * half + 1 + k] = acc[4 * half + 1 + k] + fold(taps[2 - k] * u0)

                def sums(lanes=lanes, j=j, acc=acc):
                    for half in range(2):
                        c = ci + half * N_FF_PAIRS
                        carry_ref[c, :, lanes] = dext_ref[half, j, 0:nxt, :]
                        dfb_ref[c, :, lanes] += jnp.sum(acc[4 * half], axis=0, keepdims=True)
                        dfw_ref[c, :, lanes] += jnp.concatenate(
                            [jnp.sum(acc[4 * half + 1 + k], axis=0, keepdims=True) for k in range(3)], axis=0)

                blocks += [functools.partial(grads, r) for r in range(tm // rc)]
                blocks += [functools.partial(conv_back, r) for r in range(tm // rc)] + [sums]

            def finish():
                for half in range(2):
                    dup0_ref[half, 0, :, lo_c:hi_c] = dup0_buf[half, :, lo_c:hi_c]
            blocks.append(finish)
            return blocks

        for run in dact_slices(0):
            run()
        for grp in range(n_grp):
            matmuls = (dact_slices(grp + 1) if grp + 1 < n_grp else []) + (dh2_slices(grp - 1) if grp > 0 else [])
            blocks = vector_blocks(grp)
            every = max(1, len(blocks) // (len(matmuls) + 1))
            for n, run in enumerate(blocks):
                run()
                if n % every == every - 1 and matmuls:
                    matmuls.pop(0)()
            for run in matmuls:
                run()
        for run in dh2_slices(n_grp - 1):
            run()

        @pl.when(ci == N_FF_PAIRS - 1)
        def _():
            dh2_ref[...] = dh2_acc[...].astype(BF16)

    tok = lambda i, ci: (nt - 1 - i, 0)
    acc = lambda shape: pl.BlockSpec(shape, lambda i, ci: (0,) * len(shape))
    saved = pl.BlockSpec((2, 1, tm, FF_PADDED), lambda i, ci: (0, ci, nt - 1 - i, 0))
    return pl.pallas_call(
        body, grid=(nt, N_FF_PAIRS),
        in_specs=[pl.BlockSpec((tm, D_MODEL), tok), saved, saved,
                  _resident((N_DEV, FF_CHUNK, D_MODEL)), _resident((N_DEV, 3, FF_PADDED)),
                  _resident((N_FF_PAIRS, FF_CHUNK, D_MODEL))],
        out_specs=[pl.BlockSpec((2, 1, tm, FF_CHUNK), lambda i, ci: (0, ci, nt - 1 - i, 0)),
                   pl.BlockSpec((tm, D_MODEL), tok), acc((N_DEV, 3, FF_PADDED)), acc((N_DEV, 1, FF_PADDED))],
        out_shape=[jax.ShapeDtypeStruct((2, N_FF_PAIRS, s, FF_CHUNK), BF16), jax.ShapeDtypeStruct((s, D_MODEL), BF16),
                   jax.ShapeDtypeStruct((N_DEV, 3, FF_PADDED), F32), jax.ShapeDtypeStruct((N_DEV, 1, FF_PADDED), F32)],
        scratch_shapes=[pltpu.VMEM((2, FF_LANE_CHUNKS, tm + nxt, LANES), F32), pltpu.VMEM((N_DEV, nxt, FF_PADDED), F32),
                        pltpu.VMEM((tm, FF_PADDED), F32), pltpu.VMEM((2, tm, FF_PADDED), BF16), pltpu.VMEM((tm, D_MODEL), F32)],
        compiler_params=_cparams("arbitrary", "arbitrary"), name="ffn_bwd")(dyb, up0, gu, w_up, fw, w_down)


def _ffn_norm_bwd(dh2, dy, x2, g_ffn, w_out, attn, c3):
    s = dy.shape[0]
    tm = _wide_tile(s)

    def body(dh_ref, dy_ref, x2_ref, g_ref, wo_ref, attn_ref, c3_ref, dx2_ref, dmix_ref, dg_ref, dbo_ref, dwo_ref):
        @pl.when(pl.program_id(0) == 0)
        def _():
            dg_ref[...] = jnp.zeros(dg_ref.shape, F32)
            dbo_ref[...] = jnp.zeros(dbo_ref.shape, F32)
            dwo_ref[...] = jnp.zeros(dwo_ref.shape, F32)

        x2v = x2_ref[...]
        r = lax.rsqrt(jnp.mean(x2v * x2v, axis=-1, keepdims=True) + EPS)
        n2 = x2v * r
        dh2 = dh_ref[...].astype(F32)
        dg_ref[...] += jnp.sum(dh2 * n2, axis=0, keepdims=True)
        dn = dh2 * g_ref[...]
        dx2 = dy_ref[...] + r * (dn - n2 * jnp.mean(dn * n2, axis=-1, keepdims=True))
        dx2_ref[...] = dx2
        dbo_ref[...] += jnp.sum(dx2, axis=0, keepdims=True)
        dx2b = dx2.astype(BF16)
        dmix_ref[...] = _dot_nt(dx2b, wo_ref[...]).astype(BF16)
        dwo_ref[0:Q_COLS, :] += _dot_tn(attn_ref[...], dx2b)
        dwo_ref[Q_COLS:, :] += _dot_tn(c3_ref[...], dx2b)

    tok = pl.BlockSpec((tm, D_MODEL), lambda i: (i, 0))
    vec = pl.BlockSpec((1, D_MODEL), lambda i: (0, 0))
    return pl.pallas_call(
        body, grid=(s // tm,),
        in_specs=[tok, tok, tok, _resident((1, D_MODEL)), _resident((D_MODEL, D_MODEL)),
                  pl.BlockSpec((tm, Q_COLS), lambda i: (i, 0)), pl.BlockSpec((tm, CONV_WIDTH), lambda i: (i, 0))],
        out_specs=[tok, tok, vec, vec, pl.BlockSpec((D_MODEL, D_MODEL), lambda i: (0, 0))],
        out_shape=[jax.ShapeDtypeStruct((s, D_MODEL), F32), jax.ShapeDtypeStruct((s, D_MODEL), BF16),
                   jax.ShapeDtypeStruct((1, D_MODEL), F32), jax.ShapeDtypeStruct((1, D_MODEL), F32),
                   jax.ShapeDtypeStruct((D_MODEL, D_MODEL), F32)],
        compiler_params=_cparams("arbitrary"), name="ffn_norm_bwd")(dh2, dy, x2, g_ffn, w_out, attn, c3)


def _conv_bwd(dmixed, c1, cin, cw8, gain, bias, g8s):
    ns = len(g8s)
    s_in, s_out, s_shape, s_scratch = _scatter_specs(g8s)
    s = cin.shape[0]
    tm = _tile(s)
    nt = s // tm
    rc = 64
    rn = min(256, tm)
    hal = CONV_HALO
    nchunk = CONV_WIDTH // LANES

    def body(dc3_ref, dc3n_ref, c1_ref, c1n_ref, cin_ref, cw_ref, gain_ref, bias_ref, *rest):
        dcin_ref, dcw_ref, dcb_ref, dgain_ref, dbias_ref, dbcin_ref = rest[ns:ns + 6]
        dc1_ext, dcw8 = rest[2 * ns + 6:2 * ns + 8]
        i = pl.program_id(0)
        first, last = i == 0, i == nt - 1
        own = rest[2 * ns + 8:]
        _run_scatters([_ReduceScatter(rest[a], rest[ns + 6 + a], *own[N_SCATTER_SCRATCH * a:N_SCATTER_SCRATCH * (a + 1)])
                       for a in range(ns)], i, nt)

        @pl.when(first)
        def _():
            for ref in (dcw8, dcb_ref, dgain_ref, dbias_ref, dbcin_ref):
                ref[...] = jnp.zeros(ref.shape, F32)

        lo = _lo_mask((1, LANES))

        def norm_bwd(dc3, c1v, cols):
            nrm, rstd = _group_stats(c1v, lo)
            c2 = nrm * gain_ref[:, cols] + bias_ref[:, cols]
            sg = _sigmoid(c2)
            dc2 = dc3 * (sg * (1.0 + c2 * (1.0 - sg)))
            dn = dc2 * gain_ref[:, cols]
            inv = 1.0 / HEAD_DIM
            dc1 = rstd * (dn - _half_sums(dn, lo) * inv - nrm * (_half_sums(dn * nrm, lo) * inv))
            return dc1, dc2, nrm

        def row_sum(v):
            return jnp.sum(v, axis=0, keepdims=True)

        for cc in range(nchunk):
            cols = slice(cc * LANES, (cc + 1) * LANES)
            gcols = slice(CONV_WIDTH + cc * LANES, CONV_WIDTH + (cc + 1) * LANES)
            d1e = dc1_ext.at[cc]
            dc1n, _, _ = norm_bwd(dc3n_ref[:, cols].astype(F32), c1n_ref[cc], cols)
            d1e[tm:tm + hal, :] = jnp.where(last, 0.0, dc1n)

            for r in range(tm // rn):
                rows = slice(r * rn, (r + 1) * rn)
                dc1, dc2, nrm = norm_bwd(dc3_ref[rows, cols].astype(F32), c1_ref[cc, rows, :], cols)
                d1e[rows, :] = dc1
                dgain_ref[:, cols] += row_sum(dc2 * nrm)
                dbias_ref[:, cols] += row_sum(dc2)
                dcb_ref[:, cols] += row_sum(dc1)
            zero = jnp.zeros((1, LANES), F32)

            def taps(r, sums):
                rows = pl.ds(pl.multiple_of(r * rc, rc), rc)
                a = cin_ref[rows, cols]
                sg = _sigmoid(cin_ref[rows, gcols])
                c0 = (a * sg).reshape(rc // SUBLANES, SUBLANES, LANES)
                dc0 = jnp.zeros((rc // SUBLANES, SUBLANES, LANES), F32)
                for k in range(CONV_KERNEL):
                    krows = slice(k * SUBLANES, (k + 1) * SUBLANES)
                    shifted = _rows(d1e, r * rc + CONV_KERNEL - 1 - k, rc)
                    dc0 = dc0 + cw_ref[krows, cols][None] * shifted
                    dcw8[krows, cols] += jnp.sum(shifted * c0, axis=0)
                dc0 = dc0.reshape(rc, LANES)
                da = dc0 * sg
                dgate = dc0 * a * sg * (1.0 - sg)
                dcin_ref[rows, cols] = da.astype(BF16)
                dcin_ref[rows, gcols] = dgate.astype(BF16)
                return sums[0] + row_sum(da), sums[1] + row_sum(dgate)

            sums = lax.fori_loop(0, tm // rc, taps, (zero, zero))
            dbcin_ref[:, cols] += sums[0]
            dbcin_ref[:, gcols] += sums[1]

        @pl.when(last)
        def _():
            for k in range(CONV_KERNEL):
                dcw_ref[k:k + 1, :] = jnp.sum(dcw8[k * SUBLANES:(k + 1) * SUBLANES, :], axis=0, keepdims=True)

    nh = tm // hal
    acc = lambda shape: pl.BlockSpec(shape, lambda i: (0,) * len(shape))
    return pl.pallas_call(
        body, grid=(nt,),
        in_specs=[pl.BlockSpec((tm, CONV_WIDTH), lambda i: (i, 1)),
                  pl.BlockSpec((hal, CONV_WIDTH), lambda i: (jnp.minimum((i + 1) * nh, s // hal - 1), 1)),
                  pl.BlockSpec((nchunk, tm, LANES), lambda i: (0, i, 0)),
                  pl.BlockSpec((nchunk, hal, LANES), lambda i: (0, jnp.minimum((i + 1) * nh, s // hal - 1), 0)),
                  pl.BlockSpec((tm, CIN_COLS), lambda i: (i, 0)),
                  _resident((CONV_KERNEL * SUBLANES, CONV_WIDTH)), _resident((1, CONV_WIDTH)), _resident((1, CONV_WIDTH))] + s_in,
        out_specs=[pl.BlockSpec((tm, CIN_COLS), lambda i: (i, 0)), acc((CONV_KERNEL, CONV_WIDTH)), acc((1, CONV_WIDTH)),
                   acc((1, CONV_WIDTH)), acc((1, CONV_WIDTH)), acc((1, CIN_COLS))] + s_out,
        out_shape=[jax.ShapeDtypeStruct((s, CIN_COLS), BF16), jax.ShapeDtypeStruct((CONV_KERNEL, CONV_WIDTH), F32),
                   jax.ShapeDtypeStruct((1, CONV_WIDTH), F32), jax.ShapeDtypeStruct((1, CONV_WIDTH), F32),
                   jax.ShapeDtypeStruct((1, CONV_WIDTH), F32), jax.ShapeDtypeStruct((1, CIN_COLS), F32)] + s_shape,
        scratch_shapes=[pltpu.VMEM((nchunk, tm + hal, LANES), F32),
                        pltpu.VMEM((CONV_KERNEL * SUBLANES, CONV_WIDTH), F32)] + s_scratch,
        compiler_params=_cparams("arbitrary"), name="conv_bwd")(dmixed, dmixed, c1, c1, cin, cw8, gain, bias, *g8s)


def _attn_bwd(qkv, dmixed, gq2, gk2, sinks, g8s):
    ns = len(g8s)
    s_in, s_out, s_shape, s_scratch = _scatter_specs(g8s)
    s = qkv.shape[0]
    tq = _tile(s)
    nb = tq // ATT_BLOCK
    nt = s // tq

    def body(q_ref, kv_ref, kvp_ref, do_ref, gq_ref, gk_ref, sink_ref, *rest):
        dqkv_ref, dgq_ref, dgk_ref, dsink_ref, dbqkv_ref = rest[ns:ns + 5]
        dk_acc, dv_acc, carry_k, carry_v = rest[2 * ns + 5:2 * ns + 9]
        i = pl.program_id(0)
        t = nt - 1 - i
        own = rest[2 * ns + 9:]
        _run_scatters([_ReduceScatter(rest[a], rest[ns + 5 + a], *own[N_SCATTER_SCRATCH * a:N_SCATTER_SCRATCH * (a + 1)])
                       for a in range(ns)], i, nt)

        @pl.when(i == 0)
        def _():
            for ref in (carry_k, carry_v, dgq_ref, dgk_ref, dsink_ref, dbqkv_ref):
                ref[...] = jnp.zeros(ref.shape, F32)

        lo = _lo_mask((1, LANES))
        lane_id = lax.broadcasted_iota(jnp.int32, (1, LANES), 1)
        kv_all = jnp.concatenate([kvp_ref[...], kv_ref[...]], axis=0)
        k_lo, k_hi, v_lo, v_hi, kn_pre, rk = _kv_variants(kv_all, gk_ref[...], lo)
        for acc_ref, carry in ((dk_acc, carry_k), (dv_acc, carry_v)):
            acc_ref[:, 0:tq, :] = jnp.zeros((N_KV_HEADS, tq, LANES), F32)
            acc_ref[:, tq:tq + ATT_BLOCK, :] = carry[...]
        dsink = jnp.zeros((1, LANES), F32)
        dgq = jnp.zeros((1, LANES), F32)
        gq = gq_ref[...]
        for b in range(nb):
            rel, valid = _att_consts(t == 0, b)
            rows = slice(b * ATT_BLOCK, (b + 1) * ATT_BLOCK)
            keys = slice(b * ATT_BLOCK, (b + 2) * ATT_BLOCK)
            for kvh in range(N_KV_HEADS):
                pairs = (2 * kvh, 2 * kvh + 1)
                q_raw = jnp.concatenate([q_ref[rows, p * LANES:(p + 1) * LANES] for p in pairs], axis=0)
                qn_pre, rq = _head_norm(q_raw, lo)
                q2 = (qn_pre * gq).astype(BF16)
                do2 = jnp.concatenate([do_ref[rows, p * LANES:(p + 1) * LANES] for p in pairs], axis=0).astype(BF16)
                dq2 = jnp.zeros((2 * ATT_BLOCK, LANES), F32)
                for odd, (k_op, v_op) in enumerate(((k_lo[kvh][keys], v_lo[kvh][keys]), (k_hi[kvh][keys], v_hi[kvh][keys]))):
                    ha, hb = 2 * pairs[0] + odd, 2 * pairs[1] + odd
                    p, p_sink = _probs(q2, k_op, rel, valid, _row_const(SLOPES[ha], SLOPES[hb]),
                                       _row_const(sink_ref[ha], sink_ref[hb]))
                    dp = _dot_nt(do2, v_op)
                    delta = jnp.sum(p * dp, axis=-1, keepdims=True)
                    ds = (p * (dp - delta) * (1.0 / math.sqrt(HEAD_DIM))).astype(BF16)
                    dsk = p_sink * delta
                    dsink = dsink - jnp.where(lane_id == ha, jnp.sum(dsk[0:ATT_BLOCK]), 0.0) \
                        - jnp.where(lane_id == hb, jnp.sum(dsk[ATT_BLOCK:]), 0.0)
                    dq2 = dq2 + _dot(ds, k_op)
                    half = lo if odd == 0 else jnp.logical_not(lo)
                    dk_acc[kvh, keys, :] += jnp.where(half, _dot_tn(ds, q2), 0.0)
                    dv_acc[kvh, keys, :] += jnp.where(half, _dot_tn(p.astype(BF16), do2), 0.0)
                dgq = dgq + jnp.sum(dq2 * qn_pre, axis=0, keepdims=True)
                dq_raw = _head_norm_bwd(dq2 * gq, qn_pre, rq, lo)
                for n, p_ in enumerate(pairs):
                    blk = dq_raw[n * ATT_BLOCK:(n + 1) * ATT_BLOCK]
                    dqkv_ref[rows, p_ * LANES:(p_ + 1) * LANES] = blk.astype(BF16)
                    dbqkv_ref[:, p_ * LANES:(p_ + 1) * LANES] += jnp.sum(blk, axis=0, keepdims=True)
        carry_k[...] = dk_acc[:, 0:ATT_BLOCK, :]
        carry_v[...] = dv_acc[:, 0:ATT_BLOCK, :]

        def fold(acc_ref):
            both = []
            for kvh in range(N_KV_HEADS):
                a = acc_ref[kvh, ATT_BLOCK:ATT_BLOCK + tq, :]
                both.append(a + pltpu.roll(a, HEAD_DIM, 1))
            return jnp.where(lo, both[0], both[1])

        dkn = fold(dk_acc)
        dv = fold(dv_acc)
        kn_c, rk_c = kn_pre[ATT_BLOCK:], rk[ATT_BLOCK:]
        dgk_ref[...] += jnp.sum(dkn * kn_c, axis=0, keepdims=True)
        dk_raw = _head_norm_bwd(dkn * gk_ref[...], kn_c, rk_c, lo)
        dqkv_ref[:, Q_COLS:Q_COLS + KV_COLS] = dk_raw.astype(BF16)
        dqkv_ref[:, Q_COLS + KV_COLS:] = dv.astype(BF16)
        dbqkv_ref[:, Q_COLS:Q_COLS + KV_COLS] += jnp.sum(dk_raw, axis=0, keepdims=True)
        dbqkv_ref[:, Q_COLS + KV_COLS:] += jnp.sum(dv, axis=0, keepdims=True)
        dgq_ref[...] += dgq
        dsink_ref[...] += dsink

        @pl.when(i == nt - 1)
        def _():
            for ref in (dgq_ref, dgk_ref):
                v = ref[...]
                ref[...] = v + pltpu.roll(v, HEAD_DIM, 1)

    acc = lambda shape: pl.BlockSpec(shape, lambda i: (0,) * len(shape))
    return pl.pallas_call(
        body, grid=(nt,),
        in_specs=[pl.BlockSpec((tq, Q_COLS), lambda i: (nt - 1 - i, 0)),
                  pl.BlockSpec((tq, 2 * KV_COLS), lambda i: (nt - 1 - i, 2)),
                  pl.BlockSpec((ATT_BLOCK, 2 * KV_COLS), lambda i: (jnp.maximum((nt - 1 - i) * nb - 1, 0), 2)),
                  pl.BlockSpec((tq, Q_COLS), lambda i: (nt - 1 - i, 0)),
                  _resident((1, LANES)), _resident((1, LANES)), pl.BlockSpec(memory_space=pltpu.SMEM)] + s_in,
        out_specs=[pl.BlockSpec((tq, QKV_COLS), lambda i: (nt - 1 - i, 0)), acc((1, LANES)), acc((1, LANES)),
                   acc((1, LANES)), acc((1, QKV_COLS))] + s_out,
        out_shape=[jax.ShapeDtypeStruct((s, QKV_COLS), BF16), jax.ShapeDtypeStruct((1, LANES), F32),
                   jax.ShapeDtypeStruct((1, LANES), F32), jax.ShapeDtypeStruct((1, LANES), F32),
                   jax.ShapeDtypeStruct((1, QKV_COLS), F32)] + s_shape,
        scratch_shapes=[pltpu.VMEM((N_KV_HEADS, tq + ATT_BLOCK, LANES), F32), pltpu.VMEM((N_KV_HEADS, tq + ATT_BLOCK, LANES), F32),
                        pltpu.VMEM((N_KV_HEADS, ATT_BLOCK, LANES), F32), pltpu.VMEM((N_KV_HEADS, ATT_BLOCK, LANES), F32)] + s_scratch,
        compiler_params=_cparams("arbitrary"), name="attn_bwd")(qkv, qkv, qkv, dmixed, gq2, gk2, sinks, *g8s)


def _in_bwd(dqkv, dcin, w_in_t, x, dx2, g_mix, h1):
    s = x.shape[0]
    tm = _tile(s)

    def body(dq_ref, dc_ref, w_ref, x_ref, dx2_ref, g_ref, h1_ref, gx_ref, dg_ref, dwi_ref):
        @pl.when(pl.program_id(0) == 0)
        def _():
            dg_ref[...] = jnp.zeros(dg_ref.shape, F32)
            dwi_ref[...] = jnp.zeros(dwi_ref.shape, F32)

        dwi_ref[0:QKV_COLS, :] += _dot_tn(dq_ref[...], h1_ref[...])
        dwi_ref[QKV_COLS:, :] += _dot_tn(dc_ref[...], h1_ref[...])
        dh = _dot(dq_ref[...], w_ref[0:QKV_COLS, :]) + _dot(dc_ref[...], w_ref[QKV_COLS:, :])
        xv = x_ref[...]
        r = lax.rsqrt(jnp.mean(xv * xv, axis=-1, keepdims=True) + EPS)
        n = xv * r
        dg_ref[...] += jnp.sum(dh * n, axis=0, keepdims=True)
        dn = dh * g_ref[...]
        gx_ref[...] = dx2_ref[...] + r * (dn - n * jnp.mean(dn * n, axis=-1, keepdims=True))

    return pl.pallas_call(
        body, grid=(s // tm,),
        in_specs=[pl.BlockSpec((tm, QKV_COLS), lambda i: (i, 0)), pl.BlockSpec((tm, CIN_COLS), lambda i: (i, 0)),
                  _resident((QKV_COLS + CIN_COLS, D_MODEL)),
                  pl.BlockSpec((tm, D_MODEL), lambda i: (i, 0)), pl.BlockSpec((tm, D_MODEL), lambda i: (i, 0)),
                  _resident((1, D_MODEL)), pl.BlockSpec((tm, D_MODEL), lambda i: (i, 0))],
        out_specs=[pl.BlockSpec((tm, D_MODEL), lambda i: (i, 0)), pl.BlockSpec((1, D_MODEL), lambda i: (0, 0)),
                   pl.BlockSpec((QKV_COLS + CIN_COLS, D_MODEL), lambda i: (0, 0))],
        out_shape=[jax.ShapeDtypeStruct((s, D_MODEL), F32), jax.ShapeDtypeStruct((1, D_MODEL), F32),
                   jax.ShapeDtypeStruct((QKV_COLS + CIN_COLS, D_MODEL), F32)],
        compiler_params=_cparams("arbitrary"), name="in_bwd")(dqkv, dcin, w_in_t, x, dx2, g_mix, h1)


def _tn_matmul(a, b, name, tokens):
    ga, s, m = a.shape
    gb, _, n = b.shape
    g = max(ga, gb)
    tk = min(tokens, s)

    def body(a_ref, b_ref, o_ref):
        @pl.when(pl.program_id(1) == 0)
        def _():
            o_ref[...] = jnp.zeros(o_ref.shape, F32)

        o_ref[0] += _dot_tn(a_ref[0].astype(BF16), b_ref[0].astype(BF16))

    return pl.pallas_call(
        body, grid=(g, s // tk),
        in_specs=[pl.BlockSpec((1, tk, m), (lambda gi, k: (gi, k, 0)) if ga > 1 else (lambda gi, k: (0, k, 0))),
                  pl.BlockSpec((1, tk, n), (lambda gi, k: (gi, k, 0)) if gb > 1 else (lambda gi, k: (0, k, 0)))],
        out_specs=pl.BlockSpec((1, m, n), lambda gi, k: (gi, 0, 0)),
        out_shape=jax.ShapeDtypeStruct((g, m, n), F32),
        compiler_params=_cparams("parallel", "arbitrary"), name=name)(a, b)


def _allgather(shards, dtypes):
    n = len(shards)
    n_copies = 1 + 2 * len(OTHER_CHIPS)

    def body(*refs):
        ins, outs = refs[:n], refs[n:2 * n]
        send_sems, recv_sems = refs[2 * n:]
        x, y, c = _position()
        me, sibling = (x, y, c), (x, y, 1 - c)
        chips = [(_flip(x, fx), _flip(y, fy)) for fx, fy in OTHER_CHIPS]
        for a in range(n):
            outs[a][_dev_index(*me)] = ins[a][...].astype(dtypes[a])

        def copy(a, k, block, to):
            rows = outs[a].at[_dev_index(*block)]
            return pltpu.make_async_remote_copy(src_ref=rows, dst_ref=rows, send_sem=send_sems.at[a, k],
                                                recv_sem=recv_sems.at[a, k], device_id=to, device_id_type=MESH)

        started = []
        for a in range(n):
            for j, chip in enumerate(chips):
                started.append(copy(a, 1 + j, me, (*chip, c)))
            started.append(copy(a, 0, me, sibling))
        for cp in started:
            cp.start()
        for a in range(n):
            for j, chip in enumerate(chips):
                copy(a, 1 + j, (*chip, c), me).wait_recv()
                fwd = copy(a, 1 + len(chips) + j, (*chip, c), sibling)
                fwd.start()
                started.append(fwd)
        for a in range(n):
            copy(a, 0, sibling, me).wait_recv()
            for j, chip in enumerate(chips):
                copy(a, 1 + len(chips) + j, (*chip, 1 - c), me).wait_recv()
        for cp in started:
            cp.wait_send()

    vmem = pl.BlockSpec(memory_space=pltpu.VMEM)
    return pl.pallas_call(
        body, in_specs=[vmem] * n, out_specs=[vmem] * n,
        out_shape=[jax.ShapeDtypeStruct((N_DEV,) + w.shape, dt) for w, dt in zip(shards, dtypes)],
        scratch_shapes=[pltpu.SemaphoreType.DMA((n, n_copies)), pltpu.SemaphoreType.DMA((n, n_copies))],
        compiler_params=pltpu.CompilerParams(vmem_limit_bytes=VMEM_LIMIT), name="allgather_weights")(*shards)


def _final_exchange(g8, v):
    rows = v.shape[0]
    n_chips = 1 + len(OTHER_CHIPS)
    _, _, s_shape, s_scratch = _scatter_specs([g8])

    def body(g_ref, v_ref, gout_ref, vout_ref, from_sibling, chip_sums, send_sems, recv_sems, *rs_scratch):
        scatter = _ReduceScatter(g_ref, gout_ref, *rs_scratch)
        x, y, c = _position()
        my_chip = 2 * x + y
        chips = [(_flip(x, fx), _flip(y, fy)) for fx, fy in OTHER_CHIPS]

        def swap():
            return pltpu.make_async_remote_copy(src_ref=v_ref, dst_ref=from_sibling, send_sem=send_sems.at[0],
                                                recv_sem=recv_sems.at[0], device_id=(x, y, 1 - c), device_id_type=MESH)

        def push(j):
            return pltpu.make_async_remote_copy(src_ref=chip_sums.at[my_chip], dst_ref=chip_sums.at[my_chip],
                                                send_sem=send_sems.at[1 + j], recv_sem=recv_sems.at[1 + j],
                                                device_id=(*chips[j], c), device_id_type=MESH)

        swap().start()
        scatter.start()
        swap().wait_recv()
        chip_sums[my_chip] = v_ref[...] + from_sibling[...]
        for j in range(len(chips)):
            push(j).start()
        scatter.middle()
        for j in range(len(chips)):
            push(j).wait_recv()
        swap().wait_send()
        for j in range(len(chips)):
            push(j).wait_send()
        total = chip_sums[0]
        for q in range(1, n_chips):
            total = total + chip_sums[q]
        vout_ref[...] = total
        scatter.finish()

    vmem = pl.BlockSpec(memory_space=pltpu.VMEM)
    return pl.pallas_call(
        body, in_specs=[pl.BlockSpec(memory_space=pl.ANY), vmem], out_specs=[vmem, vmem],
        out_shape=s_shape + [jax.ShapeDtypeStruct((rows, LANES), F32)],
        scratch_shapes=[pltpu.VMEM((rows, LANES), F32), pltpu.VMEM((n_chips, rows, LANES), F32),
                        pltpu.SemaphoreType.DMA((n_chips,)), pltpu.SemaphoreType.DMA((n_chips,))] + s_scratch,
        compiler_params=pltpu.CompilerParams(vmem_limit_bytes=VMEM_LIMIT), name="final_exchange")(g8, v)


def _adam_math(wv, gv, mv, vv):
    mn = ADAM_B1 * mv + (1.0 - ADAM_B1) * gv
    vn = ADAM_B2 * vv + (1.0 - ADAM_B2) * (gv * gv)
    m_hat = mn / (1.0 - ADAM_B1 ** ADAM_STEP)
    v_hat = vn / (1.0 - ADAM_B2 ** ADAM_STEP)
    return -ADAM_LR * (m_hat / (jnp.sqrt(v_hat) + ADAM_EPS) + ADAM_WD * wv), mn, vn


ADAM_ROW_BLOCKS = 4


def _adamw(ws, gs, ms, vs):
    n = len(ws)
    assert all(w.shape[0] % (ADAM_ROW_BLOCKS * SUBLANES) == 0 for w in ws)

    def body(*refs):
        ins, outs = refs[:4 * n], refs[4 * n:]
        for i in range(n):
            w_ref, g_ref, m_ref, v_ref = ins[4 * i:4 * i + 4]
            d_ref, mo_ref, vo_ref = outs[3 * i:3 * i + 3]
            d_ref[...], mo_ref[...], vo_ref[...] = _adam_math(w_ref[...], g_ref[...], m_ref[...], v_ref[...])

    specs = [pl.BlockSpec((w.shape[0] // ADAM_ROW_BLOCKS, w.shape[1]), lambda i: (i, 0)) for w in ws]
    outs = pl.pallas_call(
        body, grid=(ADAM_ROW_BLOCKS,), in_specs=[s for s in specs for _ in range(4)],
        out_specs=[s for s in specs for _ in range(3)],
        out_shape=[jax.ShapeDtypeStruct(w.shape, F32) for w in ws for _ in range(3)],
        compiler_params=_cparams("arbitrary"), name="adamw_big")(*[a for four in zip(ws, gs, ms, vs) for a in four])
    return [tuple(outs[3 * i:3 * i + 3]) for i in range(n)]


FW_ROWS = 24
CW_ROWS = 32
R_FW = 0
R_FB = R_FW + N_DEV * FW_ROWS
R_CW = R_FB + 48
R_BQKV = R_CW + (CONV_WIDTH // LANES) * CW_ROWS
R_BCIN = R_BQKV + 8
R_GMIX = R_BCIN + 8
R_BOUT = R_GMIX + 8
R_GFFN = R_BOUT + 8
R_CB = R_GFFN + 8
R_CGAIN = R_CB + 8
R_CBIAS = R_CGAIN + 8
R_QKS = R_CBIAS + 8
SMALL_ROWS = R_QKS + 8


def _pack_small(raw):
    def rows(a, n):
        a = a.reshape(-1, LANES)
        return jnp.pad(a, ((0, n - a.shape[0]), (0, 0)))

    fw = jnp.pad(raw["dfw"].reshape(N_DEV, -1, LANES), ((0, 0), (0, FW_ROWS - 3 * FF_LANE_CHUNKS), (0, 0)))
    cw = jnp.pad(raw["dcw"].reshape(CONV_KERNEL, -1, LANES).transpose(1, 0, 2), ((0, 0), (0, CW_ROWS - CONV_KERNEL), (0, 0)))
    qks = jnp.concatenate([raw["dgq"], raw["dgk"], raw["dsink"], jnp.pad(raw["loss"], ((0, 0), (0, LANES - 1)))], axis=0)
    return jnp.concatenate([
        fw.reshape(-1, LANES), rows(raw["dfb"][:, 0, :FF_CHUNK], 48), cw.reshape(-1, LANES), rows(raw["dbqkv"], 8),
        rows(raw["dbcin"], 8), rows(raw["dg_mix"], 8), rows(raw["db_out"], 8), rows(raw["dg_ffn"], 8), rows(raw["dcb"], 8),
        rows(raw["dcgain"], 8), rows(raw["dcbias"], 8), rows(qks, 8)], axis=0)


def _adamw_small(gpack, w, m, v):
    n = len(SMALL)
    ix = {name: i for i, name in enumerate(SMALL)}

    def body(g_ref, *refs):
        w_refs, m_refs, v_refs, outs = refs[:n], refs[n:2 * n], refs[2 * n:3 * n], refs[3 * n:]
        d = _dev_index(*_position())

        def step(name, idx, gv):
            i = ix[name]
            delta, mn, vn = _adam_math(w_refs[i][idx], gv, m_refs[i][idx], v_refs[i][idx])
            for ref, val in zip(outs[4 * i:4 * i + 4], (gv, delta, mn, vn)):
                ref[idx] = val

        def whole(name, row, nrows):
            step(name, (slice(None), slice(None)), g_ref[row:row + nrows, :])

        whole("mix_norm_gain", R_GMIX, 8)
        whole("b_out", R_BOUT, 8)
        whole("ffn_norm_gain", R_GFFN, 8)
        whole("conv_dw_b", R_CB, 4)
        whole("conv_norm_gain", R_CGAIN, 4)
        whole("conv_norm_bias", R_CBIAS, 4)
        whole("ffn_dw_b", R_FB, 2 * D_FF // LANES)
        nq = QKV_COLS // LANES
        step("b_in", (slice(0, nq), slice(None)), g_ref[R_BQKV:R_BQKV + nq, :])
        step("b_in", (slice(nq, nq + CIN_COLS // LANES), slice(None)), g_ref[R_BCIN:R_BCIN + CIN_COLS // LANES, :])
        step("q_norm_gain", (slice(None), slice(None)), g_ref[R_QKS:R_QKS + 1, 0:HEAD_DIM])
        step("k_norm_gain", (slice(None), slice(None)), g_ref[R_QKS + 1:R_QKS + 2, 0:HEAD_DIM])
        step("attn_sinks", (slice(None), slice(None)), g_ref[R_QKS + 2:R_QKS + 3, 0:N_Q_HEADS])
        blk = g_ref[pl.ds(pl.multiple_of(R_CW + CW_ROWS * lax.shift_right_logical(d, 1), SUBLANES), CW_ROWS), :]
        blk = jnp.where((d & 1) == 1, pltpu.roll(blk, HEAD_DIM, 1), blk)
        step("conv_dw_w", (slice(None), slice(None)), blk[0:CONV_KERNEL, 0:CONV_WIDTH // N_DEV])
        blk = g_ref[pl.ds(pl.multiple_of(R_FW + FW_ROWS * d, SUBLANES), FW_ROWS), :]
        for k in range(3):
            for j in range(FF_LANE_CHUNKS):
                wd = min(LANES, FF_CHUNK - j * LANES)
                row = k * FF_LANE_CHUNKS + j
                step("ffn_dw_w", (slice(k, k + 1), slice(j * LANES, j * LANES + wd)), blk[row:row + 1, 0:wd])

    vmem = pl.BlockSpec(memory_space=pltpu.VMEM)
    args = [gpack] + [d[name] for d in (w, m, v) for name in SMALL]
    outs = pl.pallas_call(
        body, in_specs=[vmem] * len(args), out_specs=[vmem] * (4 * n),
        out_shape=[jax.ShapeDtypeStruct(w[name].shape, F32) for name in SMALL for _ in range(4)],
        compiler_params=pltpu.CompilerParams(vmem_limit_bytes=VMEM_LIMIT), name="adamw_small")(*args)
    return {name: outs[4 * i:4 * i + 4] for i, name in enumerate(SMALL)}


def _token_mixing(x, p, attn_shards, conv_shards):
    qkv, cin, h1 = _mix_proj(x, p["g_mix"], p["w_in_t"], p["b_qkv"], p["b_cin"])
    attn, *from_attn = _attn_fwd(qkv, p["gq2"], p["gk2"], p["sinks"], attn_shards)
    c3, c1, *from_conv = _conv_fwd(cin, p["cw8"], p["cb"], p["cgain"], p["cbias"], conv_shards)
    return (qkv, cin, h1, attn, c3, c1), from_attn, from_conv


def _rest_of_step(x, target, p, saved, scatter):
    s = x.shape[0]
    qkv, cin, h1, attn, c3, c1 = saved
    cw8, w_out, w_up, w_down = p["cw8"], p["w_out"], p["w_up"], p["w_down"]
    x2, h2 = _out_proj(x, attn, c3, w_out, w_out, p["b_out"], p["g_ffn"])
    fw, fb = p["fw"], p["fb"]
    up0, gu, act, dy, dyb, loss = _ffn_fwd(h2, x2, target, w_up, fw, fb, w_down)
    dup0, dh2, dfw, dfb = _ffn_bwd(dyb, up0, gu, w_up, fw, w_down)
    dx2, dmixed, dg_ffn, db_out, dw_out = _ffn_norm_bwd(dh2, dy, x2, p["g_ffn"], w_out, attn, c3)
    dw_out = dw_out.reshape(N_DEV, -1, D_MODEL)
    dw_up = _tn_matmul(dup0.reshape(N_DEV, s, FF_CHUNK), h2[None], "dw_up", 2 * TN_TOKENS)
    dw_down = _tn_matmul(act, dyb[None], "dw_down", 2 * TN_TOKENS).reshape(N_DEV, -1, D_MODEL)
    dcin, dcw, dcb, dcgain, dcbias, dbcin, *g_up = _conv_bwd(dmixed, c1, cin, cw8, p["cgain"], p["cbias"], [dw_up] if scatter else [])
    dqkv, dgq, dgk, dsink, dbqkv, *g_down_out = _attn_bwd(qkv, dmixed, p["gq2"], p["gk2"], p["sinks"],
                                                          [dw_down, dw_out] if scatter else [])
    grad_x, dg_mix, dw_in = _in_bwd(dqkv, dcin, p["w_in_t"], x, dx2, p["g_mix"], h1)
    dw_in = dw_in.reshape(N_DEV, -1, D_MODEL)
    if scatter:
        big = {"w_up": g_up[0], "w_down": g_down_out[0], "w_in": dw_in, "w_out": g_down_out[1]}
    else:
        big = {"w_up": dw_up, "w_down": dw_down, "w_in": dw_in, "w_out": dw_out}
    small = dict(dg_mix=dg_mix, dbqkv=dbqkv, dbcin=dbcin, dgq=dgq, dgk=dgk, dsink=dsink, dcw=dcw, dcb=dcb, dcgain=dcgain,
                 dcbias=dcbias, db_out=db_out, dg_ffn=dg_ffn, dfw=dfw, dfb=dfb, loss=loss)
    return loss, grad_x, big, small


BIG = ("w_in", "w_out", "w_up", "w_down")
SMALL = ("mix_norm_gain", "b_in", "q_norm_gain", "k_norm_gain", "attn_sinks", "conv_dw_w", "conv_dw_b",
         "conv_norm_gain", "conv_norm_bias", "b_out", "ffn_norm_gain", "ffn_dw_w", "ffn_dw_b")
ORDER = ("mix_norm_gain", "w_in", "b_in", "q_norm_gain", "k_norm_gain", "attn_sinks", "conv_dw_w", "conv_dw_b",
         "conv_norm_gain", "conv_norm_bias", "w_out", "b_out", "ffn_norm_gain", "w_up", "ffn_dw_w", "ffn_dw_b", "w_down")


def kernel(x, mix_norm_gain, w_in, b_in, q_norm_gain, k_norm_gain, attn_sinks, conv_dw_w, conv_dw_b, conv_norm_gain, conv_norm_bias, w_out, b_out, ffn_norm_gain, w_up, ffn_dw_w, ffn_dw_b, w_down, loss_target, m_mix_norm_gain, m_w_in, m_b_in, m_q_norm_gain, m_k_norm_gain, m_attn_sinks, m_conv_dw_w, m_conv_dw_b, m_conv_norm_gain, m_conv_norm_bias, m_w_out, m_b_out, m_ffn_norm_gain, m_w_up, m_ffn_dw_w, m_ffn_dw_b, m_w_down, v_mix_norm_gain, v_w_in, v_b_in, v_q_norm_gain, v_k_norm_gain, v_attn_sinks, v_conv_dw_w, v_conv_dw_b, v_conv_norm_gain, v_conv_norm_bias, v_w_out, v_b_out, v_ffn_norm_gain, v_w_up, v_ffn_dw_w, v_ffn_dw_b, v_w_down):
    w = dict(mix_norm_gain=mix_norm_gain, w_in=w_in, b_in=b_in, q_norm_gain=q_norm_gain, k_norm_gain=k_norm_gain,
             attn_sinks=attn_sinks, conv_dw_w=conv_dw_w, conv_dw_b=conv_dw_b, conv_norm_gain=conv_norm_gain,
             conv_norm_bias=conv_norm_bias, w_out=w_out, b_out=b_out, ffn_norm_gain=ffn_norm_gain, w_up=w_up,
             ffn_dw_w=ffn_dw_w, ffn_dw_b=ffn_dw_b, w_down=w_down)
    m = dict(mix_norm_gain=m_mix_norm_gain, w_in=m_w_in, b_in=m_b_in, q_norm_gain=m_q_norm_gain, k_norm_gain=m_k_norm_gain,
             attn_sinks=m_attn_sinks, conv_dw_w=m_conv_dw_w, conv_dw_b=m_conv_dw_b, conv_norm_gain=m_conv_norm_gain,
             conv_norm_bias=m_conv_norm_bias, w_out=m_w_out, b_out=m_b_out, ffn_norm_gain=m_ffn_norm_gain, w_up=m_w_up,
             ffn_dw_w=m_ffn_dw_w, ffn_dw_b=m_ffn_dw_b, w_down=m_w_down)
    v = dict(mix_norm_gain=v_mix_norm_gain, w_in=v_w_in, b_in=v_b_in, q_norm_gain=v_q_norm_gain, k_norm_gain=v_k_norm_gain,
             attn_sinks=v_attn_sinks, conv_dw_w=v_conv_dw_w, conv_dw_b=v_conv_dw_b, conv_norm_gain=v_conv_norm_gain,
             conv_norm_bias=v_conv_norm_bias, w_out=v_w_out, b_out=v_b_out, ffn_norm_gain=v_ffn_norm_gain, w_up=v_w_up,
             ffn_dw_w=v_ffn_dw_w, ffn_dw_b=v_ffn_dw_b, w_down=v_w_down)
    s = x.shape[1]

    wi8, cw8, fw8 = _allgather([w_in.T, conv_dw_w, ffn_dw_w], [BF16, F32, F32])
    lane_pad = ((0, 0), (0, 0), (0, FF_PADDED - FF_CHUNK))
    p = {
        "g_mix": mix_norm_gain.reshape(1, -1), "w_in_t": wi8.reshape(QKV_COLS + CIN_COLS, D_MODEL),
        "b_qkv": b_in[:QKV_COLS].reshape(1, -1), "b_cin": b_in[QKV_COLS:].reshape(1, -1),
        "gq2": jnp.tile(q_norm_gain, 2).reshape(1, -1), "gk2": jnp.tile(k_norm_gain, 2).reshape(1, -1), "sinks": attn_sinks,
        "cw8": jnp.repeat(cw8.transpose(1, 0, 2).reshape(CONV_KERNEL, CONV_WIDTH), SUBLANES, axis=0),
        "cb": conv_dw_b.reshape(1, -1), "cgain": conv_norm_gain.reshape(1, -1), "cbias": conv_norm_bias.reshape(1, -1),
        "b_out": b_out.reshape(1, -1), "g_ffn": ffn_norm_gain.reshape(1, -1),
        "fw": jnp.pad(fw8, lane_pad), "fb": jnp.pad(ffn_dw_b.reshape(N_DEV, 1, FF_CHUNK), lane_pad),
    }

    saved, (wu8,), (wo8, wd8) = _token_mixing(x[0], p, [w_up.T], [w_out, w_down])
    p.update(w_out=wo8.reshape(D_MODEL, D_MODEL), w_up=wu8, w_down=wd8.reshape(N_FF_PAIRS, FF_CHUNK, D_MODEL))
    loss, grad_x, big, small = _rest_of_step(x[0], loss_target[0], p, saved, True)

    g = dict(big)
    g["w_in"], gpack = _final_exchange(big["w_in"], _pack_small(small))

    delta, new_m, new_v = {}, {}, {}
    transposed = ("w_in", "w_up")
    big_out = _adamw(*[[d[n].T if n in transposed and d is not g else d[n] for n in BIG] for d in (w, g, m, v)])
    for n, outs in zip(BIG, big_out):
        if n in transposed:
            g[n], delta[n], new_m[n], new_v[n] = g[n].T, *[o.T for o in outs]
        else:
            delta[n], new_m[n], new_v[n] = outs

    def view(a):
        return a if a.ndim == 2 else (a.reshape(-1, LANES) if a.size % LANES == 0 else a.reshape(1, -1))

    small_out = _adamw_small(gpack, *[{n: view(d[n]) for n in SMALL} for d in (w, m, v)])
    for n in SMALL:
        g[n], delta[n], new_m[n], new_v[n] = [a.reshape(w[n].shape) for a in small_out[n]]

    total = gpack[R_QKS + 3, 0]
    return (total, grad_x.reshape(1, s, D_MODEL), *[g[n] for n in ORDER], *[delta[n] for n in ORDER],
            *[new_m[n] for n in ORDER], *[new_v[n] for n in ORDER])
```
